```python
import jax, jax.numpy as jnp
from jax import lax
import numpy as np

D_MODEL = 1024
BATCH = 8
SEQ = 4096
DEPTH = 4

HEAD_DIM = 64
GROUP_HEADS = 4
GROUP_WIDTH = GROUP_HEADS * HEAD_DIM
N_GROUPS = 4
MIX_WIDTH = N_GROUPS * GROUP_WIDTH
BLOCK = 128
NEG_INF = -1e30

FOX_H = GROUP_HEADS
FOX_BIAS_INIT = 3.0
MLA_H = GROUP_HEADS
MLA_Q_RANK = D_MODEL // 4
MLA_KV_RANK = D_MODEL // 8
MLA_NOPE = HEAD_DIM
MLA_ROPE = HEAD_DIM // 2
MLA_V = HEAD_DIM
ROPE_THETA = 10000.0
SB_H = GROUP_HEADS
SWA_H = GROUP_HEADS
SWA_KV_H = 2
WINDOW = 128
D_FF = ((8 * D_MODEL + 767) // 768) * 256
ALPHA = (2.0 * DEPTH) ** 0.25
BETA = (8.0 * DEPTH) ** -0.25

SPLIT_SIZES = (
    FOX_H * HEAD_DIM, FOX_H * HEAD_DIM, FOX_H * HEAD_DIM, FOX_H,
    MLA_Q_RANK, MLA_KV_RANK, MLA_ROPE,
    SB_H * HEAD_DIM, SB_H * HEAD_DIM, SB_H * HEAD_DIM,
    SWA_H * HEAD_DIM, SWA_KV_H * HEAD_DIM, SWA_KV_H * HEAD_DIM,
)
IN_WIDTH = sum(SPLIT_SIZES)

kernel_name = 'hybrid_fox_mla_stickbreak_swa_deepnorm'


def _layernorm(x, g, b, eps=1e-5):
    xf = x.astype(jnp.float32)
    mu = jnp.mean(xf, axis=-1, keepdims=True)
    var = jnp.mean(jnp.square(xf - mu), axis=-1, keepdims=True)
    return ((xf - mu) * lax.rsqrt(var + eps) * g + b).astype(x.dtype)


def _rmsnorm(x, g, eps=1e-6):
    xf = x.astype(jnp.float32)
    return (xf * lax.rsqrt(jnp.mean(jnp.square(xf), axis=-1, keepdims=True) + eps) * g).astype(x.dtype)


def _group_rmsnorm(mix, g, eps=1e-6):
    B, S, _ = mix.shape
    xf = mix.astype(jnp.float32).reshape(B, S, N_GROUPS, GROUP_WIDTH)
    xf = xf * lax.rsqrt(jnp.mean(jnp.square(xf), axis=-1, keepdims=True) + eps)
    return (xf.reshape(B, S, MIX_WIDTH) * g).astype(mix.dtype)


def _heads(t, n):
    B, S, _ = t.shape
    return t.reshape(B, S, n, -1)


def _rope_tables(S):
    pos = jnp.arange(S, dtype=jnp.float32)
    inv = ROPE_THETA ** (-jnp.arange(0, MLA_ROPE, 2, dtype=jnp.float32) / MLA_ROPE)
    ang = pos[:, None] * inv[None, :]
    return jnp.cos(ang), jnp.sin(ang)


def _rope(x, cos, sin):
    x1, x2 = jnp.split(x.astype(jnp.float32), 2, axis=-1)
    c = cos[None, :, None, :]
    s = sin[None, :, None, :]
    return jnp.concatenate([x1 * c - x2 * s, x1 * s + x2 * c], axis=-1).astype(x.dtype)


def _alibi_slopes(n):
    return jnp.exp2(-8.0 * jnp.arange(1, n + 1, dtype=jnp.float32) / n)


def _causal_softmax_blocks(q, k, v, scale, cum_log_f=None):
    B, S, H, _ = q.shape
    nb = S // BLOCK
    qb = q.reshape(B, nb, BLOCK, H, -1).swapaxes(0, 1)
    kpos = jnp.arange(S)

    def one_block(args):
        i, qi = args
        s = jnp.einsum('bqhd,bkhd->bhqk', qi, k).astype(jnp.float32) * scale
        qpos = i * BLOCK + jnp.arange(BLOCK)
        if cum_log_f is not None:
            cq = lax.dynamic_slice_in_dim(cum_log_f, i * BLOCK, BLOCK, axis=2)
            s = s + cq[..., :, None] - cum_log_f[..., None, :]
        s = jnp.where(kpos[None, :] <= qpos[:, None], s, NEG_INF)
        p = jax.nn.softmax(s, axis=-1).astype(v.dtype)
        return jnp.einsum('bhqk,bkhd->bqhd', p, v)

    out = lax.map(one_block, (jnp.arange(nb), qb))
    return out.swapaxes(0, 1).reshape(B, S, H, -1)


def _stick_breaking_blocks(q, k, v):
    B, S, H, D = q.shape
    nb = S // BLOCK
    scale = D ** -0.5
    qb = q.reshape(B, nb, BLOCK, H, D).swapaxes(0, 1)
    kpos = jnp.arange(S)

    def one_block(args):
        i, qi = args
        z = jnp.einsum('bqhd,bkhd->bhqk', qi, k).astype(jnp.float32) * scale
        qpos = i * BLOCK + jnp.arange(BLOCK)
        strict = kpos[None, :] < qpos[:, None]
        log_1mb = jnp.where(strict, jax.nn.log_sigmoid(-z), 0.0)
        between = lax.cumsum(log_1mb, axis=3, reverse=True) - log_1mb
        a = jnp.where(strict, jnp.exp(jax.nn.log_sigmoid(z) + between), 0.0)
        return jnp.einsum('bhqk,bkhd->bqhd', a.astype(v.dtype), v)

    out = lax.map(one_block, (jnp.arange(nb), qb))
    return out.swapaxes(0, 1).reshape(B, S, H, D)


def _mla(cq, ckv, kr, g_q, g_kv, w_uq, w_ukv, cos, sin):
    B, S, _ = cq.shape
    q = (_rmsnorm(cq, g_q) @ w_uq).reshape(B, S, MLA_H, MLA_NOPE + MLA_ROPE)
    kv = (_rmsnorm(ckv, g_kv) @ w_ukv).reshape(B, S, MLA_H, MLA_NOPE + MLA_V)
    k_nope, v = kv[..., :MLA_NOPE], kv[..., MLA_NOPE:]
    q = jnp.concatenate([q[..., :MLA_NOPE], _rope(q[..., MLA_NOPE:], cos, sin)], axis=-1)
    k_rope = jnp.broadcast_to(_rope(kr[:, :, None, :], cos, sin), (B, S, MLA_H, MLA_ROPE))
    k = jnp.concatenate([k_nope, k_rope.astype(k_nope.dtype)], axis=-1)
    return _causal_softmax_blocks(q.astype(k.dtype), k, v, (MLA_NOPE + MLA_ROPE) ** -0.5)


def _swa_sink_alibi(q, k, v, sinks, slopes):
    B, S, Hq, D = q.shape
    Hkv = k.shape[2]
    G = Hq // Hkv
    nb = S // BLOCK
    qb = q.reshape(B, nb, BLOCK, Hkv, G, D)
    pad = jnp.zeros((B, BLOCK, Hkv, D), k.dtype)
    kp = jnp.concatenate([pad, k], axis=1).reshape(B, nb + 1, BLOCK, Hkv, D)
    vp = jnp.concatenate([pad.astype(v.dtype), v], axis=1).reshape(B, nb + 1, BLOCK, Hkv, D)
    kb = jnp.concatenate([kp[:, :-1], kp[:, 1:]], axis=2)
    vb = jnp.concatenate([vp[:, :-1], vp[:, 1:]], axis=2)
    s = jnp.einsum('bnqhgd,bnkhd->bnhgqk', qb, kb).astype(jnp.float32) * (D ** -0.5)
    dist = jnp.arange(BLOCK)[:, None] + BLOCK - jnp.arange(2 * BLOCK)[None, :]
    band = (dist >= 0) & (dist < WINDOW)
    kpos = (jnp.arange(nb)[:, None] - 1) * BLOCK + jnp.arange(2 * BLOCK)[None, :]
    valid = band[None, :, :] & (kpos >= 0)[:, None, :]
    s = s - slopes.reshape(Hkv, G)[:, :, None, None] * dist.astype(jnp.float32)
    s = jnp.where(valid[None, :, None, None], s, NEG_INF)
    sink = jnp.broadcast_to(sinks.astype(jnp.float32).reshape(Hkv, G)[None, None, :, :, None, None],
                            s.shape[:-1] + (1,))
    p = jax.nn.softmax(jnp.concatenate([s, sink], axis=-1), axis=-1)[..., :-1]
    out = jnp.einsum('bnhgqk,bnkhd->bnqhgd', p.astype(v.dtype), vb)
    return out.reshape(B, S, Hq, D)


def _fwd_setup_inputs(seed: int = 0) -> dict:
    key = jax.random.key(seed)
    ks = jax.random.split(key, 17)
    L = DEPTH

    def nrm(k, shape, scale):
        return jax.random.normal(k, shape, jnp.float32) * scale

    def gain(k, shape):
        return 1.0 + 0.02 * jax.random.normal(k, shape, jnp.float32)

    return {
        'x': nrm(ks[0], (BATCH, SEQ, D_MODEL), 1.0),
        'w_in': nrm(ks[1], (L, D_MODEL, IN_WIDTH), D_MODEL ** -0.5),
        'fox_b_f': FOX_BIAS_INIT + nrm(ks[2], (L, FOX_H), 0.1),
        'mla_g_q': gain(ks[3], (L, MLA_Q_RANK)),
        'mla_g_kv': gain(ks[4], (L, MLA_KV_RANK)),
        'mla_w_uq': nrm(ks[5], (L, MLA_Q_RANK, MLA_H * (MLA_NOPE + MLA_ROPE)), MLA_Q_RANK ** -0.5),
        'mla_w_ukv': nrm(ks[6], (L, MLA_KV_RANK, MLA_H * (MLA_NOPE + MLA_V)), MLA_KV_RANK ** -0.5),
        'swa_sinks': nrm(ks[7], (L, SWA_H), 0.5),
        'mix_g': gain(ks[8], (L, MIX_WIDTH)),
        'w_o': nrm(ks[9], (L, MIX_WIDTH, D_MODEL), BETA * MIX_WIDTH ** -0.5),
        'ln1_g': gain(ks[10], (L, D_MODEL)),
        'ln1_b': nrm(ks[11], (L, D_MODEL), 0.02),
        'w_gate': nrm(ks[12], (L, D_MODEL, D_FF), D_MODEL ** -0.5),
        'w_up': nrm(ks[13], (L, D_MODEL, D_FF), D_MODEL ** -0.5),
        'w_down': nrm(ks[14], (L, D_FF, D_MODEL), BETA * D_FF ** -0.5),
        'ln2_g': gain(ks[15], (L, D_MODEL)),
        'ln2_b': nrm(ks[16], (L, D_MODEL), 0.02),
    }


def _fwd_reference(x, w_in, fox_b_f, mla_g_q, mla_g_kv, mla_w_uq, mla_w_ukv, swa_sinks, mix_g, w_o,
              ln1_g, ln1_b, w_gate, w_up, w_down, ln2_g, ln2_b):
    B, S, _ = x.shape
    cos, sin = _rope_tables(S)
    slopes = _alibi_slopes(SWA_H)
    points = [int(p) for p in np.cumsum(SPLIT_SIZES)[:-1]]
    for l in range(DEPTH):
        h = x @ w_in[l]
        (fq, fk, fv, fgate, cq, ckv, kr, sq, sk, sv, wq, wk, wv) = jnp.split(h, points, axis=-1)
        log_f = jax.nn.log_sigmoid(fgate.astype(jnp.float32) + fox_b_f[l].astype(jnp.float32))
        cum = jnp.cumsum(log_f, axis=1).transpose(0, 2, 1)
        out_a = _causal_softmax_blocks(_heads(fq, FOX_H), _heads(fk, FOX_H), _heads(fv, FOX_H),
                                       HEAD_DIM ** -0.5, cum)
        out_b = _mla(cq, ckv, kr, mla_g_q[l], mla_g_kv[l], mla_w_uq[l], mla_w_ukv[l], cos, sin)
        out_c = _stick_breaking_blocks(_heads(sq, SB_H), _heads(sk, SB_H), _heads(sv, SB_H))
        out_d = _swa_sink_alibi(_heads(wq, SWA_H), _heads(wk, SWA_KV_H), _heads(wv, SWA_KV_H),
                                swa_sinks[l], slopes)
        mix = jnp.concatenate([out_a.reshape(B, S, GROUP_WIDTH), out_b.reshape(B, S, GROUP_WIDTH),
                               out_c.reshape(B, S, GROUP_WIDTH), out_d.reshape(B, S, GROUP_WIDTH)],
                              axis=-1).astype(x.dtype)
        y = _group_rmsnorm(mix, mix_g[l]) @ w_o[l]
        x = _layernorm(ALPHA * x + y, ln1_g[l], ln1_b[l])
        f = (jax.nn.silu(x @ w_gate[l]) * (x @ w_up[l])) @ w_down[l]
        x = _layernorm(ALPHA * x + f, ln2_g[l], ln2_b[l])
    return x


import jax as _jax
import jax.numpy as _jnp

TWIN_FORMAT = 'train_step'
FWD_PARAMS = ['x', 'w_in', 'fox_b_f', 'mla_g_q', 'mla_g_kv', 'mla_w_uq', 'mla_w_ukv', 'swa_sinks', 'mix_g', 'w_o', 'ln1_g', 'ln1_b', 'w_gate', 'w_up', 'w_down', 'ln2_g', 'ln2_b']
TWIN_WEIGHTS = ['w_in', 'fox_b_f', 'mla_g_q', 'mla_g_kv', 'mla_w_uq', 'mla_w_ukv', 'swa_sinks', 'mix_g', 'w_o', 'ln1_g', 'ln1_b', 'w_gate', 'w_up', 'w_down', 'ln2_g', 'ln2_b']
TWIN_DIFF_INPUT = 'x'
TWIN_INPUTS = ['x', 'w_in', 'fox_b_f', 'mla_g_q', 'mla_g_kv', 'mla_w_uq', 'mla_w_ukv', 'swa_sinks', 'mix_g', 'w_o', 'ln1_g', 'ln1_b', 'w_gate', 'w_up', 'w_down', 'ln2_g', 'ln2_b', 'loss_target', 'm_w_in', 'm_fox_b_f', 'm_mla_g_q', 'm_mla_g_kv', 'm_mla_w_uq', 'm_mla_w_ukv', 'm_swa_sinks', 'm_mix_g', 'm_w_o', 'm_ln1_g', 'm_ln1_b', 'm_w_gate', 'm_w_up', 'm_w_down', 'm_ln2_g', 'm_ln2_b', 'v_w_in', 'v_fox_b_f', 'v_mla_g_q', 'v_mla_g_kv', 'v_mla_w_uq', 'v_mla_w_ukv', 'v_swa_sinks', 'v_mix_g', 'v_w_o', 'v_ln1_g', 'v_ln1_b', 'v_w_gate', 'v_w_up', 'v_w_down', 'v_ln2_g', 'v_ln2_b']
TWIN_OUTPUTS = ['loss', 'grad_x', 'grad_w_in', 'grad_fox_b_f', 'grad_mla_g_q', 'grad_mla_g_kv', 'grad_mla_w_uq', 'grad_mla_w_ukv', 'grad_swa_sinks', 'grad_mix_g', 'grad_w_o', 'grad_ln1_g', 'grad_ln1_b', 'grad_w_gate', 'grad_w_up', 'grad_w_down', 'grad_ln2_g', 'grad_ln2_b', 'delta_w_in', 'delta_fox_b_f', 'delta_mla_g_q', 'delta_mla_g_kv', 'delta_mla_w_uq', 'delta_mla_w_ukv', 'delta_swa_sinks', 'delta_mix_g', 'delta_w_o', 'delta_ln1_g', 'delta_ln1_b', 'delta_w_gate', 'delta_w_up', 'delta_w_down', 'delta_ln2_g', 'delta_ln2_b', 'new_m_w_in', 'new_m_fox_b_f', 'new_m_mla_g_q', 'new_m_mla_g_kv', 'new_m_mla_w_uq', 'new_m_mla_w_ukv', 'new_m_swa_sinks', 'new_m_mix_g', 'new_m_w_o', 'new_m_ln1_g', 'new_m_ln1_b', 'new_m_w_gate', 'new_m_w_up', 'new_m_w_down', 'new_m_ln2_g', 'new_m_ln2_b', 'new_v_w_in', 'new_v_fox_b_f', 'new_v_mla_g_q', 'new_v_mla_g_kv', 'new_v_mla_w_uq', 'new_v_mla_w_ukv', 'new_v_swa_sinks', 'new_v_mix_g', 'new_v_w_o', 'new_v_ln1_g', 'new_v_ln1_b', 'new_v_w_gate', 'new_v_w_up', 'new_v_w_down', 'new_v_ln2_g', 'new_v_ln2_b']
TWIN_LEAF_KINDS = {'loss': 'loss', 'grad_x': 'grad_x', 'grad_w_in': 'grad_w', 'grad_fox_b_f': 'grad_w', 'grad_mla_g_q': 'grad_w', 'grad_mla_g_kv': 'grad_w', 'grad_mla_w_uq': 'grad_w', 'grad_mla_w_ukv': 'grad_w', 'grad_swa_sinks': 'grad_w', 'grad_mix_g': 'grad_w', 'grad_w_o': 'grad_w', 'grad_ln1_g': 'grad_w', 'grad_ln1_b': 'grad_w', 'grad_w_gate': 'grad_w', 'grad_w_up': 'grad_w', 'grad_w_down': 'grad_w', 'grad_ln2_g': 'grad_w', 'grad_ln2_b': 'grad_w', 'delta_w_in': 'delta_w', 'delta_fox_b_f': 'delta_w', 'delta_mla_g_q': 'delta_w', 'delta_mla_g_kv': 'delta_w', 'delta_mla_w_uq': 'delta_w', 'delta_mla_w_ukv': 'delta_w', 'delta_swa_sinks': 'delta_w', 'delta_mix_g': 'delta_w', 'delta_w_o': 'delta_w', 'delta_ln1_g': 'delta_w', 'delta_ln1_b': 'delta_w', 'delta_w_gate': 'delta_w', 'delta_w_up': 'delta_w', 'delta_w_down': 'delta_w', 'delta_ln2_g': 'delta_w', 'delta_ln2_b': 'delta_w', 'new_m_w_in': 'new_m', 'new_m_fox_b_f': 'new_m', 'new_m_mla_g_q': 'new_m', 'new_m_mla_g_kv': 'new_m', 'new_m_mla_w_uq': 'new_m', 'new_m_mla_w_ukv': 'new_m', 'new_m_swa_sinks': 'new_m', 'new_m_mix_g': 'new_m', 'new_m_w_o': 'new_m', 'new_m_ln1_g': 'new_m', 'new_m_ln1_b': 'new_m', 'new_m_w_gate': 'new_m', 'new_m_w_up': 'new_m', 'new_m_w_down': 'new_m', 'new_m_ln2_g': 'new_m', 'new_m_ln2_b': 'new_m', 'new_v_w_in': 'new_v', 'new_v_fox_b_f': 'new_v', 'new_v_mla_g_q': 'new_v', 'new_v_mla_g_kv': 'new_v', 'new_v_mla_w_uq': 'new_v', 'new_v_mla_w_ukv': 'new_v', 'new_v_swa_sinks': 'new_v', 'new_v_mix_g': 'new_v', 'new_v_w_o': 'new_v', 'new_v_ln1_g': 'new_v', 'new_v_ln1_b': 'new_v', 'new_v_w_gate': 'new_v', 'new_v_w_up': 'new_v', 'new_v_w_down': 'new_v', 'new_v_ln2_g': 'new_v', 'new_v_ln2_b': 'new_v'}


def _forward(args):
    return _fwd_reference(*[args[k] for k in FWD_PARAMS])


def _output_shape():
    def fwd():
        inp = _fwd_setup_inputs(0)
        return _fwd_reference(*[inp[k] for k in FWD_PARAMS])
    out = _jax.eval_shape(fwd)
    return out.shape, out.dtype

N_MICROBATCH = 1
ADAM_LR = 0.001
ADAM_B1 = 0.9
ADAM_B2 = 0.999
ADAM_EPS = 1e-08
ADAM_WD = 0.01
ADAM_STEP = 10
PER_EXAMPLE_BATCH_AXIS = {'x': 0, 'loss_target': 0}
SHARED_INPUTS = []
_WEIGHT_DTYPES = {'w_in': _jnp.float32, 'fox_b_f': _jnp.float32, 'mla_g_q': _jnp.float32, 'mla_g_kv': _jnp.float32, 'mla_w_uq': _jnp.float32, 'mla_w_ukv': _jnp.float32, 'swa_sinks': _jnp.float32, 'mix_g': _jnp.float32, 'w_o': _jnp.float32, 'ln1_g': _jnp.float32, 'ln1_b': _jnp.float32, 'w_gate': _jnp.float32, 'w_up': _jnp.float32, 'w_down': _jnp.float32, 'ln2_g': _jnp.float32, 'ln2_b': _jnp.float32}
MOMENT_SCALE = {'w_in': 4.524383e-02, 'fox_b_f': 2.783868e-01, 'mla_g_q': 3.581861e-02, 'mla_g_kv': 1.486646e-01, 'mla_w_uq': 2.990985e-02, 'mla_w_ukv': 5.972884e-02, 'swa_sinks': 3.880031e-02, 'mix_g': 5.892805e-02, 'w_o': 1.376188e-01, 'ln1_g': 1.001722e+00, 'ln1_b': 1.095932e+00, 'w_gate': 1.697387e-02, 'w_up': 1.647761e-02, 'w_down': 6.506803e-02, 'ln2_g': 1.608999e+01, 'ln2_b': 2.221851e+00}


def _to_microbatches(a, axis):
    t = _jnp.moveaxis(a, axis, 0)
    t = t.reshape((N_MICROBATCH, t.shape[0] // N_MICROBATCH) + t.shape[1:])
    return _jnp.moveaxis(t, 1, axis + 1)


def setup_inputs(seed: int = 0) -> dict:
    inp = _fwd_setup_inputs(seed)
    key = _jax.random.fold_in(_jax.random.key(seed), 7919)
    shape, _ = _output_shape()
    out = dict(inp)
    out["loss_target"] = _jax.random.normal(_jax.random.fold_in(key, 0), shape, _jnp.float32)
    for i, name in enumerate(TWIN_WEIGHTS):
        w = inp[name].astype(_jnp.float32)
        if MOMENT_SCALE is None:
            s = _jnp.sqrt(_jnp.mean(_jnp.square(w)) + 1e-30)
        else:
            s = MOMENT_SCALE[name]
        km, kv = _jax.random.split(_jax.random.fold_in(key, i + 1))
        out[name] = w
        out["m_" + name] = s * _jax.random.normal(km, w.shape, _jnp.float32)
        out["v_" + name] = (s * s) * _jax.random.uniform(kv, w.shape, _jnp.float32, 0.5, 1.5)
    if N_MICROBATCH > 1:
        for name, axis in PER_EXAMPLE_BATCH_AXIS.items():
            out[name] = _to_microbatches(out[name], axis)
    return {'x': out['x'], 'w_in': out['w_in'], 'fox_b_f': out['fox_b_f'], 'mla_g_q': out['mla_g_q'], 'mla_g_kv': out['mla_g_kv'], 'mla_w_uq': out['mla_w_uq'], 'mla_w_ukv': out['mla_w_ukv'], 'swa_sinks': out['swa_sinks'], 'mix_g': out['mix_g'], 'w_o': out['w_o'], 'ln1_g': out['ln1_g'], 'ln1_b': out['ln1_b'], 'w_gate': out['w_gate'], 'w_up': out['w_up'], 'w_down': out['w_down'], 'ln2_g': out['ln2_g'], 'ln2_b': out['ln2_b'], 'loss_target': out['loss_target'], 'm_w_in': out['m_w_in'], 'm_fox_b_f': out['m_fox_b_f'], 'm_mla_g_q': out['m_mla_g_q'], 'm_mla_g_kv': out['m_mla_g_kv'], 'm_mla_w_uq': out['m_mla_w_uq'], 'm_mla_w_ukv': out['m_mla_w_ukv'], 'm_swa_sinks': out['m_swa_sinks'], 'm_mix_g': out['m_mix_g'], 'm_w_o': out['m_w_o'], 'm_ln1_g': out['m_ln1_g'], 'm_ln1_b': out['m_ln1_b'], 'm_w_gate': out['m_w_gate'], 'm_w_up': out['m_w_up'], 'm_w_down': out['m_w_down'], 'm_ln2_g': out['m_ln2_g'], 'm_ln2_b': out['m_ln2_b'], 'v_w_in': out['v_w_in'], 'v_fox_b_f': out['v_fox_b_f'], 'v_mla_g_q': out['v_mla_g_q'], 'v_mla_g_kv': out['v_mla_g_kv'], 'v_mla_w_uq': out['v_mla_w_uq'], 'v_mla_w_ukv': out['v_mla_w_ukv'], 'v_swa_sinks': out['v_swa_sinks'], 'v_mix_g': out['v_mix_g'], 'v_w_o': out['v_w_o'], 'v_ln1_g': out['v_ln1_g'], 'v_ln1_b': out['v_ln1_b'], 'v_w_gate': out['v_w_gate'], 'v_w_up': out['v_w_up'], 'v_w_down': out['v_w_down'], 'v_ln2_g': out['v_ln2_g'], 'v_ln2_b': out['v_ln2_b']}


def _loss(weights, diff, rest, loss_target):
    with _jax.named_scope("forward"):
        args = {**rest, TWIN_DIFF_INPUT: diff, **{k: w.astype(_WEIGHT_DTYPES[k]) for k, w in weights.items()}}
        y = _forward(args)
    with _jax.named_scope("loss_head"):
        err = _jnp.square(y.astype(_jnp.float32) - loss_target)
        return 0.5 * _jnp.sum(_jnp.mean(err, axis=-1)) if err.ndim else 0.5 * err


def _adamw(w, g, m, v):
    m = ADAM_B1 * m + (1.0 - ADAM_B1) * g
    v = ADAM_B2 * v + (1.0 - ADAM_B2) * _jnp.square(g)
    m_hat = m / (1.0 - ADAM_B1 ** ADAM_STEP)
    v_hat = v / (1.0 - ADAM_B2 ** ADAM_STEP)
    delta = -ADAM_LR * (m_hat / (_jnp.sqrt(v_hat) + ADAM_EPS) + ADAM_WD * w)
    return delta, m, v


def reference(x, w_in, fox_b_f, mla_g_q, mla_g_kv, mla_w_uq, mla_w_ukv, swa_sinks, mix_g, w_o, ln1_g, ln1_b, w_gate, w_up, w_down, ln2_g, ln2_b, loss_target, m_w_in, m_fox_b_f, m_mla_g_q, m_mla_g_kv, m_mla_w_uq, m_mla_w_ukv, m_swa_sinks, m_mix_g, m_w_o, m_ln1_g, m_ln1_b, m_w_gate, m_w_up, m_w_down, m_ln2_g, m_ln2_b, v_w_in, v_fox_b_f, v_mla_g_q, v_mla_g_kv, v_mla_w_uq, v_mla_w_ukv, v_swa_sinks, v_mix_g, v_w_o, v_ln1_g, v_ln1_b, v_w_gate, v_w_up, v_w_down, v_ln2_g, v_ln2_b):
    given = dict(x=x, w_in=w_in, fox_b_f=fox_b_f, mla_g_q=mla_g_q, mla_g_kv=mla_g_kv, mla_w_uq=mla_w_uq, mla_w_ukv=mla_w_ukv, swa_sinks=swa_sinks, mix_g=mix_g, w_o=w_o, ln1_g=ln1_g, ln1_b=ln1_b, w_gate=w_gate, w_up=w_up, w_down=w_down, ln2_g=ln2_g, ln2_b=ln2_b, loss_target=loss_target, m_w_in=m_w_in, m_fox_b_f=m_fox_b_f, m_mla_g_q=m_mla_g_q, m_mla_g_kv=m_mla_g_kv, m_mla_w_uq=m_mla_w_uq, m_mla_w_ukv=m_mla_w_ukv, m_swa_sinks=m_swa_sinks, m_mix_g=m_mix_g, m_w_o=m_w_o, m_ln1_g=m_ln1_g, m_ln1_b=m_ln1_b, m_w_gate=m_w_gate, m_w_up=m_w_up, m_w_down=m_w_down, m_ln2_g=m_ln2_g, m_ln2_b=m_ln2_b, v_w_in=v_w_in, v_fox_b_f=v_fox_b_f, v_mla_g_q=v_mla_g_q, v_mla_g_kv=v_mla_g_kv, v_mla_w_uq=v_mla_w_uq, v_mla_w_ukv=v_mla_w_ukv, v_swa_sinks=v_swa_sinks, v_mix_g=v_mix_g, v_w_o=v_w_o, v_ln1_g=v_ln1_g, v_ln1_b=v_ln1_b, v_w_gate=v_w_gate, v_w_up=v_w_up, v_w_down=v_w_down, v_ln2_g=v_ln2_g, v_ln2_b=v_ln2_b)
    weights = {n: given[n] for n in TWIN_WEIGHTS}
    shared = {n: given[n] for n in SHARED_INPUTS}
    per_example = {n: given[n] for n in ['x']}
    grad_fn = _jax.value_and_grad(_loss, argnums=(0, 1))

    def one_microbatch(ex, loss_target):
        ex = dict(ex)
        diff = ex.pop(TWIN_DIFF_INPUT)
        return grad_fn(weights, diff, {**shared, **ex}, loss_target)

    if N_MICROBATCH == 1:
        loss, (grad_w, grad_x) = one_microbatch(per_example, given["loss_target"])
    else:
        def body(carry, xs):
            loss_sum, grad_sum = carry
            l_k, (gw_k, gx_k) = one_microbatch(xs[0], xs[1])
            with _jax.named_scope("update"):
                return (loss_sum + l_k, _jax.tree.map(_jnp.add, grad_sum, gw_k)), gx_k

        init = (_jnp.zeros((), _jnp.float32), _jax.tree.map(_jnp.zeros_like, weights))
        (loss, grad_w), grad_x = _jax.lax.scan(body, init, (per_example, given["loss_target"]))
    with _jax.named_scope("update"):
        delta_w, new_m, new_v = {}, {}, {}
        for n in TWIN_WEIGHTS:
            delta_w[n], new_m[n], new_v[n] = _adamw(weights[n], grad_w[n], given["m_" + n], given["v_" + n])
    return (loss, grad_x, *[grad_w[n] for n in TWIN_WEIGHTS], *[delta_w[n] for n in TWIN_WEIGHTS],
            *[new_m[n] for n in TWIN_WEIGHTS], *[new_v[n] for n in TWIN_WEIGHTS])
```

```python
import functools

import numpy as np
import jax
import jax.numpy as jnp
from jax import lax
from jax.experimental import pallas as pl
from jax.experimental.pallas import tpu as pltpu

F32 = jnp.float32
MXU_DTYPE = jnp.bfloat16
NEG_INF = -1e30

D_MODEL = 1024
DEPTH = 4
HEAD_DIM = 64
GROUP_WIDTH = 256
N_GROUPS = 4
D_FF = 2816
MLA_Q_RANK = 256
MLA_KV_RANK = 128
MLA_ROPE = 32
MLA_QK = 96
MLA_PAD = 128
ROPE_THETA = 10000.0
WINDOW = 128
ALPHA = (2.0 * DEPTH) ** 0.25
SWA_SLOPES = tuple(float(2.0 ** (-8.0 * h / 4)) for h in range(1, 5))
IN_WIDTH = 2468
ATT_W = 2048
LAT_W = 640
PERM_W = ATT_W + LAT_W
COL_FQ, COL_FK, COL_FV = 0, 256, 512
COL_SQ, COL_SK, COL_SV = 768, 1024, 1280
COL_WQ, COL_WK, COL_WV = 1536, 1792, 1920

ADAM_LR, ADAM_B1, ADAM_B2, ADAM_EPS, ADAM_WD, ADAM_STEP = 0.001, 0.9, 0.999, 1e-08, 0.01, 10

VMEM_LIMIT = 56 * 1024 * 1024
NT = (((1,), (1,)), ((), ()))
TN = (((0,), (0,)), ((), ()))
MESH = pl.DeviceIdType.MESH


def _cparams(*sem):
    return pltpu.CompilerParams(dimension_semantics=sem, vmem_limit_bytes=VMEM_LIMIT)


def _split2(x):
    hi = x.astype(MXU_DTYPE)
    lo = (x - hi.astype(F32)).astype(MXU_DTYPE)
    return hi, lo


def _dot01(x, m01, dn=None, parts=2):
    acc = None
    rem = x
    for _ in range(parts):
        part = rem.astype(MXU_DTYPE)
        rem = rem - part.astype(F32)
        if dn is None:
            t = jnp.dot(part, m01, preferred_element_type=F32)
        else:
            t = lax.dot_general(part, m01, dn, preferred_element_type=F32)
        acc = t if acc is None else acc + t
    return acc


def _mm(a, b, *, name, ta=False, tb=False, out_dtype=F32, bm=512, bn=512, bk=512, resid=None, alpha=1.0):
    M, K = (a.shape[1], a.shape[0]) if ta else a.shape
    N = b.shape[0] if tb else b.shape[1]
    assert (b.shape[1] if tb else b.shape[0]) == K
    bm, bn, bk = min(bm, M), min(bn, N), min(bk, K)
    assert M % bm == 0 and N % bn == 0 and K % bk == 0, (name, M, N, K, bm, bn, bk)
    nk = K // bk
    dn = (((0 if ta else 1,), (1 if tb else 0,)), ((), ()))

    def body(*refs):
        if resid is None:
            a_ref, b_ref, o_ref, acc_ref = refs
        else:
            a_ref, b_ref, r_ref, o_ref, acc_ref = refs
        k = pl.program_id(2)

        @pl.when(k == 0)
        def _():
            acc_ref[...] = jnp.zeros_like(acc_ref)

        acc_ref[...] += lax.dot_general(a_ref[...].astype(MXU_DTYPE), b_ref[...].astype(MXU_DTYPE), dn,
                                        preferred_element_type=F32)

        @pl.when(k == nk - 1)
        def _():
            r = acc_ref[...]
            if resid is not None:
                r = r + alpha * r_ref[...]
            o_ref[...] = r.astype(o_ref.dtype)

    a_spec = pl.BlockSpec((bk, bm), lambda i, j, k: (k, i)) if ta else pl.BlockSpec((bm, bk), lambda i, j, k: (i, k))
    b_spec = pl.BlockSpec((bn, bk), lambda i, j, k: (j, k)) if tb else pl.BlockSpec((bk, bn), lambda i, j, k: (k, j))
    in_specs = [a_spec, b_spec]
    args = [a, b]
    if resid is not None:
        in_specs.append(pl.BlockSpec((bm, bn), lambda i, j, k: (i, j)))
        args.append(resid)
    return pl.pallas_call(
        body, name=name, grid=(M // bm, N // bn, nk), in_specs=in_specs,
        out_specs=pl.BlockSpec((bm, bn), lambda i, j, k: (i, j)),
        out_shape=jax.ShapeDtypeStruct((M, N), out_dtype),
        scratch_shapes=[pltpu.VMEM((bm, bn), F32)],
        compiler_params=_cparams("parallel", "parallel", "arbitrary"),
    )(*args)


def _softmax_attn_fwd(q_arr, k_arr, v_arr, *, qcb, kcb, vcb, dk, scale, cum_col=None, cum_row=None, blk, name):
    S = q_arr.shape[0]
    nb = S // blk
    bias = cum_col is not None
    W = 2 * dk

    def body(*refs):
        if bias:
            q_ref, k_ref, v_ref, cc_ref, cr_ref, o_ref, lse_ref = refs
        else:
            q_ref, k_ref, v_ref, o_ref, lse_ref = refs
        p = pl.program_id(0)
        i = pl.program_id(1)
        row = lax.broadcasted_iota(jnp.int32, (blk, blk), 0)
        col = lax.broadcasted_iota(jnp.int32, (blk, blk), 1)
        for hh in range(2):
            q = q_ref[:, hh * dk:(hh + 1) * dk]

            def tile(j, carry, masked, hh=hh, q=q):
                m, l, acc = carry
                r0 = pl.multiple_of(j * blk, blk)
                ks = k_ref[pl.ds(r0, blk), hh * dk:(hh + 1) * dk]
                vs = v_ref[pl.ds(r0, blk), hh * HEAD_DIM:(hh + 1) * HEAD_DIM]
                s = lax.dot_general(q, ks, NT, preferred_element_type=F32) * scale
                if bias:
                    s = s + cc_ref[hh] - cr_ref[(2 * p + hh) * nb + j]
                if masked:
                    s = jnp.where(col <= row, s, NEG_INF)
                mn = jnp.maximum(m, jnp.max(s, axis=1, keepdims=True))
                a = jnp.exp(m - mn)
                pe = jnp.exp(s - mn)
                l = a * l + jnp.sum(pe, axis=1, keepdims=True)
                acc = a * acc + jnp.dot(pe.astype(MXU_DTYPE), vs, preferred_element_type=F32)
                return mn, l, acc

            init = (jnp.full((blk, 1), NEG_INF, F32), jnp.zeros((blk, 1), F32), jnp.zeros((blk, HEAD_DIM), F32))
            carry = lax.fori_loop(0, i, functools.partial(tile, masked=False), init)
            m, l, acc = tile(i, carry, True)
            o_ref[:, hh * HEAD_DIM:(hh + 1) * HEAD_DIM] = acc / l
            lse_ref[hh] = m + jnp.log(l)

    in_specs = [pl.BlockSpec((blk, W), lambda p, i: (i, qcb + p)),
                pl.BlockSpec((S, W), lambda p, i: (0, kcb + p)),
                pl.BlockSpec((S, 128), lambda p, i: (0, vcb + p))]
    args = [q_arr, k_arr, v_arr]
    if bias:
        in_specs += [pl.BlockSpec((2, blk, 1), lambda p, i: (p, i, 0)),
                     pl.BlockSpec((4 * nb, 1, blk), lambda p, i: (0, 0, 0))]
        args += [cum_col, cum_row]
    return pl.pallas_call(
        body, name=name, grid=(2, nb), in_specs=in_specs,
        out_specs=[pl.BlockSpec((blk, 128), lambda p, i: (i, p)), pl.BlockSpec((2, blk, 1), lambda p, i: (p, i, 0))],
        out_shape=[jax.ShapeDtypeStruct((S, GROUP_WIDTH), F32), jax.ShapeDtypeStruct((4, S, 1), F32)],
        compiler_params=_cparams("arbitrary", "arbitrary"),
    )(*args)


def _softmax_attn_bwd(q_arr, k_arr, v_arr, dmix, o_arr, lse, *, qcb, kcb, vcb, dcb, dk, scale,
                      cum_col=None, cum_row=None, blk, name):
    S = q_arr.shape[0]
    nb = S // blk
    bias = cum_col is not None
    W = 2 * dk

    def body(*refs):
        if bias:
            q_ref, k_ref, v_ref, do_ref, o_ref, lse_ref, cc_ref, cr_ref, dq_ref, dk_ref, dv_ref, dc_ref, dcq_ref = refs
        else:
            q_ref, k_ref, v_ref, do_ref, o_ref, lse_ref, dq_ref, dk_ref, dv_ref = refs
        p = pl.program_id(0)
        i = pl.program_id(1)

        @pl.when(i == 0)
        def _():
            dk_ref[...] = jnp.zeros_like(dk_ref)
            dv_ref[...] = jnp.zeros_like(dv_ref)
            if bias:
                dc_ref[...] = jnp.zeros_like(dc_ref)

        row = lax.broadcasted_iota(jnp.int32, (blk, blk), 0)
        col = lax.broadcasted_iota(jnp.int32, (blk, blk), 1)
        for hh in range(2):
            q = q_ref[:, hh * dk:(hh + 1) * dk]
            do = do_ref[:, hh * HEAD_DIM:(hh + 1) * HEAD_DIM]
            delta = jnp.sum(do * o_ref[:, hh * HEAD_DIM:(hh + 1) * HEAD_DIM], axis=1, keepdims=True)
            dob = do.astype(MXU_DTYPE)
            lse_h = lse_ref[hh]

            def tile(j, carry, masked, hh=hh, q=q, dob=dob, delta=delta, lse_h=lse_h):
                dq, dcq = carry
                r0 = pl.multiple_of(j * blk, blk)
                ks = k_ref[pl.ds(r0, blk), hh * dk:(hh + 1) * dk]
                vs = v_ref[pl.ds(r0, blk), hh * HEAD_DIM:(hh + 1) * HEAD_DIM]
                s = lax.dot_general(q, ks, NT, preferred_element_type=F32) * scale
                if bias:
                    s = s + cc_ref[hh] - cr_ref[(2 * p + hh) * nb + j]
                if masked:
                    s = jnp.where(col <= row, s, NEG_INF)
                pr = jnp.exp(s - lse_h)
                dp = lax.dot_general(dob, vs, NT, preferred_element_type=F32)
                ds = pr * (dp - delta)
                dsb = ds.astype(MXU_DTYPE)
                dv_ref[pl.ds(r0, blk), hh * HEAD_DIM:(hh + 1) * HEAD_DIM] += lax.dot_general(
                    pr.astype(MXU_DTYPE), dob, TN, preferred_element_type=F32)
                dk_ref[pl.ds(r0, blk), hh * dk:(hh + 1) * dk] += lax.dot_general(
                    dsb, q, TN, preferred_element_type=F32) * scale
                if bias:
                    dc_ref[hh * nb + j] -= jnp.sum(ds, axis=0, keepdims=True)
                    dcq = dcq + jnp.sum(ds, axis=1, keepdims=True)
                return dq + jnp.dot(dsb, ks, preferred_element_type=F32) * scale, dcq

            carry = lax.fori_loop(0, i, functools.partial(tile, masked=False),
                                  (jnp.zeros((blk, dk), F32), jnp.zeros((blk, 1), F32)))
            dq, dcq = tile(i, carry, True)
            dq_ref[:, hh * dk:(hh + 1) * dk] = dq
            if bias:
                dcq_ref[hh] = dcq

    in_specs = [pl.BlockSpec((blk, W), lambda p, i: (i, qcb + p)),
                pl.BlockSpec((S, W), lambda p, i: (0, kcb + p)),
                pl.BlockSpec((S, 128), lambda p, i: (0, vcb + p)),
                pl.BlockSpec((blk, 128), lambda p, i: (i, dcb + p)),
                pl.BlockSpec((blk, 128), lambda p, i: (i, p)),
                pl.BlockSpec((2, blk, 1), lambda p, i: (p, i, 0))]
    args = [q_arr, k_arr, v_arr, dmix, o_arr, lse]
    out_specs = [pl.BlockSpec((blk, W), lambda p, i: (i, p)),
                 pl.BlockSpec((S, W), lambda p, i: (0, p)),
                 pl.BlockSpec((S, 128), lambda p, i: (0, p))]
    out_shape = [jax.ShapeDtypeStruct((S, 4 * dk), F32), jax.ShapeDtypeStruct((S, 4 * dk), F32),
                 jax.ShapeDtypeStruct((S, GROUP_WIDTH), F32)]
    if bias:
        in_specs += [pl.BlockSpec((2, blk, 1), lambda p, i: (p, i, 0)),
                     pl.BlockSpec((4 * nb, 1, blk), lambda p, i: (0, 0, 0))]
        args += [cum_col, cum_row]
        out_specs += [pl.BlockSpec((2 * nb, 1, blk), lambda p, i: (p, 0, 0)), pl.BlockSpec((2, blk, 1), lambda p, i: (p, i, 0))]
        out_shape += [jax.ShapeDtypeStruct((4 * nb, 1, blk), F32), jax.ShapeDtypeStruct((4, S, 1), F32)]
    return pl.pallas_call(
        body, name=name, grid=(2, nb), in_specs=in_specs, out_specs=out_specs, out_shape=out_shape,
        compiler_params=_cparams("arbitrary", "arbitrary"),
    )(*args)


def _sb_tile(q, ks, scale, strict_mask, carry_l, tri_excl):
    z = lax.dot_general(q, ks, NT, preferred_element_type=F32) * scale
    lb = -(jnp.maximum(z, 0.0) + jnp.log(1.0 + jnp.exp(-jnp.abs(z))))
    if strict_mask is not None:
        lb = jnp.where(strict_mask, lb, 0.0)
    between = _dot01(lb, tri_excl) + carry_l
    a = jnp.exp(z + lb + between)
    if strict_mask is not None:
        a = jnp.where(strict_mask, a, 0.0)
    return z, lb, a


def _sb_attn_fwd(h_att, *, blk, name):
    S = h_att.shape[0]
    nb = S // blk
    scale = HEAD_DIM ** -0.5
    qcb, kcb, vcb = COL_SQ // 128, COL_SK // 128, COL_SV // 128

    def body(q_ref, k_ref, v_ref, o_ref, lt_ref):
        i = pl.program_id(1)
        row = lax.broadcasted_iota(jnp.int32, (blk, blk), 0)
        col = lax.broadcasted_iota(jnp.int32, (blk, blk), 1)
        strict = col < row
        tri_excl = (row > col).astype(MXU_DTYPE)
        for hh in range(2):
            sl = slice(hh * HEAD_DIM, (hh + 1) * HEAD_DIM)
            q = q_ref[:, sl]

            def tile(j, carry, mask, sl=sl, q=q):
                cl, acc = carry
                r0 = pl.multiple_of(j * blk, blk)
                _, lb, a = _sb_tile(q, k_ref[pl.ds(r0, blk), sl], scale, mask, cl, tri_excl)
                acc = acc + jnp.dot(a.astype(MXU_DTYPE), v_ref[pl.ds(r0, blk), sl], preferred_element_type=F32)
                return cl + jnp.sum(lb, axis=1, keepdims=True), acc

            carry = tile(i, (jnp.zeros((blk, 1), F32), jnp.zeros((blk, HEAD_DIM), F32)), strict)
            cl, acc = lax.fori_loop(0, i, lambda jj, c: tile(i - 1 - jj, c, None), carry)
            o_ref[:, sl] = acc
            lt_ref[hh] = cl

    return pl.pallas_call(
        body, name=name, grid=(2, nb),
        in_specs=[pl.BlockSpec((blk, 128), lambda p, i: (i, qcb + p)),
                  pl.BlockSpec((S, 128), lambda p, i: (0, kcb + p)),
                  pl.BlockSpec((S, 128), lambda p, i: (0, vcb + p))],
        out_specs=[pl.BlockSpec((blk, 128), lambda p, i: (i, p)), pl.BlockSpec((2, blk, 1), lambda p, i: (p, i, 0))],
        out_shape=[jax.ShapeDtypeStruct((S, GROUP_WIDTH), F32), jax.ShapeDtypeStruct((4, S, 1), F32)],
        compiler_params=_cparams("arbitrary", "arbitrary"),
    )(h_att, h_att, h_att)


def _sb_attn_bwd(h_att, dmix, ltot_arr, *, dcb, blk, name):
    S = h_att.shape[0]
    nb = S // blk
    scale = HEAD_DIM ** -0.5
    qcb, kcb, vcb = COL_SQ // 128, COL_SK // 128, COL_SV // 128

    def body(q_ref, k_ref, v_ref, do_ref, lt_ref, dq_ref, dk_ref, dv_ref):
        i = pl.program_id(1)

        @pl.when(i == 0)
        def _():
            dk_ref[...] = jnp.zeros_like(dk_ref)
            dv_ref[...] = jnp.zeros_like(dv_ref)

        row = lax.broadcasted_iota(jnp.int32, (blk, blk), 0)
        col = lax.broadcasted_iota(jnp.int32, (blk, blk), 1)
        strict = col < row
        up_incl = (row <= col).astype(MXU_DTYPE)
        up_excl = (row < col).astype(MXU_DTYPE)
        for hh in range(2):
            sl = slice(hh * HEAD_DIM, (hh + 1) * HEAD_DIM)
            q = q_ref[:, sl]
            dob = do_ref[:, sl].astype(MXU_DTYPE)
            ltot = lt_ref[hh]

            def tile(j, carry, mask, sl=sl, q=q, dob=dob, ltot=ltot):
                cl, cg, dq = carry
                r0 = pl.multiple_of(j * blk, blk)
                ks = k_ref[pl.ds(r0, blk), sl]
                vs = v_ref[pl.ds(r0, blk), sl]
                z = lax.dot_general(q, ks, NT, preferred_element_type=F32) * scale
                lb = -(jnp.maximum(z, 0.0) + jnp.log(1.0 + jnp.exp(-jnp.abs(z))))
                if mask is not None:
                    lb = jnp.where(mask, lb, 0.0)
                between = ltot - cl - _dot01(lb, up_incl)
                a = jnp.exp(z + lb + between)
                if mask is not None:
                    a = jnp.where(mask, a, 0.0)
                g = lax.dot_general(dob, vs, NT, preferred_element_type=F32) * a
                e = cg + _dot01(g, up_excl)
                dz = g * jnp.exp(lb) - e * jnp.exp(z + lb)
                if mask is not None:
                    dz = jnp.where(mask, dz, 0.0)
                dzb = dz.astype(MXU_DTYPE)
                dv_ref[pl.ds(r0, blk), sl] += lax.dot_general(a.astype(MXU_DTYPE), dob, TN, preferred_element_type=F32)
                dk_ref[pl.ds(r0, blk), sl] += lax.dot_general(dzb, q, TN, preferred_element_type=F32) * scale
                dq = dq + jnp.dot(dzb, ks, preferred_element_type=F32) * scale
                return cl + jnp.sum(lb, axis=1, keepdims=True), cg + jnp.sum(g, axis=1, keepdims=True), dq

            zc = jnp.zeros((blk, 1), F32)
            carry = lax.fori_loop(0, i, lambda j, c: tile(j, c, None), (zc, zc, jnp.zeros((blk, HEAD_DIM), F32)))
            _, _, dq = tile(i, carry, strict)
            dq_ref[:, sl] = dq

    return pl.pallas_call(
        body, name=name, grid=(2, nb),
        in_specs=[pl.BlockSpec((blk, 128), lambda p, i: (i, qcb + p)),
                  pl.BlockSpec((S, 128), lambda p, i: (0, kcb + p)),
                  pl.BlockSpec((S, 128), lambda p, i: (0, vcb + p)),
                  pl.BlockSpec((blk, 128), lambda p, i: (i, dcb + p)),
                  pl.BlockSpec((2, blk, 1), lambda p, i: (p, i, 0))],
        out_specs=[pl.BlockSpec((blk, 128), lambda p, i: (i, p)),
                   pl.BlockSpec((S, 128), lambda p, i: (0, p)),
                   pl.BlockSpec((S, 128), lambda p, i: (0, p))],
        out_shape=[jax.ShapeDtypeStruct((S, GROUP_WIDTH), F32)] * 3,
        compiler_params=_cparams("arbitrary", "arbitrary"),
    )(h_att, h_att, h_att, dmix, ltot_arr)


def _swa_scores(q_ref, k_ref, n, h, start):
    g = h // 2
    kb = k_ref[pl.ds(start, 2 * WINDOW), g * HEAD_DIM:(g + 1) * HEAD_DIM]
    s = lax.dot_general(q_ref[:, h * HEAD_DIM:(h + 1) * HEAD_DIM], kb, NT, preferred_element_type=F32) * (HEAD_DIM ** -0.5)
    dist = (n * WINDOW + lax.broadcasted_iota(jnp.int32, (WINDOW, 2 * WINDOW), 0)
            - start - lax.broadcasted_iota(jnp.int32, (WINDOW, 2 * WINDOW), 1))
    s = s - SWA_SLOPES[h] * dist.astype(F32)
    valid = (dist >= 0) & (dist < WINDOW)
    return jnp.where(valid, s, NEG_INF), kb


def _swa_fwd(h_att, sinks, *, name):
    S = h_att.shape[0]
    nb = S // WINDOW
    qcb, kcb, vcb = COL_WQ // 256, COL_WK // 128, COL_WV // 128

    def body(sink_ref, q_ref, k_ref, v_ref, o_ref, lse_ref):
        n = pl.program_id(0)
        start = pl.multiple_of(jnp.maximum(n - 1, 0) * WINDOW, WINDOW)
        for h in range(4):
            g = h // 2
            s, _ = _swa_scores(q_ref, k_ref, n, h, start)
            sink = sink_ref[h]
            m = jnp.maximum(jnp.max(s, axis=1, keepdims=True), sink)
            e = jnp.exp(s - m)
            den = jnp.sum(e, axis=1, keepdims=True) + jnp.exp(sink - m)
            vb = v_ref[pl.ds(start, 2 * WINDOW), g * HEAD_DIM:(g + 1) * HEAD_DIM]
            o_ref[:, h * HEAD_DIM:(h + 1) * HEAD_DIM] = jnp.dot((e / den).astype(MXU_DTYPE), vb, preferred_element_type=F32)
            lse_ref[h] = m + jnp.log(den)

    return pl.pallas_call(
        body, name=name, grid=(nb,),
        in_specs=[pl.BlockSpec(memory_space=pltpu.SMEM),
                  pl.BlockSpec((WINDOW, 256), lambda n: (n, qcb)),
                  pl.BlockSpec((S, 128), lambda n: (0, kcb)),
                  pl.BlockSpec((S, 128), lambda n: (0, vcb))],
        out_specs=[pl.BlockSpec((WINDOW, 256), lambda n: (n, 0)), pl.BlockSpec((4, WINDOW, 1), lambda n: (0, n, 0))],
        out_shape=[jax.ShapeDtypeStruct((S, GROUP_WIDTH), F32), jax.ShapeDtypeStruct((4, S, 1), F32)],
        compiler_params=_cparams("arbitrary"),
    )(sinks, h_att, h_att, h_att)


def _swa_bwd(h_att, sinks, dmix, o_arr, lse, *, dcb, name):
    S = h_att.shape[0]
    nb = S // WINDOW
    qcb, kcb, vcb = COL_WQ // 256, COL_WK // 128, COL_WV // 128

    def body(sink_ref, q_ref, k_ref, v_ref, do_ref, o_ref, lse_ref, dq_ref, dk_ref, dv_ref, dsink_ref):
        n = pl.program_id(0)

        @pl.when(n == 0)
        def _():
            dk_ref[...] = jnp.zeros_like(dk_ref)
            dv_ref[...] = jnp.zeros_like(dv_ref)
            dsink_ref[...] = jnp.zeros_like(dsink_ref)

        start = pl.multiple_of(jnp.maximum(n - 1, 0) * WINDOW, WINDOW)
        for h in range(4):
            g = h // 2
            sl = slice(h * HEAD_DIM, (h + 1) * HEAD_DIM)
            gl = slice(g * HEAD_DIM, (g + 1) * HEAD_DIM)
            s, kb = _swa_scores(q_ref, k_ref, n, h, start)
            lse_h = lse_ref[h]
            pr = jnp.exp(s - lse_h)
            do = do_ref[:, sl]
            dob = do.astype(MXU_DTYPE)
            delta = jnp.sum(do * o_ref[:, sl], axis=1, keepdims=True)
            vb = v_ref[pl.ds(start, 2 * WINDOW), gl]
            ds = pr * (lax.dot_general(dob, vb, NT, preferred_element_type=F32) - delta)
            dsb = ds.astype(MXU_DTYPE)
            dq_ref[:, sl] = jnp.dot(dsb, kb, preferred_element_type=F32) * (HEAD_DIM ** -0.5)
            dk_ref[pl.ds(start, 2 * WINDOW), gl] += lax.dot_general(
                dsb, q_ref[:, sl], TN, preferred_element_type=F32) * (HEAD_DIM ** -0.5)
            dv_ref[pl.ds(start, 2 * WINDOW), gl] += lax.dot_general(pr.astype(MXU_DTYPE), dob, TN, preferred_element_type=F32)
            dsink_ref[h:h + 1, :] += jnp.zeros((1, 128), F32) - jnp.sum(jnp.exp(sink_ref[h] - lse_h) * delta)

    return pl.pallas_call(
        body, name=name, grid=(nb,),
        in_specs=[pl.BlockSpec(memory_space=pltpu.SMEM),
                  pl.BlockSpec((WINDOW, 256), lambda n: (n, qcb)),
                  pl.BlockSpec((S, 128), lambda n: (0, kcb)),
                  pl.BlockSpec((S, 128), lambda n: (0, vcb)),
                  pl.BlockSpec((WINDOW, 256), lambda n: (n, dcb)),
                  pl.BlockSpec((WINDOW, 256), lambda n: (n, 0)),
                  pl.BlockSpec((4, WINDOW, 1), lambda n: (0, n, 0))],
        out_specs=[pl.BlockSpec((WINDOW, 256), lambda n: (n, 0)),
                   pl.BlockSpec((S, 128), lambda n: (0, 0)),
                   pl.BlockSpec((S, 128), lambda n: (0, 0)),
                   pl.BlockSpec((4, 128), lambda n: (0, 0))],
        out_shape=[jax.ShapeDtypeStruct((S, GROUP_WIDTH), F32), jax.ShapeDtypeStruct((S, 128), F32),
                   jax.ShapeDtypeStruct((S, 128), F32), jax.ShapeDtypeStruct((4, 128), F32)],
        compiler_params=_cparams("arbitrary"),
    )(sinks, h_att, h_att, h_att, dmix, o_arr, lse)


def _tri(n, incl, upper):
    r = lax.broadcasted_iota(jnp.int32, (n, n), 0)
    c = lax.broadcasted_iota(jnp.int32, (n, n), 1)
    if upper:
        m = (r <= c) if incl else (r < c)
    else:
        m = (r >= c) if incl else (r > c)
    return m.astype(MXU_DTYPE)


def _fox_gate_fwd(fg, b_f, *, name):
    _, R, _ = fg.shape

    def body(b_ref, fg_ref, cum_ref):
        up_incl = _tri(128, True, True)
        ones = jnp.ones((128, 128), MXU_DTYPE)
        for h in range(4):
            z = fg_ref[h] + b_ref[h]
            logf = jnp.minimum(z, 0.0) - jnp.log(1.0 + jnp.exp(-jnp.abs(z)))
            within = _dot01(logf, up_incl, parts=3)
            totals = _dot01(logf, ones, parts=3)
            cum_ref[h] = within + _rows_other(totals, R, after=False)

    return pl.pallas_call(
        body, name=name,
        in_specs=[pl.BlockSpec(memory_space=pltpu.SMEM), pl.BlockSpec(memory_space=pltpu.VMEM)],
        out_specs=pl.BlockSpec(memory_space=pltpu.VMEM),
        out_shape=jax.ShapeDtypeStruct(fg.shape, F32),
    )(b_f, fg)


def _rows_other(totals, n, after):
    r = lax.broadcasted_iota(jnp.int32, (n, n), 0)
    c = lax.broadcasted_iota(jnp.int32, (n, n), 1)
    m = ((c > r) if after else (c < r)).astype(MXU_DTYPE)
    acc = None
    rem = totals
    for _ in range(3):
        part = rem.astype(MXU_DTYPE)
        rem = rem - part.astype(F32)
        t = jnp.dot(m, part, preferred_element_type=F32)
        acc = t if acc is None else acc + t
    return acc


def _fox_gate_bwd(fg, b_f, dcum_k, dcum_q, *, name):
    _, R, _ = fg.shape

    def body(b_ref, fg_ref, dck_ref, dcq_ref, dfg_ref, db_ref):
        low_incl = _tri(128, True, False)
        ones = jnp.ones((128, 128), MXU_DTYPE)
        for h in range(4):
            dc = dck_ref[h] + dcq_ref[h]
            dlogf = _dot01(dc, low_incl, parts=3) + _rows_other(_dot01(dc, ones, parts=3), R, after=True)
            z = fg_ref[h] + b_ref[h]
            dz = dlogf * jnp.exp(jnp.minimum(-z, 0.0) - jnp.log(1.0 + jnp.exp(-jnp.abs(z))))
            dfg_ref[h] = dz
            db_ref[h:h + 1, :] = jnp.zeros((1, 128), F32) + jnp.sum(dz)

    return pl.pallas_call(
        body, name=name,
        in_specs=[pl.BlockSpec(memory_space=pltpu.SMEM)] + [pl.BlockSpec(memory_space=pltpu.VMEM)] * 3,
        out_specs=[pl.BlockSpec(memory_space=pltpu.VMEM), pl.BlockSpec(memory_space=pltpu.VMEM)],
        out_shape=[jax.ShapeDtypeStruct(fg.shape, F32), jax.ShapeDtypeStruct((4, 128), F32)],
    )(b_f, fg, dcum_k, dcum_q)


def _rope_rot(transpose):
    r = lax.broadcasted_iota(jnp.int32, (MLA_PAD, MLA_PAD), 0)
    c = lax.broadcasted_iota(jnp.int32, (MLA_PAD, MLA_PAD), 1)
    if transpose:
        r, c = c, r
    half = MLA_ROPE // 2
    lo, mid, hi = HEAD_DIM, HEAD_DIM + half, HEAD_DIM + MLA_ROPE
    minus = (c >= lo) & (c < mid) & (r == c + half)
    plus = (c >= mid) & (c < hi) & (r == c - half)
    return jnp.where(plus, 1.0, jnp.where(minus, -1.0, 0.0)).astype(MXU_DTYPE)


def _rope_lanes():
    lane = lax.broadcasted_iota(jnp.int32, (1, MLA_PAD), 1)
    return ((lane >= HEAD_DIM) & (lane < HEAD_DIM + MLA_ROPE)).astype(F32)


def _rms(x, g, eps=1e-6):
    r = lax.rsqrt(jnp.mean(x * x, axis=-1, keepdims=True) + eps)
    return x * r * g, r


def _rms_bwd(dy, x, r, g):
    xh = x * r
    dxh = dy * g
    dx = r * (dxh - xh * jnp.mean(dxh * xh, axis=-1, keepdims=True))
    return dx, dy * xh


def _mla_prep_fwd(lat, g_q, g_kv, wuq, wuk, wuv, cosm, sinm, *, bs, name):
    S = lat.shape[0]

    def body(lat_ref, gq_ref, gkv_ref, wuq_ref, wuk_ref, wuv_ref, cos_ref, sin_ref, q_ref, k_ref, v_ref):
        rot = _rope_rot(False)
        cosm_, sinm_ = cos_ref[...], sin_ref[...]
        nq, _ = _rms(lat_ref[:, 0:MLA_Q_RANK], gq_ref[...])
        nkv, _ = _rms(lat_ref[:, MLA_Q_RANK:MLA_Q_RANK + MLA_KV_RANK], gkv_ref[...])
        qlat = jnp.dot(nq.astype(MXU_DTYPE), wuq_ref[...], preferred_element_type=F32)
        klat = jnp.dot(nkv.astype(MXU_DTYPE), wuk_ref[...], preferred_element_type=F32)
        v_ref[...] = jnp.dot(nkv.astype(MXU_DTYPE), wuv_ref[...], preferred_element_type=F32).astype(v_ref.dtype)
        krb = lat_ref[:, 384:512]
        kr = krb * (cosm_ * _rope_lanes()) + _dot01(krb, rot, parts=3) * sinm_
        for h in range(4):
            sl = slice(h * MLA_PAD, (h + 1) * MLA_PAD)
            qh = qlat[:, sl]
            q_ref[:, sl] = (qh * cosm_ + _dot01(qh, rot, parts=3) * sinm_).astype(q_ref.dtype)
            k_ref[:, sl] = (klat[:, sl] + kr).astype(k_ref.dtype)

    full = lambda a: pl.BlockSpec(a.shape, lambda i: (0,) * a.ndim)
    return pl.pallas_call(
        body, name=name, grid=(S // bs,),
        in_specs=[pl.BlockSpec((bs, LAT_W), lambda i: (i, 0)), full(g_q), full(g_kv), full(wuq), full(wuk), full(wuv),
                  pl.BlockSpec((bs, MLA_PAD), lambda i: (i, 0)), pl.BlockSpec((bs, MLA_PAD), lambda i: (i, 0))],
        out_specs=[pl.BlockSpec((bs, 512), lambda i: (i, 0)), pl.BlockSpec((bs, 512), lambda i: (i, 0)),
                   pl.BlockSpec((bs, 256), lambda i: (i, 0))],
        out_shape=[jax.ShapeDtypeStruct((S, 512), MXU_DTYPE), jax.ShapeDtypeStruct((S, 512), MXU_DTYPE),
                   jax.ShapeDtypeStruct((S, 256), MXU_DTYPE)],
        compiler_params=_cparams("parallel"),
    )(lat, g_q, g_kv, wuq, wuk, wuv, cosm, sinm)


def _mla_prep_bwd(lat, g_q, g_kv, wuq, wuk, wuv, cosm, sinm, dq, dk, dv, *, bs, name):
    S = lat.shape[0]

    def body(lat_ref, gq_ref, gkv_ref, wuq_ref, wuk_ref, wuv_ref, cos_ref, sin_ref, dq_ref, dk_ref, dv_ref,
             dlat_ref, dwuq_ref, dwuk_ref, dwuv_ref, dgq_ref, dgkv_ref):
        @pl.when(pl.program_id(0) == 0)
        def _():
            for r in (dwuq_ref, dwuk_ref, dwuv_ref, dgq_ref, dgkv_ref):
                r[...] = jnp.zeros_like(r)

        rot_t = _rope_rot(True)
        cosm_, sinm_ = cos_ref[...], sin_ref[...]
        cq = lat_ref[:, 0:MLA_Q_RANK]
        ckv = lat_ref[:, MLA_Q_RANK:MLA_Q_RANK + MLA_KV_RANK]
        nq, rq = _rms(cq, gq_ref[...])
        nkv, rkv = _rms(ckv, gkv_ref[...])
        nqb, nkvb = nq.astype(MXU_DTYPE), nkv.astype(MXU_DTYPE)

        dqlat = []
        dkr = jnp.zeros((bs, MLA_PAD), F32)
        for h in range(4):
            sl = slice(h * MLA_PAD, (h + 1) * MLA_PAD)
            dqh = dq_ref[:, sl]
            dqlat.append(dqh * cosm_ + _dot01(dqh * sinm_, rot_t, parts=3))
            dkr = dkr + dk_ref[:, sl]
        dqlat = jnp.concatenate(dqlat, axis=1).astype(MXU_DTYPE)
        dkb = dk_ref[...].astype(MXU_DTYPE)
        dvb = dv_ref[...].astype(MXU_DTYPE)

        dnq = lax.dot_general(dqlat, wuq_ref[...], NT, preferred_element_type=F32)
        dnkv = (lax.dot_general(dkb, wuk_ref[...], NT, preferred_element_type=F32)
                + lax.dot_general(dvb, wuv_ref[...], NT, preferred_element_type=F32))
        dwuq_ref[...] += lax.dot_general(nqb, dqlat, TN, preferred_element_type=F32)
        dwuk_ref[...] += lax.dot_general(nkvb, dkb, TN, preferred_element_type=F32)
        dwuv_ref[...] += lax.dot_general(nkvb, dvb, TN, preferred_element_type=F32)
        dcq, tq = _rms_bwd(dnq, cq, rq, gq_ref[...])
        dckv, tkv = _rms_bwd(dnkv, ckv, rkv, gkv_ref[...])
        dgq_ref[...] += jnp.sum(tq, axis=0, keepdims=True)
        dgkv_ref[...] += jnp.sum(tkv, axis=0, keepdims=True)
        dlat_ref[:, 0:MLA_Q_RANK] = dcq.astype(dlat_ref.dtype)
        dlat_ref[:, MLA_Q_RANK:MLA_Q_RANK + MLA_KV_RANK] = dckv.astype(dlat_ref.dtype)
        dkrb = dkr * (cosm_ * _rope_lanes()) + _dot01(dkr * sinm_, rot_t, parts=3)
        dlat_ref[:, 384:512] = dkrb.astype(dlat_ref.dtype)

    full = lambda a: pl.BlockSpec(a.shape, lambda i: (0,) * a.ndim)
    row = lambda w: pl.BlockSpec((bs, w), lambda i: (i, 0))
    acc = lambda *shape: pl.BlockSpec(shape, lambda i: (0,) * len(shape))
    return pl.pallas_call(
        body, name=name, grid=(S // bs,),
        in_specs=[row(LAT_W), full(g_q), full(g_kv), full(wuq), full(wuk), full(wuv), row(MLA_PAD), row(MLA_PAD),
                  row(512), row(512), row(256)],
        out_specs=[row(512), acc(256, 512), acc(128, 512), acc(128, 256), acc(1, 256), acc(1, 128)],
        out_shape=[jax.ShapeDtypeStruct((S, 512), MXU_DTYPE), jax.ShapeDtypeStruct((256, 512), F32),
                   jax.ShapeDtypeStruct((128, 512), F32), jax.ShapeDtypeStruct((128, 256), F32),
                   jax.ShapeDtypeStruct((1, 256), F32), jax.ShapeDtypeStruct((1, 128), F32)],
        compiler_params=_cparams("arbitrary"),
    )(lat, g_q, g_kv, wuq, wuk, wuv, cosm, sinm, dq, dk, dv)


def _row_spec(bs, w):
    return pl.BlockSpec((bs, w), lambda i: (i, 0))


def _vec_spec(w):
    return pl.BlockSpec((1, w), lambda i: (0, 0))


def _gnorm_fwd(outs, g, *, bs, name):
    S = outs[0].shape[0]

    def body(a_ref, b_ref, c_ref, d_ref, g_ref, o_ref):
        for k, ref in enumerate((a_ref, b_ref, c_ref, d_ref)):
            sl = slice(k * GROUP_WIDTH, (k + 1) * GROUP_WIDTH)
            y, _ = _rms(ref[...], g_ref[:, sl])
            o_ref[:, sl] = y.astype(o_ref.dtype)

    return pl.pallas_call(
        body, name=name, grid=(S // bs,),
        in_specs=[_row_spec(bs, GROUP_WIDTH)] * 4 + [_vec_spec(D_MODEL)],
        out_specs=_row_spec(bs, D_MODEL), out_shape=jax.ShapeDtypeStruct((S, D_MODEL), MXU_DTYPE),
        compiler_params=_cparams("parallel"),
    )(*outs, g)


def _gnorm_bwd(dgn, outs, g, *, bs, name):
    S = dgn.shape[0]

    def body(dgn_ref, a_ref, b_ref, c_ref, d_ref, g_ref, dmix_ref, dg_ref):
        @pl.when(pl.program_id(0) == 0)
        def _():
            dg_ref[...] = jnp.zeros_like(dg_ref)

        for k, ref in enumerate((a_ref, b_ref, c_ref, d_ref)):
            sl = slice(k * GROUP_WIDTH, (k + 1) * GROUP_WIDTH)
            x = ref[...]
            _, r = _rms(x, g_ref[:, sl])
            dx, t = _rms_bwd(dgn_ref[:, sl], x, r, g_ref[:, sl])
            dmix_ref[:, sl] = dx
            dg_ref[:, sl] += jnp.sum(t, axis=0, keepdims=True)

    return pl.pallas_call(
        body, name=name, grid=(S // bs,),
        in_specs=[_row_spec(bs, D_MODEL)] + [_row_spec(bs, GROUP_WIDTH)] * 4 + [_vec_spec(D_MODEL)],
        out_specs=[_row_spec(bs, D_MODEL), _vec_spec(D_MODEL)],
        out_shape=[jax.ShapeDtypeStruct((S, D_MODEL), F32), jax.ShapeDtypeStruct((1, D_MODEL), F32)],
        compiler_params=_cparams("arbitrary"),
    )(dgn, *outs, g)


def _ln_fwd(u, g, b, *, bs, name):
    S = u.shape[0]

    def body(u_ref, g_ref, b_ref, y_ref, yb_ref, xh_ref, rs_ref):
        x = u_ref[...]
        mu = jnp.mean(x, axis=-1, keepdims=True)
        xc = x - mu
        rs = lax.rsqrt(jnp.mean(xc * xc, axis=-1, keepdims=True) + 1e-5)
        xh = xc * rs
        y = xh * g_ref[...] + b_ref[...]
        y_ref[...] = y
        yb_ref[...] = y.astype(yb_ref.dtype)
        xh_ref[...] = xh
        rs_ref[...] = rs

    return pl.pallas_call(
        body, name=name, grid=(S // bs,),
        in_specs=[_row_spec(bs, D_MODEL), _vec_spec(D_MODEL), _vec_spec(D_MODEL)],
        out_specs=[_row_spec(bs, D_MODEL), _row_spec(bs, D_MODEL), _row_spec(bs, D_MODEL), _row_spec(bs, 1)],
        out_shape=[jax.ShapeDtypeStruct((S, D_MODEL), F32), jax.ShapeDtypeStruct((S, D_MODEL), MXU_DTYPE),
                   jax.ShapeDtypeStruct((S, D_MODEL), F32), jax.ShapeDtypeStruct((S, 1), F32)],
        compiler_params=_cparams("parallel"),
    )(u, g, b)


def _ln_bwd(dy, xh, rs, g, *, bs, name):
    S = dy.shape[0]

    def body(dy_ref, xh_ref, rs_ref, g_ref, du_ref, dub_ref, dg_ref, db_ref):
        @pl.when(pl.program_id(0) == 0)
        def _():
            dg_ref[...] = jnp.zeros_like(dg_ref)
            db_ref[...] = jnp.zeros_like(db_ref)

        dy_, xh_ = dy_ref[...], xh_ref[...]
        dxh = dy_ * g_ref[...]
        du = rs_ref[...] * (dxh - jnp.mean(dxh, axis=-1, keepdims=True)
                            - xh_ * jnp.mean(dxh * xh_, axis=-1, keepdims=True))
        du_ref[...] = du
        dub_ref[...] = du.astype(dub_ref.dtype)
        dg_ref[...] += jnp.sum(dy_ * xh_, axis=0, keepdims=True)
        db_ref[...] += jnp.sum(dy_, axis=0, keepdims=True)

    return pl.pallas_call(
        body, name=name, grid=(S // bs,),
        in_specs=[_row_spec(bs, D_MODEL), _row_spec(bs, D_MODEL), _row_spec(bs, 1), _vec_spec(D_MODEL)],
        out_specs=[_row_spec(bs, D_MODEL), _row_spec(bs, D_MODEL), _vec_spec(D_MODEL), _vec_spec(D_MODEL)],
        out_shape=[jax.ShapeDtypeStruct((S, D_MODEL), F32), jax.ShapeDtypeStruct((S, D_MODEL), MXU_DTYPE),
                   jax.ShapeDtypeStruct((1, D_MODEL), F32), jax.ShapeDtypeStruct((1, D_MODEL), F32)],
        compiler_params=_cparams("arbitrary"),
    )(dy, xh, rs, g)


def _swiglu_fwd(gu, *, bs, name):
    S = gu.shape[0]

    def body(gu_ref, a_ref):
        gt = gu_ref[:, :D_FF]
        a_ref[...] = (gt / (1.0 + jnp.exp(-gt)) * gu_ref[:, D_FF:]).astype(a_ref.dtype)

    return pl.pallas_call(
        body, name=name, grid=(S // bs,),
        in_specs=[_row_spec(bs, 2 * D_FF)],
        out_specs=_row_spec(bs, D_FF), out_shape=jax.ShapeDtypeStruct((S, D_FF), MXU_DTYPE),
        compiler_params=_cparams("parallel"),
    )(gu)


def _swiglu_bwd(da, gu, *, bs, name):
    S = gu.shape[0]

    def body(da_ref, gu_ref, dgu_ref):
        gt, da_ = gu_ref[:, :D_FF], da_ref[...]
        sg = 1.0 / (1.0 + jnp.exp(-gt))
        silu = gt * sg
        dgu_ref[:, :D_FF] = (da_ * gu_ref[:, D_FF:] * (sg + silu * (1.0 - sg))).astype(dgu_ref.dtype)
        dgu_ref[:, D_FF:] = (da_ * silu).astype(dgu_ref.dtype)

    return pl.pallas_call(
        body, name=name, grid=(S // bs,),
        in_specs=[_row_spec(bs, D_FF), _row_spec(bs, 2 * D_FF)],
        out_specs=_row_spec(bs, 2 * D_FF), out_shape=jax.ShapeDtypeStruct((S, 2 * D_FF), MXU_DTYPE),
        compiler_params=_cparams("parallel"),
    )(da, gu)


def _loss_head(y, target, *, bs, name):
    S = y.shape[0]

    def body(y_ref, t_ref, dy_ref, loss_ref):
        @pl.when(pl.program_id(0) == 0)
        def _():
            loss_ref[...] = jnp.zeros_like(loss_ref)

        e = y_ref[...] - t_ref[...]
        dy_ref[...] = e * (1.0 / D_MODEL)
        per_tok = jnp.mean(e * e, axis=-1, keepdims=True)
        loss_ref[...] += 0.5 * jnp.sum(per_tok, axis=0, keepdims=True)

    return pl.pallas_call(
        body, name=name, grid=(S // bs,),
        in_specs=[_row_spec(bs, D_MODEL), _row_spec(bs, D_MODEL)],
        out_specs=[_row_spec(bs, D_MODEL), pl.BlockSpec((1, 1), lambda i: (0, 0))],
        out_shape=[jax.ShapeDtypeStruct((S, D_MODEL), F32), jax.ShapeDtypeStruct((1, 1), F32)],
        compiler_params=_cparams("arbitrary"),
    )(y, target)


def _blk(n, target):
    if n <= target:
        return n
    best = None
    for b in range(128, target + 1, 128):
        if n % b == 0:
            best = b
    assert best is not None, n
    return best


def _rope_tables(S):
    pos = jnp.arange(S, dtype=F32)
    inv = ROPE_THETA ** (-jnp.arange(0, MLA_ROPE, 2, dtype=F32) / MLA_ROPE)
    ang = pos[:, None] * inv[None, :]
    cos, sin = jnp.cos(ang), jnp.sin(ang)
    one, zero, pad = jnp.ones((S, HEAD_DIM), F32), jnp.zeros((S, HEAD_DIM), F32), jnp.zeros((S, MLA_PAD - MLA_QK), F32)
    return jnp.concatenate([one, cos, cos, pad], axis=1), jnp.concatenate([zero, sin, sin, pad], axis=1)


def _prep_weights(w_in, w_uq, w_ukv, w_o, w_gate, w_up, w_down):
    z = lambda n: jnp.zeros((D_MODEL, n), w_in.dtype)
    win_a = jnp.concatenate([w_in[:, 0:768], w_in[:, 1188:2468]], axis=1)
    win_l = jnp.concatenate([w_in[:, 772:1156], z(64), w_in[:, 1156:1188], z(32), w_in[:, 768:772], z(124)], axis=1)
    kv = w_ukv.reshape(MLA_KV_RANK, 4, 2 * HEAD_DIM)
    return dict(
        win_a=win_a, win_l=win_l, win_p=jnp.concatenate([win_a, win_l], axis=1),
        wuq=jnp.pad(w_uq.reshape(MLA_Q_RANK, 4, MLA_QK), ((0, 0), (0, 0), (0, MLA_PAD - MLA_QK))).reshape(MLA_Q_RANK, 512),
        wuk=jnp.pad(kv[:, :, :HEAD_DIM], ((0, 0), (0, 0), (0, HEAD_DIM))).reshape(MLA_KV_RANK, 512),
        wuv=kv[:, :, HEAD_DIM:].reshape(MLA_KV_RANK, 256),
        w_o=w_o, wgu=jnp.concatenate([w_gate, w_up], axis=1), w_down=w_down)


def _unprep_grads(dwin_p, dwuq, dwuk, dwuv, dwo, dwgu, dwd):
    dw_in = jnp.concatenate([dwin_p[:, 0:768], dwin_p[:, 2560:2564], dwin_p[:, 2048:2432], dwin_p[:, 2496:2528],
                             dwin_p[:, 768:2048]], axis=1)
    dw_uq = dwuq.reshape(MLA_Q_RANK, 4, MLA_PAD)[:, :, :MLA_QK].reshape(MLA_Q_RANK, 4 * MLA_QK)
    dw_ukv = jnp.concatenate([dwuk.reshape(MLA_KV_RANK, 4, MLA_PAD)[:, :, :HEAD_DIM],
                              dwuv.reshape(MLA_KV_RANK, 4, HEAD_DIM)], axis=2).reshape(MLA_KV_RANK, 512)
    return dict(w_in=dw_in, mla_w_uq=dw_uq, mla_w_ukv=dw_ukv, w_o=dwo, w_gate=dwgu[:, :D_FF], w_up=dwgu[:, D_FF:],
                w_down=dwd)


def _layer_fwd(l, x, xb, W, P, tabs, blk):
    S = x.shape[0]
    nb = S // blk
    n = lambda s: f"l{l}_{s}"
    bs = min(512, S)
    h_att = _mm(xb, W["win_a"], name=n("in_att"), out_dtype=MXU_DTYPE, bk=1024)
    lat = _mm(xb, W["win_l"], name=n("in_lat"), bn=LAT_W, bk=1024)
    fg = lat[:, 512:516].T.reshape(4, S // 128, 128)
    cum = _fox_gate_fwd(fg, P["fox_b_f"], name=n("fox_gate"))
    cum_col, cum_row = cum.reshape(4, S, 1), cum.reshape(4 * nb, 1, blk)
    out_a, lse_a = _softmax_attn_fwd(h_att, h_att, h_att, qcb=COL_FQ // 128, kcb=COL_FK // 128, vcb=COL_FV // 128,
                                     dk=HEAD_DIM, scale=HEAD_DIM ** -0.5, cum_col=cum_col, cum_row=cum_row,
                                     blk=blk, name=n("fox_fwd"))
    q_m, k_m, v_m = _mla_prep_fwd(lat, P["mla_g_q"], P["mla_g_kv"], W["wuq"], W["wuk"], W["wuv"], *tabs,
                                  bs=bs, name=n("mla_prep"))
    out_b, lse_b = _softmax_attn_fwd(q_m, k_m, v_m, qcb=0, kcb=0, vcb=0, dk=MLA_PAD, scale=MLA_QK ** -0.5,
                                     blk=blk, name=n("mla_fwd"))
    out_c, lt_c = _sb_attn_fwd(h_att, blk=blk, name=n("sb_fwd"))
    out_d, lse_d = _swa_fwd(h_att, P["swa_sinks"], name=n("swa_fwd"))
    outs = (out_a, out_b, out_c, out_d)
    gn = _gnorm_fwd(outs, P["mix_g"], bs=bs, name=n("gnorm"))
    u1 = _mm(gn, W["w_o"], name=n("out_proj"), bk=1024, resid=x, alpha=ALPHA)
    x1, x1b, xh1, rs1 = _ln_fwd(u1, P["ln1_g"], P["ln1_b"], bs=bs, name=n("ln1"))
    gu = _mm(x1b, W["wgu"], name=n("gate_up"), bk=1024)
    a = _swiglu_fwd(gu, bs=min(256, S), name=n("swiglu"))
    u2 = _mm(a, W["w_down"], name=n("down"), bk=_blk(D_FF, 1408), resid=x1, alpha=ALPHA)
    x2, x2b, xh2, rs2 = _ln_fwd(u2, P["ln2_g"], P["ln2_b"], bs=bs, name=n("ln2"))
    saved = dict(xb=xb, h_att=h_att, lat=lat, fg=fg, cum_col=cum_col, cum_row=cum_row, outs=outs,
                 lse_a=lse_a, lse_b=lse_b, lse_d=lse_d, lt_c=lt_c, q_m=q_m, k_m=k_m, v_m=v_m, gn=gn, xh1=xh1, rs1=rs1,
                 x1b=x1b, gu=gu, a=a, xh2=xh2, rs2=rs2)
    return x2, x2b, saved


def _layer_bwd(l, dx2, sv, W, P, tabs, blk):
    S = dx2.shape[0]
    n = lambda s: f"l{l}_{s}"
    bs = min(512, S)
    h_att = sv["h_att"]
    du2, du2b, dg2, db2 = _ln_bwd(dx2, sv["xh2"], sv["rs2"], P["ln2_g"], bs=bs, name=n("ln2_bwd"))
    da = _mm(du2b, W["w_down"], name=n("down_dx"), tb=True, bn=_blk(D_FF, 512), bk=1024)
    dwd = _mm(sv["a"], du2b, name=n("down_dw"), ta=True, bm=_blk(D_FF, 512))
    dgu = _swiglu_bwd(da, sv["gu"], bs=min(256, S), name=n("swiglu_bwd"))
    dx1 = _mm(dgu, W["wgu"], name=n("gate_up_dx"), tb=True, bk=_blk(2 * D_FF, 1408), resid=du2, alpha=ALPHA)
    dwgu = _mm(sv["x1b"], dgu, name=n("gate_up_dw"), ta=True)
    du1, du1b, dg1, db1 = _ln_bwd(dx1, sv["xh1"], sv["rs1"], P["ln1_g"], bs=bs, name=n("ln1_bwd"))
    dgn = _mm(du1b, W["w_o"], name=n("out_proj_dx"), tb=True, bk=1024)
    dwo = _mm(sv["gn"], du1b, name=n("out_proj_dw"), ta=True)
    dmix, dmixg = _gnorm_bwd(dgn, sv["outs"], P["mix_g"], bs=bs, name=n("gnorm_bwd"))
    dqa, dka, dva, dck, dcq = _softmax_attn_bwd(
        h_att, h_att, h_att, dmix, sv["outs"][0], sv["lse_a"], qcb=COL_FQ // 128, kcb=COL_FK // 128, vcb=COL_FV // 128,
        dcb=0, dk=HEAD_DIM, scale=HEAD_DIM ** -0.5, cum_col=sv["cum_col"], cum_row=sv["cum_row"], blk=blk, name=n("fox_bwd"))
    dqb, dkb, dvb = _softmax_attn_bwd(
        sv["q_m"], sv["k_m"], sv["v_m"], dmix, sv["outs"][1], sv["lse_b"], qcb=0, kcb=0, vcb=0, dcb=2, dk=MLA_PAD,
        scale=MLA_QK ** -0.5, blk=blk, name=n("mla_bwd"))
    dqc, dkc, dvc = _sb_attn_bwd(h_att, dmix, sv["lt_c"], dcb=4, blk=blk, name=n("sb_bwd"))
    dqd, dkd, dvd, dsink = _swa_bwd(h_att, P["swa_sinks"], dmix, sv["outs"][3], sv["lse_d"], dcb=3, name=n("swa_bwd"))
    dlat, dwuq, dwuk, dwuv, dgq, dgkv = _mla_prep_bwd(
        sv["lat"], P["mla_g_q"], P["mla_g_kv"], W["wuq"], W["wuk"], W["wuv"], *tabs, dqb, dkb, dvb,
        bs=bs, name=n("mla_prep_bwd"))
    dfg, dbf = _fox_gate_bwd(sv["fg"], P["fox_b_f"], dck.reshape(4, S // 128, 128),
                             dcq.reshape(4, S // 128, 128), name=n("fox_gate_bwd"))
    dfg_blk = jnp.pad(dfg.reshape(4, S).T, ((0, 0), (0, 124)))
    dh = jnp.concatenate([t.astype(MXU_DTYPE) for t in (dqa, dka, dva, dqc, dkc, dvc, dqd, dkd, dvd, dlat, dfg_blk)], axis=1)
    dx = _mm(dh, W["win_p"], name=n("in_dx"), tb=True, bk=_blk(PERM_W, 1024), resid=du1, alpha=ALPHA)
    dwin_p = _mm(sv["xb"], dh, name=n("in_dw"), ta=True, bn=_blk(PERM_W, 1024))
    grads = _unprep_grads(dwin_p, dwuq, dwuk, dwuv, dwo, dwgu, dwd)
    grads.update(fox_b_f=dbf[:, 0], mla_g_q=dgq[0], mla_g_kv=dgkv[0], swa_sinks=dsink[:, 0], mix_g=dmixg[0],
                 ln1_g=dg1[0], ln1_b=db1[0], ln2_g=dg2[0], ln2_b=db2[0])
    return dx, grads


BIG = ("w_in", "mla_w_uq", "mla_w_ukv", "w_o", "w_gate", "w_up", "w_down")
SMALL = ("fox_b_f", "mla_g_q", "mla_g_kv", "swa_sinks", "mix_g", "ln1_g", "ln1_b", "ln2_g", "ln2_b")
SHARD_AXIS = dict(w_in=2, mla_w_uq=2, mla_w_ukv=2, w_o=1, w_gate=2, w_up=2, w_down=1)
N_CHIPS = 4
ANY = pl.BlockSpec(memory_space=pl.ANY)


def _chip_exchange(tensors, *, scatter, name):
    nt = len(tensors)

    def body(*refs):
        ins, outs = refs[:nt], refs[nt:2 * nt]
        send_sems, recv_sems, local_sems = refs[2 * nt:]
        x, y, c = lax.axis_index("x"), lax.axis_index("y"), lax.axis_index("c")
        me = 2 * x + y
        peers = [(1 - x, y), (x, 1 - y), (1 - x, 1 - y)]
        local, sends, recvs = [], [], []
        for t in range(nt):
            local.append(pltpu.make_async_copy(ins[t].at[me] if scatter else ins[t], outs[t].at[me], local_sems.at[t]))
            for r, (px, py) in enumerate(peers):
                k = 3 * t + r
                theirs = 2 * px + py
                sends.append(pltpu.make_async_remote_copy(
                    src_ref=ins[t].at[theirs] if scatter else ins[t], dst_ref=outs[t].at[me],
                    send_sem=send_sems.at[k], recv_sem=recv_sems.at[k], device_id=(px, py, c), device_id_type=MESH))
                recvs.append(pltpu.make_async_remote_copy(
                    src_ref=ins[t].at[me] if scatter else ins[t], dst_ref=outs[t].at[theirs],
                    send_sem=send_sems.at[k], recv_sem=recv_sems.at[k], device_id=(px, py, c), device_id_type=MESH))
        for cp in local + sends:
            cp.start()
        for cp in recvs:
            cp.wait_recv()
        for cp in sends:
            cp.wait_send()
        for cp in local:
            cp.wait()

    out_shape = [jax.ShapeDtypeStruct(t.shape if scatter else (N_CHIPS,) + t.shape, t.dtype) for t in tensors]
    return pl.pallas_call(
        body, name=name, in_specs=[ANY] * nt, out_specs=[ANY] * nt, out_shape=out_shape,
        scratch_shapes=[pltpu.SemaphoreType.DMA((3 * nt,)), pltpu.SemaphoreType.DMA((3 * nt,)),
                        pltpu.SemaphoreType.DMA((nt,))],
        compiler_params=pltpu.CompilerParams(has_side_effects=True),
    )(*tensors)


def _core_exchange(tensors, *, name):
    nt = len(tensors)

    def body(*refs):
        ins, outs = refs[:nt], refs[nt:2 * nt]
        send_sems, recv_sems = refs[2 * nt:]
        sibling = (lax.axis_index("x"), lax.axis_index("y"), 1 - lax.axis_index("c"))
        copies = [pltpu.make_async_remote_copy(src_ref=ins[t], dst_ref=outs[t], send_sem=send_sems.at[t],
                                               recv_sem=recv_sems.at[t], device_id=sibling, device_id_type=MESH)
                  for t in range(nt)]
        for cp in copies:
            cp.start()
        for cp in copies:
            cp.wait_recv()
        for cp in copies:
            cp.wait_send()

    return pl.pallas_call(
        body, name=name, in_specs=[ANY] * nt, out_specs=[ANY] * nt,
        out_shape=[jax.ShapeDtypeStruct(t.shape, t.dtype) for t in tensors],
        scratch_shapes=[pltpu.SemaphoreType.DMA((nt,)), pltpu.SemaphoreType.DMA((nt,))],
        compiler_params=pltpu.CompilerParams(has_side_effects=True),
    )(*tensors)


def _all_sum_small(block, *, name):
    R = block.shape[0]
    n_dev = 8

    def body(x_ref, o_ref, slots, send_sems, recv_sems):
        x, y, c = lax.axis_index("x"), lax.axis_index("y"), lax.axis_index("c")
        me = 4 * x + 2 * y + c
        slots[me] = x_ref[...]
        sends, recvs = [], []
        for d in range(1, n_dev):
            px, py, pc = x ^ (d >> 2), y ^ ((d >> 1) & 1), c ^ (d & 1)
            theirs = 4 * px + 2 * py + pc
            sends.append(pltpu.make_async_remote_copy(
                src_ref=x_ref, dst_ref=slots.at[me], send_sem=send_sems.at[d - 1], recv_sem=recv_sems.at[d - 1],
                device_id=(px, py, pc), device_id_type=MESH))
            recvs.append(pltpu.make_async_remote_copy(
                src_ref=x_ref, dst_ref=slots.at[theirs], send_sem=send_sems.at[d - 1], recv_sem=recv_sems.at[d - 1],
                device_id=(px, py, pc), device_id_type=MESH))
        for cp in sends:
            cp.start()
        for cp in recvs:
            cp.wait_recv()
        for cp in sends:
            cp.wait_send()
        total = slots[0]
        for k in range(1, n_dev):
            total = total + slots[k]
        o_ref[...] = total

    return pl.pallas_call(
        body, name=name, in_specs=[pl.BlockSpec(memory_space=pltpu.VMEM)],
        out_specs=pl.BlockSpec(memory_space=pltpu.VMEM), out_shape=jax.ShapeDtypeStruct((R, 128), F32),
        scratch_shapes=[pltpu.VMEM((n_dev, R, 128), F32), pltpu.SemaphoreType.DMA((n_dev - 1,)),
                        pltpu.SemaphoreType.DMA((n_dev - 1,))],
        compiler_params=pltpu.CompilerParams(has_side_effects=True),
    )(block)


def _sum_chips(recv, *, br, name):
    _, R, C = recv.shape

    def body(r_ref, o_ref):
        total = r_ref[0].astype(F32)
        for k in range(1, N_CHIPS):
            total = total + r_ref[k].astype(F32)
        o_ref[...] = total

    return pl.pallas_call(
        body, name=name, grid=(R // br,), in_specs=[pl.BlockSpec((N_CHIPS, br, C), lambda i: (0, i, 0))],
        out_specs=pl.BlockSpec((br, C), lambda i: (i, 0)), out_shape=jax.ShapeDtypeStruct((R, C), F32),
        compiler_params=_cparams("parallel"),
    )(recv)


def _adamw_math(w, g, m, v):
    m = ADAM_B1 * m + (1.0 - ADAM_B1) * g
    v = ADAM_B2 * v + (1.0 - ADAM_B2) * (g * g)
    m_hat = m / (1.0 - ADAM_B1 ** ADAM_STEP)
    v_hat = v / (1.0 - ADAM_B2 ** ADAM_STEP)
    return -ADAM_LR * (m_hat / (jnp.sqrt(v_hat) + ADAM_EPS) + ADAM_WD * w), m, v


def _adamw(w, m, v, g_a, g_b, *, br, name):
    R, C = w.shape
    two = g_b is not None

    def body(*refs):
        if two:
            w_ref, m_ref, v_ref, ga_ref, gb_ref, g_ref, d_ref, nm_ref, nv_ref = refs
            g = ga_ref[...] + gb_ref[...]
        else:
            w_ref, m_ref, v_ref, ga_ref, g_ref, d_ref, nm_ref, nv_ref = refs
            g = ga_ref[...]
        g_ref[...] = g
        d_ref[...], nm_ref[...], nv_ref[...] = _adamw_math(w_ref[...], g, m_ref[...], v_ref[...])

    spec = pl.BlockSpec((br, C), lambda i: (i, 0))
    args = [w, m, v, g_a] + ([g_b] if two else [])
    return pl.pallas_call(
        body, name=name, grid=(R // br,), in_specs=[spec] * len(args), out_specs=[spec] * 4,
        out_shape=[jax.ShapeDtypeStruct((R, C), F32)] * 4,
        compiler_params=_cparams("parallel"),
    )(*args)


SMALL_ROWS = dict(fox_b_f=1, mla_g_q=2, mla_g_kv=1, swa_sinks=1, mix_g=8, ln1_g=8, ln1_b=8, ln2_g=8, ln2_b=8)
SMALL_ROWS_PER_LAYER = sum(SMALL_ROWS.values())


def _pack_small(vals, extra_rows):
    L = vals[SMALL[0]].shape[0]
    per_layer = []
    for name in SMALL:
        a = vals[name].astype(F32)
        a = jnp.pad(a, ((0, 0), (0, SMALL_ROWS[name] * 128 - a.shape[1])))
        per_layer.append(a.reshape(L, SMALL_ROWS[name], 128))
    out = jnp.concatenate(per_layer, axis=1).reshape(L * SMALL_ROWS_PER_LAYER, 128)
    return jnp.pad(out, ((0, extra_rows), (0, 0)))


def _unpack_small(block, shapes):
    L = shapes[SMALL[0]][0]
    body = block[:L * SMALL_ROWS_PER_LAYER].reshape(L, SMALL_ROWS_PER_LAYER, 128)
    out, r = {}, 0
    for name in SMALL:
        n = shapes[name][1]
        out[name] = body[:, r:r + SMALL_ROWS[name]].reshape(L, SMALL_ROWS[name] * 128)[:, :n]
        r += SMALL_ROWS[name]
    return out


def _to_chips(g, axis):
    L, a, b = g.shape
    if axis == 2:
        return g.reshape(L, a, N_CHIPS, b // N_CHIPS).transpose(2, 0, 1, 3)
    return g.reshape(L, N_CHIPS, a // N_CHIPS, b).transpose(1, 0, 2, 3)


def _from_chips(g, axis):
    _, L, a, b = g.shape
    if axis == 2:
        return g.transpose(1, 2, 0, 3).reshape(L, a, N_CHIPS * b)
    return g.transpose(1, 0, 2, 3).reshape(L, N_CHIPS * a, b)


def kernel(x, w_in, fox_b_f, mla_g_q, mla_g_kv, mla_w_uq, mla_w_ukv, swa_sinks, mix_g, w_o, ln1_g, ln1_b, w_gate, w_up, w_down, ln2_g, ln2_b, loss_target, m_w_in, m_fox_b_f, m_mla_g_q, m_mla_g_kv, m_mla_w_uq, m_mla_w_ukv, m_swa_sinks, m_mix_g, m_w_o, m_ln1_g, m_ln1_b, m_w_gate, m_w_up, m_w_down, m_ln2_g, m_ln2_b, v_w_in, v_fox_b_f, v_mla_g_q, v_mla_g_kv, v_mla_w_uq, v_mla_w_ukv, v_swa_sinks, v_mix_g, v_w_o, v_ln1_g, v_ln1_b, v_w_gate, v_w_up, v_w_down, v_ln2_g, v_ln2_b):
    w = dict(w_in=w_in, fox_b_f=fox_b_f, mla_g_q=mla_g_q, mla_g_kv=mla_g_kv, mla_w_uq=mla_w_uq, mla_w_ukv=mla_w_ukv,
             swa_sinks=swa_sinks, mix_g=mix_g, w_o=w_o, ln1_g=ln1_g, ln1_b=ln1_b, w_gate=w_gate, w_up=w_up,
             w_down=w_down, ln2_g=ln2_g, ln2_b=ln2_b)
    m = dict(w_in=m_w_in, fox_b_f=m_fox_b_f, mla_g_q=m_mla_g_q, mla_g_kv=m_mla_g_kv, mla_w_uq=m_mla_w_uq,
             mla_w_ukv=m_mla_w_ukv, swa_sinks=m_swa_sinks, mix_g=m_mix_g, w_o=m_w_o, ln1_g=m_ln1_g, ln1_b=m_ln1_b,
             w_gate=m_w_gate, w_up=m_w_up, w_down=m_w_down, ln2_g=m_ln2_g, ln2_b=m_ln2_b)
    v = dict(w_in=v_w_in, fox_b_f=v_fox_b_f, mla_g_q=v_mla_g_q, mla_g_kv=v_mla_g_kv, mla_w_uq=v_mla_w_uq,
             mla_w_ukv=v_mla_w_ukv, swa_sinks=v_swa_sinks, mix_g=v_mix_g, w_o=v_w_o, ln1_g=v_ln1_g, ln1_b=v_ln1_b,
             w_gate=v_w_gate, w_up=v_w_up, w_down=v_w_down, ln2_g=v_ln2_g, ln2_b=v_ln2_b)
    names = tuple(w)
    L = w_in.shape[0]
    S = x.shape[1]
    blk = min(256, S)
    bs = min(512, S)

    gathered = _chip_exchange([w[k].astype(MXU_DTYPE) for k in BIG], scatter=False, name="gather_weights")
    full = {k: _from_chips(g, SHARD_AXIS[k]) for k, g in zip(BIG, gathered)}
    tabs = _rope_tables(S)
    Ws = [_prep_weights(*[full[k][l] for k in BIG]) for l in range(L)]
    Ps = []
    for l in range(L):
        P = dict(fox_b_f=fox_b_f[l], swa_sinks=swa_sinks[l])
        for k in ("mla_g_q", "mla_g_kv", "mix_g", "ln1_g", "ln1_b", "ln2_g", "ln2_b"):
            P[k] = w[k][l][None, :]
        Ps.append(P)

    xa = x[0]
    xb = xa.astype(MXU_DTYPE)
    saved = []
    for l in range(L):
        xa, xb, sv = _layer_fwd(l, xa, xb, Ws[l], Ps[l], tabs, blk)
        saved.append(sv)
    dx, loss_part = _loss_head(xa, loss_target[0], bs=bs, name="loss_head")
    layer_grads = [None] * L
    for l in reversed(range(L)):
        dx, layer_grads[l] = _layer_bwd(l, dx, saved[l], Ws[l], Ps[l], tabs, blk)
    grad_x = dx[None]

    local = {k: jnp.stack([layer_grads[l][k] for l in range(L)]) for k in names}
    to_owner = [_to_chips(local[k], SHARD_AXIS[k]).astype(MXU_DTYPE) for k in BIG]
    received = _chip_exchange(to_owner, scatter=True, name="scatter_grads")
    partial = []
    for k, r in zip(BIG, received):
        _, _, a, b = r.shape
        partial.append(_sum_chips(r.reshape(N_CHIPS, L * a, b), br=_rows(L * a), name=f"sum_{k}"))
    sibling = _core_exchange(partial, name="swap_partials")
    out = {}
    for k, mine, theirs in zip(BIG, partial, sibling):
        shp = w[k].shape
        two_d = lambda t: t.reshape(shp[0] * shp[1], shp[2])
        res = _adamw(two_d(w[k]), two_d(m[k]), two_d(v[k]), mine, theirs, br=_rows(shp[0] * shp[1]), name=f"adamw_{k}")
        out[k] = [t.reshape(shp) for t in res]

    shapes = {k: w[k].shape for k in SMALL}
    extra = 8 + (-L * SMALL_ROWS_PER_LAYER) % 8
    block = _pack_small({k: local[k] for k in SMALL}, extra)
    block = block.at[L * SMALL_ROWS_PER_LAYER, 0].set(loss_part[0, 0])
    total = _all_sum_small(block, name="sum_small")
    loss = total[L * SMALL_ROWS_PER_LAYER, 0]
    res = _adamw(_pack_small({k: w[k] for k in SMALL}, extra), _pack_small({k: m[k] for k in SMALL}, extra),
                 _pack_small({k: v[k] for k in SMALL}, extra), total, None, br=total.shape[0], name="adamw_small")
    res = [_unpack_small(t, shapes) for t in res]
    for k in SMALL:
        out[k] = [r[k] for r in res]

    return (loss, grad_x, *[out[k][0] for k in names], *[out[k][1] for k in names],
            *[out[k][2] for k in names], *[out[k][3] for k in names])


def _rows(n):
    for b in (256, 128, 64, 32, 16, 8):
        if n % b == 0:
            return b
    return n
```

```python
import functools

import numpy as np
import jax
import jax.numpy as jnp
from jax import lax
from jax.experimental import pallas as pl
from jax.experimental.pallas import tpu as pltpu

F32 = jnp.float32
MXU_DTYPE = jnp.bfloat16
NEG_INF = -1e30

D_MODEL = 1024
DEPTH = 4
HEAD_DIM = 64
GROUP_WIDTH = 256
N_GROUPS = 4
D_FF = 2816
MLA_Q_RANK = 256
MLA_KV_RANK = 128
MLA_ROPE = 32
MLA_QK = 96
MLA_PAD = 128
ROPE_THETA = 10000.0
WINDOW = 128
ALPHA = (2.0 * DEPTH) ** 0.25
SWA_SLOPES = tuple(float(2.0 ** (-8.0 * h / 4)) for h in range(1, 5))
IN_WIDTH = 2468
ATT_W = 2048
LAT_W = 640
PERM_W = ATT_W + LAT_W
COL_FQ, COL_FK, COL_FV = 0, 256, 512
COL_SQ, COL_SK, COL_SV = 768, 1024, 1280
COL_WQ, COL_WK, COL_WV = 1536, 1792, 1920
Q_COLSCALE = np.ones((1, ATT_W), np.float32)
Q_COLSCALE[:, COL_FQ:COL_FQ + 256] = HEAD_DIM ** -0.5
Q_COLSCALE[:, COL_SQ:COL_SQ + 256] = HEAD_DIM ** -0.5

ADAM_LR, ADAM_B1, ADAM_B2, ADAM_EPS, ADAM_WD, ADAM_STEP = 0.001, 0.9, 0.999, 1e-08, 0.01, 10

VMEM_LIMIT = 56 * 1024 * 1024
NT = (((1,), (1,)), ((), ()))
TN = (((0,), (0,)), ((), ()))
MESH = pl.DeviceIdType.MESH


def _cparams(*sem):
    return pltpu.CompilerParams(dimension_semantics=sem, vmem_limit_bytes=VMEM_LIMIT)


def _split2(x):
    hi = x.astype(MXU_DTYPE)
    lo = (x - hi.astype(F32)).astype(MXU_DTYPE)
    return hi, lo


def _dot01(x, m01, dn=None, parts=2):
    acc = None
    rem = x
    for _ in range(parts):
        part = rem.astype(MXU_DTYPE)
        rem = rem - part.astype(F32)
        if dn is None:
            t = jnp.dot(part, m01, preferred_element_type=F32)
        else:
            t = lax.dot_general(part, m01, dn, preferred_element_type=F32)
        acc = t if acc is None else acc + t
    return acc


def _mm(a, b, *, name, ta=False, tb=False, out_dtype=F32, bm=512, bn=512, bk=512, resid=None, alpha=1.0,
        colscale=None):
    M, K = (a.shape[1], a.shape[0]) if ta else a.shape
    N = b.shape[0] if tb else b.shape[1]
    assert (b.shape[1] if tb else b.shape[0]) == K
    assert resid is None or colscale is None
    bm, bn, bk = min(bm, M), min(bn, N), min(bk, K)
    assert M % bm == 0 and N % bn == 0 and K % bk == 0, (name, M, N, K, bm, bn, bk)
    nk = K // bk
    dn = (((0 if ta else 1,), (1 if tb else 0,)), ((), ()))

    def body(*refs):
        if resid is None and colscale is None:
            a_ref, b_ref, o_ref, acc_ref = refs
        else:
            a_ref, b_ref, r_ref, o_ref, acc_ref = refs
        k = pl.program_id(2)

        @pl.when(k == 0)
        def _():
            acc_ref[...] = jnp.zeros_like(acc_ref)

        acc_ref[...] += lax.dot_general(a_ref[...].astype(MXU_DTYPE), b_ref[...].astype(MXU_DTYPE), dn,
                                        preferred_element_type=F32)

        @pl.when(k == nk - 1)
        def _():
            r = acc_ref[...]
            if resid is not None:
                r = r + alpha * r_ref[...]
            if colscale is not None:
                r = r * r_ref[...]
            o_ref[...] = r.astype(o_ref.dtype)

    a_spec = pl.BlockSpec((bk, bm), lambda i, j, k: (k, i)) if ta else pl.BlockSpec((bm, bk), lambda i, j, k: (i, k))
    b_spec = pl.BlockSpec((bn, bk), lambda i, j, k: (j, k)) if tb else pl.BlockSpec((bk, bn), lambda i, j, k: (k, j))
    in_specs = [a_spec, b_spec]
    args = [a, b]
    if resid is not None:
        in_specs.append(pl.BlockSpec((bm, bn), lambda i, j, k: (i, j)))
        args.append(resid)
    if colscale is not None:
        in_specs.append(pl.BlockSpec((1, bn), lambda i, j, k: (0, j)))
        args.append(colscale)
    return pl.pallas_call(
        body, name=name, grid=(M // bm, N // bn, nk), in_specs=in_specs,
        out_specs=pl.BlockSpec((bm, bn), lambda i, j, k: (i, j)),
        out_shape=jax.ShapeDtypeStruct((M, N), out_dtype),
        scratch_shapes=[pltpu.VMEM((bm, bn), F32)],
        compiler_params=_cparams("parallel", "parallel", "arbitrary"),
    )(*args)


def _softmax_attn_fwd(q_arr, k_arr, v_arr, *, qcb, kcb, vcb, dk, scale, cum_col=None, cum_row=None, blk, name):
    S = q_arr.shape[0]
    nb = S // blk
    bias = cum_col is not None
    W = 2 * dk

    def body(*refs):
        if bias:
            q_ref, k_ref, v_ref, cc_ref, cr_ref, o_ref, lse_ref = refs
        else:
            q_ref, k_ref, v_ref, o_ref, lse_ref = refs
        p = pl.program_id(0)
        i = pl.program_id(1)
        row = lax.broadcasted_iota(jnp.int32, (blk, blk), 0)
        col = lax.broadcasted_iota(jnp.int32, (blk, blk), 1)
        for hh in range(2):
            q = q_ref[:, hh * dk:(hh + 1) * dk]

            def tile(j, carry, masked, hh=hh, q=q):
                m, l, acc = carry
                r0 = pl.multiple_of(j * blk, blk)
                ks = k_ref[pl.ds(r0, blk), hh * dk:(hh + 1) * dk]
                vs = v_ref[pl.ds(r0, blk), hh * HEAD_DIM:(hh + 1) * HEAD_DIM]
                s = lax.dot_general(q, ks, NT, preferred_element_type=F32) * scale
                if bias:
                    s = s + cc_ref[hh] - cr_ref[(2 * p + hh) * nb + j]
                if masked:
                    s = jnp.where(col <= row, s, NEG_INF)
                mn = jnp.maximum(m, jnp.max(s, axis=1, keepdims=True))
                a = jnp.exp(m - mn)
                pe = jnp.exp(s - mn)
                l = a * l + jnp.sum(pe, axis=1, keepdims=True)
                acc = a * acc + jnp.dot(pe.astype(MXU_DTYPE), vs, preferred_element_type=F32)
                return mn, l, acc

            init = (jnp.full((blk, 1), NEG_INF, F32), jnp.zeros((blk, 1), F32), jnp.zeros((blk, HEAD_DIM), F32))
            carry = lax.fori_loop(0, i, functools.partial(tile, masked=False), init)
            m, l, acc = tile(i, carry, True)
            o_ref[:, hh * HEAD_DIM:(hh + 1) * HEAD_DIM] = acc / l
            lse_ref[hh] = m + jnp.log(l)

    in_specs = [pl.BlockSpec((blk, W), lambda p, i: (i, qcb + p)),
                pl.BlockSpec((S, W), lambda p, i: (0, kcb + p)),
                pl.BlockSpec((S, 128), lambda p, i: (0, vcb + p))]
    args = [q_arr, k_arr, v_arr]
    if bias:
        in_specs += [pl.BlockSpec((2, blk, 1), lambda p, i: (p, i, 0)),
                     pl.BlockSpec((4 * nb, 1, blk), lambda p, i: (0, 0, 0))]
        args += [cum_col, cum_row]
    return pl.pallas_call(
        body, name=name, grid=(2, nb), in_specs=in_specs,
        out_specs=[pl.BlockSpec((blk, 128), lambda p, i: (i, p)), pl.BlockSpec((2, blk, 1), lambda p, i: (p, i, 0))],
        out_shape=[jax.ShapeDtypeStruct((S, GROUP_WIDTH), F32), jax.ShapeDtypeStruct((4, S, 1), F32)],
        compiler_params=_cparams("arbitrary", "arbitrary"),
    )(*args)


def _softmax_attn_bwd(q_arr, k_arr, v_arr, dmix, o_arr, lse, *, qcb, kcb, vcb, dcb, dk, scale,
                      cum_col=None, cum_row=None, blk, name):
    S = q_arr.shape[0]
    nb = S // blk
    bias = cum_col is not None
    W = 2 * dk

    def body(*refs):
        if bias:
            q_ref, k_ref, v_ref, do_ref, o_ref, lse_ref, cc_ref, cr_ref, dq_ref, dk_ref, dv_ref, dc_ref, dcq_ref = refs
        else:
            q_ref, k_ref, v_ref, do_ref, o_ref, lse_ref, dq_ref, dk_ref, dv_ref = refs
        p = pl.program_id(0)
        i = pl.program_id(1)

        @pl.when(i == 0)
        def _():
            dk_ref[...] = jnp.zeros_like(dk_ref)
            dv_ref[...] = jnp.zeros_like(dv_ref)
            if bias:
                dc_ref[...] = jnp.zeros_like(dc_ref)

        row = lax.broadcasted_iota(jnp.int32, (blk, blk), 0)
        col = lax.broadcasted_iota(jnp.int32, (blk, blk), 1)
        for hh in range(2):
            q = q_ref[:, hh * dk:(hh + 1) * dk]
            do = do_ref[:, hh * HEAD_DIM:(hh + 1) * HEAD_DIM]
            delta = jnp.sum(do * o_ref[:, hh * HEAD_DIM:(hh + 1) * HEAD_DIM], axis=1, keepdims=True)
            dob = do.astype(MXU_DTYPE)
            lse_h = lse_ref[hh]

            def tile(j, carry, masked, hh=hh, q=q, dob=dob, delta=delta, lse_h=lse_h):
                dq, dcq = carry
                r0 = pl.multiple_of(j * blk, blk)
                ks = k_ref[pl.ds(r0, blk), hh * dk:(hh + 1) * dk]
                vs = v_ref[pl.ds(r0, blk), hh * HEAD_DIM:(hh + 1) * HEAD_DIM]
                s = lax.dot_general(q, ks, NT, preferred_element_type=F32) * scale
                if bias:
                    s = s + cc_ref[hh] - cr_ref[(2 * p + hh) * nb + j]
                if masked:
                    s = jnp.where(col <= row, s, NEG_INF)
                pr = jnp.exp(s - lse_h)
                dp = lax.dot_general(dob, vs, NT, preferred_element_type=F32)
                ds = pr * (dp - delta)
                dsb = ds.astype(MXU_DTYPE)
                dv_ref[pl.ds(r0, blk), hh * HEAD_DIM:(hh + 1) * HEAD_DIM] += lax.dot_general(
                    pr.astype(MXU_DTYPE), dob, TN, preferred_element_type=F32)
                dk_ref[pl.ds(r0, blk), hh * dk:(hh + 1) * dk] += lax.dot_general(
                    dsb, q, TN, preferred_element_type=F32) * scale
                if bias:
                    dc_ref[hh * nb + j] -= jnp.sum(ds, axis=0, keepdims=True)
                    dcq = dcq + jnp.sum(ds, axis=1, keepdims=True)
                return dq + jnp.dot(dsb, ks, preferred_element_type=F32) * scale, dcq

            carry = lax.fori_loop(0, i, functools.partial(tile, masked=False),
                                  (jnp.zeros((blk, dk), F32), jnp.zeros((blk, 1), F32)))
            dq, dcq = tile(i, carry, True)
            dq_ref[:, hh * dk:(hh + 1) * dk] = dq
            if bias:
                dcq_ref[hh] = dcq

    in_specs = [pl.BlockSpec((blk, W), lambda p, i: (i, qcb + p)),
                pl.BlockSpec((S, W), lambda p, i: (0, kcb + p)),
                pl.BlockSpec((S, 128), lambda p, i: (0, vcb + p)),
                pl.BlockSpec((blk, 128), lambda p, i: (i, dcb + p)),
                pl.BlockSpec((blk, 128), lambda p, i: (i, p)),
                pl.BlockSpec((2, blk, 1), lambda p, i: (p, i, 0))]
    args = [q_arr, k_arr, v_arr, dmix, o_arr, lse]
    out_specs = [pl.BlockSpec((blk, W), lambda p, i: (i, p)),
                 pl.BlockSpec((S, W), lambda p, i: (0, p)),
                 pl.BlockSpec((S, 128), lambda p, i: (0, p))]
    out_shape = [jax.ShapeDtypeStruct((S, 4 * dk), F32), jax.ShapeDtypeStruct((S, 4 * dk), F32),
                 jax.ShapeDtypeStruct((S, GROUP_WIDTH), F32)]
    if bias:
        in_specs += [pl.BlockSpec((2, blk, 1), lambda p, i: (p, i, 0)),
                     pl.BlockSpec((4 * nb, 1, blk), lambda p, i: (0, 0, 0))]
        args += [cum_col, cum_row]
        out_specs += [pl.BlockSpec((2 * nb, 1, blk), lambda p, i: (p, 0, 0)), pl.BlockSpec((2, blk, 1), lambda p, i: (p, i, 0))]
        out_shape += [jax.ShapeDtypeStruct((4 * nb, 1, blk), F32), jax.ShapeDtypeStruct((4, S, 1), F32)]
    return pl.pallas_call(
        body, name=name, grid=(2, nb), in_specs=in_specs, out_specs=out_specs, out_shape=out_shape,
        compiler_params=_cparams("arbitrary", "arbitrary"),
    )(*args)


def _sb_tile(q, ks, scale, strict_mask, carry_l, tri_excl):
    z = lax.dot_general(q, ks, NT, preferred_element_type=F32) * scale
    lb = -(jnp.maximum(z, 0.0) + jnp.log(1.0 + jnp.exp(-jnp.abs(z))))
    if strict_mask is not None:
        lb = jnp.where(strict_mask, lb, 0.0)
    between = _dot01(lb, tri_excl) + carry_l
    a = jnp.exp(z + lb + between)
    if strict_mask is not None:
        a = jnp.where(strict_mask, a, 0.0)
    return z, lb, a


def _sb_attn_fwd(h_att, *, blk, name):
    S = h_att.shape[0]
    nb = S // blk
    scale = HEAD_DIM ** -0.5
    qcb, kcb, vcb = COL_SQ // 128, COL_SK // 128, COL_SV // 128

    def body(q_ref, k_ref, v_ref, o_ref, lt_ref):
        i = pl.program_id(1)
        row = lax.broadcasted_iota(jnp.int32, (blk, blk), 0)
        col = lax.broadcasted_iota(jnp.int32, (blk, blk), 1)
        strict = col < row
        tri_excl = (row > col).astype(MXU_DTYPE)
        for hh in range(2):
            sl = slice(hh * HEAD_DIM, (hh + 1) * HEAD_DIM)
            q = q_ref[:, sl]

            def tile(j, carry, mask, sl=sl, q=q):
                cl, acc = carry
                r0 = pl.multiple_of(j * blk, blk)
                _, lb, a = _sb_tile(q, k_ref[pl.ds(r0, blk), sl], scale, mask, cl, tri_excl)
                acc = acc + jnp.dot(a.astype(MXU_DTYPE), v_ref[pl.ds(r0, blk), sl], preferred_element_type=F32)
                return cl + jnp.sum(lb, axis=1, keepdims=True), acc

            carry = tile(i, (jnp.zeros((blk, 1), F32), jnp.zeros((blk, HEAD_DIM), F32)), strict)
            cl, acc = lax.fori_loop(0, i, lambda jj, c: tile(i - 1 - jj, c, None), carry)
            o_ref[:, sl] = acc
            lt_ref[hh] = cl

    return pl.pallas_call(
        body, name=name, grid=(2, nb),
        in_specs=[pl.BlockSpec((blk, 128), lambda p, i: (i, qcb + p)),
                  pl.BlockSpec((S, 128), lambda p, i: (0, kcb + p)),
                  pl.BlockSpec((S, 128), lambda p, i: (0, vcb + p))],
        out_specs=[pl.BlockSpec((blk, 128), lambda p, i: (i, p)), pl.BlockSpec((2, blk, 1), lambda p, i: (p, i, 0))],
        out_shape=[jax.ShapeDtypeStruct((S, GROUP_WIDTH), F32), jax.ShapeDtypeStruct((4, S, 1), F32)],
        compiler_params=_cparams("arbitrary", "arbitrary"),
    )(h_att, h_att, h_att)


def _sb_attn_bwd(h_att, dmix, ltot_arr, *, dcb, blk, name):
    S = h_att.shape[0]
    nb = S // blk
    scale = HEAD_DIM ** -0.5
    qcb, kcb, vcb = COL_SQ // 128, COL_SK // 128, COL_SV // 128

    def body(q_ref, k_ref, v_ref, do_ref, lt_ref, dq_ref, dk_ref, dv_ref):
        i = pl.program_id(1)

        @pl.when(i == 0)
        def _():
            dk_ref[...] = jnp.zeros_like(dk_ref)
            dv_ref[...] = jnp.zeros_like(dv_ref)

        row = lax.broadcasted_iota(jnp.int32, (blk, blk), 0)
        col = lax.broadcasted_iota(jnp.int32, (blk, blk), 1)
        strict = col < row
        up_incl = (row <= col).astype(MXU_DTYPE)
        up_excl = (row < col).astype(MXU_DTYPE)
        for hh in range(2):
            sl = slice(hh * HEAD_DIM, (hh + 1) * HEAD_DIM)
            q = q_ref[:, sl]
            dob = do_ref[:, sl].astype(MXU_DTYPE)
            ltot = lt_ref[hh]

            def tile(j, carry, mask, sl=sl, q=q, dob=dob, ltot=ltot):
                cl, cg, dq = carry
                r0 = pl.multiple_of(j * blk, blk)
                ks = k_ref[pl.ds(r0, blk), sl]
                vs = v_ref[pl.ds(r0, blk), sl]
                z = lax.dot_general(q, ks, NT, preferred_element_type=F32) * scale
                lb = -(jnp.maximum(z, 0.0) + jnp.log(1.0 + jnp.exp(-jnp.abs(z))))
                if mask is not None:
                    lb = jnp.where(mask, lb, 0.0)
                between = ltot - cl - _dot01(lb, up_incl)
                a = jnp.exp(z + lb + between)
                if mask is not None:
                    a = jnp.where(mask, a, 0.0)
                g = lax.dot_general(dob, vs, NT, preferred_element_type=F32) * a
                e = cg + _dot01(g, up_excl)
                dz = g * jnp.exp(lb) - e * jnp.exp(z + lb)
                if mask is not None:
                    dz = jnp.where(mask, dz, 0.0)
                dzb = dz.astype(MXU_DTYPE)
                dv_ref[pl.ds(r0, blk), sl] += lax.dot_general(a.astype(MXU_DTYPE), dob, TN, preferred_element_type=F32)
                dk_ref[pl.ds(r0, blk), sl] += lax.dot_general(dzb, q, TN, preferred_element_type=F32) * scale
                dq = dq + jnp.dot(dzb, ks, preferred_element_type=F32) * scale
                return cl + jnp.sum(lb, axis=1, keepdims=True), cg + jnp.sum(g, axis=1, keepdims=True), dq

            zc = jnp.zeros((blk, 1), F32)
            carry = lax.fori_loop(0, i, lambda j, c: tile(j, c, None), (zc, zc, jnp.zeros((blk, HEAD_DIM), F32)))
            _, _, dq = tile(i, carry, strict)
            dq_ref[:, sl] = dq

    return pl.pallas_call(
        body, name=name, grid=(2, nb),
        in_specs=[pl.BlockSpec((blk, 128), lambda p, i: (i, qcb + p)),
                  pl.BlockSpec((S, 128), lambda p, i: (0, kcb + p)),
                  pl.BlockSpec((S, 128), lambda p, i: (0, vcb + p)),
                  pl.BlockSpec((blk, 128), lambda p, i: (i, dcb + p)),
                  pl.BlockSpec((2, blk, 1), lambda p, i: (p, i, 0))],
        out_specs=[pl.BlockSpec((blk, 128), lambda p, i: (i, p)),
                   pl.BlockSpec((S, 128), lambda p, i: (0, p)),
                   pl.BlockSpec((S, 128), lambda p, i: (0, p))],
        out_shape=[jax.ShapeDtypeStruct((S, GROUP_WIDTH), F32)] * 3,
        compiler_params=_cparams("arbitrary", "arbitrary"),
    )(h_att, h_att, h_att, dmix, ltot_arr)


HP = 4


def _kv_blocks_t(a, blk):
    S, C = a.shape
    return a.reshape(S // blk, blk, C).transpose(0, 2, 1)


def _smax_fwd_t(qT, k, vT3, *, dk, blk, name):
    S = k.shape[0]
    nb = S // blk
    H = k.shape[1] // dk

    def body(qT_ref, k_ref, vT_ref, oT_ref, lse_ref):
        i = pl.program_id(1)
        key = lax.broadcasted_iota(jnp.int32, (blk, blk), 0)
        qry = lax.broadcasted_iota(jnp.int32, (blk, blk), 1)
        qs = [qT_ref[h * dk:(h + 1) * dk, :] for h in range(HP)]

        def tile(j, carry, masked):
            r0 = pl.multiple_of(j * blk, blk)
            ss = [jnp.dot(k_ref[pl.ds(r0, blk), h * dk:(h + 1) * dk], qs[h], preferred_element_type=F32)
                  for h in range(HP)]
            stats, pes = [], []
            for h in range(HP):
                m, l, _ = carry[h]
                s = jnp.where(key <= qry, ss[h], NEG_INF) if masked else ss[h]
                mn = jnp.maximum(m, jnp.max(s, axis=0, keepdims=True))
                a = jnp.exp(m - mn)
                pe = jnp.exp(s - mn)
                stats.append((mn, a * l + jnp.sum(pe, axis=0, keepdims=True), a))
                pes.append(pe.astype(MXU_DTYPE))
            pvs = [jnp.dot(vT_ref[j, h * HEAD_DIM:(h + 1) * HEAD_DIM, :], pes[h], preferred_element_type=F32)
                   for h in range(HP)]
            return tuple((stats[h][0], stats[h][1], stats[h][2] * carry[h][2] + pvs[h]) for h in range(HP))

        init = tuple((jnp.full((1, blk), NEG_INF, F32), jnp.zeros((1, blk), F32), jnp.zeros((HEAD_DIM, blk), F32))
                     for _ in range(HP))
        carry = lax.fori_loop(0, i, functools.partial(tile, masked=False), init)
        carry = tile(i, carry, True)
        for h in range(HP):
            m, l, acc = carry[h]
            oT_ref[h * HEAD_DIM:(h + 1) * HEAD_DIM, :] = acc / l
            lse_ref[h, 0] = m + jnp.log(l)

    return pl.pallas_call(
        body, name=name, grid=(H // HP, nb),
        in_specs=[pl.BlockSpec((HP * dk, blk), lambda p, i: (p, i)),
                  pl.BlockSpec((S, HP * dk), lambda p, i: (0, p)),
                  pl.BlockSpec((nb, HP * HEAD_DIM, blk), lambda p, i: (0, p, 0))],
        out_specs=[pl.BlockSpec((HP * HEAD_DIM, blk), lambda p, i: (p, i)),
                   pl.BlockSpec((HP, 1, 1, blk), lambda p, i: (p, i, 0, 0))],
        out_shape=[jax.ShapeDtypeStruct((H * HEAD_DIM, S), F32), jax.ShapeDtypeStruct((H, nb, 1, blk), F32)],
        compiler_params=_cparams("arbitrary", "arbitrary"),
    )(qT, k, vT3)


def _smax_bwd_t(qT, q, k, kT3, v, dmix, dmixT, oT, lse, *, dk, dcb, qscale, blk, name):
    S = k.shape[0]
    nb = S // blk
    H = k.shape[1] // dk
    hd = HP * HEAD_DIM
    dcr = dcb * 128 // hd

    def body(qT_ref, q_ref, k_ref, kT_ref, v_ref, do_ref, doT_ref, oT_ref, lse_ref, dqT_ref, dk_ref, dv_ref):
        i = pl.program_id(1)

        @pl.when(i == 0)
        def _():
            dk_ref[...] = jnp.zeros_like(dk_ref)
            dv_ref[...] = jnp.zeros_like(dv_ref)

        key = lax.broadcasted_iota(jnp.int32, (blk, blk), 0)
        qry = lax.broadcasted_iota(jnp.int32, (blk, blk), 1)
        per_head = []
        for h in range(HP):
            hs = slice(h * HEAD_DIM, (h + 1) * HEAD_DIM)
            doT = doT_ref[hs, :]
            per_head.append(dict(
                qT=qT_ref[h * dk:(h + 1) * dk, :], q=q_ref[:, h * dk:(h + 1) * dk],
                doT=doT.astype(MXU_DTYPE), do=do_ref[:, hs].astype(MXU_DTYPE),
                delta=jnp.sum(doT * oT_ref[hs, :], axis=0, keepdims=True), lse=lse_ref[h, 0]))

        def tile(j, dqs, masked):
            r0 = pl.multiple_of(j * blk, blk)
            rows = pl.ds(r0, blk)
            ksl = [slice(h * dk, (h + 1) * dk) for h in range(HP)]
            hsl = [slice(h * HEAD_DIM, (h + 1) * HEAD_DIM) for h in range(HP)]
            ss = [jnp.dot(k_ref[rows, ksl[h]], per_head[h]["qT"], preferred_element_type=F32) for h in range(HP)]
            dps = [jnp.dot(v_ref[rows, hsl[h]], per_head[h]["doT"], preferred_element_type=F32) for h in range(HP)]
            prs, dss = [], []
            for h in range(HP):
                c = per_head[h]
                s = jnp.where(key <= qry, ss[h], NEG_INF) if masked else ss[h]
                pr = jnp.exp(s - c["lse"])
                dss.append((pr * (dps[h] - c["delta"])).astype(MXU_DTYPE))
                prs.append(pr.astype(MXU_DTYPE))
            for h in range(HP):
                dv_ref[rows, hsl[h]] += jnp.dot(prs[h], per_head[h]["do"], preferred_element_type=F32)
            for h in range(HP):
                dk_ref[rows, ksl[h]] += jnp.dot(dss[h], per_head[h]["q"], preferred_element_type=F32)
            return tuple(dqs[h] + jnp.dot(kT_ref[j, ksl[h], :], dss[h], preferred_element_type=F32) for h in range(HP))

        dqs = lax.fori_loop(0, i, functools.partial(tile, masked=False),
                            tuple(jnp.zeros((dk, blk), F32) for _ in range(HP)))
        dqs = tile(i, dqs, True)
        for h in range(HP):
            dqT_ref[h * dk:(h + 1) * dk, :] = dqs[h] * qscale

    return pl.pallas_call(
        body, name=name, grid=(H // HP, nb),
        in_specs=[pl.BlockSpec((HP * dk, blk), lambda p, i: (p, i)),
                  pl.BlockSpec((blk, HP * dk), lambda p, i: (i, p)),
                  pl.BlockSpec((S, HP * dk), lambda p, i: (0, p)),
                  pl.BlockSpec((nb, HP * dk, blk), lambda p, i: (0, p, 0)),
                  pl.BlockSpec((S, hd), lambda p, i: (0, p)),
                  pl.BlockSpec((blk, hd), lambda p, i: (i, dcr + p)),
                  pl.BlockSpec((hd, blk), lambda p, i: (dcr + p, i)),
                  pl.BlockSpec((hd, blk), lambda p, i: (p, i)),
                  pl.BlockSpec((HP, 1, 1, blk), lambda p, i: (p, i, 0, 0))],
        out_specs=[pl.BlockSpec((HP * dk, blk), lambda p, i: (p, i)),
                   pl.BlockSpec((S, HP * dk), lambda p, i: (0, p)),
                   pl.BlockSpec((S, hd), lambda p, i: (0, p))],
        out_shape=[jax.ShapeDtypeStruct((H * dk, S), F32), jax.ShapeDtypeStruct((S, H * dk), F32),
                   jax.ShapeDtypeStruct((S, H * HEAD_DIM), F32)],
        compiler_params=_cparams("arbitrary", "arbitrary"),
    )(qT, q, k, kT3, v, dmix, dmixT, oT, lse)


def _log1m_beta(z):
    return -(jnp.maximum(z, 0.0) + jnp.log(1.0 + jnp.exp(-jnp.abs(z))))


def _dot01_left(m01, x, parts=2):
    acc = None
    rem = x
    for _ in range(parts):
        part = rem.astype(MXU_DTYPE)
        rem = rem - part.astype(F32)
        t = jnp.dot(m01, part, preferred_element_type=F32)
        acc = t if acc is None else acc + t
    return acc


def _sb_fwd_t(qT, h_att, vT3, *, blk, name):
    S = h_att.shape[0]
    nb = S // blk
    kcb = COL_SK // (HP * HEAD_DIM)

    def body(qT_ref, k_ref, vT_ref, oT_ref, lt_ref):
        i = pl.program_id(1)
        key = lax.broadcasted_iota(jnp.int32, (blk, blk), 0)
        qry = lax.broadcasted_iota(jnp.int32, (blk, blk), 1)
        strict = key < qry
        later = (qry > key).astype(MXU_DTYPE)
        qs = [qT_ref[h * HEAD_DIM:(h + 1) * HEAD_DIM, :] for h in range(HP)]

        def tile(j, carry, mask):
            r0 = pl.multiple_of(j * blk, blk)
            hsl = [slice(h * HEAD_DIM, (h + 1) * HEAD_DIM) for h in range(HP)]
            zs = [jnp.dot(k_ref[pl.ds(r0, blk), hsl[h]], qs[h], preferred_element_type=F32) for h in range(HP)]
            lbs = []
            for h in range(HP):
                lb = _log1m_beta(zs[h])
                lbs.append(lb if mask is None else jnp.where(mask, lb, 0.0))
            sums = [_dot01_left(later, lbs[h]) for h in range(HP)]
            probs = []
            for h in range(HP):
                a = jnp.exp(zs[h] + lbs[h] + sums[h] + carry[h][0])
                probs.append((a if mask is None else jnp.where(mask, a, 0.0)).astype(MXU_DTYPE))
            pvs = [jnp.dot(vT_ref[j, hsl[h], :], probs[h], preferred_element_type=F32) for h in range(HP)]
            return tuple((carry[h][0] + jnp.sum(lbs[h], axis=0, keepdims=True), carry[h][1] + pvs[h]) for h in range(HP))

        init = tuple((jnp.zeros((1, blk), F32), jnp.zeros((HEAD_DIM, blk), F32)) for _ in range(HP))
        carry = tile(i, init, strict)
        carry = lax.fori_loop(0, i, lambda jj, c: tile(i - 1 - jj, c, None), carry)
        for h in range(HP):
            oT_ref[h * HEAD_DIM:(h + 1) * HEAD_DIM, :] = carry[h][1]
            lt_ref[h, 0] = carry[h][0]

    hd = HP * HEAD_DIM
    return pl.pallas_call(
        body, name=name, grid=(4 // HP, nb),
        in_specs=[pl.BlockSpec((hd, blk), lambda p, i: (p, i)),
                  pl.BlockSpec((S, hd), lambda p, i: (0, kcb + p)),
                  pl.BlockSpec((nb, hd, blk), lambda p, i: (0, p, 0))],
        out_specs=[pl.BlockSpec((hd, blk), lambda p, i: (p, i)), pl.BlockSpec((HP, 1, 1, blk), lambda p, i: (p, i, 0, 0))],
        out_shape=[jax.ShapeDtypeStruct((GROUP_WIDTH, S), F32), jax.ShapeDtypeStruct((4, nb, 1, blk), F32)],
        compiler_params=_cparams("arbitrary", "arbitrary"),
    )(qT, h_att, vT3)


def _sb_bwd_t(qT, h_att, kT3, dmix, dmixT, ltot, *, dcb, qscale, blk, name):
    S = h_att.shape[0]
    nb = S // blk
    hd = HP * HEAD_DIM
    qcb, kcb, vcb = COL_SQ // hd, COL_SK // hd, COL_SV // hd
    dcr = dcb * 128 // hd

    def body(qT_ref, q_ref, k_ref, kT_ref, v_ref, do_ref, doT_ref, lt_ref, dqT_ref, dk_ref, dv_ref):
        i = pl.program_id(1)

        @pl.when(i == 0)
        def _():
            dk_ref[...] = jnp.zeros_like(dk_ref)
            dv_ref[...] = jnp.zeros_like(dv_ref)

        key = lax.broadcasted_iota(jnp.int32, (blk, blk), 0)
        qry = lax.broadcasted_iota(jnp.int32, (blk, blk), 1)
        strict = key < qry
        upto = (qry <= key).astype(MXU_DTYPE)
        before = (qry < key).astype(MXU_DTYPE)
        per_head = []
        for h in range(HP):
            hs = slice(h * HEAD_DIM, (h + 1) * HEAD_DIM)
            per_head.append(dict(qT=qT_ref[hs, :], q=q_ref[:, hs], doT=doT_ref[hs, :].astype(MXU_DTYPE),
                                 do=do_ref[:, hs].astype(MXU_DTYPE), lt=lt_ref[h, 0]))

        def tile(j, carry, mask):
            r0 = pl.multiple_of(j * blk, blk)
            rows = pl.ds(r0, blk)
            hsl = [slice(h * HEAD_DIM, (h + 1) * HEAD_DIM) for h in range(HP)]
            zs = [jnp.dot(k_ref[rows, hsl[h]], per_head[h]["qT"], preferred_element_type=F32) for h in range(HP)]
            das = [jnp.dot(v_ref[rows, hsl[h]], per_head[h]["doT"], preferred_element_type=F32) for h in range(HP)]
            lbs = []
            for h in range(HP):
                lb = _log1m_beta(zs[h])
                lbs.append(lb if mask is None else jnp.where(mask, lb, 0.0))
            sums = [_dot01_left(upto, lbs[h]) for h in range(HP)]
            probs, gs = [], []
            for h in range(HP):
                a = jnp.exp(zs[h] + lbs[h] + (per_head[h]["lt"] - carry[h][0] - sums[h]))
                a = a if mask is None else jnp.where(mask, a, 0.0)
                gs.append(das[h] * a)
                probs.append(a.astype(MXU_DTYPE))
            for h in range(HP):
                dv_ref[rows, hsl[h]] += jnp.dot(probs[h], per_head[h]["do"], preferred_element_type=F32)
            es = [_dot01_left(before, gs[h]) for h in range(HP)]
            dzs = []
            for h in range(HP):
                dz = gs[h] * jnp.exp(lbs[h]) - (carry[h][1] + es[h]) * jnp.exp(zs[h] + lbs[h])
                dzs.append((dz if mask is None else jnp.where(mask, dz, 0.0)).astype(MXU_DTYPE))
            for h in range(HP):
                dk_ref[rows, hsl[h]] += jnp.dot(dzs[h], per_head[h]["q"], preferred_element_type=F32)
            return tuple((carry[h][0] + jnp.sum(lbs[h], axis=0, keepdims=True),
                          carry[h][1] + jnp.sum(gs[h], axis=0, keepdims=True),
                          carry[h][2] + jnp.dot(kT_ref[j, hsl[h], :], dzs[h], preferred_element_type=F32))
                         for h in range(HP))

        zr = jnp.zeros((1, blk), F32)
        init = tuple((zr, zr, jnp.zeros((HEAD_DIM, blk), F32)) for _ in range(HP))
        carry = lax.fori_loop(0, i, lambda j, c: tile(j, c, None), init)
        carry = tile(i, carry, strict)
        for h in range(HP):
            dqT_ref[h * HEAD_DIM:(h + 1) * HEAD_DIM, :] = carry[h][2] * qscale

    return pl.pallas_call(
        body, name=name, grid=(4 // HP, nb),
        in_specs=[pl.BlockSpec((hd, blk), lambda p, i: (p, i)),
                  pl.BlockSpec((blk, hd), lambda p, i: (i, qcb + p)),
                  pl.BlockSpec((S, hd), lambda p, i: (0, kcb + p)),
                  pl.BlockSpec((nb, hd, blk), lambda p, i: (0, p, 0)),
                  pl.BlockSpec((S, hd), lambda p, i: (0, vcb + p)),
                  pl.BlockSpec((blk, hd), lambda p, i: (i, dcr + p)),
                  pl.BlockSpec((hd, blk), lambda p, i: (dcr + p, i)),
                  pl.BlockSpec((HP, 1, 1, blk), lambda p, i: (p, i, 0, 0))],
        out_specs=[pl.BlockSpec((hd, blk), lambda p, i: (p, i)),
                   pl.BlockSpec((S, hd), lambda p, i: (0, p)),
                   pl.BlockSpec((S, hd), lambda p, i: (0, p))],
        out_shape=[jax.ShapeDtypeStruct((GROUP_WIDTH, S), F32), jax.ShapeDtypeStruct((S, GROUP_WIDTH), F32),
                   jax.ShapeDtypeStruct((S, GROUP_WIDTH), F32)],
        compiler_params=_cparams("arbitrary", "arbitrary"),
    )(qT, h_att, h_att, kT3, h_att, dmix, dmixT, ltot)


def _swa_scores(q_ref, k_ref, n, h, start):
    g = h // 2
    kb = k_ref[pl.ds(start, 2 * WINDOW), g * HEAD_DIM:(g + 1) * HEAD_DIM]
    s = lax.dot_general(q_ref[:, h * HEAD_DIM:(h + 1) * HEAD_DIM], kb, NT, preferred_element_type=F32) * (HEAD_DIM ** -0.5)
    dist = (n * WINDOW + lax.broadcasted_iota(jnp.int32, (WINDOW, 2 * WINDOW), 0)
            - start - lax.broadcasted_iota(jnp.int32, (WINDOW, 2 * WINDOW), 1))
    s = s - SWA_SLOPES[h] * dist.astype(F32)
    valid = (dist >= 0) & (dist < WINDOW)
    return jnp.where(valid, s, NEG_INF), kb


def _swa_fwd(h_att, sinks, *, name):
    S = h_att.shape[0]
    nb = S // WINDOW
    qcb, kcb, vcb = COL_WQ // 256, COL_WK // 128, COL_WV // 128

    def body(sink_ref, q_ref, k_ref, v_ref, o_ref, lse_ref):
        n = pl.program_id(0)
        start = pl.multiple_of(jnp.maximum(n - 1, 0) * WINDOW, WINDOW)
        for h in range(4):
            g = h // 2
            s, _ = _swa_scores(q_ref, k_ref, n, h, start)
            sink = sink_ref[h]
            m = jnp.maximum(jnp.max(s, axis=1, keepdims=True), sink)
            e = jnp.exp(s - m)
            den = jnp.sum(e, axis=1, keepdims=True) + jnp.exp(sink - m)
            vb = v_ref[pl.ds(start, 2 * WINDOW), g * HEAD_DIM:(g + 1) * HEAD_DIM]
            o_ref[:, h * HEAD_DIM:(h + 1) * HEAD_DIM] = jnp.dot((e / den).astype(MXU_DTYPE), vb, preferred_element_type=F32)
            lse_ref[h] = m + jnp.log(den)

    return pl.pallas_call(
        body, name=name, grid=(nb,),
        in_specs=[pl.BlockSpec(memory_space=pltpu.SMEM),
                  pl.BlockSpec((WINDOW, 256), lambda n: (n, qcb)),
                  pl.BlockSpec((S, 128), lambda n: (0, kcb)),
                  pl.BlockSpec((S, 128), lambda n: (0, vcb))],
        out_specs=[pl.BlockSpec((WINDOW, 256), lambda n: (n, 0)), pl.BlockSpec((4, WINDOW, 1), lambda n: (0, n, 0))],
        out_shape=[jax.ShapeDtypeStruct((S, GROUP_WIDTH), F32), jax.ShapeDtypeStruct((4, S, 1), F32)],
        compiler_params=_cparams("arbitrary"),
    )(sinks, h_att, h_att, h_att)


def _swa_bwd(h_att, sinks, dmix, o_arr, lse, *, dcb, name):
    S = h_att.shape[0]
    nb = S // WINDOW
    qcb, kcb, vcb = COL_WQ // 256, COL_WK // 128, COL_WV // 128

    def body(sink_ref, q_ref, k_ref, v_ref, do_ref, o_ref, lse_ref, dq_ref, dk_ref, dv_ref, dsink_ref):
        n = pl.program_id(0)

        @pl.when(n == 0)
        def _():
            dk_ref[...] = jnp.zeros_like(dk_ref)
            dv_ref[...] = jnp.zeros_like(dv_ref)
            dsink_ref[...] = jnp.zeros_like(dsink_ref)

        start = pl.multiple_of(jnp.maximum(n - 1, 0) * WINDOW, WINDOW)
        for h in range(4):
            g = h // 2
            sl = slice(h * HEAD_DIM, (h + 1) * HEAD_DIM)
            gl = slice(g * HEAD_DIM, (g + 1) * HEAD_DIM)
            s, kb = _swa_scores(q_ref, k_ref, n, h, start)
            lse_h = lse_ref[h]
            pr = jnp.exp(s - lse_h)
            do = do_ref[:, sl]
            dob = do.astype(MXU_DTYPE)
            delta = jnp.sum(do * o_ref[:, sl], axis=1, keepdims=True)
            vb = v_ref[pl.ds(start, 2 * WINDOW), gl]
            ds = pr * (lax.dot_general(dob, vb, NT, preferred_element_type=F32) - delta)
            dsb = ds.astype(MXU_DTYPE)
            dq_ref[:, sl] = jnp.dot(dsb, kb, preferred_element_type=F32) * (HEAD_DIM ** -0.5)
            dk_ref[pl.ds(start, 2 * WINDOW), gl] += lax.dot_general(
                dsb, q_ref[:, sl], TN, preferred_element_type=F32) * (HEAD_DIM ** -0.5)
            dv_ref[pl.ds(start, 2 * WINDOW), gl] += lax.dot_general(pr.astype(MXU_DTYPE), dob, TN, preferred_element_type=F32)
            dsink_ref[h:h + 1, :] += jnp.zeros((1, 128), F32) - jnp.sum(jnp.exp(sink_ref[h] - lse_h) * delta)

    return pl.pallas_call(
        body, name=name, grid=(nb,),
        in_specs=[pl.BlockSpec(memory_space=pltpu.SMEM),
                  pl.BlockSpec((WINDOW, 256), lambda n: (n, qcb)),
                  pl.BlockSpec((S, 128), lambda n: (0, kcb)),
                  pl.BlockSpec((S, 128), lambda n: (0, vcb)),
                  pl.BlockSpec((WINDOW, 256), lambda n: (n, dcb)),
                  pl.BlockSpec((WINDOW, 256), lambda n: (n, 0)),
                  pl.BlockSpec((4, WINDOW, 1), lambda n: (0, n, 0))],
        out_specs=[pl.BlockSpec((WINDOW, 256), lambda n: (n, 0)),
                   pl.BlockSpec((S, 128), lambda n: (0, 0)),
                   pl.BlockSpec((S, 128), lambda n: (0, 0)),
                   pl.BlockSpec((4, 128), lambda n: (0, 0))],
        out_shape=[jax.ShapeDtypeStruct((S, GROUP_WIDTH), F32), jax.ShapeDtypeStruct((S, 128), F32),
                   jax.ShapeDtypeStruct((S, 128), F32), jax.ShapeDtypeStruct((4, 128), F32)],
        compiler_params=_cparams("arbitrary"),
    )(sinks, h_att, h_att, h_att, dmix, o_arr, lse)


def _tri(n, incl, upper):
    r = lax.broadcasted_iota(jnp.int32, (n, n), 0)
    c = lax.broadcasted_iota(jnp.int32, (n, n), 1)
    if upper:
        m = (r <= c) if incl else (r < c)
    else:
        m = (r >= c) if incl else (r > c)
    return m.astype(MXU_DTYPE)


def _fox_gate_fwd(fg, b_f, *, name):
    _, R, _ = fg.shape

    def body(b_ref, fg_ref, pos_ref, neg_ref):
        up_incl = _tri(128, True, True)
        ones = jnp.ones((128, 128), MXU_DTYPE)
        for h in range(4):
            z = fg_ref[h] + b_ref[h]
            logf = jnp.minimum(z, 0.0) - jnp.log(1.0 + jnp.exp(-jnp.abs(z)))
            within = _dot01(logf, up_incl, parts=3)
            totals = _dot01(logf, ones, parts=3)
            rem = within + _rows_other(totals, R, after=False)
            for part in range(3):
                piece = rem.astype(MXU_DTYPE)
                rem = rem - piece.astype(F32)
                pos_ref[h, part] = piece
                neg_ref[h, part] = -piece

    shape = (4, 3) + fg.shape[1:]
    return pl.pallas_call(
        body, name=name,
        in_specs=[pl.BlockSpec(memory_space=pltpu.SMEM), pl.BlockSpec(memory_space=pltpu.VMEM)],
        out_specs=[pl.BlockSpec(memory_space=pltpu.VMEM)] * 2,
        out_shape=[jax.ShapeDtypeStruct(shape, MXU_DTYPE)] * 2,
    )(b_f, fg)


def _rows_other(totals, n, after):
    r = lax.broadcasted_iota(jnp.int32, (n, n), 0)
    c = lax.broadcasted_iota(jnp.int32, (n, n), 1)
    m = ((c > r) if after else (c < r)).astype(MXU_DTYPE)
    acc = None
    rem = totals
    for _ in range(3):
        part = rem.astype(MXU_DTYPE)
        rem = rem - part.astype(F32)
        t = jnp.dot(m, part, preferred_element_type=F32)
        acc = t if acc is None else acc + t
    return acc


def _fox_gate_bwd(fg, b_f, dcum_k, dcum_q, *, q_unscale, name):
    _, R, _ = fg.shape

    def body(b_ref, fg_ref, dck_ref, dcq_ref, dfg_ref, db_ref):
        low_incl = _tri(128, True, False)
        ones = jnp.ones((128, 128), MXU_DTYPE)
        for h in range(4):
            dc = dcq_ref[h] * q_unscale - dck_ref[h]
            dlogf = _dot01(dc, low_incl, parts=3) + _rows_other(_dot01(dc, ones, parts=3), R, after=True)
            z = fg_ref[h] + b_ref[h]
            dz = dlogf * jnp.exp(jnp.minimum(-z, 0.0) - jnp.log(1.0 + jnp.exp(-jnp.abs(z))))
            dfg_ref[h] = dz
            db_ref[h:h + 1, :] = jnp.zeros((1, 128), F32) + jnp.sum(dz)

    return pl.pallas_call(
        body, name=name,
        in_specs=[pl.BlockSpec(memory_space=pltpu.SMEM)] + [pl.BlockSpec(memory_space=pltpu.VMEM)] * 3,
        out_specs=[pl.BlockSpec(memory_space=pltpu.VMEM), pl.BlockSpec(memory_space=pltpu.VMEM)],
        out_shape=[jax.ShapeDtypeStruct(fg.shape, F32), jax.ShapeDtypeStruct((4, 128), F32)],
    )(b_f, fg, dcum_k, dcum_q)


def _rope_rot(transpose):
    r = lax.broadcasted_iota(jnp.int32, (MLA_PAD, MLA_PAD), 0)
    c = lax.broadcasted_iota(jnp.int32, (MLA_PAD, MLA_PAD), 1)
    if transpose:
        r, c = c, r
    half = MLA_ROPE // 2
    lo, mid, hi = HEAD_DIM, HEAD_DIM + half, HEAD_DIM + MLA_ROPE
    minus = (c >= lo) & (c < mid) & (r == c + half)
    plus = (c >= mid) & (c < hi) & (r == c - half)
    return jnp.where(plus, 1.0, jnp.where(minus, -1.0, 0.0)).astype(MXU_DTYPE)


def _rope_lanes():
    lane = lax.broadcasted_iota(jnp.int32, (1, MLA_PAD), 1)
    return ((lane >= HEAD_DIM) & (lane < HEAD_DIM + MLA_ROPE)).astype(F32)


def _rms(x, g, eps=1e-6):
    r = lax.rsqrt(jnp.mean(x * x, axis=-1, keepdims=True) + eps)
    return x * r * g, r


def _rms_bwd(dy, x, r, g):
    xh = x * r
    dxh = dy * g
    dx = r * (dxh - xh * jnp.mean(dxh * xh, axis=-1, keepdims=True))
    return dx, dy * xh


def _mla_prep_fwd(lat, g_q, g_kv, wuq, wuk, wuv, cosm, sinm, *, bs, name):
    S = lat.shape[0]

    def body(lat_ref, gq_ref, gkv_ref, wuq_ref, wuk_ref, wuv_ref, cos_ref, sin_ref, q_ref, k_ref, v_ref):
        rot = _rope_rot(False)
        cosm_, sinm_ = cos_ref[...], sin_ref[...]
        nq, _ = _rms(lat_ref[:, 0:MLA_Q_RANK], gq_ref[...])
        nkv, _ = _rms(lat_ref[:, MLA_Q_RANK:MLA_Q_RANK + MLA_KV_RANK], gkv_ref[...])
        qlat = jnp.dot(nq.astype(MXU_DTYPE), wuq_ref[...], preferred_element_type=F32)
        klat = jnp.dot(nkv.astype(MXU_DTYPE), wuk_ref[...], preferred_element_type=F32)
        v_ref[...] = jnp.dot(nkv.astype(MXU_DTYPE), wuv_ref[...], preferred_element_type=F32).astype(v_ref.dtype)
        krb = lat_ref[:, 384:512]
        kr = krb * (cosm_ * _rope_lanes()) + _dot01(krb, rot, parts=3) * sinm_
        for h in range(4):
            sl = slice(h * MLA_PAD, (h + 1) * MLA_PAD)
            qh = qlat[:, sl]
            q_ref[:, sl] = ((qh * cosm_ + _dot01(qh, rot, parts=3) * sinm_) * (MLA_QK ** -0.5)).astype(q_ref.dtype)
            k_ref[:, sl] = (klat[:, sl] + kr).astype(k_ref.dtype)

    full = lambda a: pl.BlockSpec(a.shape, lambda i: (0,) * a.ndim)
    return pl.pallas_call(
        body, name=name, grid=(S // bs,),
        in_specs=[pl.BlockSpec((bs, LAT_W), lambda i: (i, 0)), full(g_q), full(g_kv), full(wuq), full(wuk), full(wuv),
                  pl.BlockSpec((bs, MLA_PAD), lambda i: (i, 0)), pl.BlockSpec((bs, MLA_PAD), lambda i: (i, 0))],
        out_specs=[pl.BlockSpec((bs, 512), lambda i: (i, 0)), pl.BlockSpec((bs, 512), lambda i: (i, 0)),
                   pl.BlockSpec((bs, 256), lambda i: (i, 0))],
        out_shape=[jax.ShapeDtypeStruct((S, 512), MXU_DTYPE), jax.ShapeDtypeStruct((S, 512), MXU_DTYPE),
                   jax.ShapeDtypeStruct((S, 256), MXU_DTYPE)],
        compiler_params=_cparams("parallel"),
    )(lat, g_q, g_kv, wuq, wuk, wuv, cosm, sinm)


def _mla_prep_bwd(lat, g_q, g_kv, wuq, wuk, wuv, cosm, sinm, dq, dk, dv, *, bs, name):
    S = lat.shape[0]

    def body(lat_ref, gq_ref, gkv_ref, wuq_ref, wuk_ref, wuv_ref, cos_ref, sin_ref, dq_ref, dk_ref, dv_ref,
             dlat_ref, dwuq_ref, dwuk_ref, dwuv_ref, dgq_ref, dgkv_ref):
        @pl.when(pl.program_id(0) == 0)
        def _():
            for r in (dwuq_ref, dwuk_ref, dwuv_ref, dgq_ref, dgkv_ref):
                r[...] = jnp.zeros_like(r)

        rot_t = _rope_rot(True)
        cosm_, sinm_ = cos_ref[...], sin_ref[...]
        cq = lat_ref[:, 0:MLA_Q_RANK]
        ckv = lat_ref[:, MLA_Q_RANK:MLA_Q_RANK + MLA_KV_RANK]
        nq, rq = _rms(cq, gq_ref[...])
        nkv, rkv = _rms(ckv, gkv_ref[...])
        nqb, nkvb = nq.astype(MXU_DTYPE), nkv.astype(MXU_DTYPE)

        dqlat = []
        dkr = jnp.zeros((bs, MLA_PAD), F32)
        for h in range(4):
            sl = slice(h * MLA_PAD, (h + 1) * MLA_PAD)
            dqh = dq_ref[:, sl]
            dqlat.append(dqh * cosm_ + _dot01(dqh * sinm_, rot_t, parts=3))
            dkr = dkr + dk_ref[:, sl]
        dqlat = jnp.concatenate(dqlat, axis=1).astype(MXU_DTYPE)
        dkb = dk_ref[...].astype(MXU_DTYPE)
        dvb = dv_ref[...].astype(MXU_DTYPE)

        dnq = lax.dot_general(dqlat, wuq_ref[...], NT, preferred_element_type=F32)
        dnkv = (lax.dot_general(dkb, wuk_ref[...], NT, preferred_element_type=F32)
                + lax.dot_general(dvb, wuv_ref[...], NT, preferred_element_type=F32))
        dwuq_ref[...] += lax.dot_general(nqb, dqlat, TN, preferred_element_type=F32)
        dwuk_ref[...] += lax.dot_general(nkvb, dkb, TN, preferred_element_type=F32)
        dwuv_ref[...] += lax.dot_general(nkvb, dvb, TN, preferred_element_type=F32)
        dcq, tq = _rms_bwd(dnq, cq, rq, gq_ref[...])
        dckv, tkv = _rms_bwd(dnkv, ckv, rkv, gkv_ref[...])
        dgq_ref[...] += jnp.sum(tq, axis=0, keepdims=True)
        dgkv_ref[...] += jnp.sum(tkv, axis=0, keepdims=True)
        dlat_ref[:, 0:MLA_Q_RANK] = dcq.astype(dlat_ref.dtype)
        dlat_ref[:, MLA_Q_RANK:MLA_Q_RANK + MLA_KV_RANK] = dckv.astype(dlat_ref.dtype)
        dkrb = dkr * (cosm_ * _rope_lanes()) + _dot01(dkr * sinm_, rot_t, parts=3)
        dlat_ref[:, 384:512] = dkrb.astype(dlat_ref.dtype)

    full = lambda a: pl.BlockSpec(a.shape, lambda i: (0,) * a.ndim)
    row = lambda w: pl.BlockSpec((bs, w), lambda i: (i, 0))
    acc = lambda *shape: pl.BlockSpec(shape, lambda i: (0,) * len(shape))
    return pl.pallas_call(
        body, name=name, grid=(S // bs,),
        in_specs=[row(LAT_W), full(g_q), full(g_kv), full(wuq), full(wuk), full(wuv), row(MLA_PAD), row(MLA_PAD),
                  row(512), row(512), row(256)],
        out_specs=[row(512), acc(256, 512), acc(128, 512), acc(128, 256), acc(1, 256), acc(1, 128)],
        out_shape=[jax.ShapeDtypeStruct((S, 512), MXU_DTYPE), jax.ShapeDtypeStruct((256, 512), F32),
                   jax.ShapeDtypeStruct((128, 512), F32), jax.ShapeDtypeStruct((128, 256), F32),
                   jax.ShapeDtypeStruct((1, 256), F32), jax.ShapeDtypeStruct((1, 128), F32)],
        compiler_params=_cparams("arbitrary"),
    )(lat, g_q, g_kv, wuq, wuk, wuv, cosm, sinm, dq, dk, dv)


def _row_spec(bs, w):
    return pl.BlockSpec((bs, w), lambda i: (i, 0))


def _vec_spec(w):
    return pl.BlockSpec((1, w), lambda i: (0, 0))


def _gnorm_fwd(outs, g, *, bs, name):
    S = outs[0].shape[0]

    def body(a_ref, b_ref, c_ref, d_ref, g_ref, o_ref):
        for k, ref in enumerate((a_ref, b_ref, c_ref, d_ref)):
            sl = slice(k * GROUP_WIDTH, (k + 1) * GROUP_WIDTH)
            y, _ = _rms(ref[...], g_ref[:, sl])
            o_ref[:, sl] = y.astype(o_ref.dtype)

    return pl.pallas_call(
        body, name=name, grid=(S // bs,),
        in_specs=[_row_spec(bs, GROUP_WIDTH)] * 4 + [_vec_spec(D_MODEL)],
        out_specs=_row_spec(bs, D_MODEL), out_shape=jax.ShapeDtypeStruct((S, D_MODEL), MXU_DTYPE),
        compiler_params=_cparams("parallel"),
    )(*outs, g)


def _gnorm_bwd(dgn, outs, g, *, bs, name):
    S = dgn.shape[0]

    def body(dgn_ref, a_ref, b_ref, c_ref, d_ref, g_ref, dmix_ref, dg_ref):
        @pl.when(pl.program_id(0) == 0)
        def _():
            dg_ref[...] = jnp.zeros_like(dg_ref)

        for k, ref in enumerate((a_ref, b_ref, c_ref, d_ref)):
            sl = slice(k * GROUP_WIDTH, (k + 1) * GROUP_WIDTH)
            x = ref[...]
            _, r = _rms(x, g_ref[:, sl])
            dx, t = _rms_bwd(dgn_ref[:, sl], x, r, g_ref[:, sl])
            dmix_ref[:, sl] = dx
            dg_ref[:, sl] += jnp.sum(t, axis=0, keepdims=True)

    return pl.pallas_call(
        body, name=name, grid=(S // bs,),
        in_specs=[_row_spec(bs, D_MODEL)] + [_row_spec(bs, GROUP_WIDTH)] * 4 + [_vec_spec(D_MODEL)],
        out_specs=[_row_spec(bs, D_MODEL), _vec_spec(D_MODEL)],
        out_shape=[jax.ShapeDtypeStruct((S, D_MODEL), F32), jax.ShapeDtypeStruct((1, D_MODEL), F32)],
        compiler_params=_cparams("arbitrary"),
    )(dgn, *outs, g)


def _ln_fwd(u, g, b, *, bs, name):
    S = u.shape[0]

    def body(u_ref, g_ref, b_ref, y_ref, yb_ref, xh_ref, rs_ref):
        x = u_ref[...]
        mu = jnp.mean(x, axis=-1, keepdims=True)
        xc = x - mu
        rs = lax.rsqrt(jnp.mean(xc * xc, axis=-1, keepdims=True) + 1e-5)
        xh = xc * rs
        y = xh * g_ref[...] + b_ref[...]
        y_ref[...] = y
        yb_ref[...] = y.astype(yb_ref.dtype)
        xh_ref[...] = xh
        rs_ref[...] = rs

    return pl.pallas_call(
        body, name=name, grid=(S // bs,),
        in_specs=[_row_spec(bs, D_MODEL), _vec_spec(D_MODEL), _vec_spec(D_MODEL)],
        out_specs=[_row_spec(bs, D_MODEL), _row_spec(bs, D_MODEL), _row_spec(bs, D_MODEL), _row_spec(bs, 1)],
        out_shape=[jax.ShapeDtypeStruct((S, D_MODEL), F32), jax.ShapeDtypeStruct((S, D_MODEL), MXU_DTYPE),
                   jax.ShapeDtypeStruct((S, D_MODEL), F32), jax.ShapeDtypeStruct((S, 1), F32)],
        compiler_params=_cparams("parallel"),
    )(u, g, b)


def _ln_bwd(dy, xh, rs, g, *, bs, name):
    S = dy.shape[0]

    def body(dy_ref, xh_ref, rs_ref, g_ref, du_ref, dub_ref, dg_ref, db_ref):
        @pl.when(pl.program_id(0) == 0)
        def _():
            dg_ref[...] = jnp.zeros_like(dg_ref)
            db_ref[...] = jnp.zeros_like(db_ref)

        dy_, xh_ = dy_ref[...], xh_ref[...]
        dxh = dy_ * g_ref[...]
        du = rs_ref[...] * (dxh - jnp.mean(dxh, axis=-1, keepdims=True)
                            - xh_ * jnp.mean(dxh * xh_, axis=-1, keepdims=True))
        du_ref[...] = du
        dub_ref[...] = du.astype(dub_ref.dtype)
        dg_ref[...] += jnp.sum(dy_ * xh_, axis=0, keepdims=True)
        db_ref[...] += jnp.sum(dy_, axis=0, keepdims=True)

    return pl.pallas_call(
        body, name=name, grid=(S // bs,),
        in_specs=[_row_spec(bs, D_MODEL), _row_spec(bs, D_MODEL), _row_spec(bs, 1), _vec_spec(D_MODEL)],
        out_specs=[_row_spec(bs, D_MODEL), _row_spec(bs, D_MODEL), _vec_spec(D_MODEL), _vec_spec(D_MODEL)],
        out_shape=[jax.ShapeDtypeStruct((S, D_MODEL), F32), jax.ShapeDtypeStruct((S, D_MODEL), MXU_DTYPE),
                   jax.ShapeDtypeStruct((1, D_MODEL), F32), jax.ShapeDtypeStruct((1, D_MODEL), F32)],
        compiler_params=_cparams("arbitrary"),
    )(dy, xh, rs, g)


def _swiglu_fwd(gu, *, bs, name):
    S = gu.shape[0]

    def body(gu_ref, a_ref):
        gt = gu_ref[:, :D_FF]
        a_ref[...] = (gt / (1.0 + jnp.exp(-gt)) * gu_ref[:, D_FF:]).astype(a_ref.dtype)

    return pl.pallas_call(
        body, name=name, grid=(S // bs,),
        in_specs=[_row_spec(bs, 2 * D_FF)],
        out_specs=_row_spec(bs, D_FF), out_shape=jax.ShapeDtypeStruct((S, D_FF), MXU_DTYPE),
        compiler_params=_cparams("parallel"),
    )(gu)


def _swiglu_bwd(da, gu, *, bs, name):
    S = gu.shape[0]

    def body(da_ref, gu_ref, dgu_ref):
        gt, da_ = gu_ref[:, :D_FF], da_ref[...]
        sg = 1.0 / (1.0 + jnp.exp(-gt))
        silu = gt * sg
        dgu_ref[:, :D_FF] = (da_ * gu_ref[:, D_FF:] * (sg + silu * (1.0 - sg))).astype(dgu_ref.dtype)
        dgu_ref[:, D_FF:] = (da_ * silu).astype(dgu_ref.dtype)

    return pl.pallas_call(
        body, name=name, grid=(S // bs,),
        in_specs=[_row_spec(bs, D_FF), _row_spec(bs, 2 * D_FF)],
        out_specs=_row_spec(bs, 2 * D_FF), out_shape=jax.ShapeDtypeStruct((S, 2 * D_FF), MXU_DTYPE),
        compiler_params=_cparams("parallel"),
    )(da, gu)


def _loss_head(y, target, *, bs, name):
    S = y.shape[0]

    def body(y_ref, t_ref, dy_ref, loss_ref):
        @pl.when(pl.program_id(0) == 0)
        def _():
            loss_ref[...] = jnp.zeros_like(loss_ref)

        e = y_ref[...] - t_ref[...]
        dy_ref[...] = e * (1.0 / D_MODEL)
        per_tok = jnp.mean(e * e, axis=-1, keepdims=True)
        loss_ref[...] += 0.5 * jnp.sum(per_tok, axis=0, keepdims=True)

    return pl.pallas_call(
        body, name=name, grid=(S // bs,),
        in_specs=[_row_spec(bs, D_MODEL), _row_spec(bs, D_MODEL)],
        out_specs=[_row_spec(bs, D_MODEL), pl.BlockSpec((1, 1), lambda i: (0, 0))],
        out_shape=[jax.ShapeDtypeStruct((S, D_MODEL), F32), jax.ShapeDtypeStruct((1, 1), F32)],
        compiler_params=_cparams("arbitrary"),
    )(y, target)


def _blk(n, target):
    if n <= target:
        return n
    best = None
    for b in range(128, target + 1, 128):
        if n % b == 0:
            best = b
    assert best is not None, n
    return best


def _rope_tables(S):
    pos = jnp.arange(S, dtype=F32)
    inv = ROPE_THETA ** (-jnp.arange(0, MLA_ROPE, 2, dtype=F32) / MLA_ROPE)
    ang = pos[:, None] * inv[None, :]
    cos, sin = jnp.cos(ang), jnp.sin(ang)
    one, zero, pad = jnp.ones((S, HEAD_DIM), F32), jnp.zeros((S, HEAD_DIM), F32), jnp.zeros((S, MLA_PAD - MLA_QK), F32)
    return jnp.concatenate([one, cos, cos, pad], axis=1), jnp.concatenate([zero, sin, sin, pad], axis=1)


def _prep_weights(w_in, w_uq, w_ukv, w_o, w_gate, w_up, w_down):
    z = lambda n: jnp.zeros((D_MODEL, n), w_in.dtype)
    win_a = jnp.concatenate([w_in[:, 0:768], w_in[:, 1188:2468]], axis=1)
    win_l = jnp.concatenate([w_in[:, 772:1156], z(64), w_in[:, 1156:1188], z(32), w_in[:, 768:772], z(124)], axis=1)
    kv = w_ukv.reshape(MLA_KV_RANK, 4, 2 * HEAD_DIM)
    return dict(
        win_a=win_a, win_l=win_l, win_p=jnp.concatenate([win_a, win_l], axis=1),
        wuq=jnp.pad(w_uq.reshape(MLA_Q_RANK, 4, MLA_QK), ((0, 0), (0, 0), (0, MLA_PAD - MLA_QK))).reshape(MLA_Q_RANK, 512),
        wuk=jnp.pad(kv[:, :, :HEAD_DIM], ((0, 0), (0, 0), (0, HEAD_DIM))).reshape(MLA_KV_RANK, 512),
        wuv=kv[:, :, HEAD_DIM:].reshape(MLA_KV_RANK, 256),
        w_o=w_o, wgu=jnp.concatenate([w_gate, w_up], axis=1), w_down=w_down)


def _unprep_grads(dwin_p, dwuq, dwuk, dwuv, dwo, dwgu, dwd):
    dw_in = jnp.concatenate([dwin_p[:, 0:768], dwin_p[:, 2560:2564], dwin_p[:, 2048:2432], dwin_p[:, 2496:2528],
                             dwin_p[:, 768:2048]], axis=1)
    dw_uq = dwuq.reshape(MLA_Q_RANK, 4, MLA_PAD)[:, :, :MLA_QK].reshape(MLA_Q_RANK, 4 * MLA_QK)
    dw_ukv = jnp.concatenate([dwuk.reshape(MLA_KV_RANK, 4, MLA_PAD)[:, :, :HEAD_DIM],
                              dwuv.reshape(MLA_KV_RANK, 4, HEAD_DIM)], axis=2).reshape(MLA_KV_RANK, 512)
    return dict(w_in=dw_in, mla_w_uq=dw_uq, mla_w_ukv=dw_ukv, w_o=dwo, w_gate=dwgu[:, :D_FF], w_up=dwgu[:, D_FF:],
                w_down=dwd)


def _layer_fwd(l, x, xb, W, P, tabs, blk):
    S = x.shape[0]
    nb = S // blk
    n = lambda s: f"l{l}_{s}"
    bs = min(512, S)
    h_att = _mm(xb, W["win_a"], name=n("in_att"), out_dtype=MXU_DTYPE, bk=1024, colscale=Q_COLSCALE)
    lat = _mm(xb, W["win_l"], name=n("in_lat"), bn=LAT_W, bk=1024)
    fg = lat[:, 512:516].T.reshape(4, S // 128, 128)
    cpos, cneg = _fox_gate_fwd(fg, P["fox_b_f"], name=n("fox_gate"))
    one3 = jnp.ones((S, 4, 3), MXU_DTYPE)
    zpad = jnp.zeros((S, 4, MLA_PAD - HEAD_DIM - 6), MXU_DTYPE)
    per_tok = lambda parts: parts.reshape(4, 3, S).transpose(2, 0, 1)
    q_f = jnp.concatenate([h_att[:, COL_FQ:COL_FQ + 256].reshape(S, 4, HEAD_DIM), per_tok(cpos), one3, zpad],
                          axis=2).reshape(S, 4 * MLA_PAD)
    k_f = jnp.concatenate([h_att[:, COL_FK:COL_FK + 256].reshape(S, 4, HEAD_DIM), one3, per_tok(cneg), zpad],
                          axis=2).reshape(S, 4 * MLA_PAD)
    v_f = h_att[:, COL_FV:COL_FV + 256]
    oT_a, lse_a = _smax_fwd_t(q_f.T, k_f, _kv_blocks_t(v_f, blk), dk=MLA_PAD, blk=blk, name=n("fox_fwd"))
    q_m, k_m, v_m = _mla_prep_fwd(lat, P["mla_g_q"], P["mla_g_kv"], W["wuq"], W["wuk"], W["wuv"], *tabs,
                                  bs=bs, name=n("mla_prep"))
    oT_b, lse_b = _smax_fwd_t(q_m.T, k_m, _kv_blocks_t(v_m, blk), dk=MLA_PAD, blk=blk, name=n("mla_fwd"))
    qT_c = h_att[:, COL_SQ:COL_SQ + 256].T
    oT_c, lt_c = _sb_fwd_t(qT_c, h_att, _kv_blocks_t(h_att[:, COL_SV:COL_SV + 256], blk), blk=blk, name=n("sb_fwd"))
    out_d, lse_d = _swa_fwd(h_att, P["swa_sinks"], name=n("swa_fwd"))
    outs = (oT_a.T, oT_b.T, oT_c.T, out_d)
    gn = _gnorm_fwd(outs, P["mix_g"], bs=bs, name=n("gnorm"))
    u1 = _mm(gn, W["w_o"], name=n("out_proj"), bk=1024, resid=x, alpha=ALPHA)
    x1, x1b, xh1, rs1 = _ln_fwd(u1, P["ln1_g"], P["ln1_b"], bs=bs, name=n("ln1"))
    gu = _mm(x1b, W["wgu"], name=n("gate_up"), bk=1024)
    a = _swiglu_fwd(gu, bs=min(256, S), name=n("swiglu"))
    u2 = _mm(a, W["w_down"], name=n("down"), bk=_blk(D_FF, 1408), resid=x1, alpha=ALPHA)
    x2, x2b, xh2, rs2 = _ln_fwd(u2, P["ln2_g"], P["ln2_b"], bs=bs, name=n("ln2"))
    saved = dict(xb=xb, h_att=h_att, lat=lat, fg=fg, outs=outs, oT_a=oT_a, oT_b=oT_b, q_f=q_f, k_f=k_f, v_f=v_f,
                 qT_c=qT_c, lse_a=lse_a, lse_b=lse_b, lse_d=lse_d, lt_c=lt_c, q_m=q_m, k_m=k_m, v_m=v_m, gn=gn,
                 xh1=xh1, rs1=rs1, x1b=x1b, gu=gu, a=a, xh2=xh2, rs2=rs2)
    return x2, x2b, saved


def _layer_bwd(l, dx2, sv, W, P, tabs, blk):
    S = dx2.shape[0]
    n = lambda s: f"l{l}_{s}"
    bs = min(512, S)
    h_att = sv["h_att"]
    du2, du2b, dg2, db2 = _ln_bwd(dx2, sv["xh2"], sv["rs2"], P["ln2_g"], bs=bs, name=n("ln2_bwd"))
    da = _mm(du2b, W["w_down"], name=n("down_dx"), tb=True, bn=_blk(D_FF, 512), bk=1024)
    dwd = _mm(sv["a"], du2b, name=n("down_dw"), ta=True, bm=_blk(D_FF, 512))
    dgu = _swiglu_bwd(da, sv["gu"], bs=min(256, S), name=n("swiglu_bwd"))
    dx1 = _mm(dgu, W["wgu"], name=n("gate_up_dx"), tb=True, bk=_blk(2 * D_FF, 1408), resid=du2, alpha=ALPHA)
    dwgu = _mm(sv["x1b"], dgu, name=n("gate_up_dw"), ta=True)
    du1, du1b, dg1, db1 = _ln_bwd(dx1, sv["xh1"], sv["rs1"], P["ln1_g"], bs=bs, name=n("ln1_bwd"))
    dgn = _mm(du1b, W["w_o"], name=n("out_proj_dx"), tb=True, bk=1024)
    dwo = _mm(sv["gn"], du1b, name=n("out_proj_dw"), ta=True)
    dmix, dmixg = _gnorm_bwd(dgn, sv["outs"], P["mix_g"], bs=bs, name=n("gnorm_bwd"))
    dmixT = dmix.T
    q_f, k_f = sv["q_f"], sv["k_f"]
    dqT_a, dk_a, dva = _smax_bwd_t(q_f.T, q_f, k_f, _kv_blocks_t(k_f, blk), sv["v_f"], dmix, dmixT, sv["oT_a"],
                                   sv["lse_a"], dk=MLA_PAD, dcb=0, qscale=HEAD_DIM ** -0.5, blk=blk, name=n("fox_bwd"))
    dq_a, dk_a = dqT_a.T.reshape(S, 4, MLA_PAD), dk_a.reshape(S, 4, MLA_PAD)
    dqa, dka = dq_a[:, :, :HEAD_DIM].reshape(S, 256), dk_a[:, :, :HEAD_DIM].reshape(S, 256)
    dcq = dq_a[:, :, HEAD_DIM].T.reshape(4, S // 128, 128)
    dck = dk_a[:, :, HEAD_DIM + 3].T.reshape(4, S // 128, 128)
    q_m, k_m = sv["q_m"], sv["k_m"]
    dqT_b, dkb, dvb = _smax_bwd_t(q_m.T, q_m, k_m, _kv_blocks_t(k_m, blk), sv["v_m"], dmix, dmixT, sv["oT_b"],
                                  sv["lse_b"], dk=MLA_PAD, dcb=2, qscale=MLA_QK ** -0.5, blk=blk, name=n("mla_bwd"))
    dqT_c, dkc, dvc = _sb_bwd_t(sv["qT_c"], h_att, _kv_blocks_t(h_att[:, COL_SK:COL_SK + 256], blk), dmix, dmixT,
                                sv["lt_c"], dcb=4, qscale=HEAD_DIM ** -0.5, blk=blk, name=n("sb_bwd"))
    dqc = dqT_c.T
    dqd, dkd, dvd, dsink = _swa_bwd(h_att, P["swa_sinks"], dmix, sv["outs"][3], sv["lse_d"], dcb=3, name=n("swa_bwd"))
    dlat, dwuq, dwuk, dwuv, dgq, dgkv = _mla_prep_bwd(
        sv["lat"], P["mla_g_q"], P["mla_g_kv"], W["wuq"], W["wuk"], W["wuv"], *tabs, dqT_b.T, dkb, dvb,
        bs=bs, name=n("mla_prep_bwd"))
    dfg, dbf = _fox_gate_bwd(sv["fg"], P["fox_b_f"], dck, dcq, q_unscale=HEAD_DIM ** 0.5, name=n("fox_gate_bwd"))
    dfg_blk = jnp.pad(dfg.reshape(4, S).T, ((0, 0), (0, 124)))
    dh = jnp.concatenate([t.astype(MXU_DTYPE) for t in (dqa, dka, dva, dqc, dkc, dvc, dqd, dkd, dvd, dlat, dfg_blk)], axis=1)
    dx = _mm(dh, W["win_p"], name=n("in_dx"), tb=True, bk=_blk(PERM_W, 1024), resid=du1, alpha=ALPHA)
    dwin_p = _mm(sv["xb"], dh, name=n("in_dw"), ta=True, bn=_blk(PERM_W, 1024))
    grads = _unprep_grads(dwin_p, dwuq, dwuk, dwuv, dwo, dwgu, dwd)
    grads.update(fox_b_f=dbf[:, 0], mla_g_q=dgq[0], mla_g_kv=dgkv[0], swa_sinks=dsink[:, 0], mix_g=dmixg[0],
                 ln1_g=dg1[0], ln1_b=db1[0], ln2_g=dg2[0], ln2_b=db2[0])
    return dx, grads


BIG = ("w_in", "mla_w_uq", "mla_w_ukv", "w_o", "w_gate", "w_up", "w_down")
SMALL = ("fox_b_f", "mla_g_q", "mla_g_kv", "swa_sinks", "mix_g", "ln1_g", "ln1_b", "ln2_g", "ln2_b")
SHARD_AXIS = dict(w_in=2, mla_w_uq=2, mla_w_ukv=2, w_o=1, w_gate=2, w_up=2, w_down=1)
N_CHIPS = 4
ANY = pl.BlockSpec(memory_space=pl.ANY)


def _chip_exchange(tensors, *, scatter, name):
    nt = len(tensors)

    def body(*refs):
        ins, outs = refs[:nt], refs[nt:2 * nt]
        send_sems, recv_sems, local_sems = refs[2 * nt:]
        x, y, c = lax.axis_index("x"), lax.axis_index("y"), lax.axis_index("c")
        me = 2 * x + y
        peers = [(1 - x, y), (x, 1 - y), (1 - x, 1 - y)]
        local, sends, recvs = [], [], []
        for t in range(nt):
            local.append(pltpu.make_async_copy(ins[t].at[me] if scatter else ins[t], outs[t].at[me], local_sems.at[t]))
            for r, (px, py) in enumerate(peers):
                k = 3 * t + r
                theirs = 2 * px + py
                sends.append(pltpu.make_async_remote_copy(
                    src_ref=ins[t].at[theirs] if scatter else ins[t], dst_ref=outs[t].at[me],
                    send_sem=send_sems.at[k], recv_sem=recv_sems.at[k], device_id=(px, py, c), device_id_type=MESH))
                recvs.append(pltpu.make_async_remote_copy(
                    src_ref=ins[t].at[me] if scatter else ins[t], dst_ref=outs[t].at[theirs],
                    send_sem=send_sems.at[k], recv_sem=recv_sems.at[k], device_id=(px, py, c), device_id_type=MESH))
        for cp in local + sends:
            cp.start()
        for cp in recvs:
            cp.wait_recv()
        for cp in sends:
            cp.wait_send()
        for cp in local:
            cp.wait()

    out_shape = [jax.ShapeDtypeStruct(t.shape if scatter else (N_CHIPS,) + t.shape, t.dtype) for t in tensors]
    return pl.pallas_call(
        body, name=name, in_specs=[ANY] * nt, out_specs=[ANY] * nt, out_shape=out_shape,
        scratch_shapes=[pltpu.SemaphoreType.DMA((3 * nt,)), pltpu.SemaphoreType.DMA((3 * nt,)),
                        pltpu.SemaphoreType.DMA((nt,))],
        compiler_params=pltpu.CompilerParams(has_side_effects=True),
    )(*tensors)


def _core_exchange(tensors, *, name):
    nt = len(tensors)

    def body(*refs):
        ins, outs = refs[:nt], refs[nt:2 * nt]
        send_sems, recv_sems = refs[2 * nt:]
        sibling = (lax.axis_index("x"), lax.axis_index("y"), 1 - lax.axis_index("c"))
        copies = [pltpu.make_async_remote_copy(src_ref=ins[t], dst_ref=outs[t], send_sem=send_sems.at[t],
                                               recv_sem=recv_sems.at[t], device_id=sibling, device_id_type=MESH)
                  for t in range(nt)]
        for cp in copies:
            cp.start()
        for cp in copies:
            cp.wait_recv()
        for cp in copies:
            cp.wait_send()

    return pl.pallas_call(
        body, name=name, in_specs=[ANY] * nt, out_specs=[ANY] * nt,
        out_shape=[jax.ShapeDtypeStruct(t.shape, t.dtype) for t in tensors],
        scratch_shapes=[pltpu.SemaphoreType.DMA((nt,)), pltpu.SemaphoreType.DMA((nt,))],
        compiler_params=pltpu.CompilerParams(has_side_effects=True),
    )(*tensors)


def _all_sum_small(block, *, name):
    R = block.shape[0]
    n_dev = 8

    def body(x_ref, o_ref, slots, send_sems, recv_sems):
        x, y, c = lax.axis_index("x"), lax.axis_index("y"), lax.axis_index("c")
        me = 4 * x + 2 * y + c
        slots[me] = x_ref[...]
        sends, recvs = [], []
        for d in range(1, n_dev):
            px, py, pc = x ^ (d >> 2), y ^ ((d >> 1) & 1), c ^ (d & 1)
            theirs = 4 * px + 2 * py + pc
            sends.append(pltpu.make_async_remote_copy(
                src_ref=x_ref, dst_ref=slots.at[me], send_sem=send_sems.at[d - 1], recv_sem=recv_sems.at[d - 1],
                device_id=(px, py, pc), device_id_type=MESH))
            recvs.append(pltpu.make_async_remote_copy(
                src_ref=x_ref, dst_ref=slots.at[theirs], send_sem=send_sems.at[d - 1], recv_sem=recv_sems.at[d - 1],
                device_id=(px, py, pc), device_id_type=MESH))
        for cp in sends:
            cp.start()
        for cp in recvs:
            cp.wait_recv()
        for cp in sends:
            cp.wait_send()
        total = slots[0]
        for k in range(1, n_dev):
            total = total + slots[k]
        o_ref[...] = total

    return pl.pallas_call(
        body, name=name, in_specs=[pl.BlockSpec(memory_space=pltpu.VMEM)],
        out_specs=pl.BlockSpec(memory_space=pltpu.VMEM), out_shape=jax.ShapeDtypeStruct((R, 128), F32),
        scratch_shapes=[pltpu.VMEM((n_dev, R, 128), F32), pltpu.SemaphoreType.DMA((n_dev - 1,)),
                        pltpu.SemaphoreType.DMA((n_dev - 1,))],
        compiler_params=pltpu.CompilerParams(has_side_effects=True),
    )(block)


def _sum_chips(recv, *, br, name):
    _, R, C = recv.shape

    def body(r_ref, o_ref):
        total = r_ref[0].astype(F32)
        for k in range(1, N_CHIPS):
            total = total + r_ref[k].astype(F32)
        o_ref[...] = total

    return pl.pallas_call(
        body, name=name, grid=(R // br,), in_specs=[pl.BlockSpec((N_CHIPS, br, C), lambda i: (0, i, 0))],
        out_specs=pl.BlockSpec((br, C), lambda i: (i, 0)), out_shape=jax.ShapeDtypeStruct((R, C), F32),
        compiler_params=_cparams("parallel"),
    )(recv)


def _adamw_math(w, g, m, v):
    m = ADAM_B1 * m + (1.0 - ADAM_B1) * g
    v = ADAM_B2 * v + (1.0 - ADAM_B2) * (g * g)
    m_hat = m / (1.0 - ADAM_B1 ** ADAM_STEP)
    v_hat = v / (1.0 - ADAM_B2 ** ADAM_STEP)
    return -ADAM_LR * (m_hat / (jnp.sqrt(v_hat) + ADAM_EPS) + ADAM_WD * w), m, v


def _adamw(w, m, v, g_a, g_b, *, br, name):
    R, C = w.shape
    two = g_b is not None

    def body(*refs):
        if two:
            w_ref, m_ref, v_ref, ga_ref, gb_ref, g_ref, d_ref, nm_ref, nv_ref = refs
            g = ga_ref[...] + gb_ref[...]
        else:
            w_ref, m_ref, v_ref, ga_ref, g_ref, d_ref, nm_ref, nv_ref = refs
            g = ga_ref[...]
        g_ref[...] = g
        d_ref[...], nm_ref[...], nv_ref[...] = _adamw_math(w_ref[...], g, m_ref[...], v_ref[...])

    spec = pl.BlockSpec((br, C), lambda i: (i, 0))
    args = [w, m, v, g_a] + ([g_b] if two else [])
    return pl.pallas_call(
        body, name=name, grid=(R // br,), in_specs=[spec] * len(args), out_specs=[spec] * 4,
        out_shape=[jax.ShapeDtypeStruct((R, C), F32)] * 4,
        compiler_params=_cparams("parallel"),
    )(*args)


SMALL_ROWS = dict(fox_b_f=1, mla_g_q=2, mla_g_kv=1, swa_sinks=1, mix_g=8, ln1_g=8, ln1_b=8, ln2_g=8, ln2_b=8)
SMALL_ROWS_PER_LAYER = sum(SMALL_ROWS.values())


def _pack_small(vals, extra_rows):
    L = vals[SMALL[0]].shape[0]
    per_layer = []
    for name in SMALL:
        a = vals[name].astype(F32)
        a = jnp.pad(a, ((0, 0), (0, SMALL_ROWS[name] * 128 - a.shape[1])))
        per_layer.append(a.reshape(L, SMALL_ROWS[name], 128))
    out = jnp.concatenate(per_layer, axis=1).reshape(L * SMALL_ROWS_PER_LAYER, 128)
    return jnp.pad(out, ((0, extra_rows), (0, 0)))


def _unpack_small(block, shapes):
    L = shapes[SMALL[0]][0]
    body = block[:L * SMALL_ROWS_PER_LAYER].reshape(L, SMALL_ROWS_PER_LAYER, 128)
    out, r = {}, 0
    for name in SMALL:
        n = shapes[name][1]
        out[name] = body[:, r:r + SMALL_ROWS[name]].reshape(L, SMALL_ROWS[name] * 128)[:, :n]
        r += SMALL_ROWS[name]
    return out


def _to_chips(g, axis):
    L, a, b = g.shape
    if axis == 2:
        return g.reshape(L, a, N_CHIPS, b // N_CHIPS).transpose(2, 0, 1, 3)
    return g.reshape(L, N_CHIPS, a // N_CHIPS, b).transpose(1, 0, 2, 3)


def _from_chips(g, axis):
    _, L, a, b = g.shape
    if axis == 2:
        return g.transpose(1, 2, 0, 3).reshape(L, a, N_CHIPS * b)
    return g.transpose(1, 0, 2, 3).reshape(L, N_CHIPS * a, b)


def kernel(x, w_in, fox_b_f, mla_g_q, mla_g_kv, mla_w_uq, mla_w_ukv, swa_sinks, mix_g, w_o, ln1_g, ln1_b, w_gate, w_up, w_down, ln2_g, ln2_b, loss_target, m_w_in, m_fox_b_f, m_mla_g_q, m_mla_g_kv, m_mla_w_uq, m_mla_w_ukv, m_swa_sinks, m_mix_g, m_w_o, m_ln1_g, m_ln1_b, m_w_gate, m_w_up, m_w_down, m_ln2_g, m_ln2_b, v_w_in, v_fox_b_f, v_mla_g_q, v_mla_g_kv, v_mla_w_uq, v_mla_w_ukv, v_swa_sinks, v_mix_g, v_w_o, v_ln1_g, v_ln1_b, v_w_gate, v_w_up, v_w_down, v_ln2_g, v_ln2_b):
    w = dict(w_in=w_in, fox_b_f=fox_b_f, mla_g_q=mla_g_q, mla_g_kv=mla_g_kv, mla_w_uq=mla_w_uq, mla_w_ukv=mla_w_ukv,
             swa_sinks=swa_sinks, mix_g=mix_g, w_o=w_o, ln1_g=ln1_g, ln1_b=ln1_b, w_gate=w_gate, w_up=w_up,
             w_down=w_down, ln2_g=ln2_g, ln2_b=ln2_b)
    m = dict(w_in=m_w_in, fox_b_f=m_fox_b_f, mla_g_q=m_mla_g_q, mla_g_kv=m_mla_g_kv, mla_w_uq=m_mla_w_uq,
             mla_w_ukv=m_mla_w_ukv, swa_sinks=m_swa_sinks, mix_g=m_mix_g, w_o=m_w_o, ln1_g=m_ln1_g, ln1_b=m_ln1_b,
             w_gate=m_w_gate, w_up=m_w_up, w_down=m_w_down, ln2_g=m_ln2_g, ln2_b=m_ln2_b)
    v = dict(w_in=v_w_in, fox_b_f=v_fox_b_f, mla_g_q=v_mla_g_q, mla_g_kv=v_mla_g_kv, mla_w_uq=v_mla_w_uq,
             mla_w_ukv=v_mla_w_ukv, swa_sinks=v_swa_sinks, mix_g=v_mix_g, w_o=v_w_o, ln1_g=v_ln1_g, ln1_b=v_ln1_b,
             w_gate=v_w_gate, w_up=v_w_up, w_down=v_w_down, ln2_g=v_ln2_g, ln2_b=v_ln2_b)
    names = tuple(w)
    L = w_in.shape[0]
    S = x.shape[1]
    blk = min(256, S)
    bs = min(512, S)

    gathered = _chip_exchange([w[k].astype(MXU_DTYPE) for k in BIG], scatter=False, name="gather_weights")
    full = {k: _from_chips(g, SHARD_AXIS[k]) for k, g in zip(BIG, gathered)}
    tabs = _rope_tables(S)
    Ws = [_prep_weights(*[full[k][l] for k in BIG]) for l in range(L)]
    Ps = []
    for l in range(L):
        P = dict(fox_b_f=fox_b_f[l], swa_sinks=swa_sinks[l])
        for k in ("mla_g_q", "mla_g_kv", "mix_g", "ln1_g", "ln1_b", "ln2_g", "ln2_b"):
            P[k] = w[k][l][None, :]
        Ps.append(P)

    xa = x[0]
    xb = xa.astype(MXU_DTYPE)
    saved = []
    for l in range(L):
        xa, xb, sv = _layer_fwd(l, xa, xb, Ws[l], Ps[l], tabs, blk)
        saved.append(sv)
    dx, loss_part = _loss_head(xa, loss_target[0], bs=bs, name="loss_head")
    layer_grads = [None] * L
    for l in reversed(range(L)):
        dx, layer_grads[l] = _layer_bwd(l, dx, saved[l], Ws[l], Ps[l], tabs, blk)
    grad_x = dx[None]

    local = {k: jnp.stack([layer_grads[l][k] for l in range(L)]) for k in names}
    to_owner = [_to_chips(local[k], SHARD_AXIS[k]).astype(MXU_DTYPE) for k in BIG]
    received = _chip_exchange(to_owner, scatter=True, name="scatter_grads")
    partial = []
    for k, r in zip(BIG, received):
        _, _, a, b = r.shape
        partial.append(_sum_chips(r.reshape(N_CHIPS, L * a, b), br=_rows(L * a), name=f"sum_{k}"))
    sibling = _core_exchange(partial, name="swap_partials")
    out = {}
    for k, mine, theirs in zip(BIG, partial, sibling):
        shp = w[k].shape
        two_d = lambda t: t.reshape(shp[0] * shp[1], shp[2])
        res = _adamw(two_d(w[k]), two_d(m[k]), two_d(v[k]), mine, theirs, br=_rows(shp[0] * shp[1]), name=f"adamw_{k}")
        out[k] = [t.reshape(shp) for t in res]

    shapes = {k: w[k].shape for k in SMALL}
    extra = 8 + (-L * SMALL_ROWS_PER_LAYER) % 8
    block = _pack_small({k: local[k] for k in SMALL}, extra)
    block = block.at[L * SMALL_ROWS_PER_LAYER, 0].set(loss_part[0, 0])
    total = _all_sum_small(block, name="sum_small")
    loss = total[L * SMALL_ROWS_PER_LAYER, 0]
    res = _adamw(_pack_small({k: w[k] for k in SMALL}, extra), _pack_small({k: m[k] for k in SMALL}, extra),
                 _pack_small({k: v[k] for k in SMALL}, extra), total, None, br=total.shape[0], name="adamw_small")
    res = [_unpack_small(t, shapes) for t in res]
    for k in SMALL:
        out[k] = [r[k] for r in res]

    return (loss, grad_x, *[out[k][0] for k in names], *[out[k][1] for k in names],
            *[out[k][2] for k in names], *[out[k][3] for k in names])


def _rows(n):
    for b in (256, 128, 64, 32, 16, 8):
        if n % b == 0:
            return b
    return n
```

```python
import functools

import numpy as np
import jax
import jax.numpy as jnp
from jax import lax
from jax.experimental import pallas as pl
from jax.experimental.pallas import tpu as pltpu

F32 = jnp.float32
MXU_DTYPE = jnp.bfloat16
NEG_INF = -1e30

D_MODEL = 1024
DEPTH = 4
HEAD_DIM = 64
GROUP_WIDTH = 256
N_GROUPS = 4
D_FF = 2816
MLA_Q_RANK = 256
MLA_KV_RANK = 128
MLA_ROPE = 32
MLA_QK = 96
MLA_PAD = 128
ROPE_THETA = 10000.0
WINDOW = 128
ALPHA = (2.0 * DEPTH) ** 0.25
SWA_SLOPES = tuple(float(2.0 ** (-8.0 * h / 4)) for h in range(1, 5))
IN_WIDTH = 2468
ATT_W = 2048
LAT_W = 640
PERM_W = ATT_W + LAT_W
COL_FQ, COL_FK, COL_FV = 0, 256, 512
COL_SQ, COL_SK, COL_SV = 768, 1024, 1280
COL_WQ, COL_WK, COL_WV = 1536, 1792, 1920
Q_COLSCALE = np.ones((1, ATT_W), np.float32)
Q_COLSCALE[:, COL_FQ:COL_FQ + 256] = HEAD_DIM ** -0.5
Q_COLSCALE[:, COL_SQ:COL_SQ + 256] = HEAD_DIM ** -0.5

ADAM_LR, ADAM_B1, ADAM_B2, ADAM_EPS, ADAM_WD, ADAM_STEP = 0.001, 0.9, 0.999, 1e-08, 0.01, 10

VMEM_LIMIT = 56 * 1024 * 1024
NT = (((1,), (1,)), ((), ()))
TN = (((0,), (0,)), ((), ()))
MESH = pl.DeviceIdType.MESH


def _cparams(*sem):
    return pltpu.CompilerParams(dimension_semantics=sem, vmem_limit_bytes=VMEM_LIMIT)


def _split2(x):
    hi = x.astype(MXU_DTYPE)
    lo = (x - hi.astype(F32)).astype(MXU_DTYPE)
    return hi, lo


def _dot01(x, m01, dn=None, parts=2):
    acc = None
    rem = x
    for _ in range(parts):
        part = rem.astype(MXU_DTYPE)
        rem = rem - part.astype(F32)
        if dn is None:
            t = jnp.dot(part, m01, preferred_element_type=F32)
        else:
            t = lax.dot_general(part, m01, dn, preferred_element_type=F32)
        acc = t if acc is None else acc + t
    return acc


def _mm(a, b, *, name, ta=False, tb=False, out_dtype=F32, bm=512, bn=512, bk=512, resid=None, alpha=1.0,
        colscale=None):
    M, K = (a.shape[1], a.shape[0]) if ta else a.shape
    N = b.shape[0] if tb else b.shape[1]
    assert (b.shape[1] if tb else b.shape[0]) == K
    assert resid is None or colscale is None
    bm, bn, bk = min(bm, M), min(bn, N), min(bk, K)
    assert M % bm == 0 and N % bn == 0 and K % bk == 0, (name, M, N, K, bm, bn, bk)
    nk = K // bk
    dn = (((0 if ta else 1,), (1 if tb else 0,)), ((), ()))

    extra = resid is not None or colscale is not None

    def body(*refs):
        a_ref, b_ref = refs[:2]
        r_ref = refs[2] if extra else None
        o_ref = refs[3] if extra else refs[2]
        acc_ref = refs[-1] if nk > 1 else None
        k = pl.program_id(2)
        part = lax.dot_general(a_ref[...].astype(MXU_DTYPE), b_ref[...].astype(MXU_DTYPE), dn,
                               preferred_element_type=F32)

        def finish(r):
            if resid is not None:
                r = r + alpha * r_ref[...]
            if colscale is not None:
                r = r * r_ref[...]
            o_ref[...] = r.astype(o_ref.dtype)

        if nk == 1:
            finish(part)
        else:
            @pl.when(k == 0)
            def _():
                acc_ref[...] = part

            @pl.when((k > 0) & (k < nk - 1))
            def _():
                acc_ref[...] += part

            @pl.when(k == nk - 1)
            def _():
                finish(acc_ref[...] + part)

    a_spec = pl.BlockSpec((bk, bm), lambda i, j, k: (k, i)) if ta else pl.BlockSpec((bm, bk), lambda i, j, k: (i, k))
    b_spec = pl.BlockSpec((bn, bk), lambda i, j, k: (j, k)) if tb else pl.BlockSpec((bk, bn), lambda i, j, k: (k, j))
    in_specs = [a_spec, b_spec]
    args = [a, b]
    if resid is not None:
        in_specs.append(pl.BlockSpec((bm, bn), lambda i, j, k: (i, j)))
        args.append(resid)
    if colscale is not None:
        in_specs.append(pl.BlockSpec((1, bn), lambda i, j, k: (0, j)))
        args.append(colscale)
    return pl.pallas_call(
        body, name=name, grid=(M // bm, N // bn, nk), in_specs=in_specs,
        out_specs=pl.BlockSpec((bm, bn), lambda i, j, k: (i, j)),
        out_shape=jax.ShapeDtypeStruct((M, N), out_dtype),
        scratch_shapes=[pltpu.VMEM((bm, bn), F32)] if nk > 1 else [],
        compiler_params=_cparams("parallel", "parallel", "arbitrary"),
    )(*args)


def _softmax_attn_fwd(q_arr, k_arr, v_arr, *, qcb, kcb, vcb, dk, scale, cum_col=None, cum_row=None, blk, name):
    S = q_arr.shape[0]
    nb = S // blk
    bias = cum_col is not None
    W = 2 * dk

    def body(*refs):
        if bias:
            q_ref, k_ref, v_ref, cc_ref, cr_ref, o_ref, lse_ref = refs
        else:
            q_ref, k_ref, v_ref, o_ref, lse_ref = refs
        p = pl.program_id(0)
        i = pl.program_id(1)
        row = lax.broadcasted_iota(jnp.int32, (blk, blk), 0)
        col = lax.broadcasted_iota(jnp.int32, (blk, blk), 1)
        for hh in range(2):
            q = q_ref[:, hh * dk:(hh + 1) * dk]

            def tile(j, carry, masked, hh=hh, q=q):
                m, l, acc = carry
                r0 = pl.multiple_of(j * blk, blk)
                ks = k_ref[pl.ds(r0, blk), hh * dk:(hh + 1) * dk]
                vs = v_ref[pl.ds(r0, blk), hh * HEAD_DIM:(hh + 1) * HEAD_DIM]
                s = lax.dot_general(q, ks, NT, preferred_element_type=F32) * scale
                if bias:
                    s = s + cc_ref[hh] - cr_ref[(2 * p + hh) * nb + j]
                if masked:
                    s = jnp.where(col <= row, s, NEG_INF)
                mn = jnp.maximum(m, jnp.max(s, axis=1, keepdims=True))
                a = jnp.exp(m - mn)
                pe = jnp.exp(s - mn)
                l = a * l + jnp.sum(pe, axis=1, keepdims=True)
                acc = a * acc + jnp.dot(pe.astype(MXU_DTYPE), vs, preferred_element_type=F32)
                return mn, l, acc

            init = (jnp.full((blk, 1), NEG_INF, F32), jnp.zeros((blk, 1), F32), jnp.zeros((blk, HEAD_DIM), F32))
            carry = lax.fori_loop(0, i, functools.partial(tile, masked=False), init)
            m, l, acc = tile(i, carry, True)
            o_ref[:, hh * HEAD_DIM:(hh + 1) * HEAD_DIM] = acc / l
            lse_ref[hh] = m + jnp.log(l)

    in_specs = [pl.BlockSpec((blk, W), lambda p, i: (i, qcb + p)),
                pl.BlockSpec((S, W), lambda p, i: (0, kcb + p)),
                pl.BlockSpec((S, 128), lambda p, i: (0, vcb + p))]
    args = [q_arr, k_arr, v_arr]
    if bias:
        in_specs += [pl.BlockSpec((2, blk, 1), lambda p, i: (p, i, 0)),
                     pl.BlockSpec((4 * nb, 1, blk), lambda p, i: (0, 0, 0))]
        args += [cum_col, cum_row]
    return pl.pallas_call(
        body, name=name, grid=(2, nb), in_specs=in_specs,
        out_specs=[pl.BlockSpec((blk, 128), lambda p, i: (i, p)), pl.BlockSpec((2, blk, 1), lambda p, i: (p, i, 0))],
        out_shape=[jax.ShapeDtypeStruct((S, GROUP_WIDTH), F32), jax.ShapeDtypeStruct((4, S, 1), F32)],
        compiler_params=_cparams("arbitrary", "arbitrary"),
    )(*args)


def _softmax_attn_bwd(q_arr, k_arr, v_arr, dmix, o_arr, lse, *, qcb, kcb, vcb, dcb, dk, scale,
                      cum_col=None, cum_row=None, blk, name):
    S = q_arr.shape[0]
    nb = S // blk
    bias = cum_col is not None
    W = 2 * dk

    def body(*refs):
        if bias:
            q_ref, k_ref, v_ref, do_ref, o_ref, lse_ref, cc_ref, cr_ref, dq_ref, dk_ref, dv_ref, dc_ref, dcq_ref = refs
        else:
            q_ref, k_ref, v_ref, do_ref, o_ref, lse_ref, dq_ref, dk_ref, dv_ref = refs
        p = pl.program_id(0)
        i = pl.program_id(1)

        @pl.when(i == 0)
        def _():
            dk_ref[...] = jnp.zeros_like(dk_ref)
            dv_ref[...] = jnp.zeros_like(dv_ref)
            if bias:
                dc_ref[...] = jnp.zeros_like(dc_ref)

        row = lax.broadcasted_iota(jnp.int32, (blk, blk), 0)
        col = lax.broadcasted_iota(jnp.int32, (blk, blk), 1)
        for hh in range(2):
            q = q_ref[:, hh * dk:(hh + 1) * dk]
            do = do_ref[:, hh * HEAD_DIM:(hh + 1) * HEAD_DIM]
            delta = jnp.sum(do * o_ref[:, hh * HEAD_DIM:(hh + 1) * HEAD_DIM], axis=1, keepdims=True)
            dob = do.astype(MXU_DTYPE)
            lse_h = lse_ref[hh]

            def tile(j, carry, masked, hh=hh, q=q, dob=dob, delta=delta, lse_h=lse_h):
                dq, dcq = carry
                r0 = pl.multiple_of(j * blk, blk)
                ks = k_ref[pl.ds(r0, blk), hh * dk:(hh + 1) * dk]
                vs = v_ref[pl.ds(r0, blk), hh * HEAD_DIM:(hh + 1) * HEAD_DIM]
                s = lax.dot_general(q, ks, NT, preferred_element_type=F32) * scale
                if bias:
                    s = s + cc_ref[hh] - cr_ref[(2 * p + hh) * nb + j]
                if masked:
                    s = jnp.where(col <= row, s, NEG_INF)
                pr = jnp.exp(s - lse_h)
                dp = lax.dot_general(dob, vs, NT, preferred_element_type=F32)
                ds = pr * (dp - delta)
                dsb = ds.astype(MXU_DTYPE)
                dv_ref[pl.ds(r0, blk), hh * HEAD_DIM:(hh + 1) * HEAD_DIM] += lax.dot_general(
                    pr.astype(MXU_DTYPE), dob, TN, preferred_element_type=F32)
                dk_ref[pl.ds(r0, blk), hh * dk:(hh + 1) * dk] += lax.dot_general(
                    dsb, q, TN, preferred_element_type=F32) * scale
                if bias:
                    dc_ref[hh * nb + j] -= jnp.sum(ds, axis=0, keepdims=True)
                    dcq = dcq + jnp.sum(ds, axis=1, keepdims=True)
                return dq + jnp.dot(dsb, ks, preferred_element_type=F32) * scale, dcq

            carry = lax.fori_loop(0, i, functools.partial(tile, masked=False),
                                  (jnp.zeros((blk, dk), F32), jnp.zeros((blk, 1), F32)))
            dq, dcq = tile(i, carry, True)
            dq_ref[:, hh * dk:(hh + 1) * dk] = dq
            if bias:
                dcq_ref[hh] = dcq

    in_specs = [pl.BlockSpec((blk, W), lambda p, i: (i, qcb + p)),
                pl.BlockSpec((S, W), lambda p, i: (0, kcb + p)),
                pl.BlockSpec((S, 128), lambda p, i: (0, vcb + p)),
                pl.BlockSpec((blk, 128), lambda p, i: (i, dcb + p)),
                pl.BlockSpec((blk, 128), lambda p, i: (i, p)),
                pl.BlockSpec((2, blk, 1), lambda p, i: (p, i, 0))]
    args = [q_arr, k_arr, v_arr, dmix, o_arr, lse]
    out_specs = [pl.BlockSpec((blk, W), lambda p, i: (i, p)),
                 pl.BlockSpec((S, W), lambda p, i: (0, p)),
                 pl.BlockSpec((S, 128), lambda p, i: (0, p))]
    out_shape = [jax.ShapeDtypeStruct((S, 4 * dk), F32), jax.ShapeDtypeStruct((S, 4 * dk), F32),
                 jax.ShapeDtypeStruct((S, GROUP_WIDTH), F32)]
    if bias:
        in_specs += [pl.BlockSpec((2, blk, 1), lambda p, i: (p, i, 0)),
                     pl.BlockSpec((4 * nb, 1, blk), lambda p, i: (0, 0, 0))]
        args += [cum_col, cum_row]
        out_specs += [pl.BlockSpec((2 * nb, 1, blk), lambda p, i: (p, 0, 0)), pl.BlockSpec((2, blk, 1), lambda p, i: (p, i, 0))]
        out_shape += [jax.ShapeDtypeStruct((4 * nb, 1, blk), F32), jax.ShapeDtypeStruct((4, S, 1), F32)]
    return pl.pallas_call(
        body, name=name, grid=(2, nb), in_specs=in_specs, out_specs=out_specs, out_shape=out_shape,
        compiler_params=_cparams("arbitrary", "arbitrary"),
    )(*args)


def _sb_tile(q, ks, scale, strict_mask, carry_l, tri_excl):
    z = lax.dot_general(q, ks, NT, preferred_element_type=F32) * scale
    lb = -(jnp.maximum(z, 0.0) + jnp.log(1.0 + jnp.exp(-jnp.abs(z))))
    if strict_mask is not None:
        lb = jnp.where(strict_mask, lb, 0.0)
    between = _dot01(lb, tri_excl) + carry_l
    a = jnp.exp(z + lb + between)
    if strict_mask is not None:
        a = jnp.where(strict_mask, a, 0.0)
    return z, lb, a


def _sb_attn_fwd(h_att, *, blk, name):
    S = h_att.shape[0]
    nb = S // blk
    scale = HEAD_DIM ** -0.5
    qcb, kcb, vcb = COL_SQ // 128, COL_SK // 128, COL_SV // 128

    def body(q_ref, k_ref, v_ref, o_ref, lt_ref):
        i = pl.program_id(1)
        row = lax.broadcasted_iota(jnp.int32, (blk, blk), 0)
        col = lax.broadcasted_iota(jnp.int32, (blk, blk), 1)
        strict = col < row
        tri_excl = (row > col).astype(MXU_DTYPE)
        for hh in range(2):
            sl = slice(hh * HEAD_DIM, (hh + 1) * HEAD_DIM)
            q = q_ref[:, sl]

            def tile(j, carry, mask, sl=sl, q=q):
                cl, acc = carry
                r0 = pl.multiple_of(j * blk, blk)
                _, lb, a = _sb_tile(q, k_ref[pl.ds(r0, blk), sl], scale, mask, cl, tri_excl)
                acc = acc + jnp.dot(a.astype(MXU_DTYPE), v_ref[pl.ds(r0, blk), sl], preferred_element_type=F32)
                return cl + jnp.sum(lb, axis=1, keepdims=True), acc

            carry = tile(i, (jnp.zeros((blk, 1), F32), jnp.zeros((blk, HEAD_DIM), F32)), strict)
            cl, acc = lax.fori_loop(0, i, lambda jj, c: tile(i - 1 - jj, c, None), carry)
            o_ref[:, sl] = acc
            lt_ref[hh] = cl

    return pl.pallas_call(
        body, name=name, grid=(2, nb),
        in_specs=[pl.BlockSpec((blk, 128), lambda p, i: (i, qcb + p)),
                  pl.BlockSpec((S, 128), lambda p, i: (0, kcb + p)),
                  pl.BlockSpec((S, 128), lambda p, i: (0, vcb + p))],
        out_specs=[pl.BlockSpec((blk, 128), lambda p, i: (i, p)), pl.BlockSpec((2, blk, 1), lambda p, i: (p, i, 0))],
        out_shape=[jax.ShapeDtypeStruct((S, GROUP_WIDTH), F32), jax.ShapeDtypeStruct((4, S, 1), F32)],
        compiler_params=_cparams("arbitrary", "arbitrary"),
    )(h_att, h_att, h_att)


def _sb_attn_bwd(h_att, dmix, ltot_arr, *, dcb, blk, name):
    S = h_att.shape[0]
    nb = S // blk
    scale = HEAD_DIM ** -0.5
    qcb, kcb, vcb = COL_SQ // 128, COL_SK // 128, COL_SV // 128

    def body(q_ref, k_ref, v_ref, do_ref, lt_ref, dq_ref, dk_ref, dv_ref):
        i = pl.program_id(1)

        @pl.when(i == 0)
        def _():
            dk_ref[...] = jnp.zeros_like(dk_ref)
            dv_ref[...] = jnp.zeros_like(dv_ref)

        row = lax.broadcasted_iota(jnp.int32, (blk, blk), 0)
        col = lax.broadcasted_iota(jnp.int32, (blk, blk), 1)
        strict = col < row
        up_incl = (row <= col).astype(MXU_DTYPE)
        up_excl = (row < col).astype(MXU_DTYPE)
        for hh in range(2):
            sl = slice(hh * HEAD_DIM, (hh + 1) * HEAD_DIM)
            q = q_ref[:, sl]
            dob = do_ref[:, sl].astype(MXU_DTYPE)
            ltot = lt_ref[hh]

            def tile(j, carry, mask, sl=sl, q=q, dob=dob, ltot=ltot):
                cl, cg, dq = carry
                r0 = pl.multiple_of(j * blk, blk)
                ks = k_ref[pl.ds(r0, blk), sl]
                vs = v_ref[pl.ds(r0, blk), sl]
                z = lax.dot_general(q, ks, NT, preferred_element_type=F32) * scale
                lb = -(jnp.maximum(z, 0.0) + jnp.log(1.0 + jnp.exp(-jnp.abs(z))))
                if mask is not None:
                    lb = jnp.where(mask, lb, 0.0)
                between = ltot - cl - _dot01(lb, up_incl)
                a = jnp.exp(z + lb + between)
                if mask is not None:
                    a = jnp.where(mask, a, 0.0)
                g = lax.dot_general(dob, vs, NT, preferred_element_type=F32) * a
                e = cg + _dot01(g, up_excl)
                dz = g * jnp.exp(lb) - e * jnp.exp(z + lb)
                if mask is not None:
                    dz = jnp.where(mask, dz, 0.0)
                dzb = dz.astype(MXU_DTYPE)
                dv_ref[pl.ds(r0, blk), sl] += lax.dot_general(a.astype(MXU_DTYPE), dob, TN, preferred_element_type=F32)
                dk_ref[pl.ds(r0, blk), sl] += lax.dot_general(dzb, q, TN, preferred_element_type=F32) * scale
                dq = dq + jnp.dot(dzb, ks, preferred_element_type=F32) * scale
                return cl + jnp.sum(lb, axis=1, keepdims=True), cg + jnp.sum(g, axis=1, keepdims=True), dq

            zc = jnp.zeros((blk, 1), F32)
            carry = lax.fori_loop(0, i, lambda j, c: tile(j, c, None), (zc, zc, jnp.zeros((blk, HEAD_DIM), F32)))
            _, _, dq = tile(i, carry, strict)
            dq_ref[:, sl] = dq

    return pl.pallas_call(
        body, name=name, grid=(2, nb),
        in_specs=[pl.BlockSpec((blk, 128), lambda p, i: (i, qcb + p)),
                  pl.BlockSpec((S, 128), lambda p, i: (0, kcb + p)),
                  pl.BlockSpec((S, 128), lambda p, i: (0, vcb + p)),
                  pl.BlockSpec((blk, 128), lambda p, i: (i, dcb + p)),
                  pl.BlockSpec((2, blk, 1), lambda p, i: (p, i, 0))],
        out_specs=[pl.BlockSpec((blk, 128), lambda p, i: (i, p)),
                   pl.BlockSpec((S, 128), lambda p, i: (0, p)),
                   pl.BlockSpec((S, 128), lambda p, i: (0, p))],
        out_shape=[jax.ShapeDtypeStruct((S, GROUP_WIDTH), F32)] * 3,
        compiler_params=_cparams("arbitrary", "arbitrary"),
    )(h_att, h_att, h_att, dmix, ltot_arr)


HP = 4


def _kv_blocks_t(a, blk):
    S, C = a.shape
    return a.reshape(S // blk, blk, C).transpose(0, 2, 1)


def _smax_fwd_t(qT, k, vT3, *, dk, blk, name):
    S = k.shape[0]
    nb = S // blk
    H = k.shape[1] // dk

    def body(qT_ref, k_ref, vT_ref, oT_ref, lse_ref):
        i = pl.program_id(1)
        key = lax.broadcasted_iota(jnp.int32, (blk, blk), 0)
        qry = lax.broadcasted_iota(jnp.int32, (blk, blk), 1)
        qs = [qT_ref[h * dk:(h + 1) * dk, :] for h in range(HP)]

        def tile(j, carry, masked):
            r0 = pl.multiple_of(j * blk, blk)
            ss = [jnp.dot(k_ref[pl.ds(r0, blk), h * dk:(h + 1) * dk], qs[h], preferred_element_type=F32)
                  for h in range(HP)]
            stats, pes = [], []
            for h in range(HP):
                m, l, _ = carry[h]
                s = jnp.where(key <= qry, ss[h], NEG_INF) if masked else ss[h]
                mn = jnp.maximum(m, jnp.max(s, axis=0, keepdims=True))
                a = jnp.exp(m - mn)
                pe = jnp.exp(s - mn)
                stats.append((mn, a * l + jnp.sum(pe, axis=0, keepdims=True), a))
                pes.append(pe.astype(MXU_DTYPE))
            pvs = [jnp.dot(vT_ref[j, h * HEAD_DIM:(h + 1) * HEAD_DIM, :], pes[h], preferred_element_type=F32)
                   for h in range(HP)]
            return tuple((stats[h][0], stats[h][1], stats[h][2] * carry[h][2] + pvs[h]) for h in range(HP))

        init = tuple((jnp.full((1, blk), NEG_INF, F32), jnp.zeros((1, blk), F32), jnp.zeros((HEAD_DIM, blk), F32))
                     for _ in range(HP))
        carry = lax.fori_loop(0, i, functools.partial(tile, masked=False), init)
        carry = tile(i, carry, True)
        for h in range(HP):
            m, l, acc = carry[h]
            oT_ref[h * HEAD_DIM:(h + 1) * HEAD_DIM, :] = acc / l
            lse_ref[h, 0] = m + jnp.log(l)

    return pl.pallas_call(
        body, name=name, grid=(H // HP, nb),
        in_specs=[pl.BlockSpec((HP * dk, blk), lambda p, i: (p, i)),
                  pl.BlockSpec((S, HP * dk), lambda p, i: (0, p)),
                  pl.BlockSpec((nb, HP * HEAD_DIM, blk), lambda p, i: (0, p, 0))],
        out_specs=[pl.BlockSpec((HP * HEAD_DIM, blk), lambda p, i: (p, i)),
                   pl.BlockSpec((HP, 1, 1, blk), lambda p, i: (p, i, 0, 0))],
        out_shape=[jax.ShapeDtypeStruct((H * HEAD_DIM, S), F32), jax.ShapeDtypeStruct((H, nb, 1, blk), F32)],
        compiler_params=_cparams("arbitrary", "arbitrary"),
    )(qT, k, vT3)


def _smax_bwd_t(qT, q, k, kT3, v, dmix, dmixT, oT, lse, *, dk, dcb, qscale, blk, name):
    S = k.shape[0]
    nb = S // blk
    H = k.shape[1] // dk
    hd = HP * HEAD_DIM
    dcr = dcb * 128 // hd

    def body(qT_ref, q_ref, k_ref, kT_ref, v_ref, do_ref, doT_ref, oT_ref, lse_ref, dqT_ref, dk_ref, dv_ref):
        i = pl.program_id(1)

        @pl.when(i == 0)
        def _():
            dk_ref[...] = jnp.zeros_like(dk_ref)
            dv_ref[...] = jnp.zeros_like(dv_ref)

        key = lax.broadcasted_iota(jnp.int32, (blk, blk), 0)
        qry = lax.broadcasted_iota(jnp.int32, (blk, blk), 1)
        per_head = []
        for h in range(HP):
            hs = slice(h * HEAD_DIM, (h + 1) * HEAD_DIM)
            doT = doT_ref[hs, :]
            per_head.append(dict(
                qT=qT_ref[h * dk:(h + 1) * dk, :], q=q_ref[:, h * dk:(h + 1) * dk],
                doT=doT.astype(MXU_DTYPE), do=do_ref[:, hs].astype(MXU_DTYPE),
                delta=jnp.sum(doT * oT_ref[hs, :], axis=0, keepdims=True), lse=lse_ref[h, 0]))

        def tile(j, dqs, masked):
            r0 = pl.multiple_of(j * blk, blk)
            rows = pl.ds(r0, blk)
            ksl = [slice(h * dk, (h + 1) * dk) for h in range(HP)]
            hsl = [slice(h * HEAD_DIM, (h + 1) * HEAD_DIM) for h in range(HP)]
            ss = [jnp.dot(k_ref[rows, ksl[h]], per_head[h]["qT"], preferred_element_type=F32) for h in range(HP)]
            dps = [jnp.dot(v_ref[rows, hsl[h]], per_head[h]["doT"], preferred_element_type=F32) for h in range(HP)]
            prs, dss = [], []
            for h in range(HP):
                c = per_head[h]
                s = jnp.where(key <= qry, ss[h], NEG_INF) if masked else ss[h]
                pr = jnp.exp(s - c["lse"])
                dss.append((pr * (dps[h] - c["delta"])).astype(MXU_DTYPE))
                prs.append(pr.astype(MXU_DTYPE))
            for h in range(HP):
                dv_ref[rows, hsl[h]] += jnp.dot(prs[h], per_head[h]["do"], preferred_element_type=F32)
            for h in range(HP):
                dk_ref[rows, ksl[h]] += jnp.dot(dss[h], per_head[h]["q"], preferred_element_type=F32)
            return tuple(dqs[h] + jnp.dot(kT_ref[j, ksl[h], :], dss[h], preferred_element_type=F32) for h in range(HP))

        dqs = lax.fori_loop(0, i, functools.partial(tile, masked=False),
                            tuple(jnp.zeros((dk, blk), F32) for _ in range(HP)))
        dqs = tile(i, dqs, True)
        for h in range(HP):
            dqT_ref[h * dk:(h + 1) * dk, :] = dqs[h] * qscale

    return pl.pallas_call(
        body, name=name, grid=(H // HP, nb),
        in_specs=[pl.BlockSpec((HP * dk, blk), lambda p, i: (p, i)),
                  pl.BlockSpec((blk, HP * dk), lambda p, i: (i, p)),
                  pl.BlockSpec((S, HP * dk), lambda p, i: (0, p)),
                  pl.BlockSpec((nb, HP * dk, blk), lambda p, i: (0, p, 0)),
                  pl.BlockSpec((S, hd), lambda p, i: (0, p)),
                  pl.BlockSpec((blk, hd), lambda p, i: (i, dcr + p)),
                  pl.BlockSpec((hd, blk), lambda p, i: (dcr + p, i)),
                  pl.BlockSpec((hd, blk), lambda p, i: (p, i)),
                  pl.BlockSpec((HP, 1, 1, blk), lambda p, i: (p, i, 0, 0))],
        out_specs=[pl.BlockSpec((HP * dk, blk), lambda p, i: (p, i)),
                   pl.BlockSpec((S, HP * dk), lambda p, i: (0, p)),
                   pl.BlockSpec((S, hd), lambda p, i: (0, p))],
        out_shape=[jax.ShapeDtypeStruct((H * dk, S), F32), jax.ShapeDtypeStruct((S, H * dk), F32),
                   jax.ShapeDtypeStruct((S, H * HEAD_DIM), F32)],
        compiler_params=_cparams("arbitrary", "arbitrary"),
    )(qT, q, k, kT3, v, dmix, dmixT, oT, lse)


def _log1m_beta(z):
    return -(jnp.maximum(z, 0.0) + jnp.log(1.0 + jnp.exp(-jnp.abs(z))))


def _dot01_left(m01, x, parts=2):
    acc = None
    rem = x
    for _ in range(parts):
        part = rem.astype(MXU_DTYPE)
        rem = rem - part.astype(F32)
        t = jnp.dot(m01, part, preferred_element_type=F32)
        acc = t if acc is None else acc + t
    return acc


def _sb_fwd_t(qT, h_att, vT3, *, blk, name):
    S = h_att.shape[0]
    nb = S // blk
    kcb = COL_SK // (HP * HEAD_DIM)

    def body(qT_ref, k_ref, vT_ref, oT_ref, lt_ref):
        i = pl.program_id(1)
        key = lax.broadcasted_iota(jnp.int32, (blk, blk), 0)
        qry = lax.broadcasted_iota(jnp.int32, (blk, blk), 1)
        strict = key < qry
        later = (qry > key).astype(MXU_DTYPE)
        qs = [qT_ref[h * HEAD_DIM:(h + 1) * HEAD_DIM, :] for h in range(HP)]

        def tile(j, carry, mask):
            r0 = pl.multiple_of(j * blk, blk)
            hsl = [slice(h * HEAD_DIM, (h + 1) * HEAD_DIM) for h in range(HP)]
            zs = [jnp.dot(k_ref[pl.ds(r0, blk), hsl[h]], qs[h], preferred_element_type=F32) for h in range(HP)]
            lbs = []
            for h in range(HP):
                lb = _log1m_beta(zs[h])
                lbs.append(lb if mask is None else jnp.where(mask, lb, 0.0))
            sums = [_dot01_left(later, lbs[h]) for h in range(HP)]
            probs = []
            for h in range(HP):
                a = jnp.exp(zs[h] + lbs[h] + sums[h] + carry[h][0])
                probs.append((a if mask is None else jnp.where(mask, a, 0.0)).astype(MXU_DTYPE))
            pvs = [jnp.dot(vT_ref[j, hsl[h], :], probs[h], preferred_element_type=F32) for h in range(HP)]
            return tuple((carry[h][0] + jnp.sum(lbs[h], axis=0, keepdims=True), carry[h][1] + pvs[h]) for h in range(HP))

        init = tuple((jnp.zeros((1, blk), F32), jnp.zeros((HEAD_DIM, blk), F32)) for _ in range(HP))
        carry = tile(i, init, strict)
        carry = lax.fori_loop(0, i, lambda jj, c: tile(i - 1 - jj, c, None), carry)
        for h in range(HP):
            oT_ref[h * HEAD_DIM:(h + 1) * HEAD_DIM, :] = carry[h][1]
            lt_ref[h, 0] = carry[h][0]

    hd = HP * HEAD_DIM
    return pl.pallas_call(
        body, name=name, grid=(4 // HP, nb),
        in_specs=[pl.BlockSpec((hd, blk), lambda p, i: (p, i)),
                  pl.BlockSpec((S, hd), lambda p, i: (0, kcb + p)),
                  pl.BlockSpec((nb, hd, blk), lambda p, i: (0, p, 0))],
        out_specs=[pl.BlockSpec((hd, blk), lambda p, i: (p, i)), pl.BlockSpec((HP, 1, 1, blk), lambda p, i: (p, i, 0, 0))],
        out_shape=[jax.ShapeDtypeStruct((GROUP_WIDTH, S), F32), jax.ShapeDtypeStruct((4, nb, 1, blk), F32)],
        compiler_params=_cparams("arbitrary", "arbitrary"),
    )(qT, h_att, vT3)


def _sb_bwd_t(qT, h_att, kT3, dmix, dmixT, ltot, *, dcb, qscale, blk, name):
    S = h_att.shape[0]
    nb = S // blk
    hd = HP * HEAD_DIM
    qcb, kcb, vcb = COL_SQ // hd, COL_SK // hd, COL_SV // hd
    dcr = dcb * 128 // hd

    def body(qT_ref, q_ref, k_ref, kT_ref, v_ref, do_ref, doT_ref, lt_ref, dqT_ref, dk_ref, dv_ref):
        i = pl.program_id(1)

        @pl.when(i == 0)
        def _():
            dk_ref[...] = jnp.zeros_like(dk_ref)
            dv_ref[...] = jnp.zeros_like(dv_ref)

        key = lax.broadcasted_iota(jnp.int32, (blk, blk), 0)
        qry = lax.broadcasted_iota(jnp.int32, (blk, blk), 1)
        strict = key < qry
        upto = (qry <= key).astype(MXU_DTYPE)
        before = (qry < key).astype(MXU_DTYPE)
        per_head = []
        for h in range(HP):
            hs = slice(h * HEAD_DIM, (h + 1) * HEAD_DIM)
            per_head.append(dict(qT=qT_ref[hs, :], q=q_ref[:, hs], doT=doT_ref[hs, :].astype(MXU_DTYPE),
                                 do=do_ref[:, hs].astype(MXU_DTYPE), lt=lt_ref[h, 0]))

        def tile(j, carry, mask):
            r0 = pl.multiple_of(j * blk, blk)
            rows = pl.ds(r0, blk)
            hsl = [slice(h * HEAD_DIM, (h + 1) * HEAD_DIM) for h in range(HP)]
            zs = [jnp.dot(k_ref[rows, hsl[h]], per_head[h]["qT"], preferred_element_type=F32) for h in range(HP)]
            das = [jnp.dot(v_ref[rows, hsl[h]], per_head[h]["doT"], preferred_element_type=F32) for h in range(HP)]
            lbs = []
            for h in range(HP):
                lb = _log1m_beta(zs[h])
                lbs.append(lb if mask is None else jnp.where(mask, lb, 0.0))
            sums = [_dot01_left(upto, lbs[h]) for h in range(HP)]
            probs, gs = [], []
            for h in range(HP):
                a = jnp.exp(zs[h] + lbs[h] + (per_head[h]["lt"] - carry[h][0] - sums[h]))
                a = a if mask is None else jnp.where(mask, a, 0.0)
                gs.append(das[h] * a)
                probs.append(a.astype(MXU_DTYPE))
            for h in range(HP):
                dv_ref[rows, hsl[h]] += jnp.dot(probs[h], per_head[h]["do"], preferred_element_type=F32)
            es = [_dot01_left(before, gs[h]) for h in range(HP)]
            dzs = []
            for h in range(HP):
                dz = gs[h] * jnp.exp(lbs[h]) - (carry[h][1] + es[h]) * jnp.exp(zs[h] + lbs[h])
                dzs.append((dz if mask is None else jnp.where(mask, dz, 0.0)).astype(MXU_DTYPE))
            for h in range(HP):
                dk_ref[rows, hsl[h]] += jnp.dot(dzs[h], per_head[h]["q"], preferred_element_type=F32)
            return tuple((carry[h][0] + jnp.sum(lbs[h], axis=0, keepdims=True),
                          carry[h][1] + jnp.sum(gs[h], axis=0, keepdims=True),
                          carry[h][2] + jnp.dot(kT_ref[j, hsl[h], :], dzs[h], preferred_element_type=F32))
                         for h in range(HP))

        zr = jnp.zeros((1, blk), F32)
        init = tuple((zr, zr, jnp.zeros((HEAD_DIM, blk), F32)) for _ in range(HP))
        carry = lax.fori_loop(0, i, lambda j, c: tile(j, c, None), init)
        carry = tile(i, carry, strict)
        for h in range(HP):
            dqT_ref[h * HEAD_DIM:(h + 1) * HEAD_DIM, :] = carry[h][2] * qscale

    return pl.pallas_call(
        body, name=name, grid=(4 // HP, nb),
        in_specs=[pl.BlockSpec((hd, blk), lambda p, i: (p, i)),
                  pl.BlockSpec((blk, hd), lambda p, i: (i, qcb + p)),
                  pl.BlockSpec((S, hd), lambda p, i: (0, kcb + p)),
                  pl.BlockSpec((nb, hd, blk), lambda p, i: (0, p, 0)),
                  pl.BlockSpec((S, hd), lambda p, i: (0, vcb + p)),
                  pl.BlockSpec((blk, hd), lambda p, i: (i, dcr + p)),
                  pl.BlockSpec((hd, blk), lambda p, i: (dcr + p, i)),
                  pl.BlockSpec((HP, 1, 1, blk), lambda p, i: (p, i, 0, 0))],
        out_specs=[pl.BlockSpec((hd, blk), lambda p, i: (p, i)),
                   pl.BlockSpec((S, hd), lambda p, i: (0, p)),
                   pl.BlockSpec((S, hd), lambda p, i: (0, p))],
        out_shape=[jax.ShapeDtypeStruct((GROUP_WIDTH, S), F32), jax.ShapeDtypeStruct((S, GROUP_WIDTH), F32),
                   jax.ShapeDtypeStruct((S, GROUP_WIDTH), F32)],
        compiler_params=_cparams("arbitrary", "arbitrary"),
    )(qT, h_att, h_att, kT3, h_att, dmix, dmixT, ltot)


def _swa_scores(q_ref, k_ref, n, h, start):
    g = h // 2
    kb = k_ref[pl.ds(start, 2 * WINDOW), g * HEAD_DIM:(g + 1) * HEAD_DIM]
    s = lax.dot_general(q_ref[:, h * HEAD_DIM:(h + 1) * HEAD_DIM], kb, NT, preferred_element_type=F32) * (HEAD_DIM ** -0.5)
    dist = (n * WINDOW + lax.broadcasted_iota(jnp.int32, (WINDOW, 2 * WINDOW), 0)
            - start - lax.broadcasted_iota(jnp.int32, (WINDOW, 2 * WINDOW), 1))
    s = s - SWA_SLOPES[h] * dist.astype(F32)
    valid = (dist >= 0) & (dist < WINDOW)
    return jnp.where(valid, s, NEG_INF), kb


def _swa_fwd(h_att, sinks, *, name):
    S = h_att.shape[0]
    nb = S // WINDOW
    qcb, kcb, vcb = COL_WQ // 256, COL_WK // 128, COL_WV // 128

    def body(sink_ref, q_ref, k_ref, v_ref, o_ref, lse_ref):
        n = pl.program_id(0)
        start = pl.multiple_of(jnp.maximum(n - 1, 0) * WINDOW, WINDOW)
        for h in range(4):
            g = h // 2
            s, _ = _swa_scores(q_ref, k_ref, n, h, start)
            sink = sink_ref[h]
            m = jnp.maximum(jnp.max(s, axis=1, keepdims=True), sink)
            e = jnp.exp(s - m)
            den = jnp.sum(e, axis=1, keepdims=True) + jnp.exp(sink - m)
            vb = v_ref[pl.ds(start, 2 * WINDOW), g * HEAD_DIM:(g + 1) * HEAD_DIM]
            o_ref[:, h * HEAD_DIM:(h + 1) * HEAD_DIM] = jnp.dot((e / den).astype(MXU_DTYPE), vb, preferred_element_type=F32)
            lse_ref[h] = m + jnp.log(den)

    return pl.pallas_call(
        body, name=name, grid=(nb,),
        in_specs=[pl.BlockSpec(memory_space=pltpu.SMEM),
                  pl.BlockSpec((WINDOW, 256), lambda n: (n, qcb)),
                  pl.BlockSpec((S, 128), lambda n: (0, kcb)),
                  pl.BlockSpec((S, 128), lambda n: (0, vcb))],
        out_specs=[pl.BlockSpec((WINDOW, 256), lambda n: (n, 0)), pl.BlockSpec((4, WINDOW, 1), lambda n: (0, n, 0))],
        out_shape=[jax.ShapeDtypeStruct((S, GROUP_WIDTH), F32), jax.ShapeDtypeStruct((4, S, 1), F32)],
        compiler_params=_cparams("arbitrary"),
    )(sinks, h_att, h_att, h_att)


def _swa_bwd(h_att, sinks, dmix, o_arr, lse, *, dcb, name):
    S = h_att.shape[0]
    nb = S // WINDOW
    qcb, kcb, vcb = COL_WQ // 256, COL_WK // 128, COL_WV // 128

    def body(sink_ref, q_ref, k_ref, v_ref, do_ref, o_ref, lse_ref, dq_ref, dk_ref, dv_ref, dsink_ref):
        n = pl.program_id(0)

        @pl.when(n == 0)
        def _():
            dk_ref[...] = jnp.zeros_like(dk_ref)
            dv_ref[...] = jnp.zeros_like(dv_ref)
            dsink_ref[...] = jnp.zeros_like(dsink_ref)

        start = pl.multiple_of(jnp.maximum(n - 1, 0) * WINDOW, WINDOW)
        for h in range(4):
            g = h // 2
            sl = slice(h * HEAD_DIM, (h + 1) * HEAD_DIM)
            gl = slice(g * HEAD_DIM, (g + 1) * HEAD_DIM)
            s, kb = _swa_scores(q_ref, k_ref, n, h, start)
            lse_h = lse_ref[h]
            pr = jnp.exp(s - lse_h)
            do = do_ref[:, sl]
            dob = do.astype(MXU_DTYPE)
            delta = jnp.sum(do * o_ref[:, sl], axis=1, keepdims=True)
            vb = v_ref[pl.ds(start, 2 * WINDOW), gl]
            ds = pr * (lax.dot_general(dob, vb, NT, preferred_element_type=F32) - delta)
            dsb = ds.astype(MXU_DTYPE)
            dq_ref[:, sl] = jnp.dot(dsb, kb, preferred_element_type=F32) * (HEAD_DIM ** -0.5)
            dk_ref[pl.ds(start, 2 * WINDOW), gl] += lax.dot_general(
                dsb, q_ref[:, sl], TN, preferred_element_type=F32) * (HEAD_DIM ** -0.5)
            dv_ref[pl.ds(start, 2 * WINDOW), gl] += lax.dot_general(pr.astype(MXU_DTYPE), dob, TN, preferred_element_type=F32)
            dsink_ref[h:h + 1, :] += jnp.zeros((1, 128), F32) - jnp.sum(jnp.exp(sink_ref[h] - lse_h) * delta)

    return pl.pallas_call(
        body, name=name, grid=(nb,),
        in_specs=[pl.BlockSpec(memory_space=pltpu.SMEM),
                  pl.BlockSpec((WINDOW, 256), lambda n: (n, qcb)),
                  pl.BlockSpec((S, 128), lambda n: (0, kcb)),
                  pl.BlockSpec((S, 128), lambda n: (0, vcb)),
                  pl.BlockSpec((WINDOW, 256), lambda n: (n, dcb)),
                  pl.BlockSpec((WINDOW, 256), lambda n: (n, 0)),
                  pl.BlockSpec((4, WINDOW, 1), lambda n: (0, n, 0))],
        out_specs=[pl.BlockSpec((WINDOW, 256), lambda n: (n, 0)),
                   pl.BlockSpec((S, 128), lambda n: (0, 0)),
                   pl.BlockSpec((S, 128), lambda n: (0, 0)),
                   pl.BlockSpec((4, 128), lambda n: (0, 0))],
        out_shape=[jax.ShapeDtypeStruct((S, GROUP_WIDTH), F32), jax.ShapeDtypeStruct((S, 128), F32),
                   jax.ShapeDtypeStruct((S, 128), F32), jax.ShapeDtypeStruct((4, 128), F32)],
        compiler_params=_cparams("arbitrary"),
    )(sinks, h_att, h_att, h_att, dmix, o_arr, lse)


def _tri(n, incl, upper):
    r = lax.broadcasted_iota(jnp.int32, (n, n), 0)
    c = lax.broadcasted_iota(jnp.int32, (n, n), 1)
    if upper:
        m = (r <= c) if incl else (r < c)
    else:
        m = (r >= c) if incl else (r > c)
    return m.astype(MXU_DTYPE)


def _fox_gate_fwd(fg, b_f, *, name):
    _, R, _ = fg.shape

    def body(b_ref, fg_ref, pos_ref, neg_ref):
        up_incl = _tri(128, True, True)
        ones = jnp.ones((128, 128), MXU_DTYPE)
        for h in range(4):
            z = fg_ref[h] + b_ref[h]
            logf = jnp.minimum(z, 0.0) - jnp.log(1.0 + jnp.exp(-jnp.abs(z)))
            within = _dot01(logf, up_incl, parts=3)
            totals = _dot01(logf, ones, parts=3)
            rem = within + _rows_other(totals, R, after=False)
            for part in range(3):
                piece = rem.astype(MXU_DTYPE)
                rem = rem - piece.astype(F32)
                pos_ref[h, part] = piece
                neg_ref[h, part] = -piece

    shape = (4, 3) + fg.shape[1:]
    return pl.pallas_call(
        body, name=name,
        in_specs=[pl.BlockSpec(memory_space=pltpu.SMEM), pl.BlockSpec(memory_space=pltpu.VMEM)],
        out_specs=[pl.BlockSpec(memory_space=pltpu.VMEM)] * 2,
        out_shape=[jax.ShapeDtypeStruct(shape, MXU_DTYPE)] * 2,
    )(b_f, fg)


def _rows_other(totals, n, after):
    r = lax.broadcasted_iota(jnp.int32, (n, n), 0)
    c = lax.broadcasted_iota(jnp.int32, (n, n), 1)
    m = ((c > r) if after else (c < r)).astype(MXU_DTYPE)
    acc = None
    rem = totals
    for _ in range(3):
        part = rem.astype(MXU_DTYPE)
        rem = rem - part.astype(F32)
        t = jnp.dot(m, part, preferred_element_type=F32)
        acc = t if acc is None else acc + t
    return acc


def _fox_gate_bwd(fg, b_f, dcum_k, dcum_q, *, q_unscale, name):
    _, R, _ = fg.shape

    def body(b_ref, fg_ref, dck_ref, dcq_ref, dfg_ref, db_ref):
        low_incl = _tri(128, True, False)
        ones = jnp.ones((128, 128), MXU_DTYPE)
        for h in range(4):
            dc = dcq_ref[h] * q_unscale - dck_ref[h]
            dlogf = _dot01(dc, low_incl, parts=3) + _rows_other(_dot01(dc, ones, parts=3), R, after=True)
            z = fg_ref[h] + b_ref[h]
            dz = dlogf * jnp.exp(jnp.minimum(-z, 0.0) - jnp.log(1.0 + jnp.exp(-jnp.abs(z))))
            dfg_ref[h] = dz
            db_ref[h:h + 1, :] = jnp.zeros((1, 128), F32) + jnp.sum(dz)

    return pl.pallas_call(
        body, name=name,
        in_specs=[pl.BlockSpec(memory_space=pltpu.SMEM)] + [pl.BlockSpec(memory_space=pltpu.VMEM)] * 3,
        out_specs=[pl.BlockSpec(memory_space=pltpu.VMEM), pl.BlockSpec(memory_space=pltpu.VMEM)],
        out_shape=[jax.ShapeDtypeStruct(fg.shape, F32), jax.ShapeDtypeStruct((4, 128), F32)],
    )(b_f, fg, dcum_k, dcum_q)


def _rope_rot(transpose):
    r = lax.broadcasted_iota(jnp.int32, (MLA_PAD, MLA_PAD), 0)
    c = lax.broadcasted_iota(jnp.int32, (MLA_PAD, MLA_PAD), 1)
    if transpose:
        r, c = c, r
    half = MLA_ROPE // 2
    lo, mid, hi = HEAD_DIM, HEAD_DIM + half, HEAD_DIM + MLA_ROPE
    minus = (c >= lo) & (c < mid) & (r == c + half)
    plus = (c >= mid) & (c < hi) & (r == c - half)
    return jnp.where(plus, 1.0, jnp.where(minus, -1.0, 0.0)).astype(MXU_DTYPE)


def _rope_lanes():
    lane = lax.broadcasted_iota(jnp.int32, (1, MLA_PAD), 1)
    return ((lane >= HEAD_DIM) & (lane < HEAD_DIM + MLA_ROPE)).astype(F32)


def _rms(x, g, eps=1e-6):
    r = lax.rsqrt(jnp.mean(x * x, axis=-1, keepdims=True) + eps)
    return x * r * g, r


def _rms_bwd(dy, x, r, g):
    xh = x * r
    dxh = dy * g
    dx = r * (dxh - xh * jnp.mean(dxh * xh, axis=-1, keepdims=True))
    return dx, dy * xh


def _mla_prep_fwd(lat, g_q, g_kv, wuq, wuk, wuv, cosm, sinm, *, bs, name):
    S = lat.shape[0]

    def body(lat_ref, gq_ref, gkv_ref, wuq_ref, wuk_ref, wuv_ref, cos_ref, sin_ref, q_ref, k_ref, v_ref):
        rot = _rope_rot(False)
        cosm_, sinm_ = cos_ref[...], sin_ref[...]
        nq, _ = _rms(lat_ref[:, 0:MLA_Q_RANK], gq_ref[...])
        nkv, _ = _rms(lat_ref[:, MLA_Q_RANK:MLA_Q_RANK + MLA_KV_RANK], gkv_ref[...])
        qlat = jnp.dot(nq.astype(MXU_DTYPE), wuq_ref[...], preferred_element_type=F32)
        klat = jnp.dot(nkv.astype(MXU_DTYPE), wuk_ref[...], preferred_element_type=F32)
        v_ref[...] = jnp.dot(nkv.astype(MXU_DTYPE), wuv_ref[...], preferred_element_type=F32).astype(v_ref.dtype)
        krb = lat_ref[:, 384:512]
        kr = krb * (cosm_ * _rope_lanes()) + _dot01(krb, rot, parts=3) * sinm_
        for h in range(4):
            sl = slice(h * MLA_PAD, (h + 1) * MLA_PAD)
            qh = qlat[:, sl]
            q_ref[:, sl] = ((qh * cosm_ + _dot01(qh, rot, parts=3) * sinm_) * (MLA_QK ** -0.5)).astype(q_ref.dtype)
            k_ref[:, sl] = (klat[:, sl] + kr).astype(k_ref.dtype)

    full = lambda a: pl.BlockSpec(a.shape, lambda i: (0,) * a.ndim)
    return pl.pallas_call(
        body, name=name, grid=(S // bs,),
        in_specs=[pl.BlockSpec((bs, LAT_W), lambda i: (i, 0)), full(g_q), full(g_kv), full(wuq), full(wuk), full(wuv),
                  pl.BlockSpec((bs, MLA_PAD), lambda i: (i, 0)), pl.BlockSpec((bs, MLA_PAD), lambda i: (i, 0))],
        out_specs=[pl.BlockSpec((bs, 512), lambda i: (i, 0)), pl.BlockSpec((bs, 512), lambda i: (i, 0)),
                   pl.BlockSpec((bs, 256), lambda i: (i, 0))],
        out_shape=[jax.ShapeDtypeStruct((S, 512), MXU_DTYPE), jax.ShapeDtypeStruct((S, 512), MXU_DTYPE),
                   jax.ShapeDtypeStruct((S, 256), MXU_DTYPE)],
        compiler_params=_cparams("parallel"),
    )(lat, g_q, g_kv, wuq, wuk, wuv, cosm, sinm)


def _mla_prep_bwd(lat, g_q, g_kv, wuq, wuk, wuv, cosm, sinm, dq, dk, dv, *, bs, name):
    S = lat.shape[0]

    def body(lat_ref, gq_ref, gkv_ref, wuq_ref, wuk_ref, wuv_ref, cos_ref, sin_ref, dq_ref, dk_ref, dv_ref,
             dlat_ref, dwuq_ref, dwuk_ref, dwuv_ref, dgq_ref, dgkv_ref):
        @pl.when(pl.program_id(0) == 0)
        def _():
            for r in (dwuq_ref, dwuk_ref, dwuv_ref, dgq_ref, dgkv_ref):
                r[...] = jnp.zeros_like(r)

        rot_t = _rope_rot(True)
        cosm_, sinm_ = cos_ref[...], sin_ref[...]
        cq = lat_ref[:, 0:MLA_Q_RANK]
        ckv = lat_ref[:, MLA_Q_RANK:MLA_Q_RANK + MLA_KV_RANK]
        nq, rq = _rms(cq, gq_ref[...])
        nkv, rkv = _rms(ckv, gkv_ref[...])
        nqb, nkvb = nq.astype(MXU_DTYPE), nkv.astype(MXU_DTYPE)

        dqlat = []
        dkr = jnp.zeros((bs, MLA_PAD), F32)
        for h in range(4):
            sl = slice(h * MLA_PAD, (h + 1) * MLA_PAD)
            dqh = dq_ref[:, sl]
            dqlat.append(dqh * cosm_ + _dot01(dqh * sinm_, rot_t, parts=3))
            dkr = dkr + dk_ref[:, sl]
        dqlat = jnp.concatenate(dqlat, axis=1).astype(MXU_DTYPE)
        dkb = dk_ref[...].astype(MXU_DTYPE)
        dvb = dv_ref[...].astype(MXU_DTYPE)

        dnq = lax.dot_general(dqlat, wuq_ref[...], NT, preferred_element_type=F32)
        dnkv = (lax.dot_general(dkb, wuk_ref[...], NT, preferred_element_type=F32)
                + lax.dot_general(dvb, wuv_ref[...], NT, preferred_element_type=F32))
        dwuq_ref[...] += lax.dot_general(nqb, dqlat, TN, preferred_element_type=F32)
        dwuk_ref[...] += lax.dot_general(nkvb, dkb, TN, preferred_element_type=F32)
        dwuv_ref[...] += lax.dot_general(nkvb, dvb, TN, preferred_element_type=F32)
        dcq, tq = _rms_bwd(dnq, cq, rq, gq_ref[...])
        dckv, tkv = _rms_bwd(dnkv, ckv, rkv, gkv_ref[...])
        dgq_ref[...] += jnp.sum(tq, axis=0, keepdims=True)
        dgkv_ref[...] += jnp.sum(tkv, axis=0, keepdims=True)
        dlat_ref[:, 0:MLA_Q_RANK] = dcq.astype(dlat_ref.dtype)
        dlat_ref[:, MLA_Q_RANK:MLA_Q_RANK + MLA_KV_RANK] = dckv.astype(dlat_ref.dtype)
        dkrb = dkr * (cosm_ * _rope_lanes()) + _dot01(dkr * sinm_, rot_t, parts=3)
        dlat_ref[:, 384:512] = dkrb.astype(dlat_ref.dtype)

    full = lambda a: pl.BlockSpec(a.shape, lambda i: (0,) * a.ndim)
    row = lambda w: pl.BlockSpec((bs, w), lambda i: (i, 0))
    acc = lambda *shape: pl.BlockSpec(shape, lambda i: (0,) * len(shape))
    return pl.pallas_call(
        body, name=name, grid=(S // bs,),
        in_specs=[row(LAT_W), full(g_q), full(g_kv), full(wuq), full(wuk), full(wuv), row(MLA_PAD), row(MLA_PAD),
                  row(512), row(512), row(256)],
        out_specs=[row(512), acc(256, 512), acc(128, 512), acc(128, 256), acc(1, 256), acc(1, 128)],
        out_shape=[jax.ShapeDtypeStruct((S, 512), MXU_DTYPE), jax.ShapeDtypeStruct((256, 512), F32),
                   jax.ShapeDtypeStruct((128, 512), F32), jax.ShapeDtypeStruct((128, 256), F32),
                   jax.ShapeDtypeStruct((1, 256), F32), jax.ShapeDtypeStruct((1, 128), F32)],
        compiler_params=_cparams("arbitrary"),
    )(lat, g_q, g_kv, wuq, wuk, wuv, cosm, sinm, dq, dk, dv)


def _row_spec(bs, w):
    return pl.BlockSpec((bs, w), lambda i: (i, 0))


def _vec_spec(w):
    return pl.BlockSpec((1, w), lambda i: (0, 0))


def _gnorm_fwd(outs, g, *, bs, name):
    S = outs[0].shape[0]

    def body(a_ref, b_ref, c_ref, d_ref, g_ref, o_ref):
        for k, ref in enumerate((a_ref, b_ref, c_ref, d_ref)):
            sl = slice(k * GROUP_WIDTH, (k + 1) * GROUP_WIDTH)
            y, _ = _rms(ref[...], g_ref[:, sl])
            o_ref[:, sl] = y.astype(o_ref.dtype)

    return pl.pallas_call(
        body, name=name, grid=(S // bs,),
        in_specs=[_row_spec(bs, GROUP_WIDTH)] * 4 + [_vec_spec(D_MODEL)],
        out_specs=_row_spec(bs, D_MODEL), out_shape=jax.ShapeDtypeStruct((S, D_MODEL), MXU_DTYPE),
        compiler_params=_cparams("parallel"),
    )(*outs, g)


def _gnorm_bwd(dgn, outs, g, *, bs, name):
    S = dgn.shape[0]

    def body(dgn_ref, a_ref, b_ref, c_ref, d_ref, g_ref, dmix_ref, dg_ref):
        @pl.when(pl.program_id(0) == 0)
        def _():
            dg_ref[...] = jnp.zeros_like(dg_ref)

        for k, ref in enumerate((a_ref, b_ref, c_ref, d_ref)):
            sl = slice(k * GROUP_WIDTH, (k + 1) * GROUP_WIDTH)
            x = ref[...]
            _, r = _rms(x, g_ref[:, sl])
            dx, t = _rms_bwd(dgn_ref[:, sl], x, r, g_ref[:, sl])
            dmix_ref[:, sl] = dx
            dg_ref[:, sl] += jnp.sum(t, axis=0, keepdims=True)

    return pl.pallas_call(
        body, name=name, grid=(S // bs,),
        in_specs=[_row_spec(bs, D_MODEL)] + [_row_spec(bs, GROUP_WIDTH)] * 4 + [_vec_spec(D_MODEL)],
        out_specs=[_row_spec(bs, D_MODEL), _vec_spec(D_MODEL)],
        out_shape=[jax.ShapeDtypeStruct((S, D_MODEL), F32), jax.ShapeDtypeStruct((1, D_MODEL), F32)],
        compiler_params=_cparams("arbitrary"),
    )(dgn, *outs, g)


def _ln_fwd(u, g, b, *, bs, name):
    S = u.shape[0]

    def body(u_ref, g_ref, b_ref, y_ref, yb_ref, xh_ref, rs_ref):
        x = u_ref[...]
        mu = jnp.mean(x, axis=-1, keepdims=True)
        xc = x - mu
        rs = lax.rsqrt(jnp.mean(xc * xc, axis=-1, keepdims=True) + 1e-5)
        xh = xc * rs
        y = xh * g_ref[...] + b_ref[...]
        y_ref[...] = y
        yb_ref[...] = y.astype(yb_ref.dtype)
        xh_ref[...] = xh
        rs_ref[...] = rs

    return pl.pallas_call(
        body, name=name, grid=(S // bs,),
        in_specs=[_row_spec(bs, D_MODEL), _vec_spec(D_MODEL), _vec_spec(D_MODEL)],
        out_specs=[_row_spec(bs, D_MODEL), _row_spec(bs, D_MODEL), _row_spec(bs, D_MODEL), _row_spec(bs, 1)],
        out_shape=[jax.ShapeDtypeStruct((S, D_MODEL), F32), jax.ShapeDtypeStruct((S, D_MODEL), MXU_DTYPE),
                   jax.ShapeDtypeStruct((S, D_MODEL), F32), jax.ShapeDtypeStruct((S, 1), F32)],
        compiler_params=_cparams("parallel"),
    )(u, g, b)


def _ln_bwd(dy, xh, rs, g, *, bs, name):
    S = dy.shape[0]

    def body(dy_ref, xh_ref, rs_ref, g_ref, du_ref, dub_ref, dg_ref, db_ref):
        @pl.when(pl.program_id(0) == 0)
        def _():
            dg_ref[...] = jnp.zeros_like(dg_ref)
            db_ref[...] = jnp.zeros_like(db_ref)

        dy_, xh_ = dy_ref[...], xh_ref[...]
        dxh = dy_ * g_ref[...]
        du = rs_ref[...] * (dxh - jnp.mean(dxh, axis=-1, keepdims=True)
                            - xh_ * jnp.mean(dxh * xh_, axis=-1, keepdims=True))
        du_ref[...] = du
        dub_ref[...] = du.astype(dub_ref.dtype)
        dg_ref[...] += jnp.sum(dy_ * xh_, axis=0, keepdims=True)
        db_ref[...] += jnp.sum(dy_, axis=0, keepdims=True)

    return pl.pallas_call(
        body, name=name, grid=(S // bs,),
        in_specs=[_row_spec(bs, D_MODEL), _row_spec(bs, D_MODEL), _row_spec(bs, 1), _vec_spec(D_MODEL)],
        out_specs=[_row_spec(bs, D_MODEL), _row_spec(bs, D_MODEL), _vec_spec(D_MODEL), _vec_spec(D_MODEL)],
        out_shape=[jax.ShapeDtypeStruct((S, D_MODEL), F32), jax.ShapeDtypeStruct((S, D_MODEL), MXU_DTYPE),
                   jax.ShapeDtypeStruct((1, D_MODEL), F32), jax.ShapeDtypeStruct((1, D_MODEL), F32)],
        compiler_params=_cparams("arbitrary"),
    )(dy, xh, rs, g)


def _swiglu_fwd(gu, *, bs, name):
    S = gu.shape[0]

    def body(gu_ref, a_ref):
        gt = gu_ref[:, :D_FF]
        a_ref[...] = (gt / (1.0 + jnp.exp(-gt)) * gu_ref[:, D_FF:]).astype(a_ref.dtype)

    return pl.pallas_call(
        body, name=name, grid=(S // bs,),
        in_specs=[_row_spec(bs, 2 * D_FF)],
        out_specs=_row_spec(bs, D_FF), out_shape=jax.ShapeDtypeStruct((S, D_FF), MXU_DTYPE),
        compiler_params=_cparams("parallel"),
    )(gu)


def _swiglu_bwd(da, gu, *, bs, name):
    S = gu.shape[0]

    def body(da_ref, gu_ref, dgu_ref):
        gt, da_ = gu_ref[:, :D_FF], da_ref[...]
        sg = 1.0 / (1.0 + jnp.exp(-gt))
        silu = gt * sg
        dgu_ref[:, :D_FF] = (da_ * gu_ref[:, D_FF:] * (sg + silu * (1.0 - sg))).astype(dgu_ref.dtype)
        dgu_ref[:, D_FF:] = (da_ * silu).astype(dgu_ref.dtype)

    return pl.pallas_call(
        body, name=name, grid=(S // bs,),
        in_specs=[_row_spec(bs, D_FF), _row_spec(bs, 2 * D_FF)],
        out_specs=_row_spec(bs, 2 * D_FF), out_shape=jax.ShapeDtypeStruct((S, 2 * D_FF), MXU_DTYPE),
        compiler_params=_cparams("parallel"),
    )(da, gu)


def _loss_head(y, target, *, bs, name):
    S = y.shape[0]

    def body(y_ref, t_ref, dy_ref, loss_ref):
        @pl.when(pl.program_id(0) == 0)
        def _():
            loss_ref[...] = jnp.zeros_like(loss_ref)

        e = y_ref[...] - t_ref[...]
        dy_ref[...] = e * (1.0 / D_MODEL)
        per_tok = jnp.mean(e * e, axis=-1, keepdims=True)
        loss_ref[...] += 0.5 * jnp.sum(per_tok, axis=0, keepdims=True)

    return pl.pallas_call(
        body, name=name, grid=(S // bs,),
        in_specs=[_row_spec(bs, D_MODEL), _row_spec(bs, D_MODEL)],
        out_specs=[_row_spec(bs, D_MODEL), pl.BlockSpec((1, 1), lambda i: (0, 0))],
        out_shape=[jax.ShapeDtypeStruct((S, D_MODEL), F32), jax.ShapeDtypeStruct((1, 1), F32)],
        compiler_params=_cparams("arbitrary"),
    )(y, target)


def _blk(n, target):
    if n <= target:
        return n
    best = None
    for b in range(128, target + 1, 128):
        if n % b == 0:
            best = b
    assert best is not None, n
    return best


def _rope_tables(S):
    pos = jnp.arange(S, dtype=F32)
    inv = ROPE_THETA ** (-jnp.arange(0, MLA_ROPE, 2, dtype=F32) / MLA_ROPE)
    ang = pos[:, None] * inv[None, :]
    cos, sin = jnp.cos(ang), jnp.sin(ang)
    one, zero, pad = jnp.ones((S, HEAD_DIM), F32), jnp.zeros((S, HEAD_DIM), F32), jnp.zeros((S, MLA_PAD - MLA_QK), F32)
    return jnp.concatenate([one, cos, cos, pad], axis=1), jnp.concatenate([zero, sin, sin, pad], axis=1)


def _prep_weights(w_in, w_uq, w_ukv, w_o, w_gate, w_up, w_down):
    z = lambda n: jnp.zeros((D_MODEL, n), w_in.dtype)
    win_a = jnp.concatenate([w_in[:, 0:768], w_in[:, 1188:2468]], axis=1)
    win_l = jnp.concatenate([w_in[:, 772:1156], z(64), w_in[:, 1156:1188], z(32), w_in[:, 768:772], z(124)], axis=1)
    kv = w_ukv.reshape(MLA_KV_RANK, 4, 2 * HEAD_DIM)
    return dict(
        win_a=win_a, win_l=win_l, win_p=jnp.concatenate([win_a, win_l], axis=1),
        wuq=jnp.pad(w_uq.reshape(MLA_Q_RANK, 4, MLA_QK), ((0, 0), (0, 0), (0, MLA_PAD - MLA_QK))).reshape(MLA_Q_RANK, 512),
        wuk=jnp.pad(kv[:, :, :HEAD_DIM], ((0, 0), (0, 0), (0, HEAD_DIM))).reshape(MLA_KV_RANK, 512),
        wuv=kv[:, :, HEAD_DIM:].reshape(MLA_KV_RANK, 256),
        w_o=w_o, wgu=jnp.concatenate([w_gate, w_up], axis=1), w_down=w_down)


def _unprep_grads(dwin_p, dwuq, dwuk, dwuv, dwo, dwgu, dwd):
    dw_in = jnp.concatenate([dwin_p[:, 0:768], dwin_p[:, 2560:2564], dwin_p[:, 2048:2432], dwin_p[:, 2496:2528],
                             dwin_p[:, 768:2048]], axis=1)
    dw_uq = dwuq.reshape(MLA_Q_RANK, 4, MLA_PAD)[:, :, :MLA_QK].reshape(MLA_Q_RANK, 4 * MLA_QK)
    dw_ukv = jnp.concatenate([dwuk.reshape(MLA_KV_RANK, 4, MLA_PAD)[:, :, :HEAD_DIM],
                              dwuv.reshape(MLA_KV_RANK, 4, HEAD_DIM)], axis=2).reshape(MLA_KV_RANK, 512)
    return dict(w_in=dw_in, mla_w_uq=dw_uq, mla_w_ukv=dw_ukv, w_o=dwo, w_gate=dwgu[:, :D_FF], w_up=dwgu[:, D_FF:],
                w_down=dwd)


def _layer_fwd(l, x, xb, W, P, tabs, blk):
    S = x.shape[0]
    nb = S // blk
    n = lambda s: f"l{l}_{s}"
    bs = min(512, S)
    h_att = _mm(xb, W["win_a"], name=n("in_att"), out_dtype=MXU_DTYPE, bm=1024, bn=1024, bk=1024, colscale=Q_COLSCALE)
    lat = _mm(xb, W["win_l"], name=n("in_lat"), bm=2048, bn=LAT_W, bk=1024)
    fg = lat[:, 512:516].T.reshape(4, S // 128, 128)
    cpos, cneg = _fox_gate_fwd(fg, P["fox_b_f"], name=n("fox_gate"))
    one3 = jnp.ones((S, 4, 3), MXU_DTYPE)
    zpad = jnp.zeros((S, 4, MLA_PAD - HEAD_DIM - 6), MXU_DTYPE)
    per_tok = lambda parts: parts.reshape(4, 3, S).transpose(2, 0, 1)
    q_f = jnp.concatenate([h_att[:, COL_FQ:COL_FQ + 256].reshape(S, 4, HEAD_DIM), per_tok(cpos), one3, zpad],
                          axis=2).reshape(S, 4 * MLA_PAD)
    k_f = jnp.concatenate([h_att[:, COL_FK:COL_FK + 256].reshape(S, 4, HEAD_DIM), one3, per_tok(cneg), zpad],
                          axis=2).reshape(S, 4 * MLA_PAD)
    v_f = h_att[:, COL_FV:COL_FV + 256]
    oT_a, lse_a = _smax_fwd_t(q_f.T, k_f, _kv_blocks_t(v_f, blk), dk=MLA_PAD, blk=blk, name=n("fox_fwd"))
    q_m, k_m, v_m = _mla_prep_fwd(lat, P["mla_g_q"], P["mla_g_kv"], W["wuq"], W["wuk"], W["wuv"], *tabs,
                                  bs=bs, name=n("mla_prep"))
    oT_b, lse_b = _smax_fwd_t(q_m.T, k_m, _kv_blocks_t(v_m, blk), dk=MLA_PAD, blk=blk, name=n("mla_fwd"))
    qT_c = h_att[:, COL_SQ:COL_SQ + 256].T
    oT_c, lt_c = _sb_fwd_t(qT_c, h_att, _kv_blocks_t(h_att[:, COL_SV:COL_SV + 256], blk), blk=blk, name=n("sb_fwd"))
    out_d, lse_d = _swa_fwd(h_att, P["swa_sinks"], name=n("swa_fwd"))
    outs = (oT_a.T, oT_b.T, oT_c.T, out_d)
    gn = _gnorm_fwd(outs, P["mix_g"], bs=bs, name=n("gnorm"))
    u1 = _mm(gn, W["w_o"], name=n("out_proj"), bm=1024, bn=1024, bk=1024, resid=x, alpha=ALPHA)
    x1, x1b, xh1, rs1 = _ln_fwd(u1, P["ln1_g"], P["ln1_b"], bs=bs, name=n("ln1"))
    gu = _mm(x1b, W["wgu"], name=n("gate_up"), bm=2048, bn=512, bk=1024)
    a = _swiglu_fwd(gu, bs=min(256, S), name=n("swiglu"))
    u2 = _mm(a, W["w_down"], name=n("down"), bm=1024, bn=1024, bk=_blk(D_FF, 1408), resid=x1, alpha=ALPHA)
    x2, x2b, xh2, rs2 = _ln_fwd(u2, P["ln2_g"], P["ln2_b"], bs=bs, name=n("ln2"))
    saved = dict(xb=xb, h_att=h_att, lat=lat, fg=fg, outs=outs, oT_a=oT_a, oT_b=oT_b, q_f=q_f, k_f=k_f, v_f=v_f,
                 qT_c=qT_c, lse_a=lse_a, lse_b=lse_b, lse_d=lse_d, lt_c=lt_c, q_m=q_m, k_m=k_m, v_m=v_m, gn=gn,
                 xh1=xh1, rs1=rs1, x1b=x1b, gu=gu, a=a, xh2=xh2, rs2=rs2)
    return x2, x2b, saved


def _layer_bwd(l, dx2, sv, W, P, tabs, blk):
    S = dx2.shape[0]
    n = lambda s: f"l{l}_{s}"
    bs = min(512, S)
    h_att = sv["h_att"]
    du2, du2b, dg2, db2 = _ln_bwd(dx2, sv["xh2"], sv["rs2"], P["ln2_g"], bs=bs, name=n("ln2_bwd"))
    da = _mm(du2b, W["w_down"], name=n("down_dx"), tb=True, bm=1024, bn=_blk(D_FF, 1408), bk=1024)
    dwd = _mm(sv["a"].T, du2b, name=n("down_dw"), bm=_blk(D_FF, 1408), bn=1024, bk=1024)
    dgu = _swiglu_bwd(da, sv["gu"], bs=min(256, S), name=n("swiglu_bwd"))
    dx1 = _mm(dgu, W["wgu"], name=n("gate_up_dx"), tb=True, bm=1024, bn=1024, bk=_blk(2 * D_FF, 1408), resid=du2,
              alpha=ALPHA)
    dwgu = _mm(sv["x1b"].T, dgu, name=n("gate_up_dw"), bm=1024, bn=_blk(2 * D_FF, 1408), bk=1024)
    du1, du1b, dg1, db1 = _ln_bwd(dx1, sv["xh1"], sv["rs1"], P["ln1_g"], bs=bs, name=n("ln1_bwd"))
    dgn = _mm(du1b, W["w_o"], name=n("out_proj_dx"), tb=True, bm=1024, bn=1024, bk=1024)
    dwo = _mm(sv["gn"].T, du1b, name=n("out_proj_dw"), bm=1024, bn=1024, bk=1024)
    dmix, dmixg = _gnorm_bwd(dgn, sv["outs"], P["mix_g"], bs=bs, name=n("gnorm_bwd"))
    dmixT = dmix.T
    q_f, k_f = sv["q_f"], sv["k_f"]
    dqT_a, dk_a, dva = _smax_bwd_t(q_f.T, q_f, k_f, _kv_blocks_t(k_f, blk), sv["v_f"], dmix, dmixT, sv["oT_a"],
                                   sv["lse_a"], dk=MLA_PAD, dcb=0, qscale=HEAD_DIM ** -0.5, blk=blk, name=n("fox_bwd"))
    dq_a, dk_a = dqT_a.T.reshape(S, 4, MLA_PAD), dk_a.reshape(S, 4, MLA_PAD)
    dqa, dka = dq_a[:, :, :HEAD_DIM].reshape(S, 256), dk_a[:, :, :HEAD_DIM].reshape(S, 256)
    dcq = dq_a[:, :, HEAD_DIM].T.reshape(4, S // 128, 128)
    dck = dk_a[:, :, HEAD_DIM + 3].T.reshape(4, S // 128, 128)
    q_m, k_m = sv["q_m"], sv["k_m"]
    dqT_b, dkb, dvb = _smax_bwd_t(q_m.T, q_m, k_m, _kv_blocks_t(k_m, blk), sv["v_m"], dmix, dmixT, sv["oT_b"],
                                  sv["lse_b"], dk=MLA_PAD, dcb=2, qscale=MLA_QK ** -0.5, blk=blk, name=n("mla_bwd"))
    dqT_c, dkc, dvc = _sb_bwd_t(sv["qT_c"], h_att, _kv_blocks_t(h_att[:, COL_SK:COL_SK + 256], blk), dmix, dmixT,
                                sv["lt_c"], dcb=4, qscale=HEAD_DIM ** -0.5, blk=blk, name=n("sb_bwd"))
    dqc = dqT_c.T
    dqd, dkd, dvd, dsink = _swa_bwd(h_att, P["swa_sinks"], dmix, sv["outs"][3], sv["lse_d"], dcb=3, name=n("swa_bwd"))
    dlat, dwuq, dwuk, dwuv, dgq, dgkv = _mla_prep_bwd(
        sv["lat"], P["mla_g_q"], P["mla_g_kv"], W["wuq"], W["wuk"], W["wuv"], *tabs, dqT_b.T, dkb, dvb,
        bs=bs, name=n("mla_prep_bwd"))
    dfg, dbf = _fox_gate_bwd(sv["fg"], P["fox_b_f"], dck, dcq, q_unscale=HEAD_DIM ** 0.5, name=n("fox_gate_bwd"))
    dfg_blk = jnp.pad(dfg.reshape(4, S).T, ((0, 0), (0, 124)))
    dh = jnp.concatenate([t.astype(MXU_DTYPE) for t in (dqa, dka, dva, dqc, dkc, dvc, dqd, dkd, dvd, dlat, dfg_blk)], axis=1)
    dx = _mm(dh, W["win_p"], name=n("in_dx"), tb=True, bm=1024, bn=1024, bk=_blk(PERM_W, 1024), resid=du1, alpha=ALPHA)
    dwin_p = _mm(sv["xb"].T, dh, name=n("in_dw"), bm=1024, bn=_blk(PERM_W, 1024), bk=1024)
    grads = _unprep_grads(dwin_p, dwuq, dwuk, dwuv, dwo, dwgu, dwd)
    grads.update(fox_b_f=dbf[:, 0], mla_g_q=dgq[0], mla_g_kv=dgkv[0], swa_sinks=dsink[:, 0], mix_g=dmixg[0],
                 ln1_g=dg1[0], ln1_b=db1[0], ln2_g=dg2[0], ln2_b=db2[0])
    return dx, grads


BIG = ("w_in", "mla_w_uq", "mla_w_ukv", "w_o", "w_gate", "w_up", "w_down")
SMALL = ("fox_b_f", "mla_g_q", "mla_g_kv", "swa_sinks", "mix_g", "ln1_g", "ln1_b", "ln2_g", "ln2_b")
SHARD_AXIS = dict(w_in=2, mla_w_uq=2, mla_w_ukv=2, w_o=1, w_gate=2, w_up=2, w_down=1)
N_CHIPS = 4
ANY = pl.BlockSpec(memory_space=pl.ANY)


def _chip_exchange(tensors, *, scatter, name):
    nt = len(tensors)

    def body(*refs):
        ins, outs = refs[:nt], refs[nt:2 * nt]
        send_sems, recv_sems, local_sems = refs[2 * nt:]
        x, y, c = lax.axis_index("x"), lax.axis_index("y"), lax.axis_index("c")
        me = 2 * x + y
        peers = [(1 - x, y), (x, 1 - y), (1 - x, 1 - y)]
        local, sends, recvs = [], [], []
        for t in range(nt):
            local.append(pltpu.make_async_copy(ins[t].at[me] if scatter else ins[t], outs[t].at[me], local_sems.at[t]))
            for r, (px, py) in enumerate(peers):
                k = 3 * t + r
                theirs = 2 * px + py
                sends.append(pltpu.make_async_remote_copy(
                    src_ref=ins[t].at[theirs] if scatter else ins[t], dst_ref=outs[t].at[me],
                    send_sem=send_sems.at[k], recv_sem=recv_sems.at[k], device_id=(px, py, c), device_id_type=MESH))
                recvs.append(pltpu.make_async_remote_copy(
                    src_ref=ins[t].at[me] if scatter else ins[t], dst_ref=outs[t].at[theirs],
                    send_sem=send_sems.at[k], recv_sem=recv_sems.at[k], device_id=(px, py, c), device_id_type=MESH))
        for cp in local + sends:
            cp.start()
        for cp in recvs:
            cp.wait_recv()
        for cp in sends:
            cp.wait_send()
        for cp in local:
            cp.wait()

    out_shape = [jax.ShapeDtypeStruct(t.shape if scatter else (N_CHIPS,) + t.shape, t.dtype) for t in tensors]
    return pl.pallas_call(
        body, name=name, in_specs=[ANY] * nt, out_specs=[ANY] * nt, out_shape=out_shape,
        scratch_shapes=[pltpu.SemaphoreType.DMA((3 * nt,)), pltpu.SemaphoreType.DMA((3 * nt,)),
                        pltpu.SemaphoreType.DMA((nt,))],
        compiler_params=pltpu.CompilerParams(has_side_effects=True),
    )(*tensors)


def _core_exchange(tensors, *, name):
    nt = len(tensors)

    def body(*refs):
        ins, outs = refs[:nt], refs[nt:2 * nt]
        send_sems, recv_sems = refs[2 * nt:]
        sibling = (lax.axis_index("x"), lax.axis_index("y"), 1 - lax.axis_index("c"))
        copies = [pltpu.make_async_remote_copy(src_ref=ins[t], dst_ref=outs[t], send_sem=send_sems.at[t],
                                               recv_sem=recv_sems.at[t], device_id=sibling, device_id_type=MESH)
                  for t in range(nt)]
        for cp in copies:
            cp.start()
        for cp in copies:
            cp.wait_recv()
        for cp in copies:
            cp.wait_send()

    return pl.pallas_call(
        body, name=name, in_specs=[ANY] * nt, out_specs=[ANY] * nt,
        out_shape=[jax.ShapeDtypeStruct(t.shape, t.dtype) for t in tensors],
        scratch_shapes=[pltpu.SemaphoreType.DMA((nt,)), pltpu.SemaphoreType.DMA((nt,))],
        compiler_params=pltpu.CompilerParams(has_side_effects=True),
    )(*tensors)


def _all_sum_small(block, *, name):
    R = block.shape[0]
    n_dev = 8

    def body(x_ref, o_ref, slots, send_sems, recv_sems):
        x, y, c = lax.axis_index("x"), lax.axis_index("y"), lax.axis_index("c")
        me = 4 * x + 2 * y + c
        slots[me] = x_ref[...]
        sends, recvs = [], []
        for d in range(1, n_dev):
            px, py, pc = x ^ (d >> 2), y ^ ((d >> 1) & 1), c ^ (d & 1)
            theirs = 4 * px + 2 * py + pc
            sends.append(pltpu.make_async_remote_copy(
                src_ref=x_ref, dst_ref=slots.at[me], send_sem=send_sems.at[d - 1], recv_sem=recv_sems.at[d - 1],
                device_id=(px, py, pc), device_id_type=MESH))
            recvs.append(pltpu.make_async_remote_copy(
                src_ref=x_ref, dst_ref=slots.at[theirs], send_sem=send_sems.at[d - 1], recv_sem=recv_sems.at[d - 1],
                device_id=(px, py, pc), device_id_type=MESH))
        for cp in sends:
            cp.start()
        for cp in recvs:
            cp.wait_recv()
        for cp in sends:
            cp.wait_send()
        total = slots[0]
        for k in range(1, n_dev):
            total = total + slots[k]
        o_ref[...] = total

    return pl.pallas_call(
        body, name=name, in_specs=[pl.BlockSpec(memory_space=pltpu.VMEM)],
        out_specs=pl.BlockSpec(memory_space=pltpu.VMEM), out_shape=jax.ShapeDtypeStruct((R, 128), F32),
        scratch_shapes=[pltpu.VMEM((n_dev, R, 128), F32), pltpu.SemaphoreType.DMA((n_dev - 1,)),
                        pltpu.SemaphoreType.DMA((n_dev - 1,))],
        compiler_params=pltpu.CompilerParams(has_side_effects=True),
    )(block)


def _sum_chips(recv, *, br, name):
    _, R, C = recv.shape

    def body(r_ref, o_ref):
        total = r_ref[0].astype(F32)
        for k in range(1, N_CHIPS):
            total = total + r_ref[k].astype(F32)
        o_ref[...] = total

    return pl.pallas_call(
        body, name=name, grid=(R // br,), in_specs=[pl.BlockSpec((N_CHIPS, br, C), lambda i: (0, i, 0))],
        out_specs=pl.BlockSpec((br, C), lambda i: (i, 0)), out_shape=jax.ShapeDtypeStruct((R, C), F32),
        compiler_params=_cparams("parallel"),
    )(recv)


def _adamw_math(w, g, m, v):
    m = ADAM_B1 * m + (1.0 - ADAM_B1) * g
    v = ADAM_B2 * v + (1.0 - ADAM_B2) * (g * g)
    m_hat = m / (1.0 - ADAM_B1 ** ADAM_STEP)
    v_hat = v / (1.0 - ADAM_B2 ** ADAM_STEP)
    return -ADAM_LR * (m_hat / (jnp.sqrt(v_hat) + ADAM_EPS) + ADAM_WD * w), m, v


def _adamw(w, m, v, g_a, g_b, *, br, name):
    R, C = w.shape
    two = g_b is not None

    def body(*refs):
        if two:
            w_ref, m_ref, v_ref, ga_ref, gb_ref, g_ref, d_ref, nm_ref, nv_ref = refs
            g = ga_ref[...] + gb_ref[...]
        else:
            w_ref, m_ref, v_ref, ga_ref, g_ref, d_ref, nm_ref, nv_ref = refs
            g = ga_ref[...]
        g_ref[...] = g
        d_ref[...], nm_ref[...], nv_ref[...] = _adamw_math(w_ref[...], g, m_ref[...], v_ref[...])

    spec = pl.BlockSpec((br, C), lambda i: (i, 0))
    args = [w, m, v, g_a] + ([g_b] if two else [])
    return pl.pallas_call(
        body, name=name, grid=(R // br,), in_specs=[spec] * len(args), out_specs=[spec] * 4,
        out_shape=[jax.ShapeDtypeStruct((R, C), F32)] * 4,
        compiler_params=_cparams("parallel"),
    )(*args)


SMALL_ROWS = dict(fox_b_f=1, mla_g_q=2, mla_g_kv=1, swa_sinks=1, mix_g=8, ln1_g=8, ln1_b=8, ln2_g=8, ln2_b=8)
SMALL_ROWS_PER_LAYER = sum(SMALL_ROWS.values())


def _pack_small(vals, extra_rows):
    L = vals[SMALL[0]].shape[0]
    per_layer = []
    for name in SMALL:
        a = vals[name].astype(F32)
        a = jnp.pad(a, ((0, 0), (0, SMALL_ROWS[name] * 128 - a.shape[1])))
        per_layer.append(a.reshape(L, SMALL_ROWS[name], 128))
    out = jnp.concatenate(per_layer, axis=1).reshape(L * SMALL_ROWS_PER_LAYER, 128)
    return jnp.pad(out, ((0, extra_rows), (0, 0)))


def _unpack_small(block, shapes):
    L = shapes[SMALL[0]][0]
    body = block[:L * SMALL_ROWS_PER_LAYER].reshape(L, SMALL_ROWS_PER_LAYER, 128)
    out, r = {}, 0
    for name in SMALL:
        n = shapes[name][1]
        out[name] = body[:, r:r + SMALL_ROWS[name]].reshape(L, SMALL_ROWS[name] * 128)[:, :n]
        r += SMALL_ROWS[name]
    return out


def _to_chips(g, axis):
    L, a, b = g.shape
    if axis == 2:
        return g.reshape(L, a, N_CHIPS, b // N_CHIPS).transpose(2, 0, 1, 3)
    return g.reshape(L, N_CHIPS, a // N_CHIPS, b).transpose(1, 0, 2, 3)


def _from_chips(g, axis):
    _, L, a, b = g.shape
    if axis == 2:
        return g.transpose(1, 2, 0, 3).reshape(L, a, N_CHIPS * b)
    return g.transpose(1, 0, 2, 3).reshape(L, N_CHIPS * a, b)


def kernel(x, w_in, fox_b_f, mla_g_q, mla_g_kv, mla_w_uq, mla_w_ukv, swa_sinks, mix_g, w_o, ln1_g, ln1_b, w_gate, w_up, w_down, ln2_g, ln2_b, loss_target, m_w_in, m_fox_b_f, m_mla_g_q, m_mla_g_kv, m_mla_w_uq, m_mla_w_ukv, m_swa_sinks, m_mix_g, m_w_o, m_ln1_g, m_ln1_b, m_w_gate, m_w_up, m_w_down, m_ln2_g, m_ln2_b, v_w_in, v_fox_b_f, v_mla_g_q, v_mla_g_kv, v_mla_w_uq, v_mla_w_ukv, v_swa_sinks, v_mix_g, v_w_o, v_ln1_g, v_ln1_b, v_w_gate, v_w_up, v_w_down, v_ln2_g, v_ln2_b):
    w = dict(w_in=w_in, fox_b_f=fox_b_f, mla_g_q=mla_g_q, mla_g_kv=mla_g_kv, mla_w_uq=mla_w_uq, mla_w_ukv=mla_w_ukv,
             swa_sinks=swa_sinks, mix_g=mix_g, w_o=w_o, ln1_g=ln1_g, ln1_b=ln1_b, w_gate=w_gate, w_up=w_up,
             w_down=w_down, ln2_g=ln2_g, ln2_b=ln2_b)
    m = dict(w_in=m_w_in, fox_b_f=m_fox_b_f, mla_g_q=m_mla_g_q, mla_g_kv=m_mla_g_kv, mla_w_uq=m_mla_w_uq,
             mla_w_ukv=m_mla_w_ukv, swa_sinks=m_swa_sinks, mix_g=m_mix_g, w_o=m_w_o, ln1_g=m_ln1_g, ln1_b=m_ln1_b,
             w_gate=m_w_gate, w_up=m_w_up, w_down=m_w_down, ln2_g=m_ln2_g, ln2_b=m_ln2_b)
    v = dict(w_in=v_w_in, fox_b_f=v_fox_b_f, mla_g_q=v_mla_g_q, mla_g_kv=v_mla_g_kv, mla_w_uq=v_mla_w_uq,
             mla_w_ukv=v_mla_w_ukv, swa_sinks=v_swa_sinks, mix_g=v_mix_g, w_o=v_w_o, ln1_g=v_ln1_g, ln1_b=v_ln1_b,
             w_gate=v_w_gate, w_up=v_w_up, w_down=v_w_down, ln2_g=v_ln2_g, ln2_b=v_ln2_b)
    names = tuple(w)
    L = w_in.shape[0]
    S = x.shape[1]
    blk = min(256, S)
    bs = min(512, S)

    gathered = _chip_exchange([w[k].astype(MXU_DTYPE) for k in BIG], scatter=False, name="gather_weights")
    full = {k: _from_chips(g, SHARD_AXIS[k]) for k, g in zip(BIG, gathered)}
    tabs = _rope_tables(S)
    Ws = [_prep_weights(*[full[k][l] for k in BIG]) for l in range(L)]
    Ps = []
    for l in range(L):
        P = dict(fox_b_f=fox_b_f[l], swa_sinks=swa_sinks[l])
        for k in ("mla_g_q", "mla_g_kv", "mix_g", "ln1_g", "ln1_b", "ln2_g", "ln2_b"):
            P[k] = w[k][l][None, :]
        Ps.append(P)

    xa = x[0]
    xb = xa.astype(MXU_DTYPE)
    saved = []
    for l in range(L):
        xa, xb, sv = _layer_fwd(l, xa, xb, Ws[l], Ps[l], tabs, blk)
        saved.append(sv)
    dx, loss_part = _loss_head(xa, loss_target[0], bs=bs, name="loss_head")
    layer_grads = [None] * L
    for l in reversed(range(L)):
        dx, layer_grads[l] = _layer_bwd(l, dx, saved[l], Ws[l], Ps[l], tabs, blk)
    grad_x = dx[None]

    local = {k: jnp.stack([layer_grads[l][k] for l in range(L)]) for k in names}
    to_owner = [_to_chips(local[k], SHARD_AXIS[k]).astype(MXU_DTYPE) for k in BIG]
    received = _chip_exchange(to_owner, scatter=True, name="scatter_grads")
    partial = []
    for k, r in zip(BIG, received):
        _, _, a, b = r.shape
        partial.append(_sum_chips(r.reshape(N_CHIPS, L * a, b), br=_rows(L * a), name=f"sum_{k}"))
    sibling = _core_exchange(partial, name="swap_partials")
    out = {}
    for k, mine, theirs in zip(BIG, partial, sibling):
        shp = w[k].shape
        two_d = lambda t: t.reshape(shp[0] * shp[1], shp[2])
        res = _adamw(two_d(w[k]), two_d(m[k]), two_d(v[k]), mine, theirs, br=_rows(shp[0] * shp[1]), name=f"adamw_{k}")
        out[k] = [t.reshape(shp) for t in res]

    shapes = {k: w[k].shape for k in SMALL}
    extra = 8 + (-L * SMALL_ROWS_PER_LAYER) % 8
    block = _pack_small({k: local[k] for k in SMALL}, extra)
    block = block.at[L * SMALL_ROWS_PER_LAYER, 0].set(loss_part[0, 0])
    total = _all_sum_small(block, name="sum_small")
    loss = total[L * SMALL_ROWS_PER_LAYER, 0]
    res = _adamw(_pack_small({k: w[k] for k in SMALL}, extra), _pack_small({k: m[k] for k in SMALL}, extra),
                 _pack_small({k: v[k] for k in SMALL}, extra), total, None, br=total.shape[0], name="adamw_small")
    res = [_unpack_small(t, shapes) for t in res]
    for k in SMALL:
        out[k] = [r[k] for r in res]

    return (loss, grad_x, *[out[k][0] for k in names], *[out[k][1] for k in names],
            *[out[k][2] for k in names], *[out[k][3] for k in names])


def _rows(n):
    for b in (256, 128, 64, 32, 16, 8):
        if n % b == 0:
            return b
    return n
```

```python
import functools

import numpy as np
import jax
import jax.numpy as jnp
from jax import lax
from jax.experimental import pallas as pl
from jax.experimental.pallas import tpu as pltpu

F32 = jnp.float32
MXU_DTYPE = jnp.bfloat16
NEG_INF = -1e30

D_MODEL = 1024
DEPTH = 4
HEAD_DIM = 64
GROUP_WIDTH = 256
N_GROUPS = 4
D_FF = 2816
MLA_Q_RANK = 256
MLA_KV_RANK = 128
MLA_ROPE = 32
MLA_QK = 96
MLA_PAD = 128
ROPE_THETA = 10000.0
WINDOW = 128
ALPHA = (2.0 * DEPTH) ** 0.25
SWA_SLOPES = tuple(float(2.0 ** (-8.0 * h / 4)) for h in range(1, 5))
IN_WIDTH = 2468
ATT_W = 2048
LAT_W = 640
PERM_W = ATT_W + LAT_W
COL_FQ, COL_FK, COL_FV = 0, 256, 512
COL_SQ, COL_SK, COL_SV = 768, 1024, 1280
COL_WQ, COL_WK, COL_WV = 1536, 1792, 1920
Q_COLSCALE = np.ones((1, ATT_W), np.float32)
Q_COLSCALE[:, COL_FQ:COL_FQ + 256] = HEAD_DIM ** -0.5
Q_COLSCALE[:, COL_SQ:COL_SQ + 256] = HEAD_DIM ** -0.5

ADAM_LR, ADAM_B1, ADAM_B2, ADAM_EPS, ADAM_WD, ADAM_STEP = 0.001, 0.9, 0.999, 1e-08, 0.01, 10

VMEM_LIMIT = 56 * 1024 * 1024
NT = (((1,), (1,)), ((), ()))
TN = (((0,), (0,)), ((), ()))
MESH = pl.DeviceIdType.MESH


def _cparams(*sem):
    return pltpu.CompilerParams(dimension_semantics=sem, vmem_limit_bytes=VMEM_LIMIT)


def _split2(x):
    hi = x.astype(MXU_DTYPE)
    lo = (x - hi.astype(F32)).astype(MXU_DTYPE)
    return hi, lo


def _dot01(x, m01, dn=None, parts=2):
    acc = None
    rem = x
    for _ in range(parts):
        part = rem.astype(MXU_DTYPE)
        rem = rem - part.astype(F32)
        if dn is None:
            t = jnp.dot(part, m01, preferred_element_type=F32)
        else:
            t = lax.dot_general(part, m01, dn, preferred_element_type=F32)
        acc = t if acc is None else acc + t
    return acc


def _mm(a, b, *, name, ta=False, tb=False, out_dtype=F32, bm=512, bn=512, bk=512, resid=None, alpha=1.0,
        colscale=None):
    M, K = (a.shape[1], a.shape[0]) if ta else a.shape
    N = b.shape[0] if tb else b.shape[1]
    assert (b.shape[1] if tb else b.shape[0]) == K
    assert resid is None or colscale is None
    bm, bn, bk = min(bm, M), min(bn, N), min(bk, K)
    assert M % bm == 0 and N % bn == 0 and K % bk == 0, (name, M, N, K, bm, bn, bk)
    nk = K // bk
    dn = (((0 if ta else 1,), (1 if tb else 0,)), ((), ()))

    extra = resid is not None or colscale is not None

    def body(*refs):
        a_ref, b_ref = refs[:2]
        r_ref = refs[2] if extra else None
        o_ref = refs[3] if extra else refs[2]
        acc_ref = refs[-1] if nk > 1 else None
        k = pl.program_id(2)
        part = lax.dot_general(a_ref[...].astype(MXU_DTYPE), b_ref[...].astype(MXU_DTYPE), dn,
                               preferred_element_type=F32)

        def finish(r):
            if resid is not None:
                r = r + alpha * r_ref[...]
            if colscale is not None:
                r = r * r_ref[...]
            o_ref[...] = r.astype(o_ref.dtype)

        if nk == 1:
            finish(part)
        else:
            @pl.when(k == 0)
            def _():
                acc_ref[...] = part

            @pl.when((k > 0) & (k < nk - 1))
            def _():
                acc_ref[...] += part

            @pl.when(k == nk - 1)
            def _():
                finish(acc_ref[...] + part)

    a_spec = pl.BlockSpec((bk, bm), lambda i, j, k: (k, i)) if ta else pl.BlockSpec((bm, bk), lambda i, j, k: (i, k))
    b_spec = pl.BlockSpec((bn, bk), lambda i, j, k: (j, k)) if tb else pl.BlockSpec((bk, bn), lambda i, j, k: (k, j))
    in_specs = [a_spec, b_spec]
    args = [a, b]
    if resid is not None:
        in_specs.append(pl.BlockSpec((bm, bn), lambda i, j, k: (i, j)))
        args.append(resid)
    if colscale is not None:
        in_specs.append(pl.BlockSpec((1, bn), lambda i, j, k: (0, j)))
        args.append(colscale)
    return pl.pallas_call(
        body, name=name, grid=(M // bm, N // bn, nk), in_specs=in_specs,
        out_specs=pl.BlockSpec((bm, bn), lambda i, j, k: (i, j)),
        out_shape=jax.ShapeDtypeStruct((M, N), out_dtype),
        scratch_shapes=[pltpu.VMEM((bm, bn), F32)] if nk > 1 else [],
        compiler_params=_cparams("parallel", "parallel", "arbitrary"),
    )(*args)


def _softmax_attn_fwd(q_arr, k_arr, v_arr, *, qcb, kcb, vcb, dk, scale, cum_col=None, cum_row=None, blk, name):
    S = q_arr.shape[0]
    nb = S // blk
    bias = cum_col is not None
    W = 2 * dk

    def body(*refs):
        if bias:
            q_ref, k_ref, v_ref, cc_ref, cr_ref, o_ref, lse_ref = refs
        else:
            q_ref, k_ref, v_ref, o_ref, lse_ref = refs
        p = pl.program_id(0)
        i = pl.program_id(1)
        row = lax.broadcasted_iota(jnp.int32, (blk, blk), 0)
        col = lax.broadcasted_iota(jnp.int32, (blk, blk), 1)
        for hh in range(2):
            q = q_ref[:, hh * dk:(hh + 1) * dk]

            def tile(j, carry, masked, hh=hh, q=q):
                m, l, acc = carry
                r0 = pl.multiple_of(j * blk, blk)
                ks = k_ref[pl.ds(r0, blk), hh * dk:(hh + 1) * dk]
                vs = v_ref[pl.ds(r0, blk), hh * HEAD_DIM:(hh + 1) * HEAD_DIM]
                s = lax.dot_general(q, ks, NT, preferred_element_type=F32) * scale
                if bias:
                    s = s + cc_ref[hh] - cr_ref[(2 * p + hh) * nb + j]
                if masked:
                    s = jnp.where(col <= row, s, NEG_INF)
                mn = jnp.maximum(m, jnp.max(s, axis=1, keepdims=True))
                a = jnp.exp(m - mn)
                pe = jnp.exp(s - mn)
                l = a * l + jnp.sum(pe, axis=1, keepdims=True)
                acc = a * acc + jnp.dot(pe.astype(MXU_DTYPE), vs, preferred_element_type=F32)
                return mn, l, acc

            init = (jnp.full((blk, 1), NEG_INF, F32), jnp.zeros((blk, 1), F32), jnp.zeros((blk, HEAD_DIM), F32))
            carry = lax.fori_loop(0, i, functools.partial(tile, masked=False), init)
            m, l, acc = tile(i, carry, True)
            o_ref[:, hh * HEAD_DIM:(hh + 1) * HEAD_DIM] = acc / l
            lse_ref[hh] = m + jnp.log(l)

    in_specs = [pl.BlockSpec((blk, W), lambda p, i: (i, qcb + p)),
                pl.BlockSpec((S, W), lambda p, i: (0, kcb + p)),
                pl.BlockSpec((S, 128), lambda p, i: (0, vcb + p))]
    args = [q_arr, k_arr, v_arr]
    if bias:
        in_specs += [pl.BlockSpec((2, blk, 1), lambda p, i: (p, i, 0)),
                     pl.BlockSpec((4 * nb, 1, blk), lambda p, i: (0, 0, 0))]
        args += [cum_col, cum_row]
    return pl.pallas_call(
        body, name=name, grid=(2, nb), in_specs=in_specs,
        out_specs=[pl.BlockSpec((blk, 128), lambda p, i: (i, p)), pl.BlockSpec((2, blk, 1), lambda p, i: (p, i, 0))],
        out_shape=[jax.ShapeDtypeStruct((S, GROUP_WIDTH), F32), jax.ShapeDtypeStruct((4, S, 1), F32)],
        compiler_params=_cparams("arbitrary", "arbitrary"),
    )(*args)


def _softmax_attn_bwd(q_arr, k_arr, v_arr, dmix, o_arr, lse, *, qcb, kcb, vcb, dcb, dk, scale,
                      cum_col=None, cum_row=None, blk, name):
    S = q_arr.shape[0]
    nb = S // blk
    bias = cum_col is not None
    W = 2 * dk

    def body(*refs):
        if bias:
            q_ref, k_ref, v_ref, do_ref, o_ref, lse_ref, cc_ref, cr_ref, dq_ref, dk_ref, dv_ref, dc_ref, dcq_ref = refs
        else:
            q_ref, k_ref, v_ref, do_ref, o_ref, lse_ref, dq_ref, dk_ref, dv_ref = refs
        p = pl.program_id(0)
        i = pl.program_id(1)

        @pl.when(i == 0)
        def _():
            dk_ref[...] = jnp.zeros_like(dk_ref)
            dv_ref[...] = jnp.zeros_like(dv_ref)
            if bias:
                dc_ref[...] = jnp.zeros_like(dc_ref)

        row = lax.broadcasted_iota(jnp.int32, (blk, blk), 0)
        col = lax.broadcasted_iota(jnp.int32, (blk, blk), 1)
        for hh in range(2):
            q = q_ref[:, hh * dk:(hh + 1) * dk]
            do = do_ref[:, hh * HEAD_DIM:(hh + 1) * HEAD_DIM]
            delta = jnp.sum(do * o_ref[:, hh * HEAD_DIM:(hh + 1) * HEAD_DIM], axis=1, keepdims=True)
            dob = do.astype(MXU_DTYPE)
            lse_h = lse_ref[hh]

            def tile(j, carry, masked, hh=hh, q=q, dob=dob, delta=delta, lse_h=lse_h):
                dq, dcq = carry
                r0 = pl.multiple_of(j * blk, blk)
                ks = k_ref[pl.ds(r0, blk), hh * dk:(hh + 1) * dk]
                vs = v_ref[pl.ds(r0, blk), hh * HEAD_DIM:(hh + 1) * HEAD_DIM]
                s = lax.dot_general(q, ks, NT, preferred_element_type=F32) * scale
                if bias:
                    s = s + cc_ref[hh] - cr_ref[(2 * p + hh) * nb + j]
                if masked:
                    s = jnp.where(col <= row, s, NEG_INF)
                pr = jnp.exp(s - lse_h)
                dp = lax.dot_general(dob, vs, NT, preferred_element_type=F32)
                ds = pr * (dp - delta)
                dsb = ds.astype(MXU_DTYPE)
                dv_ref[pl.ds(r0, blk), hh * HEAD_DIM:(hh + 1) * HEAD_DIM] += lax.dot_general(
                    pr.astype(MXU_DTYPE), dob, TN, preferred_element_type=F32)
                dk_ref[pl.ds(r0, blk), hh * dk:(hh + 1) * dk] += lax.dot_general(
                    dsb, q, TN, preferred_element_type=F32) * scale
                if bias:
                    dc_ref[hh * nb + j] -= jnp.sum(ds, axis=0, keepdims=True)
                    dcq = dcq + jnp.sum(ds, axis=1, keepdims=True)
                return dq + jnp.dot(dsb, ks, preferred_element_type=F32) * scale, dcq

            carry = lax.fori_loop(0, i, functools.partial(tile, masked=False),
                                  (jnp.zeros((blk, dk), F32), jnp.zeros((blk, 1), F32)))
            dq, dcq = tile(i, carry, True)
            dq_ref[:, hh * dk:(hh + 1) * dk] = dq
            if bias:
                dcq_ref[hh] = dcq

    in_specs = [pl.BlockSpec((blk, W), lambda p, i: (i, qcb + p)),
                pl.BlockSpec((S, W), lambda p, i: (0, kcb + p)),
                pl.BlockSpec((S, 128), lambda p, i: (0, vcb + p)),
                pl.BlockSpec((blk, 128), lambda p, i: (i, dcb + p)),
                pl.BlockSpec((blk, 128), lambda p, i: (i, p)),
                pl.BlockSpec((2, blk, 1), lambda p, i: (p, i, 0))]
    args = [q_arr, k_arr, v_arr, dmix, o_arr, lse]
    out_specs = [pl.BlockSpec((blk, W), lambda p, i: (i, p)),
                 pl.BlockSpec((S, W), lambda p, i: (0, p)),
                 pl.BlockSpec((S, 128), lambda p, i: (0, p))]
    out_shape = [jax.ShapeDtypeStruct((S, 4 * dk), F32), jax.ShapeDtypeStruct((S, 4 * dk), F32),
                 jax.ShapeDtypeStruct((S, GROUP_WIDTH), F32)]
    if bias:
        in_specs += [pl.BlockSpec((2, blk, 1), lambda p, i: (p, i, 0)),
                     pl.BlockSpec((4 * nb, 1, blk), lambda p, i: (0, 0, 0))]
        args += [cum_col, cum_row]
        out_specs += [pl.BlockSpec((2 * nb, 1, blk), lambda p, i: (p, 0, 0)), pl.BlockSpec((2, blk, 1), lambda p, i: (p, i, 0))]
        out_shape += [jax.ShapeDtypeStruct((4 * nb, 1, blk), F32), jax.ShapeDtypeStruct((4, S, 1), F32)]
    return pl.pallas_call(
        body, name=name, grid=(2, nb), in_specs=in_specs, out_specs=out_specs, out_shape=out_shape,
        compiler_params=_cparams("arbitrary", "arbitrary"),
    )(*args)


def _sb_tile(q, ks, scale, strict_mask, carry_l, tri_excl):
    z = lax.dot_general(q, ks, NT, preferred_element_type=F32) * scale
    lb = -(jnp.maximum(z, 0.0) + jnp.log(1.0 + jnp.exp(-jnp.abs(z))))
    if strict_mask is not None:
        lb = jnp.where(strict_mask, lb, 0.0)
    between = _dot01(lb, tri_excl) + carry_l
    a = jnp.exp(z + lb + between)
    if strict_mask is not None:
        a = jnp.where(strict_mask, a, 0.0)
    return z, lb, a


def _sb_attn_fwd(h_att, *, blk, name):
    S = h_att.shape[0]
    nb = S // blk
    scale = HEAD_DIM ** -0.5
    qcb, kcb, vcb = COL_SQ // 128, COL_SK // 128, COL_SV // 128

    def body(q_ref, k_ref, v_ref, o_ref, lt_ref):
        i = pl.program_id(1)
        row = lax.broadcasted_iota(jnp.int32, (blk, blk), 0)
        col = lax.broadcasted_iota(jnp.int32, (blk, blk), 1)
        strict = col < row
        tri_excl = (row > col).astype(MXU_DTYPE)
        for hh in range(2):
            sl = slice(hh * HEAD_DIM, (hh + 1) * HEAD_DIM)
            q = q_ref[:, sl]

            def tile(j, carry, mask, sl=sl, q=q):
                cl, acc = carry
                r0 = pl.multiple_of(j * blk, blk)
                _, lb, a = _sb_tile(q, k_ref[pl.ds(r0, blk), sl], scale, mask, cl, tri_excl)
                acc = acc + jnp.dot(a.astype(MXU_DTYPE), v_ref[pl.ds(r0, blk), sl], preferred_element_type=F32)
                return cl + jnp.sum(lb, axis=1, keepdims=True), acc

            carry = tile(i, (jnp.zeros((blk, 1), F32), jnp.zeros((blk, HEAD_DIM), F32)), strict)
            cl, acc = lax.fori_loop(0, i, lambda jj, c: tile(i - 1 - jj, c, None), carry)
            o_ref[:, sl] = acc
            lt_ref[hh] = cl

    return pl.pallas_call(
        body, name=name, grid=(2, nb),
        in_specs=[pl.BlockSpec((blk, 128), lambda p, i: (i, qcb + p)),
                  pl.BlockSpec((S, 128), lambda p, i: (0, kcb + p)),
                  pl.BlockSpec((S, 128), lambda p, i: (0, vcb + p))],
        out_specs=[pl.BlockSpec((blk, 128), lambda p, i: (i, p)), pl.BlockSpec((2, blk, 1), lambda p, i: (p, i, 0))],
        out_shape=[jax.ShapeDtypeStruct((S, GROUP_WIDTH), F32), jax.ShapeDtypeStruct((4, S, 1), F32)],
        compiler_params=_cparams("arbitrary", "arbitrary"),
    )(h_att, h_att, h_att)


def _sb_attn_bwd(h_att, dmix, ltot_arr, *, dcb, blk, name):
    S = h_att.shape[0]
    nb = S // blk
    scale = HEAD_DIM ** -0.5
    qcb, kcb, vcb = COL_SQ // 128, COL_SK // 128, COL_SV // 128

    def body(q_ref, k_ref, v_ref, do_ref, lt_ref, dq_ref, dk_ref, dv_ref):
        i = pl.program_id(1)

        @pl.when(i == 0)
        def _():
            dk_ref[...] = jnp.zeros_like(dk_ref)
            dv_ref[...] = jnp.zeros_like(dv_ref)

        row = lax.broadcasted_iota(jnp.int32, (blk, blk), 0)
        col = lax.broadcasted_iota(jnp.int32, (blk, blk), 1)
        strict = col < row
        up_incl = (row <= col).astype(MXU_DTYPE)
        up_excl = (row < col).astype(MXU_DTYPE)
        for hh in range(2):
            sl = slice(hh * HEAD_DIM, (hh + 1) * HEAD_DIM)
            q = q_ref[:, sl]
            dob = do_ref[:, sl].astype(MXU_DTYPE)
            ltot = lt_ref[hh]

            def tile(j, carry, mask, sl=sl, q=q, dob=dob, ltot=ltot):
                cl, cg, dq = carry
                r0 = pl.multiple_of(j * blk, blk)
                ks = k_ref[pl.ds(r0, blk), sl]
                vs = v_ref[pl.ds(r0, blk), sl]
                z = lax.dot_general(q, ks, NT, preferred_element_type=F32) * scale
                lb = -(jnp.maximum(z, 0.0) + jnp.log(1.0 + jnp.exp(-jnp.abs(z))))
                if mask is not None:
                    lb = jnp.where(mask, lb, 0.0)
                between = ltot - cl - _dot01(lb, up_incl)
                a = jnp.exp(z + lb + between)
                if mask is not None:
                    a = jnp.where(mask, a, 0.0)
                g = lax.dot_general(dob, vs, NT, preferred_element_type=F32) * a
                e = cg + _dot01(g, up_excl)
                dz = g * jnp.exp(lb) - e * jnp.exp(z + lb)
                if mask is not None:
                    dz = jnp.where(mask, dz, 0.0)
                dzb = dz.astype(MXU_DTYPE)
                dv_ref[pl.ds(r0, blk), sl] += lax.dot_general(a.astype(MXU_DTYPE), dob, TN, preferred_element_type=F32)
                dk_ref[pl.ds(r0, blk), sl] += lax.dot_general(dzb, q, TN, preferred_element_type=F32) * scale
                dq = dq + jnp.dot(dzb, ks, preferred_element_type=F32) * scale
                return cl + jnp.sum(lb, axis=1, keepdims=True), cg + jnp.sum(g, axis=1, keepdims=True), dq

            zc = jnp.zeros((blk, 1), F32)
            carry = lax.fori_loop(0, i, lambda j, c: tile(j, c, None), (zc, zc, jnp.zeros((blk, HEAD_DIM), F32)))
            _, _, dq = tile(i, carry, strict)
            dq_ref[:, sl] = dq

    return pl.pallas_call(
        body, name=name, grid=(2, nb),
        in_specs=[pl.BlockSpec((blk, 128), lambda p, i: (i, qcb + p)),
                  pl.BlockSpec((S, 128), lambda p, i: (0, kcb + p)),
                  pl.BlockSpec((S, 128), lambda p, i: (0, vcb + p)),
                  pl.BlockSpec((blk, 128), lambda p, i: (i, dcb + p)),
                  pl.BlockSpec((2, blk, 1), lambda p, i: (p, i, 0))],
        out_specs=[pl.BlockSpec((blk, 128), lambda p, i: (i, p)),
                   pl.BlockSpec((S, 128), lambda p, i: (0, p)),
                   pl.BlockSpec((S, 128), lambda p, i: (0, p))],
        out_shape=[jax.ShapeDtypeStruct((S, GROUP_WIDTH), F32)] * 3,
        compiler_params=_cparams("arbitrary", "arbitrary"),
    )(h_att, h_att, h_att, dmix, ltot_arr)


HP = 4


def _kv_blocks_t(a, blk):
    S, C = a.shape
    return a.reshape(S // blk, blk, C).transpose(0, 2, 1)


def _smax_fwd_t(qT, k, vT3, *, dk, blk, name):
    S = k.shape[0]
    nb = S // blk
    H = k.shape[1] // dk

    def body(qT_ref, k_ref, vT_ref, oT_ref, lse_ref):
        i = pl.program_id(1)
        key = lax.broadcasted_iota(jnp.int32, (blk, blk), 0)
        qry = lax.broadcasted_iota(jnp.int32, (blk, blk), 1)
        qs = [qT_ref[h * dk:(h + 1) * dk, :] for h in range(HP)]

        def tile(j, carry, masked):
            r0 = pl.multiple_of(j * blk, blk)
            ss = [jnp.dot(k_ref[pl.ds(r0, blk), h * dk:(h + 1) * dk], qs[h], preferred_element_type=F32)
                  for h in range(HP)]
            stats, pes = [], []
            for h in range(HP):
                m, l, _ = carry[h]
                s = jnp.where(key <= qry, ss[h], NEG_INF) if masked else ss[h]
                mn = jnp.maximum(m, jnp.max(s, axis=0, keepdims=True))
                a = jnp.exp(m - mn)
                pe = jnp.exp(s - mn)
                stats.append((mn, a * l + jnp.sum(pe, axis=0, keepdims=True), a))
                pes.append(pe.astype(MXU_DTYPE))
            pvs = [jnp.dot(vT_ref[j, h * HEAD_DIM:(h + 1) * HEAD_DIM, :], pes[h], preferred_element_type=F32)
                   for h in range(HP)]
            return tuple((stats[h][0], stats[h][1], stats[h][2] * carry[h][2] + pvs[h]) for h in range(HP))

        init = tuple((jnp.full((1, blk), NEG_INF, F32), jnp.zeros((1, blk), F32), jnp.zeros((HEAD_DIM, blk), F32))
                     for _ in range(HP))
        carry = lax.fori_loop(0, i, functools.partial(tile, masked=False), init)
        carry = tile(i, carry, True)
        for h in range(HP):
            m, l, acc = carry[h]
            oT_ref[h * HEAD_DIM:(h + 1) * HEAD_DIM, :] = acc / l
            lse_ref[h, 0] = m + jnp.log(l)

    return pl.pallas_call(
        body, name=name, grid=(H // HP, nb),
        in_specs=[pl.BlockSpec((HP * dk, blk), lambda p, i: (p, i)),
                  pl.BlockSpec((S, HP * dk), lambda p, i: (0, p)),
                  pl.BlockSpec((nb, HP * HEAD_DIM, blk), lambda p, i: (0, p, 0))],
        out_specs=[pl.BlockSpec((HP * HEAD_DIM, blk), lambda p, i: (p, i)),
                   pl.BlockSpec((HP, 1, 1, blk), lambda p, i: (p, i, 0, 0))],
        out_shape=[jax.ShapeDtypeStruct((H * HEAD_DIM, S), F32), jax.ShapeDtypeStruct((H, nb, 1, blk), F32)],
        compiler_params=_cparams("arbitrary", "arbitrary"),
    )(qT, k, vT3)


def _smax_bwd_t(qT, q, k, kT3, v, dmix, dmixT, oT, lse, *, dk, dcb, qscale, blk, name):
    S = k.shape[0]
    nb = S // blk
    H = k.shape[1] // dk
    hd = HP * HEAD_DIM
    dcr = dcb * 128 // hd

    def body(qT_ref, q_ref, k_ref, kT_ref, v_ref, do_ref, doT_ref, oT_ref, lse_ref, dqT_ref, dk_ref, dv_ref):
        i = pl.program_id(1)

        @pl.when(i == 0)
        def _():
            dk_ref[...] = jnp.zeros_like(dk_ref)
            dv_ref[...] = jnp.zeros_like(dv_ref)

        key = lax.broadcasted_iota(jnp.int32, (blk, blk), 0)
        qry = lax.broadcasted_iota(jnp.int32, (blk, blk), 1)
        per_head = []
        for h in range(HP):
            hs = slice(h * HEAD_DIM, (h + 1) * HEAD_DIM)
            doT = doT_ref[hs, :]
            per_head.append(dict(
                qT=qT_ref[h * dk:(h + 1) * dk, :], q=q_ref[:, h * dk:(h + 1) * dk],
                doT=doT.astype(MXU_DTYPE), do=do_ref[:, hs].astype(MXU_DTYPE),
                delta=jnp.sum(doT * oT_ref[hs, :], axis=0, keepdims=True), lse=lse_ref[h, 0]))

        def tile(j, dqs, masked):
            r0 = pl.multiple_of(j * blk, blk)
            rows = pl.ds(r0, blk)
            ksl = [slice(h * dk, (h + 1) * dk) for h in range(HP)]
            hsl = [slice(h * HEAD_DIM, (h + 1) * HEAD_DIM) for h in range(HP)]
            ss = [jnp.dot(k_ref[rows, ksl[h]], per_head[h]["qT"], preferred_element_type=F32) for h in range(HP)]
            dps = [jnp.dot(v_ref[rows, hsl[h]], per_head[h]["doT"], preferred_element_type=F32) for h in range(HP)]
            prs, dss = [], []
            for h in range(HP):
                c = per_head[h]
                s = jnp.where(key <= qry, ss[h], NEG_INF) if masked else ss[h]
                pr = jnp.exp(s - c["lse"])
                dss.append((pr * (dps[h] - c["delta"])).astype(MXU_DTYPE))
                prs.append(pr.astype(MXU_DTYPE))
            for h in range(HP):
                dv_ref[rows, hsl[h]] += jnp.dot(prs[h], per_head[h]["do"], preferred_element_type=F32)
            for h in range(HP):
                dk_ref[rows, ksl[h]] += jnp.dot(dss[h], per_head[h]["q"], preferred_element_type=F32)
            return tuple(dqs[h] + jnp.dot(kT_ref[j, ksl[h], :], dss[h], preferred_element_type=F32) for h in range(HP))

        dqs = lax.fori_loop(0, i, functools.partial(tile, masked=False),
                            tuple(jnp.zeros((dk, blk), F32) for _ in range(HP)))
        dqs = tile(i, dqs, True)
        for h in range(HP):
            dqT_ref[h * dk:(h + 1) * dk, :] = dqs[h] * qscale

    return pl.pallas_call(
        body, name=name, grid=(H // HP, nb),
        in_specs=[pl.BlockSpec((HP * dk, blk), lambda p, i: (p, i)),
                  pl.BlockSpec((blk, HP * dk), lambda p, i: (i, p)),
                  pl.BlockSpec((S, HP * dk), lambda p, i: (0, p)),
                  pl.BlockSpec((nb, HP * dk, blk), lambda p, i: (0, p, 0)),
                  pl.BlockSpec((S, hd), lambda p, i: (0, p)),
                  pl.BlockSpec((blk, hd), lambda p, i: (i, dcr + p)),
                  pl.BlockSpec((hd, blk), lambda p, i: (dcr + p, i)),
                  pl.BlockSpec((hd, blk), lambda p, i: (p, i)),
                  pl.BlockSpec((HP, 1, 1, blk), lambda p, i: (p, i, 0, 0))],
        out_specs=[pl.BlockSpec((HP * dk, blk), lambda p, i: (p, i)),
                   pl.BlockSpec((S, HP * dk), lambda p, i: (0, p)),
                   pl.BlockSpec((S, hd), lambda p, i: (0, p))],
        out_shape=[jax.ShapeDtypeStruct((H * dk, S), F32), jax.ShapeDtypeStruct((S, H * dk), F32),
                   jax.ShapeDtypeStruct((S, H * HEAD_DIM), F32)],
        compiler_params=_cparams("arbitrary", "arbitrary"),
    )(qT, q, k, kT3, v, dmix, dmixT, oT, lse)


def _log1m_beta(z):
    return -(jnp.maximum(z, 0.0) + jnp.log(1.0 + jnp.exp(-jnp.abs(z))))


def _dot01_left(m01, x, parts=2):
    acc = None
    rem = x
    for _ in range(parts):
        part = rem.astype(MXU_DTYPE)
        rem = rem - part.astype(F32)
        t = jnp.dot(m01, part, preferred_element_type=F32)
        acc = t if acc is None else acc + t
    return acc


def _sb_fwd_t(qT, h_att, vT3, *, blk, name):
    S = h_att.shape[0]
    nb = S // blk
    kcb = COL_SK // (HP * HEAD_DIM)

    def body(qT_ref, k_ref, vT_ref, oT_ref, lt_ref):
        i = pl.program_id(1)
        key = lax.broadcasted_iota(jnp.int32, (blk, blk), 0)
        qry = lax.broadcasted_iota(jnp.int32, (blk, blk), 1)
        strict = key < qry
        later = (qry > key).astype(MXU_DTYPE)
        qs = [qT_ref[h * HEAD_DIM:(h + 1) * HEAD_DIM, :] for h in range(HP)]

        def tile(j, carry, mask):
            r0 = pl.multiple_of(j * blk, blk)
            hsl = [slice(h * HEAD_DIM, (h + 1) * HEAD_DIM) for h in range(HP)]
            zs = [jnp.dot(k_ref[pl.ds(r0, blk), hsl[h]], qs[h], preferred_element_type=F32) for h in range(HP)]
            lbs = []
            for h in range(HP):
                lb = _log1m_beta(zs[h])
                lbs.append(lb if mask is None else jnp.where(mask, lb, 0.0))
            sums = [_dot01_left(later, lbs[h]) for h in range(HP)]
            probs = []
            for h in range(HP):
                a = jnp.exp(zs[h] + lbs[h] + sums[h] + carry[h][0])
                probs.append((a if mask is None else jnp.where(mask, a, 0.0)).astype(MXU_DTYPE))
            pvs = [jnp.dot(vT_ref[j, hsl[h], :], probs[h], preferred_element_type=F32) for h in range(HP)]
            return tuple((carry[h][0] + jnp.sum(lbs[h], axis=0, keepdims=True), carry[h][1] + pvs[h]) for h in range(HP))

        init = tuple((jnp.zeros((1, blk), F32), jnp.zeros((HEAD_DIM, blk), F32)) for _ in range(HP))
        carry = tile(i, init, strict)
        carry = lax.fori_loop(0, i, lambda jj, c: tile(i - 1 - jj, c, None), carry)
        for h in range(HP):
            oT_ref[h * HEAD_DIM:(h + 1) * HEAD_DIM, :] = carry[h][1]
            lt_ref[h, 0] = carry[h][0]

    hd = HP * HEAD_DIM
    return pl.pallas_call(
        body, name=name, grid=(4 // HP, nb),
        in_specs=[pl.BlockSpec((hd, blk), lambda p, i: (p, i)),
                  pl.BlockSpec((S, hd), lambda p, i: (0, kcb + p)),
                  pl.BlockSpec((nb, hd, blk), lambda p, i: (0, p, 0))],
        out_specs=[pl.BlockSpec((hd, blk), lambda p, i: (p, i)), pl.BlockSpec((HP, 1, 1, blk), lambda p, i: (p, i, 0, 0))],
        out_shape=[jax.ShapeDtypeStruct((GROUP_WIDTH, S), F32), jax.ShapeDtypeStruct((4, nb, 1, blk), F32)],
        compiler_params=_cparams("arbitrary", "arbitrary"),
    )(qT, h_att, vT3)


def _sb_bwd_t(qT, h_att, kT3, dmix, dmixT, ltot, *, dcb, qscale, blk, name):
    S = h_att.shape[0]
    nb = S // blk
    hd = HP * HEAD_DIM
    qcb, kcb, vcb = COL_SQ // hd, COL_SK // hd, COL_SV // hd
    dcr = dcb * 128 // hd

    def body(qT_ref, q_ref, k_ref, kT_ref, v_ref, do_ref, doT_ref, lt_ref, dqT_ref, dk_ref, dv_ref):
        i = pl.program_id(1)

        @pl.when(i == 0)
        def _():
            dk_ref[...] = jnp.zeros_like(dk_ref)
            dv_ref[...] = jnp.zeros_like(dv_ref)

        key = lax.broadcasted_iota(jnp.int32, (blk, blk), 0)
        qry = lax.broadcasted_iota(jnp.int32, (blk, blk), 1)
        strict = key < qry
        upto = (qry <= key).astype(MXU_DTYPE)
        before = (qry < key).astype(MXU_DTYPE)
        per_head = []
        for h in range(HP):
            hs = slice(h * HEAD_DIM, (h + 1) * HEAD_DIM)
            per_head.append(dict(qT=qT_ref[hs, :], q=q_ref[:, hs], doT=doT_ref[hs, :].astype(MXU_DTYPE),
                                 do=do_ref[:, hs].astype(MXU_DTYPE), lt=lt_ref[h, 0]))

        def tile(j, carry, mask):
            r0 = pl.multiple_of(j * blk, blk)
            rows = pl.ds(r0, blk)
            hsl = [slice(h * HEAD_DIM, (h + 1) * HEAD_DIM) for h in range(HP)]
            zs = [jnp.dot(k_ref[rows, hsl[h]], per_head[h]["qT"], preferred_element_type=F32) for h in range(HP)]
            das = [jnp.dot(v_ref[rows, hsl[h]], per_head[h]["doT"], preferred_element_type=F32) for h in range(HP)]
            lbs = []
            for h in range(HP):
                lb = _log1m_beta(zs[h])
                lbs.append(lb if mask is None else jnp.where(mask, lb, 0.0))
            sums = [_dot01_left(upto, lbs[h]) for h in range(HP)]
            probs, gs = [], []
            for h in range(HP):
                a = jnp.exp(zs[h] + lbs[h] + (per_head[h]["lt"] - carry[h][0] - sums[h]))
                a = a if mask is None else jnp.where(mask, a, 0.0)
                gs.append(das[h] * a)
                probs.append(a.astype(MXU_DTYPE))
            for h in range(HP):
                dv_ref[rows, hsl[h]] += jnp.dot(probs[h], per_head[h]["do"], preferred_element_type=F32)
            es = [_dot01_left(before, gs[h]) for h in range(HP)]
            dzs = []
            for h in range(HP):
                dz = gs[h] * jnp.exp(lbs[h]) - (carry[h][1] + es[h]) * jnp.exp(zs[h] + lbs[h])
                dzs.append((dz if mask is None else jnp.where(mask, dz, 0.0)).astype(MXU_DTYPE))
            for h in range(HP):
                dk_ref[rows, hsl[h]] += jnp.dot(dzs[h], per_head[h]["q"], preferred_element_type=F32)
            return tuple((carry[h][0] + jnp.sum(lbs[h], axis=0, keepdims=True),
                          carry[h][1] + jnp.sum(gs[h], axis=0, keepdims=True),
                          carry[h][2] + jnp.dot(kT_ref[j, hsl[h], :], dzs[h], preferred_element_type=F32))
                         for h in range(HP))

        zr = jnp.zeros((1, blk), F32)
        init = tuple((zr, zr, jnp.zeros((HEAD_DIM, blk), F32)) for _ in range(HP))
        carry = lax.fori_loop(0, i, lambda j, c: tile(j, c, None), init)
        carry = tile(i, carry, strict)
        for h in range(HP):
            dqT_ref[h * HEAD_DIM:(h + 1) * HEAD_DIM, :] = carry[h][2] * qscale

    return pl.pallas_call(
        body, name=name, grid=(4 // HP, nb),
        in_specs=[pl.BlockSpec((hd, blk), lambda p, i: (p, i)),
                  pl.BlockSpec((blk, hd), lambda p, i: (i, qcb + p)),
                  pl.BlockSpec((S, hd), lambda p, i: (0, kcb + p)),
                  pl.BlockSpec((nb, hd, blk), lambda p, i: (0, p, 0)),
                  pl.BlockSpec((S, hd), lambda p, i: (0, vcb + p)),
                  pl.BlockSpec((blk, hd), lambda p, i: (i, dcr + p)),
                  pl.BlockSpec((hd, blk), lambda p, i: (dcr + p, i)),
                  pl.BlockSpec((HP, 1, 1, blk), lambda p, i: (p, i, 0, 0))],
        out_specs=[pl.BlockSpec((hd, blk), lambda p, i: (p, i)),
                   pl.BlockSpec((S, hd), lambda p, i: (0, p)),
                   pl.BlockSpec((S, hd), lambda p, i: (0, p))],
        out_shape=[jax.ShapeDtypeStruct((GROUP_WIDTH, S), F32), jax.ShapeDtypeStruct((S, GROUP_WIDTH), F32),
                   jax.ShapeDtypeStruct((S, GROUP_WIDTH), F32)],
        compiler_params=_cparams("arbitrary", "arbitrary"),
    )(qT, h_att, h_att, kT3, h_att, dmix, dmixT, ltot)


def _swa_scores(q_ref, k_ref, n, h, start):
    g = h // 2
    kb = k_ref[pl.ds(start, 2 * WINDOW), g * HEAD_DIM:(g + 1) * HEAD_DIM]
    s = lax.dot_general(q_ref[:, h * HEAD_DIM:(h + 1) * HEAD_DIM], kb, NT, preferred_element_type=F32) * (HEAD_DIM ** -0.5)
    dist = (n * WINDOW + lax.broadcasted_iota(jnp.int32, (WINDOW, 2 * WINDOW), 0)
            - start - lax.broadcasted_iota(jnp.int32, (WINDOW, 2 * WINDOW), 1))
    s = s - SWA_SLOPES[h] * dist.astype(F32)
    valid = (dist >= 0) & (dist < WINDOW)
    return jnp.where(valid, s, NEG_INF), kb


def _swa_fwd(h_att, sinks, *, name):
    S = h_att.shape[0]
    nb = S // WINDOW
    qcb, kcb, vcb = COL_WQ // 256, COL_WK // 128, COL_WV // 128

    def body(sink_ref, q_ref, k_ref, v_ref, o_ref, lse_ref):
        n = pl.program_id(0)
        start = pl.multiple_of(jnp.maximum(n - 1, 0) * WINDOW, WINDOW)
        for h in range(4):
            g = h // 2
            s, _ = _swa_scores(q_ref, k_ref, n, h, start)
            sink = sink_ref[h]
            m = jnp.maximum(jnp.max(s, axis=1, keepdims=True), sink)
            e = jnp.exp(s - m)
            den = jnp.sum(e, axis=1, keepdims=True) + jnp.exp(sink - m)
            vb = v_ref[pl.ds(start, 2 * WINDOW), g * HEAD_DIM:(g + 1) * HEAD_DIM]
            o_ref[:, h * HEAD_DIM:(h + 1) * HEAD_DIM] = jnp.dot((e / den).astype(MXU_DTYPE), vb, preferred_element_type=F32)
            lse_ref[h] = m + jnp.log(den)

    return pl.pallas_call(
        body, name=name, grid=(nb,),
        in_specs=[pl.BlockSpec(memory_space=pltpu.SMEM),
                  pl.BlockSpec((WINDOW, 256), lambda n: (n, qcb)),
                  pl.BlockSpec((S, 128), lambda n: (0, kcb)),
                  pl.BlockSpec((S, 128), lambda n: (0, vcb))],
        out_specs=[pl.BlockSpec((WINDOW, 256), lambda n: (n, 0)), pl.BlockSpec((4, WINDOW, 1), lambda n: (0, n, 0))],
        out_shape=[jax.ShapeDtypeStruct((S, GROUP_WIDTH), F32), jax.ShapeDtypeStruct((4, S, 1), F32)],
        compiler_params=_cparams("arbitrary"),
    )(sinks, h_att, h_att, h_att)


def _swa_bwd(h_att, sinks, dmix, o_arr, lse, *, dcb, name):
    S = h_att.shape[0]
    nb = S // WINDOW
    qcb, kcb, vcb = COL_WQ // 256, COL_WK // 128, COL_WV // 128

    def body(sink_ref, q_ref, k_ref, v_ref, do_ref, o_ref, lse_ref, dq_ref, dk_ref, dv_ref, dsink_ref):
        n = pl.program_id(0)

        @pl.when(n == 0)
        def _():
            dk_ref[...] = jnp.zeros_like(dk_ref)
            dv_ref[...] = jnp.zeros_like(dv_ref)
            dsink_ref[...] = jnp.zeros_like(dsink_ref)

        start = pl.multiple_of(jnp.maximum(n - 1, 0) * WINDOW, WINDOW)
        for h in range(4):
            g = h // 2
            sl = slice(h * HEAD_DIM, (h + 1) * HEAD_DIM)
            gl = slice(g * HEAD_DIM, (g + 1) * HEAD_DIM)
            s, kb = _swa_scores(q_ref, k_ref, n, h, start)
            lse_h = lse_ref[h]
            pr = jnp.exp(s - lse_h)
            do = do_ref[:, sl]
            dob = do.astype(MXU_DTYPE)
            delta = jnp.sum(do * o_ref[:, sl], axis=1, keepdims=True)
            vb = v_ref[pl.ds(start, 2 * WINDOW), gl]
            ds = pr * (lax.dot_general(dob, vb, NT, preferred_element_type=F32) - delta)
            dsb = ds.astype(MXU_DTYPE)
            dq_ref[:, sl] = jnp.dot(dsb, kb, preferred_element_type=F32) * (HEAD_DIM ** -0.5)
            dk_ref[pl.ds(start, 2 * WINDOW), gl] += lax.dot_general(
                dsb, q_ref[:, sl], TN, preferred_element_type=F32) * (HEAD_DIM ** -0.5)
            dv_ref[pl.ds(start, 2 * WINDOW), gl] += lax.dot_general(pr.astype(MXU_DTYPE), dob, TN, preferred_element_type=F32)
            dsink_ref[h:h + 1, :] += jnp.zeros((1, 128), F32) - jnp.sum(jnp.exp(sink_ref[h] - lse_h) * delta)

    return pl.pallas_call(
        body, name=name, grid=(nb,),
        in_specs=[pl.BlockSpec(memory_space=pltpu.SMEM),
                  pl.BlockSpec((WINDOW, 256), lambda n: (n, qcb)),
                  pl.BlockSpec((S, 128), lambda n: (0, kcb)),
                  pl.BlockSpec((S, 128), lambda n: (0, vcb)),
                  pl.BlockSpec((WINDOW, 256), lambda n: (n, dcb)),
                  pl.BlockSpec((WINDOW, 256), lambda n: (n, 0)),
                  pl.BlockSpec((4, WINDOW, 1), lambda n: (0, n, 0))],
        out_specs=[pl.BlockSpec((WINDOW, 256), lambda n: (n, 0)),
                   pl.BlockSpec((S, 128), lambda n: (0, 0)),
                   pl.BlockSpec((S, 128), lambda n: (0, 0)),
                   pl.BlockSpec((4, 128), lambda n: (0, 0))],
        out_shape=[jax.ShapeDtypeStruct((S, GROUP_WIDTH), F32), jax.ShapeDtypeStruct((S, 128), F32),
                   jax.ShapeDtypeStruct((S, 128), F32), jax.ShapeDtypeStruct((4, 128), F32)],
        compiler_params=_cparams("arbitrary"),
    )(sinks, h_att, h_att, h_att, dmix, o_arr, lse)


def _tri(n, incl, upper):
    r = lax.broadcasted_iota(jnp.int32, (n, n), 0)
    c = lax.broadcasted_iota(jnp.int32, (n, n), 1)
    if upper:
        m = (r <= c) if incl else (r < c)
    else:
        m = (r >= c) if incl else (r > c)
    return m.astype(MXU_DTYPE)


def _fox_gate_fwd(fg, b_f, *, name):
    _, R, _ = fg.shape

    def body(b_ref, fg_ref, pos_ref, neg_ref):
        up_incl = _tri(128, True, True)
        ones = jnp.ones((128, 128), MXU_DTYPE)
        for h in range(4):
            z = fg_ref[h] + b_ref[h]
            logf = jnp.minimum(z, 0.0) - jnp.log(1.0 + jnp.exp(-jnp.abs(z)))
            within = _dot01(logf, up_incl, parts=3)
            totals = _dot01(logf, ones, parts=3)
            rem = within + _rows_other(totals, R, after=False)
            for part in range(3):
                piece = rem.astype(MXU_DTYPE)
                rem = rem - piece.astype(F32)
                pos_ref[h, part] = piece
                neg_ref[h, part] = -piece

    shape = (4, 3) + fg.shape[1:]
    return pl.pallas_call(
        body, name=name,
        in_specs=[pl.BlockSpec(memory_space=pltpu.SMEM), pl.BlockSpec(memory_space=pltpu.VMEM)],
        out_specs=[pl.BlockSpec(memory_space=pltpu.VMEM)] * 2,
        out_shape=[jax.ShapeDtypeStruct(shape, MXU_DTYPE)] * 2,
    )(b_f, fg)


def _rows_other(totals, n, after):
    r = lax.broadcasted_iota(jnp.int32, (n, n), 0)
    c = lax.broadcasted_iota(jnp.int32, (n, n), 1)
    m = ((c > r) if after else (c < r)).astype(MXU_DTYPE)
    acc = None
    rem = totals
    for _ in range(3):
        part = rem.astype(MXU_DTYPE)
        rem = rem - part.astype(F32)
        t = jnp.dot(m, part, preferred_element_type=F32)
        acc = t if acc is None else acc + t
    return acc


def _fox_gate_bwd(fg, b_f, dcum_k, dcum_q, *, q_unscale, name):
    _, R, _ = fg.shape

    def body(b_ref, fg_ref, dck_ref, dcq_ref, dfg_ref, db_ref):
        low_incl = _tri(128, True, False)
        ones = jnp.ones((128, 128), MXU_DTYPE)
        for h in range(4):
            dc = dcq_ref[h] * q_unscale - dck_ref[h]
            dlogf = _dot01(dc, low_incl, parts=3) + _rows_other(_dot01(dc, ones, parts=3), R, after=True)
            z = fg_ref[h] + b_ref[h]
            dz = dlogf * jnp.exp(jnp.minimum(-z, 0.0) - jnp.log(1.0 + jnp.exp(-jnp.abs(z))))
            dfg_ref[h] = dz
            db_ref[h:h + 1, :] = jnp.zeros((1, 128), F32) + jnp.sum(dz)

    return pl.pallas_call(
        body, name=name,
        in_specs=[pl.BlockSpec(memory_space=pltpu.SMEM)] + [pl.BlockSpec(memory_space=pltpu.VMEM)] * 3,
        out_specs=[pl.BlockSpec(memory_space=pltpu.VMEM), pl.BlockSpec(memory_space=pltpu.VMEM)],
        out_shape=[jax.ShapeDtypeStruct(fg.shape, F32), jax.ShapeDtypeStruct((4, 128), F32)],
    )(b_f, fg, dcum_k, dcum_q)


def _rope_rot(transpose):
    r = lax.broadcasted_iota(jnp.int32, (MLA_PAD, MLA_PAD), 0)
    c = lax.broadcasted_iota(jnp.int32, (MLA_PAD, MLA_PAD), 1)
    if transpose:
        r, c = c, r
    half = MLA_ROPE // 2
    lo, mid, hi = HEAD_DIM, HEAD_DIM + half, HEAD_DIM + MLA_ROPE
    minus = (c >= lo) & (c < mid) & (r == c + half)
    plus = (c >= mid) & (c < hi) & (r == c - half)
    return jnp.where(plus, 1.0, jnp.where(minus, -1.0, 0.0)).astype(MXU_DTYPE)


def _rope_lanes():
    lane = lax.broadcasted_iota(jnp.int32, (1, MLA_PAD), 1)
    return ((lane >= HEAD_DIM) & (lane < HEAD_DIM + MLA_ROPE)).astype(F32)


def _rms(x, g, eps=1e-6):
    r = lax.rsqrt(jnp.mean(x * x, axis=-1, keepdims=True) + eps)
    return x * r * g, r


def _rms_bwd(dy, x, r, g):
    xh = x * r
    dxh = dy * g
    dx = r * (dxh - xh * jnp.mean(dxh * xh, axis=-1, keepdims=True))
    return dx, dy * xh


def _mla_prep_fwd(lat, g_q, g_kv, wuq, wuk, wuv, cosm, sinm, *, bs, name):
    S = lat.shape[0]

    def body(lat_ref, gq_ref, gkv_ref, wuq_ref, wuk_ref, wuv_ref, cos_ref, sin_ref, q_ref, k_ref, v_ref):
        rot = _rope_rot(False)
        cosm_, sinm_ = cos_ref[...], sin_ref[...]
        nq, _ = _rms(lat_ref[:, 0:MLA_Q_RANK], gq_ref[...])
        nkv, _ = _rms(lat_ref[:, MLA_Q_RANK:MLA_Q_RANK + MLA_KV_RANK], gkv_ref[...])
        qlat = jnp.dot(nq.astype(MXU_DTYPE), wuq_ref[...], preferred_element_type=F32)
        klat = jnp.dot(nkv.astype(MXU_DTYPE), wuk_ref[...], preferred_element_type=F32)
        v_ref[...] = jnp.dot(nkv.astype(MXU_DTYPE), wuv_ref[...], preferred_element_type=F32).astype(v_ref.dtype)
        krb = lat_ref[:, 384:512]
        kr = krb * (cosm_ * _rope_lanes()) + _dot01(krb, rot, parts=3) * sinm_
        for h in range(4):
            sl = slice(h * MLA_PAD, (h + 1) * MLA_PAD)
            qh = qlat[:, sl]
            q_ref[:, sl] = ((qh * cosm_ + _dot01(qh, rot, parts=3) * sinm_) * (MLA_QK ** -0.5)).astype(q_ref.dtype)
            k_ref[:, sl] = (klat[:, sl] + kr).astype(k_ref.dtype)

    full = lambda a: pl.BlockSpec(a.shape, lambda i: (0,) * a.ndim)
    return pl.pallas_call(
        body, name=name, grid=(S // bs,),
        in_specs=[pl.BlockSpec((bs, LAT_W), lambda i: (i, 0)), full(g_q), full(g_kv), full(wuq), full(wuk), full(wuv),
                  pl.BlockSpec((bs, MLA_PAD), lambda i: (i, 0)), pl.BlockSpec((bs, MLA_PAD), lambda i: (i, 0))],
        out_specs=[pl.BlockSpec((bs, 512), lambda i: (i, 0)), pl.BlockSpec((bs, 512), lambda i: (i, 0)),
                   pl.BlockSpec((bs, 256), lambda i: (i, 0))],
        out_shape=[jax.ShapeDtypeStruct((S, 512), MXU_DTYPE), jax.ShapeDtypeStruct((S, 512), MXU_DTYPE),
                   jax.ShapeDtypeStruct((S, 256), MXU_DTYPE)],
        compiler_params=_cparams("parallel"),
    )(lat, g_q, g_kv, wuq, wuk, wuv, cosm, sinm)


def _mla_prep_bwd(lat, g_q, g_kv, wuq, wuk, wuv, cosm, sinm, dq, dk, dv, *, bs, name):
    S = lat.shape[0]

    def body(lat_ref, gq_ref, gkv_ref, wuq_ref, wuk_ref, wuv_ref, cos_ref, sin_ref, dq_ref, dk_ref, dv_ref,
             dlat_ref, dwuq_ref, dwuk_ref, dwuv_ref, dgq_ref, dgkv_ref):
        @pl.when(pl.program_id(0) == 0)
        def _():
            for r in (dwuq_ref, dwuk_ref, dwuv_ref, dgq_ref, dgkv_ref):
                r[...] = jnp.zeros_like(r)

        rot_t = _rope_rot(True)
        cosm_, sinm_ = cos_ref[...], sin_ref[...]
        cq = lat_ref[:, 0:MLA_Q_RANK]
        ckv = lat_ref[:, MLA_Q_RANK:MLA_Q_RANK + MLA_KV_RANK]
        nq, rq = _rms(cq, gq_ref[...])
        nkv, rkv = _rms(ckv, gkv_ref[...])
        nqb, nkvb = nq.astype(MXU_DTYPE), nkv.astype(MXU_DTYPE)

        dqlat = []
        dkr = jnp.zeros((bs, MLA_PAD), F32)
        for h in range(4):
            sl = slice(h * MLA_PAD, (h + 1) * MLA_PAD)
            dqh = dq_ref[:, sl]
            dqlat.append(dqh * cosm_ + _dot01(dqh * sinm_, rot_t, parts=3))
            dkr = dkr + dk_ref[:, sl]
        dqlat = jnp.concatenate(dqlat, axis=1).astype(MXU_DTYPE)
        dkb = dk_ref[...].astype(MXU_DTYPE)
        dvb = dv_ref[...].astype(MXU_DTYPE)

        dnq = lax.dot_general(dqlat, wuq_ref[...], NT, preferred_element_type=F32)
        dnkv = (lax.dot_general(dkb, wuk_ref[...], NT, preferred_element_type=F32)
                + lax.dot_general(dvb, wuv_ref[...], NT, preferred_element_type=F32))
        dwuq_ref[...] += lax.dot_general(nqb, dqlat, TN, preferred_element_type=F32)
        dwuk_ref[...] += lax.dot_general(nkvb, dkb, TN, preferred_element_type=F32)
        dwuv_ref[...] += lax.dot_general(nkvb, dvb, TN, preferred_element_type=F32)
        dcq, tq = _rms_bwd(dnq, cq, rq, gq_ref[...])
        dckv, tkv = _rms_bwd(dnkv, ckv, rkv, gkv_ref[...])
        dgq_ref[...] += jnp.sum(tq, axis=0, keepdims=True)
        dgkv_ref[...] += jnp.sum(tkv, axis=0, keepdims=True)
        dlat_ref[:, 0:MLA_Q_RANK] = dcq.astype(dlat_ref.dtype)
        dlat_ref[:, MLA_Q_RANK:MLA_Q_RANK + MLA_KV_RANK] = dckv.astype(dlat_ref.dtype)
        dkrb = dkr * (cosm_ * _rope_lanes()) + _dot01(dkr * sinm_, rot_t, parts=3)
        dlat_ref[:, 384:512] = dkrb.astype(dlat_ref.dtype)

    full = lambda a: pl.BlockSpec(a.shape, lambda i: (0,) * a.ndim)
    row = lambda w: pl.BlockSpec((bs, w), lambda i: (i, 0))
    acc = lambda *shape: pl.BlockSpec(shape, lambda i: (0,) * len(shape))
    return pl.pallas_call(
        body, name=name, grid=(S // bs,),
        in_specs=[row(LAT_W), full(g_q), full(g_kv), full(wuq), full(wuk), full(wuv), row(MLA_PAD), row(MLA_PAD),
                  row(512), row(512), row(256)],
        out_specs=[row(512), acc(256, 512), acc(128, 512), acc(128, 256), acc(1, 256), acc(1, 128)],
        out_shape=[jax.ShapeDtypeStruct((S, 512), MXU_DTYPE), jax.ShapeDtypeStruct((256, 512), F32),
                   jax.ShapeDtypeStruct((128, 512), F32), jax.ShapeDtypeStruct((128, 256), F32),
                   jax.ShapeDtypeStruct((1, 256), F32), jax.ShapeDtypeStruct((1, 128), F32)],
        compiler_params=_cparams("arbitrary"),
    )(lat, g_q, g_kv, wuq, wuk, wuv, cosm, sinm, dq, dk, dv)


def _row_spec(bs, w):
    return pl.BlockSpec((bs, w), lambda i: (i, 0))


def _vec_spec(w):
    return pl.BlockSpec((1, w), lambda i: (0, 0))


def _gnorm_fwd(outs, g, *, bs, name):
    S = outs[0].shape[0]

    def body(a_ref, b_ref, c_ref, d_ref, g_ref, o_ref):
        for k, ref in enumerate((a_ref, b_ref, c_ref, d_ref)):
            sl = slice(k * GROUP_WIDTH, (k + 1) * GROUP_WIDTH)
            y, _ = _rms(ref[...], g_ref[:, sl])
            o_ref[:, sl] = y.astype(o_ref.dtype)

    return pl.pallas_call(
        body, name=name, grid=(S // bs,),
        in_specs=[_row_spec(bs, GROUP_WIDTH)] * 4 + [_vec_spec(D_MODEL)],
        out_specs=_row_spec(bs, D_MODEL), out_shape=jax.ShapeDtypeStruct((S, D_MODEL), MXU_DTYPE),
        compiler_params=_cparams("parallel"),
    )(*outs, g)


def _gnorm_bwd(dgn, outs, g, *, bs, name):
    S = dgn.shape[0]

    def body(dgn_ref, a_ref, b_ref, c_ref, d_ref, g_ref, dmix_ref, dg_ref):
        @pl.when(pl.program_id(0) == 0)
        def _():
            dg_ref[...] = jnp.zeros_like(dg_ref)

        for k, ref in enumerate((a_ref, b_ref, c_ref, d_ref)):
            sl = slice(k * GROUP_WIDTH, (k + 1) * GROUP_WIDTH)
            x = ref[...]
            _, r = _rms(x, g_ref[:, sl])
            dx, t = _rms_bwd(dgn_ref[:, sl], x, r, g_ref[:, sl])
            dmix_ref[:, sl] = dx
            dg_ref[:, sl] += jnp.sum(t, axis=0, keepdims=True)

    return pl.pallas_call(
        body, name=name, grid=(S // bs,),
        in_specs=[_row_spec(bs, D_MODEL)] + [_row_spec(bs, GROUP_WIDTH)] * 4 + [_vec_spec(D_MODEL)],
        out_specs=[_row_spec(bs, D_MODEL), _vec_spec(D_MODEL)],
        out_shape=[jax.ShapeDtypeStruct((S, D_MODEL), F32), jax.ShapeDtypeStruct((1, D_MODEL), F32)],
        compiler_params=_cparams("arbitrary"),
    )(dgn, *outs, g)


def _ln_fwd(u, g, b, *, bs, name):
    S = u.shape[0]

    def body(u_ref, g_ref, b_ref, y_ref, yb_ref, xh_ref, rs_ref):
        x = u_ref[...]
        mu = jnp.mean(x, axis=-1, keepdims=True)
        xc = x - mu
        rs = lax.rsqrt(jnp.mean(xc * xc, axis=-1, keepdims=True) + 1e-5)
        xh = xc * rs
        y = xh * g_ref[...] + b_ref[...]
        y_ref[...] = y
        yb_ref[...] = y.astype(yb_ref.dtype)
        xh_ref[...] = xh
        rs_ref[...] = rs

    return pl.pallas_call(
        body, name=name, grid=(S // bs,),
        in_specs=[_row_spec(bs, D_MODEL), _vec_spec(D_MODEL), _vec_spec(D_MODEL)],
        out_specs=[_row_spec(bs, D_MODEL), _row_spec(bs, D_MODEL), _row_spec(bs, D_MODEL), _row_spec(bs, 1)],
        out_shape=[jax.ShapeDtypeStruct((S, D_MODEL), F32), jax.ShapeDtypeStruct((S, D_MODEL), MXU_DTYPE),
                   jax.ShapeDtypeStruct((S, D_MODEL), F32), jax.ShapeDtypeStruct((S, 1), F32)],
        compiler_params=_cparams("parallel"),
    )(u, g, b)


def _ln_bwd(dy, xh, rs, g, *, bs, name):
    S = dy.shape[0]

    def body(dy_ref, xh_ref, rs_ref, g_ref, du_ref, dub_ref, dg_ref, db_ref):
        @pl.when(pl.program_id(0) == 0)
        def _():
            dg_ref[...] = jnp.zeros_like(dg_ref)
            db_ref[...] = jnp.zeros_like(db_ref)

        dy_, xh_ = dy_ref[...], xh_ref[...]
        dxh = dy_ * g_ref[...]
        du = rs_ref[...] * (dxh - jnp.mean(dxh, axis=-1, keepdims=True)
                            - xh_ * jnp.mean(dxh * xh_, axis=-1, keepdims=True))
        du_ref[...] = du
        dub_ref[...] = du.astype(dub_ref.dtype)
        dg_ref[...] += jnp.sum(dy_ * xh_, axis=0, keepdims=True)
        db_ref[...] += jnp.sum(dy_, axis=0, keepdims=True)

    return pl.pallas_call(
        body, name=name, grid=(S // bs,),
        in_specs=[_row_spec(bs, D_MODEL), _row_spec(bs, D_MODEL), _row_spec(bs, 1), _vec_spec(D_MODEL)],
        out_specs=[_row_spec(bs, D_MODEL), _row_spec(bs, D_MODEL), _vec_spec(D_MODEL), _vec_spec(D_MODEL)],
        out_shape=[jax.ShapeDtypeStruct((S, D_MODEL), F32), jax.ShapeDtypeStruct((S, D_MODEL), MXU_DTYPE),
                   jax.ShapeDtypeStruct((1, D_MODEL), F32), jax.ShapeDtypeStruct((1, D_MODEL), F32)],
        compiler_params=_cparams("arbitrary"),
    )(dy, xh, rs, g)


def _swiglu_fwd(gu, *, bs, name):
    S = gu.shape[0]

    def body(gu_ref, a_ref):
        gt = gu_ref[:, :D_FF]
        a_ref[...] = (gt / (1.0 + jnp.exp(-gt)) * gu_ref[:, D_FF:]).astype(a_ref.dtype)

    return pl.pallas_call(
        body, name=name, grid=(S // bs,),
        in_specs=[_row_spec(bs, 2 * D_FF)],
        out_specs=_row_spec(bs, D_FF), out_shape=jax.ShapeDtypeStruct((S, D_FF), MXU_DTYPE),
        compiler_params=_cparams("parallel"),
    )(gu)


def _swiglu_bwd(da, gu, *, bs, name):
    S = gu.shape[0]

    def body(da_ref, gu_ref, dgu_ref):
        gt, da_ = gu_ref[:, :D_FF], da_ref[...]
        sg = 1.0 / (1.0 + jnp.exp(-gt))
        silu = gt * sg
        dgu_ref[:, :D_FF] = (da_ * gu_ref[:, D_FF:] * (sg + silu * (1.0 - sg))).astype(dgu_ref.dtype)
        dgu_ref[:, D_FF:] = (da_ * silu).astype(dgu_ref.dtype)

    return pl.pallas_call(
        body, name=name, grid=(S // bs,),
        in_specs=[_row_spec(bs, D_FF), _row_spec(bs, 2 * D_FF)],
        out_specs=_row_spec(bs, 2 * D_FF), out_shape=jax.ShapeDtypeStruct((S, 2 * D_FF), MXU_DTYPE),
        compiler_params=_cparams("parallel"),
    )(da, gu)


def _loss_head(y, target, *, bs, name):
    S = y.shape[0]

    def body(y_ref, t_ref, dy_ref, loss_ref):
        @pl.when(pl.program_id(0) == 0)
        def _():
            loss_ref[...] = jnp.zeros_like(loss_ref)

        e = y_ref[...] - t_ref[...]
        dy_ref[...] = e * (1.0 / D_MODEL)
        per_tok = jnp.mean(e * e, axis=-1, keepdims=True)
        loss_ref[...] += 0.5 * jnp.sum(per_tok, axis=0, keepdims=True)

    return pl.pallas_call(
        body, name=name, grid=(S // bs,),
        in_specs=[_row_spec(bs, D_MODEL), _row_spec(bs, D_MODEL)],
        out_specs=[_row_spec(bs, D_MODEL), pl.BlockSpec((1, 1), lambda i: (0, 0))],
        out_shape=[jax.ShapeDtypeStruct((S, D_MODEL), F32), jax.ShapeDtypeStruct((1, 1), F32)],
        compiler_params=_cparams("arbitrary"),
    )(y, target)


def _blk(n, target):
    if n <= target:
        return n
    best = None
    for b in range(128, target + 1, 128):
        if n % b == 0:
            best = b
    assert best is not None, n
    return best


def _rope_tables(S):
    pos = jnp.arange(S, dtype=F32)
    inv = ROPE_THETA ** (-jnp.arange(0, MLA_ROPE, 2, dtype=F32) / MLA_ROPE)
    ang = pos[:, None] * inv[None, :]
    cos, sin = jnp.cos(ang), jnp.sin(ang)
    one, zero, pad = jnp.ones((S, HEAD_DIM), F32), jnp.zeros((S, HEAD_DIM), F32), jnp.zeros((S, MLA_PAD - MLA_QK), F32)
    return jnp.concatenate([one, cos, cos, pad], axis=1), jnp.concatenate([zero, sin, sin, pad], axis=1)


def _prep_weights(w_in, w_uq, w_ukv, w_o, w_gate, w_up, w_down):
    z = lambda n: jnp.zeros((D_MODEL, n), w_in.dtype)
    win_a = jnp.concatenate([w_in[:, 0:768], w_in[:, 1188:2468]], axis=1)
    win_l = jnp.concatenate([w_in[:, 772:1156], z(64), w_in[:, 1156:1188], z(32), w_in[:, 768:772], z(124)], axis=1)
    kv = w_ukv.reshape(MLA_KV_RANK, 4, 2 * HEAD_DIM)
    return dict(
        win_a=win_a, win_l=win_l, win_p=jnp.concatenate([win_a, win_l], axis=1),
        wuq=jnp.pad(w_uq.reshape(MLA_Q_RANK, 4, MLA_QK), ((0, 0), (0, 0), (0, MLA_PAD - MLA_QK))).reshape(MLA_Q_RANK, 512),
        wuk=jnp.pad(kv[:, :, :HEAD_DIM], ((0, 0), (0, 0), (0, HEAD_DIM))).reshape(MLA_KV_RANK, 512),
        wuv=kv[:, :, HEAD_DIM:].reshape(MLA_KV_RANK, 256),
        w_o=w_o, wgu=jnp.concatenate([w_gate, w_up], axis=1), w_down=w_down)


def _unprep_grads(dwin_p, dwuq, dwuk, dwuv, dwo, dwgu, dwd):
    dw_in = jnp.concatenate([dwin_p[:, 0:768], dwin_p[:, 2560:2564], dwin_p[:, 2048:2432], dwin_p[:, 2496:2528],
                             dwin_p[:, 768:2048]], axis=1)
    dw_uq = dwuq.reshape(MLA_Q_RANK, 4, MLA_PAD)[:, :, :MLA_QK].reshape(MLA_Q_RANK, 4 * MLA_QK)
    dw_ukv = jnp.concatenate([dwuk.reshape(MLA_KV_RANK, 4, MLA_PAD)[:, :, :HEAD_DIM],
                              dwuv.reshape(MLA_KV_RANK, 4, HEAD_DIM)], axis=2).reshape(MLA_KV_RANK, 512)
    return dict(w_in=dw_in, mla_w_uq=dw_uq, mla_w_ukv=dw_ukv, w_o=dwo, w_gate=dwgu[:, :D_FF], w_up=dwgu[:, D_FF:],
                w_down=dwd)


def _layer_fwd(l, x, xb, W, P, tabs, blk):
    S = x.shape[0]
    nb = S // blk
    n = lambda s: f"l{l}_{s}"
    bs = min(512, S)
    h_att = _mm(xb, W["win_a"], name=n("in_att"), out_dtype=MXU_DTYPE, bm=1024, bn=1024, bk=1024, colscale=Q_COLSCALE)
    lat = _mm(xb, W["win_l"], name=n("in_lat"), bm=2048, bn=LAT_W, bk=1024)
    fg = lat[:, 512:516].T.reshape(4, S // 128, 128)
    cpos, cneg = _fox_gate_fwd(fg, P["fox_b_f"], name=n("fox_gate"))
    one3 = jnp.ones((S, 4, 3), MXU_DTYPE)
    zpad = jnp.zeros((S, 4, MLA_PAD - HEAD_DIM - 6), MXU_DTYPE)
    per_tok = lambda parts: parts.reshape(4, 3, S).transpose(2, 0, 1)
    q_f = jnp.concatenate([h_att[:, COL_FQ:COL_FQ + 256].reshape(S, 4, HEAD_DIM), per_tok(cpos), one3, zpad],
                          axis=2).reshape(S, 4 * MLA_PAD)
    k_f = jnp.concatenate([h_att[:, COL_FK:COL_FK + 256].reshape(S, 4, HEAD_DIM), one3, per_tok(cneg), zpad],
                          axis=2).reshape(S, 4 * MLA_PAD)
    v_f = h_att[:, COL_FV:COL_FV + 256]
    oT_a, lse_a = _smax_fwd_t(q_f.T, k_f, _kv_blocks_t(v_f, blk), dk=MLA_PAD, blk=blk, name=n("fox_fwd"))
    q_m, k_m, v_m = _mla_prep_fwd(lat, P["mla_g_q"], P["mla_g_kv"], W["wuq"], W["wuk"], W["wuv"], *tabs,
                                  bs=bs, name=n("mla_prep"))
    oT_b, lse_b = _smax_fwd_t(q_m.T, k_m, _kv_blocks_t(v_m, blk), dk=MLA_PAD, blk=blk, name=n("mla_fwd"))
    qT_c = h_att[:, COL_SQ:COL_SQ + 256].T
    oT_c, lt_c = _sb_fwd_t(qT_c, h_att, _kv_blocks_t(h_att[:, COL_SV:COL_SV + 256], blk), blk=blk, name=n("sb_fwd"))
    out_d, lse_d = _swa_fwd(h_att, P["swa_sinks"], name=n("swa_fwd"))
    outs = (oT_a.T, oT_b.T, oT_c.T, out_d)
    gn = _gnorm_fwd(outs, P["mix_g"], bs=bs, name=n("gnorm"))
    u1 = _mm(gn, W["w_o"], name=n("out_proj"), bm=1024, bn=1024, bk=1024, resid=x, alpha=ALPHA)
    x1, x1b, xh1, rs1 = _ln_fwd(u1, P["ln1_g"], P["ln1_b"], bs=bs, name=n("ln1"))
    gu = _mm(x1b, W["wgu"], name=n("gate_up"), bm=2048, bn=512, bk=1024)
    a = _swiglu_fwd(gu, bs=min(256, S), name=n("swiglu"))
    u2 = _mm(a, W["w_down"], name=n("down"), bm=1024, bn=1024, bk=_blk(D_FF, 1408), resid=x1, alpha=ALPHA)
    x2, x2b, xh2, rs2 = _ln_fwd(u2, P["ln2_g"], P["ln2_b"], bs=bs, name=n("ln2"))
    saved = dict(xb=xb, h_att=h_att, lat=lat, fg=fg, outs=outs, oT_a=oT_a, oT_b=oT_b, q_f=q_f, k_f=k_f, v_f=v_f,
                 qT_c=qT_c, lse_a=lse_a, lse_b=lse_b, lse_d=lse_d, lt_c=lt_c, q_m=q_m, k_m=k_m, v_m=v_m, gn=gn,
                 xh1=xh1, rs1=rs1, x1b=x1b, gu=gu, a=a, xh2=xh2, rs2=rs2)
    return x2, x2b, saved


def _layer_bwd(l, dx2, sv, W, P, tabs, blk):
    S = dx2.shape[0]
    n = lambda s: f"l{l}_{s}"
    bs = min(512, S)
    h_att = sv["h_att"]
    du2, du2b, dg2, db2 = _ln_bwd(dx2, sv["xh2"], sv["rs2"], P["ln2_g"], bs=bs, name=n("ln2_bwd"))
    da = _mm(du2b, W["w_down"], name=n("down_dx"), tb=True, bm=1024, bn=_blk(D_FF, 1408), bk=1024)
    dwd = _mm(sv["a"].T, du2b, name=n("down_dw"), bm=_blk(D_FF, 1408), bn=1024, bk=1024)
    dgu = _swiglu_bwd(da, sv["gu"], bs=min(256, S), name=n("swiglu_bwd"))
    dx1 = _mm(dgu, W["wgu"], name=n("gate_up_dx"), tb=True, bm=1024, bn=1024, bk=_blk(2 * D_FF, 1408), resid=du2,
              alpha=ALPHA)
    dwgu = _mm(sv["x1b"].T, dgu, name=n("gate_up_dw"), bm=1024, bn=_blk(2 * D_FF, 1408), bk=1024)
    du1, du1b, dg1, db1 = _ln_bwd(dx1, sv["xh1"], sv["rs1"], P["ln1_g"], bs=bs, name=n("ln1_bwd"))
    dgn = _mm(du1b, W["w_o"], name=n("out_proj_dx"), tb=True, bm=1024, bn=1024, bk=1024)
    dwo = _mm(sv["gn"].T, du1b, name=n("out_proj_dw"), bm=1024, bn=1024, bk=1024)
    dmix, dmixg = _gnorm_bwd(dgn, sv["outs"], P["mix_g"], bs=bs, name=n("gnorm_bwd"))
    dmixT = dmix.T
    q_f, k_f = sv["q_f"], sv["k_f"]
    dqT_a, dk_a, dva = _smax_bwd_t(q_f.T, q_f, k_f, _kv_blocks_t(k_f, blk), sv["v_f"], dmix, dmixT, sv["oT_a"],
                                   sv["lse_a"], dk=MLA_PAD, dcb=0, qscale=HEAD_DIM ** -0.5, blk=blk, name=n("fox_bwd"))
    dq_a, dk_a = dqT_a.T.reshape(S, 4, MLA_PAD), dk_a.reshape(S, 4, MLA_PAD)
    dqa, dka = dq_a[:, :, :HEAD_DIM].reshape(S, 256), dk_a[:, :, :HEAD_DIM].reshape(S, 256)
    dcq = dq_a[:, :, HEAD_DIM].T.reshape(4, S // 128, 128)
    dck = dk_a[:, :, HEAD_DIM + 3].T.reshape(4, S // 128, 128)
    q_m, k_m = sv["q_m"], sv["k_m"]
    dqT_b, dkb, dvb = _smax_bwd_t(q_m.T, q_m, k_m, _kv_blocks_t(k_m, blk), sv["v_m"], dmix, dmixT, sv["oT_b"],
                                  sv["lse_b"], dk=MLA_PAD, dcb=2, qscale=MLA_QK ** -0.5, blk=blk, name=n("mla_bwd"))
    dqT_c, dkc, dvc = _sb_bwd_t(sv["qT_c"], h_att, _kv_blocks_t(h_att[:, COL_SK:COL_SK + 256], blk), dmix, dmixT,
                                sv["lt_c"], dcb=4, qscale=HEAD_DIM ** -0.5, blk=blk, name=n("sb_bwd"))
    dqc = dqT_c.T
    dqd, dkd, dvd, dsink = _swa_bwd(h_att, P["swa_sinks"], dmix, sv["outs"][3], sv["lse_d"], dcb=3, name=n("swa_bwd"))
    dlat, dwuq, dwuk, dwuv, dgq, dgkv = _mla_prep_bwd(
        sv["lat"], P["mla_g_q"], P["mla_g_kv"], W["wuq"], W["wuk"], W["wuv"], *tabs, dqT_b.T, dkb, dvb,
        bs=bs, name=n("mla_prep_bwd"))
    dfg, dbf = _fox_gate_bwd(sv["fg"], P["fox_b_f"], dck, dcq, q_unscale=HEAD_DIM ** 0.5, name=n("fox_gate_bwd"))
    dfg_blk = jnp.pad(dfg.reshape(4, S).T, ((0, 0), (0, 124)))
    dh = jnp.concatenate([t.astype(MXU_DTYPE) for t in (dqa, dka, dva, dqc, dkc, dvc, dqd, dkd, dvd, dlat, dfg_blk)], axis=1)
    dx = _mm(dh, W["win_p"], name=n("in_dx"), tb=True, bm=1024, bn=1024, bk=_blk(PERM_W, 1024), resid=du1, alpha=ALPHA)
    dwin_p = _mm(sv["xb"].T, dh, name=n("in_dw"), bm=1024, bn=_blk(PERM_W, 1024), bk=1024)
    grads = _unprep_grads(dwin_p, dwuq, dwuk, dwuv, dwo, dwgu, dwd)
    grads.update(fox_b_f=dbf[:, 0], mla_g_q=dgq[0], mla_g_kv=dgkv[0], swa_sinks=dsink[:, 0], mix_g=dmixg[0],
                 ln1_g=dg1[0], ln1_b=db1[0], ln2_g=dg2[0], ln2_b=db2[0])
    return dx, grads


BIG = ("w_in", "mla_w_uq", "mla_w_ukv", "w_o", "w_gate", "w_up", "w_down")
SMALL = ("fox_b_f", "mla_g_q", "mla_g_kv", "swa_sinks", "mix_g", "ln1_g", "ln1_b", "ln2_g", "ln2_b")
SHARD_AXIS = dict(w_in=2, mla_w_uq=2, mla_w_ukv=2, w_o=1, w_gate=2, w_up=2, w_down=1)
N_CHIPS = 4
ANY = pl.BlockSpec(memory_space=pl.ANY)


def _chip_exchange(tensors, *, scatter, name):
    nt = len(tensors)

    def body(*refs):
        ins, outs = refs[:nt], refs[nt:2 * nt]
        send_sems, recv_sems, local_sems = refs[2 * nt:]
        x, y, c = lax.axis_index("x"), lax.axis_index("y"), lax.axis_index("c")
        me = 2 * x + y
        peers = [(1 - x, y), (x, 1 - y), (1 - x, 1 - y)]
        local, sends, recvs = [], [], []
        for t in range(nt):
            local.append(pltpu.make_async_copy(ins[t].at[me] if scatter else ins[t], outs[t].at[me], local_sems.at[t]))
            for r, (px, py) in enumerate(peers):
                k = 3 * t + r
                theirs = 2 * px + py
                sends.append(pltpu.make_async_remote_copy(
                    src_ref=ins[t].at[theirs] if scatter else ins[t], dst_ref=outs[t].at[me],
                    send_sem=send_sems.at[k], recv_sem=recv_sems.at[k], device_id=(px, py, c), device_id_type=MESH))
                recvs.append(pltpu.make_async_remote_copy(
                    src_ref=ins[t].at[me] if scatter else ins[t], dst_ref=outs[t].at[theirs],
                    send_sem=send_sems.at[k], recv_sem=recv_sems.at[k], device_id=(px, py, c), device_id_type=MESH))
        for cp in local + sends:
            cp.start()
        for cp in recvs:
            cp.wait_recv()
        for cp in sends:
            cp.wait_send()
        for cp in local:
            cp.wait()

    out_shape = [jax.ShapeDtypeStruct(t.shape if scatter else (N_CHIPS,) + t.shape, t.dtype) for t in tensors]
    return pl.pallas_call(
        body, name=name, in_specs=[ANY] * nt, out_specs=[ANY] * nt, out_shape=out_shape,
        scratch_shapes=[pltpu.SemaphoreType.DMA((3 * nt,)), pltpu.SemaphoreType.DMA((3 * nt,)),
                        pltpu.SemaphoreType.DMA((nt,))],
        compiler_params=pltpu.CompilerParams(has_side_effects=True),
    )(*tensors)


HBM = pl.BlockSpec(memory_space=pltpu.HBM)
SEM = pl.BlockSpec(memory_space=pltpu.SEMAPHORE)
N_PEER_CHIPS = N_CHIPS - 1


def _peer_copies(src_ref, land_ref, sems, scatter):
    x, y, c = lax.axis_index("x"), lax.axis_index("y"), lax.axis_index("c")
    me = 2 * x + y
    out = []
    for r, (px, py) in enumerate([(1 - x, y), (x, 1 - y), (1 - x, 1 - y)]):
        theirs = 2 * px + py
        send = pltpu.make_async_remote_copy(
            src_ref=src_ref.at[theirs] if scatter else src_ref, dst_ref=land_ref.at[me],
            send_sem=sems[2 * r], recv_sem=sems[2 * r + 1], device_id=(px, py, c), device_id_type=MESH)
        arrive = pltpu.make_async_remote_copy(
            src_ref=src_ref.at[me] if scatter else src_ref, dst_ref=land_ref.at[theirs],
            send_sem=sems[2 * r], recv_sem=sems[2 * r + 1], device_id=(px, py, c), device_id_type=MESH)
        out.append((send, arrive))
    return out


def _exchange_start(src, *, scatter, name):
    land_shape = src.shape if scatter else (N_CHIPS,) + src.shape

    def body(src_ref, land_ref, *outs):
        sems, token = outs[:2 * N_PEER_CHIPS], outs[-1]
        for send, _ in _peer_copies(src_ref, land_ref, sems, scatter):
            send.start()
        token[...] = jnp.zeros_like(token)

    res = pl.pallas_call(
        body, name=name,
        out_shape=(*[pltpu.SemaphoreType.DMA(())] * (2 * N_PEER_CHIPS), pltpu.HBM(src.shape, src.dtype),
                   pltpu.HBM(land_shape, src.dtype), jax.ShapeDtypeStruct((8, 128), F32)),
        in_specs=(HBM, HBM), out_specs=(*[SEM] * (2 * N_PEER_CHIPS), HBM, HBM, pl.BlockSpec(memory_space=pltpu.VMEM)),
        input_output_aliases={0: 2 * N_PEER_CHIPS, 1: 2 * N_PEER_CHIPS + 1},
        compiler_params=pltpu.CompilerParams(has_side_effects=pltpu.SideEffectType.DATAFLOW_SIDE_EFFECTING),
    )(pltpu.with_memory_space_constraint(src, pltpu.HBM),
      pltpu.with_memory_space_constraint(lax.empty(land_shape, src.dtype), pltpu.HBM))
    return dict(sems=res[:2 * N_PEER_CHIPS], src=res[-3], land=res[-2], token=res[-1])


def _exchange_wait(started, after, *, scatter, name):
    def body(src_ref, land_ref, *rest):
        sems = rest[:2 * N_PEER_CHIPS]
        for send, arrive in _peer_copies(src_ref, land_ref, sems, scatter):
            send.wait_send()
            arrive.wait_recv()

    src, land = started["src"], started["land"]
    return pl.pallas_call(
        body, name=name, out_shape=(pltpu.HBM(src.shape, src.dtype), pltpu.HBM(land.shape, land.dtype)),
        in_specs=(HBM, HBM, *[SEM] * (2 * N_PEER_CHIPS), ANY), out_specs=(HBM, HBM), input_output_aliases={0: 0, 1: 1},
        compiler_params=pltpu.CompilerParams(has_side_effects=pltpu.SideEffectType.DATAFLOW_SIDE_EFFECTING),
    )(src, land, *started["sems"], after)


def _core_exchange(tensors, *, name):
    nt = len(tensors)

    def body(*refs):
        ins, outs = refs[:nt], refs[nt:2 * nt]
        send_sems, recv_sems = refs[2 * nt:]
        sibling = (lax.axis_index("x"), lax.axis_index("y"), 1 - lax.axis_index("c"))
        copies = [pltpu.make_async_remote_copy(src_ref=ins[t], dst_ref=outs[t], send_sem=send_sems.at[t],
                                               recv_sem=recv_sems.at[t], device_id=sibling, device_id_type=MESH)
                  for t in range(nt)]
        for cp in copies:
            cp.start()
        for cp in copies:
            cp.wait_recv()
        for cp in copies:
            cp.wait_send()

    return pl.pallas_call(
        body, name=name, in_specs=[ANY] * nt, out_specs=[ANY] * nt,
        out_shape=[jax.ShapeDtypeStruct(t.shape, t.dtype) for t in tensors],
        scratch_shapes=[pltpu.SemaphoreType.DMA((nt,)), pltpu.SemaphoreType.DMA((nt,))],
        compiler_params=pltpu.CompilerParams(has_side_effects=True),
    )(*tensors)


def _all_sum_small(block, *, name):
    R = block.shape[0]
    n_dev = 8

    def body(x_ref, o_ref, slots, send_sems, recv_sems):
        x, y, c = lax.axis_index("x"), lax.axis_index("y"), lax.axis_index("c")
        me = 4 * x + 2 * y + c
        slots[me] = x_ref[...]
        sends, recvs = [], []
        for d in range(1, n_dev):
            px, py, pc = x ^ (d >> 2), y ^ ((d >> 1) & 1), c ^ (d & 1)
            theirs = 4 * px + 2 * py + pc
            sends.append(pltpu.make_async_remote_copy(
                src_ref=x_ref, dst_ref=slots.at[me], send_sem=send_sems.at[d - 1], recv_sem=recv_sems.at[d - 1],
                device_id=(px, py, pc), device_id_type=MESH))
            recvs.append(pltpu.make_async_remote_copy(
                src_ref=x_ref, dst_ref=slots.at[theirs], send_sem=send_sems.at[d - 1], recv_sem=recv_sems.at[d - 1],
                device_id=(px, py, pc), device_id_type=MESH))
        for cp in sends:
            cp.start()
        for cp in recvs:
            cp.wait_recv()
        for cp in sends:
            cp.wait_send()
        total = slots[0]
        for k in range(1, n_dev):
            total = total + slots[k]
        o_ref[...] = total

    return pl.pallas_call(
        body, name=name, in_specs=[pl.BlockSpec(memory_space=pltpu.VMEM)],
        out_specs=pl.BlockSpec(memory_space=pltpu.VMEM), out_shape=jax.ShapeDtypeStruct((R, 128), F32),
        scratch_shapes=[pltpu.VMEM((n_dev, R, 128), F32), pltpu.SemaphoreType.DMA((n_dev - 1,)),
                        pltpu.SemaphoreType.DMA((n_dev - 1,))],
        compiler_params=pltpu.CompilerParams(has_side_effects=True),
    )(block)


def _sum_chips(recv, *, br, name):
    _, R, C = recv.shape

    def body(r_ref, o_ref):
        total = r_ref[0].astype(F32)
        for k in range(1, N_CHIPS):
            total = total + r_ref[k].astype(F32)
        o_ref[...] = total

    return pl.pallas_call(
        body, name=name, grid=(R // br,), in_specs=[pl.BlockSpec((N_CHIPS, br, C), lambda i: (0, i, 0))],
        out_specs=pl.BlockSpec((br, C), lambda i: (i, 0)), out_shape=jax.ShapeDtypeStruct((R, C), F32),
        compiler_params=_cparams("parallel"),
    )(recv)


def _adamw_math(w, g, m, v):
    m = ADAM_B1 * m + (1.0 - ADAM_B1) * g
    v = ADAM_B2 * v + (1.0 - ADAM_B2) * (g * g)
    m_hat = m / (1.0 - ADAM_B1 ** ADAM_STEP)
    v_hat = v / (1.0 - ADAM_B2 ** ADAM_STEP)
    return -ADAM_LR * (m_hat / (jnp.sqrt(v_hat) + ADAM_EPS) + ADAM_WD * w), m, v


def _adamw(w, m, v, g_a, g_b, *, br, name):
    R, C = w.shape
    two = g_b is not None

    def body(*refs):
        if two:
            w_ref, m_ref, v_ref, ga_ref, gb_ref, g_ref, d_ref, nm_ref, nv_ref = refs
            g = ga_ref[...] + gb_ref[...]
        else:
            w_ref, m_ref, v_ref, ga_ref, g_ref, d_ref, nm_ref, nv_ref = refs
            g = ga_ref[...]
        g_ref[...] = g
        d_ref[...], nm_ref[...], nv_ref[...] = _adamw_math(w_ref[...], g, m_ref[...], v_ref[...])

    spec = pl.BlockSpec((br, C), lambda i: (i, 0))
    args = [w, m, v, g_a] + ([g_b] if two else [])
    return pl.pallas_call(
        body, name=name, grid=(R // br,), in_specs=[spec] * len(args), out_specs=[spec] * 4,
        out_shape=[jax.ShapeDtypeStruct((R, C), F32)] * 4,
        compiler_params=_cparams("parallel"),
    )(*args)


SMALL_ROWS = dict(fox_b_f=1, mla_g_q=2, mla_g_kv=1, swa_sinks=1, mix_g=8, ln1_g=8, ln1_b=8, ln2_g=8, ln2_b=8)
SMALL_ROWS_PER_LAYER = sum(SMALL_ROWS.values())


def _pack_small(vals, extra_rows):
    L = vals[SMALL[0]].shape[0]
    per_layer = []
    for name in SMALL:
        a = vals[name].astype(F32)
        a = jnp.pad(a, ((0, 0), (0, SMALL_ROWS[name] * 128 - a.shape[1])))
        per_layer.append(a.reshape(L, SMALL_ROWS[name], 128))
    out = jnp.concatenate(per_layer, axis=1).reshape(L * SMALL_ROWS_PER_LAYER, 128)
    return jnp.pad(out, ((0, extra_rows), (0, 0)))


def _unpack_small(block, shapes):
    L = shapes[SMALL[0]][0]
    body = block[:L * SMALL_ROWS_PER_LAYER].reshape(L, SMALL_ROWS_PER_LAYER, 128)
    out, r = {}, 0
    for name in SMALL:
        n = shapes[name][1]
        out[name] = body[:, r:r + SMALL_ROWS[name]].reshape(L, SMALL_ROWS[name] * 128)[:, :n]
        r += SMALL_ROWS[name]
    return out


PACK_ROW_MULTIPLE = 256


def _pack(parts):
    flat = [p.reshape(-1, 128) for p in parts]
    pad = (-sum(f.shape[0] for f in flat)) % PACK_ROW_MULTIPLE
    if pad:
        flat.append(jnp.zeros((pad, 128), flat[0].dtype))
    return jnp.concatenate(flat, axis=0)


def _unpack(block, shapes):
    out, r = [], 0
    for shp in shapes:
        n = int(np.prod(shp)) // 128
        out.append(block[r:r + n].reshape(shp))
        r += n
    return out


def _shard(g, k, axis):
    n = g.shape[axis] // N_CHIPS
    return lax.slice_in_dim(g, k * n, (k + 1) * n, axis=axis)


def _to_chips(g, axis):
    L, a, b = g.shape
    if axis == 2:
        return g.reshape(L, a, N_CHIPS, b // N_CHIPS).transpose(2, 0, 1, 3)
    return g.reshape(L, N_CHIPS, a // N_CHIPS, b).transpose(1, 0, 2, 3)


def _from_chips(g, axis):
    _, L, a, b = g.shape
    if axis == 2:
        return g.transpose(1, 2, 0, 3).reshape(L, a, N_CHIPS * b)
    return g.transpose(1, 0, 2, 3).reshape(L, N_CHIPS * a, b)


def kernel(x, w_in, fox_b_f, mla_g_q, mla_g_kv, mla_w_uq, mla_w_ukv, swa_sinks, mix_g, w_o, ln1_g, ln1_b, w_gate, w_up, w_down, ln2_g, ln2_b, loss_target, m_w_in, m_fox_b_f, m_mla_g_q, m_mla_g_kv, m_mla_w_uq, m_mla_w_ukv, m_swa_sinks, m_mix_g, m_w_o, m_ln1_g, m_ln1_b, m_w_gate, m_w_up, m_w_down, m_ln2_g, m_ln2_b, v_w_in, v_fox_b_f, v_mla_g_q, v_mla_g_kv, v_mla_w_uq, v_mla_w_ukv, v_swa_sinks, v_mix_g, v_w_o, v_ln1_g, v_ln1_b, v_w_gate, v_w_up, v_w_down, v_ln2_g, v_ln2_b):
    w = dict(w_in=w_in, fox_b_f=fox_b_f, mla_g_q=mla_g_q, mla_g_kv=mla_g_kv, mla_w_uq=mla_w_uq, mla_w_ukv=mla_w_ukv,
             swa_sinks=swa_sinks, mix_g=mix_g, w_o=w_o, ln1_g=ln1_g, ln1_b=ln1_b, w_gate=w_gate, w_up=w_up,
             w_down=w_down, ln2_g=ln2_g, ln2_b=ln2_b)
    m = dict(w_in=m_w_in, fox_b_f=m_fox_b_f, mla_g_q=m_mla_g_q, mla_g_kv=m_mla_g_kv, mla_w_uq=m_mla_w_uq,
             mla_w_ukv=m_mla_w_ukv, swa_sinks=m_swa_sinks, mix_g=m_mix_g, w_o=m_w_o, ln1_g=m_ln1_g, ln1_b=m_ln1_b,
             w_gate=m_w_gate, w_up=m_w_up, w_down=m_w_down, ln2_g=m_ln2_g, ln2_b=m_ln2_b)
    v = dict(w_in=v_w_in, fox_b_f=v_fox_b_f, mla_g_q=v_mla_g_q, mla_g_kv=v_mla_g_kv, mla_w_uq=v_mla_w_uq,
             mla_w_ukv=v_mla_w_ukv, swa_sinks=v_swa_sinks, mix_g=v_mix_g, w_o=v_w_o, ln1_g=v_ln1_g, ln1_b=v_ln1_b,
             w_gate=v_w_gate, w_up=v_w_up, w_down=v_w_down, ln2_g=v_ln2_g, ln2_b=v_ln2_b)
    names = tuple(w)
    L = w_in.shape[0]
    S = x.shape[1]
    blk = min(256, S)
    bs = min(512, S)

    me = 2 * lax.axis_index("x") + lax.axis_index("y")
    shard_shapes = [w[k].shape[1:] for k in BIG]

    started = []
    for l in range(L):
        src = _pack([w[k][l].astype(MXU_DTYPE) for k in BIG])
        if started:
            src = src + started[-1]["token"][0, 0].astype(MXU_DTYPE)
        started.append(_exchange_start(src, scatter=False, name=f"gather_start{l}"))
    all_started = sum(st["token"] for st in started)
    tabs = _rope_tables(S)
    Ps = []
    for l in range(L):
        P = dict(fox_b_f=fox_b_f[l], swa_sinks=swa_sinks[l])
        for k in ("mla_g_q", "mla_g_kv", "mix_g", "ln1_g", "ln1_b", "ln2_g", "ln2_b"):
            P[k] = w[k][l][None, :]
        Ps.append(P)

    xa = x[0]
    xb = xa.astype(MXU_DTYPE)
    saved, Ws = [], []
    for l in range(L):
        mine, land = _exchange_wait(started[l], all_started if l == 0 else xa, scatter=False, name=f"gather_wait{l}")
        land = lax.dynamic_update_slice(land, mine[None], (me, 0, 0))
        shards = [_unpack(land[k], shard_shapes) for k in range(N_CHIPS)]
        full = [jnp.concatenate([shards[k][t] for k in range(N_CHIPS)], axis=SHARD_AXIS[name] - 1)
                for t, name in enumerate(BIG)]
        Ws.append(_prep_weights(*full))
        xa, xb, sv = _layer_fwd(l, xa, xb, Ws[l], Ps[l], tabs, blk)
        saved.append(sv)
    dx, loss_part = _loss_head(xa, loss_target[0], bs=bs, name="loss_head")

    layer_grads = [None] * L
    sent = [None] * L
    pin = None
    for l in reversed(range(L)):
        P = Ps[l] if pin is None else dict(Ps[l], ln2_g=Ps[l]["ln2_g"] + pin[0, 0])
        dx, layer_grads[l] = _layer_bwd(l, dx, saved[l], Ws[l], P, tabs, blk)
        to_owner = jnp.stack([_pack([_shard(layer_grads[l][name], k, SHARD_AXIS[name] - 1) for name in BIG])
                              for k in range(N_CHIPS)]).astype(MXU_DTYPE)
        sent[l] = _exchange_start(to_owner, scatter=True, name=f"scatter_start{l}")
        pin = sent[l]["token"]
    grad_x = dx[None]

    partial = []
    for l in range(L):
        mine, land = _exchange_wait(sent[l], dx, scatter=True, name=f"scatter_wait{l}")
        land = lax.dynamic_update_slice(land, lax.dynamic_slice_in_dim(mine, me, 1, axis=0), (me, 0, 0))
        partial.append(_sum_chips(land, br=PACK_ROW_MULTIPLE, name=f"sum_l{l}"))
    sibling = _core_exchange(partial, name="swap_partials")
    local = {k: jnp.stack([layer_grads[l][k] for l in range(L)]) for k in SMALL}
    out = {}
    for t, k in enumerate(BIG):
        shp = w[k].shape
        two_d = lambda a: a.reshape(shp[0] * shp[1], shp[2])
        mine = jnp.stack([_unpack(p, shard_shapes)[t] for p in partial])
        theirs = jnp.stack([_unpack(p, shard_shapes)[t] for p in sibling])
        res = _adamw(two_d(w[k]), two_d(m[k]), two_d(v[k]), two_d(mine), two_d(theirs), br=_rows(shp[0] * shp[1]),
                     name=f"adamw_{k}")
        out[k] = [a.reshape(shp) for a in res]

    shapes = {k: w[k].shape for k in SMALL}
    extra = 8 + (-L * SMALL_ROWS_PER_LAYER) % 8
    block = _pack_small({k: local[k] for k in SMALL}, extra)
    block = block.at[L * SMALL_ROWS_PER_LAYER, 0].set(loss_part[0, 0])
    total = _all_sum_small(block, name="sum_small")
    loss = total[L * SMALL_ROWS_PER_LAYER, 0]
    res = _adamw(_pack_small({k: w[k] for k in SMALL}, extra), _pack_small({k: m[k] for k in SMALL}, extra),
                 _pack_small({k: v[k] for k in SMALL}, extra), total, None, br=total.shape[0], name="adamw_small")
    res = [_unpack_small(t, shapes) for t in res]
    for k in SMALL:
        out[k] = [r[k] for r in res]

    return (loss, grad_x, *[out[k][0] for k in names], *[out[k][1] for k in names],
            *[out[k][2] for k in names], *[out[k][3] for k in names])


def _rows(n):
    for b in (256, 128, 64, 32, 16, 8):
        if n % b == 0:
            return b
    return n
```

```python
import functools

import numpy as np
import jax
import jax.numpy as jnp
from jax import lax
from jax.experimental import pallas as pl
from jax.experimental.pallas import tpu as pltpu

F32 = jnp.float32
MXU_DTYPE = jnp.bfloat16
NEG_INF = -1e30

D_MODEL = 1024
DEPTH = 4
HEAD_DIM = 64
GROUP_WIDTH = 256
N_GROUPS = 4
D_FF = 2816
MLA_Q_RANK = 256
MLA_KV_RANK = 128
MLA_ROPE = 32
MLA_QK = 96
MLA_PAD = 128
ROPE_THETA = 10000.0
WINDOW = 128
ALPHA = (2.0 * DEPTH) ** 0.25
SWA_SLOPES = tuple(float(2.0 ** (-8.0 * h / 4)) for h in range(1, 5))
IN_WIDTH = 2468
ATT_W = 2048
LAT_W = 640
PERM_W = ATT_W + LAT_W
COL_FQ, COL_FK, COL_FV = 0, 256, 512
COL_SQ, COL_SK, COL_SV = 768, 1024, 1280
COL_WQ, COL_WK, COL_WV = 1536, 1792, 1920
Q_COLSCALE = np.ones((1, ATT_W), np.float32)
Q_COLSCALE[:, COL_FQ:COL_FQ + 256] = HEAD_DIM ** -0.5
Q_COLSCALE[:, COL_SQ:COL_SQ + 256] = HEAD_DIM ** -0.5

ADAM_LR, ADAM_B1, ADAM_B2, ADAM_EPS, ADAM_WD, ADAM_STEP = 0.001, 0.9, 0.999, 1e-08, 0.01, 10

VMEM_LIMIT = 56 * 1024 * 1024
NT = (((1,), (1,)), ((), ()))
TN = (((0,), (0,)), ((), ()))
MESH = pl.DeviceIdType.MESH


def _cparams(*sem):
    return pltpu.CompilerParams(dimension_semantics=sem, vmem_limit_bytes=VMEM_LIMIT)


def _split2(x):
    hi = x.astype(MXU_DTYPE)
    lo = (x - hi.astype(F32)).astype(MXU_DTYPE)
    return hi, lo


def _dot01(x, m01, dn=None, parts=2):
    acc = None
    rem = x
    for _ in range(parts):
        part = rem.astype(MXU_DTYPE)
        rem = rem - part.astype(F32)
        if dn is None:
            t = jnp.dot(part, m01, preferred_element_type=F32)
        else:
            t = lax.dot_general(part, m01, dn, preferred_element_type=F32)
        acc = t if acc is None else acc + t
    return acc


def _mm(a, b, *, name, ta=False, tb=False, out_dtype=F32, bm=512, bn=512, bk=512, resid=None, alpha=1.0,
        colscale=None):
    M, K = (a.shape[1], a.shape[0]) if ta else a.shape
    N = b.shape[0] if tb else b.shape[1]
    assert (b.shape[1] if tb else b.shape[0]) == K
    assert resid is None or colscale is None
    bm, bn, bk = min(bm, M), min(bn, N), min(bk, K)
    assert M % bm == 0 and N % bn == 0 and K % bk == 0, (name, M, N, K, bm, bn, bk)
    nk = K // bk
    dn = (((0 if ta else 1,), (1 if tb else 0,)), ((), ()))

    extra = resid is not None or colscale is not None

    def body(*refs):
        a_ref, b_ref = refs[:2]
        r_ref = refs[2] if extra else None
        o_ref = refs[3] if extra else refs[2]
        acc_ref = refs[-1] if nk > 1 else None
        k = pl.program_id(2)
        part = lax.dot_general(a_ref[...].astype(MXU_DTYPE), b_ref[...].astype(MXU_DTYPE), dn,
                               preferred_element_type=F32)

        def finish(r):
            if resid is not None:
                r = r + alpha * r_ref[...]
            if colscale is not None:
                r = r * r_ref[...]
            o_ref[...] = r.astype(o_ref.dtype)

        if nk == 1:
            finish(part)
        else:
            @pl.when(k == 0)
            def _():
                acc_ref[...] = part

            @pl.when((k > 0) & (k < nk - 1))
            def _():
                acc_ref[...] += part

            @pl.when(k == nk - 1)
            def _():
                finish(acc_ref[...] + part)

    a_spec = pl.BlockSpec((bk, bm), lambda i, j, k: (k, i)) if ta else pl.BlockSpec((bm, bk), lambda i, j, k: (i, k))
    b_spec = pl.BlockSpec((bn, bk), lambda i, j, k: (j, k)) if tb else pl.BlockSpec((bk, bn), lambda i, j, k: (k, j))
    in_specs = [a_spec, b_spec]
    args = [a, b]
    if resid is not None:
        in_specs.append(pl.BlockSpec((bm, bn), lambda i, j, k: (i, j)))
        args.append(resid)
    if colscale is not None:
        in_specs.append(pl.BlockSpec((1, bn), lambda i, j, k: (0, j)))
        args.append(colscale)
    return pl.pallas_call(
        body, name=name, grid=(M // bm, N // bn, nk), in_specs=in_specs,
        out_specs=pl.BlockSpec((bm, bn), lambda i, j, k: (i, j)),
        out_shape=jax.ShapeDtypeStruct((M, N), out_dtype),
        scratch_shapes=[pltpu.VMEM((bm, bn), F32)] if nk > 1 else [],
        compiler_params=_cparams("parallel", "parallel", "arbitrary"),
    )(*args)


def _softmax_attn_fwd(q_arr, k_arr, v_arr, *, qcb, kcb, vcb, dk, scale, cum_col=None, cum_row=None, blk, name):
    S = q_arr.shape[0]
    nb = S // blk
    bias = cum_col is not None
    W = 2 * dk

    def body(*refs):
        if bias:
            q_ref, k_ref, v_ref, cc_ref, cr_ref, o_ref, lse_ref = refs
        else:
            q_ref, k_ref, v_ref, o_ref, lse_ref = refs
        p = pl.program_id(0)
        i = pl.program_id(1)
        row = lax.broadcasted_iota(jnp.int32, (blk, blk), 0)
        col = lax.broadcasted_iota(jnp.int32, (blk, blk), 1)
        for hh in range(2):
            q = q_ref[:, hh * dk:(hh + 1) * dk]

            def tile(j, carry, masked, hh=hh, q=q):
                m, l, acc = carry
                r0 = pl.multiple_of(j * blk, blk)
                ks = k_ref[pl.ds(r0, blk), hh * dk:(hh + 1) * dk]
                vs = v_ref[pl.ds(r0, blk), hh * HEAD_DIM:(hh + 1) * HEAD_DIM]
                s = lax.dot_general(q, ks, NT, preferred_element_type=F32) * scale
                if bias:
                    s = s + cc_ref[hh] - cr_ref[(2 * p + hh) * nb + j]
                if masked:
                    s = jnp.where(col <= row, s, NEG_INF)
                mn = jnp.maximum(m, jnp.max(s, axis=1, keepdims=True))
                a = jnp.exp(m - mn)
                pe = jnp.exp(s - mn)
                l = a * l + jnp.sum(pe, axis=1, keepdims=True)
                acc = a * acc + jnp.dot(pe.astype(MXU_DTYPE), vs, preferred_element_type=F32)
                return mn, l, acc

            init = (jnp.full((blk, 1), NEG_INF, F32), jnp.zeros((blk, 1), F32), jnp.zeros((blk, HEAD_DIM), F32))
            carry = lax.fori_loop(0, i, functools.partial(tile, masked=False), init)
            m, l, acc = tile(i, carry, True)
            o_ref[:, hh * HEAD_DIM:(hh + 1) * HEAD_DIM] = acc / l
            lse_ref[hh] = m + jnp.log(l)

    in_specs = [pl.BlockSpec((blk, W), lambda p, i: (i, qcb + p)),
                pl.BlockSpec((S, W), lambda p, i: (0, kcb + p)),
                pl.BlockSpec((S, 128), lambda p, i: (0, vcb + p))]
    args = [q_arr, k_arr, v_arr]
    if bias:
        in_specs += [pl.BlockSpec((2, blk, 1), lambda p, i: (p, i, 0)),
                     pl.BlockSpec((4 * nb, 1, blk), lambda p, i: (0, 0, 0))]
        args += [cum_col, cum_row]
    return pl.pallas_call(
        body, name=name, grid=(2, nb), in_specs=in_specs,
        out_specs=[pl.BlockSpec((blk, 128), lambda p, i: (i, p)), pl.BlockSpec((2, blk, 1), lambda p, i: (p, i, 0))],
        out_shape=[jax.ShapeDtypeStruct((S, GROUP_WIDTH), F32), jax.ShapeDtypeStruct((4, S, 1), F32)],
        compiler_params=_cparams("arbitrary", "arbitrary"),
    )(*args)


def _softmax_attn_bwd(q_arr, k_arr, v_arr, dmix, o_arr, lse, *, qcb, kcb, vcb, dcb, dk, scale,
                      cum_col=None, cum_row=None, blk, name):
    S = q_arr.shape[0]
    nb = S // blk
    bias = cum_col is not None
    W = 2 * dk

    def body(*refs):
        if bias:
            q_ref, k_ref, v_ref, do_ref, o_ref, lse_ref, cc_ref, cr_ref, dq_ref, dk_ref, dv_ref, dc_ref, dcq_ref = refs
        else:
            q_ref, k_ref, v_ref, do_ref, o_ref, lse_ref, dq_ref, dk_ref, dv_ref = refs
        p = pl.program_id(0)
        i = pl.program_id(1)

        @pl.when(i == 0)
        def _():
            dk_ref[...] = jnp.zeros_like(dk_ref)
            dv_ref[...] = jnp.zeros_like(dv_ref)
            if bias:
                dc_ref[...] = jnp.zeros_like(dc_ref)

        row = lax.broadcasted_iota(jnp.int32, (blk, blk), 0)
        col = lax.broadcasted_iota(jnp.int32, (blk, blk), 1)
        for hh in range(2):
            q = q_ref[:, hh * dk:(hh + 1) * dk]
            do = do_ref[:, hh * HEAD_DIM:(hh + 1) * HEAD_DIM]
            delta = jnp.sum(do * o_ref[:, hh * HEAD_DIM:(hh + 1) * HEAD_DIM], axis=1, keepdims=True)
            dob = do.astype(MXU_DTYPE)
            lse_h = lse_ref[hh]

            def tile(j, carry, masked, hh=hh, q=q, dob=dob, delta=delta, lse_h=lse_h):
                dq, dcq = carry
                r0 = pl.multiple_of(j * blk, blk)
                ks = k_ref[pl.ds(r0, blk), hh * dk:(hh + 1) * dk]
                vs = v_ref[pl.ds(r0, blk), hh * HEAD_DIM:(hh + 1) * HEAD_DIM]
                s = lax.dot_general(q, ks, NT, preferred_element_type=F32) * scale
                if bias:
                    s = s + cc_ref[hh] - cr_ref[(2 * p + hh) * nb + j]
                if masked:
                    s = jnp.where(col <= row, s, NEG_INF)
                pr = jnp.exp(s - lse_h)
                dp = lax.dot_general(dob, vs, NT, preferred_element_type=F32)
                ds = pr * (dp - delta)
                dsb = ds.astype(MXU_DTYPE)
                dv_ref[pl.ds(r0, blk), hh * HEAD_DIM:(hh + 1) * HEAD_DIM] += lax.dot_general(
                    pr.astype(MXU_DTYPE), dob, TN, preferred_element_type=F32)
                dk_ref[pl.ds(r0, blk), hh * dk:(hh + 1) * dk] += lax.dot_general(
                    dsb, q, TN, preferred_element_type=F32) * scale
                if bias:
                    dc_ref[hh * nb + j] -= jnp.sum(ds, axis=0, keepdims=True)
                    dcq = dcq + jnp.sum(ds, axis=1, keepdims=True)
                return dq + jnp.dot(dsb, ks, preferred_element_type=F32) * scale, dcq

            carry = lax.fori_loop(0, i, functools.partial(tile, masked=False),
                                  (jnp.zeros((blk, dk), F32), jnp.zeros((blk, 1), F32)))
            dq, dcq = tile(i, carry, True)
            dq_ref[:, hh * dk:(hh + 1) * dk] = dq
            if bias:
                dcq_ref[hh] = dcq

    in_specs = [pl.BlockSpec((blk, W), lambda p, i: (i, qcb + p)),
                pl.BlockSpec((S, W), lambda p, i: (0, kcb + p)),
                pl.BlockSpec((S, 128), lambda p, i: (0, vcb + p)),
                pl.BlockSpec((blk, 128), lambda p, i: (i, dcb + p)),
                pl.BlockSpec((blk, 128), lambda p, i: (i, p)),
                pl.BlockSpec((2, blk, 1), lambda p, i: (p, i, 0))]
    args = [q_arr, k_arr, v_arr, dmix, o_arr, lse]
    out_specs = [pl.BlockSpec((blk, W), lambda p, i: (i, p)),
                 pl.BlockSpec((S, W), lambda p, i: (0, p)),
                 pl.BlockSpec((S, 128), lambda p, i: (0, p))]
    out_shape = [jax.ShapeDtypeStruct((S, 4 * dk), F32), jax.ShapeDtypeStruct((S, 4 * dk), F32),
                 jax.ShapeDtypeStruct((S, GROUP_WIDTH), F32)]
    if bias:
        in_specs += [pl.BlockSpec((2, blk, 1), lambda p, i: (p, i, 0)),
                     pl.BlockSpec((4 * nb, 1, blk), lambda p, i: (0, 0, 0))]
        args += [cum_col, cum_row]
        out_specs += [pl.BlockSpec((2 * nb, 1, blk), lambda p, i: (p, 0, 0)), pl.BlockSpec((2, blk, 1), lambda p, i: (p, i, 0))]
        out_shape += [jax.ShapeDtypeStruct((4 * nb, 1, blk), F32), jax.ShapeDtypeStruct((4, S, 1), F32)]
    return pl.pallas_call(
        body, name=name, grid=(2, nb), in_specs=in_specs, out_specs=out_specs, out_shape=out_shape,
        compiler_params=_cparams("arbitrary", "arbitrary"),
    )(*args)


def _sb_tile(q, ks, scale, strict_mask, carry_l, tri_excl):
    z = lax.dot_general(q, ks, NT, preferred_element_type=F32) * scale
    lb = -(jnp.maximum(z, 0.0) + jnp.log(1.0 + jnp.exp(-jnp.abs(z))))
    if strict_mask is not None:
        lb = jnp.where(strict_mask, lb, 0.0)
    between = _dot01(lb, tri_excl) + carry_l
    a = jnp.exp(z + lb + between)
    if strict_mask is not None:
        a = jnp.where(strict_mask, a, 0.0)
    return z, lb, a


def _sb_attn_fwd(h_att, *, blk, name):
    S = h_att.shape[0]
    nb = S // blk
    scale = HEAD_DIM ** -0.5
    qcb, kcb, vcb = COL_SQ // 128, COL_SK // 128, COL_SV // 128

    def body(q_ref, k_ref, v_ref, o_ref, lt_ref):
        i = pl.program_id(1)
        row = lax.broadcasted_iota(jnp.int32, (blk, blk), 0)
        col = lax.broadcasted_iota(jnp.int32, (blk, blk), 1)
        strict = col < row
        tri_excl = (row > col).astype(MXU_DTYPE)
        for hh in range(2):
            sl = slice(hh * HEAD_DIM, (hh + 1) * HEAD_DIM)
            q = q_ref[:, sl]

            def tile(j, carry, mask, sl=sl, q=q):
                cl, acc = carry
                r0 = pl.multiple_of(j * blk, blk)
                _, lb, a = _sb_tile(q, k_ref[pl.ds(r0, blk), sl], scale, mask, cl, tri_excl)
                acc = acc + jnp.dot(a.astype(MXU_DTYPE), v_ref[pl.ds(r0, blk), sl], preferred_element_type=F32)
                return cl + jnp.sum(lb, axis=1, keepdims=True), acc

            carry = tile(i, (jnp.zeros((blk, 1), F32), jnp.zeros((blk, HEAD_DIM), F32)), strict)
            cl, acc = lax.fori_loop(0, i, lambda jj, c: tile(i - 1 - jj, c, None), carry)
            o_ref[:, sl] = acc
            lt_ref[hh] = cl

    return pl.pallas_call(
        body, name=name, grid=(2, nb),
        in_specs=[pl.BlockSpec((blk, 128), lambda p, i: (i, qcb + p)),
                  pl.BlockSpec((S, 128), lambda p, i: (0, kcb + p)),
                  pl.BlockSpec((S, 128), lambda p, i: (0, vcb + p))],
        out_specs=[pl.BlockSpec((blk, 128), lambda p, i: (i, p)), pl.BlockSpec((2, blk, 1), lambda p, i: (p, i, 0))],
        out_shape=[jax.ShapeDtypeStruct((S, GROUP_WIDTH), F32), jax.ShapeDtypeStruct((4, S, 1), F32)],
        compiler_params=_cparams("arbitrary", "arbitrary"),
    )(h_att, h_att, h_att)


def _sb_attn_bwd(h_att, dmix, ltot_arr, *, dcb, blk, name):
    S = h_att.shape[0]
    nb = S // blk
    scale = HEAD_DIM ** -0.5
    qcb, kcb, vcb = COL_SQ // 128, COL_SK // 128, COL_SV // 128

    def body(q_ref, k_ref, v_ref, do_ref, lt_ref, dq_ref, dk_ref, dv_ref):
        i = pl.program_id(1)

        @pl.when(i == 0)
        def _():
            dk_ref[...] = jnp.zeros_like(dk_ref)
            dv_ref[...] = jnp.zeros_like(dv_ref)

        row = lax.broadcasted_iota(jnp.int32, (blk, blk), 0)
        col = lax.broadcasted_iota(jnp.int32, (blk, blk), 1)
        strict = col < row
        up_incl = (row <= col).astype(MXU_DTYPE)
        up_excl = (row < col).astype(MXU_DTYPE)
        for hh in range(2):
            sl = slice(hh * HEAD_DIM, (hh + 1) * HEAD_DIM)
            q = q_ref[:, sl]
            dob = do_ref[:, sl].astype(MXU_DTYPE)
            ltot = lt_ref[hh]

            def tile(j, carry, mask, sl=sl, q=q, dob=dob, ltot=ltot):
                cl, cg, dq = carry
                r0 = pl.multiple_of(j * blk, blk)
                ks = k_ref[pl.ds(r0, blk), sl]
                vs = v_ref[pl.ds(r0, blk), sl]
                z = lax.dot_general(q, ks, NT, preferred_element_type=F32) * scale
                lb = -(jnp.maximum(z, 0.0) + jnp.log(1.0 + jnp.exp(-jnp.abs(z))))
                if mask is not None:
                    lb = jnp.where(mask, lb, 0.0)
                between = ltot - cl - _dot01(lb, up_incl)
                a = jnp.exp(z + lb + between)
                if mask is not None:
                    a = jnp.where(mask, a, 0.0)
                g = lax.dot_general(dob, vs, NT, preferred_element_type=F32) * a
                e = cg + _dot01(g, up_excl)
                dz = g * jnp.exp(lb) - e * jnp.exp(z + lb)
                if mask is not None:
                    dz = jnp.where(mask, dz, 0.0)
                dzb = dz.astype(MXU_DTYPE)
                dv_ref[pl.ds(r0, blk), sl] += lax.dot_general(a.astype(MXU_DTYPE), dob, TN, preferred_element_type=F32)
                dk_ref[pl.ds(r0, blk), sl] += lax.dot_general(dzb, q, TN, preferred_element_type=F32) * scale
                dq = dq + jnp.dot(dzb, ks, preferred_element_type=F32) * scale
                return cl + jnp.sum(lb, axis=1, keepdims=True), cg + jnp.sum(g, axis=1, keepdims=True), dq

            zc = jnp.zeros((blk, 1), F32)
            carry = lax.fori_loop(0, i, lambda j, c: tile(j, c, None), (zc, zc, jnp.zeros((blk, HEAD_DIM), F32)))
            _, _, dq = tile(i, carry, strict)
            dq_ref[:, sl] = dq

    return pl.pallas_call(
        body, name=name, grid=(2, nb),
        in_specs=[pl.BlockSpec((blk, 128), lambda p, i: (i, qcb + p)),
                  pl.BlockSpec((S, 128), lambda p, i: (0, kcb + p)),
                  pl.BlockSpec((S, 128), lambda p, i: (0, vcb + p)),
                  pl.BlockSpec((blk, 128), lambda p, i: (i, dcb + p)),
                  pl.BlockSpec((2, blk, 1), lambda p, i: (p, i, 0))],
        out_specs=[pl.BlockSpec((blk, 128), lambda p, i: (i, p)),
                   pl.BlockSpec((S, 128), lambda p, i: (0, p)),
                   pl.BlockSpec((S, 128), lambda p, i: (0, p))],
        out_shape=[jax.ShapeDtypeStruct((S, GROUP_WIDTH), F32)] * 3,
        compiler_params=_cparams("arbitrary", "arbitrary"),
    )(h_att, h_att, h_att, dmix, ltot_arr)


HP = 4


def _kv_blocks_t(a, blk):
    S, C = a.shape
    return a.reshape(S // blk, blk, C).transpose(0, 2, 1)


def _smax_fwd_t(qT, k, vT3, *, dk, blk, name):
    S = k.shape[0]
    nb = S // blk
    H = k.shape[1] // dk

    def body(qT_ref, k_ref, vT_ref, oT_ref, lse_ref):
        i = pl.program_id(1)
        key = lax.broadcasted_iota(jnp.int32, (blk, blk), 0)
        qry = lax.broadcasted_iota(jnp.int32, (blk, blk), 1)
        qs = [qT_ref[h * dk:(h + 1) * dk, :] for h in range(HP)]

        def tile(j, carry, masked):
            r0 = pl.multiple_of(j * blk, blk)
            ss = [jnp.dot(k_ref[pl.ds(r0, blk), h * dk:(h + 1) * dk], qs[h], preferred_element_type=F32)
                  for h in range(HP)]
            stats, pes = [], []
            for h in range(HP):
                m, l, _ = carry[h]
                s = jnp.where(key <= qry, ss[h], NEG_INF) if masked else ss[h]
                mn = jnp.maximum(m, jnp.max(s, axis=0, keepdims=True))
                a = jnp.exp(m - mn)
                pe = jnp.exp(s - mn)
                stats.append((mn, a * l + jnp.sum(pe, axis=0, keepdims=True), a))
                pes.append(pe.astype(MXU_DTYPE))
            pvs = [jnp.dot(vT_ref[j, h * HEAD_DIM:(h + 1) * HEAD_DIM, :], pes[h], preferred_element_type=F32)
                   for h in range(HP)]
            return tuple((stats[h][0], stats[h][1], stats[h][2] * carry[h][2] + pvs[h]) for h in range(HP))

        init = tuple((jnp.full((1, blk), NEG_INF, F32), jnp.zeros((1, blk), F32), jnp.zeros((HEAD_DIM, blk), F32))
                     for _ in range(HP))
        carry = lax.fori_loop(0, i, functools.partial(tile, masked=False), init)
        carry = tile(i, carry, True)
        for h in range(HP):
            m, l, acc = carry[h]
            oT_ref[h * HEAD_DIM:(h + 1) * HEAD_DIM, :] = acc / l
            lse_ref[h, 0] = m + jnp.log(l)

    return pl.pallas_call(
        body, name=name, grid=(H // HP, nb),
        in_specs=[pl.BlockSpec((HP * dk, blk), lambda p, i: (p, i)),
                  pl.BlockSpec((S, HP * dk), lambda p, i: (0, p)),
                  pl.BlockSpec((nb, HP * HEAD_DIM, blk), lambda p, i: (0, p, 0))],
        out_specs=[pl.BlockSpec((HP * HEAD_DIM, blk), lambda p, i: (p, i)),
                   pl.BlockSpec((HP, 1, 1, blk), lambda p, i: (p, i, 0, 0))],
        out_shape=[jax.ShapeDtypeStruct((H * HEAD_DIM, S), F32), jax.ShapeDtypeStruct((H, nb, 1, blk), F32)],
        compiler_params=_cparams("arbitrary", "arbitrary"),
    )(qT, k, vT3)


def _smax_bwd_t(qT, q, k, kT3, v, dmix, dmixT, oT, lse, *, dk, dcb, qscale, blk, name):
    S = k.shape[0]
    nb = S // blk
    H = k.shape[1] // dk
    hd = HP * HEAD_DIM
    dcr = dcb * 128 // hd

    def body(qT_ref, q_ref, k_ref, kT_ref, v_ref, do_ref, doT_ref, oT_ref, lse_ref, dqT_ref, dk_ref, dv_ref):
        i = pl.program_id(1)

        @pl.when(i == 0)
        def _():
            dk_ref[...] = jnp.zeros_like(dk_ref)
            dv_ref[...] = jnp.zeros_like(dv_ref)

        key = lax.broadcasted_iota(jnp.int32, (blk, blk), 0)
        qry = lax.broadcasted_iota(jnp.int32, (blk, blk), 1)
        per_head = []
        for h in range(HP):
            hs = slice(h * HEAD_DIM, (h + 1) * HEAD_DIM)
            doT = doT_ref[hs, :]
            per_head.append(dict(
                qT=qT_ref[h * dk:(h + 1) * dk, :], q=q_ref[:, h * dk:(h + 1) * dk],
                doT=doT.astype(MXU_DTYPE), do=do_ref[:, hs].astype(MXU_DTYPE),
                delta=jnp.sum(doT * oT_ref[hs, :], axis=0, keepdims=True), lse=lse_ref[h, 0]))

        def tile(j, dqs, masked):
            r0 = pl.multiple_of(j * blk, blk)
            rows = pl.ds(r0, blk)
            ksl = [slice(h * dk, (h + 1) * dk) for h in range(HP)]
            hsl = [slice(h * HEAD_DIM, (h + 1) * HEAD_DIM) for h in range(HP)]
            ss = [jnp.dot(k_ref[rows, ksl[h]], per_head[h]["qT"], preferred_element_type=F32) for h in range(HP)]
            dps = [jnp.dot(v_ref[rows, hsl[h]], per_head[h]["doT"], preferred_element_type=F32) for h in range(HP)]
            prs, dss = [], []
            for h in range(HP):
                c = per_head[h]
                s = jnp.where(key <= qry, ss[h], NEG_INF) if masked else ss[h]
                pr = jnp.exp(s - c["lse"])
                dss.append((pr * (dps[h] - c["delta"])).astype(MXU_DTYPE))
                prs.append(pr.astype(MXU_DTYPE))
            for h in range(HP):
                dv_ref[rows, hsl[h]] += jnp.dot(prs[h], per_head[h]["do"], preferred_element_type=F32)
            for h in range(HP):
                dk_ref[rows, ksl[h]] += jnp.dot(dss[h], per_head[h]["q"], preferred_element_type=F32)
            return tuple(dqs[h] + jnp.dot(kT_ref[j, ksl[h], :], dss[h], preferred_element_type=F32) for h in range(HP))

        dqs = lax.fori_loop(0, i, functools.partial(tile, masked=False),
                            tuple(jnp.zeros((dk, blk), F32) for _ in range(HP)))
        dqs = tile(i, dqs, True)
        for h in range(HP):
            dqT_ref[h * dk:(h + 1) * dk, :] = dqs[h] * qscale

    return pl.pallas_call(
        body, name=name, grid=(H // HP, nb),
        in_specs=[pl.BlockSpec((HP * dk, blk), lambda p, i: (p, i)),
                  pl.BlockSpec((blk, HP * dk), lambda p, i: (i, p)),
                  pl.BlockSpec((S, HP * dk), lambda p, i: (0, p)),
                  pl.BlockSpec((nb, HP * dk, blk), lambda p, i: (0, p, 0)),
                  pl.BlockSpec((S, hd), lambda p, i: (0, p)),
                  pl.BlockSpec((blk, hd), lambda p, i: (i, dcr + p)),
                  pl.BlockSpec((hd, blk), lambda p, i: (dcr + p, i)),
                  pl.BlockSpec((hd, blk), lambda p, i: (p, i)),
                  pl.BlockSpec((HP, 1, 1, blk), lambda p, i: (p, i, 0, 0))],
        out_specs=[pl.BlockSpec((HP * dk, blk), lambda p, i: (p, i)),
                   pl.BlockSpec((S, HP * dk), lambda p, i: (0, p)),
                   pl.BlockSpec((S, hd), lambda p, i: (0, p))],
        out_shape=[jax.ShapeDtypeStruct((H * dk, S), F32), jax.ShapeDtypeStruct((S, H * dk), F32),
                   jax.ShapeDtypeStruct((S, H * HEAD_DIM), F32)],
        compiler_params=_cparams("arbitrary", "arbitrary"),
    )(qT, q, k, kT3, v, dmix, dmixT, oT, lse)


def _log1m_beta(z):
    return -(jnp.maximum(z, 0.0) + jnp.log(1.0 + jnp.exp(-jnp.abs(z))))


def _dot01_left(m01, x, parts=2):
    acc = None
    rem = x
    for _ in range(parts):
        part = rem.astype(MXU_DTYPE)
        rem = rem - part.astype(F32)
        t = jnp.dot(m01, part, preferred_element_type=F32)
        acc = t if acc is None else acc + t
    return acc


def _sb_fwd_t(qT, h_att, vT3, *, blk, name):
    S = h_att.shape[0]
    nb = S // blk
    kcb = COL_SK // (HP * HEAD_DIM)

    def body(qT_ref, k_ref, vT_ref, oT_ref, lt_ref):
        i = pl.program_id(1)
        key = lax.broadcasted_iota(jnp.int32, (blk, blk), 0)
        qry = lax.broadcasted_iota(jnp.int32, (blk, blk), 1)
        strict = key < qry
        later = (qry > key).astype(MXU_DTYPE)
        qs = [qT_ref[h * HEAD_DIM:(h + 1) * HEAD_DIM, :] for h in range(HP)]

        def tile(j, carry, mask):
            r0 = pl.multiple_of(j * blk, blk)
            hsl = [slice(h * HEAD_DIM, (h + 1) * HEAD_DIM) for h in range(HP)]
            zs = [jnp.dot(k_ref[pl.ds(r0, blk), hsl[h]], qs[h], preferred_element_type=F32) for h in range(HP)]
            lbs = []
            for h in range(HP):
                lb = _log1m_beta(zs[h])
                lbs.append(lb if mask is None else jnp.where(mask, lb, 0.0))
            sums = [_dot01_left(later, lbs[h]) for h in range(HP)]
            probs = []
            for h in range(HP):
                a = jnp.exp(zs[h] + lbs[h] + sums[h] + carry[h][0])
                probs.append((a if mask is None else jnp.where(mask, a, 0.0)).astype(MXU_DTYPE))
            pvs = [jnp.dot(vT_ref[j, hsl[h], :], probs[h], preferred_element_type=F32) for h in range(HP)]
            return tuple((carry[h][0] + jnp.sum(lbs[h], axis=0, keepdims=True), carry[h][1] + pvs[h]) for h in range(HP))

        init = tuple((jnp.zeros((1, blk), F32), jnp.zeros((HEAD_DIM, blk), F32)) for _ in range(HP))
        carry = tile(i, init, strict)
        carry = lax.fori_loop(0, i, lambda jj, c: tile(i - 1 - jj, c, None), carry)
        for h in range(HP):
            oT_ref[h * HEAD_DIM:(h + 1) * HEAD_DIM, :] = carry[h][1]
            lt_ref[h, 0] = carry[h][0]

    hd = HP * HEAD_DIM
    return pl.pallas_call(
        body, name=name, grid=(4 // HP, nb),
        in_specs=[pl.BlockSpec((hd, blk), lambda p, i: (p, i)),
                  pl.BlockSpec((S, hd), lambda p, i: (0, kcb + p)),
                  pl.BlockSpec((nb, hd, blk), lambda p, i: (0, p, 0))],
        out_specs=[pl.BlockSpec((hd, blk), lambda p, i: (p, i)), pl.BlockSpec((HP, 1, 1, blk), lambda p, i: (p, i, 0, 0))],
        out_shape=[jax.ShapeDtypeStruct((GROUP_WIDTH, S), F32), jax.ShapeDtypeStruct((4, nb, 1, blk), F32)],
        compiler_params=_cparams("arbitrary", "arbitrary"),
    )(qT, h_att, vT3)


def _sb_bwd_t(qT, h_att, kT3, dmix, dmixT, ltot, *, dcb, qscale, blk, name):
    S = h_att.shape[0]
    nb = S // blk
    hd = HP * HEAD_DIM
    qcb, kcb, vcb = COL_SQ // hd, COL_SK // hd, COL_SV // hd
    dcr = dcb * 128 // hd

    def body(qT_ref, q_ref, k_ref, kT_ref, v_ref, do_ref, doT_ref, lt_ref, dqT_ref, dk_ref, dv_ref):
        i = pl.program_id(1)

        @pl.when(i == 0)
        def _():
            dk_ref[...] = jnp.zeros_like(dk_ref)
            dv_ref[...] = jnp.zeros_like(dv_ref)

        key = lax.broadcasted_iota(jnp.int32, (blk, blk), 0)
        qry = lax.broadcasted_iota(jnp.int32, (blk, blk), 1)
        strict = key < qry
        upto = (qry <= key).astype(MXU_DTYPE)
        before = (qry < key).astype(MXU_DTYPE)
        per_head = []
        for h in range(HP):
            hs = slice(h * HEAD_DIM, (h + 1) * HEAD_DIM)
            per_head.append(dict(qT=qT_ref[hs, :], q=q_ref[:, hs], doT=doT_ref[hs, :].astype(MXU_DTYPE),
                                 do=do_ref[:, hs].astype(MXU_DTYPE), lt=lt_ref[h, 0]))

        def tile(j, carry, mask):
            r0 = pl.multiple_of(j * blk, blk)
            rows = pl.ds(r0, blk)
            hsl = [slice(h * HEAD_DIM, (h + 1) * HEAD_DIM) for h in range(HP)]
            zs = [jnp.dot(k_ref[rows, hsl[h]], per_head[h]["qT"], preferred_element_type=F32) for h in range(HP)]
            das = [jnp.dot(v_ref[rows, hsl[h]], per_head[h]["doT"], preferred_element_type=F32) for h in range(HP)]
            lbs = []
            for h in range(HP):
                lb = _log1m_beta(zs[h])
                lbs.append(lb if mask is None else jnp.where(mask, lb, 0.0))
            sums = [_dot01_left(upto, lbs[h]) for h in range(HP)]
            probs, gs = [], []
            for h in range(HP):
                a = jnp.exp(zs[h] + lbs[h] + (per_head[h]["lt"] - carry[h][0] - sums[h]))
                a = a if mask is None else jnp.where(mask, a, 0.0)
                gs.append(das[h] * a)
                probs.append(a.astype(MXU_DTYPE))
            for h in range(HP):
                dv_ref[rows, hsl[h]] += jnp.dot(probs[h], per_head[h]["do"], preferred_element_type=F32)
            es = [_dot01_left(before, gs[h]) for h in range(HP)]
            dzs = []
            for h in range(HP):
                dz = gs[h] * jnp.exp(lbs[h]) - (carry[h][1] + es[h]) * jnp.exp(zs[h] + lbs[h])
                dzs.append((dz if mask is None else jnp.where(mask, dz, 0.0)).astype(MXU_DTYPE))
            for h in range(HP):
                dk_ref[rows, hsl[h]] += jnp.dot(dzs[h], per_head[h]["q"], preferred_element_type=F32)
            return tuple((carry[h][0] + jnp.sum(lbs[h], axis=0, keepdims=True),
                          carry[h][1] + jnp.sum(gs[h], axis=0, keepdims=True),
                          carry[h][2] + jnp.dot(kT_ref[j, hsl[h], :], dzs[h], preferred_element_type=F32))
                         for h in range(HP))

        zr = jnp.zeros((1, blk), F32)
        init = tuple((zr, zr, jnp.zeros((HEAD_DIM, blk), F32)) for _ in range(HP))
        carry = lax.fori_loop(0, i, lambda j, c: tile(j, c, None), init)
        carry = tile(i, carry, strict)
        for h in range(HP):
            dqT_ref[h * HEAD_DIM:(h + 1) * HEAD_DIM, :] = carry[h][2] * qscale

    return pl.pallas_call(
        body, name=name, grid=(4 // HP, nb),
        in_specs=[pl.BlockSpec((hd, blk), lambda p, i: (p, i)),
                  pl.BlockSpec((blk, hd), lambda p, i: (i, qcb + p)),
                  pl.BlockSpec((S, hd), lambda p, i: (0, kcb + p)),
                  pl.BlockSpec((nb, hd, blk), lambda p, i: (0, p, 0)),
                  pl.BlockSpec((S, hd), lambda p, i: (0, vcb + p)),
                  pl.BlockSpec((blk, hd), lambda p, i: (i, dcr + p)),
                  pl.BlockSpec((hd, blk), lambda p, i: (dcr + p, i)),
                  pl.BlockSpec((HP, 1, 1, blk), lambda p, i: (p, i, 0, 0))],
        out_specs=[pl.BlockSpec((hd, blk), lambda p, i: (p, i)),
                   pl.BlockSpec((S, hd), lambda p, i: (0, p)),
                   pl.BlockSpec((S, hd), lambda p, i: (0, p))],
        out_shape=[jax.ShapeDtypeStruct((GROUP_WIDTH, S), F32), jax.ShapeDtypeStruct((S, GROUP_WIDTH), F32),
                   jax.ShapeDtypeStruct((S, GROUP_WIDTH), F32)],
        compiler_params=_cparams("arbitrary", "arbitrary"),
    )(qT, h_att, h_att, kT3, h_att, dmix, dmixT, ltot)


def _swa_scores(q_ref, k_ref, n, h, start):
    g = h // 2
    kb = k_ref[pl.ds(start, 2 * WINDOW), g * HEAD_DIM:(g + 1) * HEAD_DIM]
    s = lax.dot_general(q_ref[:, h * HEAD_DIM:(h + 1) * HEAD_DIM], kb, NT, preferred_element_type=F32) * (HEAD_DIM ** -0.5)
    dist = (n * WINDOW + lax.broadcasted_iota(jnp.int32, (WINDOW, 2 * WINDOW), 0)
            - start - lax.broadcasted_iota(jnp.int32, (WINDOW, 2 * WINDOW), 1))
    s = s - SWA_SLOPES[h] * dist.astype(F32)
    valid = (dist >= 0) & (dist < WINDOW)
    return jnp.where(valid, s, NEG_INF), kb


def _swa_fwd(h_att, sinks, *, name):
    S = h_att.shape[0]
    nb = S // WINDOW
    qcb, kcb, vcb = COL_WQ // 256, COL_WK // 128, COL_WV // 128

    def body(sink_ref, q_ref, k_ref, v_ref, o_ref, lse_ref):
        n = pl.program_id(0)
        start = pl.multiple_of(jnp.maximum(n - 1, 0) * WINDOW, WINDOW)
        for h in range(4):
            g = h // 2
            s, _ = _swa_scores(q_ref, k_ref, n, h, start)
            sink = sink_ref[h]
            m = jnp.maximum(jnp.max(s, axis=1, keepdims=True), sink)
            e = jnp.exp(s - m)
            den = jnp.sum(e, axis=1, keepdims=True) + jnp.exp(sink - m)
            vb = v_ref[pl.ds(start, 2 * WINDOW), g * HEAD_DIM:(g + 1) * HEAD_DIM]
            o_ref[:, h * HEAD_DIM:(h + 1) * HEAD_DIM] = jnp.dot((e / den).astype(MXU_DTYPE), vb, preferred_element_type=F32)
            lse_ref[h] = m + jnp.log(den)

    return pl.pallas_call(
        body, name=name, grid=(nb,),
        in_specs=[pl.BlockSpec(memory_space=pltpu.SMEM),
                  pl.BlockSpec((WINDOW, 256), lambda n: (n, qcb)),
                  pl.BlockSpec((S, 128), lambda n: (0, kcb)),
                  pl.BlockSpec((S, 128), lambda n: (0, vcb))],
        out_specs=[pl.BlockSpec((WINDOW, 256), lambda n: (n, 0)), pl.BlockSpec((4, WINDOW, 1), lambda n: (0, n, 0))],
        out_shape=[jax.ShapeDtypeStruct((S, GROUP_WIDTH), F32), jax.ShapeDtypeStruct((4, S, 1), F32)],
        compiler_params=_cparams("arbitrary"),
    )(sinks, h_att, h_att, h_att)


def _swa_bwd(h_att, sinks, dmix, o_arr, lse, *, dcb, name):
    S = h_att.shape[0]
    nb = S // WINDOW
    qcb, kcb, vcb = COL_WQ // 256, COL_WK // 128, COL_WV // 128

    def body(sink_ref, q_ref, k_ref, v_ref, do_ref, o_ref, lse_ref, dq_ref, dk_ref, dv_ref, dsink_ref):
        n = pl.program_id(0)

        @pl.when(n == 0)
        def _():
            dk_ref[...] = jnp.zeros_like(dk_ref)
            dv_ref[...] = jnp.zeros_like(dv_ref)
            dsink_ref[...] = jnp.zeros_like(dsink_ref)

        start = pl.multiple_of(jnp.maximum(n - 1, 0) * WINDOW, WINDOW)
        for h in range(4):
            g = h // 2
            sl = slice(h * HEAD_DIM, (h + 1) * HEAD_DIM)
            gl = slice(g * HEAD_DIM, (g + 1) * HEAD_DIM)
            s, kb = _swa_scores(q_ref, k_ref, n, h, start)
            lse_h = lse_ref[h]
            pr = jnp.exp(s - lse_h)
            do = do_ref[:, sl]
            dob = do.astype(MXU_DTYPE)
            delta = jnp.sum(do * o_ref[:, sl], axis=1, keepdims=True)
            vb = v_ref[pl.ds(start, 2 * WINDOW), gl]
            ds = pr * (lax.dot_general(dob, vb, NT, preferred_element_type=F32) - delta)
            dsb = ds.astype(MXU_DTYPE)
            dq_ref[:, sl] = jnp.dot(dsb, kb, preferred_element_type=F32) * (HEAD_DIM ** -0.5)
            dk_ref[pl.ds(start, 2 * WINDOW), gl] += lax.dot_general(
                dsb, q_ref[:, sl], TN, preferred_element_type=F32) * (HEAD_DIM ** -0.5)
            dv_ref[pl.ds(start, 2 * WINDOW), gl] += lax.dot_general(pr.astype(MXU_DTYPE), dob, TN, preferred_element_type=F32)
            dsink_ref[h:h + 1, :] += jnp.zeros((1, 128), F32) - jnp.sum(jnp.exp(sink_ref[h] - lse_h) * delta)

    return pl.pallas_call(
        body, name=name, grid=(nb,),
        in_specs=[pl.BlockSpec(memory_space=pltpu.SMEM),
                  pl.BlockSpec((WINDOW, 256), lambda n: (n, qcb)),
                  pl.BlockSpec((S, 128), lambda n: (0, kcb)),
                  pl.BlockSpec((S, 128), lambda n: (0, vcb)),
                  pl.BlockSpec((WINDOW, 256), lambda n: (n, dcb)),
                  pl.BlockSpec((WINDOW, 256), lambda n: (n, 0)),
                  pl.BlockSpec((4, WINDOW, 1), lambda n: (0, n, 0))],
        out_specs=[pl.BlockSpec((WINDOW, 256), lambda n: (n, 0)),
                   pl.BlockSpec((S, 128), lambda n: (0, 0)),
                   pl.BlockSpec((S, 128), lambda n: (0, 0)),
                   pl.BlockSpec((4, 128), lambda n: (0, 0))],
        out_shape=[jax.ShapeDtypeStruct((S, GROUP_WIDTH), F32), jax.ShapeDtypeStruct((S, 128), F32),
                   jax.ShapeDtypeStruct((S, 128), F32), jax.ShapeDtypeStruct((4, 128), F32)],
        compiler_params=_cparams("arbitrary"),
    )(sinks, h_att, h_att, h_att, dmix, o_arr, lse)


def _tri(n, incl, upper):
    r = lax.broadcasted_iota(jnp.int32, (n, n), 0)
    c = lax.broadcasted_iota(jnp.int32, (n, n), 1)
    if upper:
        m = (r <= c) if incl else (r < c)
    else:
        m = (r >= c) if incl else (r > c)
    return m.astype(MXU_DTYPE)


def _fox_gate_fwd(fg, b_f, *, name):
    _, R, _ = fg.shape

    def body(b_ref, fg_ref, pos_ref, neg_ref):
        up_incl = _tri(128, True, True)
        ones = jnp.ones((128, 128), MXU_DTYPE)
        for h in range(4):
            z = fg_ref[h] + b_ref[h]
            logf = jnp.minimum(z, 0.0) - jnp.log(1.0 + jnp.exp(-jnp.abs(z)))
            within = _dot01(logf, up_incl, parts=3)
            totals = _dot01(logf, ones, parts=3)
            rem = within + _rows_other(totals, R, after=False)
            for part in range(3):
                piece = rem.astype(MXU_DTYPE)
                rem = rem - piece.astype(F32)
                pos_ref[h, part] = piece
                neg_ref[h, part] = -piece

    shape = (4, 3) + fg.shape[1:]
    return pl.pallas_call(
        body, name=name,
        in_specs=[pl.BlockSpec(memory_space=pltpu.SMEM), pl.BlockSpec(memory_space=pltpu.VMEM)],
        out_specs=[pl.BlockSpec(memory_space=pltpu.VMEM)] * 2,
        out_shape=[jax.ShapeDtypeStruct(shape, MXU_DTYPE)] * 2,
    )(b_f, fg)


def _rows_other(totals, n, after):
    r = lax.broadcasted_iota(jnp.int32, (n, n), 0)
    c = lax.broadcasted_iota(jnp.int32, (n, n), 1)
    m = ((c > r) if after else (c < r)).astype(MXU_DTYPE)
    acc = None
    rem = totals
    for _ in range(3):
        part = rem.astype(MXU_DTYPE)
        rem = rem - part.astype(F32)
        t = jnp.dot(m, part, preferred_element_type=F32)
        acc = t if acc is None else acc + t
    return acc


def _fox_gate_bwd(fg, b_f, dcum_k, dcum_q, *, q_unscale, name):
    _, R, _ = fg.shape

    def body(b_ref, fg_ref, dck_ref, dcq_ref, dfg_ref, db_ref):
        low_incl = _tri(128, True, False)
        ones = jnp.ones((128, 128), MXU_DTYPE)
        for h in range(4):
            dc = dcq_ref[h] * q_unscale - dck_ref[h]
            dlogf = _dot01(dc, low_incl, parts=3) + _rows_other(_dot01(dc, ones, parts=3), R, after=True)
            z = fg_ref[h] + b_ref[h]
            dz = dlogf * jnp.exp(jnp.minimum(-z, 0.0) - jnp.log(1.0 + jnp.exp(-jnp.abs(z))))
            dfg_ref[h] = dz
            db_ref[h:h + 1, :] = jnp.zeros((1, 128), F32) + jnp.sum(dz)

    return pl.pallas_call(
        body, name=name,
        in_specs=[pl.BlockSpec(memory_space=pltpu.SMEM)] + [pl.BlockSpec(memory_space=pltpu.VMEM)] * 3,
        out_specs=[pl.BlockSpec(memory_space=pltpu.VMEM), pl.BlockSpec(memory_space=pltpu.VMEM)],
        out_shape=[jax.ShapeDtypeStruct(fg.shape, F32), jax.ShapeDtypeStruct((4, 128), F32)],
    )(b_f, fg, dcum_k, dcum_q)


def _rope_rot(transpose):
    r = lax.broadcasted_iota(jnp.int32, (MLA_PAD, MLA_PAD), 0)
    c = lax.broadcasted_iota(jnp.int32, (MLA_PAD, MLA_PAD), 1)
    if transpose:
        r, c = c, r
    half = MLA_ROPE // 2
    lo, mid, hi = HEAD_DIM, HEAD_DIM + half, HEAD_DIM + MLA_ROPE
    minus = (c >= lo) & (c < mid) & (r == c + half)
    plus = (c >= mid) & (c < hi) & (r == c - half)
    return jnp.where(plus, 1.0, jnp.where(minus, -1.0, 0.0)).astype(MXU_DTYPE)


def _rope_lanes():
    lane = lax.broadcasted_iota(jnp.int32, (1, MLA_PAD), 1)
    return ((lane >= HEAD_DIM) & (lane < HEAD_DIM + MLA_ROPE)).astype(F32)


def _rms(x, g, eps=1e-6):
    r = lax.rsqrt(jnp.mean(x * x, axis=-1, keepdims=True) + eps)
    return x * r * g, r


def _rms_bwd(dy, x, r, g):
    xh = x * r
    dxh = dy * g
    dx = r * (dxh - xh * jnp.mean(dxh * xh, axis=-1, keepdims=True))
    return dx, dy * xh


def _mla_prep_fwd(lat, g_q, g_kv, wuq, wuk, wuv, cosm, sinm, *, bs, name):
    S = lat.shape[0]

    def body(lat_ref, gq_ref, gkv_ref, wuq_ref, wuk_ref, wuv_ref, cos_ref, sin_ref, q_ref, k_ref, v_ref):
        rot = _rope_rot(False)
        cosm_, sinm_ = cos_ref[...], sin_ref[...]
        nq, _ = _rms(lat_ref[:, 0:MLA_Q_RANK], gq_ref[...])
        nkv, _ = _rms(lat_ref[:, MLA_Q_RANK:MLA_Q_RANK + MLA_KV_RANK], gkv_ref[...])
        qlat = jnp.dot(nq.astype(MXU_DTYPE), wuq_ref[...], preferred_element_type=F32)
        klat = jnp.dot(nkv.astype(MXU_DTYPE), wuk_ref[...], preferred_element_type=F32)
        v_ref[...] = jnp.dot(nkv.astype(MXU_DTYPE), wuv_ref[...], preferred_element_type=F32).astype(v_ref.dtype)
        krb = lat_ref[:, 384:512]
        kr = krb * (cosm_ * _rope_lanes()) + _dot01(krb, rot, parts=3) * sinm_
        for h in range(4):
            sl = slice(h * MLA_PAD, (h + 1) * MLA_PAD)
            qh = qlat[:, sl]
            q_ref[:, sl] = ((qh * cosm_ + _dot01(qh, rot, parts=3) * sinm_) * (MLA_QK ** -0.5)).astype(q_ref.dtype)
            k_ref[:, sl] = (klat[:, sl] + kr).astype(k_ref.dtype)

    full = lambda a: pl.BlockSpec(a.shape, lambda i: (0,) * a.ndim)
    return pl.pallas_call(
        body, name=name, grid=(S // bs,),
        in_specs=[pl.BlockSpec((bs, LAT_W), lambda i: (i, 0)), full(g_q), full(g_kv), full(wuq), full(wuk), full(wuv),
                  pl.BlockSpec((bs, MLA_PAD), lambda i: (i, 0)), pl.BlockSpec((bs, MLA_PAD), lambda i: (i, 0))],
        out_specs=[pl.BlockSpec((bs, 512), lambda i: (i, 0)), pl.BlockSpec((bs, 512), lambda i: (i, 0)),
                   pl.BlockSpec((bs, 256), lambda i: (i, 0))],
        out_shape=[jax.ShapeDtypeStruct((S, 512), MXU_DTYPE), jax.ShapeDtypeStruct((S, 512), MXU_DTYPE),
                   jax.ShapeDtypeStruct((S, 256), MXU_DTYPE)],
        compiler_params=_cparams("parallel"),
    )(lat, g_q, g_kv, wuq, wuk, wuv, cosm, sinm)


def _mla_prep_bwd(lat, g_q, g_kv, wuq, wuk, wuv, cosm, sinm, dq, dk, dv, *, bs, name):
    S = lat.shape[0]

    def body(lat_ref, gq_ref, gkv_ref, wuq_ref, wuk_ref, wuv_ref, cos_ref, sin_ref, dq_ref, dk_ref, dv_ref,
             dlat_ref, dwuq_ref, dwuk_ref, dwuv_ref, dgq_ref, dgkv_ref):
        @pl.when(pl.program_id(0) == 0)
        def _():
            for r in (dwuq_ref, dwuk_ref, dwuv_ref, dgq_ref, dgkv_ref):
                r[...] = jnp.zeros_like(r)

        rot_t = _rope_rot(True)
        cosm_, sinm_ = cos_ref[...], sin_ref[...]
        cq = lat_ref[:, 0:MLA_Q_RANK]
        ckv = lat_ref[:, MLA_Q_RANK:MLA_Q_RANK + MLA_KV_RANK]
        nq, rq = _rms(cq, gq_ref[...])
        nkv, rkv = _rms(ckv, gkv_ref[...])
        nqb, nkvb = nq.astype(MXU_DTYPE), nkv.astype(MXU_DTYPE)

        dqlat = []
        dkr = jnp.zeros((bs, MLA_PAD), F32)
        for h in range(4):
            sl = slice(h * MLA_PAD, (h + 1) * MLA_PAD)
            dqh = dq_ref[:, sl]
            dqlat.append(dqh * cosm_ + _dot01(dqh * sinm_, rot_t, parts=3))
            dkr = dkr + dk_ref[:, sl]
        dqlat = jnp.concatenate(dqlat, axis=1).astype(MXU_DTYPE)
        dkb = dk_ref[...].astype(MXU_DTYPE)
        dvb = dv_ref[...].astype(MXU_DTYPE)

        dnq = lax.dot_general(dqlat, wuq_ref[...], NT, preferred_element_type=F32)
        dnkv = (lax.dot_general(dkb, wuk_ref[...], NT, preferred_element_type=F32)
                + lax.dot_general(dvb, wuv_ref[...], NT, preferred_element_type=F32))
        dwuq_ref[...] += lax.dot_general(nqb, dqlat, TN, preferred_element_type=F32)
        dwuk_ref[...] += lax.dot_general(nkvb, dkb, TN, preferred_element_type=F32)
        dwuv_ref[...] += lax.dot_general(nkvb, dvb, TN, preferred_element_type=F32)
        dcq, tq = _rms_bwd(dnq, cq, rq, gq_ref[...])
        dckv, tkv = _rms_bwd(dnkv, ckv, rkv, gkv_ref[...])
        dgq_ref[...] += jnp.sum(tq, axis=0, keepdims=True)
        dgkv_ref[...] += jnp.sum(tkv, axis=0, keepdims=True)
        dlat_ref[:, 0:MLA_Q_RANK] = dcq.astype(dlat_ref.dtype)
        dlat_ref[:, MLA_Q_RANK:MLA_Q_RANK + MLA_KV_RANK] = dckv.astype(dlat_ref.dtype)
        dkrb = dkr * (cosm_ * _rope_lanes()) + _dot01(dkr * sinm_, rot_t, parts=3)
        dlat_ref[:, 384:512] = dkrb.astype(dlat_ref.dtype)

    full = lambda a: pl.BlockSpec(a.shape, lambda i: (0,) * a.ndim)
    row = lambda w: pl.BlockSpec((bs, w), lambda i: (i, 0))
    acc = lambda *shape: pl.BlockSpec(shape, lambda i: (0,) * len(shape))
    return pl.pallas_call(
        body, name=name, grid=(S // bs,),
        in_specs=[row(LAT_W), full(g_q), full(g_kv), full(wuq), full(wuk), full(wuv), row(MLA_PAD), row(MLA_PAD),
                  row(512), row(512), row(256)],
        out_specs=[row(512), acc(256, 512), acc(128, 512), acc(128, 256), acc(1, 256), acc(1, 128)],
        out_shape=[jax.ShapeDtypeStruct((S, 512), MXU_DTYPE), jax.ShapeDtypeStruct((256, 512), F32),
                   jax.ShapeDtypeStruct((128, 512), F32), jax.ShapeDtypeStruct((128, 256), F32),
                   jax.ShapeDtypeStruct((1, 256), F32), jax.ShapeDtypeStruct((1, 128), F32)],
        compiler_params=_cparams("arbitrary"),
    )(lat, g_q, g_kv, wuq, wuk, wuv, cosm, sinm, dq, dk, dv)


def _row_spec(bs, w):
    return pl.BlockSpec((bs, w), lambda i: (i, 0))


def _vec_spec(w):
    return pl.BlockSpec((1, w), lambda i: (0, 0))


def _gnorm_fwd(outs, g, *, bs, name):
    S = outs[0].shape[0]

    def body(a_ref, b_ref, c_ref, d_ref, g_ref, o_ref):
        for k, ref in enumerate((a_ref, b_ref, c_ref, d_ref)):
            sl = slice(k * GROUP_WIDTH, (k + 1) * GROUP_WIDTH)
            y, _ = _rms(ref[...], g_ref[:, sl])
            o_ref[:, sl] = y.astype(o_ref.dtype)

    return pl.pallas_call(
        body, name=name, grid=(S // bs,),
        in_specs=[_row_spec(bs, GROUP_WIDTH)] * 4 + [_vec_spec(D_MODEL)],
        out_specs=_row_spec(bs, D_MODEL), out_shape=jax.ShapeDtypeStruct((S, D_MODEL), MXU_DTYPE),
        compiler_params=_cparams("parallel"),
    )(*outs, g)


def _gnorm_bwd(dgn, outs, g, *, bs, name):
    S = dgn.shape[0]

    def body(dgn_ref, a_ref, b_ref, c_ref, d_ref, g_ref, dmix_ref, dg_ref):
        @pl.when(pl.program_id(0) == 0)
        def _():
            dg_ref[...] = jnp.zeros_like(dg_ref)

        for k, ref in enumerate((a_ref, b_ref, c_ref, d_ref)):
            sl = slice(k * GROUP_WIDTH, (k + 1) * GROUP_WIDTH)
            x = ref[...]
            _, r = _rms(x, g_ref[:, sl])
            dx, t = _rms_bwd(dgn_ref[:, sl], x, r, g_ref[:, sl])
            dmix_ref[:, sl] = dx
            dg_ref[:, sl] += jnp.sum(t, axis=0, keepdims=True)

    return pl.pallas_call(
        body, name=name, grid=(S // bs,),
        in_specs=[_row_spec(bs, D_MODEL)] + [_row_spec(bs, GROUP_WIDTH)] * 4 + [_vec_spec(D_MODEL)],
        out_specs=[_row_spec(bs, D_MODEL), _vec_spec(D_MODEL)],
        out_shape=[jax.ShapeDtypeStruct((S, D_MODEL), F32), jax.ShapeDtypeStruct((1, D_MODEL), F32)],
        compiler_params=_cparams("arbitrary"),
    )(dgn, *outs, g)


def _ln_fwd(u, g, b, *, bs, name):
    S = u.shape[0]

    def body(u_ref, g_ref, b_ref, y_ref, yb_ref, xh_ref, rs_ref):
        x = u_ref[...]
        mu = jnp.mean(x, axis=-1, keepdims=True)
        xc = x - mu
        rs = lax.rsqrt(jnp.mean(xc * xc, axis=-1, keepdims=True) + 1e-5)
        xh = xc * rs
        y = xh * g_ref[...] + b_ref[...]
        y_ref[...] = y
        yb_ref[...] = y.astype(yb_ref.dtype)
        xh_ref[...] = xh
        rs_ref[...] = rs

    return pl.pallas_call(
        body, name=name, grid=(S // bs,),
        in_specs=[_row_spec(bs, D_MODEL), _vec_spec(D_MODEL), _vec_spec(D_MODEL)],
        out_specs=[_row_spec(bs, D_MODEL), _row_spec(bs, D_MODEL), _row_spec(bs, D_MODEL), _row_spec(bs, 1)],
        out_shape=[jax.ShapeDtypeStruct((S, D_MODEL), F32), jax.ShapeDtypeStruct((S, D_MODEL), MXU_DTYPE),
                   jax.ShapeDtypeStruct((S, D_MODEL), F32), jax.ShapeDtypeStruct((S, 1), F32)],
        compiler_params=_cparams("parallel"),
    )(u, g, b)


def _ln_bwd(dy, xh, rs, g, *, bs, name):
    S = dy.shape[0]

    def body(dy_ref, xh_ref, rs_ref, g_ref, du_ref, dub_ref, dg_ref, db_ref):
        @pl.when(pl.program_id(0) == 0)
        def _():
            dg_ref[...] = jnp.zeros_like(dg_ref)
            db_ref[...] = jnp.zeros_like(db_ref)

        dy_, xh_ = dy_ref[...], xh_ref[...]
        dxh = dy_ * g_ref[...]
        du = rs_ref[...] * (dxh - jnp.mean(dxh, axis=-1, keepdims=True)
                            - xh_ * jnp.mean(dxh * xh_, axis=-1, keepdims=True))
        du_ref[...] = du
        dub_ref[...] = du.astype(dub_ref.dtype)
        dg_ref[...] += jnp.sum(dy_ * xh_, axis=0, keepdims=True)
        db_ref[...] += jnp.sum(dy_, axis=0, keepdims=True)

    return pl.pallas_call(
        body, name=name, grid=(S // bs,),
        in_specs=[_row_spec(bs, D_MODEL), _row_spec(bs, D_MODEL), _row_spec(bs, 1), _vec_spec(D_MODEL)],
        out_specs=[_row_spec(bs, D_MODEL), _row_spec(bs, D_MODEL), _vec_spec(D_MODEL), _vec_spec(D_MODEL)],
        out_shape=[jax.ShapeDtypeStruct((S, D_MODEL), F32), jax.ShapeDtypeStruct((S, D_MODEL), MXU_DTYPE),
                   jax.ShapeDtypeStruct((1, D_MODEL), F32), jax.ShapeDtypeStruct((1, D_MODEL), F32)],
        compiler_params=_cparams("arbitrary"),
    )(dy, xh, rs, g)


def _swiglu_fwd(gu, *, bs, name):
    S = gu.shape[0]

    def body(gu_ref, a_ref):
        gt = gu_ref[:, :D_FF]
        a_ref[...] = (gt / (1.0 + jnp.exp(-gt)) * gu_ref[:, D_FF:]).astype(a_ref.dtype)

    return pl.pallas_call(
        body, name=name, grid=(S // bs,),
        in_specs=[_row_spec(bs, 2 * D_FF)],
        out_specs=_row_spec(bs, D_FF), out_shape=jax.ShapeDtypeStruct((S, D_FF), MXU_DTYPE),
        compiler_params=_cparams("parallel"),
    )(gu)


def _swiglu_bwd(da, gu, *, bs, name):
    S = gu.shape[0]

    def body(da_ref, gu_ref, dgu_ref):
        gt, da_ = gu_ref[:, :D_FF], da_ref[...]
        sg = 1.0 / (1.0 + jnp.exp(-gt))
        silu = gt * sg
        dgu_ref[:, :D_FF] = (da_ * gu_ref[:, D_FF:] * (sg + silu * (1.0 - sg))).astype(dgu_ref.dtype)
        dgu_ref[:, D_FF:] = (da_ * silu).astype(dgu_ref.dtype)

    return pl.pallas_call(
        body, name=name, grid=(S // bs,),
        in_specs=[_row_spec(bs, D_FF), _row_spec(bs, 2 * D_FF)],
        out_specs=_row_spec(bs, 2 * D_FF), out_shape=jax.ShapeDtypeStruct((S, 2 * D_FF), MXU_DTYPE),
        compiler_params=_cparams("parallel"),
    )(da, gu)


def _loss_head(y, target, *, bs, name):
    S = y.shape[0]

    def body(y_ref, t_ref, dy_ref, loss_ref):
        @pl.when(pl.program_id(0) == 0)
        def _():
            loss_ref[...] = jnp.zeros_like(loss_ref)

        e = y_ref[...] - t_ref[...]
        dy_ref[...] = e * (1.0 / D_MODEL)
        per_tok = jnp.mean(e * e, axis=-1, keepdims=True)
        loss_ref[...] += 0.5 * jnp.sum(per_tok, axis=0, keepdims=True)

    return pl.pallas_call(
        body, name=name, grid=(S // bs,),
        in_specs=[_row_spec(bs, D_MODEL), _row_spec(bs, D_MODEL)],
        out_specs=[_row_spec(bs, D_MODEL), pl.BlockSpec((1, 1), lambda i: (0, 0))],
        out_shape=[jax.ShapeDtypeStruct((S, D_MODEL), F32), jax.ShapeDtypeStruct((1, 1), F32)],
        compiler_params=_cparams("arbitrary"),
    )(y, target)


def _blk(n, target):
    if n <= target:
        return n
    best = None
    for b in range(128, target + 1, 128):
        if n % b == 0:
            best = b
    assert best is not None, n
    return best


def _rope_tables(S):
    pos = jnp.arange(S, dtype=F32)
    inv = ROPE_THETA ** (-jnp.arange(0, MLA_ROPE, 2, dtype=F32) / MLA_ROPE)
    ang = pos[:, None] * inv[None, :]
    cos, sin = jnp.cos(ang), jnp.sin(ang)
    one, zero, pad = jnp.ones((S, HEAD_DIM), F32), jnp.zeros((S, HEAD_DIM), F32), jnp.zeros((S, MLA_PAD - MLA_QK), F32)
    return jnp.concatenate([one, cos, cos, pad], axis=1), jnp.concatenate([zero, sin, sin, pad], axis=1)


def _prep_weights(w_in, w_uq, w_ukv, w_o, w_gate, w_up, w_down):
    z = lambda n: jnp.zeros((D_MODEL, n), w_in.dtype)
    win_a = jnp.concatenate([w_in[:, 0:768], w_in[:, 1188:2468]], axis=1)
    win_l = jnp.concatenate([w_in[:, 772:1156], z(64), w_in[:, 1156:1188], z(32), w_in[:, 768:772], z(124)], axis=1)
    kv = w_ukv.reshape(MLA_KV_RANK, 4, 2 * HEAD_DIM)
    return dict(
        win_a=win_a, win_l=win_l, win_p=jnp.concatenate([win_a, win_l], axis=1),
        wuq=jnp.pad(w_uq.reshape(MLA_Q_RANK, 4, MLA_QK), ((0, 0), (0, 0), (0, MLA_PAD - MLA_QK))).reshape(MLA_Q_RANK, 512),
        wuk=jnp.pad(kv[:, :, :HEAD_DIM], ((0, 0), (0, 0), (0, HEAD_DIM))).reshape(MLA_KV_RANK, 512),
        wuv=kv[:, :, HEAD_DIM:].reshape(MLA_KV_RANK, 256),
        w_o=w_o, wgu=jnp.concatenate([w_gate, w_up], axis=1), w_down=w_down)


def _unprep_grads(dwin_p, dwuq, dwuk, dwuv, dwo, dwgu, dwd):
    dw_in = jnp.concatenate([dwin_p[:, 0:768], dwin_p[:, 2560:2564], dwin_p[:, 2048:2432], dwin_p[:, 2496:2528],
                             dwin_p[:, 768:2048]], axis=1)
    dw_uq = dwuq.reshape(MLA_Q_RANK, 4, MLA_PAD)[:, :, :MLA_QK].reshape(MLA_Q_RANK, 4 * MLA_QK)
    dw_ukv = jnp.concatenate([dwuk.reshape(MLA_KV_RANK, 4, MLA_PAD)[:, :, :HEAD_DIM],
                              dwuv.reshape(MLA_KV_RANK, 4, HEAD_DIM)], axis=2).reshape(MLA_KV_RANK, 512)
    return dict(w_in=dw_in, mla_w_uq=dw_uq, mla_w_ukv=dw_ukv, w_o=dwo, w_gate=dwgu[:, :D_FF], w_up=dwgu[:, D_FF:],
                w_down=dwd)


def _layer_fwd(l, x, xb, W, P, tabs, blk):
    S = x.shape[0]
    nb = S // blk
    n = lambda s: f"l{l}_{s}"
    bs = min(512, S)
    h_att = _mm(xb, W["win_a"], name=n("in_att"), out_dtype=MXU_DTYPE, bm=1024, bn=1024, bk=1024, colscale=Q_COLSCALE)
    lat = _mm(xb, W["win_l"], name=n("in_lat"), bm=2048, bn=LAT_W, bk=1024)
    fg = lat[:, 512:516].T.reshape(4, S // 128, 128)
    cpos, cneg = _fox_gate_fwd(fg, P["fox_b_f"], name=n("fox_gate"))
    one3 = jnp.ones((S, 4, 3), MXU_DTYPE)
    zpad = jnp.zeros((S, 4, MLA_PAD - HEAD_DIM - 6), MXU_DTYPE)
    per_tok = lambda parts: parts.reshape(4, 3, S).transpose(2, 0, 1)
    q_f = jnp.concatenate([h_att[:, COL_FQ:COL_FQ + 256].reshape(S, 4, HEAD_DIM), per_tok(cpos), one3, zpad],
                          axis=2).reshape(S, 4 * MLA_PAD)
    k_f = jnp.concatenate([h_att[:, COL_FK:COL_FK + 256].reshape(S, 4, HEAD_DIM), one3, per_tok(cneg), zpad],
                          axis=2).reshape(S, 4 * MLA_PAD)
    v_f = h_att[:, COL_FV:COL_FV + 256]
    oT_a, lse_a = _smax_fwd_t(q_f.T, k_f, _kv_blocks_t(v_f, blk), dk=MLA_PAD, blk=blk, name=n("fox_fwd"))
    q_m, k_m, v_m = _mla_prep_fwd(lat, P["mla_g_q"], P["mla_g_kv"], W["wuq"], W["wuk"], W["wuv"], *tabs,
                                  bs=bs, name=n("mla_prep"))
    oT_b, lse_b = _smax_fwd_t(q_m.T, k_m, _kv_blocks_t(v_m, blk), dk=MLA_PAD, blk=blk, name=n("mla_fwd"))
    qT_c = h_att[:, COL_SQ:COL_SQ + 256].T
    oT_c, lt_c = _sb_fwd_t(qT_c, h_att, _kv_blocks_t(h_att[:, COL_SV:COL_SV + 256], blk), blk=blk, name=n("sb_fwd"))
    out_d, lse_d = _swa_fwd(h_att, P["swa_sinks"], name=n("swa_fwd"))
    outs = (oT_a.T, oT_b.T, oT_c.T, out_d)
    gn = _gnorm_fwd(outs, P["mix_g"], bs=bs, name=n("gnorm"))
    u1 = _mm(gn, W["w_o"], name=n("out_proj"), bm=1024, bn=1024, bk=1024, resid=x, alpha=ALPHA)
    x1, x1b, xh1, rs1 = _ln_fwd(u1, P["ln1_g"], P["ln1_b"], bs=bs, name=n("ln1"))
    gu = _mm(x1b, W["wgu"], name=n("gate_up"), bm=2048, bn=512, bk=1024)
    a = _swiglu_fwd(gu, bs=min(256, S), name=n("swiglu"))
    u2 = _mm(a, W["w_down"], name=n("down"), bm=1024, bn=1024, bk=_blk(D_FF, 1408), resid=x1, alpha=ALPHA)
    x2, x2b, xh2, rs2 = _ln_fwd(u2, P["ln2_g"], P["ln2_b"], bs=bs, name=n("ln2"))
    saved = dict(xb=xb, h_att=h_att, lat=lat, fg=fg, outs=outs, oT_a=oT_a, oT_b=oT_b, q_f=q_f, k_f=k_f, v_f=v_f,
                 qT_c=qT_c, lse_a=lse_a, lse_b=lse_b, lse_d=lse_d, lt_c=lt_c, q_m=q_m, k_m=k_m, v_m=v_m, gn=gn,
                 xh1=xh1, rs1=rs1, x1b=x1b, gu=gu, a=a, xh2=xh2, rs2=rs2)
    return x2, x2b, saved


def _layer_bwd(l, dx2, sv, W, P, tabs, blk):
    S = dx2.shape[0]
    n = lambda s: f"l{l}_{s}"
    bs = min(512, S)
    h_att = sv["h_att"]
    du2, du2b, dg2, db2 = _ln_bwd(dx2, sv["xh2"], sv["rs2"], P["ln2_g"], bs=bs, name=n("ln2_bwd"))
    da = _mm(du2b, W["w_down"], name=n("down_dx"), tb=True, bm=1024, bn=_blk(D_FF, 1408), bk=1024)
    dwd = _mm(sv["a"].T, du2b, name=n("down_dw"), bm=_blk(D_FF, 1408), bn=1024, bk=1024)
    dgu = _swiglu_bwd(da, sv["gu"], bs=min(256, S), name=n("swiglu_bwd"))
    dx1 = _mm(dgu, W["wgu"], name=n("gate_up_dx"), tb=True, bm=1024, bn=1024, bk=_blk(2 * D_FF, 1408), resid=du2,
              alpha=ALPHA)
    dwgu = _mm(sv["x1b"].T, dgu, name=n("gate_up_dw"), bm=1024, bn=_blk(2 * D_FF, 1408), bk=1024)
    du1, du1b, dg1, db1 = _ln_bwd(dx1, sv["xh1"], sv["rs1"], P["ln1_g"], bs=bs, name=n("ln1_bwd"))
    dgn = _mm(du1b, W["w_o"], name=n("out_proj_dx"), tb=True, bm=1024, bn=1024, bk=1024)
    dwo = _mm(sv["gn"].T, du1b, name=n("out_proj_dw"), bm=1024, bn=1024, bk=1024)
    dmix, dmixg = _gnorm_bwd(dgn, sv["outs"], P["mix_g"], bs=bs, name=n("gnorm_bwd"))
    dmixT = dmix.T
    q_f, k_f = sv["q_f"], sv["k_f"]
    dqT_a, dk_a, dva = _smax_bwd_t(q_f.T, q_f, k_f, _kv_blocks_t(k_f, blk), sv["v_f"], dmix, dmixT, sv["oT_a"],
                                   sv["lse_a"], dk=MLA_PAD, dcb=0, qscale=HEAD_DIM ** -0.5, blk=blk, name=n("fox_bwd"))
    dq_a, dk_a = dqT_a.T.reshape(S, 4, MLA_PAD), dk_a.reshape(S, 4, MLA_PAD)
    dqa, dka = dq_a[:, :, :HEAD_DIM].reshape(S, 256), dk_a[:, :, :HEAD_DIM].reshape(S, 256)
    dcq = dq_a[:, :, HEAD_DIM].T.reshape(4, S // 128, 128)
    dck = dk_a[:, :, HEAD_DIM + 3].T.reshape(4, S // 128, 128)
    q_m, k_m = sv["q_m"], sv["k_m"]
    dqT_b, dkb, dvb = _smax_bwd_t(q_m.T, q_m, k_m, _kv_blocks_t(k_m, blk), sv["v_m"], dmix, dmixT, sv["oT_b"],
                                  sv["lse_b"], dk=MLA_PAD, dcb=2, qscale=MLA_QK ** -0.5, blk=blk, name=n("mla_bwd"))
    dqT_c, dkc, dvc = _sb_bwd_t(sv["qT_c"], h_att, _kv_blocks_t(h_att[:, COL_SK:COL_SK + 256], blk), dmix, dmixT,
                                sv["lt_c"], dcb=4, qscale=HEAD_DIM ** -0.5, blk=blk, name=n("sb_bwd"))
    dqc = dqT_c.T
    dqd, dkd, dvd, dsink = _swa_bwd(h_att, P["swa_sinks"], dmix, sv["outs"][3], sv["lse_d"], dcb=3, name=n("swa_bwd"))
    dlat, dwuq, dwuk, dwuv, dgq, dgkv = _mla_prep_bwd(
        sv["lat"], P["mla_g_q"], P["mla_g_kv"], W["wuq"], W["wuk"], W["wuv"], *tabs, dqT_b.T, dkb, dvb,
        bs=bs, name=n("mla_prep_bwd"))
    dfg, dbf = _fox_gate_bwd(sv["fg"], P["fox_b_f"], dck, dcq, q_unscale=HEAD_DIM ** 0.5, name=n("fox_gate_bwd"))
    dfg_blk = jnp.pad(dfg.reshape(4, S).T, ((0, 0), (0, 124)))
    dh = jnp.concatenate([t.astype(MXU_DTYPE) for t in (dqa, dka, dva, dqc, dkc, dvc, dqd, dkd, dvd, dlat, dfg_blk)], axis=1)
    dx = _mm(dh, W["win_p"], name=n("in_dx"), tb=True, bm=1024, bn=1024, bk=_blk(PERM_W, 1024), resid=du1, alpha=ALPHA)
    dwin_p = _mm(sv["xb"].T, dh, name=n("in_dw"), bm=1024, bn=_blk(PERM_W, 1024), bk=1024)
    grads = _unprep_grads(dwin_p, dwuq, dwuk, dwuv, dwo, dwgu, dwd)
    grads.update(fox_b_f=dbf[:, 0], mla_g_q=dgq[0], mla_g_kv=dgkv[0], swa_sinks=dsink[:, 0], mix_g=dmixg[0],
                 ln1_g=dg1[0], ln1_b=db1[0], ln2_g=dg2[0], ln2_b=db2[0])
    return dx, grads


BIG = ("w_in", "mla_w_uq", "mla_w_ukv", "w_o", "w_gate", "w_up", "w_down")
SMALL = ("fox_b_f", "mla_g_q", "mla_g_kv", "swa_sinks", "mix_g", "ln1_g", "ln1_b", "ln2_g", "ln2_b")
SHARD_AXIS = dict(w_in=2, mla_w_uq=2, mla_w_ukv=2, w_o=1, w_gate=2, w_up=2, w_down=1)
N_CHIPS = 4
ANY = pl.BlockSpec(memory_space=pl.ANY)


def _chip_exchange(tensors, *, scatter, name):
    nt = len(tensors)

    def body(*refs):
        ins, outs = refs[:nt], refs[nt:2 * nt]
        send_sems, recv_sems, local_sems = refs[2 * nt:]
        x, y, c = lax.axis_index("x"), lax.axis_index("y"), lax.axis_index("c")
        me = 2 * x + y
        peers = [(1 - x, y), (x, 1 - y), (1 - x, 1 - y)]
        local, sends, recvs = [], [], []
        for t in range(nt):
            local.append(pltpu.make_async_copy(ins[t].at[me] if scatter else ins[t], outs[t].at[me], local_sems.at[t]))
            for r, (px, py) in enumerate(peers):
                k = 3 * t + r
                theirs = 2 * px + py
                sends.append(pltpu.make_async_remote_copy(
                    src_ref=ins[t].at[theirs] if scatter else ins[t], dst_ref=outs[t].at[me],
                    send_sem=send_sems.at[k], recv_sem=recv_sems.at[k], device_id=(px, py, c), device_id_type=MESH))
                recvs.append(pltpu.make_async_remote_copy(
                    src_ref=ins[t].at[me] if scatter else ins[t], dst_ref=outs[t].at[theirs],
                    send_sem=send_sems.at[k], recv_sem=recv_sems.at[k], device_id=(px, py, c), device_id_type=MESH))
        for cp in local + sends:
            cp.start()
        for cp in recvs:
            cp.wait_recv()
        for cp in sends:
            cp.wait_send()
        for cp in local:
            cp.wait()

    out_shape = [jax.ShapeDtypeStruct(t.shape if scatter else (N_CHIPS,) + t.shape, t.dtype) for t in tensors]
    return pl.pallas_call(
        body, name=name, in_specs=[ANY] * nt, out_specs=[ANY] * nt, out_shape=out_shape,
        scratch_shapes=[pltpu.SemaphoreType.DMA((3 * nt,)), pltpu.SemaphoreType.DMA((3 * nt,)),
                        pltpu.SemaphoreType.DMA((nt,))],
        compiler_params=pltpu.CompilerParams(has_side_effects=True),
    )(*tensors)


HBM = pl.BlockSpec(memory_space=pltpu.HBM)
SEM = pl.BlockSpec(memory_space=pltpu.SEMAPHORE)
N_PEER_CHIPS = N_CHIPS - 1


def _peer_copies(src_ref, land_ref, sems, scatter):
    x, y, c = lax.axis_index("x"), lax.axis_index("y"), lax.axis_index("c")
    me = 2 * x + y
    out = []
    for r, (px, py) in enumerate([(1 - x, y), (x, 1 - y), (1 - x, 1 - y)]):
        theirs = 2 * px + py
        send = pltpu.make_async_remote_copy(
            src_ref=src_ref.at[theirs] if scatter else src_ref, dst_ref=land_ref.at[me],
            send_sem=sems[2 * r], recv_sem=sems[2 * r + 1], device_id=(px, py, c), device_id_type=MESH)
        arrive = pltpu.make_async_remote_copy(
            src_ref=src_ref.at[me] if scatter else src_ref, dst_ref=land_ref.at[theirs],
            send_sem=sems[2 * r], recv_sem=sems[2 * r + 1], device_id=(px, py, c), device_id_type=MESH)
        out.append((send, arrive))
    return out


def _exchange_start(srcs, *, scatter, name):
    nt = len(srcs)
    ns = 2 * N_PEER_CHIPS * nt
    land_shapes = [s.shape if scatter else (N_CHIPS,) + s.shape for s in srcs]

    def body(*refs):
        src_refs, land_refs, outs = refs[:nt], refs[nt:2 * nt], refs[2 * nt:]
        for t in range(nt):
            for send, _ in _peer_copies(src_refs[t], land_refs[t], outs[6 * t:6 * t + 6], scatter):
                send.start()
        outs[-1][...] = jnp.zeros_like(outs[-1])

    res = pl.pallas_call(
        body, name=name,
        out_shape=(*[pltpu.SemaphoreType.DMA(())] * ns, *[pltpu.HBM(s.shape, s.dtype) for s in srcs],
                   *[pltpu.HBM(ls, s.dtype) for ls, s in zip(land_shapes, srcs)], jax.ShapeDtypeStruct((8, 128), F32)),
        in_specs=(HBM,) * (2 * nt), out_specs=(*[SEM] * ns, *[HBM] * (2 * nt), pl.BlockSpec(memory_space=pltpu.VMEM)),
        input_output_aliases={i: ns + i for i in range(2 * nt)},
        compiler_params=pltpu.CompilerParams(has_side_effects=pltpu.SideEffectType.DATAFLOW_SIDE_EFFECTING),
    )(*[pltpu.with_memory_space_constraint(s, pltpu.HBM) for s in srcs],
      *[pltpu.with_memory_space_constraint(lax.empty(ls, s.dtype), pltpu.HBM) for ls, s in zip(land_shapes, srcs)])
    return dict(sems=res[:ns], srcs=res[ns:ns + nt], lands=res[ns + nt:ns + 2 * nt], token=res[-1])


def _exchange_wait(started, after, *, scatter, name):
    nt = len(started["srcs"])
    ns = 2 * N_PEER_CHIPS * nt

    def body(*refs):
        src_refs, land_refs, sems = refs[:nt], refs[nt:2 * nt], refs[2 * nt:2 * nt + ns]
        for t in range(nt):
            for send, arrive in _peer_copies(src_refs[t], land_refs[t], sems[6 * t:6 * t + 6], scatter):
                send.wait_send()
                arrive.wait_recv()

    both = list(started["srcs"]) + list(started["lands"])
    res = pl.pallas_call(
        body, name=name, out_shape=tuple(pltpu.HBM(a.shape, a.dtype) for a in both),
        in_specs=(*[HBM] * (2 * nt), *[SEM] * ns, ANY), out_specs=(HBM,) * (2 * nt),
        input_output_aliases={i: i for i in range(2 * nt)},
        compiler_params=pltpu.CompilerParams(has_side_effects=pltpu.SideEffectType.DATAFLOW_SIDE_EFFECTING),
    )(*both, *started["sems"], after)
    return res[:nt], res[nt:]


def _core_exchange(tensors, *, name):
    nt = len(tensors)

    def body(*refs):
        ins, outs = refs[:nt], refs[nt:2 * nt]
        send_sems, recv_sems = refs[2 * nt:]
        sibling = (lax.axis_index("x"), lax.axis_index("y"), 1 - lax.axis_index("c"))
        copies = [pltpu.make_async_remote_copy(src_ref=ins[t], dst_ref=outs[t], send_sem=send_sems.at[t],
                                               recv_sem=recv_sems.at[t], device_id=sibling, device_id_type=MESH)
                  for t in range(nt)]
        for cp in copies:
            cp.start()
        for cp in copies:
            cp.wait_recv()
        for cp in copies:
            cp.wait_send()

    return pl.pallas_call(
        body, name=name, in_specs=[ANY] * nt, out_specs=[ANY] * nt,
        out_shape=[jax.ShapeDtypeStruct(t.shape, t.dtype) for t in tensors],
        scratch_shapes=[pltpu.SemaphoreType.DMA((nt,)), pltpu.SemaphoreType.DMA((nt,))],
        compiler_params=pltpu.CompilerParams(has_side_effects=True),
    )(*tensors)


def _all_sum_small(block, *, name):
    R = block.shape[0]
    n_dev = 8

    def body(x_ref, o_ref, slots, send_sems, recv_sems):
        x, y, c = lax.axis_index("x"), lax.axis_index("y"), lax.axis_index("c")
        me = 4 * x + 2 * y + c
        slots[me] = x_ref[...]
        sends, recvs = [], []
        for d in range(1, n_dev):
            px, py, pc = x ^ (d >> 2), y ^ ((d >> 1) & 1), c ^ (d & 1)
            theirs = 4 * px + 2 * py + pc
            sends.append(pltpu.make_async_remote_copy(
                src_ref=x_ref, dst_ref=slots.at[me], send_sem=send_sems.at[d - 1], recv_sem=recv_sems.at[d - 1],
                device_id=(px, py, pc), device_id_type=MESH))
            recvs.append(pltpu.make_async_remote_copy(
                src_ref=x_ref, dst_ref=slots.at[theirs], send_sem=send_sems.at[d - 1], recv_sem=recv_sems.at[d - 1],
                device_id=(px, py, pc), device_id_type=MESH))
        for cp in sends:
            cp.start()
        for cp in recvs:
            cp.wait_recv()
        for cp in sends:
            cp.wait_send()
        total = slots[0]
        for k in range(1, n_dev):
            total = total + slots[k]
        o_ref[...] = total

    return pl.pallas_call(
        body, name=name, in_specs=[pl.BlockSpec(memory_space=pltpu.VMEM)],
        out_specs=pl.BlockSpec(memory_space=pltpu.VMEM), out_shape=jax.ShapeDtypeStruct((R, 128), F32),
        scratch_shapes=[pltpu.VMEM((n_dev, R, 128), F32), pltpu.SemaphoreType.DMA((n_dev - 1,)),
                        pltpu.SemaphoreType.DMA((n_dev - 1,))],
        compiler_params=pltpu.CompilerParams(has_side_effects=True),
    )(block)


def _sum_chips(recv, *, br, name):
    _, R, C = recv.shape

    def body(r_ref, o_ref):
        total = r_ref[0].astype(F32)
        for k in range(1, N_CHIPS):
            total = total + r_ref[k].astype(F32)
        o_ref[...] = total

    return pl.pallas_call(
        body, name=name, grid=(R // br,), in_specs=[pl.BlockSpec((N_CHIPS, br, C), lambda i: (0, i, 0))],
        out_specs=pl.BlockSpec((br, C), lambda i: (i, 0)), out_shape=jax.ShapeDtypeStruct((R, C), F32),
        compiler_params=_cparams("parallel"),
    )(recv)


def _sum_chips_into(acc, land, own, me, layer, *, br, name):
    _, R, C = land.shape

    def body(me_ref, land_ref, own_ref, acc_ref, o_ref):
        mine = me_ref[0]
        total = None
        for k in range(N_CHIPS):
            part = jnp.where(mine == k, own_ref[...], land_ref[k]).astype(F32)
            total = part if total is None else total + part
        o_ref[0] = total

    return pl.pallas_call(
        body, name=name, grid=(R // br,),
        in_specs=[pl.BlockSpec(memory_space=pltpu.SMEM), pl.BlockSpec((N_CHIPS, br, C), lambda i: (0, i, 0)),
                  pl.BlockSpec((br, C), lambda i: (i, 0)), ANY],
        out_specs=pl.BlockSpec((1, br, C), lambda i: (layer, i, 0)),
        out_shape=jax.ShapeDtypeStruct(acc.shape, F32), input_output_aliases={3: 0},
        compiler_params=_cparams("parallel"),
    )(me, land, own, acc)


def _adamw_math(w, g, m, v):
    m = ADAM_B1 * m + (1.0 - ADAM_B1) * g
    v = ADAM_B2 * v + (1.0 - ADAM_B2) * (g * g)
    m_hat = m / (1.0 - ADAM_B1 ** ADAM_STEP)
    v_hat = v / (1.0 - ADAM_B2 ** ADAM_STEP)
    return -ADAM_LR * (m_hat / (jnp.sqrt(v_hat) + ADAM_EPS) + ADAM_WD * w), m, v


def _adamw(w, m, v, g_a, g_b, *, br, name):
    R, C = w.shape
    two = g_b is not None

    def body(*refs):
        if two:
            w_ref, m_ref, v_ref, ga_ref, gb_ref, g_ref, d_ref, nm_ref, nv_ref = refs
            g = ga_ref[...] + gb_ref[...]
        else:
            w_ref, m_ref, v_ref, ga_ref, g_ref, d_ref, nm_ref, nv_ref = refs
            g = ga_ref[...]
        g_ref[...] = g
        d_ref[...], nm_ref[...], nv_ref[...] = _adamw_math(w_ref[...], g, m_ref[...], v_ref[...])

    spec = pl.BlockSpec((br, C), lambda i: (i, 0))
    args = [w, m, v, g_a] + ([g_b] if two else [])
    return pl.pallas_call(
        body, name=name, grid=(R // br,), in_specs=[spec] * len(args), out_specs=[spec] * 4,
        out_shape=[jax.ShapeDtypeStruct((R, C), F32)] * 4,
        compiler_params=_cparams("parallel"),
    )(*args)


SMALL_ROWS = dict(fox_b_f=1, mla_g_q=2, mla_g_kv=1, swa_sinks=1, mix_g=8, ln1_g=8, ln1_b=8, ln2_g=8, ln2_b=8)
SMALL_ROWS_PER_LAYER = sum(SMALL_ROWS.values())


def _pack_small(vals, extra_rows):
    L = vals[SMALL[0]].shape[0]
    per_layer = []
    for name in SMALL:
        a = vals[name].astype(F32)
        a = jnp.pad(a, ((0, 0), (0, SMALL_ROWS[name] * 128 - a.shape[1])))
        per_layer.append(a.reshape(L, SMALL_ROWS[name], 128))
    out = jnp.concatenate(per_layer, axis=1).reshape(L * SMALL_ROWS_PER_LAYER, 128)
    return jnp.pad(out, ((0, extra_rows), (0, 0)))


def _unpack_small(block, shapes):
    L = shapes[SMALL[0]][0]
    body = block[:L * SMALL_ROWS_PER_LAYER].reshape(L, SMALL_ROWS_PER_LAYER, 128)
    out, r = {}, 0
    for name in SMALL:
        n = shapes[name][1]
        out[name] = body[:, r:r + SMALL_ROWS[name]].reshape(L, SMALL_ROWS[name] * 128)[:, :n]
        r += SMALL_ROWS[name]
    return out


PACK_ROW_MULTIPLE = 256


def _pack(parts):
    flat = [p.reshape(-1, 128) for p in parts]
    pad = (-sum(f.shape[0] for f in flat)) % PACK_ROW_MULTIPLE
    if pad:
        flat.append(jnp.zeros((pad, 128), flat[0].dtype))
    return jnp.concatenate(flat, axis=0)


def _unpack(block, shapes):
    out, r = [], 0
    for shp in shapes:
        n = int(np.prod(shp)) // 128
        out.append(block[r:r + n].reshape(shp))
        r += n
    return out


def _shard(g, k, axis):
    n = g.shape[axis] // N_CHIPS
    return lax.slice_in_dim(g, k * n, (k + 1) * n, axis=axis)


def _to_chips(g, axis):
    L, a, b = g.shape
    if axis == 2:
        return g.reshape(L, a, N_CHIPS, b // N_CHIPS).transpose(2, 0, 1, 3)
    return g.reshape(L, N_CHIPS, a // N_CHIPS, b).transpose(1, 0, 2, 3)


def _from_chips(g, axis):
    _, L, a, b = g.shape
    if axis == 2:
        return g.transpose(1, 2, 0, 3).reshape(L, a, N_CHIPS * b)
    return g.transpose(1, 0, 2, 3).reshape(L, N_CHIPS * a, b)


def kernel(x, w_in, fox_b_f, mla_g_q, mla_g_kv, mla_w_uq, mla_w_ukv, swa_sinks, mix_g, w_o, ln1_g, ln1_b, w_gate, w_up, w_down, ln2_g, ln2_b, loss_target, m_w_in, m_fox_b_f, m_mla_g_q, m_mla_g_kv, m_mla_w_uq, m_mla_w_ukv, m_swa_sinks, m_mix_g, m_w_o, m_ln1_g, m_ln1_b, m_w_gate, m_w_up, m_w_down, m_ln2_g, m_ln2_b, v_w_in, v_fox_b_f, v_mla_g_q, v_mla_g_kv, v_mla_w_uq, v_mla_w_ukv, v_swa_sinks, v_mix_g, v_w_o, v_ln1_g, v_ln1_b, v_w_gate, v_w_up, v_w_down, v_ln2_g, v_ln2_b):
    w = dict(w_in=w_in, fox_b_f=fox_b_f, mla_g_q=mla_g_q, mla_g_kv=mla_g_kv, mla_w_uq=mla_w_uq, mla_w_ukv=mla_w_ukv,
             swa_sinks=swa_sinks, mix_g=mix_g, w_o=w_o, ln1_g=ln1_g, ln1_b=ln1_b, w_gate=w_gate, w_up=w_up,
             w_down=w_down, ln2_g=ln2_g, ln2_b=ln2_b)
    m = dict(w_in=m_w_in, fox_b_f=m_fox_b_f, mla_g_q=m_mla_g_q, mla_g_kv=m_mla_g_kv, mla_w_uq=m_mla_w_uq,
             mla_w_ukv=m_mla_w_ukv, swa_sinks=m_swa_sinks, mix_g=m_mix_g, w_o=m_w_o, ln1_g=m_ln1_g, ln1_b=m_ln1_b,
             w_gate=m_w_gate, w_up=m_w_up, w_down=m_w_down, ln2_g=m_ln2_g, ln2_b=m_ln2_b)
    v = dict(w_in=v_w_in, fox_b_f=v_fox_b_f, mla_g_q=v_mla_g_q, mla_g_kv=v_mla_g_kv, mla_w_uq=v_mla_w_uq,
             mla_w_ukv=v_mla_w_ukv, swa_sinks=v_swa_sinks, mix_g=v_mix_g, w_o=v_w_o, ln1_g=v_ln1_g, ln1_b=v_ln1_b,
             w_gate=v_w_gate, w_up=v_w_up, w_down=v_w_down, ln2_g=v_ln2_g, ln2_b=v_ln2_b)
    names = tuple(w)
    L = w_in.shape[0]
    S = x.shape[1]
    blk = min(256, S)
    bs = min(512, S)

    me = 2 * lax.axis_index("x") + lax.axis_index("y")
    axes = [SHARD_AXIS[k] - 1 for k in BIG]

    started = []
    for l in range(L):
        srcs = [w[k][l].astype(MXU_DTYPE) for k in BIG]
        if started:
            srcs[2] = srcs[2] + started[-1]["token"][0, 0].astype(MXU_DTYPE)
        started.append(_exchange_start(srcs, scatter=False, name=f"gather_start{l}"))
    all_started = sum(st["token"] for st in started)
    tabs = _rope_tables(S)
    Ps = []
    for l in range(L):
        P = dict(fox_b_f=fox_b_f[l], swa_sinks=swa_sinks[l])
        for k in ("mla_g_q", "mla_g_kv", "mix_g", "ln1_g", "ln1_b", "ln2_g", "ln2_b"):
            P[k] = w[k][l][None, :]
        Ps.append(P)

    xa = x[0]
    xb = xa.astype(MXU_DTYPE)
    saved, Ws = [], []
    for l in range(L):
        mine, lands = _exchange_wait(started[l], all_started if l == 0 else xa, scatter=False, name=f"gather_wait{l}")
        full = [jnp.concatenate([jnp.where(me == k, mine[t], lands[t][k]) for k in range(N_CHIPS)], axis=axes[t])
                for t in range(len(BIG))]
        Ws.append(_prep_weights(*full))
        xa, xb, sv = _layer_fwd(l, xa, xb, Ws[l], Ps[l], tabs, blk)
        saved.append(sv)
    dx, loss_part = _loss_head(xa, loss_target[0], bs=bs, name="loss_head")

    layer_grads = [None] * L
    sent = [None] * L
    pin = None
    for l in reversed(range(L)):
        P = Ps[l] if pin is None else dict(Ps[l], ln2_g=Ps[l]["ln2_g"] + pin[0, 0])
        dx, layer_grads[l] = _layer_bwd(l, dx, saved[l], Ws[l], P, tabs, blk)
        to_owner = [_to_chips(layer_grads[l][name][None], axes[t] + 1)[:, 0].astype(MXU_DTYPE)
                    for t, name in enumerate(BIG)]
        sent[l] = _exchange_start(to_owner, scatter=True, name=f"scatter_start{l}")
        pin = sent[l]["token"]
    grad_x = dx[None]

    me_arr = me.astype(jnp.int32)[None]
    partial = [jnp.zeros(w[k].shape, F32) for k in BIG]
    for l in range(L):
        mine, lands = _exchange_wait(sent[l], dx, scatter=True, name=f"scatter_wait{l}")
        for t, k in enumerate(BIG):
            own = lax.dynamic_index_in_dim(mine[t], me, 0, keepdims=False)
            partial[t] = _sum_chips_into(partial[t], lands[t], own, me_arr, l, br=_rows(own.shape[0]),
                                         name=f"sum_{k}_l{l}")
    sibling = _core_exchange(partial, name="swap_partials")
    local = {k: jnp.stack([layer_grads[l][k] for l in range(L)]) for k in SMALL}
    out = {}
    for k, mine, theirs in zip(BIG, partial, sibling):
        shp = w[k].shape
        two_d = lambda a: a.reshape(shp[0] * shp[1], shp[2])
        res = _adamw(two_d(w[k]), two_d(m[k]), two_d(v[k]), two_d(mine), two_d(theirs), br=_rows(shp[0] * shp[1]),
                     name=f"adamw_{k}")
        out[k] = [a.reshape(shp) for a in res]

    shapes = {k: w[k].shape for k in SMALL}
    extra = 8 + (-L * SMALL_ROWS_PER_LAYER) % 8
    block = _pack_small({k: local[k] for k in SMALL}, extra)
    block = block.at[L * SMALL_ROWS_PER_LAYER, 0].set(loss_part[0, 0])
    total = _all_sum_small(block, name="sum_small")
    loss = total[L * SMALL_ROWS_PER_LAYER, 0]
    res = _adamw(_pack_small({k: w[k] for k in SMALL}, extra), _pack_small({k: m[k] for k in SMALL}, extra),
                 _pack_small({k: v[k] for k in SMALL}, extra), total, None, br=total.shape[0], name="adamw_small")
    res = [_unpack_small(t, shapes) for t in res]
    for k in SMALL:
        out[k] = [r[k] for r in res]

    return (loss, grad_x, *[out[k][0] for k in names], *[out[k][1] for k in names],
            *[out[k][2] for k in names], *[out[k][3] for k in names])


def _rows(n):
    for b in (256, 128, 64, 32, 16, 8):
        if n % b == 0:
            return b
    return n
```

```python
import functools

import numpy as np
import jax
import jax.numpy as jnp
from jax import lax
from jax.experimental import pallas as pl
from jax.experimental.pallas import tpu as pltpu

F32 = jnp.float32
MXU_DTYPE = jnp.bfloat16
NEG_INF = -1e30

D_MODEL = 1024
DEPTH = 4
HEAD_DIM = 64
GROUP_WIDTH = 256
N_GROUPS = 4
D_FF = 2816
MLA_Q_RANK = 256
MLA_KV_RANK = 128
MLA_ROPE = 32
MLA_QK = 96
MLA_PAD = 128
ROPE_THETA = 10000.0
WINDOW = 128
ALPHA = (2.0 * DEPTH) ** 0.25
SWA_SLOPES = tuple(float(2.0 ** (-8.0 * h / 4)) for h in range(1, 5))
IN_WIDTH = 2468
ATT_W = 2048
LAT_W = 640
PERM_W = ATT_W + LAT_W
COL_FQ, COL_FK, COL_FV = 0, 256, 512
COL_SQ, COL_SK, COL_SV = 768, 1024, 1280
COL_WQ, COL_WK, COL_WV = 1536, 1792, 1920
Q_COLSCALE = np.ones((1, ATT_W), np.float32)
Q_COLSCALE[:, COL_FQ:COL_FQ + 256] = HEAD_DIM ** -0.5
Q_COLSCALE[:, COL_SQ:COL_SQ + 256] = HEAD_DIM ** -0.5

ADAM_LR, ADAM_B1, ADAM_B2, ADAM_EPS, ADAM_WD, ADAM_STEP = 0.001, 0.9, 0.999, 1e-08, 0.01, 10

VMEM_LIMIT = 56 * 1024 * 1024
NT = (((1,), (1,)), ((), ()))
TN = (((0,), (0,)), ((), ()))
MESH = pl.DeviceIdType.MESH


def _cparams(*sem):
    return pltpu.CompilerParams(dimension_semantics=sem, vmem_limit_bytes=VMEM_LIMIT)


def _split2(x):
    hi = x.astype(MXU_DTYPE)
    lo = (x - hi.astype(F32)).astype(MXU_DTYPE)
    return hi, lo


def _dot01(x, m01, dn=None, parts=2):
    acc = None
    rem = x
    for _ in range(parts):
        part = rem.astype(MXU_DTYPE)
        rem = rem - part.astype(F32)
        if dn is None:
            t = jnp.dot(part, m01, preferred_element_type=F32)
        else:
            t = lax.dot_general(part, m01, dn, preferred_element_type=F32)
        acc = t if acc is None else acc + t
    return acc


def _mm(a, b, *, name, ta=False, tb=False, out_dtype=F32, bm=512, bn=512, bk=512, resid=None, alpha=1.0,
        colscale=None):
    M, K = (a.shape[1], a.shape[0]) if ta else a.shape
    N = b.shape[0] if tb else b.shape[1]
    assert (b.shape[1] if tb else b.shape[0]) == K
    assert resid is None or colscale is None
    bm, bn, bk = min(bm, M), min(bn, N), min(bk, K)
    assert M % bm == 0 and N % bn == 0 and K % bk == 0, (name, M, N, K, bm, bn, bk)
    nk = K // bk
    dn = (((0 if ta else 1,), (1 if tb else 0,)), ((), ()))

    extra = resid is not None or colscale is not None

    def body(*refs):
        a_ref, b_ref = refs[:2]
        r_ref = refs[2] if extra else None
        o_ref = refs[3] if extra else refs[2]
        acc_ref = refs[-1] if nk > 1 else None
        k = pl.program_id(2)
        part = lax.dot_general(a_ref[...].astype(MXU_DTYPE), b_ref[...].astype(MXU_DTYPE), dn,
                               preferred_element_type=F32)

        def finish(r):
            if resid is not None:
                r = r + alpha * r_ref[...]
            if colscale is not None:
                r = r * r_ref[...]
            o_ref[...] = r.astype(o_ref.dtype)

        if nk == 1:
            finish(part)
        else:
            @pl.when(k == 0)
            def _():
                acc_ref[...] = part

            @pl.when((k > 0) & (k < nk - 1))
            def _():
                acc_ref[...] += part

            @pl.when(k == nk - 1)
            def _():
                finish(acc_ref[...] + part)

    a_spec = pl.BlockSpec((bk, bm), lambda i, j, k: (k, i)) if ta else pl.BlockSpec((bm, bk), lambda i, j, k: (i, k))
    b_spec = pl.BlockSpec((bn, bk), lambda i, j, k: (j, k)) if tb else pl.BlockSpec((bk, bn), lambda i, j, k: (k, j))
    in_specs = [a_spec, b_spec]
    args = [a, b]
    if resid is not None:
        in_specs.append(pl.BlockSpec((bm, bn), lambda i, j, k: (i, j)))
        args.append(resid)
    if colscale is not None:
        in_specs.append(pl.BlockSpec((1, bn), lambda i, j, k: (0, j)))
        args.append(colscale)
    return pl.pallas_call(
        body, name=name, grid=(M // bm, N // bn, nk), in_specs=in_specs,
        out_specs=pl.BlockSpec((bm, bn), lambda i, j, k: (i, j)),
        out_shape=jax.ShapeDtypeStruct((M, N), out_dtype),
        scratch_shapes=[pltpu.VMEM((bm, bn), F32)] if nk > 1 else [],
        compiler_params=_cparams("parallel", "parallel", "arbitrary"),
    )(*args)


def _softmax_attn_fwd(q_arr, k_arr, v_arr, *, qcb, kcb, vcb, dk, scale, cum_col=None, cum_row=None, blk, name):
    S = q_arr.shape[0]
    nb = S // blk
    bias = cum_col is not None
    W = 2 * dk

    def body(*refs):
        if bias:
            q_ref, k_ref, v_ref, cc_ref, cr_ref, o_ref, lse_ref = refs
        else:
            q_ref, k_ref, v_ref, o_ref, lse_ref = refs
        p = pl.program_id(0)
        i = pl.program_id(1)
        row = lax.broadcasted_iota(jnp.int32, (blk, blk), 0)
        col = lax.broadcasted_iota(jnp.int32, (blk, blk), 1)
        for hh in range(2):
            q = q_ref[:, hh * dk:(hh + 1) * dk]

            def tile(j, carry, masked, hh=hh, q=q):
                m, l, acc = carry
                r0 = pl.multiple_of(j * blk, blk)
                ks = k_ref[pl.ds(r0, blk), hh * dk:(hh + 1) * dk]
                vs = v_ref[pl.ds(r0, blk), hh * HEAD_DIM:(hh + 1) * HEAD_DIM]
                s = lax.dot_general(q, ks, NT, preferred_element_type=F32) * scale
                if bias:
                    s = s + cc_ref[hh] - cr_ref[(2 * p + hh) * nb + j]
                if masked:
                    s = jnp.where(col <= row, s, NEG_INF)
                mn = jnp.maximum(m, jnp.max(s, axis=1, keepdims=True))
                a = jnp.exp(m - mn)
                pe = jnp.exp(s - mn)
                l = a * l + jnp.sum(pe, axis=1, keepdims=True)
                acc = a * acc + jnp.dot(pe.astype(MXU_DTYPE), vs, preferred_element_type=F32)
                return mn, l, acc

            init = (jnp.full((blk, 1), NEG_INF, F32), jnp.zeros((blk, 1), F32), jnp.zeros((blk, HEAD_DIM), F32))
            carry = lax.fori_loop(0, i, functools.partial(tile, masked=False), init)
            m, l, acc = tile(i, carry, True)
            o_ref[:, hh * HEAD_DIM:(hh + 1) * HEAD_DIM] = acc / l
            lse_ref[hh] = m + jnp.log(l)

    in_specs = [pl.BlockSpec((blk, W), lambda p, i: (i, qcb + p)),
                pl.BlockSpec((S, W), lambda p, i: (0, kcb + p)),
                pl.BlockSpec((S, 128), lambda p, i: (0, vcb + p))]
    args = [q_arr, k_arr, v_arr]
    if bias:
        in_specs += [pl.BlockSpec((2, blk, 1), lambda p, i: (p, i, 0)),
                     pl.BlockSpec((4 * nb, 1, blk), lambda p, i: (0, 0, 0))]
        args += [cum_col, cum_row]
    return pl.pallas_call(
        body, name=name, grid=(2, nb), in_specs=in_specs,
        out_specs=[pl.BlockSpec((blk, 128), lambda p, i: (i, p)), pl.BlockSpec((2, blk, 1), lambda p, i: (p, i, 0))],
        out_shape=[jax.ShapeDtypeStruct((S, GROUP_WIDTH), F32), jax.ShapeDtypeStruct((4, S, 1), F32)],
        compiler_params=_cparams("arbitrary", "arbitrary"),
    )(*args)


def _softmax_attn_bwd(q_arr, k_arr, v_arr, dmix, o_arr, lse, *, qcb, kcb, vcb, dcb, dk, scale,
                      cum_col=None, cum_row=None, blk, name):
    S = q_arr.shape[0]
    nb = S // blk
    bias = cum_col is not None
    W = 2 * dk

    def body(*refs):
        if bias:
            q_ref, k_ref, v_ref, do_ref, o_ref, lse_ref, cc_ref, cr_ref, dq_ref, dk_ref, dv_ref, dc_ref, dcq_ref = refs
        else:
            q_ref, k_ref, v_ref, do_ref, o_ref, lse_ref, dq_ref, dk_ref, dv_ref = refs
        p = pl.program_id(0)
        i = pl.program_id(1)

        @pl.when(i == 0)
        def _():
            dk_ref[...] = jnp.zeros_like(dk_ref)
            dv_ref[...] = jnp.zeros_like(dv_ref)
            if bias:
                dc_ref[...] = jnp.zeros_like(dc_ref)

        row = lax.broadcasted_iota(jnp.int32, (blk, blk), 0)
        col = lax.broadcasted_iota(jnp.int32, (blk, blk), 1)
        for hh in range(2):
            q = q_ref[:, hh * dk:(hh + 1) * dk]
            do = do_ref[:, hh * HEAD_DIM:(hh + 1) * HEAD_DIM]
            delta = jnp.sum(do * o_ref[:, hh * HEAD_DIM:(hh + 1) * HEAD_DIM], axis=1, keepdims=True)
            dob = do.astype(MXU_DTYPE)
            lse_h = lse_ref[hh]

            def tile(j, carry, masked, hh=hh, q=q, dob=dob, delta=delta, lse_h=lse_h):
                dq, dcq = carry
                r0 = pl.multiple_of(j * blk, blk)
                ks = k_ref[pl.ds(r0, blk), hh * dk:(hh + 1) * dk]
                vs = v_ref[pl.ds(r0, blk), hh * HEAD_DIM:(hh + 1) * HEAD_DIM]
                s = lax.dot_general(q, ks, NT, preferred_element_type=F32) * scale
                if bias:
                    s = s + cc_ref[hh] - cr_ref[(2 * p + hh) * nb + j]
                if masked:
                    s = jnp.where(col <= row, s, NEG_INF)
                pr = jnp.exp(s - lse_h)
                dp = lax.dot_general(dob, vs, NT, preferred_element_type=F32)
                ds = pr * (dp - delta)
                dsb = ds.astype(MXU_DTYPE)
                dv_ref[pl.ds(r0, blk), hh * HEAD_DIM:(hh + 1) * HEAD_DIM] += lax.dot_general(
                    pr.astype(MXU_DTYPE), dob, TN, preferred_element_type=F32)
                dk_ref[pl.ds(r0, blk), hh * dk:(hh + 1) * dk] += lax.dot_general(
                    dsb, q, TN, preferred_element_type=F32) * scale
                if bias:
                    dc_ref[hh * nb + j] -= jnp.sum(ds, axis=0, keepdims=True)
                    dcq = dcq + jnp.sum(ds, axis=1, keepdims=True)
                return dq + jnp.dot(dsb, ks, preferred_element_type=F32) * scale, dcq

            carry = lax.fori_loop(0, i, functools.partial(tile, masked=False),
                                  (jnp.zeros((blk, dk), F32), jnp.zeros((blk, 1), F32)))
            dq, dcq = tile(i, carry, True)
            dq_ref[:, hh * dk:(hh + 1) * dk] = dq
            if bias:
                dcq_ref[hh] = dcq

    in_specs = [pl.BlockSpec((blk, W), lambda p, i: (i, qcb + p)),
                pl.BlockSpec((S, W), lambda p, i: (0, kcb + p)),
                pl.BlockSpec((S, 128), lambda p, i: (0, vcb + p)),
                pl.BlockSpec((blk, 128), lambda p, i: (i, dcb + p)),
                pl.BlockSpec((blk, 128), lambda p, i: (i, p)),
                pl.BlockSpec((2, blk, 1), lambda p, i: (p, i, 0))]
    args = [q_arr, k_arr, v_arr, dmix, o_arr, lse]
    out_specs = [pl.BlockSpec((blk, W), lambda p, i: (i, p)),
                 pl.BlockSpec((S, W), lambda p, i: (0, p)),
                 pl.BlockSpec((S, 128), lambda p, i: (0, p))]
    out_shape = [jax.ShapeDtypeStruct((S, 4 * dk), F32), jax.ShapeDtypeStruct((S, 4 * dk), F32),
                 jax.ShapeDtypeStruct((S, GROUP_WIDTH), F32)]
    if bias:
        in_specs += [pl.BlockSpec((2, blk, 1), lambda p, i: (p, i, 0)),
                     pl.BlockSpec((4 * nb, 1, blk), lambda p, i: (0, 0, 0))]
        args += [cum_col, cum_row]
        out_specs += [pl.BlockSpec((2 * nb, 1, blk), lambda p, i: (p, 0, 0)), pl.BlockSpec((2, blk, 1), lambda p, i: (p, i, 0))]
        out_shape += [jax.ShapeDtypeStruct((4 * nb, 1, blk), F32), jax.ShapeDtypeStruct((4, S, 1), F32)]
    return pl.pallas_call(
        body, name=name, grid=(2, nb), in_specs=in_specs, out_specs=out_specs, out_shape=out_shape,
        compiler_params=_cparams("arbitrary", "arbitrary"),
    )(*args)


def _sb_tile(q, ks, scale, strict_mask, carry_l, tri_excl):
    z = lax.dot_general(q, ks, NT, preferred_element_type=F32) * scale
    lb = -(jnp.maximum(z, 0.0) + jnp.log(1.0 + jnp.exp(-jnp.abs(z))))
    if strict_mask is not None:
        lb = jnp.where(strict_mask, lb, 0.0)
    between = _dot01(lb, tri_excl) + carry_l
    a = jnp.exp(z + lb + between)
    if strict_mask is not None:
        a = jnp.where(strict_mask, a, 0.0)
    return z, lb, a


def _sb_attn_fwd(h_att, *, blk, name):
    S = h_att.shape[0]
    nb = S // blk
    scale = HEAD_DIM ** -0.5
    qcb, kcb, vcb = COL_SQ // 128, COL_SK // 128, COL_SV // 128

    def body(q_ref, k_ref, v_ref, o_ref, lt_ref):
        i = pl.program_id(1)
        row = lax.broadcasted_iota(jnp.int32, (blk, blk), 0)
        col = lax.broadcasted_iota(jnp.int32, (blk, blk), 1)
        strict = col < row
        tri_excl = (row > col).astype(MXU_DTYPE)
        for hh in range(2):
            sl = slice(hh * HEAD_DIM, (hh + 1) * HEAD_DIM)
            q = q_ref[:, sl]

            def tile(j, carry, mask, sl=sl, q=q):
                cl, acc = carry
                r0 = pl.multiple_of(j * blk, blk)
                _, lb, a = _sb_tile(q, k_ref[pl.ds(r0, blk), sl], scale, mask, cl, tri_excl)
                acc = acc + jnp.dot(a.astype(MXU_DTYPE), v_ref[pl.ds(r0, blk), sl], preferred_element_type=F32)
                return cl + jnp.sum(lb, axis=1, keepdims=True), acc

            carry = tile(i, (jnp.zeros((blk, 1), F32), jnp.zeros((blk, HEAD_DIM), F32)), strict)
            cl, acc = lax.fori_loop(0, i, lambda jj, c: tile(i - 1 - jj, c, None), carry)
            o_ref[:, sl] = acc
            lt_ref[hh] = cl

    return pl.pallas_call(
        body, name=name, grid=(2, nb),
        in_specs=[pl.BlockSpec((blk, 128), lambda p, i: (i, qcb + p)),
                  pl.BlockSpec((S, 128), lambda p, i: (0, kcb + p)),
                  pl.BlockSpec((S, 128), lambda p, i: (0, vcb + p))],
        out_specs=[pl.BlockSpec((blk, 128), lambda p, i: (i, p)), pl.BlockSpec((2, blk, 1), lambda p, i: (p, i, 0))],
        out_shape=[jax.ShapeDtypeStruct((S, GROUP_WIDTH), F32), jax.ShapeDtypeStruct((4, S, 1), F32)],
        compiler_params=_cparams("arbitrary", "arbitrary"),
    )(h_att, h_att, h_att)


def _sb_attn_bwd(h_att, dmix, ltot_arr, *, dcb, blk, name):
    S = h_att.shape[0]
    nb = S // blk
    scale = HEAD_DIM ** -0.5
    qcb, kcb, vcb = COL_SQ // 128, COL_SK // 128, COL_SV // 128

    def body(q_ref, k_ref, v_ref, do_ref, lt_ref, dq_ref, dk_ref, dv_ref):
        i = pl.program_id(1)

        @pl.when(i == 0)
        def _():
            dk_ref[...] = jnp.zeros_like(dk_ref)
            dv_ref[...] = jnp.zeros_like(dv_ref)

        row = lax.broadcasted_iota(jnp.int32, (blk, blk), 0)
        col = lax.broadcasted_iota(jnp.int32, (blk, blk), 1)
        strict = col < row
        up_incl = (row <= col).astype(MXU_DTYPE)
        up_excl = (row < col).astype(MXU_DTYPE)
        for hh in range(2):
            sl = slice(hh * HEAD_DIM, (hh + 1) * HEAD_DIM)
            q = q_ref[:, sl]
            dob = do_ref[:, sl].astype(MXU_DTYPE)
            ltot = lt_ref[hh]

            def tile(j, carry, mask, sl=sl, q=q, dob=dob, ltot=ltot):
                cl, cg, dq = carry
                r0 = pl.multiple_of(j * blk, blk)
                ks = k_ref[pl.ds(r0, blk), sl]
                vs = v_ref[pl.ds(r0, blk), sl]
                z = lax.dot_general(q, ks, NT, preferred_element_type=F32) * scale
                lb = -(jnp.maximum(z, 0.0) + jnp.log(1.0 + jnp.exp(-jnp.abs(z))))
                if mask is not None:
                    lb = jnp.where(mask, lb, 0.0)
                between = ltot - cl - _dot01(lb, up_incl)
                a = jnp.exp(z + lb + between)
                if mask is not None:
                    a = jnp.where(mask, a, 0.0)
                g = lax.dot_general(dob, vs, NT, preferred_element_type=F32) * a
                e = cg + _dot01(g, up_excl)
                dz = g * jnp.exp(lb) - e * jnp.exp(z + lb)
                if mask is not None:
                    dz = jnp.where(mask, dz, 0.0)
                dzb = dz.astype(MXU_DTYPE)
                dv_ref[pl.ds(r0, blk), sl] += lax.dot_general(a.astype(MXU_DTYPE), dob, TN, preferred_element_type=F32)
                dk_ref[pl.ds(r0, blk), sl] += lax.dot_general(dzb, q, TN, preferred_element_type=F32) * scale
                dq = dq + jnp.dot(dzb, ks, preferred_element_type=F32) * scale
                return cl + jnp.sum(lb, axis=1, keepdims=True), cg + jnp.sum(g, axis=1, keepdims=True), dq

            zc = jnp.zeros((blk, 1), F32)
            carry = lax.fori_loop(0, i, lambda j, c: tile(j, c, None), (zc, zc, jnp.zeros((blk, HEAD_DIM), F32)))
            _, _, dq = tile(i, carry, strict)
            dq_ref[:, sl] = dq

    return pl.pallas_call(
        body, name=name, grid=(2, nb),
        in_specs=[pl.BlockSpec((blk, 128), lambda p, i: (i, qcb + p)),
                  pl.BlockSpec((S, 128), lambda p, i: (0, kcb + p)),
                  pl.BlockSpec((S, 128), lambda p, i: (0, vcb + p)),
                  pl.BlockSpec((blk, 128), lambda p, i: (i, dcb + p)),
                  pl.BlockSpec((2, blk, 1), lambda p, i: (p, i, 0))],
        out_specs=[pl.BlockSpec((blk, 128), lambda p, i: (i, p)),
                   pl.BlockSpec((S, 128), lambda p, i: (0, p)),
                   pl.BlockSpec((S, 128), lambda p, i: (0, p))],
        out_shape=[jax.ShapeDtypeStruct((S, GROUP_WIDTH), F32)] * 3,
        compiler_params=_cparams("arbitrary", "arbitrary"),
    )(h_att, h_att, h_att, dmix, ltot_arr)


HP = 4


def _kv_blocks_t(a, blk):
    S, C = a.shape
    return a.reshape(S // blk, blk, C).transpose(0, 2, 1)


def _smax_fwd_t(qT, k, vT3, *, dk, blk, name):
    S = k.shape[0]
    nb = S // blk
    H = k.shape[1] // dk

    def body(qT_ref, k_ref, vT_ref, oT_ref, lse_ref):
        i = pl.program_id(1)
        key = lax.broadcasted_iota(jnp.int32, (blk, blk), 0)
        qry = lax.broadcasted_iota(jnp.int32, (blk, blk), 1)
        qs = [qT_ref[h * dk:(h + 1) * dk, :] for h in range(HP)]

        def tile(j, carry, masked):
            r0 = pl.multiple_of(j * blk, blk)
            ss = [jnp.dot(k_ref[pl.ds(r0, blk), h * dk:(h + 1) * dk], qs[h], preferred_element_type=F32)
                  for h in range(HP)]
            stats, pes = [], []
            for h in range(HP):
                m, l, _ = carry[h]
                s = jnp.where(key <= qry, ss[h], NEG_INF) if masked else ss[h]
                mn = jnp.maximum(m, jnp.max(s, axis=0, keepdims=True))
                a = jnp.exp(m - mn)
                pe = jnp.exp(s - mn)
                stats.append((mn, a * l + jnp.sum(pe, axis=0, keepdims=True), a))
                pes.append(pe.astype(MXU_DTYPE))
            pvs = [jnp.dot(vT_ref[j, h * HEAD_DIM:(h + 1) * HEAD_DIM, :], pes[h], preferred_element_type=F32)
                   for h in range(HP)]
            return tuple((stats[h][0], stats[h][1], stats[h][2] * carry[h][2] + pvs[h]) for h in range(HP))

        init = tuple((jnp.full((1, blk), NEG_INF, F32), jnp.zeros((1, blk), F32), jnp.zeros((HEAD_DIM, blk), F32))
                     for _ in range(HP))
        carry = lax.fori_loop(0, i, functools.partial(tile, masked=False), init)
        carry = tile(i, carry, True)
        for h in range(HP):
            m, l, acc = carry[h]
            oT_ref[h * HEAD_DIM:(h + 1) * HEAD_DIM, :] = acc / l
            lse_ref[h, 0] = m + jnp.log(l)

    return pl.pallas_call(
        body, name=name, grid=(H // HP, nb),
        in_specs=[pl.BlockSpec((HP * dk, blk), lambda p, i: (p, i)),
                  pl.BlockSpec((S, HP * dk), lambda p, i: (0, p)),
                  pl.BlockSpec((nb, HP * HEAD_DIM, blk), lambda p, i: (0, p, 0))],
        out_specs=[pl.BlockSpec((HP * HEAD_DIM, blk), lambda p, i: (p, i)),
                   pl.BlockSpec((HP, 1, 1, blk), lambda p, i: (p, i, 0, 0))],
        out_shape=[jax.ShapeDtypeStruct((H * HEAD_DIM, S), F32), jax.ShapeDtypeStruct((H, nb, 1, blk), F32)],
        compiler_params=_cparams("arbitrary", "arbitrary"),
    )(qT, k, vT3)


def _smax_bwd_t(qT, q, k, kT3, v, dmix, dmixT, oT, lse, *, dk, dcb, qscale, blk, name):
    S = k.shape[0]
    nb = S // blk
    H = k.shape[1] // dk
    hd = HP * HEAD_DIM
    dcr = dcb * 128 // hd

    def body(qT_ref, q_ref, k_ref, kT_ref, v_ref, do_ref, doT_ref, oT_ref, lse_ref, dqT_ref, dk_ref, dv_ref):
        i = pl.program_id(1)

        @pl.when(i == 0)
        def _():
            dk_ref[...] = jnp.zeros_like(dk_ref)
            dv_ref[...] = jnp.zeros_like(dv_ref)

        key = lax.broadcasted_iota(jnp.int32, (blk, blk), 0)
        qry = lax.broadcasted_iota(jnp.int32, (blk, blk), 1)
        per_head = []
        for h in range(HP):
            hs = slice(h * HEAD_DIM, (h + 1) * HEAD_DIM)
            doT = doT_ref[hs, :]
            per_head.append(dict(
                qT=qT_ref[h * dk:(h + 1) * dk, :], q=q_ref[:, h * dk:(h + 1) * dk],
                doT=doT.astype(MXU_DTYPE), do=do_ref[:, hs].astype(MXU_DTYPE),
                delta=jnp.sum(doT * oT_ref[hs, :], axis=0, keepdims=True), lse=lse_ref[h, 0]))

        def tile(j, dqs, masked):
            r0 = pl.multiple_of(j * blk, blk)
            rows = pl.ds(r0, blk)
            ksl = [slice(h * dk, (h + 1) * dk) for h in range(HP)]
            hsl = [slice(h * HEAD_DIM, (h + 1) * HEAD_DIM) for h in range(HP)]
            ss = [jnp.dot(k_ref[rows, ksl[h]], per_head[h]["qT"], preferred_element_type=F32) for h in range(HP)]
            dps = [jnp.dot(v_ref[rows, hsl[h]], per_head[h]["doT"], preferred_element_type=F32) for h in range(HP)]
            prs, dss = [], []
            for h in range(HP):
                c = per_head[h]
                s = jnp.where(key <= qry, ss[h], NEG_INF) if masked else ss[h]
                pr = jnp.exp(s - c["lse"])
                dss.append((pr * (dps[h] - c["delta"])).astype(MXU_DTYPE))
                prs.append(pr.astype(MXU_DTYPE))
            for h in range(HP):
                dv_ref[rows, hsl[h]] += jnp.dot(prs[h], per_head[h]["do"], preferred_element_type=F32)
            for h in range(HP):
                dk_ref[rows, ksl[h]] += jnp.dot(dss[h], per_head[h]["q"], preferred_element_type=F32)
            return tuple(dqs[h] + jnp.dot(kT_ref[j, ksl[h], :], dss[h], preferred_element_type=F32) for h in range(HP))

        dqs = lax.fori_loop(0, i, functools.partial(tile, masked=False),
                            tuple(jnp.zeros((dk, blk), F32) for _ in range(HP)))
        dqs = tile(i, dqs, True)
        for h in range(HP):
            dqT_ref[h * dk:(h + 1) * dk, :] = dqs[h] * qscale

    return pl.pallas_call(
        body, name=name, grid=(H // HP, nb),
        in_specs=[pl.BlockSpec((HP * dk, blk), lambda p, i: (p, i)),
                  pl.BlockSpec((blk, HP * dk), lambda p, i: (i, p)),
                  pl.BlockSpec((S, HP * dk), lambda p, i: (0, p)),
                  pl.BlockSpec((nb, HP * dk, blk), lambda p, i: (0, p, 0)),
                  pl.BlockSpec((S, hd), lambda p, i: (0, p)),
                  pl.BlockSpec((blk, hd), lambda p, i: (i, dcr + p)),
                  pl.BlockSpec((hd, blk), lambda p, i: (dcr + p, i)),
                  pl.BlockSpec((hd, blk), lambda p, i: (p, i)),
                  pl.BlockSpec((HP, 1, 1, blk), lambda p, i: (p, i, 0, 0))],
        out_specs=[pl.BlockSpec((HP * dk, blk), lambda p, i: (p, i)),
                   pl.BlockSpec((S, HP * dk), lambda p, i: (0, p)),
                   pl.BlockSpec((S, hd), lambda p, i: (0, p))],
        out_shape=[jax.ShapeDtypeStruct((H * dk, S), F32), jax.ShapeDtypeStruct((S, H * dk), F32),
                   jax.ShapeDtypeStruct((S, H * HEAD_DIM), F32)],
        compiler_params=_cparams("arbitrary", "arbitrary"),
    )(qT, q, k, kT3, v, dmix, dmixT, oT, lse)


def _log1m_beta(z):
    return -(jnp.maximum(z, 0.0) + jnp.log(1.0 + jnp.exp(-jnp.abs(z))))


def _dot01_left(m01, x, parts=2):
    acc = None
    rem = x
    for _ in range(parts):
        part = rem.astype(MXU_DTYPE)
        rem = rem - part.astype(F32)
        t = jnp.dot(m01, part, preferred_element_type=F32)
        acc = t if acc is None else acc + t
    return acc


def _sb_fwd_t(qT, h_att, vT3, *, blk, name):
    S = h_att.shape[0]
    nb = S // blk
    kcb = COL_SK // (HP * HEAD_DIM)

    def body(qT_ref, k_ref, vT_ref, oT_ref, lt_ref):
        i = pl.program_id(1)
        key = lax.broadcasted_iota(jnp.int32, (blk, blk), 0)
        qry = lax.broadcasted_iota(jnp.int32, (blk, blk), 1)
        strict = key < qry
        later = (qry > key).astype(MXU_DTYPE)
        qs = [qT_ref[h * HEAD_DIM:(h + 1) * HEAD_DIM, :] for h in range(HP)]

        def tile(j, carry, mask):
            r0 = pl.multiple_of(j * blk, blk)
            hsl = [slice(h * HEAD_DIM, (h + 1) * HEAD_DIM) for h in range(HP)]
            zs = [jnp.dot(k_ref[pl.ds(r0, blk), hsl[h]], qs[h], preferred_element_type=F32) for h in range(HP)]
            lbs = []
            for h in range(HP):
                lb = _log1m_beta(zs[h])
                lbs.append(lb if mask is None else jnp.where(mask, lb, 0.0))
            sums = [_dot01_left(later, lbs[h]) for h in range(HP)]
            probs = []
            for h in range(HP):
                a = jnp.exp(zs[h] + lbs[h] + sums[h] + carry[h][0])
                probs.append((a if mask is None else jnp.where(mask, a, 0.0)).astype(MXU_DTYPE))
            pvs = [jnp.dot(vT_ref[j, hsl[h], :], probs[h], preferred_element_type=F32) for h in range(HP)]
            return tuple((carry[h][0] + jnp.sum(lbs[h], axis=0, keepdims=True), carry[h][1] + pvs[h]) for h in range(HP))

        init = tuple((jnp.zeros((1, blk), F32), jnp.zeros((HEAD_DIM, blk), F32)) for _ in range(HP))
        carry = tile(i, init, strict)
        carry = lax.fori_loop(0, i, lambda jj, c: tile(i - 1 - jj, c, None), carry)
        for h in range(HP):
            oT_ref[h * HEAD_DIM:(h + 1) * HEAD_DIM, :] = carry[h][1]
            lt_ref[h, 0] = carry[h][0]

    hd = HP * HEAD_DIM
    return pl.pallas_call(
        body, name=name, grid=(4 // HP, nb),
        in_specs=[pl.BlockSpec((hd, blk), lambda p, i: (p, i)),
                  pl.BlockSpec((S, hd), lambda p, i: (0, kcb + p)),
                  pl.BlockSpec((nb, hd, blk), lambda p, i: (0, p, 0))],
        out_specs=[pl.BlockSpec((hd, blk), lambda p, i: (p, i)), pl.BlockSpec((HP, 1, 1, blk), lambda p, i: (p, i, 0, 0))],
        out_shape=[jax.ShapeDtypeStruct((GROUP_WIDTH, S), F32), jax.ShapeDtypeStruct((4, nb, 1, blk), F32)],
        compiler_params=_cparams("arbitrary", "arbitrary"),
    )(qT, h_att, vT3)


def _sb_bwd_t(qT, h_att, kT3, dmix, dmixT, ltot, *, dcb, qscale, blk, name):
    S = h_att.shape[0]
    nb = S // blk
    hd = HP * HEAD_DIM
    qcb, kcb, vcb = COL_SQ // hd, COL_SK // hd, COL_SV // hd
    dcr = dcb * 128 // hd

    def body(qT_ref, q_ref, k_ref, kT_ref, v_ref, do_ref, doT_ref, lt_ref, dqT_ref, dk_ref, dv_ref):
        i = pl.program_id(1)

        @pl.when(i == 0)
        def _():
            dk_ref[...] = jnp.zeros_like(dk_ref)
            dv_ref[...] = jnp.zeros_like(dv_ref)

        key = lax.broadcasted_iota(jnp.int32, (blk, blk), 0)
        qry = lax.broadcasted_iota(jnp.int32, (blk, blk), 1)
        strict = key < qry
        upto = (qry <= key).astype(MXU_DTYPE)
        before = (qry < key).astype(MXU_DTYPE)
        per_head = []
        for h in range(HP):
            hs = slice(h * HEAD_DIM, (h + 1) * HEAD_DIM)
            per_head.append(dict(qT=qT_ref[hs, :], q=q_ref[:, hs], doT=doT_ref[hs, :].astype(MXU_DTYPE),
                                 do=do_ref[:, hs].astype(MXU_DTYPE), lt=lt_ref[h, 0]))

        def tile(j, carry, mask):
            r0 = pl.multiple_of(j * blk, blk)
            rows = pl.ds(r0, blk)
            hsl = [slice(h * HEAD_DIM, (h + 1) * HEAD_DIM) for h in range(HP)]
            zs = [jnp.dot(k_ref[rows, hsl[h]], per_head[h]["qT"], preferred_element_type=F32) for h in range(HP)]
            das = [jnp.dot(v_ref[rows, hsl[h]], per_head[h]["doT"], preferred_element_type=F32) for h in range(HP)]
            lbs = []
            for h in range(HP):
                lb = _log1m_beta(zs[h])
                lbs.append(lb if mask is None else jnp.where(mask, lb, 0.0))
            sums = [_dot01_left(upto, lbs[h]) for h in range(HP)]
            probs, gs = [], []
            for h in range(HP):
                a = jnp.exp(zs[h] + lbs[h] + (per_head[h]["lt"] - carry[h][0] - sums[h]))
                a = a if mask is None else jnp.where(mask, a, 0.0)
                gs.append(das[h] * a)
                probs.append(a.astype(MXU_DTYPE))
            for h in range(HP):
                dv_ref[rows, hsl[h]] += jnp.dot(probs[h], per_head[h]["do"], preferred_element_type=F32)
            es = [_dot01_left(before, gs[h]) for h in range(HP)]
            dzs = []
            for h in range(HP):
                dz = gs[h] * jnp.exp(lbs[h]) - (carry[h][1] + es[h]) * jnp.exp(zs[h] + lbs[h])
                dzs.append((dz if mask is None else jnp.where(mask, dz, 0.0)).astype(MXU_DTYPE))
            for h in range(HP):
                dk_ref[rows, hsl[h]] += jnp.dot(dzs[h], per_head[h]["q"], preferred_element_type=F32)
            return tuple((carry[h][0] + jnp.sum(lbs[h], axis=0, keepdims=True),
                          carry[h][1] + jnp.sum(gs[h], axis=0, keepdims=True),
                          carry[h][2] + jnp.dot(kT_ref[j, hsl[h], :], dzs[h], preferred_element_type=F32))
                         for h in range(HP))

        zr = jnp.zeros((1, blk), F32)
        init = tuple((zr, zr, jnp.zeros((HEAD_DIM, blk), F32)) for _ in range(HP))
        carry = lax.fori_loop(0, i, lambda j, c: tile(j, c, None), init)
        carry = tile(i, carry, strict)
        for h in range(HP):
            dqT_ref[h * HEAD_DIM:(h + 1) * HEAD_DIM, :] = carry[h][2] * qscale

    return pl.pallas_call(
        body, name=name, grid=(4 // HP, nb),
        in_specs=[pl.BlockSpec((hd, blk), lambda p, i: (p, i)),
                  pl.BlockSpec((blk, hd), lambda p, i: (i, qcb + p)),
                  pl.BlockSpec((S, hd), lambda p, i: (0, kcb + p)),
                  pl.BlockSpec((nb, hd, blk), lambda p, i: (0, p, 0)),
                  pl.BlockSpec((S, hd), lambda p, i: (0, vcb + p)),
                  pl.BlockSpec((blk, hd), lambda p, i: (i, dcr + p)),
                  pl.BlockSpec((hd, blk), lambda p, i: (dcr + p, i)),
                  pl.BlockSpec((HP, 1, 1, blk), lambda p, i: (p, i, 0, 0))],
        out_specs=[pl.BlockSpec((hd, blk), lambda p, i: (p, i)),
                   pl.BlockSpec((S, hd), lambda p, i: (0, p)),
                   pl.BlockSpec((S, hd), lambda p, i: (0, p))],
        out_shape=[jax.ShapeDtypeStruct((GROUP_WIDTH, S), F32), jax.ShapeDtypeStruct((S, GROUP_WIDTH), F32),
                   jax.ShapeDtypeStruct((S, GROUP_WIDTH), F32)],
        compiler_params=_cparams("arbitrary", "arbitrary"),
    )(qT, h_att, h_att, kT3, h_att, dmix, dmixT, ltot)


def _swa_scores(q_ref, k_ref, n, h, start):
    g = h // 2
    kb = k_ref[pl.ds(start, 2 * WINDOW), g * HEAD_DIM:(g + 1) * HEAD_DIM]
    s = lax.dot_general(q_ref[:, h * HEAD_DIM:(h + 1) * HEAD_DIM], kb, NT, preferred_element_type=F32) * (HEAD_DIM ** -0.5)
    dist = (n * WINDOW + lax.broadcasted_iota(jnp.int32, (WINDOW, 2 * WINDOW), 0)
            - start - lax.broadcasted_iota(jnp.int32, (WINDOW, 2 * WINDOW), 1))
    s = s - SWA_SLOPES[h] * dist.astype(F32)
    valid = (dist >= 0) & (dist < WINDOW)
    return jnp.where(valid, s, NEG_INF), kb


def _swa_fwd(h_att, sinks, *, name):
    S = h_att.shape[0]
    nb = S // WINDOW
    qcb, kcb, vcb = COL_WQ // 256, COL_WK // 128, COL_WV // 128

    def body(sink_ref, q_ref, k_ref, v_ref, o_ref, lse_ref):
        n = pl.program_id(0)
        start = pl.multiple_of(jnp.maximum(n - 1, 0) * WINDOW, WINDOW)
        for h in range(4):
            g = h // 2
            s, _ = _swa_scores(q_ref, k_ref, n, h, start)
            sink = sink_ref[h]
            m = jnp.maximum(jnp.max(s, axis=1, keepdims=True), sink)
            e = jnp.exp(s - m)
            den = jnp.sum(e, axis=1, keepdims=True) + jnp.exp(sink - m)
            vb = v_ref[pl.ds(start, 2 * WINDOW), g * HEAD_DIM:(g + 1) * HEAD_DIM]
            o_ref[:, h * HEAD_DIM:(h + 1) * HEAD_DIM] = jnp.dot((e / den).astype(MXU_DTYPE), vb, preferred_element_type=F32)
            lse_ref[h] = m + jnp.log(den)

    return pl.pallas_call(
        body, name=name, grid=(nb,),
        in_specs=[pl.BlockSpec(memory_space=pltpu.SMEM),
                  pl.BlockSpec((WINDOW, 256), lambda n: (n, qcb)),
                  pl.BlockSpec((S, 128), lambda n: (0, kcb)),
                  pl.BlockSpec((S, 128), lambda n: (0, vcb))],
        out_specs=[pl.BlockSpec((WINDOW, 256), lambda n: (n, 0)), pl.BlockSpec((4, WINDOW, 1), lambda n: (0, n, 0))],
        out_shape=[jax.ShapeDtypeStruct((S, GROUP_WIDTH), F32), jax.ShapeDtypeStruct((4, S, 1), F32)],
        compiler_params=_cparams("arbitrary"),
    )(sinks, h_att, h_att, h_att)


def _swa_bwd(h_att, sinks, dmix, o_arr, lse, *, dcb, name):
    S = h_att.shape[0]
    nb = S // WINDOW
    qcb, kcb, vcb = COL_WQ // 256, COL_WK // 128, COL_WV // 128

    def body(sink_ref, q_ref, k_ref, v_ref, do_ref, o_ref, lse_ref, dq_ref, dk_ref, dv_ref, dsink_ref):
        n = pl.program_id(0)

        @pl.when(n == 0)
        def _():
            dk_ref[...] = jnp.zeros_like(dk_ref)
            dv_ref[...] = jnp.zeros_like(dv_ref)
            dsink_ref[...] = jnp.zeros_like(dsink_ref)

        start = pl.multiple_of(jnp.maximum(n - 1, 0) * WINDOW, WINDOW)
        for h in range(4):
            g = h // 2
            sl = slice(h * HEAD_DIM, (h + 1) * HEAD_DIM)
            gl = slice(g * HEAD_DIM, (g + 1) * HEAD_DIM)
            s, kb = _swa_scores(q_ref, k_ref, n, h, start)
            lse_h = lse_ref[h]
            pr = jnp.exp(s - lse_h)
            do = do_ref[:, sl]
            dob = do.astype(MXU_DTYPE)
            delta = jnp.sum(do * o_ref[:, sl], axis=1, keepdims=True)
            vb = v_ref[pl.ds(start, 2 * WINDOW), gl]
            ds = pr * (lax.dot_general(dob, vb, NT, preferred_element_type=F32) - delta)
            dsb = ds.astype(MXU_DTYPE)
            dq_ref[:, sl] = jnp.dot(dsb, kb, preferred_element_type=F32) * (HEAD_DIM ** -0.5)
            dk_ref[pl.ds(start, 2 * WINDOW), gl] += lax.dot_general(
                dsb, q_ref[:, sl], TN, preferred_element_type=F32) * (HEAD_DIM ** -0.5)
            dv_ref[pl.ds(start, 2 * WINDOW), gl] += lax.dot_general(pr.astype(MXU_DTYPE), dob, TN, preferred_element_type=F32)
            dsink_ref[h:h + 1, :] += jnp.zeros((1, 128), F32) - jnp.sum(jnp.exp(sink_ref[h] - lse_h) * delta)

    return pl.pallas_call(
        body, name=name, grid=(nb,),
        in_specs=[pl.BlockSpec(memory_space=pltpu.SMEM),
                  pl.BlockSpec((WINDOW, 256), lambda n: (n, qcb)),
                  pl.BlockSpec((S, 128), lambda n: (0, kcb)),
                  pl.BlockSpec((S, 128), lambda n: (0, vcb)),
                  pl.BlockSpec((WINDOW, 256), lambda n: (n, dcb)),
                  pl.BlockSpec((WINDOW, 256), lambda n: (n, 0)),
                  pl.BlockSpec((4, WINDOW, 1), lambda n: (0, n, 0))],
        out_specs=[pl.BlockSpec((WINDOW, 256), lambda n: (n, 0)),
                   pl.BlockSpec((S, 128), lambda n: (0, 0)),
                   pl.BlockSpec((S, 128), lambda n: (0, 0)),
                   pl.BlockSpec((4, 128), lambda n: (0, 0))],
        out_shape=[jax.ShapeDtypeStruct((S, GROUP_WIDTH), F32), jax.ShapeDtypeStruct((S, 128), F32),
                   jax.ShapeDtypeStruct((S, 128), F32), jax.ShapeDtypeStruct((4, 128), F32)],
        compiler_params=_cparams("arbitrary"),
    )(sinks, h_att, h_att, h_att, dmix, o_arr, lse)


def _tri(n, incl, upper):
    r = lax.broadcasted_iota(jnp.int32, (n, n), 0)
    c = lax.broadcasted_iota(jnp.int32, (n, n), 1)
    if upper:
        m = (r <= c) if incl else (r < c)
    else:
        m = (r >= c) if incl else (r > c)
    return m.astype(MXU_DTYPE)


def _fox_gate_fwd(fg, b_f, *, name):
    _, R, _ = fg.shape

    def body(b_ref, fg_ref, pos_ref, neg_ref):
        up_incl = _tri(128, True, True)
        ones = jnp.ones((128, 128), MXU_DTYPE)
        for h in range(4):
            z = fg_ref[h] + b_ref[h]
            logf = jnp.minimum(z, 0.0) - jnp.log(1.0 + jnp.exp(-jnp.abs(z)))
            within = _dot01(logf, up_incl, parts=3)
            totals = _dot01(logf, ones, parts=3)
            rem = within + _rows_other(totals, R, after=False)
            for part in range(3):
                piece = rem.astype(MXU_DTYPE)
                rem = rem - piece.astype(F32)
                pos_ref[h, part] = piece
                neg_ref[h, part] = -piece

    shape = (4, 3) + fg.shape[1:]
    return pl.pallas_call(
        body, name=name,
        in_specs=[pl.BlockSpec(memory_space=pltpu.SMEM), pl.BlockSpec(memory_space=pltpu.VMEM)],
        out_specs=[pl.BlockSpec(memory_space=pltpu.VMEM)] * 2,
        out_shape=[jax.ShapeDtypeStruct(shape, MXU_DTYPE)] * 2,
    )(b_f, fg)


def _rows_other(totals, n, after):
    r = lax.broadcasted_iota(jnp.int32, (n, n), 0)
    c = lax.broadcasted_iota(jnp.int32, (n, n), 1)
    m = ((c > r) if after else (c < r)).astype(MXU_DTYPE)
    acc = None
    rem = totals
    for _ in range(3):
        part = rem.astype(MXU_DTYPE)
        rem = rem - part.astype(F32)
        t = jnp.dot(m, part, preferred_element_type=F32)
        acc = t if acc is None else acc + t
    return acc


def _fox_gate_bwd(fg, b_f, dcum_k, dcum_q, *, q_unscale, name):
    _, R, _ = fg.shape

    def body(b_ref, fg_ref, dck_ref, dcq_ref, dfg_ref, db_ref):
        low_incl = _tri(128, True, False)
        ones = jnp.ones((128, 128), MXU_DTYPE)
        for h in range(4):
            dc = dcq_ref[h] * q_unscale - dck_ref[h]
            dlogf = _dot01(dc, low_incl, parts=3) + _rows_other(_dot01(dc, ones, parts=3), R, after=True)
            z = fg_ref[h] + b_ref[h]
            dz = dlogf * jnp.exp(jnp.minimum(-z, 0.0) - jnp.log(1.0 + jnp.exp(-jnp.abs(z))))
            dfg_ref[h] = dz
            db_ref[h:h + 1, :] = jnp.zeros((1, 128), F32) + jnp.sum(dz)

    return pl.pallas_call(
        body, name=name,
        in_specs=[pl.BlockSpec(memory_space=pltpu.SMEM)] + [pl.BlockSpec(memory_space=pltpu.VMEM)] * 3,
        out_specs=[pl.BlockSpec(memory_space=pltpu.VMEM), pl.BlockSpec(memory_space=pltpu.VMEM)],
        out_shape=[jax.ShapeDtypeStruct(fg.shape, F32), jax.ShapeDtypeStruct((4, 128), F32)],
    )(b_f, fg, dcum_k, dcum_q)


def _rope_rot(transpose):
    r = lax.broadcasted_iota(jnp.int32, (MLA_PAD, MLA_PAD), 0)
    c = lax.broadcasted_iota(jnp.int32, (MLA_PAD, MLA_PAD), 1)
    if transpose:
        r, c = c, r
    half = MLA_ROPE // 2
    lo, mid, hi = HEAD_DIM, HEAD_DIM + half, HEAD_DIM + MLA_ROPE
    minus = (c >= lo) & (c < mid) & (r == c + half)
    plus = (c >= mid) & (c < hi) & (r == c - half)
    return jnp.where(plus, 1.0, jnp.where(minus, -1.0, 0.0)).astype(MXU_DTYPE)


def _rope_lanes():
    lane = lax.broadcasted_iota(jnp.int32, (1, MLA_PAD), 1)
    return ((lane >= HEAD_DIM) & (lane < HEAD_DIM + MLA_ROPE)).astype(F32)


def _rms(x, g, eps=1e-6):
    r = lax.rsqrt(jnp.mean(x * x, axis=-1, keepdims=True) + eps)
    return x * r * g, r


def _rms_bwd(dy, x, r, g):
    xh = x * r
    dxh = dy * g
    dx = r * (dxh - xh * jnp.mean(dxh * xh, axis=-1, keepdims=True))
    return dx, dy * xh


def _mla_prep_fwd(lat, g_q, g_kv, wuq, wuk, wuv, cosm, sinm, *, bs, name):
    S = lat.shape[0]

    def body(lat_ref, gq_ref, gkv_ref, wuq_ref, wuk_ref, wuv_ref, cos_ref, sin_ref, q_ref, k_ref, v_ref):
        rot = _rope_rot(False)
        cosm_, sinm_ = cos_ref[...], sin_ref[...]
        nq, _ = _rms(lat_ref[:, 0:MLA_Q_RANK], gq_ref[...])
        nkv, _ = _rms(lat_ref[:, MLA_Q_RANK:MLA_Q_RANK + MLA_KV_RANK], gkv_ref[...])
        qlat = jnp.dot(nq.astype(MXU_DTYPE), wuq_ref[...], preferred_element_type=F32)
        klat = jnp.dot(nkv.astype(MXU_DTYPE), wuk_ref[...], preferred_element_type=F32)
        v_ref[...] = jnp.dot(nkv.astype(MXU_DTYPE), wuv_ref[...], preferred_element_type=F32).astype(v_ref.dtype)
        krb = lat_ref[:, 384:512]
        kr = krb * (cosm_ * _rope_lanes()) + _dot01(krb, rot, parts=3) * sinm_
        for h in range(4):
            sl = slice(h * MLA_PAD, (h + 1) * MLA_PAD)
            qh = qlat[:, sl]
            q_ref[:, sl] = ((qh * cosm_ + _dot01(qh, rot, parts=3) * sinm_) * (MLA_QK ** -0.5)).astype(q_ref.dtype)
            k_ref[:, sl] = (klat[:, sl] + kr).astype(k_ref.dtype)

    full = lambda a: pl.BlockSpec(a.shape, lambda i: (0,) * a.ndim)
    return pl.pallas_call(
        body, name=name, grid=(S // bs,),
        in_specs=[pl.BlockSpec((bs, LAT_W), lambda i: (i, 0)), full(g_q), full(g_kv), full(wuq), full(wuk), full(wuv),
                  pl.BlockSpec((bs, MLA_PAD), lambda i: (i, 0)), pl.BlockSpec((bs, MLA_PAD), lambda i: (i, 0))],
        out_specs=[pl.BlockSpec((bs, 512), lambda i: (i, 0)), pl.BlockSpec((bs, 512), lambda i: (i, 0)),
                   pl.BlockSpec((bs, 256), lambda i: (i, 0))],
        out_shape=[jax.ShapeDtypeStruct((S, 512), MXU_DTYPE), jax.ShapeDtypeStruct((S, 512), MXU_DTYPE),
                   jax.ShapeDtypeStruct((S, 256), MXU_DTYPE)],
        compiler_params=_cparams("parallel"),
    )(lat, g_q, g_kv, wuq, wuk, wuv, cosm, sinm)


def _mla_prep_bwd(lat, g_q, g_kv, wuq, wuk, wuv, cosm, sinm, dq, dk, dv, *, bs, name):
    S = lat.shape[0]

    def body(lat_ref, gq_ref, gkv_ref, wuq_ref, wuk_ref, wuv_ref, cos_ref, sin_ref, dq_ref, dk_ref, dv_ref,
             dlat_ref, dwuq_ref, dwuk_ref, dwuv_ref, dgq_ref, dgkv_ref):
        @pl.when(pl.program_id(0) == 0)
        def _():
            for r in (dwuq_ref, dwuk_ref, dwuv_ref, dgq_ref, dgkv_ref):
                r[...] = jnp.zeros_like(r)

        rot_t = _rope_rot(True)
        cosm_, sinm_ = cos_ref[...], sin_ref[...]
        cq = lat_ref[:, 0:MLA_Q_RANK]
        ckv = lat_ref[:, MLA_Q_RANK:MLA_Q_RANK + MLA_KV_RANK]
        nq, rq = _rms(cq, gq_ref[...])
        nkv, rkv = _rms(ckv, gkv_ref[...])
        nqb, nkvb = nq.astype(MXU_DTYPE), nkv.astype(MXU_DTYPE)

        dqlat = []
        dkr = jnp.zeros((bs, MLA_PAD), F32)
        for h in range(4):
            sl = slice(h * MLA_PAD, (h + 1) * MLA_PAD)
            dqh = dq_ref[:, sl]
            dqlat.append(dqh * cosm_ + _dot01(dqh * sinm_, rot_t, parts=3))
            dkr = dkr + dk_ref[:, sl]
        dqlat = jnp.concatenate(dqlat, axis=1).astype(MXU_DTYPE)
        dkb = dk_ref[...].astype(MXU_DTYPE)
        dvb = dv_ref[...].astype(MXU_DTYPE)

        dnq = lax.dot_general(dqlat, wuq_ref[...], NT, preferred_element_type=F32)
        dnkv = (lax.dot_general(dkb, wuk_ref[...], NT, preferred_element_type=F32)
                + lax.dot_general(dvb, wuv_ref[...], NT, preferred_element_type=F32))
        dwuq_ref[...] += lax.dot_general(nqb, dqlat, TN, preferred_element_type=F32)
        dwuk_ref[...] += lax.dot_general(nkvb, dkb, TN, preferred_element_type=F32)
        dwuv_ref[...] += lax.dot_general(nkvb, dvb, TN, preferred_element_type=F32)
        dcq, tq = _rms_bwd(dnq, cq, rq, gq_ref[...])
        dckv, tkv = _rms_bwd(dnkv, ckv, rkv, gkv_ref[...])
        dgq_ref[...] += jnp.sum(tq, axis=0, keepdims=True)
        dgkv_ref[...] += jnp.sum(tkv, axis=0, keepdims=True)
        dlat_ref[:, 0:MLA_Q_RANK] = dcq.astype(dlat_ref.dtype)
        dlat_ref[:, MLA_Q_RANK:MLA_Q_RANK + MLA_KV_RANK] = dckv.astype(dlat_ref.dtype)
        dkrb = dkr * (cosm_ * _rope_lanes()) + _dot01(dkr * sinm_, rot_t, parts=3)
        dlat_ref[:, 384:512] = dkrb.astype(dlat_ref.dtype)

    full = lambda a: pl.BlockSpec(a.shape, lambda i: (0,) * a.ndim)
    row = lambda w: pl.BlockSpec((bs, w), lambda i: (i, 0))
    acc = lambda *shape: pl.BlockSpec(shape, lambda i: (0,) * len(shape))
    return pl.pallas_call(
        body, name=name, grid=(S // bs,),
        in_specs=[row(LAT_W), full(g_q), full(g_kv), full(wuq), full(wuk), full(wuv), row(MLA_PAD), row(MLA_PAD),
                  row(512), row(512), row(256)],
        out_specs=[row(512), acc(256, 512), acc(128, 512), acc(128, 256), acc(1, 256), acc(1, 128)],
        out_shape=[jax.ShapeDtypeStruct((S, 512), MXU_DTYPE), jax.ShapeDtypeStruct((256, 512), F32),
                   jax.ShapeDtypeStruct((128, 512), F32), jax.ShapeDtypeStruct((128, 256), F32),
                   jax.ShapeDtypeStruct((1, 256), F32), jax.ShapeDtypeStruct((1, 128), F32)],
        compiler_params=_cparams("arbitrary"),
    )(lat, g_q, g_kv, wuq, wuk, wuv, cosm, sinm, dq, dk, dv)


def _row_spec(bs, w):
    return pl.BlockSpec((bs, w), lambda i: (i, 0))


def _vec_spec(w):
    return pl.BlockSpec((1, w), lambda i: (0, 0))


def _gnorm_fwd(outs, g, *, bs, name):
    S = outs[0].shape[0]

    def body(a_ref, b_ref, c_ref, d_ref, g_ref, o_ref):
        for k, ref in enumerate((a_ref, b_ref, c_ref, d_ref)):
            sl = slice(k * GROUP_WIDTH, (k + 1) * GROUP_WIDTH)
            y, _ = _rms(ref[...], g_ref[:, sl])
            o_ref[:, sl] = y.astype(o_ref.dtype)

    return pl.pallas_call(
        body, name=name, grid=(S // bs,),
        in_specs=[_row_spec(bs, GROUP_WIDTH)] * 4 + [_vec_spec(D_MODEL)],
        out_specs=_row_spec(bs, D_MODEL), out_shape=jax.ShapeDtypeStruct((S, D_MODEL), MXU_DTYPE),
        compiler_params=_cparams("parallel"),
    )(*outs, g)


def _gnorm_bwd(dgn, outs, g, *, bs, name):
    S = dgn.shape[0]

    def body(dgn_ref, a_ref, b_ref, c_ref, d_ref, g_ref, dmix_ref, dg_ref):
        @pl.when(pl.program_id(0) == 0)
        def _():
            dg_ref[...] = jnp.zeros_like(dg_ref)

        for k, ref in enumerate((a_ref, b_ref, c_ref, d_ref)):
            sl = slice(k * GROUP_WIDTH, (k + 1) * GROUP_WIDTH)
            x = ref[...]
            _, r = _rms(x, g_ref[:, sl])
            dx, t = _rms_bwd(dgn_ref[:, sl], x, r, g_ref[:, sl])
            dmix_ref[:, sl] = dx
            dg_ref[:, sl] += jnp.sum(t, axis=0, keepdims=True)

    return pl.pallas_call(
        body, name=name, grid=(S // bs,),
        in_specs=[_row_spec(bs, D_MODEL)] + [_row_spec(bs, GROUP_WIDTH)] * 4 + [_vec_spec(D_MODEL)],
        out_specs=[_row_spec(bs, D_MODEL), _vec_spec(D_MODEL)],
        out_shape=[jax.ShapeDtypeStruct((S, D_MODEL), F32), jax.ShapeDtypeStruct((1, D_MODEL), F32)],
        compiler_params=_cparams("arbitrary"),
    )(dgn, *outs, g)


def _ln_fwd(u, g, b, *, bs, name):
    S = u.shape[0]

    def body(u_ref, g_ref, b_ref, y_ref, yb_ref, xh_ref, rs_ref):
        x = u_ref[...]
        mu = jnp.mean(x, axis=-1, keepdims=True)
        xc = x - mu
        rs = lax.rsqrt(jnp.mean(xc * xc, axis=-1, keepdims=True) + 1e-5)
        xh = xc * rs
        y = xh * g_ref[...] + b_ref[...]
        y_ref[...] = y
        yb_ref[...] = y.astype(yb_ref.dtype)
        xh_ref[...] = xh
        rs_ref[...] = rs

    return pl.pallas_call(
        body, name=name, grid=(S // bs,),
        in_specs=[_row_spec(bs, D_MODEL), _vec_spec(D_MODEL), _vec_spec(D_MODEL)],
        out_specs=[_row_spec(bs, D_MODEL), _row_spec(bs, D_MODEL), _row_spec(bs, D_MODEL), _row_spec(bs, 1)],
        out_shape=[jax.ShapeDtypeStruct((S, D_MODEL), F32), jax.ShapeDtypeStruct((S, D_MODEL), MXU_DTYPE),
                   jax.ShapeDtypeStruct((S, D_MODEL), F32), jax.ShapeDtypeStruct((S, 1), F32)],
        compiler_params=_cparams("parallel"),
    )(u, g, b)


def _ln_bwd(dy, xh, rs, g, *, bs, name):
    S = dy.shape[0]

    def body(dy_ref, xh_ref, rs_ref, g_ref, du_ref, dub_ref, dg_ref, db_ref):
        @pl.when(pl.program_id(0) == 0)
        def _():
            dg_ref[...] = jnp.zeros_like(dg_ref)
            db_ref[...] = jnp.zeros_like(db_ref)

        dy_, xh_ = dy_ref[...], xh_ref[...]
        dxh = dy_ * g_ref[...]
        du = rs_ref[...] * (dxh - jnp.mean(dxh, axis=-1, keepdims=True)
                            - xh_ * jnp.mean(dxh * xh_, axis=-1, keepdims=True))
        du_ref[...] = du
        dub_ref[...] = du.astype(dub_ref.dtype)
        dg_ref[...] += jnp.sum(dy_ * xh_, axis=0, keepdims=True)
        db_ref[...] += jnp.sum(dy_, axis=0, keepdims=True)

    return pl.pallas_call(
        body, name=name, grid=(S // bs,),
        in_specs=[_row_spec(bs, D_MODEL), _row_spec(bs, D_MODEL), _row_spec(bs, 1), _vec_spec(D_MODEL)],
        out_specs=[_row_spec(bs, D_MODEL), _row_spec(bs, D_MODEL), _vec_spec(D_MODEL), _vec_spec(D_MODEL)],
        out_shape=[jax.ShapeDtypeStruct((S, D_MODEL), F32), jax.ShapeDtypeStruct((S, D_MODEL), MXU_DTYPE),
                   jax.ShapeDtypeStruct((1, D_MODEL), F32), jax.ShapeDtypeStruct((1, D_MODEL), F32)],
        compiler_params=_cparams("arbitrary"),
    )(dy, xh, rs, g)


def _swiglu_fwd(gu, *, bs, name):
    S = gu.shape[0]

    def body(gu_ref, a_ref):
        gt = gu_ref[:, :D_FF]
        a_ref[...] = (gt / (1.0 + jnp.exp(-gt)) * gu_ref[:, D_FF:]).astype(a_ref.dtype)

    return pl.pallas_call(
        body, name=name, grid=(S // bs,),
        in_specs=[_row_spec(bs, 2 * D_FF)],
        out_specs=_row_spec(bs, D_FF), out_shape=jax.ShapeDtypeStruct((S, D_FF), MXU_DTYPE),
        compiler_params=_cparams("parallel"),
    )(gu)


def _swiglu_bwd(da, gu, *, bs, name):
    S = gu.shape[0]

    def body(da_ref, gu_ref, dgu_ref):
        gt, da_ = gu_ref[:, :D_FF], da_ref[...]
        sg = 1.0 / (1.0 + jnp.exp(-gt))
        silu = gt * sg
        dgu_ref[:, :D_FF] = (da_ * gu_ref[:, D_FF:] * (sg + silu * (1.0 - sg))).astype(dgu_ref.dtype)
        dgu_ref[:, D_FF:] = (da_ * silu).astype(dgu_ref.dtype)

    return pl.pallas_call(
        body, name=name, grid=(S // bs,),
        in_specs=[_row_spec(bs, D_FF), _row_spec(bs, 2 * D_FF)],
        out_specs=_row_spec(bs, 2 * D_FF), out_shape=jax.ShapeDtypeStruct((S, 2 * D_FF), MXU_DTYPE),
        compiler_params=_cparams("parallel"),
    )(da, gu)


def _loss_head(y, target, *, bs, name):
    S = y.shape[0]

    def body(y_ref, t_ref, dy_ref, loss_ref):
        @pl.when(pl.program_id(0) == 0)
        def _():
            loss_ref[...] = jnp.zeros_like(loss_ref)

        e = y_ref[...] - t_ref[...]
        dy_ref[...] = e * (1.0 / D_MODEL)
        per_tok = jnp.mean(e * e, axis=-1, keepdims=True)
        loss_ref[...] += 0.5 * jnp.sum(per_tok, axis=0, keepdims=True)

    return pl.pallas_call(
        body, name=name, grid=(S // bs,),
        in_specs=[_row_spec(bs, D_MODEL), _row_spec(bs, D_MODEL)],
        out_specs=[_row_spec(bs, D_MODEL), pl.BlockSpec((1, 1), lambda i: (0, 0))],
        out_shape=[jax.ShapeDtypeStruct((S, D_MODEL), F32), jax.ShapeDtypeStruct((1, 1), F32)],
        compiler_params=_cparams("arbitrary"),
    )(y, target)


def _blk(n, target):
    if n <= target:
        return n
    best = None
    for b in range(128, target + 1, 128):
        if n % b == 0:
            best = b
    assert best is not None, n
    return best


def _rope_tables(S):
    pos = jnp.arange(S, dtype=F32)
    inv = ROPE_THETA ** (-jnp.arange(0, MLA_ROPE, 2, dtype=F32) / MLA_ROPE)
    ang = pos[:, None] * inv[None, :]
    cos, sin = jnp.cos(ang), jnp.sin(ang)
    one, zero, pad = jnp.ones((S, HEAD_DIM), F32), jnp.zeros((S, HEAD_DIM), F32), jnp.zeros((S, MLA_PAD - MLA_QK), F32)
    return jnp.concatenate([one, cos, cos, pad], axis=1), jnp.concatenate([zero, sin, sin, pad], axis=1)


def _prep_weights_a(w_in, w_uq, w_ukv):
    z = lambda n: jnp.zeros((D_MODEL, n), w_in.dtype)
    win_a = jnp.concatenate([w_in[:, 0:768], w_in[:, 1188:2468]], axis=1)
    win_l = jnp.concatenate([w_in[:, 772:1156], z(64), w_in[:, 1156:1188], z(32), w_in[:, 768:772], z(124)], axis=1)
    kv = w_ukv.reshape(MLA_KV_RANK, 4, 2 * HEAD_DIM)
    return dict(
        win_a=win_a, win_l=win_l, win_p=jnp.concatenate([win_a, win_l], axis=1),
        wuq=jnp.pad(w_uq.reshape(MLA_Q_RANK, 4, MLA_QK), ((0, 0), (0, 0), (0, MLA_PAD - MLA_QK))).reshape(MLA_Q_RANK, 512),
        wuk=jnp.pad(kv[:, :, :HEAD_DIM], ((0, 0), (0, 0), (0, HEAD_DIM))).reshape(MLA_KV_RANK, 512),
        wuv=kv[:, :, HEAD_DIM:].reshape(MLA_KV_RANK, 256))


def _prep_weights_b(w_o, w_gate, w_up, w_down):
    return dict(w_o=w_o, wgu=jnp.concatenate([w_gate, w_up], axis=1), w_down=w_down)


def _unprep_grads(dwin_p, dwuq, dwuk, dwuv, dwo, dwgu, dwd):
    dw_in = jnp.concatenate([dwin_p[:, 0:768], dwin_p[:, 2560:2564], dwin_p[:, 2048:2432], dwin_p[:, 2496:2528],
                             dwin_p[:, 768:2048]], axis=1)
    dw_uq = dwuq.reshape(MLA_Q_RANK, 4, MLA_PAD)[:, :, :MLA_QK].reshape(MLA_Q_RANK, 4 * MLA_QK)
    dw_ukv = jnp.concatenate([dwuk.reshape(MLA_KV_RANK, 4, MLA_PAD)[:, :, :HEAD_DIM],
                              dwuv.reshape(MLA_KV_RANK, 4, HEAD_DIM)], axis=2).reshape(MLA_KV_RANK, 512)
    return dict(w_in=dw_in, mla_w_uq=dw_uq, mla_w_ukv=dw_ukv, w_o=dwo, w_gate=dwgu[:, :D_FF], w_up=dwgu[:, D_FF:],
                w_down=dwd)


def _layer_fwd(l, x, xb, W, P, tabs, blk, late_weights=None):
    S = x.shape[0]
    nb = S // blk
    n = lambda s: f"l{l}_{s}"
    bs = min(512, S)
    h_att = _mm(xb, W["win_a"], name=n("in_att"), out_dtype=MXU_DTYPE, bm=1024, bn=1024, bk=1024, colscale=Q_COLSCALE)
    lat = _mm(xb, W["win_l"], name=n("in_lat"), bm=2048, bn=LAT_W, bk=1024)
    fg = lat[:, 512:516].T.reshape(4, S // 128, 128)
    cpos, cneg = _fox_gate_fwd(fg, P["fox_b_f"], name=n("fox_gate"))
    one3 = jnp.ones((S, 4, 3), MXU_DTYPE)
    zpad = jnp.zeros((S, 4, MLA_PAD - HEAD_DIM - 6), MXU_DTYPE)
    per_tok = lambda parts: parts.reshape(4, 3, S).transpose(2, 0, 1)
    q_f = jnp.concatenate([h_att[:, COL_FQ:COL_FQ + 256].reshape(S, 4, HEAD_DIM), per_tok(cpos), one3, zpad],
                          axis=2).reshape(S, 4 * MLA_PAD)
    k_f = jnp.concatenate([h_att[:, COL_FK:COL_FK + 256].reshape(S, 4, HEAD_DIM), one3, per_tok(cneg), zpad],
                          axis=2).reshape(S, 4 * MLA_PAD)
    v_f = h_att[:, COL_FV:COL_FV + 256]
    oT_a, lse_a = _smax_fwd_t(q_f.T, k_f, _kv_blocks_t(v_f, blk), dk=MLA_PAD, blk=blk, name=n("fox_fwd"))
    q_m, k_m, v_m = _mla_prep_fwd(lat, P["mla_g_q"], P["mla_g_kv"], W["wuq"], W["wuk"], W["wuv"], *tabs,
                                  bs=bs, name=n("mla_prep"))
    oT_b, lse_b = _smax_fwd_t(q_m.T, k_m, _kv_blocks_t(v_m, blk), dk=MLA_PAD, blk=blk, name=n("mla_fwd"))
    qT_c = h_att[:, COL_SQ:COL_SQ + 256].T
    oT_c, lt_c = _sb_fwd_t(qT_c, h_att, _kv_blocks_t(h_att[:, COL_SV:COL_SV + 256], blk), blk=blk, name=n("sb_fwd"))
    out_d, lse_d = _swa_fwd(h_att, P["swa_sinks"], name=n("swa_fwd"))
    outs = (oT_a.T, oT_b.T, oT_c.T, out_d)
    gn = _gnorm_fwd(outs, P["mix_g"], bs=bs, name=n("gnorm"))
    if late_weights is not None:
        W = dict(W, **late_weights(gn))
    u1 = _mm(gn, W["w_o"], name=n("out_proj"), bm=1024, bn=1024, bk=1024, resid=x, alpha=ALPHA)
    x1, x1b, xh1, rs1 = _ln_fwd(u1, P["ln1_g"], P["ln1_b"], bs=bs, name=n("ln1"))
    gu = _mm(x1b, W["wgu"], name=n("gate_up"), bm=2048, bn=512, bk=1024)
    a = _swiglu_fwd(gu, bs=min(256, S), name=n("swiglu"))
    u2 = _mm(a, W["w_down"], name=n("down"), bm=1024, bn=1024, bk=_blk(D_FF, 1408), resid=x1, alpha=ALPHA)
    x2, x2b, xh2, rs2 = _ln_fwd(u2, P["ln2_g"], P["ln2_b"], bs=bs, name=n("ln2"))
    saved = dict(xb=xb, h_att=h_att, lat=lat, fg=fg, outs=outs, oT_a=oT_a, oT_b=oT_b, q_f=q_f, k_f=k_f, v_f=v_f,
                 qT_c=qT_c, lse_a=lse_a, lse_b=lse_b, lse_d=lse_d, lt_c=lt_c, q_m=q_m, k_m=k_m, v_m=v_m, gn=gn,
                 xh1=xh1, rs1=rs1, x1b=x1b, gu=gu, a=a, xh2=xh2, rs2=rs2)
    return x2, x2b, saved, W


def _layer_bwd(l, dx2, sv, W, P, tabs, blk, send_early=None):
    S = dx2.shape[0]
    n = lambda s: f"l{l}_{s}"
    bs = min(512, S)
    h_att = sv["h_att"]
    du2, du2b, dg2, db2 = _ln_bwd(dx2, sv["xh2"], sv["rs2"], P["ln2_g"], bs=bs, name=n("ln2_bwd"))
    da = _mm(du2b, W["w_down"], name=n("down_dx"), tb=True, bm=1024, bn=_blk(D_FF, 1408), bk=1024)
    dwd = _mm(sv["a"].T, du2b, name=n("down_dw"), bm=_blk(D_FF, 1408), bn=1024, bk=1024)
    dgu = _swiglu_bwd(da, sv["gu"], bs=min(256, S), name=n("swiglu_bwd"))
    dx1 = _mm(dgu, W["wgu"], name=n("gate_up_dx"), tb=True, bm=1024, bn=1024, bk=_blk(2 * D_FF, 1408), resid=du2,
              alpha=ALPHA)
    dwgu = _mm(sv["x1b"].T, dgu, name=n("gate_up_dw"), bm=1024, bn=_blk(2 * D_FF, 1408), bk=1024)
    du1, du1b, dg1, db1 = _ln_bwd(dx1, sv["xh1"], sv["rs1"], P["ln1_g"], bs=bs, name=n("ln1_bwd"))
    dgn = _mm(du1b, W["w_o"], name=n("out_proj_dx"), tb=True, bm=1024, bn=1024, bk=1024)
    dwo = _mm(sv["gn"].T, du1b, name=n("out_proj_dw"), bm=1024, bn=1024, bk=1024)
    mix_g = P["mix_g"]
    if send_early is not None:
        mix_g = mix_g + send_early(dict(w_o=dwo, w_gate=dwgu[:, :D_FF], w_up=dwgu[:, D_FF:], w_down=dwd))[0, 0]
    dmix, dmixg = _gnorm_bwd(dgn, sv["outs"], mix_g, bs=bs, name=n("gnorm_bwd"))
    dmixT = dmix.T
    q_f, k_f = sv["q_f"], sv["k_f"]
    dqT_a, dk_a, dva = _smax_bwd_t(q_f.T, q_f, k_f, _kv_blocks_t(k_f, blk), sv["v_f"], dmix, dmixT, sv["oT_a"],
                                   sv["lse_a"], dk=MLA_PAD, dcb=0, qscale=HEAD_DIM ** -0.5, blk=blk, name=n("fox_bwd"))
    dq_a, dk_a = dqT_a.T.reshape(S, 4, MLA_PAD), dk_a.reshape(S, 4, MLA_PAD)
    dqa, dka = dq_a[:, :, :HEAD_DIM].reshape(S, 256), dk_a[:, :, :HEAD_DIM].reshape(S, 256)
    dcq = dq_a[:, :, HEAD_DIM].T.reshape(4, S // 128, 128)
    dck = dk_a[:, :, HEAD_DIM + 3].T.reshape(4, S // 128, 128)
    q_m, k_m = sv["q_m"], sv["k_m"]
    dqT_b, dkb, dvb = _smax_bwd_t(q_m.T, q_m, k_m, _kv_blocks_t(k_m, blk), sv["v_m"], dmix, dmixT, sv["oT_b"],
                                  sv["lse_b"], dk=MLA_PAD, dcb=2, qscale=MLA_QK ** -0.5, blk=blk, name=n("mla_bwd"))
    dqT_c, dkc, dvc = _sb_bwd_t(sv["qT_c"], h_att, _kv_blocks_t(h_att[:, COL_SK:COL_SK + 256], blk), dmix, dmixT,
                                sv["lt_c"], dcb=4, qscale=HEAD_DIM ** -0.5, blk=blk, name=n("sb_bwd"))
    dqc = dqT_c.T
    dqd, dkd, dvd, dsink = _swa_bwd(h_att, P["swa_sinks"], dmix, sv["outs"][3], sv["lse_d"], dcb=3, name=n("swa_bwd"))
    dlat, dwuq, dwuk, dwuv, dgq, dgkv = _mla_prep_bwd(
        sv["lat"], P["mla_g_q"], P["mla_g_kv"], W["wuq"], W["wuk"], W["wuv"], *tabs, dqT_b.T, dkb, dvb,
        bs=bs, name=n("mla_prep_bwd"))
    dfg, dbf = _fox_gate_bwd(sv["fg"], P["fox_b_f"], dck, dcq, q_unscale=HEAD_DIM ** 0.5, name=n("fox_gate_bwd"))
    dfg_blk = jnp.pad(dfg.reshape(4, S).T, ((0, 0), (0, 124)))
    dh = jnp.concatenate([t.astype(MXU_DTYPE) for t in (dqa, dka, dva, dqc, dkc, dvc, dqd, dkd, dvd, dlat, dfg_blk)], axis=1)
    dx = _mm(dh, W["win_p"], name=n("in_dx"), tb=True, bm=1024, bn=1024, bk=_blk(PERM_W, 1024), resid=du1, alpha=ALPHA)
    dwin_p = _mm(sv["xb"].T, dh, name=n("in_dw"), bm=1024, bn=_blk(PERM_W, 1024), bk=1024)
    grads = _unprep_grads(dwin_p, dwuq, dwuk, dwuv, dwo, dwgu, dwd)
    grads.update(fox_b_f=dbf[:, 0], mla_g_q=dgq[0], mla_g_kv=dgkv[0], swa_sinks=dsink[:, 0], mix_g=dmixg[0],
                 ln1_g=dg1[0], ln1_b=db1[0], ln2_g=dg2[0], ln2_b=db2[0])
    return dx, grads


BIG = ("w_in", "mla_w_uq", "mla_w_ukv", "w_o", "w_gate", "w_up", "w_down")
SMALL = ("fox_b_f", "mla_g_q", "mla_g_kv", "swa_sinks", "mix_g", "ln1_g", "ln1_b", "ln2_g", "ln2_b")
SHARD_AXIS = dict(w_in=2, mla_w_uq=2, mla_w_ukv=2, w_o=1, w_gate=2, w_up=2, w_down=1)
N_CHIPS = 4
ANY = pl.BlockSpec(memory_space=pl.ANY)


def _chip_exchange(tensors, *, scatter, name):
    nt = len(tensors)

    def body(*refs):
        ins, outs = refs[:nt], refs[nt:2 * nt]
        send_sems, recv_sems, local_sems = refs[2 * nt:]
        x, y, c = lax.axis_index("x"), lax.axis_index("y"), lax.axis_index("c")
        me = 2 * x + y
        peers = [(1 - x, y), (x, 1 - y), (1 - x, 1 - y)]
        local, sends, recvs = [], [], []
        for t in range(nt):
            local.append(pltpu.make_async_copy(ins[t].at[me] if scatter else ins[t], outs[t].at[me], local_sems.at[t]))
            for r, (px, py) in enumerate(peers):
                k = 3 * t + r
                theirs = 2 * px + py
                sends.append(pltpu.make_async_remote_copy(
                    src_ref=ins[t].at[theirs] if scatter else ins[t], dst_ref=outs[t].at[me],
                    send_sem=send_sems.at[k], recv_sem=recv_sems.at[k], device_id=(px, py, c), device_id_type=MESH))
                recvs.append(pltpu.make_async_remote_copy(
                    src_ref=ins[t].at[me] if scatter else ins[t], dst_ref=outs[t].at[theirs],
                    send_sem=send_sems.at[k], recv_sem=recv_sems.at[k], device_id=(px, py, c), device_id_type=MESH))
        for cp in local + sends:
            cp.start()
        for cp in recvs:
            cp.wait_recv()
        for cp in sends:
            cp.wait_send()
        for cp in local:
            cp.wait()

    out_shape = [jax.ShapeDtypeStruct(t.shape if scatter else (N_CHIPS,) + t.shape, t.dtype) for t in tensors]
    return pl.pallas_call(
        body, name=name, in_specs=[ANY] * nt, out_specs=[ANY] * nt, out_shape=out_shape,
        scratch_shapes=[pltpu.SemaphoreType.DMA((3 * nt,)), pltpu.SemaphoreType.DMA((3 * nt,)),
                        pltpu.SemaphoreType.DMA((nt,))],
        compiler_params=pltpu.CompilerParams(has_side_effects=True),
    )(*tensors)


HBM = pl.BlockSpec(memory_space=pltpu.HBM)
SEM = pl.BlockSpec(memory_space=pltpu.SEMAPHORE)
N_PEER_CHIPS = N_CHIPS - 1


def _peer_copies(src_ref, land_ref, sems, scatter):
    x, y, c = lax.axis_index("x"), lax.axis_index("y"), lax.axis_index("c")
    me = 2 * x + y
    out = []
    for r, (px, py) in enumerate([(1 - x, y), (x, 1 - y), (1 - x, 1 - y)]):
        theirs = 2 * px + py
        send = pltpu.make_async_remote_copy(
            src_ref=src_ref.at[theirs] if scatter else src_ref, dst_ref=land_ref.at[me],
            send_sem=sems[2 * r], recv_sem=sems[2 * r + 1], device_id=(px, py, c), device_id_type=MESH)
        arrive = pltpu.make_async_remote_copy(
            src_ref=src_ref.at[me] if scatter else src_ref, dst_ref=land_ref.at[theirs],
            send_sem=sems[2 * r], recv_sem=sems[2 * r + 1], device_id=(px, py, c), device_id_type=MESH)
        out.append((send, arrive))
    return out


def _exchange_start(srcs, *, scatter, name):
    nt = len(srcs)
    ns = 2 * N_PEER_CHIPS * nt
    land_shapes = [s.shape if scatter else (N_CHIPS,) + s.shape for s in srcs]

    def body(*refs):
        src_refs, land_refs, outs = refs[:nt], refs[nt:2 * nt], refs[2 * nt:]
        for t in range(nt):
            for send, _ in _peer_copies(src_refs[t], land_refs[t], outs[6 * t:6 * t + 6], scatter):
                send.start()
        outs[-1][...] = jnp.zeros_like(outs[-1])

    res = pl.pallas_call(
        body, name=name,
        out_shape=(*[pltpu.SemaphoreType.DMA(())] * ns, *[pltpu.HBM(s.shape, s.dtype) for s in srcs],
                   *[pltpu.HBM(ls, s.dtype) for ls, s in zip(land_shapes, srcs)], jax.ShapeDtypeStruct((8, 128), F32)),
        in_specs=(HBM,) * (2 * nt), out_specs=(*[SEM] * ns, *[HBM] * (2 * nt), pl.BlockSpec(memory_space=pltpu.VMEM)),
        input_output_aliases={i: ns + i for i in range(2 * nt)},
        compiler_params=pltpu.CompilerParams(has_side_effects=pltpu.SideEffectType.DATAFLOW_SIDE_EFFECTING),
    )(*[pltpu.with_memory_space_constraint(s, pltpu.HBM) for s in srcs],
      *[pltpu.with_memory_space_constraint(lax.empty(ls, s.dtype), pltpu.HBM) for ls, s in zip(land_shapes, srcs)])
    return dict(sems=res[:ns], srcs=res[ns:ns + nt], lands=res[ns + nt:ns + 2 * nt], token=res[-1])


def _exchange_wait(started, after, *, scatter, name):
    nt = len(started["srcs"])
    ns = 2 * N_PEER_CHIPS * nt

    def body(*refs):
        src_refs, land_refs, sems = refs[:nt], refs[nt:2 * nt], refs[2 * nt:2 * nt + ns]
        for t in range(nt):
            for send, arrive in _peer_copies(src_refs[t], land_refs[t], sems[6 * t:6 * t + 6], scatter):
                send.wait_send()
                arrive.wait_recv()

    both = list(started["srcs"]) + list(started["lands"])
    res = pl.pallas_call(
        body, name=name, out_shape=tuple(pltpu.HBM(a.shape, a.dtype) for a in both),
        in_specs=(*[HBM] * (2 * nt), *[SEM] * ns, ANY), out_specs=(HBM,) * (2 * nt),
        input_output_aliases={i: i for i in range(2 * nt)},
        compiler_params=pltpu.CompilerParams(has_side_effects=pltpu.SideEffectType.DATAFLOW_SIDE_EFFECTING),
    )(*both, *started["sems"], after)
    return res[:nt], res[nt:]


def _core_exchange(tensors, *, name):
    nt = len(tensors)

    def body(*refs):
        ins, outs = refs[:nt], refs[nt:2 * nt]
        send_sems, recv_sems = refs[2 * nt:]
        sibling = (lax.axis_index("x"), lax.axis_index("y"), 1 - lax.axis_index("c"))
        copies = [pltpu.make_async_remote_copy(src_ref=ins[t], dst_ref=outs[t], send_sem=send_sems.at[t],
                                               recv_sem=recv_sems.at[t], device_id=sibling, device_id_type=MESH)
                  for t in range(nt)]
        for cp in copies:
            cp.start()
        for cp in copies:
            cp.wait_recv()
        for cp in copies:
            cp.wait_send()

    return pl.pallas_call(
        body, name=name, in_specs=[ANY] * nt, out_specs=[ANY] * nt,
        out_shape=[jax.ShapeDtypeStruct(t.shape, t.dtype) for t in tensors],
        scratch_shapes=[pltpu.SemaphoreType.DMA((nt,)), pltpu.SemaphoreType.DMA((nt,))],
        compiler_params=pltpu.CompilerParams(has_side_effects=True),
    )(*tensors)


def _all_sum_small(block, *, name):
    R = block.shape[0]
    n_dev = 8

    def body(x_ref, o_ref, slots, send_sems, recv_sems):
        x, y, c = lax.axis_index("x"), lax.axis_index("y"), lax.axis_index("c")
        me = 4 * x + 2 * y + c
        slots[me] = x_ref[...]
        sends, recvs = [], []
        for d in range(1, n_dev):
            px, py, pc = x ^ (d >> 2), y ^ ((d >> 1) & 1), c ^ (d & 1)
            theirs = 4 * px + 2 * py + pc
            sends.append(pltpu.make_async_remote_copy(
                src_ref=x_ref, dst_ref=slots.at[me], send_sem=send_sems.at[d - 1], recv_sem=recv_sems.at[d - 1],
                device_id=(px, py, pc), device_id_type=MESH))
            recvs.append(pltpu.make_async_remote_copy(
                src_ref=x_ref, dst_ref=slots.at[theirs], send_sem=send_sems.at[d - 1], recv_sem=recv_sems.at[d - 1],
                device_id=(px, py, pc), device_id_type=MESH))
        for cp in sends:
            cp.start()
        for cp in recvs:
            cp.wait_recv()
        for cp in sends:
            cp.wait_send()
        total = slots[0]
        for k in range(1, n_dev):
            total = total + slots[k]
        o_ref[...] = total

    return pl.pallas_call(
        body, name=name, in_specs=[pl.BlockSpec(memory_space=pltpu.VMEM)],
        out_specs=pl.BlockSpec(memory_space=pltpu.VMEM), out_shape=jax.ShapeDtypeStruct((R, 128), F32),
        scratch_shapes=[pltpu.VMEM((n_dev, R, 128), F32), pltpu.SemaphoreType.DMA((n_dev - 1,)),
                        pltpu.SemaphoreType.DMA((n_dev - 1,))],
        compiler_params=pltpu.CompilerParams(has_side_effects=True),
    )(block)


def _sum_chips(recv, *, br, name):
    _, R, C = recv.shape

    def body(r_ref, o_ref):
        total = r_ref[0].astype(F32)
        for k in range(1, N_CHIPS):
            total = total + r_ref[k].astype(F32)
        o_ref[...] = total

    return pl.pallas_call(
        body, name=name, grid=(R // br,), in_specs=[pl.BlockSpec((N_CHIPS, br, C), lambda i: (0, i, 0))],
        out_specs=pl.BlockSpec((br, C), lambda i: (i, 0)), out_shape=jax.ShapeDtypeStruct((R, C), F32),
        compiler_params=_cparams("parallel"),
    )(recv)


def _sum_chips_into(acc, land, own, me, layer, *, br, name):
    _, R, C = land.shape

    def body(me_ref, land_ref, own_ref, acc_ref, o_ref):
        mine = me_ref[0]
        total = None
        for k in range(N_CHIPS):
            part = jnp.where(mine == k, own_ref[...], land_ref[k]).astype(F32)
            total = part if total is None else total + part
        o_ref[0] = total

    return pl.pallas_call(
        body, name=name, grid=(R // br,),
        in_specs=[pl.BlockSpec(memory_space=pltpu.SMEM), pl.BlockSpec((N_CHIPS, br, C), lambda i: (0, i, 0)),
                  pl.BlockSpec((br, C), lambda i: (i, 0)), ANY],
        out_specs=pl.BlockSpec((1, br, C), lambda i: (layer, i, 0)),
        out_shape=jax.ShapeDtypeStruct(acc.shape, F32), input_output_aliases={3: 0},
        compiler_params=_cparams("parallel"),
    )(me, land, own, acc)


def _adamw_math(w, g, m, v):
    m = ADAM_B1 * m + (1.0 - ADAM_B1) * g
    v = ADAM_B2 * v + (1.0 - ADAM_B2) * (g * g)
    m_hat = m / (1.0 - ADAM_B1 ** ADAM_STEP)
    v_hat = v / (1.0 - ADAM_B2 ** ADAM_STEP)
    return -ADAM_LR * (m_hat / (jnp.sqrt(v_hat) + ADAM_EPS) + ADAM_WD * w), m, v


def _adamw(w, m, v, g_a, g_b, *, br, name):
    R, C = w.shape
    two = g_b is not None

    def body(*refs):
        if two:
            w_ref, m_ref, v_ref, ga_ref, gb_ref, g_ref, d_ref, nm_ref, nv_ref = refs
            g = ga_ref[...] + gb_ref[...]
        else:
            w_ref, m_ref, v_ref, ga_ref, g_ref, d_ref, nm_ref, nv_ref = refs
            g = ga_ref[...]
        g_ref[...] = g
        d_ref[...], nm_ref[...], nv_ref[...] = _adamw_math(w_ref[...], g, m_ref[...], v_ref[...])

    spec = pl.BlockSpec((br, C), lambda i: (i, 0))
    args = [w, m, v, g_a] + ([g_b] if two else [])
    return pl.pallas_call(
        body, name=name, grid=(R // br,), in_specs=[spec] * len(args), out_specs=[spec] * 4,
        out_shape=[jax.ShapeDtypeStruct((R, C), F32)] * 4,
        compiler_params=_cparams("parallel"),
    )(*args)


SMALL_ROWS = dict(fox_b_f=1, mla_g_q=2, mla_g_kv=1, swa_sinks=1, mix_g=8, ln1_g=8, ln1_b=8, ln2_g=8, ln2_b=8)
SMALL_ROWS_PER_LAYER = sum(SMALL_ROWS.values())


def _pack_small(vals, extra_rows):
    L = vals[SMALL[0]].shape[0]
    per_layer = []
    for name in SMALL:
        a = vals[name].astype(F32)
        a = jnp.pad(a, ((0, 0), (0, SMALL_ROWS[name] * 128 - a.shape[1])))
        per_layer.append(a.reshape(L, SMALL_ROWS[name], 128))
    out = jnp.concatenate(per_layer, axis=1).reshape(L * SMALL_ROWS_PER_LAYER, 128)
    return jnp.pad(out, ((0, extra_rows), (0, 0)))


def _unpack_small(block, shapes):
    L = shapes[SMALL[0]][0]
    body = block[:L * SMALL_ROWS_PER_LAYER].reshape(L, SMALL_ROWS_PER_LAYER, 128)
    out, r = {}, 0
    for name in SMALL:
        n = shapes[name][1]
        out[name] = body[:, r:r + SMALL_ROWS[name]].reshape(L, SMALL_ROWS[name] * 128)[:, :n]
        r += SMALL_ROWS[name]
    return out


PACK_ROW_MULTIPLE = 256


def _pack(parts):
    flat = [p.reshape(-1, 128) for p in parts]
    pad = (-sum(f.shape[0] for f in flat)) % PACK_ROW_MULTIPLE
    if pad:
        flat.append(jnp.zeros((pad, 128), flat[0].dtype))
    return jnp.concatenate(flat, axis=0)


def _unpack(block, shapes):
    out, r = [], 0
    for shp in shapes:
        n = int(np.prod(shp)) // 128
        out.append(block[r:r + n].reshape(shp))
        r += n
    return out


def _shard(g, k, axis):
    n = g.shape[axis] // N_CHIPS
    return lax.slice_in_dim(g, k * n, (k + 1) * n, axis=axis)


def _to_chips(g, axis):
    L, a, b = g.shape
    if axis == 2:
        return g.reshape(L, a, N_CHIPS, b // N_CHIPS).transpose(2, 0, 1, 3)
    return g.reshape(L, N_CHIPS, a // N_CHIPS, b).transpose(1, 0, 2, 3)


def _from_chips(g, axis):
    _, L, a, b = g.shape
    if axis == 2:
        return g.transpose(1, 2, 0, 3).reshape(L, a, N_CHIPS * b)
    return g.transpose(1, 0, 2, 3).reshape(L, N_CHIPS * a, b)


def kernel(x, w_in, fox_b_f, mla_g_q, mla_g_kv, mla_w_uq, mla_w_ukv, swa_sinks, mix_g, w_o, ln1_g, ln1_b, w_gate, w_up, w_down, ln2_g, ln2_b, loss_target, m_w_in, m_fox_b_f, m_mla_g_q, m_mla_g_kv, m_mla_w_uq, m_mla_w_ukv, m_swa_sinks, m_mix_g, m_w_o, m_ln1_g, m_ln1_b, m_w_gate, m_w_up, m_w_down, m_ln2_g, m_ln2_b, v_w_in, v_fox_b_f, v_mla_g_q, v_mla_g_kv, v_mla_w_uq, v_mla_w_ukv, v_swa_sinks, v_mix_g, v_w_o, v_ln1_g, v_ln1_b, v_w_gate, v_w_up, v_w_down, v_ln2_g, v_ln2_b):
    w = dict(w_in=w_in, fox_b_f=fox_b_f, mla_g_q=mla_g_q, mla_g_kv=mla_g_kv, mla_w_uq=mla_w_uq, mla_w_ukv=mla_w_ukv,
             swa_sinks=swa_sinks, mix_g=mix_g, w_o=w_o, ln1_g=ln1_g, ln1_b=ln1_b, w_gate=w_gate, w_up=w_up,
             w_down=w_down, ln2_g=ln2_g, ln2_b=ln2_b)
    m = dict(w_in=m_w_in, fox_b_f=m_fox_b_f, mla_g_q=m_mla_g_q, mla_g_kv=m_mla_g_kv, mla_w_uq=m_mla_w_uq,
             mla_w_ukv=m_mla_w_ukv, swa_sinks=m_swa_sinks, mix_g=m_mix_g, w_o=m_w_o, ln1_g=m_ln1_g, ln1_b=m_ln1_b,
             w_gate=m_w_gate, w_up=m_w_up, w_down=m_w_down, ln2_g=m_ln2_g, ln2_b=m_ln2_b)
    v = dict(w_in=v_w_in, fox_b_f=v_fox_b_f, mla_g_q=v_mla_g_q, mla_g_kv=v_mla_g_kv, mla_w_uq=v_mla_w_uq,
             mla_w_ukv=v_mla_w_ukv, swa_sinks=v_swa_sinks, mix_g=v_mix_g, w_o=v_w_o, ln1_g=v_ln1_g, ln1_b=v_ln1_b,
             w_gate=v_w_gate, w_up=v_w_up, w_down=v_w_down, ln2_g=v_ln2_g, ln2_b=v_ln2_b)
    names = tuple(w)
    L = w_in.shape[0]
    S = x.shape[1]
    blk = min(256, S)
    bs = min(512, S)

    me = 2 * lax.axis_index("x") + lax.axis_index("y")
    axis_of = {k: SHARD_AXIS[k] - 1 for k in BIG}
    groups = (("w_in", "mla_w_uq", "mla_w_ukv"), ("w_o", "w_gate", "w_up", "w_down"))

    started, last = [], None
    for l in range(L):
        per_group = []
        for g, group in enumerate(groups):
            srcs = [w[k][l].astype(MXU_DTYPE) for k in group]
            if last is not None:
                t = min(range(len(srcs)), key=lambda i: srcs[i].size)
                srcs[t] = srcs[t] + last["token"][0, 0].astype(MXU_DTYPE)
            last = _exchange_start(srcs, scatter=False, name=f"gather_start{l}_{g}")
            per_group.append(last)
        started.append(per_group)
    all_started = sum(st["token"] for per_group in started for st in per_group)

    def gathered(l, g, after):
        mine, lands = _exchange_wait(started[l][g], after, scatter=False, name=f"gather_wait{l}_{g}")
        return [jnp.concatenate([jnp.where(me == k, mine[t], lands[t][k]) for k in range(N_CHIPS)], axis=axis_of[name])
                for t, name in enumerate(groups[g])]

    def scatter(l, g, grads):
        to_owner = [_to_chips(grads[k][None], axis_of[k] + 1)[:, 0].astype(MXU_DTYPE) for k in groups[g]]
        return _exchange_start(to_owner, scatter=True, name=f"scatter_start{l}_{g}")

    tabs = _rope_tables(S)
    Ps = []
    for l in range(L):
        P = dict(fox_b_f=fox_b_f[l], swa_sinks=swa_sinks[l])
        for k in ("mla_g_q", "mla_g_kv", "mix_g", "ln1_g", "ln1_b", "ln2_g", "ln2_b"):
            P[k] = w[k][l][None, :]
        Ps.append(P)

    xa = x[0]
    xb = xa.astype(MXU_DTYPE)
    saved, Ws = [], []
    for l in range(L):
        W = _prep_weights_a(*gathered(l, 0, all_started if l == 0 else xa))
        late = lambda after, l=l: _prep_weights_b(*gathered(l, 1, after))
        xa, xb, sv, W = _layer_fwd(l, xa, xb, W, Ps[l], tabs, blk, late_weights=late)
        saved.append(sv)
        Ws.append(W)
    dx, loss_part = _loss_head(xa, loss_target[0], bs=bs, name="loss_head")

    layer_grads = [None] * L
    sent = [[None, None] for _ in range(L)]
    pin = None
    for l in reversed(range(L)):
        P = Ps[l] if pin is None else dict(Ps[l], ln2_g=Ps[l]["ln2_g"] + pin[0, 0])

        def send_early(grads, l=l):
            sent[l][1] = scatter(l, 1, grads)
            return sent[l][1]["token"]

        dx, layer_grads[l] = _layer_bwd(l, dx, saved[l], Ws[l], P, tabs, blk, send_early=send_early)
        sent[l][0] = scatter(l, 0, layer_grads[l])
        pin = sent[l][0]["token"]
    grad_x = dx[None]

    me_arr = me.astype(jnp.int32)[None]
    partial = {k: jnp.zeros(w[k].shape, F32) for k in BIG}
    after = dx
    for l in reversed(range(L)):
        for g in (1, 0):
            mine, lands = _exchange_wait(sent[l][g], after, scatter=True, name=f"scatter_wait{l}_{g}")
            for t, k in enumerate(groups[g]):
                own = lax.dynamic_index_in_dim(mine[t], me, 0, keepdims=False)
                partial[k] = _sum_chips_into(partial[k], lands[t], own, me_arr, l, br=_rows(own.shape[0]),
                                             name=f"sum_{k}_l{l}")
            after = partial[groups[g][-1]]
    partial = [partial[k] for k in BIG]
    sibling = _core_exchange(partial, name="swap_partials")
    local = {k: jnp.stack([layer_grads[l][k] for l in range(L)]) for k in SMALL}
    out = {}
    for k, mine, theirs in zip(BIG, partial, sibling):
        shp = w[k].shape
        two_d = lambda a: a.reshape(shp[0] * shp[1], shp[2])
        res = _adamw(two_d(w[k]), two_d(m[k]), two_d(v[k]), two_d(mine), two_d(theirs), br=_rows(shp[0] * shp[1]),
                     name=f"adamw_{k}")
        out[k] = [a.reshape(shp) for a in res]

    shapes = {k: w[k].shape for k in SMALL}
    extra = 8 + (-L * SMALL_ROWS_PER_LAYER) % 8
    block = _pack_small({k: local[k] for k in SMALL}, extra)
    block = block.at[L * SMALL_ROWS_PER_LAYER, 0].set(loss_part[0, 0])
    total = _all_sum_small(block, name="sum_small")
    loss = total[L * SMALL_ROWS_PER_LAYER, 0]
    res = _adamw(_pack_small({k: w[k] for k in SMALL}, extra), _pack_small({k: m[k] for k in SMALL}, extra),
                 _pack_small({k: v[k] for k in SMALL}, extra), total, None, br=total.shape[0], name="adamw_small")
    res = [_unpack_small(t, shapes) for t in res]
    for k in SMALL:
        out[k] = [r[k] for r in res]

    return (loss, grad_x, *[out[k][0] for k in names], *[out[k][1] for k in names],
            *[out[k][2] for k in names], *[out[k][3] for k in names])


def _rows(n):
    for b in (256, 128, 64, 32, 16, 8):
        if n % b == 0:
            return b
    return n
```

```python
import functools

import numpy as np
import jax
import jax.numpy as jnp
from jax import lax
from jax.experimental import pallas as pl
from jax.experimental.pallas import tpu as pltpu

F32 = jnp.float32
MXU_DTYPE = jnp.bfloat16
NEG_INF = -1e30

D_MODEL = 1024
DEPTH = 4
HEAD_DIM = 64
GROUP_WIDTH = 256
N_GROUPS = 4
D_FF = 2816
MLA_Q_RANK = 256
MLA_KV_RANK = 128
MLA_ROPE = 32
MLA_QK = 96
MLA_PAD = 128
ROPE_THETA = 10000.0
WINDOW = 128
ALPHA = (2.0 * DEPTH) ** 0.25
SWA_SLOPES = tuple(float(2.0 ** (-8.0 * h / 4)) for h in range(1, 5))
IN_WIDTH = 2468
ATT_W = 2048
LAT_W = 640
PERM_W = ATT_W + LAT_W
COL_FQ, COL_FK, COL_FV = 0, 256, 512
COL_SQ, COL_SK, COL_SV = 768, 1024, 1280
COL_WQ, COL_WK, COL_WV = 1536, 1792, 1920
Q_COLSCALE = np.ones((1, ATT_W), np.float32)
Q_COLSCALE[:, COL_FQ:COL_FQ + 256] = HEAD_DIM ** -0.5
Q_COLSCALE[:, COL_SQ:COL_SQ + 256] = HEAD_DIM ** -0.5

ADAM_LR, ADAM_B1, ADAM_B2, ADAM_EPS, ADAM_WD, ADAM_STEP = 0.001, 0.9, 0.999, 1e-08, 0.01, 10

VMEM_LIMIT = 56 * 1024 * 1024
NT = (((1,), (1,)), ((), ()))
TN = (((0,), (0,)), ((), ()))
MESH = pl.DeviceIdType.MESH


def _cparams(*sem):
    return pltpu.CompilerParams(dimension_semantics=sem, vmem_limit_bytes=VMEM_LIMIT)


def _split2(x):
    hi = x.astype(MXU_DTYPE)
    lo = (x - hi.astype(F32)).astype(MXU_DTYPE)
    return hi, lo


def _dot01(x, m01, dn=None, parts=2):
    acc = None
    rem = x
    for _ in range(parts):
        part = rem.astype(MXU_DTYPE)
        rem = rem - part.astype(F32)
        if dn is None:
            t = jnp.dot(part, m01, preferred_element_type=F32)
        else:
            t = lax.dot_general(part, m01, dn, preferred_element_type=F32)
        acc = t if acc is None else acc + t
    return acc


def _mm(a, b, *, name, ta=False, tb=False, out_dtype=F32, bm=512, bn=512, bk=512, resid=None, alpha=1.0,
        colscale=None):
    M, K = (a.shape[1], a.shape[0]) if ta else a.shape
    N = b.shape[0] if tb else b.shape[1]
    assert (b.shape[1] if tb else b.shape[0]) == K
    assert resid is None or colscale is None
    bm, bn, bk = min(bm, M), min(bn, N), min(bk, K)
    assert M % bm == 0 and N % bn == 0 and K % bk == 0, (name, M, N, K, bm, bn, bk)
    nk = K // bk
    dn = (((0 if ta else 1,), (1 if tb else 0,)), ((), ()))

    extra = resid is not None or colscale is not None

    def body(*refs):
        a_ref, b_ref = refs[:2]
        r_ref = refs[2] if extra else None
        o_ref = refs[3] if extra else refs[2]
        acc_ref = refs[-1] if nk > 1 else None
        k = pl.program_id(2)
        part = lax.dot_general(a_ref[...].astype(MXU_DTYPE), b_ref[...].astype(MXU_DTYPE), dn,
                               preferred_element_type=F32)

        def finish(r):
            if resid is not None:
                r = r + alpha * r_ref[...]
            if colscale is not None:
                r = r * r_ref[...]
            o_ref[...] = r.astype(o_ref.dtype)

        if nk == 1:
            finish(part)
        else:
            @pl.when(k == 0)
            def _():
                acc_ref[...] = part

            @pl.when((k > 0) & (k < nk - 1))
            def _():
                acc_ref[...] += part

            @pl.when(k == nk - 1)
            def _():
                finish(acc_ref[...] + part)

    a_spec = pl.BlockSpec((bk, bm), lambda i, j, k: (k, i)) if ta else pl.BlockSpec((bm, bk), lambda i, j, k: (i, k))
    b_spec = pl.BlockSpec((bn, bk), lambda i, j, k: (j, k)) if tb else pl.BlockSpec((bk, bn), lambda i, j, k: (k, j))
    in_specs = [a_spec, b_spec]
    args = [a, b]
    if resid is not None:
        in_specs.append(pl.BlockSpec((bm, bn), lambda i, j, k: (i, j)))
        args.append(resid)
    if colscale is not None:
        in_specs.append(pl.BlockSpec((1, bn), lambda i, j, k: (0, j)))
        args.append(colscale)
    return pl.pallas_call(
        body, name=name, grid=(M // bm, N // bn, nk), in_specs=in_specs,
        out_specs=pl.BlockSpec((bm, bn), lambda i, j, k: (i, j)),
        out_shape=jax.ShapeDtypeStruct((M, N), out_dtype),
        scratch_shapes=[pltpu.VMEM((bm, bn), F32)] if nk > 1 else [],
        compiler_params=_cparams("parallel", "parallel", "arbitrary"),
    )(*args)


def _softmax_attn_fwd(q_arr, k_arr, v_arr, *, qcb, kcb, vcb, dk, scale, cum_col=None, cum_row=None, blk, name):
    S = q_arr.shape[0]
    nb = S // blk
    bias = cum_col is not None
    W = 2 * dk

    def body(*refs):
        if bias:
            q_ref, k_ref, v_ref, cc_ref, cr_ref, o_ref, lse_ref = refs
        else:
            q_ref, k_ref, v_ref, o_ref, lse_ref = refs
        p = pl.program_id(0)
        i = pl.program_id(1)
        row = lax.broadcasted_iota(jnp.int32, (blk, blk), 0)
        col = lax.broadcasted_iota(jnp.int32, (blk, blk), 1)
        for hh in range(2):
            q = q_ref[:, hh * dk:(hh + 1) * dk]

            def tile(j, carry, masked, hh=hh, q=q):
                m, l, acc = carry
                r0 = pl.multiple_of(j * blk, blk)
                ks = k_ref[pl.ds(r0, blk), hh * dk:(hh + 1) * dk]
                vs = v_ref[pl.ds(r0, blk), hh * HEAD_DIM:(hh + 1) * HEAD_DIM]
                s = lax.dot_general(q, ks, NT, preferred_element_type=F32) * scale
                if bias:
                    s = s + cc_ref[hh] - cr_ref[(2 * p + hh) * nb + j]
                if masked:
                    s = jnp.where(col <= row, s, NEG_INF)
                mn = jnp.maximum(m, jnp.max(s, axis=1, keepdims=True))
                a = jnp.exp(m - mn)
                pe = jnp.exp(s - mn)
                l = a * l + jnp.sum(pe, axis=1, keepdims=True)
                acc = a * acc + jnp.dot(pe.astype(MXU_DTYPE), vs, preferred_element_type=F32)
                return mn, l, acc

            init = (jnp.full((blk, 1), NEG_INF, F32), jnp.zeros((blk, 1), F32), jnp.zeros((blk, HEAD_DIM), F32))
            carry = lax.fori_loop(0, i, functools.partial(tile, masked=False), init)
            m, l, acc = tile(i, carry, True)
            o_ref[:, hh * HEAD_DIM:(hh + 1) * HEAD_DIM] = acc / l
            lse_ref[hh] = m + jnp.log(l)

    in_specs = [pl.BlockSpec((blk, W), lambda p, i: (i, qcb + p)),
                pl.BlockSpec((S, W), lambda p, i: (0, kcb + p)),
                pl.BlockSpec((S, 128), lambda p, i: (0, vcb + p))]
    args = [q_arr, k_arr, v_arr]
    if bias:
        in_specs += [pl.BlockSpec((2, blk, 1), lambda p, i: (p, i, 0)),
                     pl.BlockSpec((4 * nb, 1, blk), lambda p, i: (0, 0, 0))]
        args += [cum_col, cum_row]
    return pl.pallas_call(
        body, name=name, grid=(2, nb), in_specs=in_specs,
        out_specs=[pl.BlockSpec((blk, 128), lambda p, i: (i, p)), pl.BlockSpec((2, blk, 1), lambda p, i: (p, i, 0))],
        out_shape=[jax.ShapeDtypeStruct((S, GROUP_WIDTH), F32), jax.ShapeDtypeStruct((4, S, 1), F32)],
        compiler_params=_cparams("arbitrary", "arbitrary"),
    )(*args)


def _softmax_attn_bwd(q_arr, k_arr, v_arr, dmix, o_arr, lse, *, qcb, kcb, vcb, dcb, dk, scale,
                      cum_col=None, cum_row=None, blk, name):
    S = q_arr.shape[0]
    nb = S // blk
    bias = cum_col is not None
    W = 2 * dk

    def body(*refs):
        if bias:
            q_ref, k_ref, v_ref, do_ref, o_ref, lse_ref, cc_ref, cr_ref, dq_ref, dk_ref, dv_ref, dc_ref, dcq_ref = refs
        else:
            q_ref, k_ref, v_ref, do_ref, o_ref, lse_ref, dq_ref, dk_ref, dv_ref = refs
        p = pl.program_id(0)
        i = pl.program_id(1)

        @pl.when(i == 0)
        def _():
            dk_ref[...] = jnp.zeros_like(dk_ref)
            dv_ref[...] = jnp.zeros_like(dv_ref)
            if bias:
                dc_ref[...] = jnp.zeros_like(dc_ref)

        row = lax.broadcasted_iota(jnp.int32, (blk, blk), 0)
        col = lax.broadcasted_iota(jnp.int32, (blk, blk), 1)
        for hh in range(2):
            q = q_ref[:, hh * dk:(hh + 1) * dk]
            do = do_ref[:, hh * HEAD_DIM:(hh + 1) * HEAD_DIM]
            delta = jnp.sum(do * o_ref[:, hh * HEAD_DIM:(hh + 1) * HEAD_DIM], axis=1, keepdims=True)
            dob = do.astype(MXU_DTYPE)
            lse_h = lse_ref[hh]

            def tile(j, carry, masked, hh=hh, q=q, dob=dob, delta=delta, lse_h=lse_h):
                dq, dcq = carry
                r0 = pl.multiple_of(j * blk, blk)
                ks = k_ref[pl.ds(r0, blk), hh * dk:(hh + 1) * dk]
                vs = v_ref[pl.ds(r0, blk), hh * HEAD_DIM:(hh + 1) * HEAD_DIM]
                s = lax.dot_general(q, ks, NT, preferred_element_type=F32) * scale
                if bias:
                    s = s + cc_ref[hh] - cr_ref[(2 * p + hh) * nb + j]
                if masked:
                    s = jnp.where(col <= row, s, NEG_INF)
                pr = jnp.exp(s - lse_h)
                dp = lax.dot_general(dob, vs, NT, preferred_element_type=F32)
                ds = pr * (dp - delta)
                dsb = ds.astype(MXU_DTYPE)
                dv_ref[pl.ds(r0, blk), hh * HEAD_DIM:(hh + 1) * HEAD_DIM] += lax.dot_general(
                    pr.astype(MXU_DTYPE), dob, TN, preferred_element_type=F32)
                dk_ref[pl.ds(r0, blk), hh * dk:(hh + 1) * dk] += lax.dot_general(
                    dsb, q, TN, preferred_element_type=F32) * scale
                if bias:
                    dc_ref[hh * nb + j] -= jnp.sum(ds, axis=0, keepdims=True)
                    dcq = dcq + jnp.sum(ds, axis=1, keepdims=True)
                return dq + jnp.dot(dsb, ks, preferred_element_type=F32) * scale, dcq

            carry = lax.fori_loop(0, i, functools.partial(tile, masked=False),
                                  (jnp.zeros((blk, dk), F32), jnp.zeros((blk, 1), F32)))
            dq, dcq = tile(i, carry, True)
            dq_ref[:, hh * dk:(hh + 1) * dk] = dq
            if bias:
                dcq_ref[hh] = dcq

    in_specs = [pl.BlockSpec((blk, W), lambda p, i: (i, qcb + p)),
                pl.BlockSpec((S, W), lambda p, i: (0, kcb + p)),
                pl.BlockSpec((S, 128), lambda p, i: (0, vcb + p)),
                pl.BlockSpec((blk, 128), lambda p, i: (i, dcb + p)),
                pl.BlockSpec((blk, 128), lambda p, i: (i, p)),
                pl.BlockSpec((2, blk, 1), lambda p, i: (p, i, 0))]
    args = [q_arr, k_arr, v_arr, dmix, o_arr, lse]
    out_specs = [pl.BlockSpec((blk, W), lambda p, i: (i, p)),
                 pl.BlockSpec((S, W), lambda p, i: (0, p)),
                 pl.BlockSpec((S, 128), lambda p, i: (0, p))]
    out_shape = [jax.ShapeDtypeStruct((S, 4 * dk), F32), jax.ShapeDtypeStruct((S, 4 * dk), F32),
                 jax.ShapeDtypeStruct((S, GROUP_WIDTH), F32)]
    if bias:
        in_specs += [pl.BlockSpec((2, blk, 1), lambda p, i: (p, i, 0)),
                     pl.BlockSpec((4 * nb, 1, blk), lambda p, i: (0, 0, 0))]
        args += [cum_col, cum_row]
        out_specs += [pl.BlockSpec((2 * nb, 1, blk), lambda p, i: (p, 0, 0)), pl.BlockSpec((2, blk, 1), lambda p, i: (p, i, 0))]
        out_shape += [jax.ShapeDtypeStruct((4 * nb, 1, blk), F32), jax.ShapeDtypeStruct((4, S, 1), F32)]
    return pl.pallas_call(
        body, name=name, grid=(2, nb), in_specs=in_specs, out_specs=out_specs, out_shape=out_shape,
        compiler_params=_cparams("arbitrary", "arbitrary"),
    )(*args)


def _sb_tile(q, ks, scale, strict_mask, carry_l, tri_excl):
    z = lax.dot_general(q, ks, NT, preferred_element_type=F32) * scale
    lb = -(jnp.maximum(z, 0.0) + jnp.log(1.0 + jnp.exp(-jnp.abs(z))))
    if strict_mask is not None:
        lb = jnp.where(strict_mask, lb, 0.0)
    between = _dot01(lb, tri_excl) + carry_l
    a = jnp.exp(z + lb + between)
    if strict_mask is not None:
        a = jnp.where(strict_mask, a, 0.0)
    return z, lb, a


def _sb_attn_fwd(h_att, *, blk, name):
    S = h_att.shape[0]
    nb = S // blk
    scale = HEAD_DIM ** -0.5
    qcb, kcb, vcb = COL_SQ // 128, COL_SK // 128, COL_SV // 128

    def body(q_ref, k_ref, v_ref, o_ref, lt_ref):
        i = pl.program_id(1)
        row = lax.broadcasted_iota(jnp.int32, (blk, blk), 0)
        col = lax.broadcasted_iota(jnp.int32, (blk, blk), 1)
        strict = col < row
        tri_excl = (row > col).astype(MXU_DTYPE)
        for hh in range(2):
            sl = slice(hh * HEAD_DIM, (hh + 1) * HEAD_DIM)
            q = q_ref[:, sl]

            def tile(j, carry, mask, sl=sl, q=q):
                cl, acc = carry
                r0 = pl.multiple_of(j * blk, blk)
                _, lb, a = _sb_tile(q, k_ref[pl.ds(r0, blk), sl], scale, mask, cl, tri_excl)
                acc = acc + jnp.dot(a.astype(MXU_DTYPE), v_ref[pl.ds(r0, blk), sl], preferred_element_type=F32)
                return cl + jnp.sum(lb, axis=1, keepdims=True), acc

            carry = tile(i, (jnp.zeros((blk, 1), F32), jnp.zeros((blk, HEAD_DIM), F32)), strict)
            cl, acc = lax.fori_loop(0, i, lambda jj, c: tile(i - 1 - jj, c, None), carry)
            o_ref[:, sl] = acc
            lt_ref[hh] = cl

    return pl.pallas_call(
        body, name=name, grid=(2, nb),
        in_specs=[pl.BlockSpec((blk, 128), lambda p, i: (i, qcb + p)),
                  pl.BlockSpec((S, 128), lambda p, i: (0, kcb + p)),
                  pl.BlockSpec((S, 128), lambda p, i: (0, vcb + p))],
        out_specs=[pl.BlockSpec((blk, 128), lambda p, i: (i, p)), pl.BlockSpec((2, blk, 1), lambda p, i: (p, i, 0))],
        out_shape=[jax.ShapeDtypeStruct((S, GROUP_WIDTH), F32), jax.ShapeDtypeStruct((4, S, 1), F32)],
        compiler_params=_cparams("arbitrary", "arbitrary"),
    )(h_att, h_att, h_att)


def _sb_attn_bwd(h_att, dmix, ltot_arr, *, dcb, blk, name):
    S = h_att.shape[0]
    nb = S // blk
    scale = HEAD_DIM ** -0.5
    qcb, kcb, vcb = COL_SQ // 128, COL_SK // 128, COL_SV // 128

    def body(q_ref, k_ref, v_ref, do_ref, lt_ref, dq_ref, dk_ref, dv_ref):
        i = pl.program_id(1)

        @pl.when(i == 0)
        def _():
            dk_ref[...] = jnp.zeros_like(dk_ref)
            dv_ref[...] = jnp.zeros_like(dv_ref)

        row = lax.broadcasted_iota(jnp.int32, (blk, blk), 0)
        col = lax.broadcasted_iota(jnp.int32, (blk, blk), 1)
        strict = col < row
        up_incl = (row <= col).astype(MXU_DTYPE)
        up_excl = (row < col).astype(MXU_DTYPE)
        for hh in range(2):
            sl = slice(hh * HEAD_DIM, (hh + 1) * HEAD_DIM)
            q = q_ref[:, sl]
            dob = do_ref[:, sl].astype(MXU_DTYPE)
            ltot = lt_ref[hh]

            def tile(j, carry, mask, sl=sl, q=q, dob=dob, ltot=ltot):
                cl, cg, dq = carry
                r0 = pl.multiple_of(j * blk, blk)
                ks = k_ref[pl.ds(r0, blk), sl]
                vs = v_ref[pl.ds(r0, blk), sl]
                z = lax.dot_general(q, ks, NT, preferred_element_type=F32) * scale
                lb = -(jnp.maximum(z, 0.0) + jnp.log(1.0 + jnp.exp(-jnp.abs(z))))
                if mask is not None:
                    lb = jnp.where(mask, lb, 0.0)
                between = ltot - cl - _dot01(lb, up_incl)
                a = jnp.exp(z + lb + between)
                if mask is not None:
                    a = jnp.where(mask, a, 0.0)
                g = lax.dot_general(dob, vs, NT, preferred_element_type=F32) * a
                e = cg + _dot01(g, up_excl)
                dz = g * jnp.exp(lb) - e * jnp.exp(z + lb)
                if mask is not None:
                    dz = jnp.where(mask, dz, 0.0)
                dzb = dz.astype(MXU_DTYPE)
                dv_ref[pl.ds(r0, blk), sl] += lax.dot_general(a.astype(MXU_DTYPE), dob, TN, preferred_element_type=F32)
                dk_ref[pl.ds(r0, blk), sl] += lax.dot_general(dzb, q, TN, preferred_element_type=F32) * scale
                dq = dq + jnp.dot(dzb, ks, preferred_element_type=F32) * scale
                return cl + jnp.sum(lb, axis=1, keepdims=True), cg + jnp.sum(g, axis=1, keepdims=True), dq

            zc = jnp.zeros((blk, 1), F32)
            carry = lax.fori_loop(0, i, lambda j, c: tile(j, c, None), (zc, zc, jnp.zeros((blk, HEAD_DIM), F32)))
            _, _, dq = tile(i, carry, strict)
            dq_ref[:, sl] = dq

    return pl.pallas_call(
        body, name=name, grid=(2, nb),
        in_specs=[pl.BlockSpec((blk, 128), lambda p, i: (i, qcb + p)),
                  pl.BlockSpec((S, 128), lambda p, i: (0, kcb + p)),
                  pl.BlockSpec((S, 128), lambda p, i: (0, vcb + p)),
                  pl.BlockSpec((blk, 128), lambda p, i: (i, dcb + p)),
                  pl.BlockSpec((2, blk, 1), lambda p, i: (p, i, 0))],
        out_specs=[pl.BlockSpec((blk, 128), lambda p, i: (i, p)),
                   pl.BlockSpec((S, 128), lambda p, i: (0, p)),
                   pl.BlockSpec((S, 128), lambda p, i: (0, p))],
        out_shape=[jax.ShapeDtypeStruct((S, GROUP_WIDTH), F32)] * 3,
        compiler_params=_cparams("arbitrary", "arbitrary"),
    )(h_att, h_att, h_att, dmix, ltot_arr)


HP = 4


def _kv_blocks_t(a, blk):
    S, C = a.shape
    return a.reshape(S // blk, blk, C).transpose(0, 2, 1)


def _smax_fwd_t(qT, k, vT3, *, dk, blk, name):
    S = k.shape[0]
    nb = S // blk
    H = k.shape[1] // dk

    def body(qT_ref, k_ref, vT_ref, oT_ref, lse_ref):
        i = pl.program_id(1)
        key = lax.broadcasted_iota(jnp.int32, (blk, blk), 0)
        qry = lax.broadcasted_iota(jnp.int32, (blk, blk), 1)
        qs = [qT_ref[h * dk:(h + 1) * dk, :] for h in range(HP)]

        def tile(j, carry, masked):
            r0 = pl.multiple_of(j * blk, blk)
            ss = [jnp.dot(k_ref[pl.ds(r0, blk), h * dk:(h + 1) * dk], qs[h], preferred_element_type=F32)
                  for h in range(HP)]
            stats, pes = [], []
            for h in range(HP):
                m, l, _ = carry[h]
                s = jnp.where(key <= qry, ss[h], NEG_INF) if masked else ss[h]
                mn = jnp.maximum(m, jnp.max(s, axis=0, keepdims=True))
                a = jnp.exp(m - mn)
                pe = jnp.exp(s - mn)
                stats.append((mn, a * l + jnp.sum(pe, axis=0, keepdims=True), a))
                pes.append(pe.astype(MXU_DTYPE))
            pvs = [jnp.dot(vT_ref[j, h * HEAD_DIM:(h + 1) * HEAD_DIM, :], pes[h], preferred_element_type=F32)
                   for h in range(HP)]
            return tuple((stats[h][0], stats[h][1], stats[h][2] * carry[h][2] + pvs[h]) for h in range(HP))

        init = tuple((jnp.full((1, blk), NEG_INF, F32), jnp.zeros((1, blk), F32), jnp.zeros((HEAD_DIM, blk), F32))
                     for _ in range(HP))
        carry = lax.fori_loop(0, i, functools.partial(tile, masked=False), init)
        carry = tile(i, carry, True)
        for h in range(HP):
            m, l, acc = carry[h]
            oT_ref[h * HEAD_DIM:(h + 1) * HEAD_DIM, :] = acc / l
            lse_ref[h, 0] = m + jnp.log(l)

    return pl.pallas_call(
        body, name=name, grid=(H // HP, nb),
        in_specs=[pl.BlockSpec((HP * dk, blk), lambda p, i: (p, i)),
                  pl.BlockSpec((S, HP * dk), lambda p, i: (0, p)),
                  pl.BlockSpec((nb, HP * HEAD_DIM, blk), lambda p, i: (0, p, 0))],
        out_specs=[pl.BlockSpec((HP * HEAD_DIM, blk), lambda p, i: (p, i)),
                   pl.BlockSpec((HP, 1, 1, blk), lambda p, i: (p, i, 0, 0))],
        out_shape=[jax.ShapeDtypeStruct((H * HEAD_DIM, S), F32), jax.ShapeDtypeStruct((H, nb, 1, blk), F32)],
        compiler_params=_cparams("arbitrary", "arbitrary"),
    )(qT, k, vT3)


def _smax_bwd_t(qT, q, k, kT3, v, dmix, dmixT, oT, lse, *, dk, dcb, qscale, blk, name):
    S = k.shape[0]
    nb = S // blk
    H = k.shape[1] // dk
    hd = HP * HEAD_DIM
    dcr = dcb * 128 // hd

    def body(qT_ref, q_ref, k_ref, kT_ref, v_ref, do_ref, doT_ref, oT_ref, lse_ref, dqT_ref, dk_ref, dv_ref):
        i = pl.program_id(1)

        @pl.when(i == 0)
        def _():
            dk_ref[...] = jnp.zeros_like(dk_ref)
            dv_ref[...] = jnp.zeros_like(dv_ref)

        key = lax.broadcasted_iota(jnp.int32, (blk, blk), 0)
        qry = lax.broadcasted_iota(jnp.int32, (blk, blk), 1)
        per_head = []
        for h in range(HP):
            hs = slice(h * HEAD_DIM, (h + 1) * HEAD_DIM)
            doT = doT_ref[hs, :]
            per_head.append(dict(
                qT=qT_ref[h * dk:(h + 1) * dk, :], q=q_ref[:, h * dk:(h + 1) * dk],
                doT=doT.astype(MXU_DTYPE), do=do_ref[:, hs].astype(MXU_DTYPE),
                delta=jnp.sum(doT * oT_ref[hs, :], axis=0, keepdims=True), lse=lse_ref[h, 0]))

        def tile(j, dqs, masked):
            r0 = pl.multiple_of(j * blk, blk)
            rows = pl.ds(r0, blk)
            ksl = [slice(h * dk, (h + 1) * dk) for h in range(HP)]
            hsl = [slice(h * HEAD_DIM, (h + 1) * HEAD_DIM) for h in range(HP)]
            ss = [jnp.dot(k_ref[rows, ksl[h]], per_head[h]["qT"], preferred_element_type=F32) for h in range(HP)]
            dps = [jnp.dot(v_ref[rows, hsl[h]], per_head[h]["doT"], preferred_element_type=F32) for h in range(HP)]
            prs, dss = [], []
            for h in range(HP):
                c = per_head[h]
                s = jnp.where(key <= qry, ss[h], NEG_INF) if masked else ss[h]
                pr = jnp.exp(s - c["lse"])
                dss.append((pr * (dps[h] - c["delta"])).astype(MXU_DTYPE))
                prs.append(pr.astype(MXU_DTYPE))
            for h in range(HP):
                dv_ref[rows, hsl[h]] += jnp.dot(prs[h], per_head[h]["do"], preferred_element_type=F32)
            for h in range(HP):
                dk_ref[rows, ksl[h]] += jnp.dot(dss[h], per_head[h]["q"], preferred_element_type=F32)
            return tuple(dqs[h] + jnp.dot(kT_ref[j, ksl[h], :], dss[h], preferred_element_type=F32) for h in range(HP))

        dqs = lax.fori_loop(0, i, functools.partial(tile, masked=False),
                            tuple(jnp.zeros((dk, blk), F32) for _ in range(HP)))
        dqs = tile(i, dqs, True)
        for h in range(HP):
            dqT_ref[h * dk:(h + 1) * dk, :] = dqs[h] * qscale

    return pl.pallas_call(
        body, name=name, grid=(H // HP, nb),
        in_specs=[pl.BlockSpec((HP * dk, blk), lambda p, i: (p, i)),
                  pl.BlockSpec((blk, HP * dk), lambda p, i: (i, p)),
                  pl.BlockSpec((S, HP * dk), lambda p, i: (0, p)),
                  pl.BlockSpec((nb, HP * dk, blk), lambda p, i: (0, p, 0)),
                  pl.BlockSpec((S, hd), lambda p, i: (0, p)),
                  pl.BlockSpec((blk, hd), lambda p, i: (i, dcr + p)),
                  pl.BlockSpec((hd, blk), lambda p, i: (dcr + p, i)),
                  pl.BlockSpec((hd, blk), lambda p, i: (p, i)),
                  pl.BlockSpec((HP, 1, 1, blk), lambda p, i: (p, i, 0, 0))],
        out_specs=[pl.BlockSpec((HP * dk, blk), lambda p, i: (p, i)),
                   pl.BlockSpec((S, HP * dk), lambda p, i: (0, p)),
                   pl.BlockSpec((S, hd), lambda p, i: (0, p))],
        out_shape=[jax.ShapeDtypeStruct((H * dk, S), F32), jax.ShapeDtypeStruct((S, H * dk), F32),
                   jax.ShapeDtypeStruct((S, H * HEAD_DIM), F32)],
        compiler_params=_cparams("arbitrary", "arbitrary"),
    )(qT, q, k, kT3, v, dmix, dmixT, oT, lse)


def _log1m_beta(z):
    return -(jnp.maximum(z, 0.0) + jnp.log(1.0 + jnp.exp(-jnp.abs(z))))


def _dot01_left(m01, x, parts=2):
    acc = None
    rem = x
    for _ in range(parts):
        part = rem.astype(MXU_DTYPE)
        rem = rem - part.astype(F32)
        t = jnp.dot(m01, part, preferred_element_type=F32)
        acc = t if acc is None else acc + t
    return acc


def _sb_fwd_t(qT, h_att, vT3, *, blk, name):
    S = h_att.shape[0]
    nb = S // blk
    kcb = COL_SK // (HP * HEAD_DIM)

    def body(qT_ref, k_ref, vT_ref, oT_ref, lt_ref):
        i = pl.program_id(1)
        key = lax.broadcasted_iota(jnp.int32, (blk, blk), 0)
        qry = lax.broadcasted_iota(jnp.int32, (blk, blk), 1)
        strict = key < qry
        later = (qry > key).astype(MXU_DTYPE)
        qs = [qT_ref[h * HEAD_DIM:(h + 1) * HEAD_DIM, :] for h in range(HP)]

        def tile(j, carry, mask):
            r0 = pl.multiple_of(j * blk, blk)
            hsl = [slice(h * HEAD_DIM, (h + 1) * HEAD_DIM) for h in range(HP)]
            zs = [jnp.dot(k_ref[pl.ds(r0, blk), hsl[h]], qs[h], preferred_element_type=F32) for h in range(HP)]
            lbs = []
            for h in range(HP):
                lb = _log1m_beta(zs[h])
                lbs.append(lb if mask is None else jnp.where(mask, lb, 0.0))
            sums = [_dot01_left(later, lbs[h]) for h in range(HP)]
            probs = []
            for h in range(HP):
                a = jnp.exp(zs[h] + lbs[h] + sums[h] + carry[h][0])
                probs.append((a if mask is None else jnp.where(mask, a, 0.0)).astype(MXU_DTYPE))
            pvs = [jnp.dot(vT_ref[j, hsl[h], :], probs[h], preferred_element_type=F32) for h in range(HP)]
            return tuple((carry[h][0] + jnp.sum(lbs[h], axis=0, keepdims=True), carry[h][1] + pvs[h]) for h in range(HP))

        init = tuple((jnp.zeros((1, blk), F32), jnp.zeros((HEAD_DIM, blk), F32)) for _ in range(HP))
        carry = tile(i, init, strict)
        carry = lax.fori_loop(0, i, lambda jj, c: tile(i - 1 - jj, c, None), carry)
        for h in range(HP):
            oT_ref[h * HEAD_DIM:(h + 1) * HEAD_DIM, :] = carry[h][1]
            lt_ref[h, 0] = carry[h][0]

    hd = HP * HEAD_DIM
    return pl.pallas_call(
        body, name=name, grid=(4 // HP, nb),
        in_specs=[pl.BlockSpec((hd, blk), lambda p, i: (p, i)),
                  pl.BlockSpec((S, hd), lambda p, i: (0, kcb + p)),
                  pl.BlockSpec((nb, hd, blk), lambda p, i: (0, p, 0))],
        out_specs=[pl.BlockSpec((hd, blk), lambda p, i: (p, i)), pl.BlockSpec((HP, 1, 1, blk), lambda p, i: (p, i, 0, 0))],
        out_shape=[jax.ShapeDtypeStruct((GROUP_WIDTH, S), F32), jax.ShapeDtypeStruct((4, nb, 1, blk), F32)],
        compiler_params=_cparams("arbitrary", "arbitrary"),
    )(qT, h_att, vT3)


def _sb_bwd_t(qT, h_att, kT3, dmix, dmixT, ltot, *, dcb, qscale, blk, name):
    S = h_att.shape[0]
    nb = S // blk
    hd = HP * HEAD_DIM
    qcb, kcb, vcb = COL_SQ // hd, COL_SK // hd, COL_SV // hd
    dcr = dcb * 128 // hd

    def body(qT_ref, q_ref, k_ref, kT_ref, v_ref, do_ref, doT_ref, lt_ref, dqT_ref, dk_ref, dv_ref):
        i = pl.program_id(1)

        @pl.when(i == 0)
        def _():
            dk_ref[...] = jnp.zeros_like(dk_ref)
            dv_ref[...] = jnp.zeros_like(dv_ref)

        key = lax.broadcasted_iota(jnp.int32, (blk, blk), 0)
        qry = lax.broadcasted_iota(jnp.int32, (blk, blk), 1)
        strict = key < qry
        upto = (qry <= key).astype(MXU_DTYPE)
        before = (qry < key).astype(MXU_DTYPE)
        per_head = []
        for h in range(HP):
            hs = slice(h * HEAD_DIM, (h + 1) * HEAD_DIM)
            per_head.append(dict(qT=qT_ref[hs, :], q=q_ref[:, hs], doT=doT_ref[hs, :].astype(MXU_DTYPE),
                                 do=do_ref[:, hs].astype(MXU_DTYPE), lt=lt_ref[h, 0]))

        def tile(j, carry, mask):
            r0 = pl.multiple_of(j * blk, blk)
            rows = pl.ds(r0, blk)
            hsl = [slice(h * HEAD_DIM, (h + 1) * HEAD_DIM) for h in range(HP)]
            zs = [jnp.dot(k_ref[rows, hsl[h]], per_head[h]["qT"], preferred_element_type=F32) for h in range(HP)]
            das = [jnp.dot(v_ref[rows, hsl[h]], per_head[h]["doT"], preferred_element_type=F32) for h in range(HP)]
            lbs = []
            for h in range(HP):
                lb = _log1m_beta(zs[h])
                lbs.append(lb if mask is None else jnp.where(mask, lb, 0.0))
            sums = [_dot01_left(upto, lbs[h]) for h in range(HP)]
            probs, gs = [], []
            for h in range(HP):
                a = jnp.exp(zs[h] + lbs[h] + (per_head[h]["lt"] - carry[h][0] - sums[h]))
                a = a if mask is None else jnp.where(mask, a, 0.0)
                gs.append(das[h] * a)
                probs.append(a.astype(MXU_DTYPE))
            for h in range(HP):
                dv_ref[rows, hsl[h]] += jnp.dot(probs[h], per_head[h]["do"], preferred_element_type=F32)
            es = [_dot01_left(before, gs[h]) for h in range(HP)]
            dzs = []
            for h in range(HP):
                dz = gs[h] * jnp.exp(lbs[h]) - (carry[h][1] + es[h]) * jnp.exp(zs[h] + lbs[h])
                dzs.append((dz if mask is None else jnp.where(mask, dz, 0.0)).astype(MXU_DTYPE))
            for h in range(HP):
                dk_ref[rows, hsl[h]] += jnp.dot(dzs[h], per_head[h]["q"], preferred_element_type=F32)
            return tuple((carry[h][0] + jnp.sum(lbs[h], axis=0, keepdims=True),
                          carry[h][1] + jnp.sum(gs[h], axis=0, keepdims=True),
                          carry[h][2] + jnp.dot(kT_ref[j, hsl[h], :], dzs[h], preferred_element_type=F32))
                         for h in range(HP))

        zr = jnp.zeros((1, blk), F32)
        init = tuple((zr, zr, jnp.zeros((HEAD_DIM, blk), F32)) for _ in range(HP))
        carry = lax.fori_loop(0, i, lambda j, c: tile(j, c, None), init)
        carry = tile(i, carry, strict)
        for h in range(HP):
            dqT_ref[h * HEAD_DIM:(h + 1) * HEAD_DIM, :] = carry[h][2] * qscale

    return pl.pallas_call(
        body, name=name, grid=(4 // HP, nb),
        in_specs=[pl.BlockSpec((hd, blk), lambda p, i: (p, i)),
                  pl.BlockSpec((blk, hd), lambda p, i: (i, qcb + p)),
                  pl.BlockSpec((S, hd), lambda p, i: (0, kcb + p)),
                  pl.BlockSpec((nb, hd, blk), lambda p, i: (0, p, 0)),
                  pl.BlockSpec((S, hd), lambda p, i: (0, vcb + p)),
                  pl.BlockSpec((blk, hd), lambda p, i: (i, dcr + p)),
                  pl.BlockSpec((hd, blk), lambda p, i: (dcr + p, i)),
                  pl.BlockSpec((HP, 1, 1, blk), lambda p, i: (p, i, 0, 0))],
        out_specs=[pl.BlockSpec((hd, blk), lambda p, i: (p, i)),
                   pl.BlockSpec((S, hd), lambda p, i: (0, p)),
                   pl.BlockSpec((S, hd), lambda p, i: (0, p))],
        out_shape=[jax.ShapeDtypeStruct((GROUP_WIDTH, S), F32), jax.ShapeDtypeStruct((S, GROUP_WIDTH), F32),
                   jax.ShapeDtypeStruct((S, GROUP_WIDTH), F32)],
        compiler_params=_cparams("arbitrary", "arbitrary"),
    )(qT, h_att, h_att, kT3, h_att, dmix, dmixT, ltot)


def _swa_scores(q_ref, k_ref, n, h, start):
    g = h // 2
    kb = k_ref[pl.ds(start, 2 * WINDOW), g * HEAD_DIM:(g + 1) * HEAD_DIM]
    s = lax.dot_general(q_ref[:, h * HEAD_DIM:(h + 1) * HEAD_DIM], kb, NT, preferred_element_type=F32) * (HEAD_DIM ** -0.5)
    dist = (n * WINDOW + lax.broadcasted_iota(jnp.int32, (WINDOW, 2 * WINDOW), 0)
            - start - lax.broadcasted_iota(jnp.int32, (WINDOW, 2 * WINDOW), 1))
    s = s - SWA_SLOPES[h] * dist.astype(F32)
    valid = (dist >= 0) & (dist < WINDOW)
    return jnp.where(valid, s, NEG_INF), kb


def _swa_fwd(h_att, sinks, *, name):
    S = h_att.shape[0]
    nb = S // WINDOW
    qcb, kcb, vcb = COL_WQ // 256, COL_WK // 128, COL_WV // 128

    def body(sink_ref, q_ref, k_ref, v_ref, o_ref, lse_ref):
        n = pl.program_id(0)
        start = pl.multiple_of(jnp.maximum(n - 1, 0) * WINDOW, WINDOW)
        scores = [_swa_scores(q_ref, k_ref, n, h, start)[0] for h in range(4)]
        probs = []
        for h in range(4):
            sink = sink_ref[h]
            m = jnp.maximum(jnp.max(scores[h], axis=1, keepdims=True), sink)
            e = jnp.exp(scores[h] - m)
            den = jnp.sum(e, axis=1, keepdims=True) + jnp.exp(sink - m)
            probs.append((e / den).astype(MXU_DTYPE))
            lse_ref[h] = m + jnp.log(den)
        for h in range(4):
            vb = v_ref[pl.ds(start, 2 * WINDOW), (h // 2) * HEAD_DIM:(h // 2 + 1) * HEAD_DIM]
            o_ref[:, h * HEAD_DIM:(h + 1) * HEAD_DIM] = jnp.dot(probs[h], vb, preferred_element_type=F32)

    return pl.pallas_call(
        body, name=name, grid=(nb,),
        in_specs=[pl.BlockSpec(memory_space=pltpu.SMEM),
                  pl.BlockSpec((WINDOW, 256), lambda n: (n, qcb)),
                  pl.BlockSpec((S, 128), lambda n: (0, kcb)),
                  pl.BlockSpec((S, 128), lambda n: (0, vcb))],
        out_specs=[pl.BlockSpec((WINDOW, 256), lambda n: (n, 0)), pl.BlockSpec((4, WINDOW, 1), lambda n: (0, n, 0))],
        out_shape=[jax.ShapeDtypeStruct((S, GROUP_WIDTH), F32), jax.ShapeDtypeStruct((4, S, 1), F32)],
        compiler_params=_cparams("arbitrary"),
    )(sinks, h_att, h_att, h_att)


def _swa_bwd(h_att, sinks, dmix, o_arr, lse, *, dcb, name):
    S = h_att.shape[0]
    nb = S // WINDOW
    qcb, kcb, vcb = COL_WQ // 256, COL_WK // 128, COL_WV // 128

    def body(sink_ref, q_ref, k_ref, v_ref, do_ref, o_ref, lse_ref, dq_ref, dk_ref, dv_ref, dsink_ref):
        n = pl.program_id(0)

        @pl.when(n == 0)
        def _():
            dk_ref[...] = jnp.zeros_like(dk_ref)
            dv_ref[...] = jnp.zeros_like(dv_ref)
            dsink_ref[...] = jnp.zeros_like(dsink_ref)

        start = pl.multiple_of(jnp.maximum(n - 1, 0) * WINDOW, WINDOW)
        rows = pl.ds(start, 2 * WINDOW)
        hsl = [slice(h * HEAD_DIM, (h + 1) * HEAD_DIM) for h in range(4)]
        gsl = [slice(g * HEAD_DIM, (g + 1) * HEAD_DIM) for g in range(2)]
        scale = HEAD_DIM ** -0.5
        sk = [_swa_scores(q_ref, k_ref, n, h, start) for h in range(4)]
        dobs = [do_ref[:, hsl[h]].astype(MXU_DTYPE) for h in range(4)]
        dps = [lax.dot_general(dobs[h], v_ref[rows, gsl[h // 2]], NT, preferred_element_type=F32) for h in range(4)]
        prs, dss = [], []
        for h in range(4):
            lse_h = lse_ref[h]
            pr = jnp.exp(sk[h][0] - lse_h)
            delta = jnp.sum(do_ref[:, hsl[h]] * o_ref[:, hsl[h]], axis=1, keepdims=True)
            dss.append((pr * (dps[h] - delta)).astype(MXU_DTYPE))
            prs.append(pr.astype(MXU_DTYPE))
            dsink_ref[h:h + 1, :] += jnp.zeros((1, 128), F32) - jnp.sum(jnp.exp(sink_ref[h] - lse_h) * delta)
        for h in range(4):
            dq_ref[:, hsl[h]] = jnp.dot(dss[h], sk[h][1], preferred_element_type=F32) * scale
        for g in range(2):
            dk_ref[rows, gsl[g]] += (lax.dot_general(dss[2 * g], q_ref[:, hsl[2 * g]], TN, preferred_element_type=F32)
                                     + lax.dot_general(dss[2 * g + 1], q_ref[:, hsl[2 * g + 1]], TN,
                                                       preferred_element_type=F32)) * scale
            dv_ref[rows, gsl[g]] += (lax.dot_general(prs[2 * g], dobs[2 * g], TN, preferred_element_type=F32)
                                     + lax.dot_general(prs[2 * g + 1], dobs[2 * g + 1], TN, preferred_element_type=F32))

    return pl.pallas_call(
        body, name=name, grid=(nb,),
        in_specs=[pl.BlockSpec(memory_space=pltpu.SMEM),
                  pl.BlockSpec((WINDOW, 256), lambda n: (n, qcb)),
                  pl.BlockSpec((S, 128), lambda n: (0, kcb)),
                  pl.BlockSpec((S, 128), lambda n: (0, vcb)),
                  pl.BlockSpec((WINDOW, 256), lambda n: (n, dcb)),
                  pl.BlockSpec((WINDOW, 256), lambda n: (n, 0)),
                  pl.BlockSpec((4, WINDOW, 1), lambda n: (0, n, 0))],
        out_specs=[pl.BlockSpec((WINDOW, 256), lambda n: (n, 0)),
                   pl.BlockSpec((S, 128), lambda n: (0, 0)),
                   pl.BlockSpec((S, 128), lambda n: (0, 0)),
                   pl.BlockSpec((4, 128), lambda n: (0, 0))],
        out_shape=[jax.ShapeDtypeStruct((S, GROUP_WIDTH), F32), jax.ShapeDtypeStruct((S, 128), F32),
                   jax.ShapeDtypeStruct((S, 128), F32), jax.ShapeDtypeStruct((4, 128), F32)],
        compiler_params=_cparams("arbitrary"),
    )(sinks, h_att, h_att, h_att, dmix, o_arr, lse)


def _tri(n, incl, upper):
    r = lax.broadcasted_iota(jnp.int32, (n, n), 0)
    c = lax.broadcasted_iota(jnp.int32, (n, n), 1)
    if upper:
        m = (r <= c) if incl else (r < c)
    else:
        m = (r >= c) if incl else (r > c)
    return m.astype(MXU_DTYPE)


def _fox_gate_fwd(fg, b_f, *, name):
    _, R, _ = fg.shape

    def body(b_ref, fg_ref, pos_ref, neg_ref):
        up_incl = _tri(128, True, True)
        ones = jnp.ones((128, 128), MXU_DTYPE)
        for h in range(4):
            z = fg_ref[h] + b_ref[h]
            logf = jnp.minimum(z, 0.0) - jnp.log(1.0 + jnp.exp(-jnp.abs(z)))
            within = _dot01(logf, up_incl, parts=3)
            totals = _dot01(logf, ones, parts=3)
            rem = within + _rows_other(totals, R, after=False)
            for part in range(3):
                piece = rem.astype(MXU_DTYPE)
                rem = rem - piece.astype(F32)
                pos_ref[h, part] = piece
                neg_ref[h, part] = -piece

    shape = (4, 3) + fg.shape[1:]
    return pl.pallas_call(
        body, name=name,
        in_specs=[pl.BlockSpec(memory_space=pltpu.SMEM), pl.BlockSpec(memory_space=pltpu.VMEM)],
        out_specs=[pl.BlockSpec(memory_space=pltpu.VMEM)] * 2,
        out_shape=[jax.ShapeDtypeStruct(shape, MXU_DTYPE)] * 2,
    )(b_f, fg)


def _rows_other(totals, n, after):
    r = lax.broadcasted_iota(jnp.int32, (n, n), 0)
    c = lax.broadcasted_iota(jnp.int32, (n, n), 1)
    m = ((c > r) if after else (c < r)).astype(MXU_DTYPE)
    acc = None
    rem = totals
    for _ in range(3):
        part = rem.astype(MXU_DTYPE)
        rem = rem - part.astype(F32)
        t = jnp.dot(m, part, preferred_element_type=F32)
        acc = t if acc is None else acc + t
    return acc


def _fox_gate_bwd(fg, b_f, dcum_k, dcum_q, *, q_unscale, name):
    _, R, _ = fg.shape

    def body(b_ref, fg_ref, dck_ref, dcq_ref, dfg_ref, db_ref):
        low_incl = _tri(128, True, False)
        ones = jnp.ones((128, 128), MXU_DTYPE)
        for h in range(4):
            dc = dcq_ref[h] * q_unscale - dck_ref[h]
            dlogf = _dot01(dc, low_incl, parts=3) + _rows_other(_dot01(dc, ones, parts=3), R, after=True)
            z = fg_ref[h] + b_ref[h]
            dz = dlogf * jnp.exp(jnp.minimum(-z, 0.0) - jnp.log(1.0 + jnp.exp(-jnp.abs(z))))
            dfg_ref[h] = dz
            db_ref[h:h + 1, :] = jnp.zeros((1, 128), F32) + jnp.sum(dz)

    return pl.pallas_call(
        body, name=name,
        in_specs=[pl.BlockSpec(memory_space=pltpu.SMEM)] + [pl.BlockSpec(memory_space=pltpu.VMEM)] * 3,
        out_specs=[pl.BlockSpec(memory_space=pltpu.VMEM), pl.BlockSpec(memory_space=pltpu.VMEM)],
        out_shape=[jax.ShapeDtypeStruct(fg.shape, F32), jax.ShapeDtypeStruct((4, 128), F32)],
    )(b_f, fg, dcum_k, dcum_q)


def _rope_rot(transpose):
    r = lax.broadcasted_iota(jnp.int32, (MLA_PAD, MLA_PAD), 0)
    c = lax.broadcasted_iota(jnp.int32, (MLA_PAD, MLA_PAD), 1)
    if transpose:
        r, c = c, r
    half = MLA_ROPE // 2
    lo, mid, hi = HEAD_DIM, HEAD_DIM + half, HEAD_DIM + MLA_ROPE
    minus = (c >= lo) & (c < mid) & (r == c + half)
    plus = (c >= mid) & (c < hi) & (r == c - half)
    return jnp.where(plus, 1.0, jnp.where(minus, -1.0, 0.0)).astype(MXU_DTYPE)


def _rope_lanes():
    lane = lax.broadcasted_iota(jnp.int32, (1, MLA_PAD), 1)
    return ((lane >= HEAD_DIM) & (lane < HEAD_DIM + MLA_ROPE)).astype(F32)


def _rms(x, g, eps=1e-6):
    r = lax.rsqrt(jnp.mean(x * x, axis=-1, keepdims=True) + eps)
    return x * r * g, r


def _rms_bwd(dy, x, r, g):
    xh = x * r
    dxh = dy * g
    dx = r * (dxh - xh * jnp.mean(dxh * xh, axis=-1, keepdims=True))
    return dx, dy * xh


def _mla_prep_fwd(lat, g_q, g_kv, wuq, wuk, wuv, cosm, sinm, *, bs, name):
    S = lat.shape[0]

    def body(lat_ref, gq_ref, gkv_ref, wuq_ref, wuk_ref, wuv_ref, cos_ref, sin_ref, q_ref, k_ref, v_ref):
        rot = _rope_rot(False)
        cosm_, sinm_ = cos_ref[...], sin_ref[...]
        nq, _ = _rms(lat_ref[:, 0:MLA_Q_RANK], gq_ref[...])
        nkv, _ = _rms(lat_ref[:, MLA_Q_RANK:MLA_Q_RANK + MLA_KV_RANK], gkv_ref[...])
        qlat = jnp.dot(nq.astype(MXU_DTYPE), wuq_ref[...], preferred_element_type=F32)
        klat = jnp.dot(nkv.astype(MXU_DTYPE), wuk_ref[...], preferred_element_type=F32)
        v_ref[...] = jnp.dot(nkv.astype(MXU_DTYPE), wuv_ref[...], preferred_element_type=F32).astype(v_ref.dtype)
        krb = lat_ref[:, 384:512]
        kr = krb * (cosm_ * _rope_lanes()) + _dot01(krb, rot, parts=3) * sinm_
        for h in range(4):
            sl = slice(h * MLA_PAD, (h + 1) * MLA_PAD)
            qh = qlat[:, sl]
            q_ref[:, sl] = ((qh * cosm_ + _dot01(qh, rot, parts=3) * sinm_) * (MLA_QK ** -0.5)).astype(q_ref.dtype)
            k_ref[:, sl] = (klat[:, sl] + kr).astype(k_ref.dtype)

    full = lambda a: pl.BlockSpec(a.shape, lambda i: (0,) * a.ndim)
    return pl.pallas_call(
        body, name=name, grid=(S // bs,),
        in_specs=[pl.BlockSpec((bs, LAT_W), lambda i: (i, 0)), full(g_q), full(g_kv), full(wuq), full(wuk), full(wuv),
                  pl.BlockSpec((bs, MLA_PAD), lambda i: (i, 0)), pl.BlockSpec((bs, MLA_PAD), lambda i: (i, 0))],
        out_specs=[pl.BlockSpec((bs, 512), lambda i: (i, 0)), pl.BlockSpec((bs, 512), lambda i: (i, 0)),
                   pl.BlockSpec((bs, 256), lambda i: (i, 0))],
        out_shape=[jax.ShapeDtypeStruct((S, 512), MXU_DTYPE), jax.ShapeDtypeStruct((S, 512), MXU_DTYPE),
                   jax.ShapeDtypeStruct((S, 256), MXU_DTYPE)],
        compiler_params=_cparams("parallel"),
    )(lat, g_q, g_kv, wuq, wuk, wuv, cosm, sinm)


def _mla_prep_bwd(lat, g_q, g_kv, wuq, wuk, wuv, cosm, sinm, dq, dk, dv, *, bs, name):
    S = lat.shape[0]

    def body(lat_ref, gq_ref, gkv_ref, wuq_ref, wuk_ref, wuv_ref, cos_ref, sin_ref, dq_ref, dk_ref, dv_ref,
             dlat_ref, dwuq_ref, dwuk_ref, dwuv_ref, dgq_ref, dgkv_ref):
        @pl.when(pl.program_id(0) == 0)
        def _():
            for r in (dwuq_ref, dwuk_ref, dwuv_ref, dgq_ref, dgkv_ref):
                r[...] = jnp.zeros_like(r)

        rot_t = _rope_rot(True)
        cosm_, sinm_ = cos_ref[...], sin_ref[...]
        cq = lat_ref[:, 0:MLA_Q_RANK]
        ckv = lat_ref[:, MLA_Q_RANK:MLA_Q_RANK + MLA_KV_RANK]
        nq, rq = _rms(cq, gq_ref[...])
        nkv, rkv = _rms(ckv, gkv_ref[...])
        nqb, nkvb = nq.astype(MXU_DTYPE), nkv.astype(MXU_DTYPE)

        dqlat = []
        dkr = jnp.zeros((bs, MLA_PAD), F32)
        for h in range(4):
            sl = slice(h * MLA_PAD, (h + 1) * MLA_PAD)
            dqh = dq_ref[:, sl]
            dqlat.append(dqh * cosm_ + _dot01(dqh * sinm_, rot_t, parts=3))
            dkr = dkr + dk_ref[:, sl]
        dqlat = jnp.concatenate(dqlat, axis=1).astype(MXU_DTYPE)
        dkb = dk_ref[...].astype(MXU_DTYPE)
        dvb = dv_ref[...].astype(MXU_DTYPE)

        dnq = lax.dot_general(dqlat, wuq_ref[...], NT, preferred_element_type=F32)
        dnkv = (lax.dot_general(dkb, wuk_ref[...], NT, preferred_element_type=F32)
                + lax.dot_general(dvb, wuv_ref[...], NT, preferred_element_type=F32))
        dwuq_ref[...] += lax.dot_general(nqb, dqlat, TN, preferred_element_type=F32)
        dwuk_ref[...] += lax.dot_general(nkvb, dkb, TN, preferred_element_type=F32)
        dwuv_ref[...] += lax.dot_general(nkvb, dvb, TN, preferred_element_type=F32)
        dcq, tq = _rms_bwd(dnq, cq, rq, gq_ref[...])
        dckv, tkv = _rms_bwd(dnkv, ckv, rkv, gkv_ref[...])
        dgq_ref[...] += jnp.sum(tq, axis=0, keepdims=True)
        dgkv_ref[...] += jnp.sum(tkv, axis=0, keepdims=True)
        dlat_ref[:, 0:MLA_Q_RANK] = dcq.astype(dlat_ref.dtype)
        dlat_ref[:, MLA_Q_RANK:MLA_Q_RANK + MLA_KV_RANK] = dckv.astype(dlat_ref.dtype)
        dkrb = dkr * (cosm_ * _rope_lanes()) + _dot01(dkr * sinm_, rot_t, parts=3)
        dlat_ref[:, 384:512] = dkrb.astype(dlat_ref.dtype)

    full = lambda a: pl.BlockSpec(a.shape, lambda i: (0,) * a.ndim)
    row = lambda w: pl.BlockSpec((bs, w), lambda i: (i, 0))
    acc = lambda *shape: pl.BlockSpec(shape, lambda i: (0,) * len(shape))
    return pl.pallas_call(
        body, name=name, grid=(S // bs,),
        in_specs=[row(LAT_W), full(g_q), full(g_kv), full(wuq), full(wuk), full(wuv), row(MLA_PAD), row(MLA_PAD),
                  row(512), row(512), row(256)],
        out_specs=[row(512), acc(256, 512), acc(128, 512), acc(128, 256), acc(1, 256), acc(1, 128)],
        out_shape=[jax.ShapeDtypeStruct((S, 512), MXU_DTYPE), jax.ShapeDtypeStruct((256, 512), F32),
                   jax.ShapeDtypeStruct((128, 512), F32), jax.ShapeDtypeStruct((128, 256), F32),
                   jax.ShapeDtypeStruct((1, 256), F32), jax.ShapeDtypeStruct((1, 128), F32)],
        compiler_params=_cparams("arbitrary"),
    )(lat, g_q, g_kv, wuq, wuk, wuv, cosm, sinm, dq, dk, dv)


def _row_spec(bs, w):
    return pl.BlockSpec((bs, w), lambda i: (i, 0))


def _vec_spec(w):
    return pl.BlockSpec((1, w), lambda i: (0, 0))


def _mix_specs(bs):
    return [pl.BlockSpec((GROUP_WIDTH, bs), lambda i: (0, i))] * 3 + [_row_spec(bs, GROUP_WIDTH)]


def _mix_groups(a_ref, b_ref, c_ref, d_ref):
    return [a_ref[...].T, b_ref[...].T, c_ref[...].T, d_ref[...]]


def _gnorm_fwd(outs, g, *, bs, name):
    S = outs[3].shape[0]

    def body(a_ref, b_ref, c_ref, d_ref, g_ref, o_ref, oT_ref):
        for k, x in enumerate(_mix_groups(a_ref, b_ref, c_ref, d_ref)):
            sl = slice(k * GROUP_WIDTH, (k + 1) * GROUP_WIDTH)
            y, _ = _rms(x, g_ref[:, sl])
            o_ref[:, sl] = y.astype(o_ref.dtype)
            oT_ref[sl, :] = y.T.astype(oT_ref.dtype)

    return pl.pallas_call(
        body, name=name, grid=(S // bs,),
        in_specs=_mix_specs(bs) + [_vec_spec(D_MODEL)],
        out_specs=[_row_spec(bs, D_MODEL), pl.BlockSpec((D_MODEL, bs), lambda i: (0, i))],
        out_shape=[jax.ShapeDtypeStruct((S, D_MODEL), MXU_DTYPE), jax.ShapeDtypeStruct((D_MODEL, S), MXU_DTYPE)],
        compiler_params=_cparams("parallel"),
    )(*outs, g)


def _gnorm_bwd(dgn, outs, g, *, bs, name):
    S = dgn.shape[0]

    def body(dgn_ref, a_ref, b_ref, c_ref, d_ref, g_ref, dmix_ref, dmixT_ref, dg_ref):
        @pl.when(pl.program_id(0) == 0)
        def _():
            dg_ref[...] = jnp.zeros_like(dg_ref)

        for k, x in enumerate(_mix_groups(a_ref, b_ref, c_ref, d_ref)):
            sl = slice(k * GROUP_WIDTH, (k + 1) * GROUP_WIDTH)
            _, r = _rms(x, g_ref[:, sl])
            dx, t = _rms_bwd(dgn_ref[:, sl], x, r, g_ref[:, sl])
            dmix_ref[:, sl] = dx
            dmixT_ref[sl, :] = dx.T
            dg_ref[:, sl] += jnp.sum(t, axis=0, keepdims=True)

    return pl.pallas_call(
        body, name=name, grid=(S // bs,),
        in_specs=[_row_spec(bs, D_MODEL)] + _mix_specs(bs) + [_vec_spec(D_MODEL)],
        out_specs=[_row_spec(bs, D_MODEL), pl.BlockSpec((D_MODEL, bs), lambda i: (0, i)), _vec_spec(D_MODEL)],
        out_shape=[jax.ShapeDtypeStruct((S, D_MODEL), F32), jax.ShapeDtypeStruct((D_MODEL, S), F32),
                   jax.ShapeDtypeStruct((1, D_MODEL), F32)],
        compiler_params=_cparams("arbitrary"),
    )(dgn, *outs, g)


def _ln_fwd(u, g, b, *, bs, name):
    S = u.shape[0]

    def body(u_ref, g_ref, b_ref, y_ref, yb_ref, ybT_ref, xh_ref, rs_ref):
        x = u_ref[...]
        mu = jnp.mean(x, axis=-1, keepdims=True)
        xc = x - mu
        rs = lax.rsqrt(jnp.mean(xc * xc, axis=-1, keepdims=True) + 1e-5)
        xh = xc * rs
        y = xh * g_ref[...] + b_ref[...]
        y_ref[...] = y
        yb_ref[...] = y.astype(yb_ref.dtype)
        ybT_ref[...] = y.T.astype(ybT_ref.dtype)
        xh_ref[...] = xh
        rs_ref[...] = rs

    return pl.pallas_call(
        body, name=name, grid=(S // bs,),
        in_specs=[_row_spec(bs, D_MODEL), _vec_spec(D_MODEL), _vec_spec(D_MODEL)],
        out_specs=[_row_spec(bs, D_MODEL), _row_spec(bs, D_MODEL), pl.BlockSpec((D_MODEL, bs), lambda i: (0, i)),
                   _row_spec(bs, D_MODEL), _row_spec(bs, 1)],
        out_shape=[jax.ShapeDtypeStruct((S, D_MODEL), F32), jax.ShapeDtypeStruct((S, D_MODEL), MXU_DTYPE),
                   jax.ShapeDtypeStruct((D_MODEL, S), MXU_DTYPE), jax.ShapeDtypeStruct((S, D_MODEL), F32),
                   jax.ShapeDtypeStruct((S, 1), F32)],
        compiler_params=_cparams("parallel"),
    )(u, g, b)


def _ln_bwd(dy, xh, rs, g, *, bs, name):
    S = dy.shape[0]

    def body(dy_ref, xh_ref, rs_ref, g_ref, du_ref, dub_ref, dg_ref, db_ref):
        @pl.when(pl.program_id(0) == 0)
        def _():
            dg_ref[...] = jnp.zeros_like(dg_ref)
            db_ref[...] = jnp.zeros_like(db_ref)

        dy_, xh_ = dy_ref[...], xh_ref[...]
        dxh = dy_ * g_ref[...]
        du = rs_ref[...] * (dxh - jnp.mean(dxh, axis=-1, keepdims=True)
                            - xh_ * jnp.mean(dxh * xh_, axis=-1, keepdims=True))
        du_ref[...] = du
        dub_ref[...] = du.astype(dub_ref.dtype)
        dg_ref[...] += jnp.sum(dy_ * xh_, axis=0, keepdims=True)
        db_ref[...] += jnp.sum(dy_, axis=0, keepdims=True)

    return pl.pallas_call(
        body, name=name, grid=(S // bs,),
        in_specs=[_row_spec(bs, D_MODEL), _row_spec(bs, D_MODEL), _row_spec(bs, 1), _vec_spec(D_MODEL)],
        out_specs=[_row_spec(bs, D_MODEL), _row_spec(bs, D_MODEL), _vec_spec(D_MODEL), _vec_spec(D_MODEL)],
        out_shape=[jax.ShapeDtypeStruct((S, D_MODEL), F32), jax.ShapeDtypeStruct((S, D_MODEL), MXU_DTYPE),
                   jax.ShapeDtypeStruct((1, D_MODEL), F32), jax.ShapeDtypeStruct((1, D_MODEL), F32)],
        compiler_params=_cparams("arbitrary"),
    )(dy, xh, rs, g)


def _swiglu_fwd(gu, *, bs, name):
    S = gu.shape[0]

    def body(gu_ref, a_ref):
        gt = gu_ref[:, :D_FF]
        a_ref[...] = (gt / (1.0 + jnp.exp(-gt)) * gu_ref[:, D_FF:]).astype(a_ref.dtype)

    return pl.pallas_call(
        body, name=name, grid=(S // bs,),
        in_specs=[_row_spec(bs, 2 * D_FF)],
        out_specs=_row_spec(bs, D_FF), out_shape=jax.ShapeDtypeStruct((S, D_FF), MXU_DTYPE),
        compiler_params=_cparams("parallel"),
    )(gu)


def _swiglu_bwd(da, gu, *, bs, name):
    S = gu.shape[0]

    def body(da_ref, gu_ref, dgu_ref):
        gt, da_ = gu_ref[:, :D_FF], da_ref[...]
        sg = 1.0 / (1.0 + jnp.exp(-gt))
        silu = gt * sg
        dgu_ref[:, :D_FF] = (da_ * gu_ref[:, D_FF:] * (sg + silu * (1.0 - sg))).astype(dgu_ref.dtype)
        dgu_ref[:, D_FF:] = (da_ * silu).astype(dgu_ref.dtype)

    return pl.pallas_call(
        body, name=name, grid=(S // bs,),
        in_specs=[_row_spec(bs, D_FF), _row_spec(bs, 2 * D_FF)],
        out_specs=_row_spec(bs, 2 * D_FF), out_shape=jax.ShapeDtypeStruct((S, 2 * D_FF), MXU_DTYPE),
        compiler_params=_cparams("parallel"),
    )(da, gu)


def _loss_head(y, target, *, bs, name):
    S = y.shape[0]

    def body(y_ref, t_ref, dy_ref, loss_ref):
        @pl.when(pl.program_id(0) == 0)
        def _():
            loss_ref[...] = jnp.zeros_like(loss_ref)

        e = y_ref[...] - t_ref[...]
        dy_ref[...] = e * (1.0 / D_MODEL)
        per_tok = jnp.mean(e * e, axis=-1, keepdims=True)
        loss_ref[...] += 0.5 * jnp.sum(per_tok, axis=0, keepdims=True)

    return pl.pallas_call(
        body, name=name, grid=(S // bs,),
        in_specs=[_row_spec(bs, D_MODEL), _row_spec(bs, D_MODEL)],
        out_specs=[_row_spec(bs, D_MODEL), pl.BlockSpec((1, 1), lambda i: (0, 0))],
        out_shape=[jax.ShapeDtypeStruct((S, D_MODEL), F32), jax.ShapeDtypeStruct((1, 1), F32)],
        compiler_params=_cparams("arbitrary"),
    )(y, target)


def _blk(n, target):
    if n <= target:
        return n
    best = None
    for b in range(128, target + 1, 128):
        if n % b == 0:
            best = b
    assert best is not None, n
    return best


def _rope_tables(S):
    pos = jnp.arange(S, dtype=F32)
    inv = ROPE_THETA ** (-jnp.arange(0, MLA_ROPE, 2, dtype=F32) / MLA_ROPE)
    ang = pos[:, None] * inv[None, :]
    cos, sin = jnp.cos(ang), jnp.sin(ang)
    one, zero, pad = jnp.ones((S, HEAD_DIM), F32), jnp.zeros((S, HEAD_DIM), F32), jnp.zeros((S, MLA_PAD - MLA_QK), F32)
    return jnp.concatenate([one, cos, cos, pad], axis=1), jnp.concatenate([zero, sin, sin, pad], axis=1)


def _prep_weights_a(w_in, w_uq, w_ukv):
    z = lambda n: jnp.zeros((D_MODEL, n), w_in.dtype)
    win_a = jnp.concatenate([w_in[:, 0:768], w_in[:, 1188:2468]], axis=1)
    win_l = jnp.concatenate([w_in[:, 772:1156], z(64), w_in[:, 1156:1188], z(32), w_in[:, 768:772], z(124)], axis=1)
    kv = w_ukv.reshape(MLA_KV_RANK, 4, 2 * HEAD_DIM)
    return dict(
        win_a=win_a, win_l=win_l, win_p=jnp.concatenate([win_a, win_l], axis=1),
        wuq=jnp.pad(w_uq.reshape(MLA_Q_RANK, 4, MLA_QK), ((0, 0), (0, 0), (0, MLA_PAD - MLA_QK))).reshape(MLA_Q_RANK, 512),
        wuk=jnp.pad(kv[:, :, :HEAD_DIM], ((0, 0), (0, 0), (0, HEAD_DIM))).reshape(MLA_KV_RANK, 512),
        wuv=kv[:, :, HEAD_DIM:].reshape(MLA_KV_RANK, 256))


def _prep_weights_b(w_o, w_gate, w_up, w_down):
    return dict(w_o=w_o, wgu=jnp.concatenate([w_gate, w_up], axis=1), w_down=w_down)


def _unprep_grads(dwin_p, dwuq, dwuk, dwuv, dwo, dwgu, dwd):
    dw_in = jnp.concatenate([dwin_p[:, 0:768], dwin_p[:, 2560:2564], dwin_p[:, 2048:2432], dwin_p[:, 2496:2528],
                             dwin_p[:, 768:2048]], axis=1)
    dw_uq = dwuq.reshape(MLA_Q_RANK, 4, MLA_PAD)[:, :, :MLA_QK].reshape(MLA_Q_RANK, 4 * MLA_QK)
    dw_ukv = jnp.concatenate([dwuk.reshape(MLA_KV_RANK, 4, MLA_PAD)[:, :, :HEAD_DIM],
                              dwuv.reshape(MLA_KV_RANK, 4, HEAD_DIM)], axis=2).reshape(MLA_KV_RANK, 512)
    return dict(w_in=dw_in, mla_w_uq=dw_uq, mla_w_ukv=dw_ukv, w_o=dwo, w_gate=dwgu[:, :D_FF], w_up=dwgu[:, D_FF:],
                w_down=dwd)


def _layer_fwd(l, x, xb, xbT, W, P, tabs, blk, late_weights=None):
    S = x.shape[0]
    nb = S // blk
    n = lambda s: f"l{l}_{s}"
    bs = min(512, S)
    h_att = _mm(xb, W["win_a"], name=n("in_att"), out_dtype=MXU_DTYPE, bm=1024, bn=1024, bk=1024, colscale=Q_COLSCALE)
    lat = _mm(xb, W["win_l"], name=n("in_lat"), bm=2048, bn=LAT_W, bk=1024)
    fg = lat[:, 512:516].T.reshape(4, S // 128, 128)
    cpos, cneg = _fox_gate_fwd(fg, P["fox_b_f"], name=n("fox_gate"))
    one3 = jnp.ones((S, 4, 3), MXU_DTYPE)
    zpad = jnp.zeros((S, 4, MLA_PAD - HEAD_DIM - 6), MXU_DTYPE)
    per_tok = lambda parts: parts.reshape(4, 3, S).transpose(2, 0, 1)
    q_f = jnp.concatenate([h_att[:, COL_FQ:COL_FQ + 256].reshape(S, 4, HEAD_DIM), per_tok(cpos), one3, zpad],
                          axis=2).reshape(S, 4 * MLA_PAD)
    k_f = jnp.concatenate([h_att[:, COL_FK:COL_FK + 256].reshape(S, 4, HEAD_DIM), one3, per_tok(cneg), zpad],
                          axis=2).reshape(S, 4 * MLA_PAD)
    v_f = h_att[:, COL_FV:COL_FV + 256]
    oT_a, lse_a = _smax_fwd_t(q_f.T, k_f, _kv_blocks_t(v_f, blk), dk=MLA_PAD, blk=blk, name=n("fox_fwd"))
    q_m, k_m, v_m = _mla_prep_fwd(lat, P["mla_g_q"], P["mla_g_kv"], W["wuq"], W["wuk"], W["wuv"], *tabs,
                                  bs=bs, name=n("mla_prep"))
    oT_b, lse_b = _smax_fwd_t(q_m.T, k_m, _kv_blocks_t(v_m, blk), dk=MLA_PAD, blk=blk, name=n("mla_fwd"))
    qT_c = h_att[:, COL_SQ:COL_SQ + 256].T
    oT_c, lt_c = _sb_fwd_t(qT_c, h_att, _kv_blocks_t(h_att[:, COL_SV:COL_SV + 256], blk), blk=blk, name=n("sb_fwd"))
    out_d, lse_d = _swa_fwd(h_att, P["swa_sinks"], name=n("swa_fwd"))
    outs = (oT_a, oT_b, oT_c, out_d)
    gn, gnT = _gnorm_fwd(outs, P["mix_g"], bs=bs, name=n("gnorm"))
    if late_weights is not None:
        W = dict(W, **late_weights(gn))
    u1 = _mm(gn, W["w_o"], name=n("out_proj"), bm=1024, bn=1024, bk=1024, resid=x, alpha=ALPHA)
    x1, x1b, x1bT, xh1, rs1 = _ln_fwd(u1, P["ln1_g"], P["ln1_b"], bs=bs, name=n("ln1"))
    gu = _mm(x1b, W["wgu"], name=n("gate_up"), bm=2048, bn=512, bk=1024)
    a = _swiglu_fwd(gu, bs=min(256, S), name=n("swiglu"))
    u2 = _mm(a, W["w_down"], name=n("down"), bm=1024, bn=1024, bk=_blk(D_FF, 1408), resid=x1, alpha=ALPHA)
    x2, x2b, x2bT, xh2, rs2 = _ln_fwd(u2, P["ln2_g"], P["ln2_b"], bs=bs, name=n("ln2"))
    saved = dict(xbT=xbT, gnT=gnT, x1bT=x1bT, h_att=h_att, lat=lat, fg=fg, outs=outs, oT_a=oT_a, oT_b=oT_b, q_f=q_f, k_f=k_f, v_f=v_f,
                 qT_c=qT_c, lse_a=lse_a, lse_b=lse_b, lse_d=lse_d, lt_c=lt_c, q_m=q_m, k_m=k_m, v_m=v_m, gn=gn,
                 xh1=xh1, rs1=rs1, gu=gu, a=a, xh2=xh2, rs2=rs2)
    return x2, x2b, x2bT, saved, W


def _layer_bwd(l, dx2, sv, W, P, tabs, blk, send_early=None):
    S = dx2.shape[0]
    n = lambda s: f"l{l}_{s}"
    bs = min(512, S)
    h_att = sv["h_att"]
    du2, du2b, dg2, db2 = _ln_bwd(dx2, sv["xh2"], sv["rs2"], P["ln2_g"], bs=bs, name=n("ln2_bwd"))
    da = _mm(du2b, W["w_down"], name=n("down_dx"), tb=True, bm=1024, bn=_blk(D_FF, 1408), bk=1024)
    dwd = _mm(sv["a"].T, du2b, name=n("down_dw"), bm=_blk(D_FF, 1408), bn=1024, bk=1024)
    dgu = _swiglu_bwd(da, sv["gu"], bs=min(256, S), name=n("swiglu_bwd"))
    dx1 = _mm(dgu, W["wgu"], name=n("gate_up_dx"), tb=True, bm=1024, bn=1024, bk=_blk(2 * D_FF, 1408), resid=du2,
              alpha=ALPHA)
    dwgu = _mm(sv["x1bT"], dgu, name=n("gate_up_dw"), bm=1024, bn=_blk(2 * D_FF, 1408), bk=1024)
    du1, du1b, dg1, db1 = _ln_bwd(dx1, sv["xh1"], sv["rs1"], P["ln1_g"], bs=bs, name=n("ln1_bwd"))
    dgn = _mm(du1b, W["w_o"], name=n("out_proj_dx"), tb=True, bm=1024, bn=1024, bk=1024)
    dwo = _mm(sv["gnT"], du1b, name=n("out_proj_dw"), bm=1024, bn=1024, bk=1024)
    mix_g = P["mix_g"]
    if send_early is not None:
        mix_g = mix_g + send_early(dict(w_o=dwo, w_gate=dwgu[:, :D_FF], w_up=dwgu[:, D_FF:], w_down=dwd))[0, 0]
    dmix, dmixT, dmixg = _gnorm_bwd(dgn, sv["outs"], mix_g, bs=bs, name=n("gnorm_bwd"))
    q_f, k_f = sv["q_f"], sv["k_f"]
    dqT_a, dk_a, dva = _smax_bwd_t(q_f.T, q_f, k_f, _kv_blocks_t(k_f, blk), sv["v_f"], dmix, dmixT, sv["oT_a"],
                                   sv["lse_a"], dk=MLA_PAD, dcb=0, qscale=HEAD_DIM ** -0.5, blk=blk, name=n("fox_bwd"))
    dq_a, dk_a = dqT_a.T.reshape(S, 4, MLA_PAD), dk_a.reshape(S, 4, MLA_PAD)
    dqa, dka = dq_a[:, :, :HEAD_DIM].reshape(S, 256), dk_a[:, :, :HEAD_DIM].reshape(S, 256)
    dcq = dq_a[:, :, HEAD_DIM].T.reshape(4, S // 128, 128)
    dck = dk_a[:, :, HEAD_DIM + 3].T.reshape(4, S // 128, 128)
    q_m, k_m = sv["q_m"], sv["k_m"]
    dqT_b, dkb, dvb = _smax_bwd_t(q_m.T, q_m, k_m, _kv_blocks_t(k_m, blk), sv["v_m"], dmix, dmixT, sv["oT_b"],
                                  sv["lse_b"], dk=MLA_PAD, dcb=2, qscale=MLA_QK ** -0.5, blk=blk, name=n("mla_bwd"))
    dqT_c, dkc, dvc = _sb_bwd_t(sv["qT_c"], h_att, _kv_blocks_t(h_att[:, COL_SK:COL_SK + 256], blk), dmix, dmixT,
                                sv["lt_c"], dcb=4, qscale=HEAD_DIM ** -0.5, blk=blk, name=n("sb_bwd"))
    dqc = dqT_c.T
    dqd, dkd, dvd, dsink = _swa_bwd(h_att, P["swa_sinks"], dmix, sv["outs"][3], sv["lse_d"], dcb=3, name=n("swa_bwd"))
    dlat, dwuq, dwuk, dwuv, dgq, dgkv = _mla_prep_bwd(
        sv["lat"], P["mla_g_q"], P["mla_g_kv"], W["wuq"], W["wuk"], W["wuv"], *tabs, dqT_b.T, dkb, dvb,
        bs=bs, name=n("mla_prep_bwd"))
    dfg, dbf = _fox_gate_bwd(sv["fg"], P["fox_b_f"], dck, dcq, q_unscale=HEAD_DIM ** 0.5, name=n("fox_gate_bwd"))
    dfg_blk = jnp.pad(dfg.reshape(4, S).T, ((0, 0), (0, 124)))
    dh = jnp.concatenate([t.astype(MXU_DTYPE) for t in (dqa, dka, dva, dqc, dkc, dvc, dqd, dkd, dvd, dlat, dfg_blk)], axis=1)
    dx = _mm(dh, W["win_p"], name=n("in_dx"), tb=True, bm=1024, bn=1024, bk=_blk(PERM_W, 1024), resid=du1, alpha=ALPHA)
    dwin_p = _mm(sv["xbT"], dh, name=n("in_dw"), bm=1024, bn=_blk(PERM_W, 1024), bk=1024)
    grads = _unprep_grads(dwin_p, dwuq, dwuk, dwuv, dwo, dwgu, dwd)
    grads.update(fox_b_f=dbf[:, 0], mla_g_q=dgq[0], mla_g_kv=dgkv[0], swa_sinks=dsink[:, 0], mix_g=dmixg[0],
                 ln1_g=dg1[0], ln1_b=db1[0], ln2_g=dg2[0], ln2_b=db2[0])
    return dx, grads


BIG = ("w_in", "mla_w_uq", "mla_w_ukv", "w_o", "w_gate", "w_up", "w_down")
SMALL = ("fox_b_f", "mla_g_q", "mla_g_kv", "swa_sinks", "mix_g", "ln1_g", "ln1_b", "ln2_g", "ln2_b")
SHARD_AXIS = dict(w_in=2, mla_w_uq=2, mla_w_ukv=2, w_o=1, w_gate=2, w_up=2, w_down=1)
N_CHIPS = 4
ANY = pl.BlockSpec(memory_space=pl.ANY)


def _chip_exchange(tensors, *, scatter, name):
    nt = len(tensors)

    def body(*refs):
        ins, outs = refs[:nt], refs[nt:2 * nt]
        send_sems, recv_sems, local_sems = refs[2 * nt:]
        x, y, c = lax.axis_index("x"), lax.axis_index("y"), lax.axis_index("c")
        me = 2 * x + y
        peers = [(1 - x, y), (x, 1 - y), (1 - x, 1 - y)]
        local, sends, recvs = [], [], []
        for t in range(nt):
            local.append(pltpu.make_async_copy(ins[t].at[me] if scatter else ins[t], outs[t].at[me], local_sems.at[t]))
            for r, (px, py) in enumerate(peers):
                k = 3 * t + r
                theirs = 2 * px + py
                sends.append(pltpu.make_async_remote_copy(
                    src_ref=ins[t].at[theirs] if scatter else ins[t], dst_ref=outs[t].at[me],
                    send_sem=send_sems.at[k], recv_sem=recv_sems.at[k], device_id=(px, py, c), device_id_type=MESH))
                recvs.append(pltpu.make_async_remote_copy(
                    src_ref=ins[t].at[me] if scatter else ins[t], dst_ref=outs[t].at[theirs],
                    send_sem=send_sems.at[k], recv_sem=recv_sems.at[k], device_id=(px, py, c), device_id_type=MESH))
        for cp in local + sends:
            cp.start()
        for cp in recvs:
            cp.wait_recv()
        for cp in sends:
            cp.wait_send()
        for cp in local:
            cp.wait()

    out_shape = [jax.ShapeDtypeStruct(t.shape if scatter else (N_CHIPS,) + t.shape, t.dtype) for t in tensors]
    return pl.pallas_call(
        body, name=name, in_specs=[ANY] * nt, out_specs=[ANY] * nt, out_shape=out_shape,
        scratch_shapes=[pltpu.SemaphoreType.DMA((3 * nt,)), pltpu.SemaphoreType.DMA((3 * nt,)),
                        pltpu.SemaphoreType.DMA((nt,))],
        compiler_params=pltpu.CompilerParams(has_side_effects=True),
    )(*tensors)


HBM = pl.BlockSpec(memory_space=pltpu.HBM)
SEM = pl.BlockSpec(memory_space=pltpu.SEMAPHORE)
N_PEER_CHIPS = N_CHIPS - 1


def _peer_copies(src_ref, land_ref, sems, scatter):
    x, y, c = lax.axis_index("x"), lax.axis_index("y"), lax.axis_index("c")
    me = 2 * x + y
    out = []
    for r, (px, py) in enumerate([(1 - x, y), (x, 1 - y), (1 - x, 1 - y)]):
        theirs = 2 * px + py
        send = pltpu.make_async_remote_copy(
            src_ref=src_ref.at[theirs] if scatter else src_ref, dst_ref=land_ref.at[me],
            send_sem=sems[2 * r], recv_sem=sems[2 * r + 1], device_id=(px, py, c), device_id_type=MESH)
        arrive = pltpu.make_async_remote_copy(
            src_ref=src_ref.at[me] if scatter else src_ref, dst_ref=land_ref.at[theirs],
            send_sem=sems[2 * r], recv_sem=sems[2 * r + 1], device_id=(px, py, c), device_id_type=MESH)
        out.append((send, arrive))
    return out


def _exchange_start(srcs, *, scatter, name):
    nt = len(srcs)
    ns = 2 * N_PEER_CHIPS * nt
    land_shapes = [s.shape if scatter else (N_CHIPS,) + s.shape for s in srcs]

    def body(*refs):
        src_refs, land_refs, outs = refs[:nt], refs[nt:2 * nt], refs[2 * nt:]
        for t in range(nt):
            for send, _ in _peer_copies(src_refs[t], land_refs[t], outs[6 * t:6 * t + 6], scatter):
                send.start()
        outs[-1][...] = jnp.zeros_like(outs[-1])

    res = pl.pallas_call(
        body, name=name,
        out_shape=(*[pltpu.SemaphoreType.DMA(())] * ns, *[pltpu.HBM(s.shape, s.dtype) for s in srcs],
                   *[pltpu.HBM(ls, s.dtype) for ls, s in zip(land_shapes, srcs)], jax.ShapeDtypeStruct((8, 128), F32)),
        in_specs=(HBM,) * (2 * nt), out_specs=(*[SEM] * ns, *[HBM] * (2 * nt), pl.BlockSpec(memory_space=pltpu.VMEM)),
        input_output_aliases={i: ns + i for i in range(2 * nt)},
        compiler_params=pltpu.CompilerParams(has_side_effects=pltpu.SideEffectType.DATAFLOW_SIDE_EFFECTING),
    )(*[pltpu.with_memory_space_constraint(s, pltpu.HBM) for s in srcs],
      *[pltpu.with_memory_space_constraint(lax.empty(ls, s.dtype), pltpu.HBM) for ls, s in zip(land_shapes, srcs)])
    return dict(sems=res[:ns], srcs=res[ns:ns + nt], lands=res[ns + nt:ns + 2 * nt], token=res[-1])


def _exchange_wait(started, after, *, scatter, name):
    nt = len(started["srcs"])
    ns = 2 * N_PEER_CHIPS * nt

    def body(*refs):
        src_refs, land_refs, sems = refs[:nt], refs[nt:2 * nt], refs[2 * nt:2 * nt + ns]
        for t in range(nt):
            for send, arrive in _peer_copies(src_refs[t], land_refs[t], sems[6 * t:6 * t + 6], scatter):
                send.wait_send()
                arrive.wait_recv()

    both = list(started["srcs"]) + list(started["lands"])
    res = pl.pallas_call(
        body, name=name, out_shape=tuple(pltpu.HBM(a.shape, a.dtype) for a in both),
        in_specs=(*[HBM] * (2 * nt), *[SEM] * ns, ANY), out_specs=(HBM,) * (2 * nt),
        input_output_aliases={i: i for i in range(2 * nt)},
        compiler_params=pltpu.CompilerParams(has_side_effects=pltpu.SideEffectType.DATAFLOW_SIDE_EFFECTING),
    )(*both, *started["sems"], after)
    return res[:nt], res[nt:]


def _core_exchange(tensors, *, name):
    nt = len(tensors)

    def body(*refs):
        ins, outs = refs[:nt], refs[nt:2 * nt]
        send_sems, recv_sems = refs[2 * nt:]
        sibling = (lax.axis_index("x"), lax.axis_index("y"), 1 - lax.axis_index("c"))
        copies = [pltpu.make_async_remote_copy(src_ref=ins[t], dst_ref=outs[t], send_sem=send_sems.at[t],
                                               recv_sem=recv_sems.at[t], device_id=sibling, device_id_type=MESH)
                  for t in range(nt)]
        for cp in copies:
            cp.start()
        for cp in copies:
            cp.wait_recv()
        for cp in copies:
            cp.wait_send()

    return pl.pallas_call(
        body, name=name, in_specs=[ANY] * nt, out_specs=[ANY] * nt,
        out_shape=[jax.ShapeDtypeStruct(t.shape, t.dtype) for t in tensors],
        scratch_shapes=[pltpu.SemaphoreType.DMA((nt,)), pltpu.SemaphoreType.DMA((nt,))],
        compiler_params=pltpu.CompilerParams(has_side_effects=True),
    )(*tensors)


def _all_sum_small(block, *, name):
    R = block.shape[0]
    n_dev = 8

    def body(x_ref, o_ref, slots, send_sems, recv_sems):
        x, y, c = lax.axis_index("x"), lax.axis_index("y"), lax.axis_index("c")
        me = 4 * x + 2 * y + c
        slots[me] = x_ref[...]
        sends, recvs = [], []
        for d in range(1, n_dev):
            px, py, pc = x ^ (d >> 2), y ^ ((d >> 1) & 1), c ^ (d & 1)
            theirs = 4 * px + 2 * py + pc
            sends.append(pltpu.make_async_remote_copy(
                src_ref=x_ref, dst_ref=slots.at[me], send_sem=send_sems.at[d - 1], recv_sem=recv_sems.at[d - 1],
                device_id=(px, py, pc), device_id_type=MESH))
            recvs.append(pltpu.make_async_remote_copy(
                src_ref=x_ref, dst_ref=slots.at[theirs], send_sem=send_sems.at[d - 1], recv_sem=recv_sems.at[d - 1],
                device_id=(px, py, pc), device_id_type=MESH))
        for cp in sends:
            cp.start()
        for cp in recvs:
            cp.wait_recv()
        for cp in sends:
            cp.wait_send()
        total = slots[0]
        for k in range(1, n_dev):
            total = total + slots[k]
        o_ref[...] = total

    return pl.pallas_call(
        body, name=name, in_specs=[pl.BlockSpec(memory_space=pltpu.VMEM)],
        out_specs=pl.BlockSpec(memory_space=pltpu.VMEM), out_shape=jax.ShapeDtypeStruct((R, 128), F32),
        scratch_shapes=[pltpu.VMEM((n_dev, R, 128), F32), pltpu.SemaphoreType.DMA((n_dev - 1,)),
                        pltpu.SemaphoreType.DMA((n_dev - 1,))],
        compiler_params=pltpu.CompilerParams(has_side_effects=True),
    )(block)


def _sum_chips(recv, *, br, name):
    _, R, C = recv.shape

    def body(r_ref, o_ref):
        total = r_ref[0].astype(F32)
        for k in range(1, N_CHIPS):
            total = total + r_ref[k].astype(F32)
        o_ref[...] = total

    return pl.pallas_call(
        body, name=name, grid=(R // br,), in_specs=[pl.BlockSpec((N_CHIPS, br, C), lambda i: (0, i, 0))],
        out_specs=pl.BlockSpec((br, C), lambda i: (i, 0)), out_shape=jax.ShapeDtypeStruct((R, C), F32),
        compiler_params=_cparams("parallel"),
    )(recv)


def _sum_chips_into(acc, land, own, me, layer, *, br, name):
    _, R, C = land.shape

    def body(me_ref, land_ref, own_ref, acc_ref, o_ref):
        mine = me_ref[0]
        total = None
        for k in range(N_CHIPS):
            part = jnp.where(mine == k, own_ref[...], land_ref[k]).astype(F32)
            total = part if total is None else total + part
        o_ref[0] = total

    return pl.pallas_call(
        body, name=name, grid=(R // br,),
        in_specs=[pl.BlockSpec(memory_space=pltpu.SMEM), pl.BlockSpec((N_CHIPS, br, C), lambda i: (0, i, 0)),
                  pl.BlockSpec((br, C), lambda i: (i, 0)), ANY],
        out_specs=pl.BlockSpec((1, br, C), lambda i: (layer, i, 0)),
        out_shape=jax.ShapeDtypeStruct(acc.shape, F32), input_output_aliases={3: 0},
        compiler_params=_cparams("parallel"),
    )(me, land, own, acc)


def _adamw_math(w, g, m, v):
    m = ADAM_B1 * m + (1.0 - ADAM_B1) * g
    v = ADAM_B2 * v + (1.0 - ADAM_B2) * (g * g)
    m_hat = m / (1.0 - ADAM_B1 ** ADAM_STEP)
    v_hat = v / (1.0 - ADAM_B2 ** ADAM_STEP)
    return -ADAM_LR * (m_hat / (jnp.sqrt(v_hat) + ADAM_EPS) + ADAM_WD * w), m, v


def _adamw(w, m, v, g_a, g_b, *, br, name):
    R, C = w.shape
    two = g_b is not None

    def body(*refs):
        if two:
            w_ref, m_ref, v_ref, ga_ref, gb_ref, g_ref, d_ref, nm_ref, nv_ref = refs
            g = ga_ref[...] + gb_ref[...]
        else:
            w_ref, m_ref, v_ref, ga_ref, g_ref, d_ref, nm_ref, nv_ref = refs
            g = ga_ref[...]
        g_ref[...] = g
        d_ref[...], nm_ref[...], nv_ref[...] = _adamw_math(w_ref[...], g, m_ref[...], v_ref[...])

    spec = pl.BlockSpec((br, C), lambda i: (i, 0))
    args = [w, m, v, g_a] + ([g_b] if two else [])
    return pl.pallas_call(
        body, name=name, grid=(R // br,), in_specs=[spec] * len(args), out_specs=[spec] * 4,
        out_shape=[jax.ShapeDtypeStruct((R, C), F32)] * 4,
        compiler_params=_cparams("parallel"),
    )(*args)


SMALL_ROWS = dict(fox_b_f=1, mla_g_q=2, mla_g_kv=1, swa_sinks=1, mix_g=8, ln1_g=8, ln1_b=8, ln2_g=8, ln2_b=8)
SMALL_ROWS_PER_LAYER = sum(SMALL_ROWS.values())


def _pack_small(vals, extra_rows):
    L = vals[SMALL[0]].shape[0]
    per_layer = []
    for name in SMALL:
        a = vals[name].astype(F32)
        a = jnp.pad(a, ((0, 0), (0, SMALL_ROWS[name] * 128 - a.shape[1])))
        per_layer.append(a.reshape(L, SMALL_ROWS[name], 128))
    out = jnp.concatenate(per_layer, axis=1).reshape(L * SMALL_ROWS_PER_LAYER, 128)
    return jnp.pad(out, ((0, extra_rows), (0, 0)))


def _unpack_small(block, shapes):
    L = shapes[SMALL[0]][0]
    body = block[:L * SMALL_ROWS_PER_LAYER].reshape(L, SMALL_ROWS_PER_LAYER, 128)
    out, r = {}, 0
    for name in SMALL:
        n = shapes[name][1]
        out[name] = body[:, r:r + SMALL_ROWS[name]].reshape(L, SMALL_ROWS[name] * 128)[:, :n]
        r += SMALL_ROWS[name]
    return out


PACK_ROW_MULTIPLE = 256


def _pack(parts):
    flat = [p.reshape(-1, 128) for p in parts]
    pad = (-sum(f.shape[0] for f in flat)) % PACK_ROW_MULTIPLE
    if pad:
        flat.append(jnp.zeros((pad, 128), flat[0].dtype))
    return jnp.concatenate(flat, axis=0)


def _unpack(block, shapes):
    out, r = [], 0
    for shp in shapes:
        n = int(np.prod(shp)) // 128
        out.append(block[r:r + n].reshape(shp))
        r += n
    return out


def _shard(g, k, axis):
    n = g.shape[axis] // N_CHIPS
    return lax.slice_in_dim(g, k * n, (k + 1) * n, axis=axis)


def _to_chips(g, axis):
    L, a, b = g.shape
    if axis == 2:
        return g.reshape(L, a, N_CHIPS, b // N_CHIPS).transpose(2, 0, 1, 3)
    return g.reshape(L, N_CHIPS, a // N_CHIPS, b).transpose(1, 0, 2, 3)


def _from_chips(g, axis):
    _, L, a, b = g.shape
    if axis == 2:
        return g.transpose(1, 2, 0, 3).reshape(L, a, N_CHIPS * b)
    return g.transpose(1, 0, 2, 3).reshape(L, N_CHIPS * a, b)


def kernel(x, w_in, fox_b_f, mla_g_q, mla_g_kv, mla_w_uq, mla_w_ukv, swa_sinks, mix_g, w_o, ln1_g, ln1_b, w_gate, w_up, w_down, ln2_g, ln2_b, loss_target, m_w_in, m_fox_b_f, m_mla_g_q, m_mla_g_kv, m_mla_w_uq, m_mla_w_ukv, m_swa_sinks, m_mix_g, m_w_o, m_ln1_g, m_ln1_b, m_w_gate, m_w_up, m_w_down, m_ln2_g, m_ln2_b, v_w_in, v_fox_b_f, v_mla_g_q, v_mla_g_kv, v_mla_w_uq, v_mla_w_ukv, v_swa_sinks, v_mix_g, v_w_o, v_ln1_g, v_ln1_b, v_w_gate, v_w_up, v_w_down, v_ln2_g, v_ln2_b):
    w = dict(w_in=w_in, fox_b_f=fox_b_f, mla_g_q=mla_g_q, mla_g_kv=mla_g_kv, mla_w_uq=mla_w_uq, mla_w_ukv=mla_w_ukv,
             swa_sinks=swa_sinks, mix_g=mix_g, w_o=w_o, ln1_g=ln1_g, ln1_b=ln1_b, w_gate=w_gate, w_up=w_up,
             w_down=w_down, ln2_g=ln2_g, ln2_b=ln2_b)
    m = dict(w_in=m_w_in, fox_b_f=m_fox_b_f, mla_g_q=m_mla_g_q, mla_g_kv=m_mla_g_kv, mla_w_uq=m_mla_w_uq,
             mla_w_ukv=m_mla_w_ukv, swa_sinks=m_swa_sinks, mix_g=m_mix_g, w_o=m_w_o, ln1_g=m_ln1_g, ln1_b=m_ln1_b,
             w_gate=m_w_gate, w_up=m_w_up, w_down=m_w_down, ln2_g=m_ln2_g, ln2_b=m_ln2_b)
    v = dict(w_in=v_w_in, fox_b_f=v_fox_b_f, mla_g_q=v_mla_g_q, mla_g_kv=v_mla_g_kv, mla_w_uq=v_mla_w_uq,
             mla_w_ukv=v_mla_w_ukv, swa_sinks=v_swa_sinks, mix_g=v_mix_g, w_o=v_w_o, ln1_g=v_ln1_g, ln1_b=v_ln1_b,
             w_gate=v_w_gate, w_up=v_w_up, w_down=v_w_down, ln2_g=v_ln2_g, ln2_b=v_ln2_b)
    names = tuple(w)
    L = w_in.shape[0]
    S = x.shape[1]
    blk = min(256, S)
    bs = min(512, S)

    me = 2 * lax.axis_index("x") + lax.axis_index("y")
    axis_of = {k: SHARD_AXIS[k] - 1 for k in BIG}
    groups = (("w_in", "mla_w_uq", "mla_w_ukv"), ("w_o", "w_gate", "w_up", "w_down"))

    started, last = [], None
    for l in range(L):
        per_group = []
        for g, group in enumerate(groups):
            srcs = [w[k][l].astype(MXU_DTYPE) for k in group]
            if last is not None:
                t = min(range(len(srcs)), key=lambda i: srcs[i].size)
                srcs[t] = srcs[t] + last["token"][0, 0].astype(MXU_DTYPE)
            last = _exchange_start(srcs, scatter=False, name=f"gather_start{l}_{g}")
            per_group.append(last)
        started.append(per_group)
    all_started = sum(st["token"] for per_group in started for st in per_group)

    def gathered(l, g, after):
        mine, lands = _exchange_wait(started[l][g], after, scatter=False, name=f"gather_wait{l}_{g}")
        return [jnp.concatenate([jnp.where(me == k, mine[t], lands[t][k]) for k in range(N_CHIPS)], axis=axis_of[name])
                for t, name in enumerate(groups[g])]

    def scatter(l, g, grads):
        to_owner = [_to_chips(grads[k][None], axis_of[k] + 1)[:, 0].astype(MXU_DTYPE) for k in groups[g]]
        return _exchange_start(to_owner, scatter=True, name=f"scatter_start{l}_{g}")

    tabs = _rope_tables(S)
    Ps = []
    for l in range(L):
        P = dict(fox_b_f=fox_b_f[l], swa_sinks=swa_sinks[l])
        for k in ("mla_g_q", "mla_g_kv", "mix_g", "ln1_g", "ln1_b", "ln2_g", "ln2_b"):
            P[k] = w[k][l][None, :]
        Ps.append(P)

    xa = x[0]
    xb = xa.astype(MXU_DTYPE)
    xbT = xb.T
    saved, Ws = [], []
    for l in range(L):
        W = _prep_weights_a(*gathered(l, 0, all_started if l == 0 else xa))
        late = lambda after, l=l: _prep_weights_b(*gathered(l, 1, after))
        xa, xb, xbT, sv, W = _layer_fwd(l, xa, xb, xbT, W, Ps[l], tabs, blk, late_weights=late)
        saved.append(sv)
        Ws.append(W)
    dx, loss_part = _loss_head(xa, loss_target[0], bs=bs, name="loss_head")

    layer_grads = [None] * L
    sent = [[None, None] for _ in range(L)]
    pin = None
    for l in reversed(range(L)):
        P = Ps[l] if pin is None else dict(Ps[l], ln2_g=Ps[l]["ln2_g"] + pin[0, 0])

        def send_early(grads, l=l):
            sent[l][1] = scatter(l, 1, grads)
            return sent[l][1]["token"]

        dx, layer_grads[l] = _layer_bwd(l, dx, saved[l], Ws[l], P, tabs, blk, send_early=send_early)
        sent[l][0] = scatter(l, 0, layer_grads[l])
        pin = sent[l][0]["token"]
    grad_x = dx[None]

    me_arr = me.astype(jnp.int32)[None]
    partial = {k: jnp.zeros(w[k].shape, F32) for k in BIG}
    after = dx
    for l in reversed(range(L)):
        for g in (1, 0):
            mine, lands = _exchange_wait(sent[l][g], after, scatter=True, name=f"scatter_wait{l}_{g}")
            for t, k in enumerate(groups[g]):
                own = lax.dynamic_index_in_dim(mine[t], me, 0, keepdims=False)
                partial[k] = _sum_chips_into(partial[k], lands[t], own, me_arr, l, br=_rows(own.shape[0]),
                                             name=f"sum_{k}_l{l}")
            after = partial[groups[g][-1]]
    partial = [partial[k] for k in BIG]
    sibling = _core_exchange(partial, name="swap_partials")
    local = {k: jnp.stack([layer_grads[l][k] for l in range(L)]) for k in SMALL}
    out = {}
    for k, mine, theirs in zip(BIG, partial, sibling):
        shp = w[k].shape
        two_d = lambda a: a.reshape(shp[0] * shp[1], shp[2])
        res = _adamw(two_d(w[k]), two_d(m[k]), two_d(v[k]), two_d(mine), two_d(theirs), br=_rows(shp[0] * shp[1]),
                     name=f"adamw_{k}")
        out[k] = [a.reshape(shp) for a in res]

    shapes = {k: w[k].shape for k in SMALL}
    extra = 8 + (-L * SMALL_ROWS_PER_LAYER) % 8
    block = _pack_small({k: local[k] for k in SMALL}, extra)
    block = block.at[L * SMALL_ROWS_PER_LAYER, 0].set(loss_part[0, 0])
    total = _all_sum_small(block, name="sum_small")
    loss = total[L * SMALL_ROWS_PER_LAYER, 0]
    res = _adamw(_pack_small({k: w[k] for k in SMALL}, extra), _pack_small({k: m[k] for k in SMALL}, extra),
                 _pack_small({k: v[k] for k in SMALL}, extra), total, None, br=total.shape[0], name="adamw_small")
    res = [_unpack_small(t, shapes) for t in res]
    for k in SMALL:
        out[k] = [r[k] for r in res]

    return (loss, grad_x, *[out[k][0] for k in names], *[out[k][1] for k in names],
            *[out[k][2] for k in names], *[out[k][3] for k in names])


def _rows(n):
    for b in (256, 128, 64, 32, 16, 8):
        if n % b == 0:
            return b
    return n
```

```python
import functools

import numpy as np
import jax
import jax.numpy as jnp
from jax import lax
from jax.experimental import pallas as pl
from jax.experimental.pallas import tpu as pltpu

F32 = jnp.float32
MXU_DTYPE = jnp.bfloat16
NEG_INF = -1e30

D_MODEL = 1024
DEPTH = 4
HEAD_DIM = 64
GROUP_WIDTH = 256
N_GROUPS = 4
D_FF = 2816
MLA_Q_RANK = 256
MLA_KV_RANK = 128
MLA_ROPE = 32
MLA_QK = 96
MLA_PAD = 128
ROPE_THETA = 10000.0
WINDOW = 128
ALPHA = (2.0 * DEPTH) ** 0.25
SWA_SLOPES = tuple(float(2.0 ** (-8.0 * h / 4)) for h in range(1, 5))
IN_WIDTH = 2468
ATT_W = 2048
LAT_W = 640
PERM_W = ATT_W + LAT_W
COL_FQ, COL_FK, COL_FV = 0, 256, 512
COL_SQ, COL_SK, COL_SV = 768, 1024, 1280
COL_WQ, COL_WK, COL_WV = 1536, 1792, 1920
Q_COLSCALE = np.ones((1, ATT_W), np.float32)
Q_COLSCALE[:, COL_FQ:COL_FQ + 256] = HEAD_DIM ** -0.5
Q_COLSCALE[:, COL_SQ:COL_SQ + 256] = HEAD_DIM ** -0.5

ADAM_LR, ADAM_B1, ADAM_B2, ADAM_EPS, ADAM_WD, ADAM_STEP = 0.001, 0.9, 0.999, 1e-08, 0.01, 10

VMEM_LIMIT = 56 * 1024 * 1024
NT = (((1,), (1,)), ((), ()))
TN = (((0,), (0,)), ((), ()))
MESH = pl.DeviceIdType.MESH


def _cparams(*sem):
    return pltpu.CompilerParams(dimension_semantics=sem, vmem_limit_bytes=VMEM_LIMIT)


def _split2(x):
    hi = x.astype(MXU_DTYPE)
    lo = (x - hi.astype(F32)).astype(MXU_DTYPE)
    return hi, lo


def _dot01(x, m01, dn=None, parts=2):
    acc = None
    rem = x
    for _ in range(parts):
        part = rem.astype(MXU_DTYPE)
        rem = rem - part.astype(F32)
        if dn is None:
            t = jnp.dot(part, m01, preferred_element_type=F32)
        else:
            t = lax.dot_general(part, m01, dn, preferred_element_type=F32)
        acc = t if acc is None else acc + t
    return acc


def _mm(a, b, *, name, ta=False, tb=False, out_dtype=F32, bm=512, bn=512, bk=512, resid=None, alpha=1.0,
        colscale=None):
    M, K = (a.shape[1], a.shape[0]) if ta else a.shape
    N = b.shape[0] if tb else b.shape[1]
    assert (b.shape[1] if tb else b.shape[0]) == K
    assert resid is None or colscale is None
    bm, bn, bk = min(bm, M), min(bn, N), min(bk, K)
    assert M % bm == 0 and N % bn == 0 and K % bk == 0, (name, M, N, K, bm, bn, bk)
    nk = K // bk
    assert nk == 1 or (out_dtype == F32 and colscale is None), name
    dn = (((0 if ta else 1,), (1 if tb else 0,)), ((), ()))

    extra = resid is not None or colscale is not None

    def body(*refs):
        a_ref, b_ref = refs[:2]
        r_ref = refs[2] if extra else None
        o_ref = refs[3] if extra else refs[2]
        k = pl.program_id(2)

        def first():
            r = lax.dot_general(a_ref[...].astype(MXU_DTYPE), b_ref[...].astype(MXU_DTYPE), dn,
                                preferred_element_type=F32)
            if resid is not None:
                r = r + alpha * r_ref[...]
            if colscale is not None:
                r = r * r_ref[...]
            o_ref[...] = r.astype(o_ref.dtype)

        if nk == 1:
            first()
        else:
            pl.when(k == 0)(first)

            @pl.when(k > 0)
            def _():
                o_ref[...] += lax.dot_general(a_ref[...].astype(MXU_DTYPE), b_ref[...].astype(MXU_DTYPE), dn,
                                              preferred_element_type=F32)

    a_spec = pl.BlockSpec((bk, bm), lambda i, j, k: (k, i)) if ta else pl.BlockSpec((bm, bk), lambda i, j, k: (i, k))
    b_spec = pl.BlockSpec((bn, bk), lambda i, j, k: (j, k)) if tb else pl.BlockSpec((bk, bn), lambda i, j, k: (k, j))
    in_specs = [a_spec, b_spec]
    args = [a, b]
    if resid is not None:
        in_specs.append(pl.BlockSpec((bm, bn), lambda i, j, k: (i, j)))
        args.append(resid)
    if colscale is not None:
        in_specs.append(pl.BlockSpec((1, bn), lambda i, j, k: (0, j)))
        args.append(colscale)
    return pl.pallas_call(
        body, name=name, grid=(M // bm, N // bn, nk), in_specs=in_specs,
        out_specs=pl.BlockSpec((bm, bn), lambda i, j, k: (i, j)),
        out_shape=jax.ShapeDtypeStruct((M, N), out_dtype),
        compiler_params=_cparams("parallel", "parallel", "arbitrary"),
    )(*args)


def _softmax_attn_fwd(q_arr, k_arr, v_arr, *, qcb, kcb, vcb, dk, scale, cum_col=None, cum_row=None, blk, name):
    S = q_arr.shape[0]
    nb = S // blk
    bias = cum_col is not None
    W = 2 * dk

    def body(*refs):
        if bias:
            q_ref, k_ref, v_ref, cc_ref, cr_ref, o_ref, lse_ref = refs
        else:
            q_ref, k_ref, v_ref, o_ref, lse_ref = refs
        p = pl.program_id(0)
        i = pl.program_id(1)
        row = lax.broadcasted_iota(jnp.int32, (blk, blk), 0)
        col = lax.broadcasted_iota(jnp.int32, (blk, blk), 1)
        for hh in range(2):
            q = q_ref[:, hh * dk:(hh + 1) * dk]

            def tile(j, carry, masked, hh=hh, q=q):
                m, l, acc = carry
                r0 = pl.multiple_of(j * blk, blk)
                ks = k_ref[pl.ds(r0, blk), hh * dk:(hh + 1) * dk]
                vs = v_ref[pl.ds(r0, blk), hh * HEAD_DIM:(hh + 1) * HEAD_DIM]
                s = lax.dot_general(q, ks, NT, preferred_element_type=F32) * scale
                if bias:
                    s = s + cc_ref[hh] - cr_ref[(2 * p + hh) * nb + j]
                if masked:
                    s = jnp.where(col <= row, s, NEG_INF)
                mn = jnp.maximum(m, jnp.max(s, axis=1, keepdims=True))
                a = jnp.exp(m - mn)
                pe = jnp.exp(s - mn)
                l = a * l + jnp.sum(pe, axis=1, keepdims=True)
                acc = a * acc + jnp.dot(pe.astype(MXU_DTYPE), vs, preferred_element_type=F32)
                return mn, l, acc

            init = (jnp.full((blk, 1), NEG_INF, F32), jnp.zeros((blk, 1), F32), jnp.zeros((blk, HEAD_DIM), F32))
            carry = lax.fori_loop(0, i, functools.partial(tile, masked=False), init)
            m, l, acc = tile(i, carry, True)
            o_ref[:, hh * HEAD_DIM:(hh + 1) * HEAD_DIM] = acc / l
            lse_ref[hh] = m + jnp.log(l)

    in_specs = [pl.BlockSpec((blk, W), lambda p, i: (i, qcb + p)),
                pl.BlockSpec((S, W), lambda p, i: (0, kcb + p)),
                pl.BlockSpec((S, 128), lambda p, i: (0, vcb + p))]
    args = [q_arr, k_arr, v_arr]
    if bias:
        in_specs += [pl.BlockSpec((2, blk, 1), lambda p, i: (p, i, 0)),
                     pl.BlockSpec((4 * nb, 1, blk), lambda p, i: (0, 0, 0))]
        args += [cum_col, cum_row]
    return pl.pallas_call(
        body, name=name, grid=(2, nb), in_specs=in_specs,
        out_specs=[pl.BlockSpec((blk, 128), lambda p, i: (i, p)), pl.BlockSpec((2, blk, 1), lambda p, i: (p, i, 0))],
        out_shape=[jax.ShapeDtypeStruct((S, GROUP_WIDTH), F32), jax.ShapeDtypeStruct((4, S, 1), F32)],
        compiler_params=_cparams("arbitrary", "arbitrary"),
    )(*args)


def _softmax_attn_bwd(q_arr, k_arr, v_arr, dmix, o_arr, lse, *, qcb, kcb, vcb, dcb, dk, scale,
                      cum_col=None, cum_row=None, blk, name):
    S = q_arr.shape[0]
    nb = S // blk
    bias = cum_col is not None
    W = 2 * dk

    def body(*refs):
        if bias:
            q_ref, k_ref, v_ref, do_ref, o_ref, lse_ref, cc_ref, cr_ref, dq_ref, dk_ref, dv_ref, dc_ref, dcq_ref = refs
        else:
            q_ref, k_ref, v_ref, do_ref, o_ref, lse_ref, dq_ref, dk_ref, dv_ref = refs
        p = pl.program_id(0)
        i = pl.program_id(1)

        @pl.when(i == 0)
        def _():
            dk_ref[...] = jnp.zeros_like(dk_ref)
            dv_ref[...] = jnp.zeros_like(dv_ref)
            if bias:
                dc_ref[...] = jnp.zeros_like(dc_ref)

        row = lax.broadcasted_iota(jnp.int32, (blk, blk), 0)
        col = lax.broadcasted_iota(jnp.int32, (blk, blk), 1)
        for hh in range(2):
            q = q_ref[:, hh * dk:(hh + 1) * dk]
            do = do_ref[:, hh * HEAD_DIM:(hh + 1) * HEAD_DIM]
            delta = jnp.sum(do * o_ref[:, hh * HEAD_DIM:(hh + 1) * HEAD_DIM], axis=1, keepdims=True)
            dob = do.astype(MXU_DTYPE)
            lse_h = lse_ref[hh]

            def tile(j, carry, masked, hh=hh, q=q, dob=dob, delta=delta, lse_h=lse_h):
                dq, dcq = carry
                r0 = pl.multiple_of(j * blk, blk)
                ks = k_ref[pl.ds(r0, blk), hh * dk:(hh + 1) * dk]
                vs = v_ref[pl.ds(r0, blk), hh * HEAD_DIM:(hh + 1) * HEAD_DIM]
                s = lax.dot_general(q, ks, NT, preferred_element_type=F32) * scale
                if bias:
                    s = s + cc_ref[hh] - cr_ref[(2 * p + hh) * nb + j]
                if masked:
                    s = jnp.where(col <= row, s, NEG_INF)
                pr = jnp.exp(s - lse_h)
                dp = lax.dot_general(dob, vs, NT, preferred_element_type=F32)
                ds = pr * (dp - delta)
                dsb = ds.astype(MXU_DTYPE)
                dv_ref[pl.ds(r0, blk), hh * HEAD_DIM:(hh + 1) * HEAD_DIM] += lax.dot_general(
                    pr.astype(MXU_DTYPE), dob, TN, preferred_element_type=F32)
                dk_ref[pl.ds(r0, blk), hh * dk:(hh + 1) * dk] += lax.dot_general(
                    dsb, q, TN, preferred_element_type=F32) * scale
                if bias:
                    dc_ref[hh * nb + j] -= jnp.sum(ds, axis=0, keepdims=True)
                    dcq = dcq + jnp.sum(ds, axis=1, keepdims=True)
                return dq + jnp.dot(dsb, ks, preferred_element_type=F32) * scale, dcq

            carry = lax.fori_loop(0, i, functools.partial(tile, masked=False),
                                  (jnp.zeros((blk, dk), F32), jnp.zeros((blk, 1), F32)))
            dq, dcq = tile(i, carry, True)
            dq_ref[:, hh * dk:(hh + 1) * dk] = dq
            if bias:
                dcq_ref[hh] = dcq

    in_specs = [pl.BlockSpec((blk, W), lambda p, i: (i, qcb + p)),
                pl.BlockSpec((S, W), lambda p, i: (0, kcb + p)),
                pl.BlockSpec((S, 128), lambda p, i: (0, vcb + p)),
                pl.BlockSpec((blk, 128), lambda p, i: (i, dcb + p)),
                pl.BlockSpec((blk, 128), lambda p, i: (i, p)),
                pl.BlockSpec((2, blk, 1), lambda p, i: (p, i, 0))]
    args = [q_arr, k_arr, v_arr, dmix, o_arr, lse]
    out_specs = [pl.BlockSpec((blk, W), lambda p, i: (i, p)),
                 pl.BlockSpec((S, W), lambda p, i: (0, p)),
                 pl.BlockSpec((S, 128), lambda p, i: (0, p))]
    out_shape = [jax.ShapeDtypeStruct((S, 4 * dk), F32), jax.ShapeDtypeStruct((S, 4 * dk), F32),
                 jax.ShapeDtypeStruct((S, GROUP_WIDTH), F32)]
    if bias:
        in_specs += [pl.BlockSpec((2, blk, 1), lambda p, i: (p, i, 0)),
                     pl.BlockSpec((4 * nb, 1, blk), lambda p, i: (0, 0, 0))]
        args += [cum_col, cum_row]
        out_specs += [pl.BlockSpec((2 * nb, 1, blk), lambda p, i: (p, 0, 0)), pl.BlockSpec((2, blk, 1), lambda p, i: (p, i, 0))]
        out_shape += [jax.ShapeDtypeStruct((4 * nb, 1, blk), F32), jax.ShapeDtypeStruct((4, S, 1), F32)]
    return pl.pallas_call(
        body, name=name, grid=(2, nb), in_specs=in_specs, out_specs=out_specs, out_shape=out_shape,
        compiler_params=_cparams("arbitrary", "arbitrary"),
    )(*args)


def _sb_tile(q, ks, scale, strict_mask, carry_l, tri_excl):
    z = lax.dot_general(q, ks, NT, preferred_element_type=F32) * scale
    lb = -(jnp.maximum(z, 0.0) + jnp.log(1.0 + jnp.exp(-jnp.abs(z))))
    if strict_mask is not None:
        lb = jnp.where(strict_mask, lb, 0.0)
    between = _dot01(lb, tri_excl) + carry_l
    a = jnp.exp(z + lb + between)
    if strict_mask is not None:
        a = jnp.where(strict_mask, a, 0.0)
    return z, lb, a


def _sb_attn_fwd(h_att, *, blk, name):
    S = h_att.shape[0]
    nb = S // blk
    scale = HEAD_DIM ** -0.5
    qcb, kcb, vcb = COL_SQ // 128, COL_SK // 128, COL_SV // 128

    def body(q_ref, k_ref, v_ref, o_ref, lt_ref):
        i = pl.program_id(1)
        row = lax.broadcasted_iota(jnp.int32, (blk, blk), 0)
        col = lax.broadcasted_iota(jnp.int32, (blk, blk), 1)
        strict = col < row
        tri_excl = (row > col).astype(MXU_DTYPE)
        for hh in range(2):
            sl = slice(hh * HEAD_DIM, (hh + 1) * HEAD_DIM)
            q = q_ref[:, sl]

            def tile(j, carry, mask, sl=sl, q=q):
                cl, acc = carry
                r0 = pl.multiple_of(j * blk, blk)
                _, lb, a = _sb_tile(q, k_ref[pl.ds(r0, blk), sl], scale, mask, cl, tri_excl)
                acc = acc + jnp.dot(a.astype(MXU_DTYPE), v_ref[pl.ds(r0, blk), sl], preferred_element_type=F32)
                return cl + jnp.sum(lb, axis=1, keepdims=True), acc

            carry = tile(i, (jnp.zeros((blk, 1), F32), jnp.zeros((blk, HEAD_DIM), F32)), strict)
            cl, acc = lax.fori_loop(0, i, lambda jj, c: tile(i - 1 - jj, c, None), carry)
            o_ref[:, sl] = acc
            lt_ref[hh] = cl

    return pl.pallas_call(
        body, name=name, grid=(2, nb),
        in_specs=[pl.BlockSpec((blk, 128), lambda p, i: (i, qcb + p)),
                  pl.BlockSpec((S, 128), lambda p, i: (0, kcb + p)),
                  pl.BlockSpec((S, 128), lambda p, i: (0, vcb + p))],
        out_specs=[pl.BlockSpec((blk, 128), lambda p, i: (i, p)), pl.BlockSpec((2, blk, 1), lambda p, i: (p, i, 0))],
        out_shape=[jax.ShapeDtypeStruct((S, GROUP_WIDTH), F32), jax.ShapeDtypeStruct((4, S, 1), F32)],
        compiler_params=_cparams("arbitrary", "arbitrary"),
    )(h_att, h_att, h_att)


def _sb_attn_bwd(h_att, dmix, ltot_arr, *, dcb, blk, name):
    S = h_att.shape[0]
    nb = S // blk
    scale = HEAD_DIM ** -0.5
    qcb, kcb, vcb = COL_SQ // 128, COL_SK // 128, COL_SV // 128

    def body(q_ref, k_ref, v_ref, do_ref, lt_ref, dq_ref, dk_ref, dv_ref):
        i = pl.program_id(1)

        @pl.when(i == 0)
        def _():
            dk_ref[...] = jnp.zeros_like(dk_ref)
            dv_ref[...] = jnp.zeros_like(dv_ref)

        row = lax.broadcasted_iota(jnp.int32, (blk, blk), 0)
        col = lax.broadcasted_iota(jnp.int32, (blk, blk), 1)
        strict = col < row
        up_incl = (row <= col).astype(MXU_DTYPE)
        up_excl = (row < col).astype(MXU_DTYPE)
        for hh in range(2):
            sl = slice(hh * HEAD_DIM, (hh + 1) * HEAD_DIM)
            q = q_ref[:, sl]
            dob = do_ref[:, sl].astype(MXU_DTYPE)
            ltot = lt_ref[hh]

            def tile(j, carry, mask, sl=sl, q=q, dob=dob, ltot=ltot):
                cl, cg, dq = carry
                r0 = pl.multiple_of(j * blk, blk)
                ks = k_ref[pl.ds(r0, blk), sl]
                vs = v_ref[pl.ds(r0, blk), sl]
                z = lax.dot_general(q, ks, NT, preferred_element_type=F32) * scale
                lb = -(jnp.maximum(z, 0.0) + jnp.log(1.0 + jnp.exp(-jnp.abs(z))))
                if mask is not None:
                    lb = jnp.where(mask, lb, 0.0)
                between = ltot - cl - _dot01(lb, up_incl)
                a = jnp.exp(z + lb + between)
                if mask is not None:
                    a = jnp.where(mask, a, 0.0)
                g = lax.dot_general(dob, vs, NT, preferred_element_type=F32) * a
                e = cg + _dot01(g, up_excl)
                dz = g * jnp.exp(lb) - e * jnp.exp(z + lb)
                if mask is not None:
                    dz = jnp.where(mask, dz, 0.0)
                dzb = dz.astype(MXU_DTYPE)
                dv_ref[pl.ds(r0, blk), sl] += lax.dot_general(a.astype(MXU_DTYPE), dob, TN, preferred_element_type=F32)
                dk_ref[pl.ds(r0, blk), sl] += lax.dot_general(dzb, q, TN, preferred_element_type=F32) * scale
                dq = dq + jnp.dot(dzb, ks, preferred_element_type=F32) * scale
                return cl + jnp.sum(lb, axis=1, keepdims=True), cg + jnp.sum(g, axis=1, keepdims=True), dq

            zc = jnp.zeros((blk, 1), F32)
            carry = lax.fori_loop(0, i, lambda j, c: tile(j, c, None), (zc, zc, jnp.zeros((blk, HEAD_DIM), F32)))
            _, _, dq = tile(i, carry, strict)
            dq_ref[:, sl] = dq

    return pl.pallas_call(
        body, name=name, grid=(2, nb),
        in_specs=[pl.BlockSpec((blk, 128), lambda p, i: (i, qcb + p)),
                  pl.BlockSpec((S, 128), lambda p, i: (0, kcb + p)),
                  pl.BlockSpec((S, 128), lambda p, i: (0, vcb + p)),
                  pl.BlockSpec((blk, 128), lambda p, i: (i, dcb + p)),
                  pl.BlockSpec((2, blk, 1), lambda p, i: (p, i, 0))],
        out_specs=[pl.BlockSpec((blk, 128), lambda p, i: (i, p)),
                   pl.BlockSpec((S, 128), lambda p, i: (0, p)),
                   pl.BlockSpec((S, 128), lambda p, i: (0, p))],
        out_shape=[jax.ShapeDtypeStruct((S, GROUP_WIDTH), F32)] * 3,
        compiler_params=_cparams("arbitrary", "arbitrary"),
    )(h_att, h_att, h_att, dmix, ltot_arr)


HP = 4


def _kv_blocks_t(a, blk):
    S, C = a.shape
    return a.reshape(S // blk, blk, C).transpose(0, 2, 1)


def _smax_fwd_t(qT, k, vT3, *, dk, blk, name):
    S = k.shape[0]
    nb = S // blk
    H = k.shape[1] // dk

    def body(qT_ref, k_ref, vT_ref, oT_ref, lse_ref):
        i = pl.program_id(1)
        key = lax.broadcasted_iota(jnp.int32, (blk, blk), 0)
        qry = lax.broadcasted_iota(jnp.int32, (blk, blk), 1)
        qs = [qT_ref[h * dk:(h + 1) * dk, :] for h in range(HP)]

        def tile(j, carry, masked):
            r0 = pl.multiple_of(j * blk, blk)
            ss = [jnp.dot(k_ref[pl.ds(r0, blk), h * dk:(h + 1) * dk], qs[h], preferred_element_type=F32)
                  for h in range(HP)]
            stats, pes = [], []
            for h in range(HP):
                m, l, _ = carry[h]
                s = jnp.where(key <= qry, ss[h], NEG_INF) if masked else ss[h]
                mn = jnp.maximum(m, jnp.max(s, axis=0, keepdims=True))
                a = jnp.exp(m - mn)
                pe = jnp.exp(s - mn)
                stats.append((mn, a * l + jnp.sum(pe, axis=0, keepdims=True), a))
                pes.append(pe.astype(MXU_DTYPE))
            pvs = [jnp.dot(vT_ref[j, h * HEAD_DIM:(h + 1) * HEAD_DIM, :], pes[h], preferred_element_type=F32)
                   for h in range(HP)]
            return tuple((stats[h][0], stats[h][1], stats[h][2] * carry[h][2] + pvs[h]) for h in range(HP))

        init = tuple((jnp.full((1, blk), NEG_INF, F32), jnp.zeros((1, blk), F32), jnp.zeros((HEAD_DIM, blk), F32))
                     for _ in range(HP))
        carry = lax.fori_loop(0, i, functools.partial(tile, masked=False), init)
        carry = tile(i, carry, True)
        for h in range(HP):
            m, l, acc = carry[h]
            oT_ref[h * HEAD_DIM:(h + 1) * HEAD_DIM, :] = acc / l
            lse_ref[h, 0] = m + jnp.log(l)

    return pl.pallas_call(
        body, name=name, grid=(H // HP, nb),
        in_specs=[pl.BlockSpec((HP * dk, blk), lambda p, i: (p, i)),
                  pl.BlockSpec((S, HP * dk), lambda p, i: (0, p)),
                  pl.BlockSpec((nb, HP * HEAD_DIM, blk), lambda p, i: (0, p, 0))],
        out_specs=[pl.BlockSpec((HP * HEAD_DIM, blk), lambda p, i: (p, i)),
                   pl.BlockSpec((HP, 1, 1, blk), lambda p, i: (p, i, 0, 0))],
        out_shape=[jax.ShapeDtypeStruct((H * HEAD_DIM, S), F32), jax.ShapeDtypeStruct((H, nb, 1, blk), F32)],
        compiler_params=_cparams("arbitrary", "arbitrary"),
    )(qT, k, vT3)


def _smax_bwd_t(qT, q, k, kT3, v, dmix, dmixT, oT, lse, *, dk, dcb, qscale, blk, name):
    S = k.shape[0]
    nb = S // blk
    H = k.shape[1] // dk
    hd = HP * HEAD_DIM
    dcr = dcb * 128 // hd

    def body(qT_ref, q_ref, k_ref, kT_ref, v_ref, do_ref, doT_ref, oT_ref, lse_ref, dqT_ref, dk_ref, dv_ref):
        i = pl.program_id(1)

        @pl.when(i == 0)
        def _():
            dk_ref[...] = jnp.zeros_like(dk_ref)
            dv_ref[...] = jnp.zeros_like(dv_ref)

        key = lax.broadcasted_iota(jnp.int32, (blk, blk), 0)
        qry = lax.broadcasted_iota(jnp.int32, (blk, blk), 1)
        per_head = []
        for h in range(HP):
            hs = slice(h * HEAD_DIM, (h + 1) * HEAD_DIM)
            doT = doT_ref[hs, :]
            per_head.append(dict(
                qT=qT_ref[h * dk:(h + 1) * dk, :], q=q_ref[:, h * dk:(h + 1) * dk],
                doT=doT.astype(MXU_DTYPE), do=do_ref[:, hs].astype(MXU_DTYPE),
                delta=jnp.sum(doT * oT_ref[hs, :], axis=0, keepdims=True), lse=lse_ref[h, 0]))

        def tile(j, dqs, masked):
            r0 = pl.multiple_of(j * blk, blk)
            rows = pl.ds(r0, blk)
            ksl = [slice(h * dk, (h + 1) * dk) for h in range(HP)]
            hsl = [slice(h * HEAD_DIM, (h + 1) * HEAD_DIM) for h in range(HP)]
            ss = [jnp.dot(k_ref[rows, ksl[h]], per_head[h]["qT"], preferred_element_type=F32) for h in range(HP)]
            dps = [jnp.dot(v_ref[rows, hsl[h]], per_head[h]["doT"], preferred_element_type=F32) for h in range(HP)]
            prs, dss = [], []
            for h in range(HP):
                c = per_head[h]
                s = jnp.where(key <= qry, ss[h], NEG_INF) if masked else ss[h]
                pr = jnp.exp(s - c["lse"])
                dss.append((pr * (dps[h] - c["delta"])).astype(MXU_DTYPE))
                prs.append(pr.astype(MXU_DTYPE))
            for h in range(HP):
                dv_ref[rows, hsl[h]] += jnp.dot(prs[h], per_head[h]["do"], preferred_element_type=F32)
            for h in range(HP):
                dk_ref[rows, ksl[h]] += jnp.dot(dss[h], per_head[h]["q"], preferred_element_type=F32)
            return tuple(dqs[h] + jnp.dot(kT_ref[j, ksl[h], :], dss[h], preferred_element_type=F32) for h in range(HP))

        dqs = lax.fori_loop(0, i, functools.partial(tile, masked=False),
                            tuple(jnp.zeros((dk, blk), F32) for _ in range(HP)))
        dqs = tile(i, dqs, True)
        for h in range(HP):
            dqT_ref[h * dk:(h + 1) * dk, :] = dqs[h] * qscale

    return pl.pallas_call(
        body, name=name, grid=(H // HP, nb),
        in_specs=[pl.BlockSpec((HP * dk, blk), lambda p, i: (p, i)),
                  pl.BlockSpec((blk, HP * dk), lambda p, i: (i, p)),
                  pl.BlockSpec((S, HP * dk), lambda p, i: (0, p)),
                  pl.BlockSpec((nb, HP * dk, blk), lambda p, i: (0, p, 0)),
                  pl.BlockSpec((S, hd), lambda p, i: (0, p)),
                  pl.BlockSpec((blk, hd), lambda p, i: (i, dcr + p)),
                  pl.BlockSpec((hd, blk), lambda p, i: (dcr + p, i)),
                  pl.BlockSpec((hd, blk), lambda p, i: (p, i)),
                  pl.BlockSpec((HP, 1, 1, blk), lambda p, i: (p, i, 0, 0))],
        out_specs=[pl.BlockSpec((HP * dk, blk), lambda p, i: (p, i)),
                   pl.BlockSpec((S, HP * dk), lambda p, i: (0, p)),
                   pl.BlockSpec((S, hd), lambda p, i: (0, p))],
        out_shape=[jax.ShapeDtypeStruct((H * dk, S), F32), jax.ShapeDtypeStruct((S, H * dk), F32),
                   jax.ShapeDtypeStruct((S, H * HEAD_DIM), F32)],
        compiler_params=_cparams("arbitrary", "arbitrary"),
    )(qT, q, k, kT3, v, dmix, dmixT, oT, lse)


def _log1m_beta(z):
    return -(jnp.maximum(z, 0.0) + jnp.log(1.0 + jnp.exp(-jnp.abs(z))))


def _dot01_left(m01, x, parts=2):
    acc = None
    rem = x
    for _ in range(parts):
        part = rem.astype(MXU_DTYPE)
        rem = rem - part.astype(F32)
        t = jnp.dot(m01, part, preferred_element_type=F32)
        acc = t if acc is None else acc + t
    return acc


def _sb_fwd_t(qT, h_att, vT3, *, blk, name):
    S = h_att.shape[0]
    nb = S // blk
    kcb = COL_SK // (HP * HEAD_DIM)

    def body(qT_ref, k_ref, vT_ref, oT_ref, lt_ref):
        i = pl.program_id(1)
        key = lax.broadcasted_iota(jnp.int32, (blk, blk), 0)
        qry = lax.broadcasted_iota(jnp.int32, (blk, blk), 1)
        strict = key < qry
        later = (qry > key).astype(MXU_DTYPE)
        qs = [qT_ref[h * HEAD_DIM:(h + 1) * HEAD_DIM, :] for h in range(HP)]

        def tile(j, carry, mask):
            r0 = pl.multiple_of(j * blk, blk)
            hsl = [slice(h * HEAD_DIM, (h + 1) * HEAD_DIM) for h in range(HP)]
            zs = [jnp.dot(k_ref[pl.ds(r0, blk), hsl[h]], qs[h], preferred_element_type=F32) for h in range(HP)]
            lbs = []
            for h in range(HP):
                lb = _log1m_beta(zs[h])
                lbs.append(lb if mask is None else jnp.where(mask, lb, 0.0))
            sums = [_dot01_left(later, lbs[h]) for h in range(HP)]
            probs = []
            for h in range(HP):
                a = jnp.exp(zs[h] + lbs[h] + sums[h] + carry[h][0])
                probs.append((a if mask is None else jnp.where(mask, a, 0.0)).astype(MXU_DTYPE))
            pvs = [jnp.dot(vT_ref[j, hsl[h], :], probs[h], preferred_element_type=F32) for h in range(HP)]
            return tuple((carry[h][0] + jnp.sum(lbs[h], axis=0, keepdims=True), carry[h][1] + pvs[h]) for h in range(HP))

        init = tuple((jnp.zeros((1, blk), F32), jnp.zeros((HEAD_DIM, blk), F32)) for _ in range(HP))
        carry = tile(i, init, strict)
        carry = lax.fori_loop(0, i, lambda jj, c: tile(i - 1 - jj, c, None), carry)
        for h in range(HP):
            oT_ref[h * HEAD_DIM:(h + 1) * HEAD_DIM, :] = carry[h][1]
            lt_ref[h, 0] = carry[h][0]

    hd = HP * HEAD_DIM
    return pl.pallas_call(
        body, name=name, grid=(4 // HP, nb),
        in_specs=[pl.BlockSpec((hd, blk), lambda p, i: (p, i)),
                  pl.BlockSpec((S, hd), lambda p, i: (0, kcb + p)),
                  pl.BlockSpec((nb, hd, blk), lambda p, i: (0, p, 0))],
        out_specs=[pl.BlockSpec((hd, blk), lambda p, i: (p, i)), pl.BlockSpec((HP, 1, 1, blk), lambda p, i: (p, i, 0, 0))],
        out_shape=[jax.ShapeDtypeStruct((GROUP_WIDTH, S), F32), jax.ShapeDtypeStruct((4, nb, 1, blk), F32)],
        compiler_params=_cparams("arbitrary", "arbitrary"),
    )(qT, h_att, vT3)


def _sb_bwd_t(qT, h_att, kT3, dmix, dmixT, ltot, *, dcb, qscale, blk, name):
    S = h_att.shape[0]
    nb = S // blk
    hd = HP * HEAD_DIM
    qcb, kcb, vcb = COL_SQ // hd, COL_SK // hd, COL_SV // hd
    dcr = dcb * 128 // hd

    def body(qT_ref, q_ref, k_ref, kT_ref, v_ref, do_ref, doT_ref, lt_ref, dqT_ref, dk_ref, dv_ref):
        i = pl.program_id(1)

        @pl.when(i == 0)
        def _():
            dk_ref[...] = jnp.zeros_like(dk_ref)
            dv_ref[...] = jnp.zeros_like(dv_ref)

        key = lax.broadcasted_iota(jnp.int32, (blk, blk), 0)
        qry = lax.broadcasted_iota(jnp.int32, (blk, blk), 1)
        strict = key < qry
        upto = (qry <= key).astype(MXU_DTYPE)
        before = (qry < key).astype(MXU_DTYPE)
        per_head = []
        for h in range(HP):
            hs = slice(h * HEAD_DIM, (h + 1) * HEAD_DIM)
            per_head.append(dict(qT=qT_ref[hs, :], q=q_ref[:, hs], doT=doT_ref[hs, :].astype(MXU_DTYPE),
                                 do=do_ref[:, hs].astype(MXU_DTYPE), lt=lt_ref[h, 0]))

        def tile(j, carry, mask):
            r0 = pl.multiple_of(j * blk, blk)
            rows = pl.ds(r0, blk)
            hsl = [slice(h * HEAD_DIM, (h + 1) * HEAD_DIM) for h in range(HP)]
            zs = [jnp.dot(k_ref[rows, hsl[h]], per_head[h]["qT"], preferred_element_type=F32) for h in range(HP)]
            das = [jnp.dot(v_ref[rows, hsl[h]], per_head[h]["doT"], preferred_element_type=F32) for h in range(HP)]
            lbs = []
            for h in range(HP):
                lb = _log1m_beta(zs[h])
                lbs.append(lb if mask is None else jnp.where(mask, lb, 0.0))
            sums = [_dot01_left(upto, lbs[h]) for h in range(HP)]
            probs, gs = [], []
            for h in range(HP):
                a = jnp.exp(zs[h] + lbs[h] + (per_head[h]["lt"] - carry[h][0] - sums[h]))
                a = a if mask is None else jnp.where(mask, a, 0.0)
                gs.append(das[h] * a)
                probs.append(a.astype(MXU_DTYPE))
            for h in range(HP):
                dv_ref[rows, hsl[h]] += jnp.dot(probs[h], per_head[h]["do"], preferred_element_type=F32)
            es = [_dot01_left(before, gs[h]) for h in range(HP)]
            dzs = []
            for h in range(HP):
                dz = gs[h] * jnp.exp(lbs[h]) - (carry[h][1] + es[h]) * jnp.exp(zs[h] + lbs[h])
                dzs.append((dz if mask is None else jnp.where(mask, dz, 0.0)).astype(MXU_DTYPE))
            for h in range(HP):
                dk_ref[rows, hsl[h]] += jnp.dot(dzs[h], per_head[h]["q"], preferred_element_type=F32)
            return tuple((carry[h][0] + jnp.sum(lbs[h], axis=0, keepdims=True),
                          carry[h][1] + jnp.sum(gs[h], axis=0, keepdims=True),
                          carry[h][2] + jnp.dot(kT_ref[j, hsl[h], :], dzs[h], preferred_element_type=F32))
                         for h in range(HP))

        zr = jnp.zeros((1, blk), F32)
        init = tuple((zr, zr, jnp.zeros((HEAD_DIM, blk), F32)) for _ in range(HP))
        carry = lax.fori_loop(0, i, lambda j, c: tile(j, c, None), init)
        carry = tile(i, carry, strict)
        for h in range(HP):
            dqT_ref[h * HEAD_DIM:(h + 1) * HEAD_DIM, :] = carry[h][2] * qscale

    return pl.pallas_call(
        body, name=name, grid=(4 // HP, nb),
        in_specs=[pl.BlockSpec((hd, blk), lambda p, i: (p, i)),
                  pl.BlockSpec((blk, hd), lambda p, i: (i, qcb + p)),
                  pl.BlockSpec((S, hd), lambda p, i: (0, kcb + p)),
                  pl.BlockSpec((nb, hd, blk), lambda p, i: (0, p, 0)),
                  pl.BlockSpec((S, hd), lambda p, i: (0, vcb + p)),
                  pl.BlockSpec((blk, hd), lambda p, i: (i, dcr + p)),
                  pl.BlockSpec((hd, blk), lambda p, i: (dcr + p, i)),
                  pl.BlockSpec((HP, 1, 1, blk), lambda p, i: (p, i, 0, 0))],
        out_specs=[pl.BlockSpec((hd, blk), lambda p, i: (p, i)),
                   pl.BlockSpec((S, hd), lambda p, i: (0, p)),
                   pl.BlockSpec((S, hd), lambda p, i: (0, p))],
        out_shape=[jax.ShapeDtypeStruct((GROUP_WIDTH, S), F32), jax.ShapeDtypeStruct((S, GROUP_WIDTH), F32),
                   jax.ShapeDtypeStruct((S, GROUP_WIDTH), F32)],
        compiler_params=_cparams("arbitrary", "arbitrary"),
    )(qT, h_att, h_att, kT3, h_att, dmix, dmixT, ltot)


def _swa_scores(q_ref, k_ref, n, h, start):
    g = h // 2
    kb = k_ref[pl.ds(start, 2 * WINDOW), g * HEAD_DIM:(g + 1) * HEAD_DIM]
    s = lax.dot_general(q_ref[:, h * HEAD_DIM:(h + 1) * HEAD_DIM], kb, NT, preferred_element_type=F32) * (HEAD_DIM ** -0.5)
    dist = (n * WINDOW + lax.broadcasted_iota(jnp.int32, (WINDOW, 2 * WINDOW), 0)
            - start - lax.broadcasted_iota(jnp.int32, (WINDOW, 2 * WINDOW), 1))
    s = s - SWA_SLOPES[h] * dist.astype(F32)
    valid = (dist >= 0) & (dist < WINDOW)
    return jnp.where(valid, s, NEG_INF), kb


def _swa_fwd(h_att, sinks, *, name):
    S = h_att.shape[0]
    nb = S // WINDOW
    qcb, kcb, vcb = COL_WQ // 256, COL_WK // 128, COL_WV // 128

    def body(sink_ref, q_ref, k_ref, v_ref, o_ref, lse_ref):
        n = pl.program_id(0)
        start = pl.multiple_of(jnp.maximum(n - 1, 0) * WINDOW, WINDOW)
        scores = [_swa_scores(q_ref, k_ref, n, h, start)[0] for h in range(4)]
        probs = []
        for h in range(4):
            sink = sink_ref[h]
            m = jnp.maximum(jnp.max(scores[h], axis=1, keepdims=True), sink)
            e = jnp.exp(scores[h] - m)
            den = jnp.sum(e, axis=1, keepdims=True) + jnp.exp(sink - m)
            probs.append((e / den).astype(MXU_DTYPE))
            lse_ref[h] = m + jnp.log(den)
        for h in range(4):
            vb = v_ref[pl.ds(start, 2 * WINDOW), (h // 2) * HEAD_DIM:(h // 2 + 1) * HEAD_DIM]
            o_ref[:, h * HEAD_DIM:(h + 1) * HEAD_DIM] = jnp.dot(probs[h], vb, preferred_element_type=F32)

    return pl.pallas_call(
        body, name=name, grid=(nb,),
        in_specs=[pl.BlockSpec(memory_space=pltpu.SMEM),
                  pl.BlockSpec((WINDOW, 256), lambda n: (n, qcb)),
                  pl.BlockSpec((S, 128), lambda n: (0, kcb)),
                  pl.BlockSpec((S, 128), lambda n: (0, vcb))],
        out_specs=[pl.BlockSpec((WINDOW, 256), lambda n: (n, 0)), pl.BlockSpec((4, WINDOW, 1), lambda n: (0, n, 0))],
        out_shape=[jax.ShapeDtypeStruct((S, GROUP_WIDTH), F32), jax.ShapeDtypeStruct((4, S, 1), F32)],
        compiler_params=_cparams("arbitrary"),
    )(sinks, h_att, h_att, h_att)


def _swa_bwd(h_att, sinks, dmix, o_arr, lse, *, dcb, name):
    S = h_att.shape[0]
    nb = S // WINDOW
    qcb, kcb, vcb = COL_WQ // 256, COL_WK // 128, COL_WV // 128

    def body(sink_ref, q_ref, k_ref, v_ref, do_ref, o_ref, lse_ref, dq_ref, dk_ref, dv_ref, dsink_ref):
        n = pl.program_id(0)

        @pl.when(n == 0)
        def _():
            dk_ref[...] = jnp.zeros_like(dk_ref)
            dv_ref[...] = jnp.zeros_like(dv_ref)
            dsink_ref[...] = jnp.zeros_like(dsink_ref)

        start = pl.multiple_of(jnp.maximum(n - 1, 0) * WINDOW, WINDOW)
        rows = pl.ds(start, 2 * WINDOW)
        hsl = [slice(h * HEAD_DIM, (h + 1) * HEAD_DIM) for h in range(4)]
        gsl = [slice(g * HEAD_DIM, (g + 1) * HEAD_DIM) for g in range(2)]
        scale = HEAD_DIM ** -0.5
        sk = [_swa_scores(q_ref, k_ref, n, h, start) for h in range(4)]
        dobs = [do_ref[:, hsl[h]].astype(MXU_DTYPE) for h in range(4)]
        dps = [lax.dot_general(dobs[h], v_ref[rows, gsl[h // 2]], NT, preferred_element_type=F32) for h in range(4)]
        prs, dss = [], []
        for h in range(4):
            lse_h = lse_ref[h]
            pr = jnp.exp(sk[h][0] - lse_h)
            delta = jnp.sum(do_ref[:, hsl[h]] * o_ref[:, hsl[h]], axis=1, keepdims=True)
            dss.append((pr * (dps[h] - delta)).astype(MXU_DTYPE))
            prs.append(pr.astype(MXU_DTYPE))
            dsink_ref[h:h + 1, :] += jnp.zeros((1, 128), F32) - jnp.sum(jnp.exp(sink_ref[h] - lse_h) * delta)
        for h in range(4):
            dq_ref[:, hsl[h]] = jnp.dot(dss[h], sk[h][1], preferred_element_type=F32) * scale
        for g in range(2):
            dk_ref[rows, gsl[g]] += (lax.dot_general(dss[2 * g], q_ref[:, hsl[2 * g]], TN, preferred_element_type=F32)
                                     + lax.dot_general(dss[2 * g + 1], q_ref[:, hsl[2 * g + 1]], TN,
                                                       preferred_element_type=F32)) * scale
            dv_ref[rows, gsl[g]] += (lax.dot_general(prs[2 * g], dobs[2 * g], TN, preferred_element_type=F32)
                                     + lax.dot_general(prs[2 * g + 1], dobs[2 * g + 1], TN, preferred_element_type=F32))

    return pl.pallas_call(
        body, name=name, grid=(nb,),
        in_specs=[pl.BlockSpec(memory_space=pltpu.SMEM),
                  pl.BlockSpec((WINDOW, 256), lambda n: (n, qcb)),
                  pl.BlockSpec((S, 128), lambda n: (0, kcb)),
                  pl.BlockSpec((S, 128), lambda n: (0, vcb)),
                  pl.BlockSpec((WINDOW, 256), lambda n: (n, dcb)),
                  pl.BlockSpec((WINDOW, 256), lambda n: (n, 0)),
                  pl.BlockSpec((4, WINDOW, 1), lambda n: (0, n, 0))],
        out_specs=[pl.BlockSpec((WINDOW, 256), lambda n: (n, 0)),
                   pl.BlockSpec((S, 128), lambda n: (0, 0)),
                   pl.BlockSpec((S, 128), lambda n: (0, 0)),
                   pl.BlockSpec((4, 128), lambda n: (0, 0))],
        out_shape=[jax.ShapeDtypeStruct((S, GROUP_WIDTH), F32), jax.ShapeDtypeStruct((S, 128), F32),
                   jax.ShapeDtypeStruct((S, 128), F32), jax.ShapeDtypeStruct((4, 128), F32)],
        compiler_params=_cparams("arbitrary"),
    )(sinks, h_att, h_att, h_att, dmix, o_arr, lse)


def _tri(n, incl, upper):
    r = lax.broadcasted_iota(jnp.int32, (n, n), 0)
    c = lax.broadcasted_iota(jnp.int32, (n, n), 1)
    if upper:
        m = (r <= c) if incl else (r < c)
    else:
        m = (r >= c) if incl else (r > c)
    return m.astype(MXU_DTYPE)


def _fox_gate_fwd(fg, b_f, *, name):
    _, R, _ = fg.shape

    def body(b_ref, fg_ref, pos_ref, neg_ref):
        up_incl = _tri(128, True, True)
        ones = jnp.ones((128, 128), MXU_DTYPE)
        for h in range(4):
            z = fg_ref[h] + b_ref[h]
            logf = jnp.minimum(z, 0.0) - jnp.log(1.0 + jnp.exp(-jnp.abs(z)))
            within = _dot01(logf, up_incl, parts=3)
            totals = _dot01(logf, ones, parts=3)
            rem = within + _rows_other(totals, R, after=False)
            for part in range(3):
                piece = rem.astype(MXU_DTYPE)
                rem = rem - piece.astype(F32)
                pos_ref[h, part] = piece
                neg_ref[h, part] = -piece

    shape = (4, 3) + fg.shape[1:]
    return pl.pallas_call(
        body, name=name,
        in_specs=[pl.BlockSpec(memory_space=pltpu.SMEM), pl.BlockSpec(memory_space=pltpu.VMEM)],
        out_specs=[pl.BlockSpec(memory_space=pltpu.VMEM)] * 2,
        out_shape=[jax.ShapeDtypeStruct(shape, MXU_DTYPE)] * 2,
    )(b_f, fg)


def _rows_other(totals, n, after):
    r = lax.broadcasted_iota(jnp.int32, (n, n), 0)
    c = lax.broadcasted_iota(jnp.int32, (n, n), 1)
    m = ((c > r) if after else (c < r)).astype(MXU_DTYPE)
    acc = None
    rem = totals
    for _ in range(3):
        part = rem.astype(MXU_DTYPE)
        rem = rem - part.astype(F32)
        t = jnp.dot(m, part, preferred_element_type=F32)
        acc = t if acc is None else acc + t
    return acc


def _fox_gate_bwd(fg, b_f, dcum_k, dcum_q, *, q_unscale, name):
    _, R, _ = fg.shape

    def body(b_ref, fg_ref, dck_ref, dcq_ref, dfg_ref, db_ref):
        low_incl = _tri(128, True, False)
        ones = jnp.ones((128, 128), MXU_DTYPE)
        for h in range(4):
            dc = dcq_ref[h] * q_unscale - dck_ref[h]
            dlogf = _dot01(dc, low_incl, parts=3) + _rows_other(_dot01(dc, ones, parts=3), R, after=True)
            z = fg_ref[h] + b_ref[h]
            dz = dlogf * jnp.exp(jnp.minimum(-z, 0.0) - jnp.log(1.0 + jnp.exp(-jnp.abs(z))))
            dfg_ref[h] = dz
            db_ref[h:h + 1, :] = jnp.zeros((1, 128), F32) + jnp.sum(dz)

    return pl.pallas_call(
        body, name=name,
        in_specs=[pl.BlockSpec(memory_space=pltpu.SMEM)] + [pl.BlockSpec(memory_space=pltpu.VMEM)] * 3,
        out_specs=[pl.BlockSpec(memory_space=pltpu.VMEM), pl.BlockSpec(memory_space=pltpu.VMEM)],
        out_shape=[jax.ShapeDtypeStruct(fg.shape, F32), jax.ShapeDtypeStruct((4, 128), F32)],
    )(b_f, fg, dcum_k, dcum_q)


def _rope_rot(transpose):
    r = lax.broadcasted_iota(jnp.int32, (MLA_PAD, MLA_PAD), 0)
    c = lax.broadcasted_iota(jnp.int32, (MLA_PAD, MLA_PAD), 1)
    if transpose:
        r, c = c, r
    half = MLA_ROPE // 2
    lo, mid, hi = HEAD_DIM, HEAD_DIM + half, HEAD_DIM + MLA_ROPE
    minus = (c >= lo) & (c < mid) & (r == c + half)
    plus = (c >= mid) & (c < hi) & (r == c - half)
    return jnp.where(plus, 1.0, jnp.where(minus, -1.0, 0.0)).astype(MXU_DTYPE)


def _rope_lanes():
    lane = lax.broadcasted_iota(jnp.int32, (1, MLA_PAD), 1)
    return ((lane >= HEAD_DIM) & (lane < HEAD_DIM + MLA_ROPE)).astype(F32)


def _rms(x, g, eps=1e-6):
    r = lax.rsqrt(jnp.mean(x * x, axis=-1, keepdims=True) + eps)
    return x * r * g, r


def _rms_bwd(dy, x, r, g):
    xh = x * r
    dxh = dy * g
    dx = r * (dxh - xh * jnp.mean(dxh * xh, axis=-1, keepdims=True))
    return dx, dy * xh


def _mla_prep_fwd(lat, g_q, g_kv, wuq, wuk, wuv, cosm, sinm, *, bs, name):
    S = lat.shape[0]

    def body(lat_ref, gq_ref, gkv_ref, wuq_ref, wuk_ref, wuv_ref, cos_ref, sin_ref,
             q_ref, k_ref, v_ref, qT_ref, kT_ref, vT_ref):
        rot = _rope_rot(False)
        cosm_, sinm_ = cos_ref[...], sin_ref[...]
        nq, _ = _rms(lat_ref[:, 0:MLA_Q_RANK], gq_ref[...])
        nkv, _ = _rms(lat_ref[:, MLA_Q_RANK:MLA_Q_RANK + MLA_KV_RANK], gkv_ref[...])
        qlat = jnp.dot(nq.astype(MXU_DTYPE), wuq_ref[...], preferred_element_type=F32)
        klat = jnp.dot(nkv.astype(MXU_DTYPE), wuk_ref[...], preferred_element_type=F32)
        v = jnp.dot(nkv.astype(MXU_DTYPE), wuv_ref[...], preferred_element_type=F32)
        v_ref[...] = v.astype(v_ref.dtype)
        vT_ref[0] = v.T.astype(vT_ref.dtype)
        krb = lat_ref[:, 384:512]
        kr = krb * (cosm_ * _rope_lanes()) + _dot01(krb, rot, parts=3) * sinm_
        for h in range(4):
            sl = slice(h * MLA_PAD, (h + 1) * MLA_PAD)
            qh = qlat[:, sl]
            q = (qh * cosm_ + _dot01(qh, rot, parts=3) * sinm_) * (MLA_QK ** -0.5)
            k = klat[:, sl] + kr
            q_ref[:, sl] = q.astype(q_ref.dtype)
            k_ref[:, sl] = k.astype(k_ref.dtype)
            qT_ref[sl, :] = q.T.astype(qT_ref.dtype)
            kT_ref[0, sl, :] = k.T.astype(kT_ref.dtype)

    full = lambda a: pl.BlockSpec(a.shape, lambda i: (0,) * a.ndim)
    return pl.pallas_call(
        body, name=name, grid=(S // bs,),
        in_specs=[pl.BlockSpec((bs, LAT_W), lambda i: (i, 0)), full(g_q), full(g_kv), full(wuq), full(wuk), full(wuv),
                  pl.BlockSpec((bs, MLA_PAD), lambda i: (i, 0)), pl.BlockSpec((bs, MLA_PAD), lambda i: (i, 0))],
        out_specs=[pl.BlockSpec((bs, 512), lambda i: (i, 0)), pl.BlockSpec((bs, 512), lambda i: (i, 0)),
                   pl.BlockSpec((bs, 256), lambda i: (i, 0)), pl.BlockSpec((512, bs), lambda i: (0, i)),
                   pl.BlockSpec((1, 512, bs), lambda i: (i, 0, 0)), pl.BlockSpec((1, 256, bs), lambda i: (i, 0, 0))],
        out_shape=[jax.ShapeDtypeStruct((S, 512), MXU_DTYPE), jax.ShapeDtypeStruct((S, 512), MXU_DTYPE),
                   jax.ShapeDtypeStruct((S, 256), MXU_DTYPE), jax.ShapeDtypeStruct((512, S), MXU_DTYPE),
                   jax.ShapeDtypeStruct((S // bs, 512, bs), MXU_DTYPE), jax.ShapeDtypeStruct((S // bs, 256, bs), MXU_DTYPE)],
        compiler_params=_cparams("parallel"),
    )(lat, g_q, g_kv, wuq, wuk, wuv, cosm, sinm)


def _mla_prep_bwd(lat, g_q, g_kv, wuq, wuk, wuv, cosm, sinm, dq, dk, dv, *, bs, name):
    S = lat.shape[0]

    def body(lat_ref, gq_ref, gkv_ref, wuq_ref, wuk_ref, wuv_ref, cos_ref, sin_ref, dq_ref, dk_ref, dv_ref,
             dlat_ref, dwuq_ref, dwuk_ref, dwuv_ref, dgq_ref, dgkv_ref):
        @pl.when(pl.program_id(0) == 0)
        def _():
            for r in (dwuq_ref, dwuk_ref, dwuv_ref, dgq_ref, dgkv_ref):
                r[...] = jnp.zeros_like(r)

        rot_t = _rope_rot(True)
        cosm_, sinm_ = cos_ref[...], sin_ref[...]
        cq = lat_ref[:, 0:MLA_Q_RANK]
        ckv = lat_ref[:, MLA_Q_RANK:MLA_Q_RANK + MLA_KV_RANK]
        nq, rq = _rms(cq, gq_ref[...])
        nkv, rkv = _rms(ckv, gkv_ref[...])
        nqb, nkvb = nq.astype(MXU_DTYPE), nkv.astype(MXU_DTYPE)

        dqlat = []
        dkr = jnp.zeros((bs, MLA_PAD), F32)
        for h in range(4):
            sl = slice(h * MLA_PAD, (h + 1) * MLA_PAD)
            dqh = dq_ref[sl, :].T
            dqlat.append(dqh * cosm_ + _dot01(dqh * sinm_, rot_t, parts=3))
            dkr = dkr + dk_ref[:, sl]
        dqlat = jnp.concatenate(dqlat, axis=1).astype(MXU_DTYPE)
        dkb = dk_ref[...].astype(MXU_DTYPE)
        dvb = dv_ref[...].astype(MXU_DTYPE)

        dnq = lax.dot_general(dqlat, wuq_ref[...], NT, preferred_element_type=F32)
        dnkv = (lax.dot_general(dkb, wuk_ref[...], NT, preferred_element_type=F32)
                + lax.dot_general(dvb, wuv_ref[...], NT, preferred_element_type=F32))
        dwuq_ref[...] += lax.dot_general(nqb, dqlat, TN, preferred_element_type=F32)
        dwuk_ref[...] += lax.dot_general(nkvb, dkb, TN, preferred_element_type=F32)
        dwuv_ref[...] += lax.dot_general(nkvb, dvb, TN, preferred_element_type=F32)
        dcq, tq = _rms_bwd(dnq, cq, rq, gq_ref[...])
        dckv, tkv = _rms_bwd(dnkv, ckv, rkv, gkv_ref[...])
        dgq_ref[...] += jnp.sum(tq, axis=0, keepdims=True)
        dgkv_ref[...] += jnp.sum(tkv, axis=0, keepdims=True)
        dlat_ref[:, 0:MLA_Q_RANK] = dcq.astype(dlat_ref.dtype)
        dlat_ref[:, MLA_Q_RANK:MLA_Q_RANK + MLA_KV_RANK] = dckv.astype(dlat_ref.dtype)
        dkrb = dkr * (cosm_ * _rope_lanes()) + _dot01(dkr * sinm_, rot_t, parts=3)
        dlat_ref[:, 384:512] = dkrb.astype(dlat_ref.dtype)

    full = lambda a: pl.BlockSpec(a.shape, lambda i: (0,) * a.ndim)
    row = lambda w: pl.BlockSpec((bs, w), lambda i: (i, 0))
    acc = lambda *shape: pl.BlockSpec(shape, lambda i: (0,) * len(shape))
    return pl.pallas_call(
        body, name=name, grid=(S // bs,),
        in_specs=[row(LAT_W), full(g_q), full(g_kv), full(wuq), full(wuk), full(wuv), row(MLA_PAD), row(MLA_PAD),
                  pl.BlockSpec((512, bs), lambda i: (0, i)), row(512), row(256)],
        out_specs=[row(512), acc(256, 512), acc(128, 512), acc(128, 256), acc(1, 256), acc(1, 128)],
        out_shape=[jax.ShapeDtypeStruct((S, 512), MXU_DTYPE), jax.ShapeDtypeStruct((256, 512), F32),
                   jax.ShapeDtypeStruct((128, 512), F32), jax.ShapeDtypeStruct((128, 256), F32),
                   jax.ShapeDtypeStruct((1, 256), F32), jax.ShapeDtypeStruct((1, 128), F32)],
        compiler_params=_cparams("arbitrary"),
    )(lat, g_q, g_kv, wuq, wuk, wuv, cosm, sinm, dq, dk, dv)


def _row_spec(bs, w):
    return pl.BlockSpec((bs, w), lambda i: (i, 0))


def _vec_spec(w):
    return pl.BlockSpec((1, w), lambda i: (0, 0))


def _mix_specs(bs):
    return [pl.BlockSpec((GROUP_WIDTH, bs), lambda i: (0, i))] * 3 + [_row_spec(bs, GROUP_WIDTH)]


def _mix_groups(a_ref, b_ref, c_ref, d_ref):
    return [a_ref[...].T, b_ref[...].T, c_ref[...].T, d_ref[...]]


def _gnorm_fwd(outs, g, *, bs, name):
    S = outs[3].shape[0]

    def body(a_ref, b_ref, c_ref, d_ref, g_ref, o_ref, oT_ref):
        for k, x in enumerate(_mix_groups(a_ref, b_ref, c_ref, d_ref)):
            sl = slice(k * GROUP_WIDTH, (k + 1) * GROUP_WIDTH)
            y, _ = _rms(x, g_ref[:, sl])
            o_ref[:, sl] = y.astype(o_ref.dtype)
            oT_ref[sl, :] = y.T.astype(oT_ref.dtype)

    return pl.pallas_call(
        body, name=name, grid=(S // bs,),
        in_specs=_mix_specs(bs) + [_vec_spec(D_MODEL)],
        out_specs=[_row_spec(bs, D_MODEL), pl.BlockSpec((D_MODEL, bs), lambda i: (0, i))],
        out_shape=[jax.ShapeDtypeStruct((S, D_MODEL), MXU_DTYPE), jax.ShapeDtypeStruct((D_MODEL, S), MXU_DTYPE)],
        compiler_params=_cparams("parallel"),
    )(*outs, g)


def _gnorm_bwd(dgn, outs, g, *, bs, name):
    S = dgn.shape[0]

    def body(dgn_ref, a_ref, b_ref, c_ref, d_ref, g_ref, dmix_ref, dmixT_ref, dg_ref):
        @pl.when(pl.program_id(0) == 0)
        def _():
            dg_ref[...] = jnp.zeros_like(dg_ref)

        for k, x in enumerate(_mix_groups(a_ref, b_ref, c_ref, d_ref)):
            sl = slice(k * GROUP_WIDTH, (k + 1) * GROUP_WIDTH)
            _, r = _rms(x, g_ref[:, sl])
            dx, t = _rms_bwd(dgn_ref[:, sl], x, r, g_ref[:, sl])
            dmix_ref[:, sl] = dx
            dmixT_ref[sl, :] = dx.T
            dg_ref[:, sl] += jnp.sum(t, axis=0, keepdims=True)

    return pl.pallas_call(
        body, name=name, grid=(S // bs,),
        in_specs=[_row_spec(bs, D_MODEL)] + _mix_specs(bs) + [_vec_spec(D_MODEL)],
        out_specs=[_row_spec(bs, D_MODEL), pl.BlockSpec((D_MODEL, bs), lambda i: (0, i)), _vec_spec(D_MODEL)],
        out_shape=[jax.ShapeDtypeStruct((S, D_MODEL), F32), jax.ShapeDtypeStruct((D_MODEL, S), F32),
                   jax.ShapeDtypeStruct((1, D_MODEL), F32)],
        compiler_params=_cparams("arbitrary"),
    )(dgn, *outs, g)


def _ln_fwd(u, g, b, *, bs, name):
    S = u.shape[0]

    def body(u_ref, g_ref, b_ref, y_ref, yb_ref, ybT_ref, xh_ref, rs_ref):
        x = u_ref[...]
        mu = jnp.mean(x, axis=-1, keepdims=True)
        xc = x - mu
        rs = lax.rsqrt(jnp.mean(xc * xc, axis=-1, keepdims=True) + 1e-5)
        xh = xc * rs
        y = xh * g_ref[...] + b_ref[...]
        y_ref[...] = y
        yb_ref[...] = y.astype(yb_ref.dtype)
        ybT_ref[...] = y.T.astype(ybT_ref.dtype)
        xh_ref[...] = xh
        rs_ref[...] = rs

    return pl.pallas_call(
        body, name=name, grid=(S // bs,),
        in_specs=[_row_spec(bs, D_MODEL), _vec_spec(D_MODEL), _vec_spec(D_MODEL)],
        out_specs=[_row_spec(bs, D_MODEL), _row_spec(bs, D_MODEL), pl.BlockSpec((D_MODEL, bs), lambda i: (0, i)),
                   _row_spec(bs, D_MODEL), _row_spec(bs, 1)],
        out_shape=[jax.ShapeDtypeStruct((S, D_MODEL), F32), jax.ShapeDtypeStruct((S, D_MODEL), MXU_DTYPE),
                   jax.ShapeDtypeStruct((D_MODEL, S), MXU_DTYPE), jax.ShapeDtypeStruct((S, D_MODEL), F32),
                   jax.ShapeDtypeStruct((S, 1), F32)],
        compiler_params=_cparams("parallel"),
    )(u, g, b)


def _ln_bwd(dy, xh, rs, g, *, bs, name):
    S = dy.shape[0]

    def body(dy_ref, xh_ref, rs_ref, g_ref, du_ref, dub_ref, dg_ref, db_ref):
        @pl.when(pl.program_id(0) == 0)
        def _():
            dg_ref[...] = jnp.zeros_like(dg_ref)
            db_ref[...] = jnp.zeros_like(db_ref)

        dy_, xh_ = dy_ref[...], xh_ref[...]
        dxh = dy_ * g_ref[...]
        du = rs_ref[...] * (dxh - jnp.mean(dxh, axis=-1, keepdims=True)
                            - xh_ * jnp.mean(dxh * xh_, axis=-1, keepdims=True))
        du_ref[...] = du
        dub_ref[...] = du.astype(dub_ref.dtype)
        dg_ref[...] += jnp.sum(dy_ * xh_, axis=0, keepdims=True)
        db_ref[...] += jnp.sum(dy_, axis=0, keepdims=True)

    return pl.pallas_call(
        body, name=name, grid=(S // bs,),
        in_specs=[_row_spec(bs, D_MODEL), _row_spec(bs, D_MODEL), _row_spec(bs, 1), _vec_spec(D_MODEL)],
        out_specs=[_row_spec(bs, D_MODEL), _row_spec(bs, D_MODEL), _vec_spec(D_MODEL), _vec_spec(D_MODEL)],
        out_shape=[jax.ShapeDtypeStruct((S, D_MODEL), F32), jax.ShapeDtypeStruct((S, D_MODEL), MXU_DTYPE),
                   jax.ShapeDtypeStruct((1, D_MODEL), F32), jax.ShapeDtypeStruct((1, D_MODEL), F32)],
        compiler_params=_cparams("arbitrary"),
    )(dy, xh, rs, g)


def _swiglu_fwd(gu, *, bs, name):
    S = gu.shape[0]

    def body(gu_ref, a_ref, aT_ref):
        gt = gu_ref[:, :D_FF]
        a = gt / (1.0 + jnp.exp(-gt)) * gu_ref[:, D_FF:]
        a_ref[...] = a.astype(a_ref.dtype)
        aT_ref[...] = a.T.astype(aT_ref.dtype)

    return pl.pallas_call(
        body, name=name, grid=(S // bs,),
        in_specs=[_row_spec(bs, 2 * D_FF)],
        out_specs=[_row_spec(bs, D_FF), pl.BlockSpec((D_FF, bs), lambda i: (0, i))],
        out_shape=[jax.ShapeDtypeStruct((S, D_FF), MXU_DTYPE), jax.ShapeDtypeStruct((D_FF, S), MXU_DTYPE)],
        compiler_params=_cparams("parallel"),
    )(gu)


def _swiglu_bwd(da, gu, *, bs, name):
    S = gu.shape[0]

    def body(da_ref, gu_ref, dgu_ref):
        gt, da_ = gu_ref[:, :D_FF], da_ref[...]
        sg = 1.0 / (1.0 + jnp.exp(-gt))
        silu = gt * sg
        dgu_ref[:, :D_FF] = (da_ * gu_ref[:, D_FF:] * (sg + silu * (1.0 - sg))).astype(dgu_ref.dtype)
        dgu_ref[:, D_FF:] = (da_ * silu).astype(dgu_ref.dtype)

    return pl.pallas_call(
        body, name=name, grid=(S // bs,),
        in_specs=[_row_spec(bs, D_FF), _row_spec(bs, 2 * D_FF)],
        out_specs=_row_spec(bs, 2 * D_FF), out_shape=jax.ShapeDtypeStruct((S, 2 * D_FF), MXU_DTYPE),
        compiler_params=_cparams("parallel"),
    )(da, gu)


def _loss_head(y, target, *, bs, name):
    S = y.shape[0]

    def body(y_ref, t_ref, dy_ref, loss_ref):
        @pl.when(pl.program_id(0) == 0)
        def _():
            loss_ref[...] = jnp.zeros_like(loss_ref)

        e = y_ref[...] - t_ref[...]
        dy_ref[...] = e * (1.0 / D_MODEL)
        per_tok = jnp.mean(e * e, axis=-1, keepdims=True)
        loss_ref[...] += 0.5 * jnp.sum(per_tok, axis=0, keepdims=True)

    return pl.pallas_call(
        body, name=name, grid=(S // bs,),
        in_specs=[_row_spec(bs, D_MODEL), _row_spec(bs, D_MODEL)],
        out_specs=[_row_spec(bs, D_MODEL), pl.BlockSpec((1, 1), lambda i: (0, 0))],
        out_shape=[jax.ShapeDtypeStruct((S, D_MODEL), F32), jax.ShapeDtypeStruct((1, 1), F32)],
        compiler_params=_cparams("arbitrary"),
    )(y, target)


def _blk(n, target):
    if n <= target:
        return n
    best = None
    for b in range(128, target + 1, 128):
        if n % b == 0:
            best = b
    assert best is not None, n
    return best


def _rope_tables(S):
    pos = jnp.arange(S, dtype=F32)
    inv = ROPE_THETA ** (-jnp.arange(0, MLA_ROPE, 2, dtype=F32) / MLA_ROPE)
    ang = pos[:, None] * inv[None, :]
    cos, sin = jnp.cos(ang), jnp.sin(ang)
    one, zero, pad = jnp.ones((S, HEAD_DIM), F32), jnp.zeros((S, HEAD_DIM), F32), jnp.zeros((S, MLA_PAD - MLA_QK), F32)
    return jnp.concatenate([one, cos, cos, pad], axis=1), jnp.concatenate([zero, sin, sin, pad], axis=1)


def _prep_weights_a(w_in, w_uq, w_ukv):
    z = lambda n: jnp.zeros((D_MODEL, n), w_in.dtype)
    win_a = jnp.concatenate([w_in[:, 0:768], w_in[:, 1188:2468]], axis=1)
    win_l = jnp.concatenate([w_in[:, 772:1156], z(64), w_in[:, 1156:1188], z(32), w_in[:, 768:772], z(124)], axis=1)
    kv = w_ukv.reshape(MLA_KV_RANK, 4, 2 * HEAD_DIM)
    return dict(
        win_a=win_a, win_l=win_l, win_p=jnp.concatenate([win_a, win_l], axis=1),
        wuq=jnp.pad(w_uq.reshape(MLA_Q_RANK, 4, MLA_QK), ((0, 0), (0, 0), (0, MLA_PAD - MLA_QK))).reshape(MLA_Q_RANK, 512),
        wuk=jnp.pad(kv[:, :, :HEAD_DIM], ((0, 0), (0, 0), (0, HEAD_DIM))).reshape(MLA_KV_RANK, 512),
        wuv=kv[:, :, HEAD_DIM:].reshape(MLA_KV_RANK, 256))


def _prep_weights_b(w_o, w_gate, w_up, w_down):
    return dict(w_o=w_o, wgu=jnp.concatenate([w_gate, w_up], axis=1), w_down=w_down)


def _unprep_grads(dwin_p, dwuq, dwuk, dwuv, dwo, dwgu, dwd):
    dw_in = jnp.concatenate([dwin_p[:, 0:768], dwin_p[:, 2560:2564], dwin_p[:, 2048:2432], dwin_p[:, 2496:2528],
                             dwin_p[:, 768:2048]], axis=1)
    dw_uq = dwuq.reshape(MLA_Q_RANK, 4, MLA_PAD)[:, :, :MLA_QK].reshape(MLA_Q_RANK, 4 * MLA_QK)
    dw_ukv = jnp.concatenate([dwuk.reshape(MLA_KV_RANK, 4, MLA_PAD)[:, :, :HEAD_DIM],
                              dwuv.reshape(MLA_KV_RANK, 4, HEAD_DIM)], axis=2).reshape(MLA_KV_RANK, 512)
    return dict(w_in=dw_in, mla_w_uq=dw_uq, mla_w_ukv=dw_ukv, w_o=dwo, w_gate=dwgu[:, :D_FF], w_up=dwgu[:, D_FF:],
                w_down=dwd)


def _layer_fwd(l, x, xb, xbT, W, P, tabs, blk, late_weights=None):
    S = x.shape[0]
    nb = S // blk
    n = lambda s: f"l{l}_{s}"
    bs = min(512, S)
    h_att = _mm(xb, W["win_a"], name=n("in_att"), out_dtype=MXU_DTYPE, bm=1024, bn=1024, bk=1024, colscale=Q_COLSCALE)
    lat = _mm(xb, W["win_l"], name=n("in_lat"), bm=2048, bn=LAT_W, bk=1024)
    fg = lat[:, 512:516].T.reshape(4, S // 128, 128)
    cpos, cneg = _fox_gate_fwd(fg, P["fox_b_f"], name=n("fox_gate"))
    one3 = jnp.ones((S, 4, 3), MXU_DTYPE)
    zpad = jnp.zeros((S, 4, MLA_PAD - HEAD_DIM - 6), MXU_DTYPE)
    per_tok = lambda parts: parts.reshape(4, 3, S).transpose(2, 0, 1)
    q_f = jnp.concatenate([h_att[:, COL_FQ:COL_FQ + 256].reshape(S, 4, HEAD_DIM), per_tok(cpos), one3, zpad],
                          axis=2).reshape(S, 4 * MLA_PAD)
    k_f = jnp.concatenate([h_att[:, COL_FK:COL_FK + 256].reshape(S, 4, HEAD_DIM), one3, per_tok(cneg), zpad],
                          axis=2).reshape(S, 4 * MLA_PAD)
    v_f = h_att[:, COL_FV:COL_FV + 256]
    oT_a, lse_a = _smax_fwd_t(q_f.T, k_f, _kv_blocks_t(v_f, blk), dk=MLA_PAD, blk=blk, name=n("fox_fwd"))
    q_m, k_m, v_m, qT_m, kT3_m, vT3_m = _mla_prep_fwd(lat, P["mla_g_q"], P["mla_g_kv"], W["wuq"], W["wuk"], W["wuv"],
                                                      *tabs, bs=blk, name=n("mla_prep"))
    oT_b, lse_b = _smax_fwd_t(qT_m, k_m, vT3_m, dk=MLA_PAD, blk=blk, name=n("mla_fwd"))
    qT_c = h_att[:, COL_SQ:COL_SQ + 256].T
    oT_c, lt_c = _sb_fwd_t(qT_c, h_att, _kv_blocks_t(h_att[:, COL_SV:COL_SV + 256], blk), blk=blk, name=n("sb_fwd"))
    out_d, lse_d = _swa_fwd(h_att, P["swa_sinks"], name=n("swa_fwd"))
    outs = (oT_a, oT_b, oT_c, out_d)
    gn, gnT = _gnorm_fwd(outs, P["mix_g"], bs=bs, name=n("gnorm"))
    if late_weights is not None:
        W = dict(W, **late_weights(gn))
    u1 = _mm(gn, W["w_o"], name=n("out_proj"), bm=1024, bn=1024, bk=1024, resid=x, alpha=ALPHA)
    x1, x1b, x1bT, xh1, rs1 = _ln_fwd(u1, P["ln1_g"], P["ln1_b"], bs=bs, name=n("ln1"))
    gu = _mm(x1b, W["wgu"], name=n("gate_up"), bm=2048, bn=512, bk=1024)
    a, aT = _swiglu_fwd(gu, bs=min(256, S), name=n("swiglu"))
    u2 = _mm(a, W["w_down"], name=n("down"), bm=1024, bn=1024, bk=_blk(D_FF, 1408), resid=x1, alpha=ALPHA)
    x2, x2b, x2bT, xh2, rs2 = _ln_fwd(u2, P["ln2_g"], P["ln2_b"], bs=bs, name=n("ln2"))
    saved = dict(xbT=xbT, gnT=gnT, x1bT=x1bT, h_att=h_att, lat=lat, fg=fg, outs=outs, oT_a=oT_a, oT_b=oT_b, q_f=q_f, k_f=k_f, v_f=v_f,
                 qT_c=qT_c, lse_a=lse_a, lse_b=lse_b, lse_d=lse_d, lt_c=lt_c, q_m=q_m, k_m=k_m, v_m=v_m, qT_m=qT_m, kT3_m=kT3_m,
                 xh1=xh1, rs1=rs1, gu=gu, aT=aT, xh2=xh2, rs2=rs2)
    return x2, x2b, x2bT, saved, W


def _layer_bwd(l, dx2, sv, W, P, tabs, blk, send_early=None):
    S = dx2.shape[0]
    n = lambda s: f"l{l}_{s}"
    bs = min(512, S)
    h_att = sv["h_att"]
    du2, du2b, dg2, db2 = _ln_bwd(dx2, sv["xh2"], sv["rs2"], P["ln2_g"], bs=bs, name=n("ln2_bwd"))
    da = _mm(du2b, W["w_down"], name=n("down_dx"), tb=True, bm=1024, bn=_blk(D_FF, 1408), bk=1024)
    dwd = _mm(sv["aT"], du2b, name=n("down_dw"), bm=_blk(D_FF, 1408), bn=1024, bk=1024)
    dgu = _swiglu_bwd(da, sv["gu"], bs=min(256, S), name=n("swiglu_bwd"))
    dx1 = _mm(dgu, W["wgu"], name=n("gate_up_dx"), tb=True, bm=1024, bn=1024, bk=_blk(2 * D_FF, 1408), resid=du2,
              alpha=ALPHA)
    dwgu = _mm(sv["x1bT"], dgu, name=n("gate_up_dw"), bm=1024, bn=_blk(2 * D_FF, 1408), bk=1024)
    du1, du1b, dg1, db1 = _ln_bwd(dx1, sv["xh1"], sv["rs1"], P["ln1_g"], bs=bs, name=n("ln1_bwd"))
    dgn = _mm(du1b, W["w_o"], name=n("out_proj_dx"), tb=True, bm=1024, bn=1024, bk=1024)
    dwo = _mm(sv["gnT"], du1b, name=n("out_proj_dw"), bm=1024, bn=1024, bk=1024)
    mix_g = P["mix_g"]
    if send_early is not None:
        mix_g = mix_g + send_early(dict(w_o=dwo, w_gate=dwgu[:, :D_FF], w_up=dwgu[:, D_FF:], w_down=dwd))[0, 0]
    dmix, dmixT, dmixg = _gnorm_bwd(dgn, sv["outs"], mix_g, bs=bs, name=n("gnorm_bwd"))
    q_f, k_f = sv["q_f"], sv["k_f"]
    dqT_a, dk_a, dva = _smax_bwd_t(q_f.T, q_f, k_f, _kv_blocks_t(k_f, blk), sv["v_f"], dmix, dmixT, sv["oT_a"],
                                   sv["lse_a"], dk=MLA_PAD, dcb=0, qscale=HEAD_DIM ** -0.5, blk=blk, name=n("fox_bwd"))
    dq_a, dk_a = dqT_a.T.reshape(S, 4, MLA_PAD), dk_a.reshape(S, 4, MLA_PAD)
    dqa, dka = dq_a[:, :, :HEAD_DIM].reshape(S, 256), dk_a[:, :, :HEAD_DIM].reshape(S, 256)
    dcq = dq_a[:, :, HEAD_DIM].T.reshape(4, S // 128, 128)
    dck = dk_a[:, :, HEAD_DIM + 3].T.reshape(4, S // 128, 128)
    q_m, k_m = sv["q_m"], sv["k_m"]
    dqT_b, dkb, dvb = _smax_bwd_t(sv["qT_m"], q_m, k_m, sv["kT3_m"], sv["v_m"], dmix, dmixT, sv["oT_b"],
                                  sv["lse_b"], dk=MLA_PAD, dcb=2, qscale=MLA_QK ** -0.5, blk=blk, name=n("mla_bwd"))
    dqT_c, dkc, dvc = _sb_bwd_t(sv["qT_c"], h_att, _kv_blocks_t(h_att[:, COL_SK:COL_SK + 256], blk), dmix, dmixT,
                                sv["lt_c"], dcb=4, qscale=HEAD_DIM ** -0.5, blk=blk, name=n("sb_bwd"))
    dqc = dqT_c.T
    dqd, dkd, dvd, dsink = _swa_bwd(h_att, P["swa_sinks"], dmix, sv["outs"][3], sv["lse_d"], dcb=3, name=n("swa_bwd"))
    dlat, dwuq, dwuk, dwuv, dgq, dgkv = _mla_prep_bwd(
        sv["lat"], P["mla_g_q"], P["mla_g_kv"], W["wuq"], W["wuk"], W["wuv"], *tabs, dqT_b, dkb, dvb,
        bs=bs, name=n("mla_prep_bwd"))
    dfg, dbf = _fox_gate_bwd(sv["fg"], P["fox_b_f"], dck, dcq, q_unscale=HEAD_DIM ** 0.5, name=n("fox_gate_bwd"))
    dfg_blk = jnp.pad(dfg.reshape(4, S).T, ((0, 0), (0, 124)))
    dh = jnp.concatenate([t.astype(MXU_DTYPE) for t in (dqa, dka, dva, dqc, dkc, dvc, dqd, dkd, dvd, dlat, dfg_blk)], axis=1)
    dx = _mm(dh, W["win_p"], name=n("in_dx"), tb=True, bm=1024, bn=1024, bk=_blk(PERM_W, 1024), resid=du1, alpha=ALPHA)
    dwin_p = _mm(sv["xbT"], dh, name=n("in_dw"), bm=1024, bn=_blk(PERM_W, 1024), bk=1024)
    grads = _unprep_grads(dwin_p, dwuq, dwuk, dwuv, dwo, dwgu, dwd)
    grads.update(fox_b_f=dbf[:, 0], mla_g_q=dgq[0], mla_g_kv=dgkv[0], swa_sinks=dsink[:, 0], mix_g=dmixg[0],
                 ln1_g=dg1[0], ln1_b=db1[0], ln2_g=dg2[0], ln2_b=db2[0])
    return dx, grads


BIG = ("w_in", "mla_w_uq", "mla_w_ukv", "w_o", "w_gate", "w_up", "w_down")
SMALL = ("fox_b_f", "mla_g_q", "mla_g_kv", "swa_sinks", "mix_g", "ln1_g", "ln1_b", "ln2_g", "ln2_b")
SHARD_AXIS = dict(w_in=2, mla_w_uq=2, mla_w_ukv=2, w_o=1, w_gate=2, w_up=2, w_down=1)
N_CHIPS = 4
ANY = pl.BlockSpec(memory_space=pl.ANY)


def _chip_exchange(tensors, *, scatter, name):
    nt = len(tensors)

    def body(*refs):
        ins, outs = refs[:nt], refs[nt:2 * nt]
        send_sems, recv_sems, local_sems = refs[2 * nt:]
        x, y, c = lax.axis_index("x"), lax.axis_index("y"), lax.axis_index("c")
        me = 2 * x + y
        peers = [(1 - x, y), (x, 1 - y), (1 - x, 1 - y)]
        local, sends, recvs = [], [], []
        for t in range(nt):
            local.append(pltpu.make_async_copy(ins[t].at[me] if scatter else ins[t], outs[t].at[me], local_sems.at[t]))
            for r, (px, py) in enumerate(peers):
                k = 3 * t + r
                theirs = 2 * px + py
                sends.append(pltpu.make_async_remote_copy(
                    src_ref=ins[t].at[theirs] if scatter else ins[t], dst_ref=outs[t].at[me],
                    send_sem=send_sems.at[k], recv_sem=recv_sems.at[k], device_id=(px, py, c), device_id_type=MESH))
                recvs.append(pltpu.make_async_remote_copy(
                    src_ref=ins[t].at[me] if scatter else ins[t], dst_ref=outs[t].at[theirs],
                    send_sem=send_sems.at[k], recv_sem=recv_sems.at[k], device_id=(px, py, c), device_id_type=MESH))
        for cp in local + sends:
            cp.start()
        for cp in recvs:
            cp.wait_recv()
        for cp in sends:
            cp.wait_send()
        for cp in local:
            cp.wait()

    out_shape = [jax.ShapeDtypeStruct(t.shape if scatter else (N_CHIPS,) + t.shape, t.dtype) for t in tensors]
    return pl.pallas_call(
        body, name=name, in_specs=[ANY] * nt, out_specs=[ANY] * nt, out_shape=out_shape,
        scratch_shapes=[pltpu.SemaphoreType.DMA((3 * nt,)), pltpu.SemaphoreType.DMA((3 * nt,)),
                        pltpu.SemaphoreType.DMA((nt,))],
        compiler_params=pltpu.CompilerParams(has_side_effects=True),
    )(*tensors)


HBM = pl.BlockSpec(memory_space=pltpu.HBM)
SEM = pl.BlockSpec(memory_space=pltpu.SEMAPHORE)
N_PEER_CHIPS = N_CHIPS - 1


def _peer_copies(src_ref, land_ref, sems, scatter):
    x, y, c = lax.axis_index("x"), lax.axis_index("y"), lax.axis_index("c")
    me = 2 * x + y
    out = []
    for r, (px, py) in enumerate([(1 - x, y), (x, 1 - y), (1 - x, 1 - y)]):
        theirs = 2 * px + py
        send = pltpu.make_async_remote_copy(
            src_ref=src_ref.at[theirs] if scatter else src_ref, dst_ref=land_ref.at[me],
            send_sem=sems[2 * r], recv_sem=sems[2 * r + 1], device_id=(px, py, c), device_id_type=MESH)
        arrive = pltpu.make_async_remote_copy(
            src_ref=src_ref.at[me] if scatter else src_ref, dst_ref=land_ref.at[theirs],
            send_sem=sems[2 * r], recv_sem=sems[2 * r + 1], device_id=(px, py, c), device_id_type=MESH)
        out.append((send, arrive))
    return out


def _exchange_start(srcs, *, scatter, name):
    nt = len(srcs)
    ns = 2 * N_PEER_CHIPS * nt
    land_shapes = [s.shape if scatter else (N_CHIPS,) + s.shape for s in srcs]

    def body(*refs):
        src_refs, land_refs, outs = refs[:nt], refs[nt:2 * nt], refs[2 * nt:]
        for t in range(nt):
            for send, _ in _peer_copies(src_refs[t], land_refs[t], outs[6 * t:6 * t + 6], scatter):
                send.start()
        outs[-1][...] = jnp.zeros_like(outs[-1])

    res = pl.pallas_call(
        body, name=name,
        out_shape=(*[pltpu.SemaphoreType.DMA(())] * ns, *[pltpu.HBM(s.shape, s.dtype) for s in srcs],
                   *[pltpu.HBM(ls, s.dtype) for ls, s in zip(land_shapes, srcs)], jax.ShapeDtypeStruct((8, 128), F32)),
        in_specs=(HBM,) * (2 * nt), out_specs=(*[SEM] * ns, *[HBM] * (2 * nt), pl.BlockSpec(memory_space=pltpu.VMEM)),
        input_output_aliases={i: ns + i for i in range(2 * nt)},
        compiler_params=pltpu.CompilerParams(has_side_effects=pltpu.SideEffectType.DATAFLOW_SIDE_EFFECTING),
    )(*[pltpu.with_memory_space_constraint(s, pltpu.HBM) for s in srcs],
      *[pltpu.with_memory_space_constraint(lax.empty(ls, s.dtype), pltpu.HBM) for ls, s in zip(land_shapes, srcs)])
    return dict(sems=res[:ns], srcs=res[ns:ns + nt], lands=res[ns + nt:ns + 2 * nt], token=res[-1])


def _exchange_wait(started, after, *, scatter, name):
    nt = len(started["srcs"])
    ns = 2 * N_PEER_CHIPS * nt

    def body(*refs):
        src_refs, land_refs, sems = refs[:nt], refs[nt:2 * nt], refs[2 * nt:2 * nt + ns]
        for t in range(nt):
            for send, arrive in _peer_copies(src_refs[t], land_refs[t], sems[6 * t:6 * t + 6], scatter):
                send.wait_send()
                arrive.wait_recv()

    both = list(started["srcs"]) + list(started["lands"])
    res = pl.pallas_call(
        body, name=name, out_shape=tuple(pltpu.HBM(a.shape, a.dtype) for a in both),
        in_specs=(*[HBM] * (2 * nt), *[SEM] * ns, ANY), out_specs=(HBM,) * (2 * nt),
        input_output_aliases={i: i for i in range(2 * nt)},
        compiler_params=pltpu.CompilerParams(has_side_effects=pltpu.SideEffectType.DATAFLOW_SIDE_EFFECTING),
    )(*both, *started["sems"], after)
    return res[:nt], res[nt:]


def _core_exchange(tensors, *, name):
    nt = len(tensors)

    def body(*refs):
        ins, outs = refs[:nt], refs[nt:2 * nt]
        send_sems, recv_sems = refs[2 * nt:]
        sibling = (lax.axis_index("x"), lax.axis_index("y"), 1 - lax.axis_index("c"))
        copies = [pltpu.make_async_remote_copy(src_ref=ins[t], dst_ref=outs[t], send_sem=send_sems.at[t],
                                               recv_sem=recv_sems.at[t], device_id=sibling, device_id_type=MESH)
                  for t in range(nt)]
        for cp in copies:
            cp.start()
        for cp in copies:
            cp.wait_recv()
        for cp in copies:
            cp.wait_send()

    return pl.pallas_call(
        body, name=name, in_specs=[ANY] * nt, out_specs=[ANY] * nt,
        out_shape=[jax.ShapeDtypeStruct(t.shape, t.dtype) for t in tensors],
        scratch_shapes=[pltpu.SemaphoreType.DMA((nt,)), pltpu.SemaphoreType.DMA((nt,))],
        compiler_params=pltpu.CompilerParams(has_side_effects=True),
    )(*tensors)


def _all_sum_small(block, *, name):
    R = block.shape[0]
    n_dev = 8

    def body(x_ref, o_ref, slots, send_sems, recv_sems):
        x, y, c = lax.axis_index("x"), lax.axis_index("y"), lax.axis_index("c")
        me = 4 * x + 2 * y + c
        slots[me] = x_ref[...]
        sends, recvs = [], []
        for d in range(1, n_dev):
            px, py, pc = x ^ (d >> 2), y ^ ((d >> 1) & 1), c ^ (d & 1)
            theirs = 4 * px + 2 * py + pc
            sends.append(pltpu.make_async_remote_copy(
                src_ref=x_ref, dst_ref=slots.at[me], send_sem=send_sems.at[d - 1], recv_sem=recv_sems.at[d - 1],
                device_id=(px, py, pc), device_id_type=MESH))
            recvs.append(pltpu.make_async_remote_copy(
                src_ref=x_ref, dst_ref=slots.at[theirs], send_sem=send_sems.at[d - 1], recv_sem=recv_sems.at[d - 1],
                device_id=(px, py, pc), device_id_type=MESH))
        for cp in sends:
            cp.start()
        for cp in recvs:
            cp.wait_recv()
        for cp in sends:
            cp.wait_send()
        total = slots[0]
        for k in range(1, n_dev):
            total = total + slots[k]
        o_ref[...] = total

    return pl.pallas_call(
        body, name=name, in_specs=[pl.BlockSpec(memory_space=pltpu.VMEM)],
        out_specs=pl.BlockSpec(memory_space=pltpu.VMEM), out_shape=jax.ShapeDtypeStruct((R, 128), F32),
        scratch_shapes=[pltpu.VMEM((n_dev, R, 128), F32), pltpu.SemaphoreType.DMA((n_dev - 1,)),
                        pltpu.SemaphoreType.DMA((n_dev - 1,))],
        compiler_params=pltpu.CompilerParams(has_side_effects=True),
    )(block)


def _sum_chips(recv, *, br, name):
    _, R, C = recv.shape

    def body(r_ref, o_ref):
        total = r_ref[0].astype(F32)
        for k in range(1, N_CHIPS):
            total = total + r_ref[k].astype(F32)
        o_ref[...] = total

    return pl.pallas_call(
        body, name=name, grid=(R // br,), in_specs=[pl.BlockSpec((N_CHIPS, br, C), lambda i: (0, i, 0))],
        out_specs=pl.BlockSpec((br, C), lambda i: (i, 0)), out_shape=jax.ShapeDtypeStruct((R, C), F32),
        compiler_params=_cparams("parallel"),
    )(recv)


def _sum_chips_into(acc, land, own, me, layer, *, br, name):
    _, R, C = land.shape

    def body(me_ref, land_ref, own_ref, acc_ref, o_ref):
        mine = me_ref[0]
        total = None
        for k in range(N_CHIPS):
            part = jnp.where(mine == k, own_ref[...], land_ref[k]).astype(F32)
            total = part if total is None else total + part
        o_ref[0] = total

    return pl.pallas_call(
        body, name=name, grid=(R // br,),
        in_specs=[pl.BlockSpec(memory_space=pltpu.SMEM), pl.BlockSpec((N_CHIPS, br, C), lambda i: (0, i, 0)),
                  pl.BlockSpec((br, C), lambda i: (i, 0)), ANY],
        out_specs=pl.BlockSpec((1, br, C), lambda i: (layer, i, 0)),
        out_shape=jax.ShapeDtypeStruct(acc.shape, F32), input_output_aliases={3: 0},
        compiler_params=_cparams("parallel"),
    )(me, land, own, acc)


def _adamw_math(w, g, m, v):
    m = ADAM_B1 * m + (1.0 - ADAM_B1) * g
    v = ADAM_B2 * v + (1.0 - ADAM_B2) * (g * g)
    m_hat = m / (1.0 - ADAM_B1 ** ADAM_STEP)
    v_hat = v / (1.0 - ADAM_B2 ** ADAM_STEP)
    return -ADAM_LR * (m_hat / (jnp.sqrt(v_hat) + ADAM_EPS) + ADAM_WD * w), m, v


def _adamw(w, m, v, g_a, g_b, *, br, name):
    R, C = w.shape
    two = g_b is not None

    def body(*refs):
        if two:
            w_ref, m_ref, v_ref, ga_ref, gb_ref, g_ref, d_ref, nm_ref, nv_ref = refs
            g = ga_ref[...] + gb_ref[...]
        else:
            w_ref, m_ref, v_ref, ga_ref, g_ref, d_ref, nm_ref, nv_ref = refs
            g = ga_ref[...]
        g_ref[...] = g
        d_ref[...], nm_ref[...], nv_ref[...] = _adamw_math(w_ref[...], g, m_ref[...], v_ref[...])

    spec = pl.BlockSpec((br, C), lambda i: (i, 0))
    args = [w, m, v, g_a] + ([g_b] if two else [])
    return pl.pallas_call(
        body, name=name, grid=(R // br,), in_specs=[spec] * len(args), out_specs=[spec] * 4,
        out_shape=[jax.ShapeDtypeStruct((R, C), F32)] * 4,
        compiler_params=_cparams("parallel"),
    )(*args)


SMALL_ROWS = dict(fox_b_f=1, mla_g_q=2, mla_g_kv=1, swa_sinks=1, mix_g=8, ln1_g=8, ln1_b=8, ln2_g=8, ln2_b=8)
SMALL_ROWS_PER_LAYER = sum(SMALL_ROWS.values())


def _pack_small(vals, extra_rows):
    L = vals[SMALL[0]].shape[0]
    per_layer = []
    for name in SMALL:
        a = vals[name].astype(F32)
        a = jnp.pad(a, ((0, 0), (0, SMALL_ROWS[name] * 128 - a.shape[1])))
        per_layer.append(a.reshape(L, SMALL_ROWS[name], 128))
    out = jnp.concatenate(per_layer, axis=1).reshape(L * SMALL_ROWS_PER_LAYER, 128)
    return jnp.pad(out, ((0, extra_rows), (0, 0)))


def _unpack_small(block, shapes):
    L = shapes[SMALL[0]][0]
    body = block[:L * SMALL_ROWS_PER_LAYER].reshape(L, SMALL_ROWS_PER_LAYER, 128)
    out, r = {}, 0
    for name in SMALL:
        n = shapes[name][1]
        out[name] = body[:, r:r + SMALL_ROWS[name]].reshape(L, SMALL_ROWS[name] * 128)[:, :n]
        r += SMALL_ROWS[name]
    return out


PACK_ROW_MULTIPLE = 256


def _pack(parts):
    flat = [p.reshape(-1, 128) for p in parts]
    pad = (-sum(f.shape[0] for f in flat)) % PACK_ROW_MULTIPLE
    if pad:
        flat.append(jnp.zeros((pad, 128), flat[0].dtype))
    return jnp.concatenate(flat, axis=0)


def _unpack(block, shapes):
    out, r = [], 0
    for shp in shapes:
        n = int(np.prod(shp)) // 128
        out.append(block[r:r + n].reshape(shp))
        r += n
    return out


def _shard(g, k, axis):
    n = g.shape[axis] // N_CHIPS
    return lax.slice_in_dim(g, k * n, (k + 1) * n, axis=axis)


def _to_chips(g, axis):
    L, a, b = g.shape
    if axis == 2:
        return g.reshape(L, a, N_CHIPS, b // N_CHIPS).transpose(2, 0, 1, 3)
    return g.reshape(L, N_CHIPS, a // N_CHIPS, b).transpose(1, 0, 2, 3)


def _from_chips(g, axis):
    _, L, a, b = g.shape
    if axis == 2:
        return g.transpose(1, 2, 0, 3).reshape(L, a, N_CHIPS * b)
    return g.transpose(1, 0, 2, 3).reshape(L, N_CHIPS * a, b)


def kernel(x, w_in, fox_b_f, mla_g_q, mla_g_kv, mla_w_uq, mla_w_ukv, swa_sinks, mix_g, w_o, ln1_g, ln1_b, w_gate, w_up, w_down, ln2_g, ln2_b, loss_target, m_w_in, m_fox_b_f, m_mla_g_q, m_mla_g_kv, m_mla_w_uq, m_mla_w_ukv, m_swa_sinks, m_mix_g, m_w_o, m_ln1_g, m_ln1_b, m_w_gate, m_w_up, m_w_down, m_ln2_g, m_ln2_b, v_w_in, v_fox_b_f, v_mla_g_q, v_mla_g_kv, v_mla_w_uq, v_mla_w_ukv, v_swa_sinks, v_mix_g, v_w_o, v_ln1_g, v_ln1_b, v_w_gate, v_w_up, v_w_down, v_ln2_g, v_ln2_b):
    w = dict(w_in=w_in, fox_b_f=fox_b_f, mla_g_q=mla_g_q, mla_g_kv=mla_g_kv, mla_w_uq=mla_w_uq, mla_w_ukv=mla_w_ukv,
             swa_sinks=swa_sinks, mix_g=mix_g, w_o=w_o, ln1_g=ln1_g, ln1_b=ln1_b, w_gate=w_gate, w_up=w_up,
             w_down=w_down, ln2_g=ln2_g, ln2_b=ln2_b)
    m = dict(w_in=m_w_in, fox_b_f=m_fox_b_f, mla_g_q=m_mla_g_q, mla_g_kv=m_mla_g_kv, mla_w_uq=m_mla_w_uq,
             mla_w_ukv=m_mla_w_ukv, swa_sinks=m_swa_sinks, mix_g=m_mix_g, w_o=m_w_o, ln1_g=m_ln1_g, ln1_b=m_ln1_b,
             w_gate=m_w_gate, w_up=m_w_up, w_down=m_w_down, ln2_g=m_ln2_g, ln2_b=m_ln2_b)
    v = dict(w_in=v_w_in, fox_b_f=v_fox_b_f, mla_g_q=v_mla_g_q, mla_g_kv=v_mla_g_kv, mla_w_uq=v_mla_w_uq,
             mla_w_ukv=v_mla_w_ukv, swa_sinks=v_swa_sinks, mix_g=v_mix_g, w_o=v_w_o, ln1_g=v_ln1_g, ln1_b=v_ln1_b,
             w_gate=v_w_gate, w_up=v_w_up, w_down=v_w_down, ln2_g=v_ln2_g, ln2_b=v_ln2_b)
    names = tuple(w)
    L = w_in.shape[0]
    S = x.shape[1]
    blk = min(256, S)
    bs = min(512, S)

    me = 2 * lax.axis_index("x") + lax.axis_index("y")
    axis_of = {k: SHARD_AXIS[k] - 1 for k in BIG}
    groups = (("w_in", "mla_w_uq", "mla_w_ukv"), ("w_o", "w_gate", "w_up", "w_down"))

    started, last = [], None
    for l in range(L):
        per_group = []
        for g, group in enumerate(groups):
            srcs = [w[k][l].astype(MXU_DTYPE) for k in group]
            if last is not None:
                t = min(range(len(srcs)), key=lambda i: srcs[i].size)
                srcs[t] = srcs[t] + last["token"][0, 0].astype(MXU_DTYPE)
            last = _exchange_start(srcs, scatter=False, name=f"gather_start{l}_{g}")
            per_group.append(last)
        started.append(per_group)
    all_started = sum(st["token"] for per_group in started for st in per_group)

    def gathered(l, g, after):
        mine, lands = _exchange_wait(started[l][g], after, scatter=False, name=f"gather_wait{l}_{g}")
        return [jnp.concatenate([jnp.where(me == k, mine[t], lands[t][k]) for k in range(N_CHIPS)], axis=axis_of[name])
                for t, name in enumerate(groups[g])]

    def scatter(l, g, grads):
        to_owner = [_to_chips(grads[k][None], axis_of[k] + 1)[:, 0].astype(MXU_DTYPE) for k in groups[g]]
        return _exchange_start(to_owner, scatter=True, name=f"scatter_start{l}_{g}")

    tabs = _rope_tables(S)
    Ps = []
    for l in range(L):
        P = dict(fox_b_f=fox_b_f[l], swa_sinks=swa_sinks[l])
        for k in ("mla_g_q", "mla_g_kv", "mix_g", "ln1_g", "ln1_b", "ln2_g", "ln2_b"):
            P[k] = w[k][l][None, :]
        Ps.append(P)

    xa = x[0]
    xb = xa.astype(MXU_DTYPE)
    xbT = xb.T
    saved, Ws = [], []
    for l in range(L):
        W = _prep_weights_a(*gathered(l, 0, all_started if l == 0 else xa))
        late = lambda after, l=l: _prep_weights_b(*gathered(l, 1, after))
        xa, xb, xbT, sv, W = _layer_fwd(l, xa, xb, xbT, W, Ps[l], tabs, blk, late_weights=late)
        saved.append(sv)
        Ws.append(W)
    dx, loss_part = _loss_head(xa, loss_target[0], bs=bs, name="loss_head")

    layer_grads = [None] * L
    sent = [[None, None] for _ in range(L)]
    pin = None
    for l in reversed(range(L)):
        P = Ps[l] if pin is None else dict(Ps[l], ln2_g=Ps[l]["ln2_g"] + pin[0, 0])

        def send_early(grads, l=l):
            sent[l][1] = scatter(l, 1, grads)
            return sent[l][1]["token"]

        dx, layer_grads[l] = _layer_bwd(l, dx, saved[l], Ws[l], P, tabs, blk, send_early=send_early)
        sent[l][0] = scatter(l, 0, layer_grads[l])
        pin = sent[l][0]["token"]
    grad_x = dx[None]

    me_arr = me.astype(jnp.int32)[None]
    partial = {k: jnp.zeros(w[k].shape, F32) for k in BIG}
    after = dx
    for l in reversed(range(L)):
        for g in (1, 0):
            mine, lands = _exchange_wait(sent[l][g], after, scatter=True, name=f"scatter_wait{l}_{g}")
            for t, k in enumerate(groups[g]):
                own = lax.dynamic_index_in_dim(mine[t], me, 0, keepdims=False)
                partial[k] = _sum_chips_into(partial[k], lands[t], own, me_arr, l, br=_rows(own.shape[0]),
                                             name=f"sum_{k}_l{l}")
            after = partial[groups[g][-1]]
    partial = [partial[k] for k in BIG]
    sibling = _core_exchange(partial, name="swap_partials")
    local = {k: jnp.stack([layer_grads[l][k] for l in range(L)]) for k in SMALL}
    out = {}
    for k, mine, theirs in zip(BIG, partial, sibling):
        shp = w[k].shape
        two_d = lambda a: a.reshape(shp[0] * shp[1], shp[2])
        res = _adamw(two_d(w[k]), two_d(m[k]), two_d(v[k]), two_d(mine), two_d(theirs), br=_rows(shp[0] * shp[1]),
                     name=f"adamw_{k}")
        out[k] = [a.reshape(shp) for a in res]

    shapes = {k: w[k].shape for k in SMALL}
    extra = 8 + (-L * SMALL_ROWS_PER_LAYER) % 8
    block = _pack_small({k: local[k] for k in SMALL}, extra)
    block = block.at[L * SMALL_ROWS_PER_LAYER, 0].set(loss_part[0, 0])
    total = _all_sum_small(block, name="sum_small")
    loss = total[L * SMALL_ROWS_PER_LAYER, 0]
    res = _adamw(_pack_small({k: w[k] for k in SMALL}, extra), _pack_small({k: m[k] for k in SMALL}, extra),
                 _pack_small({k: v[k] for k in SMALL}, extra), total, None, br=total.shape[0], name="adamw_small")
    res = [_unpack_small(t, shapes) for t in res]
    for k in SMALL:
        out[k] = [r[k] for r in res]

    return (loss, grad_x, *[out[k][0] for k in names], *[out[k][1] for k in names],
            *[out[k][2] for k in names], *[out[k][3] for k in names])


def _rows(n):
    for b in (256, 128, 64, 32, 16, 8):
        if n % b == 0:
            return b
    return n
```

```python
import functools

import numpy as np
import jax
import jax.numpy as jnp
from jax import lax
from jax.experimental import pallas as pl
from jax.experimental.pallas import tpu as pltpu

F32 = jnp.float32
MXU_DTYPE = jnp.bfloat16
NEG_INF = -1e30

D_MODEL = 1024
DEPTH = 4
HEAD_DIM = 64
GROUP_WIDTH = 256
N_GROUPS = 4
D_FF = 2816
MLA_Q_RANK = 256
MLA_KV_RANK = 128
MLA_ROPE = 32
MLA_QK = 96
MLA_PAD = 128
ROPE_THETA = 10000.0
WINDOW = 128
ALPHA = (2.0 * DEPTH) ** 0.25
SWA_SLOPES = tuple(float(2.0 ** (-8.0 * h / 4)) for h in range(1, 5))
IN_WIDTH = 2468
ATT_W = 2048
LAT_W = 640
PERM_W = ATT_W + LAT_W
COL_FQ, COL_FK, COL_FV = 0, 256, 512
COL_SQ, COL_SK, COL_SV = 768, 1024, 1280
COL_WQ, COL_WK, COL_WV = 1536, 1792, 1920
Q_COLSCALE = np.ones((1, ATT_W), np.float32)
Q_COLSCALE[:, COL_FQ:COL_FQ + 256] = HEAD_DIM ** -0.5
Q_COLSCALE[:, COL_SQ:COL_SQ + 256] = HEAD_DIM ** -0.5

ADAM_LR, ADAM_B1, ADAM_B2, ADAM_EPS, ADAM_WD, ADAM_STEP = 0.001, 0.9, 0.999, 1e-08, 0.01, 10

VMEM_LIMIT = 56 * 1024 * 1024
NT = (((1,), (1,)), ((), ()))
TN = (((0,), (0,)), ((), ()))
MESH = pl.DeviceIdType.MESH


def _cparams(*sem):
    return pltpu.CompilerParams(dimension_semantics=sem, vmem_limit_bytes=VMEM_LIMIT)


def _split2(x):
    hi = x.astype(MXU_DTYPE)
    lo = (x - hi.astype(F32)).astype(MXU_DTYPE)
    return hi, lo


def _dot01(x, m01, dn=None, parts=2):
    acc = None
    rem = x
    for _ in range(parts):
        part = rem.astype(MXU_DTYPE)
        rem = rem - part.astype(F32)
        if dn is None:
            t = jnp.dot(part, m01, preferred_element_type=F32)
        else:
            t = lax.dot_general(part, m01, dn, preferred_element_type=F32)
        acc = t if acc is None else acc + t
    return acc


def _mm(a, b, *, name, ta=False, tb=False, out_dtype=F32, bm=512, bn=512, bk=512, resid=None, alpha=1.0,
        colscale=None):
    M, K = (a.shape[1], a.shape[0]) if ta else a.shape
    N = b.shape[0] if tb else b.shape[1]
    assert (b.shape[1] if tb else b.shape[0]) == K
    assert resid is None or colscale is None
    bm, bn, bk = min(bm, M), min(bn, N), min(bk, K)
    assert M % bm == 0 and N % bn == 0 and K % bk == 0, (name, M, N, K, bm, bn, bk)
    nk = K // bk
    assert nk == 1 or (out_dtype == F32 and colscale is None), name
    dn = (((0 if ta else 1,), (1 if tb else 0,)), ((), ()))

    extra = resid is not None or colscale is not None

    def body(*refs):
        a_ref, b_ref = refs[:2]
        r_ref = refs[2] if extra else None
        o_ref = refs[3] if extra else refs[2]
        k = pl.program_id(2)

        def first():
            r = lax.dot_general(a_ref[...].astype(MXU_DTYPE), b_ref[...].astype(MXU_DTYPE), dn,
                                preferred_element_type=F32)
            if resid is not None:
                r = r + alpha * r_ref[...]
            if colscale is not None:
                r = r * r_ref[...]
            o_ref[...] = r.astype(o_ref.dtype)

        if nk == 1:
            first()
        else:
            pl.when(k == 0)(first)

            @pl.when(k > 0)
            def _():
                o_ref[...] += lax.dot_general(a_ref[...].astype(MXU_DTYPE), b_ref[...].astype(MXU_DTYPE), dn,
                                              preferred_element_type=F32)

    a_spec = pl.BlockSpec((bk, bm), lambda i, j, k: (k, i)) if ta else pl.BlockSpec((bm, bk), lambda i, j, k: (i, k))
    b_spec = pl.BlockSpec((bn, bk), lambda i, j, k: (j, k)) if tb else pl.BlockSpec((bk, bn), lambda i, j, k: (k, j))
    in_specs = [a_spec, b_spec]
    args = [a, b]
    if resid is not None:
        in_specs.append(pl.BlockSpec((bm, bn), lambda i, j, k: (i, j)))
        args.append(resid)
    if colscale is not None:
        in_specs.append(pl.BlockSpec((1, bn), lambda i, j, k: (0, j)))
        args.append(colscale)
    return pl.pallas_call(
        body, name=name, grid=(M // bm, N // bn, nk), in_specs=in_specs,
        out_specs=pl.BlockSpec((bm, bn), lambda i, j, k: (i, j)),
        out_shape=jax.ShapeDtypeStruct((M, N), out_dtype),
        compiler_params=_cparams("parallel", "parallel", "arbitrary"),
    )(*args)


def _softmax_attn_fwd(q_arr, k_arr, v_arr, *, qcb, kcb, vcb, dk, scale, cum_col=None, cum_row=None, blk, name):
    S = q_arr.shape[0]
    nb = S // blk
    bias = cum_col is not None
    W = 2 * dk

    def body(*refs):
        if bias:
            q_ref, k_ref, v_ref, cc_ref, cr_ref, o_ref, lse_ref = refs
        else:
            q_ref, k_ref, v_ref, o_ref, lse_ref = refs
        p = pl.program_id(0)
        i = pl.program_id(1)
        row = lax.broadcasted_iota(jnp.int32, (blk, blk), 0)
        col = lax.broadcasted_iota(jnp.int32, (blk, blk), 1)
        for hh in range(2):
            q = q_ref[:, hh * dk:(hh + 1) * dk]

            def tile(j, carry, masked, hh=hh, q=q):
                m, l, acc = carry
                r0 = pl.multiple_of(j * blk, blk)
                ks = k_ref[pl.ds(r0, blk), hh * dk:(hh + 1) * dk]
                vs = v_ref[pl.ds(r0, blk), hh * HEAD_DIM:(hh + 1) * HEAD_DIM]
                s = lax.dot_general(q, ks, NT, preferred_element_type=F32) * scale
                if bias:
                    s = s + cc_ref[hh] - cr_ref[(2 * p + hh) * nb + j]
                if masked:
                    s = jnp.where(col <= row, s, NEG_INF)
                mn = jnp.maximum(m, jnp.max(s, axis=1, keepdims=True))
                a = jnp.exp(m - mn)
                pe = jnp.exp(s - mn)
                l = a * l + jnp.sum(pe, axis=1, keepdims=True)
                acc = a * acc + jnp.dot(pe.astype(MXU_DTYPE), vs, preferred_element_type=F32)
                return mn, l, acc

            init = (jnp.full((blk, 1), NEG_INF, F32), jnp.zeros((blk, 1), F32), jnp.zeros((blk, HEAD_DIM), F32))
            carry = lax.fori_loop(0, i, functools.partial(tile, masked=False), init)
            m, l, acc = tile(i, carry, True)
            o_ref[:, hh * HEAD_DIM:(hh + 1) * HEAD_DIM] = acc / l
            lse_ref[hh] = m + jnp.log(l)

    in_specs = [pl.BlockSpec((blk, W), lambda p, i: (i, qcb + p)),
                pl.BlockSpec((S, W), lambda p, i: (0, kcb + p)),
                pl.BlockSpec((S, 128), lambda p, i: (0, vcb + p))]
    args = [q_arr, k_arr, v_arr]
    if bias:
        in_specs += [pl.BlockSpec((2, blk, 1), lambda p, i: (p, i, 0)),
                     pl.BlockSpec((4 * nb, 1, blk), lambda p, i: (0, 0, 0))]
        args += [cum_col, cum_row]
    return pl.pallas_call(
        body, name=name, grid=(2, nb), in_specs=in_specs,
        out_specs=[pl.BlockSpec((blk, 128), lambda p, i: (i, p)), pl.BlockSpec((2, blk, 1), lambda p, i: (p, i, 0))],
        out_shape=[jax.ShapeDtypeStruct((S, GROUP_WIDTH), F32), jax.ShapeDtypeStruct((4, S, 1), F32)],
        compiler_params=_cparams("arbitrary", "arbitrary"),
    )(*args)


def _softmax_attn_bwd(q_arr, k_arr, v_arr, dmix, o_arr, lse, *, qcb, kcb, vcb, dcb, dk, scale,
                      cum_col=None, cum_row=None, blk, name):
    S = q_arr.shape[0]
    nb = S // blk
    bias = cum_col is not None
    W = 2 * dk

    def body(*refs):
        if bias:
            q_ref, k_ref, v_ref, do_ref, o_ref, lse_ref, cc_ref, cr_ref, dq_ref, dk_ref, dv_ref, dc_ref, dcq_ref = refs
        else:
            q_ref, k_ref, v_ref, do_ref, o_ref, lse_ref, dq_ref, dk_ref, dv_ref = refs
        p = pl.program_id(0)
        i = pl.program_id(1)

        @pl.when(i == 0)
        def _():
            dk_ref[...] = jnp.zeros_like(dk_ref)
            dv_ref[...] = jnp.zeros_like(dv_ref)
            if bias:
                dc_ref[...] = jnp.zeros_like(dc_ref)

        row = lax.broadcasted_iota(jnp.int32, (blk, blk), 0)
        col = lax.broadcasted_iota(jnp.int32, (blk, blk), 1)
        for hh in range(2):
            q = q_ref[:, hh * dk:(hh + 1) * dk]
            do = do_ref[:, hh * HEAD_DIM:(hh + 1) * HEAD_DIM]
            delta = jnp.sum(do * o_ref[:, hh * HEAD_DIM:(hh + 1) * HEAD_DIM], axis=1, keepdims=True)
            dob = do.astype(MXU_DTYPE)
            lse_h = lse_ref[hh]

            def tile(j, carry, masked, hh=hh, q=q, dob=dob, delta=delta, lse_h=lse_h):
                dq, dcq = carry
                r0 = pl.multiple_of(j * blk, blk)
                ks = k_ref[pl.ds(r0, blk), hh * dk:(hh + 1) * dk]
                vs = v_ref[pl.ds(r0, blk), hh * HEAD_DIM:(hh + 1) * HEAD_DIM]
                s = lax.dot_general(q, ks, NT, preferred_element_type=F32) * scale
                if bias:
                    s = s + cc_ref[hh] - cr_ref[(2 * p + hh) * nb + j]
                if masked:
                    s = jnp.where(col <= row, s, NEG_INF)
                pr = jnp.exp(s - lse_h)
                dp = lax.dot_general(dob, vs, NT, preferred_element_type=F32)
                ds = pr * (dp - delta)
                dsb = ds.astype(MXU_DTYPE)
                dv_ref[pl.ds(r0, blk), hh * HEAD_DIM:(hh + 1) * HEAD_DIM] += lax.dot_general(
                    pr.astype(MXU_DTYPE), dob, TN, preferred_element_type=F32)
                dk_ref[pl.ds(r0, blk), hh * dk:(hh + 1) * dk] += lax.dot_general(
                    dsb, q, TN, preferred_element_type=F32) * scale
                if bias:
                    dc_ref[hh * nb + j] -= jnp.sum(ds, axis=0, keepdims=True)
                    dcq = dcq + jnp.sum(ds, axis=1, keepdims=True)
                return dq + jnp.dot(dsb, ks, preferred_element_type=F32) * scale, dcq

            carry = lax.fori_loop(0, i, functools.partial(tile, masked=False),
                                  (jnp.zeros((blk, dk), F32), jnp.zeros((blk, 1), F32)))
            dq, dcq = tile(i, carry, True)
            dq_ref[:, hh * dk:(hh + 1) * dk] = dq
            if bias:
                dcq_ref[hh] = dcq

    in_specs = [pl.BlockSpec((blk, W), lambda p, i: (i, qcb + p)),
                pl.BlockSpec((S, W), lambda p, i: (0, kcb + p)),
                pl.BlockSpec((S, 128), lambda p, i: (0, vcb + p)),
                pl.BlockSpec((blk, 128), lambda p, i: (i, dcb + p)),
                pl.BlockSpec((blk, 128), lambda p, i: (i, p)),
                pl.BlockSpec((2, blk, 1), lambda p, i: (p, i, 0))]
    args = [q_arr, k_arr, v_arr, dmix, o_arr, lse]
    out_specs = [pl.BlockSpec((blk, W), lambda p, i: (i, p)),
                 pl.BlockSpec((S, W), lambda p, i: (0, p)),
                 pl.BlockSpec((S, 128), lambda p, i: (0, p))]
    out_shape = [jax.ShapeDtypeStruct((S, 4 * dk), F32), jax.ShapeDtypeStruct((S, 4 * dk), F32),
                 jax.ShapeDtypeStruct((S, GROUP_WIDTH), F32)]
    if bias:
        in_specs += [pl.BlockSpec((2, blk, 1), lambda p, i: (p, i, 0)),
                     pl.BlockSpec((4 * nb, 1, blk), lambda p, i: (0, 0, 0))]
        args += [cum_col, cum_row]
        out_specs += [pl.BlockSpec((2 * nb, 1, blk), lambda p, i: (p, 0, 0)), pl.BlockSpec((2, blk, 1), lambda p, i: (p, i, 0))]
        out_shape += [jax.ShapeDtypeStruct((4 * nb, 1, blk), F32), jax.ShapeDtypeStruct((4, S, 1), F32)]
    return pl.pallas_call(
        body, name=name, grid=(2, nb), in_specs=in_specs, out_specs=out_specs, out_shape=out_shape,
        compiler_params=_cparams("arbitrary", "arbitrary"),
    )(*args)


def _sb_tile(q, ks, scale, strict_mask, carry_l, tri_excl):
    z = lax.dot_general(q, ks, NT, preferred_element_type=F32) * scale
    lb = -(jnp.maximum(z, 0.0) + jnp.log(1.0 + jnp.exp(-jnp.abs(z))))
    if strict_mask is not None:
        lb = jnp.where(strict_mask, lb, 0.0)
    between = _dot01(lb, tri_excl) + carry_l
    a = jnp.exp(z + lb + between)
    if strict_mask is not None:
        a = jnp.where(strict_mask, a, 0.0)
    return z, lb, a


def _sb_attn_fwd(h_att, *, blk, name):
    S = h_att.shape[0]
    nb = S // blk
    scale = HEAD_DIM ** -0.5
    qcb, kcb, vcb = COL_SQ // 128, COL_SK // 128, COL_SV // 128

    def body(q_ref, k_ref, v_ref, o_ref, lt_ref):
        i = pl.program_id(1)
        row = lax.broadcasted_iota(jnp.int32, (blk, blk), 0)
        col = lax.broadcasted_iota(jnp.int32, (blk, blk), 1)
        strict = col < row
        tri_excl = (row > col).astype(MXU_DTYPE)
        for hh in range(2):
            sl = slice(hh * HEAD_DIM, (hh + 1) * HEAD_DIM)
            q = q_ref[:, sl]

            def tile(j, carry, mask, sl=sl, q=q):
                cl, acc = carry
                r0 = pl.multiple_of(j * blk, blk)
                _, lb, a = _sb_tile(q, k_ref[pl.ds(r0, blk), sl], scale, mask, cl, tri_excl)
                acc = acc + jnp.dot(a.astype(MXU_DTYPE), v_ref[pl.ds(r0, blk), sl], preferred_element_type=F32)
                return cl + jnp.sum(lb, axis=1, keepdims=True), acc

            carry = tile(i, (jnp.zeros((blk, 1), F32), jnp.zeros((blk, HEAD_DIM), F32)), strict)
            cl, acc = lax.fori_loop(0, i, lambda jj, c: tile(i - 1 - jj, c, None), carry)
            o_ref[:, sl] = acc
            lt_ref[hh] = cl

    return pl.pallas_call(
        body, name=name, grid=(2, nb),
        in_specs=[pl.BlockSpec((blk, 128), lambda p, i: (i, qcb + p)),
                  pl.BlockSpec((S, 128), lambda p, i: (0, kcb + p)),
                  pl.BlockSpec((S, 128), lambda p, i: (0, vcb + p))],
        out_specs=[pl.BlockSpec((blk, 128), lambda p, i: (i, p)), pl.BlockSpec((2, blk, 1), lambda p, i: (p, i, 0))],
        out_shape=[jax.ShapeDtypeStruct((S, GROUP_WIDTH), F32), jax.ShapeDtypeStruct((4, S, 1), F32)],
        compiler_params=_cparams("arbitrary", "arbitrary"),
    )(h_att, h_att, h_att)


def _sb_attn_bwd(h_att, dmix, ltot_arr, *, dcb, blk, name):
    S = h_att.shape[0]
    nb = S // blk
    scale = HEAD_DIM ** -0.5
    qcb, kcb, vcb = COL_SQ // 128, COL_SK // 128, COL_SV // 128

    def body(q_ref, k_ref, v_ref, do_ref, lt_ref, dq_ref, dk_ref, dv_ref):
        i = pl.program_id(1)

        @pl.when(i == 0)
        def _():
            dk_ref[...] = jnp.zeros_like(dk_ref)
            dv_ref[...] = jnp.zeros_like(dv_ref)

        row = lax.broadcasted_iota(jnp.int32, (blk, blk), 0)
        col = lax.broadcasted_iota(jnp.int32, (blk, blk), 1)
        strict = col < row
        up_incl = (row <= col).astype(MXU_DTYPE)
        up_excl = (row < col).astype(MXU_DTYPE)
        for hh in range(2):
            sl = slice(hh * HEAD_DIM, (hh + 1) * HEAD_DIM)
            q = q_ref[:, sl]
            dob = do_ref[:, sl].astype(MXU_DTYPE)
            ltot = lt_ref[hh]

            def tile(j, carry, mask, sl=sl, q=q, dob=dob, ltot=ltot):
                cl, cg, dq = carry
                r0 = pl.multiple_of(j * blk, blk)
                ks = k_ref[pl.ds(r0, blk), sl]
                vs = v_ref[pl.ds(r0, blk), sl]
                z = lax.dot_general(q, ks, NT, preferred_element_type=F32) * scale
                lb = -(jnp.maximum(z, 0.0) + jnp.log(1.0 + jnp.exp(-jnp.abs(z))))
                if mask is not None:
                    lb = jnp.where(mask, lb, 0.0)
                between = ltot - cl - _dot01(lb, up_incl)
                a = jnp.exp(z + lb + between)
                if mask is not None:
                    a = jnp.where(mask, a, 0.0)
                g = lax.dot_general(dob, vs, NT, preferred_element_type=F32) * a
                e = cg + _dot01(g, up_excl)
                dz = g * jnp.exp(lb) - e * jnp.exp(z + lb)
                if mask is not None:
                    dz = jnp.where(mask, dz, 0.0)
                dzb = dz.astype(MXU_DTYPE)
                dv_ref[pl.ds(r0, blk), sl] += lax.dot_general(a.astype(MXU_DTYPE), dob, TN, preferred_element_type=F32)
                dk_ref[pl.ds(r0, blk), sl] += lax.dot_general(dzb, q, TN, preferred_element_type=F32) * scale
                dq = dq + jnp.dot(dzb, ks, preferred_element_type=F32) * scale
                return cl + jnp.sum(lb, axis=1, keepdims=True), cg + jnp.sum(g, axis=1, keepdims=True), dq

            zc = jnp.zeros((blk, 1), F32)
            carry = lax.fori_loop(0, i, lambda j, c: tile(j, c, None), (zc, zc, jnp.zeros((blk, HEAD_DIM), F32)))
            _, _, dq = tile(i, carry, strict)
            dq_ref[:, sl] = dq

    return pl.pallas_call(
        body, name=name, grid=(2, nb),
        in_specs=[pl.BlockSpec((blk, 128), lambda p, i: (i, qcb + p)),
                  pl.BlockSpec((S, 128), lambda p, i: (0, kcb + p)),
                  pl.BlockSpec((S, 128), lambda p, i: (0, vcb + p)),
                  pl.BlockSpec((blk, 128), lambda p, i: (i, dcb + p)),
                  pl.BlockSpec((2, blk, 1), lambda p, i: (p, i, 0))],
        out_specs=[pl.BlockSpec((blk, 128), lambda p, i: (i, p)),
                   pl.BlockSpec((S, 128), lambda p, i: (0, p)),
                   pl.BlockSpec((S, 128), lambda p, i: (0, p))],
        out_shape=[jax.ShapeDtypeStruct((S, GROUP_WIDTH), F32)] * 3,
        compiler_params=_cparams("arbitrary", "arbitrary"),
    )(h_att, h_att, h_att, dmix, ltot_arr)


HP = 4
BLK_SOFTMAX = 512
BLK_STICK = 256


def _kv_blocks_t(a, blk):
    S, C = a.shape
    return a.reshape(S // blk, blk, C).transpose(0, 2, 1)


def _smax_fwd_t(qT, k, vT3, *, dk, blk, name):
    S = k.shape[0]
    nb = S // blk
    H = k.shape[1] // dk

    def body(qT_ref, k_ref, vT_ref, oT_ref, lse_ref):
        i = pl.program_id(1)
        key = lax.broadcasted_iota(jnp.int32, (blk, blk), 0)
        qry = lax.broadcasted_iota(jnp.int32, (blk, blk), 1)
        qs = [qT_ref[h * dk:(h + 1) * dk, :] for h in range(HP)]

        def tile(j, carry, masked):
            r0 = pl.multiple_of(j * blk, blk)
            ss = [jnp.dot(k_ref[pl.ds(r0, blk), h * dk:(h + 1) * dk], qs[h], preferred_element_type=F32)
                  for h in range(HP)]
            stats, pes = [], []
            for h in range(HP):
                m, l, _ = carry[h]
                s = jnp.where(key <= qry, ss[h], NEG_INF) if masked else ss[h]
                mn = jnp.maximum(m, jnp.max(s, axis=0, keepdims=True))
                a = jnp.exp(m - mn)
                pe = jnp.exp(s - mn)
                stats.append((mn, a * l + jnp.sum(pe, axis=0, keepdims=True), a))
                pes.append(pe.astype(MXU_DTYPE))
            pvs = [jnp.dot(vT_ref[j, h * HEAD_DIM:(h + 1) * HEAD_DIM, :], pes[h], preferred_element_type=F32)
                   for h in range(HP)]
            return tuple((stats[h][0], stats[h][1], stats[h][2] * carry[h][2] + pvs[h]) for h in range(HP))

        init = tuple((jnp.full((1, blk), NEG_INF, F32), jnp.zeros((1, blk), F32), jnp.zeros((HEAD_DIM, blk), F32))
                     for _ in range(HP))
        carry = lax.fori_loop(0, i, functools.partial(tile, masked=False), init)
        carry = tile(i, carry, True)
        for h in range(HP):
            m, l, acc = carry[h]
            oT_ref[h * HEAD_DIM:(h + 1) * HEAD_DIM, :] = acc / l
            lse_ref[h, 0] = m + jnp.log(l)

    return pl.pallas_call(
        body, name=name, grid=(H // HP, nb),
        in_specs=[pl.BlockSpec((HP * dk, blk), lambda p, i: (p, i)),
                  pl.BlockSpec((S, HP * dk), lambda p, i: (0, p)),
                  pl.BlockSpec((nb, HP * HEAD_DIM, blk), lambda p, i: (0, p, 0))],
        out_specs=[pl.BlockSpec((HP * HEAD_DIM, blk), lambda p, i: (p, i)),
                   pl.BlockSpec((HP, 1, 1, blk), lambda p, i: (p, i, 0, 0))],
        out_shape=[jax.ShapeDtypeStruct((H * HEAD_DIM, S), F32), jax.ShapeDtypeStruct((H, nb, 1, blk), F32)],
        compiler_params=_cparams("arbitrary", "arbitrary"),
    )(qT, k, vT3)


def _smax_bwd_t(qT, q, k, kT3, v, dmix, dmixT, oT, lse, *, dk, dcb, qscale, blk, name):
    S = k.shape[0]
    nb = S // blk
    H = k.shape[1] // dk
    hd = HP * HEAD_DIM
    dcr = dcb * 128 // hd

    def body(qT_ref, q_ref, k_ref, kT_ref, v_ref, do_ref, doT_ref, oT_ref, lse_ref, dqT_ref, dk_ref, dv_ref):
        i = pl.program_id(1)

        @pl.when(i == 0)
        def _():
            dk_ref[...] = jnp.zeros_like(dk_ref)
            dv_ref[...] = jnp.zeros_like(dv_ref)

        key = lax.broadcasted_iota(jnp.int32, (blk, blk), 0)
        qry = lax.broadcasted_iota(jnp.int32, (blk, blk), 1)
        per_head = []
        for h in range(HP):
            hs = slice(h * HEAD_DIM, (h + 1) * HEAD_DIM)
            doT = doT_ref[hs, :]
            per_head.append(dict(
                qT=qT_ref[h * dk:(h + 1) * dk, :], q=q_ref[:, h * dk:(h + 1) * dk],
                doT=doT.astype(MXU_DTYPE), do=do_ref[:, hs].astype(MXU_DTYPE),
                delta=jnp.sum(doT * oT_ref[hs, :], axis=0, keepdims=True), lse=lse_ref[h, 0]))

        def tile(j, dqs, masked):
            r0 = pl.multiple_of(j * blk, blk)
            rows = pl.ds(r0, blk)
            ksl = [slice(h * dk, (h + 1) * dk) for h in range(HP)]
            hsl = [slice(h * HEAD_DIM, (h + 1) * HEAD_DIM) for h in range(HP)]
            ss = [jnp.dot(k_ref[rows, ksl[h]], per_head[h]["qT"], preferred_element_type=F32) for h in range(HP)]
            dps = [jnp.dot(v_ref[rows, hsl[h]], per_head[h]["doT"], preferred_element_type=F32) for h in range(HP)]
            prs, dss = [], []
            for h in range(HP):
                c = per_head[h]
                s = jnp.where(key <= qry, ss[h], NEG_INF) if masked else ss[h]
                pr = jnp.exp(s - c["lse"])
                dss.append((pr * (dps[h] - c["delta"])).astype(MXU_DTYPE))
                prs.append(pr.astype(MXU_DTYPE))
            for h in range(HP):
                dv_ref[rows, hsl[h]] += jnp.dot(prs[h], per_head[h]["do"], preferred_element_type=F32)
            for h in range(HP):
                dk_ref[rows, ksl[h]] += jnp.dot(dss[h], per_head[h]["q"], preferred_element_type=F32)
            return tuple(dqs[h] + jnp.dot(kT_ref[j, ksl[h], :], dss[h], preferred_element_type=F32) for h in range(HP))

        dqs = lax.fori_loop(0, i, functools.partial(tile, masked=False),
                            tuple(jnp.zeros((dk, blk), F32) for _ in range(HP)))
        dqs = tile(i, dqs, True)
        for h in range(HP):
            dqT_ref[h * dk:(h + 1) * dk, :] = dqs[h] * qscale

    return pl.pallas_call(
        body, name=name, grid=(H // HP, nb),
        in_specs=[pl.BlockSpec((HP * dk, blk), lambda p, i: (p, i)),
                  pl.BlockSpec((blk, HP * dk), lambda p, i: (i, p)),
                  pl.BlockSpec((S, HP * dk), lambda p, i: (0, p)),
                  pl.BlockSpec((nb, HP * dk, blk), lambda p, i: (0, p, 0)),
                  pl.BlockSpec((S, hd), lambda p, i: (0, p)),
                  pl.BlockSpec((blk, hd), lambda p, i: (i, dcr + p)),
                  pl.BlockSpec((hd, blk), lambda p, i: (dcr + p, i)),
                  pl.BlockSpec((hd, blk), lambda p, i: (p, i)),
                  pl.BlockSpec((HP, 1, 1, blk), lambda p, i: (p, i, 0, 0))],
        out_specs=[pl.BlockSpec((HP * dk, blk), lambda p, i: (p, i)),
                   pl.BlockSpec((S, HP * dk), lambda p, i: (0, p)),
                   pl.BlockSpec((S, hd), lambda p, i: (0, p))],
        out_shape=[jax.ShapeDtypeStruct((H * dk, S), F32), jax.ShapeDtypeStruct((S, H * dk), F32),
                   jax.ShapeDtypeStruct((S, H * HEAD_DIM), F32)],
        compiler_params=_cparams("arbitrary", "arbitrary"),
    )(qT, q, k, kT3, v, dmix, dmixT, oT, lse)


def _log1m_beta(z):
    return -(jnp.maximum(z, 0.0) + jnp.log(1.0 + jnp.exp(-jnp.abs(z))))


def _dot01_left(m01, x, parts=2):
    acc = None
    rem = x
    for _ in range(parts):
        part = rem.astype(MXU_DTYPE)
        rem = rem - part.astype(F32)
        t = jnp.dot(m01, part, preferred_element_type=F32)
        acc = t if acc is None else acc + t
    return acc


def _sb_fwd_t(qT, h_att, vT3, *, blk, name):
    S = h_att.shape[0]
    nb = S // blk
    kcb = COL_SK // (HP * HEAD_DIM)

    def body(qT_ref, k_ref, vT_ref, oT_ref, lt_ref):
        i = pl.program_id(1)
        key = lax.broadcasted_iota(jnp.int32, (blk, blk), 0)
        qry = lax.broadcasted_iota(jnp.int32, (blk, blk), 1)
        strict = key < qry
        later = (qry > key).astype(MXU_DTYPE)
        qs = [qT_ref[h * HEAD_DIM:(h + 1) * HEAD_DIM, :] for h in range(HP)]

        def tile(j, carry, mask):
            r0 = pl.multiple_of(j * blk, blk)
            hsl = [slice(h * HEAD_DIM, (h + 1) * HEAD_DIM) for h in range(HP)]
            zs = [jnp.dot(k_ref[pl.ds(r0, blk), hsl[h]], qs[h], preferred_element_type=F32) for h in range(HP)]
            lbs = []
            for h in range(HP):
                lb = _log1m_beta(zs[h])
                lbs.append(lb if mask is None else jnp.where(mask, lb, 0.0))
            sums = [_dot01_left(later, lbs[h]) for h in range(HP)]
            probs = []
            for h in range(HP):
                a = jnp.exp(zs[h] + lbs[h] + sums[h] + carry[h][0])
                probs.append((a if mask is None else jnp.where(mask, a, 0.0)).astype(MXU_DTYPE))
            pvs = [jnp.dot(vT_ref[j, hsl[h], :], probs[h], preferred_element_type=F32) for h in range(HP)]
            return tuple((carry[h][0] + jnp.sum(lbs[h], axis=0, keepdims=True), carry[h][1] + pvs[h]) for h in range(HP))

        init = tuple((jnp.zeros((1, blk), F32), jnp.zeros((HEAD_DIM, blk), F32)) for _ in range(HP))
        carry = tile(i, init, strict)
        carry = lax.fori_loop(0, i, lambda jj, c: tile(i - 1 - jj, c, None), carry)
        for h in range(HP):
            oT_ref[h * HEAD_DIM:(h + 1) * HEAD_DIM, :] = carry[h][1]
            lt_ref[h, 0] = carry[h][0]

    hd = HP * HEAD_DIM
    return pl.pallas_call(
        body, name=name, grid=(4 // HP, nb),
        in_specs=[pl.BlockSpec((hd, blk), lambda p, i: (p, i)),
                  pl.BlockSpec((S, hd), lambda p, i: (0, kcb + p)),
                  pl.BlockSpec((nb, hd, blk), lambda p, i: (0, p, 0))],
        out_specs=[pl.BlockSpec((hd, blk), lambda p, i: (p, i)), pl.BlockSpec((HP, 1, 1, blk), lambda p, i: (p, i, 0, 0))],
        out_shape=[jax.ShapeDtypeStruct((GROUP_WIDTH, S), F32), jax.ShapeDtypeStruct((4, nb, 1, blk), F32)],
        compiler_params=_cparams("arbitrary", "arbitrary"),
    )(qT, h_att, vT3)


def _sb_bwd_t(qT, h_att, kT3, dmix, dmixT, ltot, *, dcb, qscale, blk, name):
    S = h_att.shape[0]
    nb = S // blk
    hd = HP * HEAD_DIM
    qcb, kcb, vcb = COL_SQ // hd, COL_SK // hd, COL_SV // hd
    dcr = dcb * 128 // hd

    def body(qT_ref, q_ref, k_ref, kT_ref, v_ref, do_ref, doT_ref, lt_ref, dqT_ref, dk_ref, dv_ref):
        i = pl.program_id(1)

        @pl.when(i == 0)
        def _():
            dk_ref[...] = jnp.zeros_like(dk_ref)
            dv_ref[...] = jnp.zeros_like(dv_ref)

        key = lax.broadcasted_iota(jnp.int32, (blk, blk), 0)
        qry = lax.broadcasted_iota(jnp.int32, (blk, blk), 1)
        strict = key < qry
        upto = (qry <= key).astype(MXU_DTYPE)
        before = (qry < key).astype(MXU_DTYPE)
        per_head = []
        for h in range(HP):
            hs = slice(h * HEAD_DIM, (h + 1) * HEAD_DIM)
            per_head.append(dict(qT=qT_ref[hs, :], q=q_ref[:, hs], doT=doT_ref[hs, :].astype(MXU_DTYPE),
                                 do=do_ref[:, hs].astype(MXU_DTYPE), lt=lt_ref[h, 0]))

        def tile(j, carry, mask):
            r0 = pl.multiple_of(j * blk, blk)
            rows = pl.ds(r0, blk)
            hsl = [slice(h * HEAD_DIM, (h + 1) * HEAD_DIM) for h in range(HP)]
            zs = [jnp.dot(k_ref[rows, hsl[h]], per_head[h]["qT"], preferred_element_type=F32) for h in range(HP)]
            das = [jnp.dot(v_ref[rows, hsl[h]], per_head[h]["doT"], preferred_element_type=F32) for h in range(HP)]
            lbs = []
            for h in range(HP):
                lb = _log1m_beta(zs[h])
                lbs.append(lb if mask is None else jnp.where(mask, lb, 0.0))
            sums = [_dot01_left(upto, lbs[h]) for h in range(HP)]
            probs, gs = [], []
            for h in range(HP):
                a = jnp.exp(zs[h] + lbs[h] + (per_head[h]["lt"] - carry[h][0] - sums[h]))
                a = a if mask is None else jnp.where(mask, a, 0.0)
                gs.append(das[h] * a)
                probs.append(a.astype(MXU_DTYPE))
            for h in range(HP):
                dv_ref[rows, hsl[h]] += jnp.dot(probs[h], per_head[h]["do"], preferred_element_type=F32)
            es = [_dot01_left(before, gs[h]) for h in range(HP)]
            dzs = []
            for h in range(HP):
                dz = gs[h] * jnp.exp(lbs[h]) - (carry[h][1] + es[h]) * jnp.exp(zs[h] + lbs[h])
                dzs.append((dz if mask is None else jnp.where(mask, dz, 0.0)).astype(MXU_DTYPE))
            for h in range(HP):
                dk_ref[rows, hsl[h]] += jnp.dot(dzs[h], per_head[h]["q"], preferred_element_type=F32)
            return tuple((carry[h][0] + jnp.sum(lbs[h], axis=0, keepdims=True),
                          carry[h][1] + jnp.sum(gs[h], axis=0, keepdims=True),
                          carry[h][2] + jnp.dot(kT_ref[j, hsl[h], :], dzs[h], preferred_element_type=F32))
                         for h in range(HP))

        zr = jnp.zeros((1, blk), F32)
        init = tuple((zr, zr, jnp.zeros((HEAD_DIM, blk), F32)) for _ in range(HP))
        carry = lax.fori_loop(0, i, lambda j, c: tile(j, c, None), init)
        carry = tile(i, carry, strict)
        for h in range(HP):
            dqT_ref[h * HEAD_DIM:(h + 1) * HEAD_DIM, :] = carry[h][2] * qscale

    return pl.pallas_call(
        body, name=name, grid=(4 // HP, nb),
        in_specs=[pl.BlockSpec((hd, blk), lambda p, i: (p, i)),
                  pl.BlockSpec((blk, hd), lambda p, i: (i, qcb + p)),
                  pl.BlockSpec((S, hd), lambda p, i: (0, kcb + p)),
                  pl.BlockSpec((nb, hd, blk), lambda p, i: (0, p, 0)),
                  pl.BlockSpec((S, hd), lambda p, i: (0, vcb + p)),
                  pl.BlockSpec((blk, hd), lambda p, i: (i, dcr + p)),
                  pl.BlockSpec((hd, blk), lambda p, i: (dcr + p, i)),
                  pl.BlockSpec((HP, 1, 1, blk), lambda p, i: (p, i, 0, 0))],
        out_specs=[pl.BlockSpec((hd, blk), lambda p, i: (p, i)),
                   pl.BlockSpec((S, hd), lambda p, i: (0, p)),
                   pl.BlockSpec((S, hd), lambda p, i: (0, p))],
        out_shape=[jax.ShapeDtypeStruct((GROUP_WIDTH, S), F32), jax.ShapeDtypeStruct((S, GROUP_WIDTH), F32),
                   jax.ShapeDtypeStruct((S, GROUP_WIDTH), F32)],
        compiler_params=_cparams("arbitrary", "arbitrary"),
    )(qT, h_att, h_att, kT3, h_att, dmix, dmixT, ltot)


def _swa_scores(q_ref, k_ref, n, h, start):
    g = h // 2
    kb = k_ref[pl.ds(start, 2 * WINDOW), g * HEAD_DIM:(g + 1) * HEAD_DIM]
    s = lax.dot_general(q_ref[:, h * HEAD_DIM:(h + 1) * HEAD_DIM], kb, NT, preferred_element_type=F32) * (HEAD_DIM ** -0.5)
    dist = (n * WINDOW + lax.broadcasted_iota(jnp.int32, (WINDOW, 2 * WINDOW), 0)
            - start - lax.broadcasted_iota(jnp.int32, (WINDOW, 2 * WINDOW), 1))
    s = s - SWA_SLOPES[h] * dist.astype(F32)
    valid = (dist >= 0) & (dist < WINDOW)
    return jnp.where(valid, s, NEG_INF), kb


def _swa_fwd(h_att, sinks, *, name):
    S = h_att.shape[0]
    nb = S // WINDOW
    qcb, kcb, vcb = COL_WQ // 256, COL_WK // 128, COL_WV // 128

    def body(sink_ref, q_ref, k_ref, v_ref, o_ref, lse_ref):
        n = pl.program_id(0)
        start = pl.multiple_of(jnp.maximum(n - 1, 0) * WINDOW, WINDOW)
        scores = [_swa_scores(q_ref, k_ref, n, h, start)[0] for h in range(4)]
        probs = []
        for h in range(4):
            sink = sink_ref[h]
            m = jnp.maximum(jnp.max(scores[h], axis=1, keepdims=True), sink)
            e = jnp.exp(scores[h] - m)
            den = jnp.sum(e, axis=1, keepdims=True) + jnp.exp(sink - m)
            probs.append((e / den).astype(MXU_DTYPE))
            lse_ref[h] = m + jnp.log(den)
        for h in range(4):
            vb = v_ref[pl.ds(start, 2 * WINDOW), (h // 2) * HEAD_DIM:(h // 2 + 1) * HEAD_DIM]
            o_ref[:, h * HEAD_DIM:(h + 1) * HEAD_DIM] = jnp.dot(probs[h], vb, preferred_element_type=F32)

    return pl.pallas_call(
        body, name=name, grid=(nb,),
        in_specs=[pl.BlockSpec(memory_space=pltpu.SMEM),
                  pl.BlockSpec((WINDOW, 256), lambda n: (n, qcb)),
                  pl.BlockSpec((S, 128), lambda n: (0, kcb)),
                  pl.BlockSpec((S, 128), lambda n: (0, vcb))],
        out_specs=[pl.BlockSpec((WINDOW, 256), lambda n: (n, 0)), pl.BlockSpec((4, WINDOW, 1), lambda n: (0, n, 0))],
        out_shape=[jax.ShapeDtypeStruct((S, GROUP_WIDTH), F32), jax.ShapeDtypeStruct((4, S, 1), F32)],
        compiler_params=_cparams("arbitrary"),
    )(sinks, h_att, h_att, h_att)


def _swa_bwd(h_att, sinks, dmix, o_arr, lse, *, dcb, name):
    S = h_att.shape[0]
    nb = S // WINDOW
    qcb, kcb, vcb = COL_WQ // 256, COL_WK // 128, COL_WV // 128

    def body(sink_ref, q_ref, k_ref, v_ref, do_ref, o_ref, lse_ref, dq_ref, dk_ref, dv_ref, dsink_ref):
        n = pl.program_id(0)

        @pl.when(n == 0)
        def _():
            dk_ref[...] = jnp.zeros_like(dk_ref)
            dv_ref[...] = jnp.zeros_like(dv_ref)
            dsink_ref[...] = jnp.zeros_like(dsink_ref)

        start = pl.multiple_of(jnp.maximum(n - 1, 0) * WINDOW, WINDOW)
        rows = pl.ds(start, 2 * WINDOW)
        hsl = [slice(h * HEAD_DIM, (h + 1) * HEAD_DIM) for h in range(4)]
        gsl = [slice(g * HEAD_DIM, (g + 1) * HEAD_DIM) for g in range(2)]
        scale = HEAD_DIM ** -0.5
        sk = [_swa_scores(q_ref, k_ref, n, h, start) for h in range(4)]
        dobs = [do_ref[:, hsl[h]].astype(MXU_DTYPE) for h in range(4)]
        dps = [lax.dot_general(dobs[h], v_ref[rows, gsl[h // 2]], NT, preferred_element_type=F32) for h in range(4)]
        prs, dss = [], []
        for h in range(4):
            lse_h = lse_ref[h]
            pr = jnp.exp(sk[h][0] - lse_h)
            delta = jnp.sum(do_ref[:, hsl[h]] * o_ref[:, hsl[h]], axis=1, keepdims=True)
            dss.append((pr * (dps[h] - delta)).astype(MXU_DTYPE))
            prs.append(pr.astype(MXU_DTYPE))
            dsink_ref[h:h + 1, :] += jnp.zeros((1, 128), F32) - jnp.sum(jnp.exp(sink_ref[h] - lse_h) * delta)
        for h in range(4):
            dq_ref[:, hsl[h]] = jnp.dot(dss[h], sk[h][1], preferred_element_type=F32) * scale
        for g in range(2):
            dk_ref[rows, gsl[g]] += (lax.dot_general(dss[2 * g], q_ref[:, hsl[2 * g]], TN, preferred_element_type=F32)
                                     + lax.dot_general(dss[2 * g + 1], q_ref[:, hsl[2 * g + 1]], TN,
                                                       preferred_element_type=F32)) * scale
            dv_ref[rows, gsl[g]] += (lax.dot_general(prs[2 * g], dobs[2 * g], TN, preferred_element_type=F32)
                                     + lax.dot_general(prs[2 * g + 1], dobs[2 * g + 1], TN, preferred_element_type=F32))

    return pl.pallas_call(
        body, name=name, grid=(nb,),
        in_specs=[pl.BlockSpec(memory_space=pltpu.SMEM),
                  pl.BlockSpec((WINDOW, 256), lambda n: (n, qcb)),
                  pl.BlockSpec((S, 128), lambda n: (0, kcb)),
                  pl.BlockSpec((S, 128), lambda n: (0, vcb)),
                  pl.BlockSpec((WINDOW, 256), lambda n: (n, dcb)),
                  pl.BlockSpec((WINDOW, 256), lambda n: (n, 0)),
                  pl.BlockSpec((4, WINDOW, 1), lambda n: (0, n, 0))],
        out_specs=[pl.BlockSpec((WINDOW, 256), lambda n: (n, 0)),
                   pl.BlockSpec((S, 128), lambda n: (0, 0)),
                   pl.BlockSpec((S, 128), lambda n: (0, 0)),
                   pl.BlockSpec((4, 128), lambda n: (0, 0))],
        out_shape=[jax.ShapeDtypeStruct((S, GROUP_WIDTH), F32), jax.ShapeDtypeStruct((S, 128), F32),
                   jax.ShapeDtypeStruct((S, 128), F32), jax.ShapeDtypeStruct((4, 128), F32)],
        compiler_params=_cparams("arbitrary"),
    )(sinks, h_att, h_att, h_att, dmix, o_arr, lse)


def _tri(n, incl, upper):
    r = lax.broadcasted_iota(jnp.int32, (n, n), 0)
    c = lax.broadcasted_iota(jnp.int32, (n, n), 1)
    if upper:
        m = (r <= c) if incl else (r < c)
    else:
        m = (r >= c) if incl else (r > c)
    return m.astype(MXU_DTYPE)


def _fox_gate_fwd(fg, b_f, *, name):
    _, R, _ = fg.shape

    def body(b_ref, fg_ref, pos_ref, neg_ref):
        up_incl = _tri(128, True, True)
        ones = jnp.ones((128, 128), MXU_DTYPE)
        for h in range(4):
            z = fg_ref[h] + b_ref[h]
            logf = jnp.minimum(z, 0.0) - jnp.log(1.0 + jnp.exp(-jnp.abs(z)))
            within = _dot01(logf, up_incl, parts=3)
            totals = _dot01(logf, ones, parts=3)
            rem = within + _rows_other(totals, R, after=False)
            for part in range(3):
                piece = rem.astype(MXU_DTYPE)
                rem = rem - piece.astype(F32)
                pos_ref[h, part] = piece
                neg_ref[h, part] = -piece

    shape = (4, 3) + fg.shape[1:]
    return pl.pallas_call(
        body, name=name,
        in_specs=[pl.BlockSpec(memory_space=pltpu.SMEM), pl.BlockSpec(memory_space=pltpu.VMEM)],
        out_specs=[pl.BlockSpec(memory_space=pltpu.VMEM)] * 2,
        out_shape=[jax.ShapeDtypeStruct(shape, MXU_DTYPE)] * 2,
    )(b_f, fg)


def _rows_other(totals, n, after):
    r = lax.broadcasted_iota(jnp.int32, (n, n), 0)
    c = lax.broadcasted_iota(jnp.int32, (n, n), 1)
    m = ((c > r) if after else (c < r)).astype(MXU_DTYPE)
    acc = None
    rem = totals
    for _ in range(3):
        part = rem.astype(MXU_DTYPE)
        rem = rem - part.astype(F32)
        t = jnp.dot(m, part, preferred_element_type=F32)
        acc = t if acc is None else acc + t
    return acc


def _fox_gate_bwd(fg, b_f, dcum_k, dcum_q, *, q_unscale, name):
    _, R, _ = fg.shape

    def body(b_ref, fg_ref, dck_ref, dcq_ref, dfg_ref, db_ref):
        low_incl = _tri(128, True, False)
        ones = jnp.ones((128, 128), MXU_DTYPE)
        for h in range(4):
            dc = dcq_ref[h] * q_unscale - dck_ref[h]
            dlogf = _dot01(dc, low_incl, parts=3) + _rows_other(_dot01(dc, ones, parts=3), R, after=True)
            z = fg_ref[h] + b_ref[h]
            dz = dlogf * jnp.exp(jnp.minimum(-z, 0.0) - jnp.log(1.0 + jnp.exp(-jnp.abs(z))))
            dfg_ref[h] = dz
            db_ref[h:h + 1, :] = jnp.zeros((1, 128), F32) + jnp.sum(dz)

    return pl.pallas_call(
        body, name=name,
        in_specs=[pl.BlockSpec(memory_space=pltpu.SMEM)] + [pl.BlockSpec(memory_space=pltpu.VMEM)] * 3,
        out_specs=[pl.BlockSpec(memory_space=pltpu.VMEM), pl.BlockSpec(memory_space=pltpu.VMEM)],
        out_shape=[jax.ShapeDtypeStruct(fg.shape, F32), jax.ShapeDtypeStruct((4, 128), F32)],
    )(b_f, fg, dcum_k, dcum_q)


def _rope_rot(transpose):
    r = lax.broadcasted_iota(jnp.int32, (MLA_PAD, MLA_PAD), 0)
    c = lax.broadcasted_iota(jnp.int32, (MLA_PAD, MLA_PAD), 1)
    if transpose:
        r, c = c, r
    half = MLA_ROPE // 2
    lo, mid, hi = HEAD_DIM, HEAD_DIM + half, HEAD_DIM + MLA_ROPE
    minus = (c >= lo) & (c < mid) & (r == c + half)
    plus = (c >= mid) & (c < hi) & (r == c - half)
    return jnp.where(plus, 1.0, jnp.where(minus, -1.0, 0.0)).astype(MXU_DTYPE)


def _rope_lanes():
    lane = lax.broadcasted_iota(jnp.int32, (1, MLA_PAD), 1)
    return ((lane >= HEAD_DIM) & (lane < HEAD_DIM + MLA_ROPE)).astype(F32)


def _rms(x, g, eps=1e-6):
    r = lax.rsqrt(jnp.mean(x * x, axis=-1, keepdims=True) + eps)
    return x * r * g, r


def _rms_bwd(dy, x, r, g):
    xh = x * r
    dxh = dy * g
    dx = r * (dxh - xh * jnp.mean(dxh * xh, axis=-1, keepdims=True))
    return dx, dy * xh


def _mla_prep_fwd(lat, g_q, g_kv, wuq, wuk, wuv, cosm, sinm, *, bs, name):
    S = lat.shape[0]

    def body(lat_ref, gq_ref, gkv_ref, wuq_ref, wuk_ref, wuv_ref, cos_ref, sin_ref,
             q_ref, k_ref, v_ref, qT_ref, kT_ref, vT_ref):
        rot = _rope_rot(False)
        cosm_, sinm_ = cos_ref[...], sin_ref[...]
        nq, _ = _rms(lat_ref[:, 0:MLA_Q_RANK], gq_ref[...])
        nkv, _ = _rms(lat_ref[:, MLA_Q_RANK:MLA_Q_RANK + MLA_KV_RANK], gkv_ref[...])
        qlat = jnp.dot(nq.astype(MXU_DTYPE), wuq_ref[...], preferred_element_type=F32)
        klat = jnp.dot(nkv.astype(MXU_DTYPE), wuk_ref[...], preferred_element_type=F32)
        v = jnp.dot(nkv.astype(MXU_DTYPE), wuv_ref[...], preferred_element_type=F32)
        v_ref[...] = v.astype(v_ref.dtype)
        vT_ref[0] = v.T.astype(vT_ref.dtype)
        krb = lat_ref[:, 384:512]
        kr = krb * (cosm_ * _rope_lanes()) + _dot01(krb, rot, parts=3) * sinm_
        for h in range(4):
            sl = slice(h * MLA_PAD, (h + 1) * MLA_PAD)
            qh = qlat[:, sl]
            q = (qh * cosm_ + _dot01(qh, rot, parts=3) * sinm_) * (MLA_QK ** -0.5)
            k = klat[:, sl] + kr
            q_ref[:, sl] = q.astype(q_ref.dtype)
            k_ref[:, sl] = k.astype(k_ref.dtype)
            qT_ref[sl, :] = q.T.astype(qT_ref.dtype)
            kT_ref[0, sl, :] = k.T.astype(kT_ref.dtype)

    full = lambda a: pl.BlockSpec(a.shape, lambda i: (0,) * a.ndim)
    return pl.pallas_call(
        body, name=name, grid=(S // bs,),
        in_specs=[pl.BlockSpec((bs, LAT_W), lambda i: (i, 0)), full(g_q), full(g_kv), full(wuq), full(wuk), full(wuv),
                  pl.BlockSpec((bs, MLA_PAD), lambda i: (i, 0)), pl.BlockSpec((bs, MLA_PAD), lambda i: (i, 0))],
        out_specs=[pl.BlockSpec((bs, 512), lambda i: (i, 0)), pl.BlockSpec((bs, 512), lambda i: (i, 0)),
                   pl.BlockSpec((bs, 256), lambda i: (i, 0)), pl.BlockSpec((512, bs), lambda i: (0, i)),
                   pl.BlockSpec((1, 512, bs), lambda i: (i, 0, 0)), pl.BlockSpec((1, 256, bs), lambda i: (i, 0, 0))],
        out_shape=[jax.ShapeDtypeStruct((S, 512), MXU_DTYPE), jax.ShapeDtypeStruct((S, 512), MXU_DTYPE),
                   jax.ShapeDtypeStruct((S, 256), MXU_DTYPE), jax.ShapeDtypeStruct((512, S), MXU_DTYPE),
                   jax.ShapeDtypeStruct((S // bs, 512, bs), MXU_DTYPE), jax.ShapeDtypeStruct((S // bs, 256, bs), MXU_DTYPE)],
        compiler_params=_cparams("parallel"),
    )(lat, g_q, g_kv, wuq, wuk, wuv, cosm, sinm)


def _mla_prep_bwd(lat, g_q, g_kv, wuq, wuk, wuv, cosm, sinm, dq, dk, dv, *, bs, name):
    S = lat.shape[0]

    def body(lat_ref, gq_ref, gkv_ref, wuq_ref, wuk_ref, wuv_ref, cos_ref, sin_ref, dq_ref, dk_ref, dv_ref,
             dlat_ref, dwuq_ref, dwuk_ref, dwuv_ref, dgq_ref, dgkv_ref):
        @pl.when(pl.program_id(0) == 0)
        def _():
            for r in (dwuq_ref, dwuk_ref, dwuv_ref, dgq_ref, dgkv_ref):
                r[...] = jnp.zeros_like(r)

        rot_t = _rope_rot(True)
        cosm_, sinm_ = cos_ref[...], sin_ref[...]
        cq = lat_ref[:, 0:MLA_Q_RANK]
        ckv = lat_ref[:, MLA_Q_RANK:MLA_Q_RANK + MLA_KV_RANK]
        nq, rq = _rms(cq, gq_ref[...])
        nkv, rkv = _rms(ckv, gkv_ref[...])
        nqb, nkvb = nq.astype(MXU_DTYPE), nkv.astype(MXU_DTYPE)

        dqlat = []
        dkr = jnp.zeros((bs, MLA_PAD), F32)
        for h in range(4):
            sl = slice(h * MLA_PAD, (h + 1) * MLA_PAD)
            dqh = dq_ref[sl, :].T
            dqlat.append(dqh * cosm_ + _dot01(dqh * sinm_, rot_t, parts=3))
            dkr = dkr + dk_ref[:, sl]
        dqlat = jnp.concatenate(dqlat, axis=1).astype(MXU_DTYPE)
        dkb = dk_ref[...].astype(MXU_DTYPE)
        dvb = dv_ref[...].astype(MXU_DTYPE)

        dnq = lax.dot_general(dqlat, wuq_ref[...], NT, preferred_element_type=F32)
        dnkv = (lax.dot_general(dkb, wuk_ref[...], NT, preferred_element_type=F32)
                + lax.dot_general(dvb, wuv_ref[...], NT, preferred_element_type=F32))
        dwuq_ref[...] += lax.dot_general(nqb, dqlat, TN, preferred_element_type=F32)
        dwuk_ref[...] += lax.dot_general(nkvb, dkb, TN, preferred_element_type=F32)
        dwuv_ref[...] += lax.dot_general(nkvb, dvb, TN, preferred_element_type=F32)
        dcq, tq = _rms_bwd(dnq, cq, rq, gq_ref[...])
        dckv, tkv = _rms_bwd(dnkv, ckv, rkv, gkv_ref[...])
        dgq_ref[...] += jnp.sum(tq, axis=0, keepdims=True)
        dgkv_ref[...] += jnp.sum(tkv, axis=0, keepdims=True)
        dlat_ref[:, 0:MLA_Q_RANK] = dcq.astype(dlat_ref.dtype)
        dlat_ref[:, MLA_Q_RANK:MLA_Q_RANK + MLA_KV_RANK] = dckv.astype(dlat_ref.dtype)
        dkrb = dkr * (cosm_ * _rope_lanes()) + _dot01(dkr * sinm_, rot_t, parts=3)
        dlat_ref[:, 384:512] = dkrb.astype(dlat_ref.dtype)

    full = lambda a: pl.BlockSpec(a.shape, lambda i: (0,) * a.ndim)
    row = lambda w: pl.BlockSpec((bs, w), lambda i: (i, 0))
    acc = lambda *shape: pl.BlockSpec(shape, lambda i: (0,) * len(shape))
    return pl.pallas_call(
        body, name=name, grid=(S // bs,),
        in_specs=[row(LAT_W), full(g_q), full(g_kv), full(wuq), full(wuk), full(wuv), row(MLA_PAD), row(MLA_PAD),
                  pl.BlockSpec((512, bs), lambda i: (0, i)), row(512), row(256)],
        out_specs=[row(512), acc(256, 512), acc(128, 512), acc(128, 256), acc(1, 256), acc(1, 128)],
        out_shape=[jax.ShapeDtypeStruct((S, 512), MXU_DTYPE), jax.ShapeDtypeStruct((256, 512), F32),
                   jax.ShapeDtypeStruct((128, 512), F32), jax.ShapeDtypeStruct((128, 256), F32),
                   jax.ShapeDtypeStruct((1, 256), F32), jax.ShapeDtypeStruct((1, 128), F32)],
        compiler_params=_cparams("arbitrary"),
    )(lat, g_q, g_kv, wuq, wuk, wuv, cosm, sinm, dq, dk, dv)


def _row_spec(bs, w):
    return pl.BlockSpec((bs, w), lambda i: (i, 0))


def _vec_spec(w):
    return pl.BlockSpec((1, w), lambda i: (0, 0))


def _mix_specs(bs):
    return [pl.BlockSpec((GROUP_WIDTH, bs), lambda i: (0, i))] * 3 + [_row_spec(bs, GROUP_WIDTH)]


def _mix_groups(a_ref, b_ref, c_ref, d_ref):
    return [a_ref[...].T, b_ref[...].T, c_ref[...].T, d_ref[...]]


def _gnorm_fwd(outs, g, *, bs, name):
    S = outs[3].shape[0]

    def body(a_ref, b_ref, c_ref, d_ref, g_ref, o_ref, oT_ref):
        for k, x in enumerate(_mix_groups(a_ref, b_ref, c_ref, d_ref)):
            sl = slice(k * GROUP_WIDTH, (k + 1) * GROUP_WIDTH)
            y, _ = _rms(x, g_ref[:, sl])
            o_ref[:, sl] = y.astype(o_ref.dtype)
            oT_ref[sl, :] = y.T.astype(oT_ref.dtype)

    return pl.pallas_call(
        body, name=name, grid=(S // bs,),
        in_specs=_mix_specs(bs) + [_vec_spec(D_MODEL)],
        out_specs=[_row_spec(bs, D_MODEL), pl.BlockSpec((D_MODEL, bs), lambda i: (0, i))],
        out_shape=[jax.ShapeDtypeStruct((S, D_MODEL), MXU_DTYPE), jax.ShapeDtypeStruct((D_MODEL, S), MXU_DTYPE)],
        compiler_params=_cparams("parallel"),
    )(*outs, g)


def _gnorm_bwd(dgn, outs, g, *, bs, name):
    S = dgn.shape[0]

    def body(dgn_ref, a_ref, b_ref, c_ref, d_ref, g_ref, dmix_ref, dmixT_ref, dg_ref):
        @pl.when(pl.program_id(0) == 0)
        def _():
            dg_ref[...] = jnp.zeros_like(dg_ref)

        for k, x in enumerate(_mix_groups(a_ref, b_ref, c_ref, d_ref)):
            sl = slice(k * GROUP_WIDTH, (k + 1) * GROUP_WIDTH)
            _, r = _rms(x, g_ref[:, sl])
            dx, t = _rms_bwd(dgn_ref[:, sl], x, r, g_ref[:, sl])
            dmix_ref[:, sl] = dx
            dmixT_ref[sl, :] = dx.T
            dg_ref[:, sl] += jnp.sum(t, axis=0, keepdims=True)

    return pl.pallas_call(
        body, name=name, grid=(S // bs,),
        in_specs=[_row_spec(bs, D_MODEL)] + _mix_specs(bs) + [_vec_spec(D_MODEL)],
        out_specs=[_row_spec(bs, D_MODEL), pl.BlockSpec((D_MODEL, bs), lambda i: (0, i)), _vec_spec(D_MODEL)],
        out_shape=[jax.ShapeDtypeStruct((S, D_MODEL), F32), jax.ShapeDtypeStruct((D_MODEL, S), F32),
                   jax.ShapeDtypeStruct((1, D_MODEL), F32)],
        compiler_params=_cparams("arbitrary"),
    )(dgn, *outs, g)


def _ln_fwd(u, g, b, *, bs, name):
    S = u.shape[0]

    def body(u_ref, g_ref, b_ref, y_ref, yb_ref, ybT_ref, xh_ref, rs_ref):
        x = u_ref[...]
        mu = jnp.mean(x, axis=-1, keepdims=True)
        xc = x - mu
        rs = lax.rsqrt(jnp.mean(xc * xc, axis=-1, keepdims=True) + 1e-5)
        xh = xc * rs
        y = xh * g_ref[...] + b_ref[...]
        y_ref[...] = y
        yb_ref[...] = y.astype(yb_ref.dtype)
        ybT_ref[...] = y.T.astype(ybT_ref.dtype)
        xh_ref[...] = xh
        rs_ref[...] = rs

    return pl.pallas_call(
        body, name=name, grid=(S // bs,),
        in_specs=[_row_spec(bs, D_MODEL), _vec_spec(D_MODEL), _vec_spec(D_MODEL)],
        out_specs=[_row_spec(bs, D_MODEL), _row_spec(bs, D_MODEL), pl.BlockSpec((D_MODEL, bs), lambda i: (0, i)),
                   _row_spec(bs, D_MODEL), _row_spec(bs, 1)],
        out_shape=[jax.ShapeDtypeStruct((S, D_MODEL), F32), jax.ShapeDtypeStruct((S, D_MODEL), MXU_DTYPE),
                   jax.ShapeDtypeStruct((D_MODEL, S), MXU_DTYPE), jax.ShapeDtypeStruct((S, D_MODEL), F32),
                   jax.ShapeDtypeStruct((S, 1), F32)],
        compiler_params=_cparams("parallel"),
    )(u, g, b)


def _ln_bwd(dy, xh, rs, g, *, bs, name):
    S = dy.shape[0]

    def body(dy_ref, xh_ref, rs_ref, g_ref, du_ref, dub_ref, dg_ref, db_ref):
        @pl.when(pl.program_id(0) == 0)
        def _():
            dg_ref[...] = jnp.zeros_like(dg_ref)
            db_ref[...] = jnp.zeros_like(db_ref)

        dy_, xh_ = dy_ref[...], xh_ref[...]
        dxh = dy_ * g_ref[...]
        du = rs_ref[...] * (dxh - jnp.mean(dxh, axis=-1, keepdims=True)
                            - xh_ * jnp.mean(dxh * xh_, axis=-1, keepdims=True))
        du_ref[...] = du
        dub_ref[...] = du.astype(dub_ref.dtype)
        dg_ref[...] += jnp.sum(dy_ * xh_, axis=0, keepdims=True)
        db_ref[...] += jnp.sum(dy_, axis=0, keepdims=True)

    return pl.pallas_call(
        body, name=name, grid=(S // bs,),
        in_specs=[_row_spec(bs, D_MODEL), _row_spec(bs, D_MODEL), _row_spec(bs, 1), _vec_spec(D_MODEL)],
        out_specs=[_row_spec(bs, D_MODEL), _row_spec(bs, D_MODEL), _vec_spec(D_MODEL), _vec_spec(D_MODEL)],
        out_shape=[jax.ShapeDtypeStruct((S, D_MODEL), F32), jax.ShapeDtypeStruct((S, D_MODEL), MXU_DTYPE),
                   jax.ShapeDtypeStruct((1, D_MODEL), F32), jax.ShapeDtypeStruct((1, D_MODEL), F32)],
        compiler_params=_cparams("arbitrary"),
    )(dy, xh, rs, g)


def _swiglu_fwd(gu, *, bs, name):
    S = gu.shape[0]

    def body(gu_ref, a_ref, aT_ref):
        gt = gu_ref[:, :D_FF]
        a = gt / (1.0 + jnp.exp(-gt)) * gu_ref[:, D_FF:]
        a_ref[...] = a.astype(a_ref.dtype)
        aT_ref[...] = a.T.astype(aT_ref.dtype)

    return pl.pallas_call(
        body, name=name, grid=(S // bs,),
        in_specs=[_row_spec(bs, 2 * D_FF)],
        out_specs=[_row_spec(bs, D_FF), pl.BlockSpec((D_FF, bs), lambda i: (0, i))],
        out_shape=[jax.ShapeDtypeStruct((S, D_FF), MXU_DTYPE), jax.ShapeDtypeStruct((D_FF, S), MXU_DTYPE)],
        compiler_params=_cparams("parallel"),
    )(gu)


def _swiglu_bwd(da, gu, *, bs, name):
    S = gu.shape[0]

    def body(da_ref, gu_ref, dgu_ref):
        gt, da_ = gu_ref[:, :D_FF], da_ref[...]
        sg = 1.0 / (1.0 + jnp.exp(-gt))
        silu = gt * sg
        dgu_ref[:, :D_FF] = (da_ * gu_ref[:, D_FF:] * (sg + silu * (1.0 - sg))).astype(dgu_ref.dtype)
        dgu_ref[:, D_FF:] = (da_ * silu).astype(dgu_ref.dtype)

    return pl.pallas_call(
        body, name=name, grid=(S // bs,),
        in_specs=[_row_spec(bs, D_FF), _row_spec(bs, 2 * D_FF)],
        out_specs=_row_spec(bs, 2 * D_FF), out_shape=jax.ShapeDtypeStruct((S, 2 * D_FF), MXU_DTYPE),
        compiler_params=_cparams("parallel"),
    )(da, gu)


def _loss_head(y, target, *, bs, name):
    S = y.shape[0]

    def body(y_ref, t_ref, dy_ref, loss_ref):
        @pl.when(pl.program_id(0) == 0)
        def _():
            loss_ref[...] = jnp.zeros_like(loss_ref)

        e = y_ref[...] - t_ref[...]
        dy_ref[...] = e * (1.0 / D_MODEL)
        per_tok = jnp.mean(e * e, axis=-1, keepdims=True)
        loss_ref[...] += 0.5 * jnp.sum(per_tok, axis=0, keepdims=True)

    return pl.pallas_call(
        body, name=name, grid=(S // bs,),
        in_specs=[_row_spec(bs, D_MODEL), _row_spec(bs, D_MODEL)],
        out_specs=[_row_spec(bs, D_MODEL), pl.BlockSpec((1, 1), lambda i: (0, 0))],
        out_shape=[jax.ShapeDtypeStruct((S, D_MODEL), F32), jax.ShapeDtypeStruct((1, 1), F32)],
        compiler_params=_cparams("arbitrary"),
    )(y, target)


def _blk(n, target):
    if n <= target:
        return n
    best = None
    for b in range(128, target + 1, 128):
        if n % b == 0:
            best = b
    assert best is not None, n
    return best


def _rope_tables(S):
    pos = jnp.arange(S, dtype=F32)
    inv = ROPE_THETA ** (-jnp.arange(0, MLA_ROPE, 2, dtype=F32) / MLA_ROPE)
    ang = pos[:, None] * inv[None, :]
    cos, sin = jnp.cos(ang), jnp.sin(ang)
    one, zero, pad = jnp.ones((S, HEAD_DIM), F32), jnp.zeros((S, HEAD_DIM), F32), jnp.zeros((S, MLA_PAD - MLA_QK), F32)
    return jnp.concatenate([one, cos, cos, pad], axis=1), jnp.concatenate([zero, sin, sin, pad], axis=1)


def _prep_weights_a(w_in, w_uq, w_ukv):
    z = lambda n: jnp.zeros((D_MODEL, n), w_in.dtype)
    win_a = jnp.concatenate([w_in[:, 0:768], w_in[:, 1188:2468]], axis=1)
    win_l = jnp.concatenate([w_in[:, 772:1156], z(64), w_in[:, 1156:1188], z(32), w_in[:, 768:772], z(124)], axis=1)
    kv = w_ukv.reshape(MLA_KV_RANK, 4, 2 * HEAD_DIM)
    return dict(
        win_a=win_a, win_l=win_l, win_p=jnp.concatenate([win_a, win_l], axis=1),
        wuq=jnp.pad(w_uq.reshape(MLA_Q_RANK, 4, MLA_QK), ((0, 0), (0, 0), (0, MLA_PAD - MLA_QK))).reshape(MLA_Q_RANK, 512),
        wuk=jnp.pad(kv[:, :, :HEAD_DIM], ((0, 0), (0, 0), (0, HEAD_DIM))).reshape(MLA_KV_RANK, 512),
        wuv=kv[:, :, HEAD_DIM:].reshape(MLA_KV_RANK, 256))


def _prep_weights_b(w_o, w_gate, w_up, w_down):
    return dict(w_o=w_o, wgu=jnp.concatenate([w_gate, w_up], axis=1), w_down=w_down)


def _unprep_grads(dwin_p, dwuq, dwuk, dwuv, dwo, dwgu, dwd):
    dw_in = jnp.concatenate([dwin_p[:, 0:768], dwin_p[:, 2560:2564], dwin_p[:, 2048:2432], dwin_p[:, 2496:2528],
                             dwin_p[:, 768:2048]], axis=1)
    dw_uq = dwuq.reshape(MLA_Q_RANK, 4, MLA_PAD)[:, :, :MLA_QK].reshape(MLA_Q_RANK, 4 * MLA_QK)
    dw_ukv = jnp.concatenate([dwuk.reshape(MLA_KV_RANK, 4, MLA_PAD)[:, :, :HEAD_DIM],
                              dwuv.reshape(MLA_KV_RANK, 4, HEAD_DIM)], axis=2).reshape(MLA_KV_RANK, 512)
    return dict(w_in=dw_in, mla_w_uq=dw_uq, mla_w_ukv=dw_ukv, w_o=dwo, w_gate=dwgu[:, :D_FF], w_up=dwgu[:, D_FF:],
                w_down=dwd)


def _layer_fwd(l, x, xb, xbT, W, P, tabs, blk, late_weights=None):
    S = x.shape[0]
    nb = S // blk
    n = lambda s: f"l{l}_{s}"
    bs = min(512, S)
    h_att = _mm(xb, W["win_a"], name=n("in_att"), out_dtype=MXU_DTYPE, bm=1024, bn=1024, bk=1024, colscale=Q_COLSCALE)
    lat = _mm(xb, W["win_l"], name=n("in_lat"), bm=2048, bn=LAT_W, bk=1024)
    fg = lat[:, 512:516].T.reshape(4, S // 128, 128)
    cpos, cneg = _fox_gate_fwd(fg, P["fox_b_f"], name=n("fox_gate"))
    one3 = jnp.ones((S, 4, 3), MXU_DTYPE)
    zpad = jnp.zeros((S, 4, MLA_PAD - HEAD_DIM - 6), MXU_DTYPE)
    per_tok = lambda parts: parts.reshape(4, 3, S).transpose(2, 0, 1)
    q_f = jnp.concatenate([h_att[:, COL_FQ:COL_FQ + 256].reshape(S, 4, HEAD_DIM), per_tok(cpos), one3, zpad],
                          axis=2).reshape(S, 4 * MLA_PAD)
    k_f = jnp.concatenate([h_att[:, COL_FK:COL_FK + 256].reshape(S, 4, HEAD_DIM), one3, per_tok(cneg), zpad],
                          axis=2).reshape(S, 4 * MLA_PAD)
    v_f = h_att[:, COL_FV:COL_FV + 256]
    oT_a, lse_a = _smax_fwd_t(q_f.T, k_f, _kv_blocks_t(v_f, blk), dk=MLA_PAD, blk=blk, name=n("fox_fwd"))
    q_m, k_m, v_m, qT_m, kT3_m, vT3_m = _mla_prep_fwd(lat, P["mla_g_q"], P["mla_g_kv"], W["wuq"], W["wuk"], W["wuv"],
                                                      *tabs, bs=blk, name=n("mla_prep"))
    oT_b, lse_b = _smax_fwd_t(qT_m, k_m, vT3_m, dk=MLA_PAD, blk=blk, name=n("mla_fwd"))
    qT_c = h_att[:, COL_SQ:COL_SQ + 256].T
    bsb = min(BLK_STICK, S)
    oT_c, lt_c = _sb_fwd_t(qT_c, h_att, _kv_blocks_t(h_att[:, COL_SV:COL_SV + 256], bsb), blk=bsb, name=n("sb_fwd"))
    out_d, lse_d = _swa_fwd(h_att, P["swa_sinks"], name=n("swa_fwd"))
    outs = (oT_a, oT_b, oT_c, out_d)
    gn, gnT = _gnorm_fwd(outs, P["mix_g"], bs=bs, name=n("gnorm"))
    if late_weights is not None:
        W = dict(W, **late_weights(gn))
    u1 = _mm(gn, W["w_o"], name=n("out_proj"), bm=1024, bn=1024, bk=1024, resid=x, alpha=ALPHA)
    x1, x1b, x1bT, xh1, rs1 = _ln_fwd(u1, P["ln1_g"], P["ln1_b"], bs=bs, name=n("ln1"))
    gu = _mm(x1b, W["wgu"], name=n("gate_up"), bm=2048, bn=512, bk=1024)
    a, aT = _swiglu_fwd(gu, bs=min(256, S), name=n("swiglu"))
    u2 = _mm(a, W["w_down"], name=n("down"), bm=1024, bn=1024, bk=_blk(D_FF, 1408), resid=x1, alpha=ALPHA)
    x2, x2b, x2bT, xh2, rs2 = _ln_fwd(u2, P["ln2_g"], P["ln2_b"], bs=bs, name=n("ln2"))
    saved = dict(xbT=xbT, gnT=gnT, x1bT=x1bT, h_att=h_att, lat=lat, fg=fg, outs=outs, oT_a=oT_a, oT_b=oT_b, q_f=q_f, k_f=k_f, v_f=v_f,
                 qT_c=qT_c, lse_a=lse_a, lse_b=lse_b, lse_d=lse_d, lt_c=lt_c, q_m=q_m, k_m=k_m, v_m=v_m, qT_m=qT_m, kT3_m=kT3_m,
                 xh1=xh1, rs1=rs1, gu=gu, aT=aT, xh2=xh2, rs2=rs2)
    return x2, x2b, x2bT, saved, W


def _layer_bwd(l, dx2, sv, W, P, tabs, blk, send_early=None):
    S = dx2.shape[0]
    n = lambda s: f"l{l}_{s}"
    bs = min(512, S)
    h_att = sv["h_att"]
    du2, du2b, dg2, db2 = _ln_bwd(dx2, sv["xh2"], sv["rs2"], P["ln2_g"], bs=bs, name=n("ln2_bwd"))
    da = _mm(du2b, W["w_down"], name=n("down_dx"), tb=True, bm=1024, bn=_blk(D_FF, 1408), bk=1024)
    dwd = _mm(sv["aT"], du2b, name=n("down_dw"), bm=_blk(D_FF, 1408), bn=1024, bk=1024)
    dgu = _swiglu_bwd(da, sv["gu"], bs=min(256, S), name=n("swiglu_bwd"))
    dx1 = _mm(dgu, W["wgu"], name=n("gate_up_dx"), tb=True, bm=1024, bn=1024, bk=_blk(2 * D_FF, 1408), resid=du2,
              alpha=ALPHA)
    dwgu = _mm(sv["x1bT"], dgu, name=n("gate_up_dw"), bm=1024, bn=_blk(2 * D_FF, 1408), bk=1024)
    du1, du1b, dg1, db1 = _ln_bwd(dx1, sv["xh1"], sv["rs1"], P["ln1_g"], bs=bs, name=n("ln1_bwd"))
    dgn = _mm(du1b, W["w_o"], name=n("out_proj_dx"), tb=True, bm=1024, bn=1024, bk=1024)
    dwo = _mm(sv["gnT"], du1b, name=n("out_proj_dw"), bm=1024, bn=1024, bk=1024)
    mix_g = P["mix_g"]
    if send_early is not None:
        mix_g = mix_g + send_early(dict(w_o=dwo, w_gate=dwgu[:, :D_FF], w_up=dwgu[:, D_FF:], w_down=dwd))[0, 0]
    dmix, dmixT, dmixg = _gnorm_bwd(dgn, sv["outs"], mix_g, bs=bs, name=n("gnorm_bwd"))
    q_f, k_f = sv["q_f"], sv["k_f"]
    dqT_a, dk_a, dva = _smax_bwd_t(q_f.T, q_f, k_f, _kv_blocks_t(k_f, blk), sv["v_f"], dmix, dmixT, sv["oT_a"],
                                   sv["lse_a"], dk=MLA_PAD, dcb=0, qscale=HEAD_DIM ** -0.5, blk=blk, name=n("fox_bwd"))
    dq_a, dk_a = dqT_a.T.reshape(S, 4, MLA_PAD), dk_a.reshape(S, 4, MLA_PAD)
    dqa, dka = dq_a[:, :, :HEAD_DIM].reshape(S, 256), dk_a[:, :, :HEAD_DIM].reshape(S, 256)
    dcq = dq_a[:, :, HEAD_DIM].T.reshape(4, S // 128, 128)
    dck = dk_a[:, :, HEAD_DIM + 3].T.reshape(4, S // 128, 128)
    q_m, k_m = sv["q_m"], sv["k_m"]
    dqT_b, dkb, dvb = _smax_bwd_t(sv["qT_m"], q_m, k_m, sv["kT3_m"], sv["v_m"], dmix, dmixT, sv["oT_b"],
                                  sv["lse_b"], dk=MLA_PAD, dcb=2, qscale=MLA_QK ** -0.5, blk=blk, name=n("mla_bwd"))
    bsb = min(BLK_STICK, S)
    dqT_c, dkc, dvc = _sb_bwd_t(sv["qT_c"], h_att, _kv_blocks_t(h_att[:, COL_SK:COL_SK + 256], bsb), dmix, dmixT,
                                sv["lt_c"], dcb=4, qscale=HEAD_DIM ** -0.5, blk=bsb, name=n("sb_bwd"))
    dqc = dqT_c.T
    dqd, dkd, dvd, dsink = _swa_bwd(h_att, P["swa_sinks"], dmix, sv["outs"][3], sv["lse_d"], dcb=3, name=n("swa_bwd"))
    dlat, dwuq, dwuk, dwuv, dgq, dgkv = _mla_prep_bwd(
        sv["lat"], P["mla_g_q"], P["mla_g_kv"], W["wuq"], W["wuk"], W["wuv"], *tabs, dqT_b, dkb, dvb,
        bs=bs, name=n("mla_prep_bwd"))
    dfg, dbf = _fox_gate_bwd(sv["fg"], P["fox_b_f"], dck, dcq, q_unscale=HEAD_DIM ** 0.5, name=n("fox_gate_bwd"))
    dfg_blk = jnp.pad(dfg.reshape(4, S).T, ((0, 0), (0, 124)))
    dh = jnp.concatenate([t.astype(MXU_DTYPE) for t in (dqa, dka, dva, dqc, dkc, dvc, dqd, dkd, dvd, dlat, dfg_blk)], axis=1)
    dx = _mm(dh, W["win_p"], name=n("in_dx"), tb=True, bm=1024, bn=1024, bk=_blk(PERM_W, 1024), resid=du1, alpha=ALPHA)
    dwin_p = _mm(sv["xbT"], dh, name=n("in_dw"), bm=1024, bn=_blk(PERM_W, 1024), bk=1024)
    grads = _unprep_grads(dwin_p, dwuq, dwuk, dwuv, dwo, dwgu, dwd)
    grads.update(fox_b_f=dbf[:, 0], mla_g_q=dgq[0], mla_g_kv=dgkv[0], swa_sinks=dsink[:, 0], mix_g=dmixg[0],
                 ln1_g=dg1[0], ln1_b=db1[0], ln2_g=dg2[0], ln2_b=db2[0])
    return dx, grads


BIG = ("w_in", "mla_w_uq", "mla_w_ukv", "w_o", "w_gate", "w_up", "w_down")
SMALL = ("fox_b_f", "mla_g_q", "mla_g_kv", "swa_sinks", "mix_g", "ln1_g", "ln1_b", "ln2_g", "ln2_b")
SHARD_AXIS = dict(w_in=2, mla_w_uq=2, mla_w_ukv=2, w_o=1, w_gate=2, w_up=2, w_down=1)
N_CHIPS = 4
ANY = pl.BlockSpec(memory_space=pl.ANY)


def _chip_exchange(tensors, *, scatter, name):
    nt = len(tensors)

    def body(*refs):
        ins, outs = refs[:nt], refs[nt:2 * nt]
        send_sems, recv_sems, local_sems = refs[2 * nt:]
        x, y, c = lax.axis_index("x"), lax.axis_index("y"), lax.axis_index("c")
        me = 2 * x + y
        peers = [(1 - x, y), (x, 1 - y), (1 - x, 1 - y)]
        local, sends, recvs = [], [], []
        for t in range(nt):
            local.append(pltpu.make_async_copy(ins[t].at[me] if scatter else ins[t], outs[t].at[me], local_sems.at[t]))
            for r, (px, py) in enumerate(peers):
                k = 3 * t + r
                theirs = 2 * px + py
                sends.append(pltpu.make_async_remote_copy(
                    src_ref=ins[t].at[theirs] if scatter else ins[t], dst_ref=outs[t].at[me],
                    send_sem=send_sems.at[k], recv_sem=recv_sems.at[k], device_id=(px, py, c), device_id_type=MESH))
                recvs.append(pltpu.make_async_remote_copy(
                    src_ref=ins[t].at[me] if scatter else ins[t], dst_ref=outs[t].at[theirs],
                    send_sem=send_sems.at[k], recv_sem=recv_sems.at[k], device_id=(px, py, c), device_id_type=MESH))
        for cp in local + sends:
            cp.start()
        for cp in recvs:
            cp.wait_recv()
        for cp in sends:
            cp.wait_send()
        for cp in local:
            cp.wait()

    out_shape = [jax.ShapeDtypeStruct(t.shape if scatter else (N_CHIPS,) + t.shape, t.dtype) for t in tensors]
    return pl.pallas_call(
        body, name=name, in_specs=[ANY] * nt, out_specs=[ANY] * nt, out_shape=out_shape,
        scratch_shapes=[pltpu.SemaphoreType.DMA((3 * nt,)), pltpu.SemaphoreType.DMA((3 * nt,)),
                        pltpu.SemaphoreType.DMA((nt,))],
        compiler_params=pltpu.CompilerParams(has_side_effects=True),
    )(*tensors)


HBM = pl.BlockSpec(memory_space=pltpu.HBM)
SEM = pl.BlockSpec(memory_space=pltpu.SEMAPHORE)
N_PEER_CHIPS = N_CHIPS - 1


def _peer_copies(src_ref, land_ref, sems, scatter):
    x, y, c = lax.axis_index("x"), lax.axis_index("y"), lax.axis_index("c")
    me = 2 * x + y
    out = []
    for r, (px, py) in enumerate([(1 - x, y), (x, 1 - y), (1 - x, 1 - y)]):
        theirs = 2 * px + py
        send = pltpu.make_async_remote_copy(
            src_ref=src_ref.at[theirs] if scatter else src_ref, dst_ref=land_ref.at[me],
            send_sem=sems[2 * r], recv_sem=sems[2 * r + 1], device_id=(px, py, c), device_id_type=MESH)
        arrive = pltpu.make_async_remote_copy(
            src_ref=src_ref.at[me] if scatter else src_ref, dst_ref=land_ref.at[theirs],
            send_sem=sems[2 * r], recv_sem=sems[2 * r + 1], device_id=(px, py, c), device_id_type=MESH)
        out.append((send, arrive))
    return out


def _exchange_start(srcs, *, scatter, name):
    nt = len(srcs)
    ns = 2 * N_PEER_CHIPS * nt
    land_shapes = [s.shape if scatter else (N_CHIPS,) + s.shape for s in srcs]

    def body(*refs):
        src_refs, land_refs, outs = refs[:nt], refs[nt:2 * nt], refs[2 * nt:]
        for t in range(nt):
            for send, _ in _peer_copies(src_refs[t], land_refs[t], outs[6 * t:6 * t + 6], scatter):
                send.start()
        outs[-1][...] = jnp.zeros_like(outs[-1])

    res = pl.pallas_call(
        body, name=name,
        out_shape=(*[pltpu.SemaphoreType.DMA(())] * ns, *[pltpu.HBM(s.shape, s.dtype) for s in srcs],
                   *[pltpu.HBM(ls, s.dtype) for ls, s in zip(land_shapes, srcs)], jax.ShapeDtypeStruct((8, 128), F32)),
        in_specs=(HBM,) * (2 * nt), out_specs=(*[SEM] * ns, *[HBM] * (2 * nt), pl.BlockSpec(memory_space=pltpu.VMEM)),
        input_output_aliases={i: ns + i for i in range(2 * nt)},
        compiler_params=pltpu.CompilerParams(has_side_effects=pltpu.SideEffectType.DATAFLOW_SIDE_EFFECTING),
    )(*[pltpu.with_memory_space_constraint(s, pltpu.HBM) for s in srcs],
      *[pltpu.with_memory_space_constraint(lax.empty(ls, s.dtype), pltpu.HBM) for ls, s in zip(land_shapes, srcs)])
    return dict(sems=res[:ns], srcs=res[ns:ns + nt], lands=res[ns + nt:ns + 2 * nt], token=res[-1])


def _exchange_wait(started, after, *, scatter, name):
    nt = len(started["srcs"])
    ns = 2 * N_PEER_CHIPS * nt

    def body(*refs):
        src_refs, land_refs, sems = refs[:nt], refs[nt:2 * nt], refs[2 * nt:2 * nt + ns]
        for t in range(nt):
            for send, arrive in _peer_copies(src_refs[t], land_refs[t], sems[6 * t:6 * t + 6], scatter):
                send.wait_send()
                arrive.wait_recv()

    both = list(started["srcs"]) + list(started["lands"])
    res = pl.pallas_call(
        body, name=name, out_shape=tuple(pltpu.HBM(a.shape, a.dtype) for a in both),
        in_specs=(*[HBM] * (2 * nt), *[SEM] * ns, ANY), out_specs=(HBM,) * (2 * nt),
        input_output_aliases={i: i for i in range(2 * nt)},
        compiler_params=pltpu.CompilerParams(has_side_effects=pltpu.SideEffectType.DATAFLOW_SIDE_EFFECTING),
    )(*both, *started["sems"], after)
    return res[:nt], res[nt:]


def _core_exchange(tensors, *, name):
    nt = len(tensors)

    def body(*refs):
        ins, outs = refs[:nt], refs[nt:2 * nt]
        send_sems, recv_sems = refs[2 * nt:]
        sibling = (lax.axis_index("x"), lax.axis_index("y"), 1 - lax.axis_index("c"))
        copies = [pltpu.make_async_remote_copy(src_ref=ins[t], dst_ref=outs[t], send_sem=send_sems.at[t],
                                               recv_sem=recv_sems.at[t], device_id=sibling, device_id_type=MESH)
                  for t in range(nt)]
        for cp in copies:
            cp.start()
        for cp in copies:
            cp.wait_recv()
        for cp in copies:
            cp.wait_send()

    return pl.pallas_call(
        body, name=name, in_specs=[ANY] * nt, out_specs=[ANY] * nt,
        out_shape=[jax.ShapeDtypeStruct(t.shape, t.dtype) for t in tensors],
        scratch_shapes=[pltpu.SemaphoreType.DMA((nt,)), pltpu.SemaphoreType.DMA((nt,))],
        compiler_params=pltpu.CompilerParams(has_side_effects=True),
    )(*tensors)


def _all_sum_small(block, *, name):
    R = block.shape[0]
    n_dev = 8

    def body(x_ref, o_ref, slots, send_sems, recv_sems):
        x, y, c = lax.axis_index("x"), lax.axis_index("y"), lax.axis_index("c")
        me = 4 * x + 2 * y + c
        slots[me] = x_ref[...]
        sends, recvs = [], []
        for d in range(1, n_dev):
            px, py, pc = x ^ (d >> 2), y ^ ((d >> 1) & 1), c ^ (d & 1)
            theirs = 4 * px + 2 * py + pc
            sends.append(pltpu.make_async_remote_copy(
                src_ref=x_ref, dst_ref=slots.at[me], send_sem=send_sems.at[d - 1], recv_sem=recv_sems.at[d - 1],
                device_id=(px, py, pc), device_id_type=MESH))
            recvs.append(pltpu.make_async_remote_copy(
                src_ref=x_ref, dst_ref=slots.at[theirs], send_sem=send_sems.at[d - 1], recv_sem=recv_sems.at[d - 1],
                device_id=(px, py, pc), device_id_type=MESH))
        for cp in sends:
            cp.start()
        for cp in recvs:
            cp.wait_recv()
        for cp in sends:
            cp.wait_send()
        total = slots[0]
        for k in range(1, n_dev):
            total = total + slots[k]
        o_ref[...] = total

    return pl.pallas_call(
        body, name=name, in_specs=[pl.BlockSpec(memory_space=pltpu.VMEM)],
        out_specs=pl.BlockSpec(memory_space=pltpu.VMEM), out_shape=jax.ShapeDtypeStruct((R, 128), F32),
        scratch_shapes=[pltpu.VMEM((n_dev, R, 128), F32), pltpu.SemaphoreType.DMA((n_dev - 1,)),
                        pltpu.SemaphoreType.DMA((n_dev - 1,))],
        compiler_params=pltpu.CompilerParams(has_side_effects=True),
    )(block)


def _sum_chips(recv, *, br, name):
    _, R, C = recv.shape

    def body(r_ref, o_ref):
        total = r_ref[0].astype(F32)
        for k in range(1, N_CHIPS):
            total = total + r_ref[k].astype(F32)
        o_ref[...] = total

    return pl.pallas_call(
        body, name=name, grid=(R // br,), in_specs=[pl.BlockSpec((N_CHIPS, br, C), lambda i: (0, i, 0))],
        out_specs=pl.BlockSpec((br, C), lambda i: (i, 0)), out_shape=jax.ShapeDtypeStruct((R, C), F32),
        compiler_params=_cparams("parallel"),
    )(recv)


def _sum_chips_into(acc, land, own, me, layer, *, br, name):
    _, R, C = land.shape

    def body(me_ref, land_ref, own_ref, acc_ref, o_ref):
        mine = me_ref[0]
        total = None
        for k in range(N_CHIPS):
            part = jnp.where(mine == k, own_ref[...], land_ref[k]).astype(F32)
            total = part if total is None else total + part
        o_ref[0] = total

    return pl.pallas_call(
        body, name=name, grid=(R // br,),
        in_specs=[pl.BlockSpec(memory_space=pltpu.SMEM), pl.BlockSpec((N_CHIPS, br, C), lambda i: (0, i, 0)),
                  pl.BlockSpec((br, C), lambda i: (i, 0)), ANY],
        out_specs=pl.BlockSpec((1, br, C), lambda i: (layer, i, 0)),
        out_shape=jax.ShapeDtypeStruct(acc.shape, F32), input_output_aliases={3: 0},
        compiler_params=_cparams("parallel"),
    )(me, land, own, acc)


def _adamw_math(w, g, m, v):
    m = ADAM_B1 * m + (1.0 - ADAM_B1) * g
    v = ADAM_B2 * v + (1.0 - ADAM_B2) * (g * g)
    m_hat = m / (1.0 - ADAM_B1 ** ADAM_STEP)
    v_hat = v / (1.0 - ADAM_B2 ** ADAM_STEP)
    return -ADAM_LR * (m_hat / (jnp.sqrt(v_hat) + ADAM_EPS) + ADAM_WD * w), m, v


def _adamw(w, m, v, g_a, g_b, *, br, name):
    R, C = w.shape
    two = g_b is not None

    def body(*refs):
        if two:
            w_ref, m_ref, v_ref, ga_ref, gb_ref, g_ref, d_ref, nm_ref, nv_ref = refs
            g = ga_ref[...] + gb_ref[...]
        else:
            w_ref, m_ref, v_ref, ga_ref, g_ref, d_ref, nm_ref, nv_ref = refs
            g = ga_ref[...]
        g_ref[...] = g
        d_ref[...], nm_ref[...], nv_ref[...] = _adamw_math(w_ref[...], g, m_ref[...], v_ref[...])

    spec = pl.BlockSpec((br, C), lambda i: (i, 0))
    args = [w, m, v, g_a] + ([g_b] if two else [])
    return pl.pallas_call(
        body, name=name, grid=(R // br,), in_specs=[spec] * len(args), out_specs=[spec] * 4,
        out_shape=[jax.ShapeDtypeStruct((R, C), F32)] * 4,
        compiler_params=_cparams("parallel"),
    )(*args)


SMALL_ROWS = dict(fox_b_f=1, mla_g_q=2, mla_g_kv=1, swa_sinks=1, mix_g=8, ln1_g=8, ln1_b=8, ln2_g=8, ln2_b=8)
SMALL_ROWS_PER_LAYER = sum(SMALL_ROWS.values())


def _pack_small(vals, extra_rows):
    L = vals[SMALL[0]].shape[0]
    per_layer = []
    for name in SMALL:
        a = vals[name].astype(F32)
        a = jnp.pad(a, ((0, 0), (0, SMALL_ROWS[name] * 128 - a.shape[1])))
        per_layer.append(a.reshape(L, SMALL_ROWS[name], 128))
    out = jnp.concatenate(per_layer, axis=1).reshape(L * SMALL_ROWS_PER_LAYER, 128)
    return jnp.pad(out, ((0, extra_rows), (0, 0)))


def _unpack_small(block, shapes):
    L = shapes[SMALL[0]][0]
    body = block[:L * SMALL_ROWS_PER_LAYER].reshape(L, SMALL_ROWS_PER_LAYER, 128)
    out, r = {}, 0
    for name in SMALL:
        n = shapes[name][1]
        out[name] = body[:, r:r + SMALL_ROWS[name]].reshape(L, SMALL_ROWS[name] * 128)[:, :n]
        r += SMALL_ROWS[name]
    return out


PACK_ROW_MULTIPLE = 256


def _pack(parts):
    flat = [p.reshape(-1, 128) for p in parts]
    pad = (-sum(f.shape[0] for f in flat)) % PACK_ROW_MULTIPLE
    if pad:
        flat.append(jnp.zeros((pad, 128), flat[0].dtype))
    return jnp.concatenate(flat, axis=0)


def _unpack(block, shapes):
    out, r = [], 0
    for shp in shapes:
        n = int(np.prod(shp)) // 128
        out.append(block[r:r + n].reshape(shp))
        r += n
    return out


def _shard(g, k, axis):
    n = g.shape[axis] // N_CHIPS
    return lax.slice_in_dim(g, k * n, (k + 1) * n, axis=axis)


def _to_chips(g, axis):
    L, a, b = g.shape
    if axis == 2:
        return g.reshape(L, a, N_CHIPS, b // N_CHIPS).transpose(2, 0, 1, 3)
    return g.reshape(L, N_CHIPS, a // N_CHIPS, b).transpose(1, 0, 2, 3)


def _from_chips(g, axis):
    _, L, a, b = g.shape
    if axis == 2:
        return g.transpose(1, 2, 0, 3).reshape(L, a, N_CHIPS * b)
    return g.transpose(1, 0, 2, 3).reshape(L, N_CHIPS * a, b)


def kernel(x, w_in, fox_b_f, mla_g_q, mla_g_kv, mla_w_uq, mla_w_ukv, swa_sinks, mix_g, w_o, ln1_g, ln1_b, w_gate, w_up, w_down, ln2_g, ln2_b, loss_target, m_w_in, m_fox_b_f, m_mla_g_q, m_mla_g_kv, m_mla_w_uq, m_mla_w_ukv, m_swa_sinks, m_mix_g, m_w_o, m_ln1_g, m_ln1_b, m_w_gate, m_w_up, m_w_down, m_ln2_g, m_ln2_b, v_w_in, v_fox_b_f, v_mla_g_q, v_mla_g_kv, v_mla_w_uq, v_mla_w_ukv, v_swa_sinks, v_mix_g, v_w_o, v_ln1_g, v_ln1_b, v_w_gate, v_w_up, v_w_down, v_ln2_g, v_ln2_b):
    w = dict(w_in=w_in, fox_b_f=fox_b_f, mla_g_q=mla_g_q, mla_g_kv=mla_g_kv, mla_w_uq=mla_w_uq, mla_w_ukv=mla_w_ukv,
             swa_sinks=swa_sinks, mix_g=mix_g, w_o=w_o, ln1_g=ln1_g, ln1_b=ln1_b, w_gate=w_gate, w_up=w_up,
             w_down=w_down, ln2_g=ln2_g, ln2_b=ln2_b)
    m = dict(w_in=m_w_in, fox_b_f=m_fox_b_f, mla_g_q=m_mla_g_q, mla_g_kv=m_mla_g_kv, mla_w_uq=m_mla_w_uq,
             mla_w_ukv=m_mla_w_ukv, swa_sinks=m_swa_sinks, mix_g=m_mix_g, w_o=m_w_o, ln1_g=m_ln1_g, ln1_b=m_ln1_b,
             w_gate=m_w_gate, w_up=m_w_up, w_down=m_w_down, ln2_g=m_ln2_g, ln2_b=m_ln2_b)
    v = dict(w_in=v_w_in, fox_b_f=v_fox_b_f, mla_g_q=v_mla_g_q, mla_g_kv=v_mla_g_kv, mla_w_uq=v_mla_w_uq,
             mla_w_ukv=v_mla_w_ukv, swa_sinks=v_swa_sinks, mix_g=v_mix_g, w_o=v_w_o, ln1_g=v_ln1_g, ln1_b=v_ln1_b,
             w_gate=v_w_gate, w_up=v_w_up, w_down=v_w_down, ln2_g=v_ln2_g, ln2_b=v_ln2_b)
    names = tuple(w)
    L = w_in.shape[0]
    S = x.shape[1]
    blk = min(BLK_SOFTMAX, S)
    bs = min(512, S)

    me = 2 * lax.axis_index("x") + lax.axis_index("y")
    axis_of = {k: SHARD_AXIS[k] - 1 for k in BIG}
    groups = (("w_in", "mla_w_uq", "mla_w_ukv"), ("w_o", "w_gate", "w_up", "w_down"))

    started, last = [], None
    for l in range(L):
        per_group = []
        for g, group in enumerate(groups):
            srcs = [w[k][l].astype(MXU_DTYPE) for k in group]
            if last is not None:
                t = min(range(len(srcs)), key=lambda i: srcs[i].size)
                srcs[t] = srcs[t] + last["token"][0, 0].astype(MXU_DTYPE)
            last = _exchange_start(srcs, scatter=False, name=f"gather_start{l}_{g}")
            per_group.append(last)
        started.append(per_group)
    all_started = sum(st["token"] for per_group in started for st in per_group)

    def gathered(l, g, after):
        mine, lands = _exchange_wait(started[l][g], after, scatter=False, name=f"gather_wait{l}_{g}")
        return [jnp.concatenate([jnp.where(me == k, mine[t], lands[t][k]) for k in range(N_CHIPS)], axis=axis_of[name])
                for t, name in enumerate(groups[g])]

    def scatter(l, g, grads):
        to_owner = [_to_chips(grads[k][None], axis_of[k] + 1)[:, 0].astype(MXU_DTYPE) for k in groups[g]]
        return _exchange_start(to_owner, scatter=True, name=f"scatter_start{l}_{g}")

    tabs = _rope_tables(S)
    Ps = []
    for l in range(L):
        P = dict(fox_b_f=fox_b_f[l], swa_sinks=swa_sinks[l])
        for k in ("mla_g_q", "mla_g_kv", "mix_g", "ln1_g", "ln1_b", "ln2_g", "ln2_b"):
            P[k] = w[k][l][None, :]
        Ps.append(P)

    xa = x[0]
    xb = xa.astype(MXU_DTYPE)
    xbT = xb.T
    saved, Ws = [], []
    for l in range(L):
        W = _prep_weights_a(*gathered(l, 0, all_started if l == 0 else xa))
        late = lambda after, l=l: _prep_weights_b(*gathered(l, 1, after))
        xa, xb, xbT, sv, W = _layer_fwd(l, xa, xb, xbT, W, Ps[l], tabs, blk, late_weights=late)
        saved.append(sv)
        Ws.append(W)
    dx, loss_part = _loss_head(xa, loss_target[0], bs=bs, name="loss_head")

    layer_grads = [None] * L
    sent = [[None, None] for _ in range(L)]
    pin = None
    for l in reversed(range(L)):
        P = Ps[l] if pin is None else dict(Ps[l], ln2_g=Ps[l]["ln2_g"] + pin[0, 0])

        def send_early(grads, l=l):
            sent[l][1] = scatter(l, 1, grads)
            return sent[l][1]["token"]

        dx, layer_grads[l] = _layer_bwd(l, dx, saved[l], Ws[l], P, tabs, blk, send_early=send_early)
        sent[l][0] = scatter(l, 0, layer_grads[l])
        pin = sent[l][0]["token"]
    grad_x = dx[None]

    me_arr = me.astype(jnp.int32)[None]
    partial = {k: jnp.zeros(w[k].shape, F32) for k in BIG}
    after = dx
    for l in reversed(range(L)):
        for g in (1, 0):
            mine, lands = _exchange_wait(sent[l][g], after, scatter=True, name=f"scatter_wait{l}_{g}")
            for t, k in enumerate(groups[g]):
                own = lax.dynamic_index_in_dim(mine[t], me, 0, keepdims=False)
                partial[k] = _sum_chips_into(partial[k], lands[t], own, me_arr, l, br=_rows(own.shape[0]),
                                             name=f"sum_{k}_l{l}")
            after = partial[groups[g][-1]]
    partial = [partial[k] for k in BIG]
    sibling = _core_exchange(partial, name="swap_partials")
    local = {k: jnp.stack([layer_grads[l][k] for l in range(L)]) for k in SMALL}
    out = {}
    for k, mine, theirs in zip(BIG, partial, sibling):
        shp = w[k].shape
        two_d = lambda a: a.reshape(shp[0] * shp[1], shp[2])
        res = _adamw(two_d(w[k]), two_d(m[k]), two_d(v[k]), two_d(mine), two_d(theirs), br=_rows(shp[0] * shp[1]),
                     name=f"adamw_{k}")
        out[k] = [a.reshape(shp) for a in res]

    shapes = {k: w[k].shape for k in SMALL}
    extra = 8 + (-L * SMALL_ROWS_PER_LAYER) % 8
    block = _pack_small({k: local[k] for k in SMALL}, extra)
    block = block.at[L * SMALL_ROWS_PER_LAYER, 0].set(loss_part[0, 0])
    total = _all_sum_small(block, name="sum_small")
    loss = total[L * SMALL_ROWS_PER_LAYER, 0]
    res = _adamw(_pack_small({k: w[k] for k in SMALL}, extra), _pack_small({k: m[k] for k in SMALL}, extra),
                 _pack_small({k: v[k] for k in SMALL}, extra), total, None, br=total.shape[0], name="adamw_small")
    res = [_unpack_small(t, shapes) for t in res]
    for k in SMALL:
        out[k] = [r[k] for r in res]

    return (loss, grad_x, *[out[k][0] for k in names], *[out[k][1] for k in names],
            *[out[k][2] for k in names], *[out[k][3] for k in names])


def _rows(n):
    for b in (256, 128, 64, 32, 16, 8):
        if n % b == 0:
            return b
    return n
```

```python
import functools

import numpy as np
import jax
import jax.numpy as jnp
from jax import lax
from jax.experimental import pallas as pl
from jax.experimental.pallas import tpu as pltpu

F32 = jnp.float32
MXU_DTYPE = jnp.bfloat16
NEG_INF = -1e30

D_MODEL = 1024
DEPTH = 4
HEAD_DIM = 64
GROUP_WIDTH = 256
D_FF = 2816
MLA_Q_RANK = 256
MLA_KV_RANK = 128
MLA_ROPE = 32
MLA_QK = 96
MLA_PAD = 128
ROPE_THETA = 10000.0
WINDOW = 128
ALPHA = (2.0 * DEPTH) ** 0.25
SWA_SLOPES = tuple(float(2.0 ** (-8.0 * h / 4)) for h in range(1, 5))
ATT_W = 2048
LAT_W = 640
PERM_W = ATT_W + LAT_W
COL_FQ, COL_FK, COL_FV = 0, 256, 512
COL_SQ, COL_SK, COL_SV = 768, 1024, 1280
COL_WQ, COL_WK, COL_WV = 1536, 1792, 1920
Q_COLSCALE = np.ones((1, ATT_W), np.float32)
Q_COLSCALE[:, COL_FQ:COL_FQ + 256] = HEAD_DIM ** -0.5
Q_COLSCALE[:, COL_SQ:COL_SQ + 256] = HEAD_DIM ** -0.5

ADAM_LR, ADAM_B1, ADAM_B2, ADAM_EPS, ADAM_WD, ADAM_STEP = 0.001, 0.9, 0.999, 1e-08, 0.01, 10

VMEM_LIMIT = 56 * 1024 * 1024
NT = (((1,), (1,)), ((), ()))
TN = (((0,), (0,)), ((), ()))
MESH = pl.DeviceIdType.MESH


def _cparams(*sem):
    return pltpu.CompilerParams(dimension_semantics=sem, vmem_limit_bytes=VMEM_LIMIT)


def _dot01(x, m01, dn=None, parts=2):
    acc = None
    rem = x
    for _ in range(parts):
        part = rem.astype(MXU_DTYPE)
        rem = rem - part.astype(F32)
        if dn is None:
            t = jnp.dot(part, m01, preferred_element_type=F32)
        else:
            t = lax.dot_general(part, m01, dn, preferred_element_type=F32)
        acc = t if acc is None else acc + t
    return acc


def _mm(a, b, *, name, ta=False, tb=False, out_dtype=F32, bm=512, bn=512, bk=512, resid=None, alpha=1.0,
        colscale=None):
    M, K = (a.shape[1], a.shape[0]) if ta else a.shape
    N = b.shape[0] if tb else b.shape[1]
    assert (b.shape[1] if tb else b.shape[0]) == K
    assert resid is None or colscale is None
    bm, bn, bk = min(bm, M), min(bn, N), min(bk, K)
    assert M % bm == 0 and N % bn == 0 and K % bk == 0, (name, M, N, K, bm, bn, bk)
    nk = K // bk
    assert nk == 1 or (out_dtype == F32 and colscale is None), name
    dn = (((0 if ta else 1,), (1 if tb else 0,)), ((), ()))

    extra = resid is not None or colscale is not None

    def body(*refs):
        a_ref, b_ref = refs[:2]
        r_ref = refs[2] if extra else None
        o_ref = refs[3] if extra else refs[2]
        k = pl.program_id(2)

        def first():
            r = lax.dot_general(a_ref[...].astype(MXU_DTYPE), b_ref[...].astype(MXU_DTYPE), dn,
                                preferred_element_type=F32)
            if resid is not None:
                r = r + alpha * r_ref[...]
            if colscale is not None:
                r = r * r_ref[...]
            o_ref[...] = r.astype(o_ref.dtype)

        if nk == 1:
            first()
        else:
            pl.when(k == 0)(first)

            @pl.when(k > 0)
            def _():
                o_ref[...] += lax.dot_general(a_ref[...].astype(MXU_DTYPE), b_ref[...].astype(MXU_DTYPE), dn,
                                              preferred_element_type=F32)

    a_spec = pl.BlockSpec((bk, bm), lambda i, j, k: (k, i)) if ta else pl.BlockSpec((bm, bk), lambda i, j, k: (i, k))
    b_spec = pl.BlockSpec((bn, bk), lambda i, j, k: (j, k)) if tb else pl.BlockSpec((bk, bn), lambda i, j, k: (k, j))
    in_specs = [a_spec, b_spec]
    args = [a, b]
    if resid is not None:
        in_specs.append(pl.BlockSpec((bm, bn), lambda i, j, k: (i, j)))
        args.append(resid)
    if colscale is not None:
        in_specs.append(pl.BlockSpec((1, bn), lambda i, j, k: (0, j)))
        args.append(colscale)
    return pl.pallas_call(
        body, name=name, grid=(M // bm, N // bn, nk), in_specs=in_specs,
        out_specs=pl.BlockSpec((bm, bn), lambda i, j, k: (i, j)),
        out_shape=jax.ShapeDtypeStruct((M, N), out_dtype),
        compiler_params=_cparams("parallel", "parallel", "arbitrary"),
    )(*args)


HP = 4
BLK_SOFTMAX = 512
BLK_STICK = 256


def _kv_blocks_t(a, blk):
    S, C = a.shape
    return a.reshape(S // blk, blk, C).transpose(0, 2, 1)


def _smax_fwd_t(qT, k, vT3, *, dk, blk, name):
    S = k.shape[0]
    nb = S // blk
    H = k.shape[1] // dk

    def body(qT_ref, k_ref, vT_ref, oT_ref, lse_ref):
        i = pl.program_id(1)
        key = lax.broadcasted_iota(jnp.int32, (blk, blk), 0)
        qry = lax.broadcasted_iota(jnp.int32, (blk, blk), 1)
        qs = [qT_ref[h * dk:(h + 1) * dk, :] for h in range(HP)]

        def tile(j, carry, masked):
            r0 = pl.multiple_of(j * blk, blk)
            ss = [jnp.dot(k_ref[pl.ds(r0, blk), h * dk:(h + 1) * dk], qs[h], preferred_element_type=F32)
                  for h in range(HP)]
            stats, pes = [], []
            for h in range(HP):
                m, l, _ = carry[h]
                s = jnp.where(key <= qry, ss[h], NEG_INF) if masked else ss[h]
                mn = jnp.maximum(m, jnp.max(s, axis=0, keepdims=True))
                a = jnp.exp(m - mn)
                pe = jnp.exp(s - mn)
                stats.append((mn, a * l + jnp.sum(pe, axis=0, keepdims=True), a))
                pes.append(pe.astype(MXU_DTYPE))
            pvs = [jnp.dot(vT_ref[j, h * HEAD_DIM:(h + 1) * HEAD_DIM, :], pes[h], preferred_element_type=F32)
                   for h in range(HP)]
            return tuple((stats[h][0], stats[h][1], stats[h][2] * carry[h][2] + pvs[h]) for h in range(HP))

        init = tuple((jnp.full((1, blk), NEG_INF, F32), jnp.zeros((1, blk), F32), jnp.zeros((HEAD_DIM, blk), F32))
                     for _ in range(HP))
        carry = lax.fori_loop(0, i, functools.partial(tile, masked=False), init)
        carry = tile(i, carry, True)
        for h in range(HP):
            m, l, acc = carry[h]
            oT_ref[h * HEAD_DIM:(h + 1) * HEAD_DIM, :] = acc / l
            lse_ref[h, 0] = m + jnp.log(l)

    return pl.pallas_call(
        body, name=name, grid=(H // HP, nb),
        in_specs=[pl.BlockSpec((HP * dk, blk), lambda p, i: (p, i)),
                  pl.BlockSpec((S, HP * dk), lambda p, i: (0, p)),
                  pl.BlockSpec((nb, HP * HEAD_DIM, blk), lambda p, i: (0, p, 0))],
        out_specs=[pl.BlockSpec((HP * HEAD_DIM, blk), lambda p, i: (p, i)),
                   pl.BlockSpec((HP, 1, 1, blk), lambda p, i: (p, i, 0, 0))],
        out_shape=[jax.ShapeDtypeStruct((H * HEAD_DIM, S), F32), jax.ShapeDtypeStruct((H, nb, 1, blk), F32)],
        compiler_params=_cparams("arbitrary", "arbitrary"),
    )(qT, k, vT3)


def _smax_bwd_t(qT, q, k, kT3, v, dmix, dmixT, oT, lse, *, dk, dcb, qscale, blk, name):
    S = k.shape[0]
    nb = S // blk
    H = k.shape[1] // dk
    hd = HP * HEAD_DIM
    dcr = dcb * 128 // hd

    def body(qT_ref, q_ref, k_ref, kT_ref, v_ref, do_ref, doT_ref, oT_ref, lse_ref, dqT_ref, dk_ref, dv_ref):
        i = pl.program_id(1)

        @pl.when(i == 0)
        def _():
            dk_ref[...] = jnp.zeros_like(dk_ref)
            dv_ref[...] = jnp.zeros_like(dv_ref)

        key = lax.broadcasted_iota(jnp.int32, (blk, blk), 0)
        qry = lax.broadcasted_iota(jnp.int32, (blk, blk), 1)
        per_head = []
        for h in range(HP):
            hs = slice(h * HEAD_DIM, (h + 1) * HEAD_DIM)
            doT = doT_ref[hs, :]
            per_head.append(dict(
                qT=qT_ref[h * dk:(h + 1) * dk, :], q=q_ref[:, h * dk:(h + 1) * dk],
                doT=doT.astype(MXU_DTYPE), do=do_ref[:, hs].astype(MXU_DTYPE),
                delta=jnp.sum(doT * oT_ref[hs, :], axis=0, keepdims=True), lse=lse_ref[h, 0]))

        def tile(j, dqs, masked):
            r0 = pl.multiple_of(j * blk, blk)
            rows = pl.ds(r0, blk)
            ksl = [slice(h * dk, (h + 1) * dk) for h in range(HP)]
            hsl = [slice(h * HEAD_DIM, (h + 1) * HEAD_DIM) for h in range(HP)]
            ss = [jnp.dot(k_ref[rows, ksl[h]], per_head[h]["qT"], preferred_element_type=F32) for h in range(HP)]
            dps = [jnp.dot(v_ref[rows, hsl[h]], per_head[h]["doT"], preferred_element_type=F32) for h in range(HP)]
            prs, dss = [], []
            for h in range(HP):
                c = per_head[h]
                s = jnp.where(key <= qry, ss[h], NEG_INF) if masked else ss[h]
                pr = jnp.exp(s - c["lse"])
                dss.append((pr * (dps[h] - c["delta"])).astype(MXU_DTYPE))
                prs.append(pr.astype(MXU_DTYPE))
            for h in range(HP):
                dv_ref[rows, hsl[h]] += jnp.dot(prs[h], per_head[h]["do"], preferred_element_type=F32)
            for h in range(HP):
                dk_ref[rows, ksl[h]] += jnp.dot(dss[h], per_head[h]["q"], preferred_element_type=F32)
            return tuple(dqs[h] + jnp.dot(kT_ref[j, ksl[h], :], dss[h], preferred_element_type=F32) for h in range(HP))

        dqs = lax.fori_loop(0, i, functools.partial(tile, masked=False),
                            tuple(jnp.zeros((dk, blk), F32) for _ in range(HP)))
        dqs = tile(i, dqs, True)
        for h in range(HP):
            dqT_ref[h * dk:(h + 1) * dk, :] = dqs[h] * qscale

    return pl.pallas_call(
        body, name=name, grid=(H // HP, nb),
        in_specs=[pl.BlockSpec((HP * dk, blk), lambda p, i: (p, i)),
                  pl.BlockSpec((blk, HP * dk), lambda p, i: (i, p)),
                  pl.BlockSpec((S, HP * dk), lambda p, i: (0, p)),
                  pl.BlockSpec((nb, HP * dk, blk), lambda p, i: (0, p, 0)),
                  pl.BlockSpec((S, hd), lambda p, i: (0, p)),
                  pl.BlockSpec((blk, hd), lambda p, i: (i, dcr + p)),
                  pl.BlockSpec((hd, blk), lambda p, i: (dcr + p, i)),
                  pl.BlockSpec((hd, blk), lambda p, i: (p, i)),
                  pl.BlockSpec((HP, 1, 1, blk), lambda p, i: (p, i, 0, 0))],
        out_specs=[pl.BlockSpec((HP * dk, blk), lambda p, i: (p, i)),
                   pl.BlockSpec((S, HP * dk), lambda p, i: (0, p)),
                   pl.BlockSpec((S, hd), lambda p, i: (0, p))],
        out_shape=[jax.ShapeDtypeStruct((H * dk, S), F32), jax.ShapeDtypeStruct((S, H * dk), F32),
                   jax.ShapeDtypeStruct((S, H * HEAD_DIM), F32)],
        compiler_params=_cparams("arbitrary", "arbitrary"),
    )(qT, q, k, kT3, v, dmix, dmixT, oT, lse)


def _log1m_beta(z):
    return -(jnp.maximum(z, 0.0) + jnp.log(1.0 + jnp.exp(-jnp.abs(z))))


def _dot01_left(m01, x, parts=2):
    acc = None
    rem = x
    for _ in range(parts):
        part = rem.astype(MXU_DTYPE)
        rem = rem - part.astype(F32)
        t = jnp.dot(m01, part, preferred_element_type=F32)
        acc = t if acc is None else acc + t
    return acc


def _sb_fwd_t(qT, h_att, vT3, *, blk, name):
    S = h_att.shape[0]
    nb = S // blk
    kcb = COL_SK // (HP * HEAD_DIM)

    def body(qT_ref, k_ref, vT_ref, oT_ref, lt_ref):
        i = pl.program_id(1)
        key = lax.broadcasted_iota(jnp.int32, (blk, blk), 0)
        qry = lax.broadcasted_iota(jnp.int32, (blk, blk), 1)
        strict = key < qry
        later = (qry > key).astype(MXU_DTYPE)
        qs = [qT_ref[h * HEAD_DIM:(h + 1) * HEAD_DIM, :] for h in range(HP)]

        def tile(j, carry, mask):
            r0 = pl.multiple_of(j * blk, blk)
            hsl = [slice(h * HEAD_DIM, (h + 1) * HEAD_DIM) for h in range(HP)]
            zs = [jnp.dot(k_ref[pl.ds(r0, blk), hsl[h]], qs[h], preferred_element_type=F32) for h in range(HP)]
            lbs = []
            for h in range(HP):
                lb = _log1m_beta(zs[h])
                lbs.append(lb if mask is None else jnp.where(mask, lb, 0.0))
            sums = [_dot01_left(later, lbs[h]) for h in range(HP)]
            probs = []
            for h in range(HP):
                a = jnp.exp(zs[h] + lbs[h] + sums[h] + carry[h][0])
                probs.append((a if mask is None else jnp.where(mask, a, 0.0)).astype(MXU_DTYPE))
            pvs = [jnp.dot(vT_ref[j, hsl[h], :], probs[h], preferred_element_type=F32) for h in range(HP)]
            return tuple((carry[h][0] + jnp.sum(lbs[h], axis=0, keepdims=True), carry[h][1] + pvs[h]) for h in range(HP))

        init = tuple((jnp.zeros((1, blk), F32), jnp.zeros((HEAD_DIM, blk), F32)) for _ in range(HP))
        carry = tile(i, init, strict)
        carry = lax.fori_loop(0, i, lambda jj, c: tile(i - 1 - jj, c, None), carry)
        for h in range(HP):
            oT_ref[h * HEAD_DIM:(h + 1) * HEAD_DIM, :] = carry[h][1]
            lt_ref[h, 0] = carry[h][0]

    hd = HP * HEAD_DIM
    return pl.pallas_call(
        body, name=name, grid=(4 // HP, nb),
        in_specs=[pl.BlockSpec((hd, blk), lambda p, i: (p, i)),
                  pl.BlockSpec((S, hd), lambda p, i: (0, kcb + p)),
                  pl.BlockSpec((nb, hd, blk), lambda p, i: (0, p, 0))],
        out_specs=[pl.BlockSpec((hd, blk), lambda p, i: (p, i)), pl.BlockSpec((HP, 1, 1, blk), lambda p, i: (p, i, 0, 0))],
        out_shape=[jax.ShapeDtypeStruct((GROUP_WIDTH, S), F32), jax.ShapeDtypeStruct((4, nb, 1, blk), F32)],
        compiler_params=_cparams("arbitrary", "arbitrary"),
    )(qT, h_att, vT3)


def _sb_bwd_t(qT, h_att, kT3, dmix, dmixT, ltot, *, dcb, qscale, blk, name):
    S = h_att.shape[0]
    nb = S // blk
    hd = HP * HEAD_DIM
    qcb, kcb, vcb = COL_SQ // hd, COL_SK // hd, COL_SV // hd
    dcr = dcb * 128 // hd

    def body(qT_ref, q_ref, k_ref, kT_ref, v_ref, do_ref, doT_ref, lt_ref, dqT_ref, dk_ref, dv_ref):
        i = pl.program_id(1)

        @pl.when(i == 0)
        def _():
            dk_ref[...] = jnp.zeros_like(dk_ref)
            dv_ref[...] = jnp.zeros_like(dv_ref)

        key = lax.broadcasted_iota(jnp.int32, (blk, blk), 0)
        qry = lax.broadcasted_iota(jnp.int32, (blk, blk), 1)
        strict = key < qry
        upto = (qry <= key).astype(MXU_DTYPE)
        before = (qry < key).astype(MXU_DTYPE)
        per_head = []
        for h in range(HP):
            hs = slice(h * HEAD_DIM, (h + 1) * HEAD_DIM)
            per_head.append(dict(qT=qT_ref[hs, :], q=q_ref[:, hs], doT=doT_ref[hs, :].astype(MXU_DTYPE),
                                 do=do_ref[:, hs].astype(MXU_DTYPE), lt=lt_ref[h, 0]))

        def tile(j, carry, mask):
            r0 = pl.multiple_of(j * blk, blk)
            rows = pl.ds(r0, blk)
            hsl = [slice(h * HEAD_DIM, (h + 1) * HEAD_DIM) for h in range(HP)]
            zs = [jnp.dot(k_ref[rows, hsl[h]], per_head[h]["qT"], preferred_element_type=F32) for h in range(HP)]
            das = [jnp.dot(v_ref[rows, hsl[h]], per_head[h]["doT"], preferred_element_type=F32) for h in range(HP)]
            lbs = []
            for h in range(HP):
                lb = _log1m_beta(zs[h])
                lbs.append(lb if mask is None else jnp.where(mask, lb, 0.0))
            sums = [_dot01_left(upto, lbs[h]) for h in range(HP)]
            probs, gs = [], []
            for h in range(HP):
                a = jnp.exp(zs[h] + lbs[h] + (per_head[h]["lt"] - carry[h][0] - sums[h]))
                a = a if mask is None else jnp.where(mask, a, 0.0)
                gs.append(das[h] * a)
                probs.append(a.astype(MXU_DTYPE))
            for h in range(HP):
                dv_ref[rows, hsl[h]] += jnp.dot(probs[h], per_head[h]["do"], preferred_element_type=F32)
            es = [_dot01_left(before, gs[h]) for h in range(HP)]
            dzs = []
            for h in range(HP):
                dz = gs[h] * jnp.exp(lbs[h]) - (carry[h][1] + es[h]) * jnp.exp(zs[h] + lbs[h])
                dzs.append((dz if mask is None else jnp.where(mask, dz, 0.0)).astype(MXU_DTYPE))
            for h in range(HP):
                dk_ref[rows, hsl[h]] += jnp.dot(dzs[h], per_head[h]["q"], preferred_element_type=F32)
            return tuple((carry[h][0] + jnp.sum(lbs[h], axis=0, keepdims=True),
                          carry[h][1] + jnp.sum(gs[h], axis=0, keepdims=True),
                          carry[h][2] + jnp.dot(kT_ref[j, hsl[h], :], dzs[h], preferred_element_type=F32))
                         for h in range(HP))

        zr = jnp.zeros((1, blk), F32)
        init = tuple((zr, zr, jnp.zeros((HEAD_DIM, blk), F32)) for _ in range(HP))
        carry = lax.fori_loop(0, i, lambda j, c: tile(j, c, None), init)
        carry = tile(i, carry, strict)
        for h in range(HP):
            dqT_ref[h * HEAD_DIM:(h + 1) * HEAD_DIM, :] = carry[h][2] * qscale

    return pl.pallas_call(
        body, name=name, grid=(4 // HP, nb),
        in_specs=[pl.BlockSpec((hd, blk), lambda p, i: (p, i)),
                  pl.BlockSpec((blk, hd), lambda p, i: (i, qcb + p)),
                  pl.BlockSpec((S, hd), lambda p, i: (0, kcb + p)),
                  pl.BlockSpec((nb, hd, blk), lambda p, i: (0, p, 0)),
                  pl.BlockSpec((S, hd), lambda p, i: (0, vcb + p)),
                  pl.BlockSpec((blk, hd), lambda p, i: (i, dcr + p)),
                  pl.BlockSpec((hd, blk), lambda p, i: (dcr + p, i)),
                  pl.BlockSpec((HP, 1, 1, blk), lambda p, i: (p, i, 0, 0))],
        out_specs=[pl.BlockSpec((hd, blk), lambda p, i: (p, i)),
                   pl.BlockSpec((S, hd), lambda p, i: (0, p)),
                   pl.BlockSpec((S, hd), lambda p, i: (0, p))],
        out_shape=[jax.ShapeDtypeStruct((GROUP_WIDTH, S), F32), jax.ShapeDtypeStruct((S, GROUP_WIDTH), F32),
                   jax.ShapeDtypeStruct((S, GROUP_WIDTH), F32)],
        compiler_params=_cparams("arbitrary", "arbitrary"),
    )(qT, h_att, h_att, kT3, h_att, dmix, dmixT, ltot)


def _swa_scores(q_ref, k_ref, n, h, start):
    g = h // 2
    kb = k_ref[pl.ds(start, 2 * WINDOW), g * HEAD_DIM:(g + 1) * HEAD_DIM]
    s = lax.dot_general(q_ref[:, h * HEAD_DIM:(h + 1) * HEAD_DIM], kb, NT, preferred_element_type=F32) * (HEAD_DIM ** -0.5)
    dist = (n * WINDOW + lax.broadcasted_iota(jnp.int32, (WINDOW, 2 * WINDOW), 0)
            - start - lax.broadcasted_iota(jnp.int32, (WINDOW, 2 * WINDOW), 1))
    s = s - SWA_SLOPES[h] * dist.astype(F32)
    valid = (dist >= 0) & (dist < WINDOW)
    return jnp.where(valid, s, NEG_INF), kb


def _swa_fwd(h_att, sinks, *, name):
    S = h_att.shape[0]
    nb = S // WINDOW
    qcb, kcb, vcb = COL_WQ // 256, COL_WK // 128, COL_WV // 128

    def body(sink_ref, q_ref, k_ref, v_ref, o_ref, lse_ref):
        n = pl.program_id(0)
        start = pl.multiple_of(jnp.maximum(n - 1, 0) * WINDOW, WINDOW)
        scores = [_swa_scores(q_ref, k_ref, n, h, start)[0] for h in range(4)]
        probs = []
        for h in range(4):
            sink = sink_ref[h]
            m = jnp.maximum(jnp.max(scores[h], axis=1, keepdims=True), sink)
            e = jnp.exp(scores[h] - m)
            den = jnp.sum(e, axis=1, keepdims=True) + jnp.exp(sink - m)
            probs.append((e / den).astype(MXU_DTYPE))
            lse_ref[h] = m + jnp.log(den)
        for h in range(4):
            vb = v_ref[pl.ds(start, 2 * WINDOW), (h // 2) * HEAD_DIM:(h // 2 + 1) * HEAD_DIM]
            o_ref[:, h * HEAD_DIM:(h + 1) * HEAD_DIM] = jnp.dot(probs[h], vb, preferred_element_type=F32)

    return pl.pallas_call(
        body, name=name, grid=(nb,),
        in_specs=[pl.BlockSpec(memory_space=pltpu.SMEM),
                  pl.BlockSpec((WINDOW, 256), lambda n: (n, qcb)),
                  pl.BlockSpec((S, 128), lambda n: (0, kcb)),
                  pl.BlockSpec((S, 128), lambda n: (0, vcb))],
        out_specs=[pl.BlockSpec((WINDOW, 256), lambda n: (n, 0)), pl.BlockSpec((4, WINDOW, 1), lambda n: (0, n, 0))],
        out_shape=[jax.ShapeDtypeStruct((S, GROUP_WIDTH), F32), jax.ShapeDtypeStruct((4, S, 1), F32)],
        compiler_params=_cparams("arbitrary"),
    )(sinks, h_att, h_att, h_att)


def _swa_bwd(h_att, sinks, dmix, o_arr, lse, *, dcb, name):
    S = h_att.shape[0]
    nb = S // WINDOW
    qcb, kcb, vcb = COL_WQ // 256, COL_WK // 128, COL_WV // 128

    def body(sink_ref, q_ref, k_ref, v_ref, do_ref, o_ref, lse_ref, dq_ref, dk_ref, dv_ref, dsink_ref):
        n = pl.program_id(0)

        @pl.when(n == 0)
        def _():
            dk_ref[...] = jnp.zeros_like(dk_ref)
            dv_ref[...] = jnp.zeros_like(dv_ref)
            dsink_ref[...] = jnp.zeros_like(dsink_ref)

        start = pl.multiple_of(jnp.maximum(n - 1, 0) * WINDOW, WINDOW)
        rows = pl.ds(start, 2 * WINDOW)
        hsl = [slice(h * HEAD_DIM, (h + 1) * HEAD_DIM) for h in range(4)]
        gsl = [slice(g * HEAD_DIM, (g + 1) * HEAD_DIM) for g in range(2)]
        scale = HEAD_DIM ** -0.5
        sk = [_swa_scores(q_ref, k_ref, n, h, start) for h in range(4)]
        dobs = [do_ref[:, hsl[h]].astype(MXU_DTYPE) for h in range(4)]
        dps = [lax.dot_general(dobs[h], v_ref[rows, gsl[h // 2]], NT, preferred_element_type=F32) for h in range(4)]
        prs, dss = [], []
        for h in range(4):
            lse_h = lse_ref[h]
            pr = jnp.exp(sk[h][0] - lse_h)
            delta = jnp.sum(do_ref[:, hsl[h]] * o_ref[:, hsl[h]], axis=1, keepdims=True)
            dss.append((pr * (dps[h] - delta)).astype(MXU_DTYPE))
            prs.append(pr.astype(MXU_DTYPE))
            dsink_ref[h:h + 1, :] += jnp.zeros((1, 128), F32) - jnp.sum(jnp.exp(sink_ref[h] - lse_h) * delta)
        for h in range(4):
            dq_ref[:, hsl[h]] = jnp.dot(dss[h], sk[h][1], preferred_element_type=F32) * scale
        for g in range(2):
            dk_ref[rows, gsl[g]] += (lax.dot_general(dss[2 * g], q_ref[:, hsl[2 * g]], TN, preferred_element_type=F32)
                                     + lax.dot_general(dss[2 * g + 1], q_ref[:, hsl[2 * g + 1]], TN,
                                                       preferred_element_type=F32)) * scale
            dv_ref[rows, gsl[g]] += (lax.dot_general(prs[2 * g], dobs[2 * g], TN, preferred_element_type=F32)
                                     + lax.dot_general(prs[2 * g + 1], dobs[2 * g + 1], TN, preferred_element_type=F32))

    return pl.pallas_call(
        body, name=name, grid=(nb,),
        in_specs=[pl.BlockSpec(memory_space=pltpu.SMEM),
                  pl.BlockSpec((WINDOW, 256), lambda n: (n, qcb)),
                  pl.BlockSpec((S, 128), lambda n: (0, kcb)),
                  pl.BlockSpec((S, 128), lambda n: (0, vcb)),
                  pl.BlockSpec((WINDOW, 256), lambda n: (n, dcb)),
                  pl.BlockSpec((WINDOW, 256), lambda n: (n, 0)),
                  pl.BlockSpec((4, WINDOW, 1), lambda n: (0, n, 0))],
        out_specs=[pl.BlockSpec((WINDOW, 256), lambda n: (n, 0)),
                   pl.BlockSpec((S, 128), lambda n: (0, 0)),
                   pl.BlockSpec((S, 128), lambda n: (0, 0)),
                   pl.BlockSpec((4, 128), lambda n: (0, 0))],
        out_shape=[jax.ShapeDtypeStruct((S, GROUP_WIDTH), F32), jax.ShapeDtypeStruct((S, 128), F32),
                   jax.ShapeDtypeStruct((S, 128), F32), jax.ShapeDtypeStruct((4, 128), F32)],
        compiler_params=_cparams("arbitrary"),
    )(sinks, h_att, h_att, h_att, dmix, o_arr, lse)


def _tri(n, incl, upper):
    r = lax.broadcasted_iota(jnp.int32, (n, n), 0)
    c = lax.broadcasted_iota(jnp.int32, (n, n), 1)
    if upper:
        m = (r <= c) if incl else (r < c)
    else:
        m = (r >= c) if incl else (r > c)
    return m.astype(MXU_DTYPE)


def _fox_gate_fwd(fg, b_f, *, name):
    _, R, _ = fg.shape

    def body(b_ref, fg_ref, pos_ref, neg_ref):
        up_incl = _tri(128, True, True)
        ones = jnp.ones((128, 128), MXU_DTYPE)
        for h in range(4):
            z = fg_ref[h] + b_ref[h]
            logf = jnp.minimum(z, 0.0) - jnp.log(1.0 + jnp.exp(-jnp.abs(z)))
            within = _dot01(logf, up_incl, parts=3)
            totals = _dot01(logf, ones, parts=3)
            rem = within + _rows_other(totals, R, after=False)
            for part in range(3):
                piece = rem.astype(MXU_DTYPE)
                rem = rem - piece.astype(F32)
                pos_ref[h, part] = piece
                neg_ref[h, part] = -piece

    shape = (4, 3) + fg.shape[1:]
    return pl.pallas_call(
        body, name=name,
        in_specs=[pl.BlockSpec(memory_space=pltpu.SMEM), pl.BlockSpec(memory_space=pltpu.VMEM)],
        out_specs=[pl.BlockSpec(memory_space=pltpu.VMEM)] * 2,
        out_shape=[jax.ShapeDtypeStruct(shape, MXU_DTYPE)] * 2,
    )(b_f, fg)


def _rows_other(totals, n, after):
    r = lax.broadcasted_iota(jnp.int32, (n, n), 0)
    c = lax.broadcasted_iota(jnp.int32, (n, n), 1)
    m = ((c > r) if after else (c < r)).astype(MXU_DTYPE)
    acc = None
    rem = totals
    for _ in range(3):
        part = rem.astype(MXU_DTYPE)
        rem = rem - part.astype(F32)
        t = jnp.dot(m, part, preferred_element_type=F32)
        acc = t if acc is None else acc + t
    return acc


def _fox_gate_bwd(fg, b_f, dcum_k, dcum_q, *, q_unscale, name):
    _, R, _ = fg.shape

    def body(b_ref, fg_ref, dck_ref, dcq_ref, dfg_ref, db_ref):
        low_incl = _tri(128, True, False)
        ones = jnp.ones((128, 128), MXU_DTYPE)
        for h in range(4):
            dc = dcq_ref[h] * q_unscale - dck_ref[h]
            dlogf = _dot01(dc, low_incl, parts=3) + _rows_other(_dot01(dc, ones, parts=3), R, after=True)
            z = fg_ref[h] + b_ref[h]
            dz = dlogf * jnp.exp(jnp.minimum(-z, 0.0) - jnp.log(1.0 + jnp.exp(-jnp.abs(z))))
            dfg_ref[h] = dz
            db_ref[h:h + 1, :] = jnp.zeros((1, 128), F32) + jnp.sum(dz)

    return pl.pallas_call(
        body, name=name,
        in_specs=[pl.BlockSpec(memory_space=pltpu.SMEM)] + [pl.BlockSpec(memory_space=pltpu.VMEM)] * 3,
        out_specs=[pl.BlockSpec(memory_space=pltpu.VMEM), pl.BlockSpec(memory_space=pltpu.VMEM)],
        out_shape=[jax.ShapeDtypeStruct(fg.shape, F32), jax.ShapeDtypeStruct((4, 128), F32)],
    )(b_f, fg, dcum_k, dcum_q)


def _rope_rot(transpose):
    r = lax.broadcasted_iota(jnp.int32, (MLA_PAD, MLA_PAD), 0)
    c = lax.broadcasted_iota(jnp.int32, (MLA_PAD, MLA_PAD), 1)
    if transpose:
        r, c = c, r
    half = MLA_ROPE // 2
    lo, mid, hi = HEAD_DIM, HEAD_DIM + half, HEAD_DIM + MLA_ROPE
    minus = (c >= lo) & (c < mid) & (r == c + half)
    plus = (c >= mid) & (c < hi) & (r == c - half)
    return jnp.where(plus, 1.0, jnp.where(minus, -1.0, 0.0)).astype(MXU_DTYPE)


def _rope_lanes():
    lane = lax.broadcasted_iota(jnp.int32, (1, MLA_PAD), 1)
    return ((lane >= HEAD_DIM) & (lane < HEAD_DIM + MLA_ROPE)).astype(F32)


def _rms(x, g, eps=1e-6):
    r = lax.rsqrt(jnp.mean(x * x, axis=-1, keepdims=True) + eps)
    return x * r * g, r


def _rms_bwd(dy, x, r, g):
    xh = x * r
    dxh = dy * g
    dx = r * (dxh - xh * jnp.mean(dxh * xh, axis=-1, keepdims=True))
    return dx, dy * xh


def _mla_prep_fwd(lat, g_q, g_kv, wuq, wuk, wuv, cosm, sinm, *, bs, name):
    S = lat.shape[0]

    def body(lat_ref, gq_ref, gkv_ref, wuq_ref, wuk_ref, wuv_ref, cos_ref, sin_ref,
             q_ref, k_ref, v_ref, qT_ref, kT_ref, vT_ref):
        rot = _rope_rot(False)
        cosm_, sinm_ = cos_ref[...], sin_ref[...]
        nq, _ = _rms(lat_ref[:, 0:MLA_Q_RANK], gq_ref[...])
        nkv, _ = _rms(lat_ref[:, MLA_Q_RANK:MLA_Q_RANK + MLA_KV_RANK], gkv_ref[...])
        qlat = jnp.dot(nq.astype(MXU_DTYPE), wuq_ref[...], preferred_element_type=F32)
        klat = jnp.dot(nkv.astype(MXU_DTYPE), wuk_ref[...], preferred_element_type=F32)
        v = jnp.dot(nkv.astype(MXU_DTYPE), wuv_ref[...], preferred_element_type=F32)
        v_ref[...] = v.astype(v_ref.dtype)
        vT_ref[0] = v.T.astype(vT_ref.dtype)
        krb = lat_ref[:, 384:512]
        kr = krb * (cosm_ * _rope_lanes()) + _dot01(krb, rot, parts=3) * sinm_
        for h in range(4):
            sl = slice(h * MLA_PAD, (h + 1) * MLA_PAD)
            qh = qlat[:, sl]
            q = (qh * cosm_ + _dot01(qh, rot, parts=3) * sinm_) * (MLA_QK ** -0.5)
            k = klat[:, sl] + kr
            q_ref[:, sl] = q.astype(q_ref.dtype)
            k_ref[:, sl] = k.astype(k_ref.dtype)
            qT_ref[sl, :] = q.T.astype(qT_ref.dtype)
            kT_ref[0, sl, :] = k.T.astype(kT_ref.dtype)

    full = lambda a: pl.BlockSpec(a.shape, lambda i: (0,) * a.ndim)
    return pl.pallas_call(
        body, name=name, grid=(S // bs,),
        in_specs=[pl.BlockSpec((bs, LAT_W), lambda i: (i, 0)), full(g_q), full(g_kv), full(wuq), full(wuk), full(wuv),
                  pl.BlockSpec((bs, MLA_PAD), lambda i: (i, 0)), pl.BlockSpec((bs, MLA_PAD), lambda i: (i, 0))],
        out_specs=[pl.BlockSpec((bs, 512), lambda i: (i, 0)), pl.BlockSpec((bs, 512), lambda i: (i, 0)),
                   pl.BlockSpec((bs, 256), lambda i: (i, 0)), pl.BlockSpec((512, bs), lambda i: (0, i)),
                   pl.BlockSpec((1, 512, bs), lambda i: (i, 0, 0)), pl.BlockSpec((1, 256, bs), lambda i: (i, 0, 0))],
        out_shape=[jax.ShapeDtypeStruct((S, 512), MXU_DTYPE), jax.ShapeDtypeStruct((S, 512), MXU_DTYPE),
                   jax.ShapeDtypeStruct((S, 256), MXU_DTYPE), jax.ShapeDtypeStruct((512, S), MXU_DTYPE),
                   jax.ShapeDtypeStruct((S // bs, 512, bs), MXU_DTYPE), jax.ShapeDtypeStruct((S // bs, 256, bs), MXU_DTYPE)],
        compiler_params=_cparams("parallel"),
    )(lat, g_q, g_kv, wuq, wuk, wuv, cosm, sinm)


def _mla_prep_bwd(lat, g_q, g_kv, wuq, wuk, wuv, cosm, sinm, dq, dk, dv, *, bs, name):
    S = lat.shape[0]

    def body(lat_ref, gq_ref, gkv_ref, wuq_ref, wuk_ref, wuv_ref, cos_ref, sin_ref, dq_ref, dk_ref, dv_ref,
             dlat_ref, dwuq_ref, dwuk_ref, dwuv_ref, dgq_ref, dgkv_ref):
        @pl.when(pl.program_id(0) == 0)
        def _():
            for r in (dwuq_ref, dwuk_ref, dwuv_ref, dgq_ref, dgkv_ref):
                r[...] = jnp.zeros_like(r)

        rot_t = _rope_rot(True)
        cosm_, sinm_ = cos_ref[...], sin_ref[...]
        cq = lat_ref[:, 0:MLA_Q_RANK]
        ckv = lat_ref[:, MLA_Q_RANK:MLA_Q_RANK + MLA_KV_RANK]
        nq, rq = _rms(cq, gq_ref[...])
        nkv, rkv = _rms(ckv, gkv_ref[...])
        nqb, nkvb = nq.astype(MXU_DTYPE), nkv.astype(MXU_DTYPE)

        dqlat = []
        dkr = jnp.zeros((bs, MLA_PAD), F32)
        for h in range(4):
            sl = slice(h * MLA_PAD, (h + 1) * MLA_PAD)
            dqh = dq_ref[sl, :].T
            dqlat.append(dqh * cosm_ + _dot01(dqh * sinm_, rot_t, parts=3))
            dkr = dkr + dk_ref[:, sl]
        dqlat = jnp.concatenate(dqlat, axis=1).astype(MXU_DTYPE)
        dkb = dk_ref[...].astype(MXU_DTYPE)
        dvb = dv_ref[...].astype(MXU_DTYPE)

        dnq = lax.dot_general(dqlat, wuq_ref[...], NT, preferred_element_type=F32)
        dnkv = (lax.dot_general(dkb, wuk_ref[...], NT, preferred_element_type=F32)
                + lax.dot_general(dvb, wuv_ref[...], NT, preferred_element_type=F32))
        dwuq_ref[...] += lax.dot_general(nqb, dqlat, TN, preferred_element_type=F32)
        dwuk_ref[...] += lax.dot_general(nkvb, dkb, TN, preferred_element_type=F32)
        dwuv_ref[...] += lax.dot_general(nkvb, dvb, TN, preferred_element_type=F32)
        dcq, tq = _rms_bwd(dnq, cq, rq, gq_ref[...])
        dckv, tkv = _rms_bwd(dnkv, ckv, rkv, gkv_ref[...])
        dgq_ref[...] += jnp.sum(tq, axis=0, keepdims=True)
        dgkv_ref[...] += jnp.sum(tkv, axis=0, keepdims=True)
        dlat_ref[:, 0:MLA_Q_RANK] = dcq.astype(dlat_ref.dtype)
        dlat_ref[:, MLA_Q_RANK:MLA_Q_RANK + MLA_KV_RANK] = dckv.astype(dlat_ref.dtype)
        dkrb = dkr * (cosm_ * _rope_lanes()) + _dot01(dkr * sinm_, rot_t, parts=3)
        dlat_ref[:, 384:512] = dkrb.astype(dlat_ref.dtype)

    full = lambda a: pl.BlockSpec(a.shape, lambda i: (0,) * a.ndim)
    row = lambda w: pl.BlockSpec((bs, w), lambda i: (i, 0))
    acc = lambda *shape: pl.BlockSpec(shape, lambda i: (0,) * len(shape))
    return pl.pallas_call(
        body, name=name, grid=(S // bs,),
        in_specs=[row(LAT_W), full(g_q), full(g_kv), full(wuq), full(wuk), full(wuv), row(MLA_PAD), row(MLA_PAD),
                  pl.BlockSpec((512, bs), lambda i: (0, i)), row(512), row(256)],
        out_specs=[row(512), acc(256, 512), acc(128, 512), acc(128, 256), acc(1, 256), acc(1, 128)],
        out_shape=[jax.ShapeDtypeStruct((S, 512), MXU_DTYPE), jax.ShapeDtypeStruct((256, 512), F32),
                   jax.ShapeDtypeStruct((128, 512), F32), jax.ShapeDtypeStruct((128, 256), F32),
                   jax.ShapeDtypeStruct((1, 256), F32), jax.ShapeDtypeStruct((1, 128), F32)],
        compiler_params=_cparams("arbitrary"),
    )(lat, g_q, g_kv, wuq, wuk, wuv, cosm, sinm, dq, dk, dv)


def _row_spec(bs, w):
    return pl.BlockSpec((bs, w), lambda i: (i, 0))


def _vec_spec(w):
    return pl.BlockSpec((1, w), lambda i: (0, 0))


def _mix_specs(bs):
    return [pl.BlockSpec((GROUP_WIDTH, bs), lambda i: (0, i))] * 3 + [_row_spec(bs, GROUP_WIDTH)]


def _mix_groups(a_ref, b_ref, c_ref, d_ref):
    return [a_ref[...].T, b_ref[...].T, c_ref[...].T, d_ref[...]]


def _gnorm_fwd(outs, g, *, bs, name):
    S = outs[3].shape[0]

    def body(a_ref, b_ref, c_ref, d_ref, g_ref, o_ref, oT_ref):
        for k, x in enumerate(_mix_groups(a_ref, b_ref, c_ref, d_ref)):
            sl = slice(k * GROUP_WIDTH, (k + 1) * GROUP_WIDTH)
            y, _ = _rms(x, g_ref[:, sl])
            o_ref[:, sl] = y.astype(o_ref.dtype)
            oT_ref[sl, :] = y.T.astype(oT_ref.dtype)

    return pl.pallas_call(
        body, name=name, grid=(S // bs,),
        in_specs=_mix_specs(bs) + [_vec_spec(D_MODEL)],
        out_specs=[_row_spec(bs, D_MODEL), pl.BlockSpec((D_MODEL, bs), lambda i: (0, i))],
        out_shape=[jax.ShapeDtypeStruct((S, D_MODEL), MXU_DTYPE), jax.ShapeDtypeStruct((D_MODEL, S), MXU_DTYPE)],
        compiler_params=_cparams("parallel"),
    )(*outs, g)


def _gnorm_bwd(dgn, outs, g, *, bs, name):
    S = dgn.shape[0]

    def body(dgn_ref, a_ref, b_ref, c_ref, d_ref, g_ref, dmix_ref, dmixT_ref, dg_ref):
        @pl.when(pl.program_id(0) == 0)
        def _():
            dg_ref[...] = jnp.zeros_like(dg_ref)

        for k, x in enumerate(_mix_groups(a_ref, b_ref, c_ref, d_ref)):
            sl = slice(k * GROUP_WIDTH, (k + 1) * GROUP_WIDTH)
            _, r = _rms(x, g_ref[:, sl])
            dx, t = _rms_bwd(dgn_ref[:, sl], x, r, g_ref[:, sl])
            dmix_ref[:, sl] = dx
            dmixT_ref[sl, :] = dx.T
            dg_ref[:, sl] += jnp.sum(t, axis=0, keepdims=True)

    return pl.pallas_call(
        body, name=name, grid=(S // bs,),
        in_specs=[_row_spec(bs, D_MODEL)] + _mix_specs(bs) + [_vec_spec(D_MODEL)],
        out_specs=[_row_spec(bs, D_MODEL), pl.BlockSpec((D_MODEL, bs), lambda i: (0, i)), _vec_spec(D_MODEL)],
        out_shape=[jax.ShapeDtypeStruct((S, D_MODEL), F32), jax.ShapeDtypeStruct((D_MODEL, S), F32),
                   jax.ShapeDtypeStruct((1, D_MODEL), F32)],
        compiler_params=_cparams("arbitrary"),
    )(dgn, *outs, g)


def _ln_fwd(u, g, b, *, bs, name):
    S = u.shape[0]

    def body(u_ref, g_ref, b_ref, y_ref, yb_ref, ybT_ref, xh_ref, rs_ref):
        x = u_ref[...]
        mu = jnp.mean(x, axis=-1, keepdims=True)
        xc = x - mu
        rs = lax.rsqrt(jnp.mean(xc * xc, axis=-1, keepdims=True) + 1e-5)
        xh = xc * rs
        y = xh * g_ref[...] + b_ref[...]
        y_ref[...] = y
        yb_ref[...] = y.astype(yb_ref.dtype)
        ybT_ref[...] = y.T.astype(ybT_ref.dtype)
        xh_ref[...] = xh
        rs_ref[...] = rs

    return pl.pallas_call(
        body, name=name, grid=(S // bs,),
        in_specs=[_row_spec(bs, D_MODEL), _vec_spec(D_MODEL), _vec_spec(D_MODEL)],
        out_specs=[_row_spec(bs, D_MODEL), _row_spec(bs, D_MODEL), pl.BlockSpec((D_MODEL, bs), lambda i: (0, i)),
                   _row_spec(bs, D_MODEL), _row_spec(bs, 1)],
        out_shape=[jax.ShapeDtypeStruct((S, D_MODEL), F32), jax.ShapeDtypeStruct((S, D_MODEL), MXU_DTYPE),
                   jax.ShapeDtypeStruct((D_MODEL, S), MXU_DTYPE), jax.ShapeDtypeStruct((S, D_MODEL), F32),
                   jax.ShapeDtypeStruct((S, 1), F32)],
        compiler_params=_cparams("parallel"),
    )(u, g, b)


def _ln_bwd(dy, xh, rs, g, *, bs, name):
    S = dy.shape[0]

    def body(dy_ref, xh_ref, rs_ref, g_ref, du_ref, dub_ref, dg_ref, db_ref):
        @pl.when(pl.program_id(0) == 0)
        def _():
            dg_ref[...] = jnp.zeros_like(dg_ref)
            db_ref[...] = jnp.zeros_like(db_ref)

        dy_, xh_ = dy_ref[...], xh_ref[...]
        dxh = dy_ * g_ref[...]
        du = rs_ref[...] * (dxh - jnp.mean(dxh, axis=-1, keepdims=True)
                            - xh_ * jnp.mean(dxh * xh_, axis=-1, keepdims=True))
        du_ref[...] = du
        dub_ref[...] = du.astype(dub_ref.dtype)
        dg_ref[...] += jnp.sum(dy_ * xh_, axis=0, keepdims=True)
        db_ref[...] += jnp.sum(dy_, axis=0, keepdims=True)

    return pl.pallas_call(
        body, name=name, grid=(S // bs,),
        in_specs=[_row_spec(bs, D_MODEL), _row_spec(bs, D_MODEL), _row_spec(bs, 1), _vec_spec(D_MODEL)],
        out_specs=[_row_spec(bs, D_MODEL), _row_spec(bs, D_MODEL), _vec_spec(D_MODEL), _vec_spec(D_MODEL)],
        out_shape=[jax.ShapeDtypeStruct((S, D_MODEL), F32), jax.ShapeDtypeStruct((S, D_MODEL), MXU_DTYPE),
                   jax.ShapeDtypeStruct((1, D_MODEL), F32), jax.ShapeDtypeStruct((1, D_MODEL), F32)],
        compiler_params=_cparams("arbitrary"),
    )(dy, xh, rs, g)


def _swiglu_fwd(gu, *, bs, name):
    S = gu.shape[0]

    def body(gu_ref, a_ref, aT_ref):
        gt = gu_ref[:, :D_FF]
        a = gt / (1.0 + jnp.exp(-gt)) * gu_ref[:, D_FF:]
        a_ref[...] = a.astype(a_ref.dtype)
        aT_ref[...] = a.T.astype(aT_ref.dtype)

    return pl.pallas_call(
        body, name=name, grid=(S // bs,),
        in_specs=[_row_spec(bs, 2 * D_FF)],
        out_specs=[_row_spec(bs, D_FF), pl.BlockSpec((D_FF, bs), lambda i: (0, i))],
        out_shape=[jax.ShapeDtypeStruct((S, D_FF), MXU_DTYPE), jax.ShapeDtypeStruct((D_FF, S), MXU_DTYPE)],
        compiler_params=_cparams("parallel"),
    )(gu)


def _swiglu_bwd(da, gu, *, bs, name):
    S = gu.shape[0]

    def body(da_ref, gu_ref, dgu_ref):
        gt, da_ = gu_ref[:, :D_FF], da_ref[...]
        sg = 1.0 / (1.0 + jnp.exp(-gt))
        silu = gt * sg
        dgu_ref[:, :D_FF] = (da_ * gu_ref[:, D_FF:] * (sg + silu * (1.0 - sg))).astype(dgu_ref.dtype)
        dgu_ref[:, D_FF:] = (da_ * silu).astype(dgu_ref.dtype)

    return pl.pallas_call(
        body, name=name, grid=(S // bs,),
        in_specs=[_row_spec(bs, D_FF), _row_spec(bs, 2 * D_FF)],
        out_specs=_row_spec(bs, 2 * D_FF), out_shape=jax.ShapeDtypeStruct((S, 2 * D_FF), MXU_DTYPE),
        compiler_params=_cparams("parallel"),
    )(da, gu)


def _loss_head(y, target, *, bs, name):
    S = y.shape[0]

    def body(y_ref, t_ref, dy_ref, loss_ref):
        @pl.when(pl.program_id(0) == 0)
        def _():
            loss_ref[...] = jnp.zeros_like(loss_ref)

        e = y_ref[...] - t_ref[...]
        dy_ref[...] = e * (1.0 / D_MODEL)
        per_tok = jnp.mean(e * e, axis=-1, keepdims=True)
        loss_ref[...] += 0.5 * jnp.sum(per_tok, axis=0, keepdims=True)

    return pl.pallas_call(
        body, name=name, grid=(S // bs,),
        in_specs=[_row_spec(bs, D_MODEL), _row_spec(bs, D_MODEL)],
        out_specs=[_row_spec(bs, D_MODEL), pl.BlockSpec((1, 1), lambda i: (0, 0))],
        out_shape=[jax.ShapeDtypeStruct((S, D_MODEL), F32), jax.ShapeDtypeStruct((1, 1), F32)],
        compiler_params=_cparams("arbitrary"),
    )(y, target)


def _blk(n, target):
    if n <= target:
        return n
    best = None
    for b in range(128, target + 1, 128):
        if n % b == 0:
            best = b
    assert best is not None, n
    return best


def _rope_tables(S):
    pos = jnp.arange(S, dtype=F32)
    inv = ROPE_THETA ** (-jnp.arange(0, MLA_ROPE, 2, dtype=F32) / MLA_ROPE)
    ang = pos[:, None] * inv[None, :]
    cos, sin = jnp.cos(ang), jnp.sin(ang)
    one, zero, pad = jnp.ones((S, HEAD_DIM), F32), jnp.zeros((S, HEAD_DIM), F32), jnp.zeros((S, MLA_PAD - MLA_QK), F32)
    return jnp.concatenate([one, cos, cos, pad], axis=1), jnp.concatenate([zero, sin, sin, pad], axis=1)


def _prep_weights_a(w_in, w_uq, w_ukv):
    z = lambda n: jnp.zeros((D_MODEL, n), w_in.dtype)
    win_a = jnp.concatenate([w_in[:, 0:768], w_in[:, 1188:2468]], axis=1)
    win_l = jnp.concatenate([w_in[:, 772:1156], z(64), w_in[:, 1156:1188], z(32), w_in[:, 768:772], z(124)], axis=1)
    kv = w_ukv.reshape(MLA_KV_RANK, 4, 2 * HEAD_DIM)
    return dict(
        win_a=win_a, win_l=win_l, win_p=jnp.concatenate([win_a, win_l], axis=1),
        wuq=jnp.pad(w_uq.reshape(MLA_Q_RANK, 4, MLA_QK), ((0, 0), (0, 0), (0, MLA_PAD - MLA_QK))).reshape(MLA_Q_RANK, 512),
        wuk=jnp.pad(kv[:, :, :HEAD_DIM], ((0, 0), (0, 0), (0, HEAD_DIM))).reshape(MLA_KV_RANK, 512),
        wuv=kv[:, :, HEAD_DIM:].reshape(MLA_KV_RANK, 256))


def _unprep_grads(dwin_p, dwuq, dwuk, dwuv, dwo, dwgu, dwd):
    dw_in = jnp.concatenate([dwin_p[:, 0:768], dwin_p[:, 2560:2564], dwin_p[:, 2048:2432], dwin_p[:, 2496:2528],
                             dwin_p[:, 768:2048]], axis=1)
    dw_uq = dwuq.reshape(MLA_Q_RANK, 4, MLA_PAD)[:, :, :MLA_QK].reshape(MLA_Q_RANK, 4 * MLA_QK)
    dw_ukv = jnp.concatenate([dwuk.reshape(MLA_KV_RANK, 4, MLA_PAD)[:, :, :HEAD_DIM],
                              dwuv.reshape(MLA_KV_RANK, 4, HEAD_DIM)], axis=2).reshape(MLA_KV_RANK, 512)
    return dict(w_in=dw_in, mla_w_uq=dw_uq, mla_w_ukv=dw_ukv, w_o=dwo, w_gate=dwgu[:, :D_FF], w_up=dwgu[:, D_FF:],
                w_down=dwd)


def _layer_fwd(l, x, xb, xbT, W, P, tabs, blk, late_weights=None):
    S = x.shape[0]
    nb = S // blk
    n = lambda s: f"l{l}_{s}"
    bs = min(512, S)
    h_att = _mm(xb, W["win_a"], name=n("in_att"), out_dtype=MXU_DTYPE, bm=1024, bn=1024, bk=1024, colscale=Q_COLSCALE)
    lat = _mm(xb, W["win_l"], name=n("in_lat"), bm=2048, bn=LAT_W, bk=1024)
    fg = lat[:, 512:516].T.reshape(4, S // 128, 128)
    cpos, cneg = _fox_gate_fwd(fg, P["fox_b_f"], name=n("fox_gate"))
    one3 = jnp.ones((S, 4, 3), MXU_DTYPE)
    zpad = jnp.zeros((S, 4, MLA_PAD - HEAD_DIM - 6), MXU_DTYPE)
    per_tok = lambda parts: parts.reshape(4, 3, S).transpose(2, 0, 1)
    q_f = jnp.concatenate([h_att[:, COL_FQ:COL_FQ + 256].reshape(S, 4, HEAD_DIM), per_tok(cpos), one3, zpad],
                          axis=2).reshape(S, 4 * MLA_PAD)
    k_f = jnp.concatenate([h_att[:, COL_FK:COL_FK + 256].reshape(S, 4, HEAD_DIM), one3, per_tok(cneg), zpad],
                          axis=2).reshape(S, 4 * MLA_PAD)
    v_f = h_att[:, COL_FV:COL_FV + 256]
    oT_a, lse_a = _smax_fwd_t(q_f.T, k_f, _kv_blocks_t(v_f, blk), dk=MLA_PAD, blk=blk, name=n("fox_fwd"))
    q_m, k_m, v_m, qT_m, kT3_m, vT3_m = _mla_prep_fwd(lat, P["mla_g_q"], P["mla_g_kv"], W["wuq"], W["wuk"], W["wuv"],
                                                      *tabs, bs=blk, name=n("mla_prep"))
    oT_b, lse_b = _smax_fwd_t(qT_m, k_m, vT3_m, dk=MLA_PAD, blk=blk, name=n("mla_fwd"))
    qT_c = h_att[:, COL_SQ:COL_SQ + 256].T
    bsb = min(BLK_STICK, S)
    oT_c, lt_c = _sb_fwd_t(qT_c, h_att, _kv_blocks_t(h_att[:, COL_SV:COL_SV + 256], bsb), blk=bsb, name=n("sb_fwd"))
    out_d, lse_d = _swa_fwd(h_att, P["swa_sinks"], name=n("swa_fwd"))
    outs = (oT_a, oT_b, oT_c, out_d)
    gn, gnT = _gnorm_fwd(outs, P["mix_g"], bs=bs, name=n("gnorm"))
    if late_weights is not None:
        W = dict(W, **late_weights(gn))
    u1 = _mm(gn, W["w_o"], name=n("out_proj"), bm=1024, bn=1024, bk=1024, resid=x, alpha=ALPHA)
    x1, x1b, x1bT, xh1, rs1 = _ln_fwd(u1, P["ln1_g"], P["ln1_b"], bs=bs, name=n("ln1"))
    gu = _mm(x1b, W["wgu"], name=n("gate_up"), bm=2048, bn=512, bk=1024)
    a, aT = _swiglu_fwd(gu, bs=min(256, S), name=n("swiglu"))
    u2 = _mm(a, W["w_down"], name=n("down"), bm=1024, bn=1024, bk=_blk(D_FF, 1408), resid=x1, alpha=ALPHA)
    x2, x2b, x2bT, xh2, rs2 = _ln_fwd(u2, P["ln2_g"], P["ln2_b"], bs=bs, name=n("ln2"))
    saved = dict(xbT=xbT, gnT=gnT, x1bT=x1bT, h_att=h_att, lat=lat, fg=fg, outs=outs, oT_a=oT_a, oT_b=oT_b, q_f=q_f, k_f=k_f, v_f=v_f,
                 qT_c=qT_c, lse_a=lse_a, lse_b=lse_b, lse_d=lse_d, lt_c=lt_c, q_m=q_m, k_m=k_m, v_m=v_m, qT_m=qT_m, kT3_m=kT3_m,
                 xh1=xh1, rs1=rs1, gu=gu, aT=aT, xh2=xh2, rs2=rs2)
    return x2, x2b, x2bT, saved, W


def _layer_bwd(l, dx2, sv, W, P, tabs, blk, send_early=None):
    S = dx2.shape[0]
    n = lambda s: f"l{l}_{s}"
    bs = min(512, S)
    h_att = sv["h_att"]
    du2, du2b, dg2, db2 = _ln_bwd(dx2, sv["xh2"], sv["rs2"], P["ln2_g"], bs=bs, name=n("ln2_bwd"))
    da = _mm(du2b, W["w_down"], name=n("down_dx"), tb=True, bm=1024, bn=_blk(D_FF, 1408), bk=1024)
    dwd = _mm(sv["aT"], du2b, name=n("down_dw"), bm=_blk(D_FF, 1408), bn=1024, bk=1024)
    dgu = _swiglu_bwd(da, sv["gu"], bs=min(256, S), name=n("swiglu_bwd"))
    dx1 = _mm(dgu, W["wgu"], name=n("gate_up_dx"), tb=True, bm=1024, bn=1024, bk=_blk(2 * D_FF, 1408), resid=du2,
              alpha=ALPHA)
    dwgu = _mm(sv["x1bT"], dgu, name=n("gate_up_dw"), bm=1024, bn=_blk(2 * D_FF, 1408), bk=1024)
    du1, du1b, dg1, db1 = _ln_bwd(dx1, sv["xh1"], sv["rs1"], P["ln1_g"], bs=bs, name=n("ln1_bwd"))
    dgn = _mm(du1b, W["w_o"], name=n("out_proj_dx"), tb=True, bm=1024, bn=1024, bk=1024)
    dwo = _mm(sv["gnT"], du1b, name=n("out_proj_dw"), bm=1024, bn=1024, bk=1024)
    mix_g = P["mix_g"]
    if send_early is not None:
        mix_g = mix_g + send_early(dict(w_o=dwo, w_gate=dwgu[:, :D_FF], w_up=dwgu[:, D_FF:], w_down=dwd))[0, 0]
    dmix, dmixT, dmixg = _gnorm_bwd(dgn, sv["outs"], mix_g, bs=bs, name=n("gnorm_bwd"))
    q_f, k_f = sv["q_f"], sv["k_f"]
    dqT_a, dk_a, dva = _smax_bwd_t(q_f.T, q_f, k_f, _kv_blocks_t(k_f, blk), sv["v_f"], dmix, dmixT, sv["oT_a"],
                                   sv["lse_a"], dk=MLA_PAD, dcb=0, qscale=HEAD_DIM ** -0.5, blk=blk, name=n("fox_bwd"))
    dq_a, dk_a = dqT_a.T.reshape(S, 4, MLA_PAD), dk_a.reshape(S, 4, MLA_PAD)
    dqa, dka = dq_a[:, :, :HEAD_DIM].reshape(S, 256), dk_a[:, :, :HEAD_DIM].reshape(S, 256)
    dcq = dq_a[:, :, HEAD_DIM].T.reshape(4, S // 128, 128)
    dck = dk_a[:, :, HEAD_DIM + 3].T.reshape(4, S // 128, 128)
    q_m, k_m = sv["q_m"], sv["k_m"]
    dqT_b, dkb, dvb = _smax_bwd_t(sv["qT_m"], q_m, k_m, sv["kT3_m"], sv["v_m"], dmix, dmixT, sv["oT_b"],
                                  sv["lse_b"], dk=MLA_PAD, dcb=2, qscale=MLA_QK ** -0.5, blk=blk, name=n("mla_bwd"))
    bsb = min(BLK_STICK, S)
    dqT_c, dkc, dvc = _sb_bwd_t(sv["qT_c"], h_att, _kv_blocks_t(h_att[:, COL_SK:COL_SK + 256], bsb), dmix, dmixT,
                                sv["lt_c"], dcb=4, qscale=HEAD_DIM ** -0.5, blk=bsb, name=n("sb_bwd"))
    dqc = dqT_c.T
    dqd, dkd, dvd, dsink = _swa_bwd(h_att, P["swa_sinks"], dmix, sv["outs"][3], sv["lse_d"], dcb=3, name=n("swa_bwd"))
    dlat, dwuq, dwuk, dwuv, dgq, dgkv = _mla_prep_bwd(
        sv["lat"], P["mla_g_q"], P["mla_g_kv"], W["wuq"], W["wuk"], W["wuv"], *tabs, dqT_b, dkb, dvb,
        bs=bs, name=n("mla_prep_bwd"))
    dfg, dbf = _fox_gate_bwd(sv["fg"], P["fox_b_f"], dck, dcq, q_unscale=HEAD_DIM ** 0.5, name=n("fox_gate_bwd"))
    dfg_blk = jnp.pad(dfg.reshape(4, S).T, ((0, 0), (0, 124)))
    dh = jnp.concatenate([t.astype(MXU_DTYPE) for t in (dqa, dka, dva, dqc, dkc, dvc, dqd, dkd, dvd, dlat, dfg_blk)], axis=1)
    dx = _mm(dh, W["win_p"], name=n("in_dx"), tb=True, bm=1024, bn=1024, bk=_blk(PERM_W, 1024), resid=du1, alpha=ALPHA)
    dwin_p = _mm(sv["xbT"], dh, name=n("in_dw"), bm=1024, bn=_blk(PERM_W, 1024), bk=1024)
    grads = _unprep_grads(dwin_p, dwuq, dwuk, dwuv, dwo, dwgu, dwd)
    grads.update(fox_b_f=dbf[:, 0], mla_g_q=dgq[0], mla_g_kv=dgkv[0], swa_sinks=dsink[:, 0], mix_g=dmixg[0],
                 ln1_g=dg1[0], ln1_b=db1[0], ln2_g=dg2[0], ln2_b=db2[0])
    return dx, grads


BIG = ("w_in", "mla_w_uq", "mla_w_ukv", "w_o", "w_gate", "w_up", "w_down")
SMALL = ("fox_b_f", "mla_g_q", "mla_g_kv", "swa_sinks", "mix_g", "ln1_g", "ln1_b", "ln2_g", "ln2_b")
SHARD_AXIS = dict(w_in=2, mla_w_uq=2, mla_w_ukv=2, w_o=1, w_gate=2, w_up=2, w_down=1)
N_CHIPS = 4
ANY = pl.BlockSpec(memory_space=pl.ANY)


HBM = pl.BlockSpec(memory_space=pltpu.HBM)
SEM = pl.BlockSpec(memory_space=pltpu.SEMAPHORE)
N_PEER_CHIPS = N_CHIPS - 1


def _peer_copies(src_ref, land_ref, sems, scatter):
    x, y, c = lax.axis_index("x"), lax.axis_index("y"), lax.axis_index("c")
    me = 2 * x + y
    out = []
    for r, (px, py) in enumerate([(1 - x, y), (x, 1 - y), (1 - x, 1 - y)]):
        theirs = 2 * px + py
        send = pltpu.make_async_remote_copy(
            src_ref=src_ref.at[theirs] if scatter else src_ref, dst_ref=land_ref.at[me],
            send_sem=sems[2 * r], recv_sem=sems[2 * r + 1], device_id=(px, py, c), device_id_type=MESH)
        arrive = pltpu.make_async_remote_copy(
            src_ref=src_ref.at[me] if scatter else src_ref, dst_ref=land_ref.at[theirs],
            send_sem=sems[2 * r], recv_sem=sems[2 * r + 1], device_id=(px, py, c), device_id_type=MESH)
        out.append((send, arrive))
    return out


def _exchange_start(srcs, *, scatter, name):
    nt = len(srcs)
    ns = 2 * N_PEER_CHIPS * nt
    land_shapes = [s.shape if scatter else (N_CHIPS,) + s.shape for s in srcs]

    def body(*refs):
        src_refs, land_refs, outs = refs[:nt], refs[nt:2 * nt], refs[2 * nt:]
        for t in range(nt):
            for send, _ in _peer_copies(src_refs[t], land_refs[t], outs[6 * t:6 * t + 6], scatter):
                send.start()
        outs[-1][...] = jnp.zeros_like(outs[-1])

    res = pl.pallas_call(
        body, name=name,
        out_shape=(*[pltpu.SemaphoreType.DMA(())] * ns, *[pltpu.HBM(s.shape, s.dtype) for s in srcs],
                   *[pltpu.HBM(ls, s.dtype) for ls, s in zip(land_shapes, srcs)], jax.ShapeDtypeStruct((8, 128), F32)),
        in_specs=(HBM,) * (2 * nt), out_specs=(*[SEM] * ns, *[HBM] * (2 * nt), pl.BlockSpec(memory_space=pltpu.VMEM)),
        input_output_aliases={i: ns + i for i in range(2 * nt)},
        compiler_params=pltpu.CompilerParams(has_side_effects=pltpu.SideEffectType.DATAFLOW_SIDE_EFFECTING),
    )(*[pltpu.with_memory_space_constraint(s, pltpu.HBM) for s in srcs],
      *[pltpu.with_memory_space_constraint(lax.empty(ls, s.dtype), pltpu.HBM) for ls, s in zip(land_shapes, srcs)])
    return dict(sems=res[:ns], srcs=res[ns:ns + nt], lands=res[ns + nt:ns + 2 * nt], token=res[-1])


def _exchange_wait(started, after, *, scatter, name):
    nt = len(started["srcs"])
    ns = 2 * N_PEER_CHIPS * nt

    def body(*refs):
        src_refs, land_refs, sems = refs[:nt], refs[nt:2 * nt], refs[2 * nt:2 * nt + ns]
        for t in range(nt):
            for send, arrive in _peer_copies(src_refs[t], land_refs[t], sems[6 * t:6 * t + 6], scatter):
                send.wait_send()
                arrive.wait_recv()

    both = list(started["srcs"]) + list(started["lands"])
    res = pl.pallas_call(
        body, name=name, out_shape=tuple(pltpu.HBM(a.shape, a.dtype) for a in both),
        in_specs=(*[HBM] * (2 * nt), *[SEM] * ns, ANY), out_specs=(HBM,) * (2 * nt),
        input_output_aliases={i: i for i in range(2 * nt)},
        compiler_params=pltpu.CompilerParams(has_side_effects=pltpu.SideEffectType.DATAFLOW_SIDE_EFFECTING),
    )(*both, *started["sems"], after)
    return res[:nt], res[nt:]


def _core_exchange(tensors, *, name):
    nt = len(tensors)

    def body(*refs):
        ins, outs = refs[:nt], refs[nt:2 * nt]
        send_sems, recv_sems = refs[2 * nt:]
        sibling = (lax.axis_index("x"), lax.axis_index("y"), 1 - lax.axis_index("c"))
        copies = [pltpu.make_async_remote_copy(src_ref=ins[t], dst_ref=outs[t], send_sem=send_sems.at[t],
                                               recv_sem=recv_sems.at[t], device_id=sibling, device_id_type=MESH)
                  for t in range(nt)]
        for cp in copies:
            cp.start()
        for cp in copies:
            cp.wait_recv()
        for cp in copies:
            cp.wait_send()

    return pl.pallas_call(
        body, name=name, in_specs=[ANY] * nt, out_specs=[ANY] * nt,
        out_shape=[jax.ShapeDtypeStruct(t.shape, t.dtype) for t in tensors],
        scratch_shapes=[pltpu.SemaphoreType.DMA((nt,)), pltpu.SemaphoreType.DMA((nt,))],
        compiler_params=pltpu.CompilerParams(has_side_effects=True),
    )(*tensors)


def _all_sum_small(block, *, name):
    R = block.shape[0]
    n_dev = 8

    def body(x_ref, o_ref, slots, send_sems, recv_sems):
        x, y, c = lax.axis_index("x"), lax.axis_index("y"), lax.axis_index("c")
        me = 4 * x + 2 * y + c
        slots[me] = x_ref[...]
        sends, recvs = [], []
        for d in range(1, n_dev):
            px, py, pc = x ^ (d >> 2), y ^ ((d >> 1) & 1), c ^ (d & 1)
            theirs = 4 * px + 2 * py + pc
            sends.append(pltpu.make_async_remote_copy(
                src_ref=x_ref, dst_ref=slots.at[me], send_sem=send_sems.at[d - 1], recv_sem=recv_sems.at[d - 1],
                device_id=(px, py, pc), device_id_type=MESH))
            recvs.append(pltpu.make_async_remote_copy(
                src_ref=x_ref, dst_ref=slots.at[theirs], send_sem=send_sems.at[d - 1], recv_sem=recv_sems.at[d - 1],
                device_id=(px, py, pc), device_id_type=MESH))
        for cp in sends:
            cp.start()
        for cp in recvs:
            cp.wait_recv()
        for cp in sends:
            cp.wait_send()
        total = slots[0]
        for k in range(1, n_dev):
            total = total + slots[k]
        o_ref[...] = total

    return pl.pallas_call(
        body, name=name, in_specs=[pl.BlockSpec(memory_space=pltpu.VMEM)],
        out_specs=pl.BlockSpec(memory_space=pltpu.VMEM), out_shape=jax.ShapeDtypeStruct((R, 128), F32),
        scratch_shapes=[pltpu.VMEM((n_dev, R, 128), F32), pltpu.SemaphoreType.DMA((n_dev - 1,)),
                        pltpu.SemaphoreType.DMA((n_dev - 1,))],
        compiler_params=pltpu.CompilerParams(has_side_effects=True),
    )(block)


def _sum_chips_into(acc, land, own, me, layer, *, br, name):
    _, R, C = land.shape

    def body(me_ref, land_ref, own_ref, acc_ref, o_ref):
        mine = me_ref[0]
        total = None
        for k in range(N_CHIPS):
            part = jnp.where(mine == k, own_ref[...], land_ref[k]).astype(F32)
            total = part if total is None else total + part
        o_ref[0] = total

    return pl.pallas_call(
        body, name=name, grid=(R // br,),
        in_specs=[pl.BlockSpec(memory_space=pltpu.SMEM), pl.BlockSpec((N_CHIPS, br, C), lambda i: (0, i, 0)),
                  pl.BlockSpec((br, C), lambda i: (i, 0)), ANY],
        out_specs=pl.BlockSpec((1, br, C), lambda i: (layer, i, 0)),
        out_shape=jax.ShapeDtypeStruct(acc.shape, F32), input_output_aliases={3: 0},
        compiler_params=_cparams("parallel"),
    )(me, land, own, acc)


def _adamw_math(w, g, m, v):
    m = ADAM_B1 * m + (1.0 - ADAM_B1) * g
    v = ADAM_B2 * v + (1.0 - ADAM_B2) * (g * g)
    m_hat = m / (1.0 - ADAM_B1 ** ADAM_STEP)
    v_hat = v / (1.0 - ADAM_B2 ** ADAM_STEP)
    return -ADAM_LR * (m_hat / (jnp.sqrt(v_hat) + ADAM_EPS) + ADAM_WD * w), m, v


def _adamw(w, m, v, g_a, g_b, *, br, name):
    R, C = w.shape
    two = g_b is not None

    def body(*refs):
        if two:
            w_ref, m_ref, v_ref, ga_ref, gb_ref, g_ref, d_ref, nm_ref, nv_ref = refs
            g = ga_ref[...] + gb_ref[...]
        else:
            w_ref, m_ref, v_ref, ga_ref, g_ref, d_ref, nm_ref, nv_ref = refs
            g = ga_ref[...]
        g_ref[...] = g
        d_ref[...], nm_ref[...], nv_ref[...] = _adamw_math(w_ref[...], g, m_ref[...], v_ref[...])

    spec = pl.BlockSpec((br, C), lambda i: (i, 0))
    args = [w, m, v, g_a] + ([g_b] if two else [])
    return pl.pallas_call(
        body, name=name, grid=(R // br,), in_specs=[spec] * len(args), out_specs=[spec] * 4,
        out_shape=[jax.ShapeDtypeStruct((R, C), F32)] * 4,
        compiler_params=_cparams("parallel"),
    )(*args)


SMALL_ROWS = dict(fox_b_f=1, mla_g_q=2, mla_g_kv=1, swa_sinks=1, mix_g=8, ln1_g=8, ln1_b=8, ln2_g=8, ln2_b=8)
SMALL_ROWS_PER_LAYER = sum(SMALL_ROWS.values())


def _pack_small(vals, extra_rows):
    L = vals[SMALL[0]].shape[0]
    per_layer = []
    for name in SMALL:
        a = vals[name].astype(F32)
        a = jnp.pad(a, ((0, 0), (0, SMALL_ROWS[name] * 128 - a.shape[1])))
        per_layer.append(a.reshape(L, SMALL_ROWS[name], 128))
    out = jnp.concatenate(per_layer, axis=1).reshape(L * SMALL_ROWS_PER_LAYER, 128)
    return jnp.pad(out, ((0, extra_rows), (0, 0)))


def _unpack_small(block, shapes):
    L = shapes[SMALL[0]][0]
    body = block[:L * SMALL_ROWS_PER_LAYER].reshape(L, SMALL_ROWS_PER_LAYER, 128)
    out, r = {}, 0
    for name in SMALL:
        n = shapes[name][1]
        out[name] = body[:, r:r + SMALL_ROWS[name]].reshape(L, SMALL_ROWS[name] * 128)[:, :n]
        r += SMALL_ROWS[name]
    return out


def _to_chips(g, axis):
    L, a, b = g.shape
    if axis == 2:
        return g.reshape(L, a, N_CHIPS, b // N_CHIPS).transpose(2, 0, 1, 3)
    return g.reshape(L, N_CHIPS, a // N_CHIPS, b).transpose(1, 0, 2, 3)


def kernel(x, w_in, fox_b_f, mla_g_q, mla_g_kv, mla_w_uq, mla_w_ukv, swa_sinks, mix_g, w_o, ln1_g, ln1_b, w_gate, w_up, w_down, ln2_g, ln2_b, loss_target, m_w_in, m_fox_b_f, m_mla_g_q, m_mla_g_kv, m_mla_w_uq, m_mla_w_ukv, m_swa_sinks, m_mix_g, m_w_o, m_ln1_g, m_ln1_b, m_w_gate, m_w_up, m_w_down, m_ln2_g, m_ln2_b, v_w_in, v_fox_b_f, v_mla_g_q, v_mla_g_kv, v_mla_w_uq, v_mla_w_ukv, v_swa_sinks, v_mix_g, v_w_o, v_ln1_g, v_ln1_b, v_w_gate, v_w_up, v_w_down, v_ln2_g, v_ln2_b):
    w = dict(w_in=w_in, fox_b_f=fox_b_f, mla_g_q=mla_g_q, mla_g_kv=mla_g_kv, mla_w_uq=mla_w_uq, mla_w_ukv=mla_w_ukv,
             swa_sinks=swa_sinks, mix_g=mix_g, w_o=w_o, ln1_g=ln1_g, ln1_b=ln1_b, w_gate=w_gate, w_up=w_up,
             w_down=w_down, ln2_g=ln2_g, ln2_b=ln2_b)
    m = dict(w_in=m_w_in, fox_b_f=m_fox_b_f, mla_g_q=m_mla_g_q, mla_g_kv=m_mla_g_kv, mla_w_uq=m_mla_w_uq,
             mla_w_ukv=m_mla_w_ukv, swa_sinks=m_swa_sinks, mix_g=m_mix_g, w_o=m_w_o, ln1_g=m_ln1_g, ln1_b=m_ln1_b,
             w_gate=m_w_gate, w_up=m_w_up, w_down=m_w_down, ln2_g=m_ln2_g, ln2_b=m_ln2_b)
    v = dict(w_in=v_w_in, fox_b_f=v_fox_b_f, mla_g_q=v_mla_g_q, mla_g_kv=v_mla_g_kv, mla_w_uq=v_mla_w_uq,
             mla_w_ukv=v_mla_w_ukv, swa_sinks=v_swa_sinks, mix_g=v_mix_g, w_o=v_w_o, ln1_g=v_ln1_g, ln1_b=v_ln1_b,
             w_gate=v_w_gate, w_up=v_w_up, w_down=v_w_down, ln2_g=v_ln2_g, ln2_b=v_ln2_b)
    names = tuple(w)
    L = w_in.shape[0]
    S = x.shape[1]
    blk = min(BLK_SOFTMAX, S)
    bs = min(512, S)

    me = 2 * lax.axis_index("x") + lax.axis_index("y")
    axis_of = {k: SHARD_AXIS[k] - 1 for k in BIG}
    groups = (("w_in", "mla_w_uq", "mla_w_ukv"), ("w_o", "w_gate", "w_up", "w_down"))

    started, last = [], None
    for l in range(L):
        per_group = []
        for g, group in enumerate(groups):
            srcs = [w[k][l].astype(MXU_DTYPE) for k in group]
            if last is not None:
                t = min(range(len(srcs)), key=lambda i: srcs[i].size)
                srcs[t] = srcs[t] + last["token"][0, 0].astype(MXU_DTYPE)
            last = _exchange_start(srcs, scatter=False, name=f"gather_start{l}_{g}")
            per_group.append(last)
        started.append(per_group)
    all_started = sum(st["token"] for per_group in started for st in per_group)

    def gathered(l, g, after):
        mine, lands = _exchange_wait(started[l][g], after, scatter=False, name=f"gather_wait{l}_{g}")
        shard = lambda t, k: jnp.where(me == k, mine[t], lands[t][k])
        whole = lambda t, axis: jnp.concatenate([shard(t, k) for k in range(N_CHIPS)], axis=axis)
        if g == 0:
            return _prep_weights_a(*[whole(t, axis_of[name]) for t, name in enumerate(groups[0])])
        gate_up = jnp.concatenate([shard(t, k) for t in (1, 2) for k in range(N_CHIPS)], axis=1)
        return dict(w_o=whole(0, 0), wgu=gate_up, w_down=whole(3, 0))

    def scatter(l, g, grads):
        to_owner = [_to_chips(grads[k].astype(MXU_DTYPE)[None], axis_of[k] + 1)[:, 0] for k in groups[g]]
        return _exchange_start(to_owner, scatter=True, name=f"scatter_start{l}_{g}")

    tabs = _rope_tables(S)
    Ps = []
    for l in range(L):
        P = dict(fox_b_f=fox_b_f[l], swa_sinks=swa_sinks[l])
        for k in ("mla_g_q", "mla_g_kv", "mix_g", "ln1_g", "ln1_b", "ln2_g", "ln2_b"):
            P[k] = w[k][l][None, :]
        Ps.append(P)

    xa = x[0]
    xb = xa.astype(MXU_DTYPE)
    xbT = xb.T
    saved, Ws = [], []
    for l in range(L):
        W = gathered(l, 0, all_started if l == 0 else xa)
        late = lambda after, l=l: gathered(l, 1, after)
        xa, xb, xbT, sv, W = _layer_fwd(l, xa, xb, xbT, W, Ps[l], tabs, blk, late_weights=late)
        saved.append(sv)
        Ws.append(W)
    dx, loss_part = _loss_head(xa, loss_target[0], bs=bs, name="loss_head")

    layer_grads = [None] * L
    sent = [[None, None] for _ in range(L)]
    pin = None
    for l in reversed(range(L)):
        P = Ps[l] if pin is None else dict(Ps[l], ln2_g=Ps[l]["ln2_g"] + pin[0, 0])

        def send_early(grads, l=l):
            sent[l][1] = scatter(l, 1, grads)
            return sent[l][1]["token"]

        dx, layer_grads[l] = _layer_bwd(l, dx, saved[l], Ws[l], P, tabs, blk, send_early=send_early)
        sent[l][0] = scatter(l, 0, layer_grads[l])
        pin = sent[l][0]["token"]
    grad_x = dx[None]

    me_arr = me.astype(jnp.int32)[None]
    partial = {k: jnp.zeros(w[k].shape, F32) for k in BIG}
    after = dx
    for l in reversed(range(L)):
        for g in (1, 0):
            mine, lands = _exchange_wait(sent[l][g], after, scatter=True, name=f"scatter_wait{l}_{g}")
            for t, k in enumerate(groups[g]):
                own = lax.dynamic_index_in_dim(mine[t], me, 0, keepdims=False)
                partial[k] = _sum_chips_into(partial[k], lands[t], own, me_arr, l, br=_rows(own.shape[0]),
                                             name=f"sum_{k}_l{l}")
            after = partial[groups[g][-1]]
    partial = [partial[k] for k in BIG]
    sibling = _core_exchange(partial, name="swap_partials")
    local = {k: jnp.stack([layer_grads[l][k] for l in range(L)]) for k in SMALL}
    out = {}
    for k, mine, theirs in zip(BIG, partial, sibling):
        shp = w[k].shape
        two_d = lambda a: a.reshape(shp[0] * shp[1], shp[2])
        res = _adamw(two_d(w[k]), two_d(m[k]), two_d(v[k]), two_d(mine), two_d(theirs), br=_rows(shp[0] * shp[1]),
                     name=f"adamw_{k}")
        out[k] = [a.reshape(shp) for a in res]

    shapes = {k: w[k].shape for k in SMALL}
    extra = 8 + (-L * SMALL_ROWS_PER_LAYER) % 8
    block = _pack_small({k: local[k] for k in SMALL}, extra)
    block = block.at[L * SMALL_ROWS_PER_LAYER, 0].set(loss_part[0, 0])
    total = _all_sum_small(block, name="sum_small")
    loss = total[L * SMALL_ROWS_PER_LAYER, 0]
    res = _adamw(_pack_small({k: w[k] for k in SMALL}, extra), _pack_small({k: m[k] for k in SMALL}, extra),
                 _pack_small({k: v[k] for k in SMALL}, extra), total, None, br=total.shape[0], name="adamw_small")
    res = [_unpack_small(t, shapes) for t in res]
    for k in SMALL:
        out[k] = [r[k] for r in res]

    return (loss, grad_x, *[out[k][0] for k in names], *[out[k][1] for k in names],
            *[out[k][2] for k in names], *[out[k][3] for k in names])


def _rows(n):
    for b in (256, 128, 64, 32, 16, 8):
        if n % b == 0:
            return b
    return n
```

```python
import functools

import numpy as np
import jax
import jax.numpy as jnp
from jax import lax
from jax.experimental import pallas as pl
from jax.experimental.pallas import tpu as pltpu

F32 = jnp.float32
MXU_DTYPE = jnp.bfloat16
NEG_INF = -1e30

D_MODEL = 1024
DEPTH = 4
HEAD_DIM = 64
GROUP_WIDTH = 256
D_FF = 2816
MLA_Q_RANK = 256
MLA_KV_RANK = 128
MLA_ROPE = 32
MLA_QK = 96
MLA_PAD = 128
ROPE_THETA = 10000.0
WINDOW = 128
ALPHA = (2.0 * DEPTH) ** 0.25
SWA_SLOPES = tuple(float(2.0 ** (-8.0 * h / 4)) for h in range(1, 5))
ATT_W = 2048
LAT_W = 640
PERM_W = ATT_W + LAT_W
COL_FQ, COL_FK, COL_FV = 0, 256, 512
COL_SQ, COL_SK, COL_SV = 768, 1024, 1280
COL_WQ, COL_WK, COL_WV = 1536, 1792, 1920
Q_COLSCALE = np.ones((1, ATT_W), np.float32)
Q_COLSCALE[:, COL_FQ:COL_FQ + 256] = HEAD_DIM ** -0.5
Q_COLSCALE[:, COL_SQ:COL_SQ + 256] = HEAD_DIM ** -0.5

ADAM_LR, ADAM_B1, ADAM_B2, ADAM_EPS, ADAM_WD, ADAM_STEP = 0.001, 0.9, 0.999, 1e-08, 0.01, 10

VMEM_LIMIT = 56 * 1024 * 1024
NT = (((1,), (1,)), ((), ()))
TN = (((0,), (0,)), ((), ()))
MESH = pl.DeviceIdType.MESH


def _cparams(*sem):
    return pltpu.CompilerParams(dimension_semantics=sem, vmem_limit_bytes=VMEM_LIMIT)


def _dot01(x, m01, dn=None, parts=2):
    acc = None
    rem = x
    for _ in range(parts):
        part = rem.astype(MXU_DTYPE)
        rem = rem - part.astype(F32)
        if dn is None:
            t = jnp.dot(part, m01, preferred_element_type=F32)
        else:
            t = lax.dot_general(part, m01, dn, preferred_element_type=F32)
        acc = t if acc is None else acc + t
    return acc


def _mm(a, b, *, name, ta=False, tb=False, out_dtype=F32, bm=512, bn=512, bk=512, resid=None, alpha=1.0,
        colscale=None):
    M, K = (a.shape[1], a.shape[0]) if ta else a.shape
    N = b.shape[0] if tb else b.shape[1]
    assert (b.shape[1] if tb else b.shape[0]) == K
    assert resid is None or colscale is None
    bm, bn, bk = min(bm, M), min(bn, N), min(bk, K)
    assert M % bm == 0 and N % bn == 0 and K % bk == 0, (name, M, N, K, bm, bn, bk)
    nk = K // bk
    assert nk == 1 or (out_dtype == F32 and colscale is None), name
    dn = (((0 if ta else 1,), (1 if tb else 0,)), ((), ()))

    extra = resid is not None or colscale is not None

    def body(*refs):
        a_ref, b_ref = refs[:2]
        r_ref = refs[2] if extra else None
        o_ref = refs[3] if extra else refs[2]
        k = pl.program_id(2)

        def first():
            r = lax.dot_general(a_ref[...].astype(MXU_DTYPE), b_ref[...].astype(MXU_DTYPE), dn,
                                preferred_element_type=F32)
            if resid is not None:
                r = r + alpha * r_ref[...]
            if colscale is not None:
                r = r * r_ref[...]
            o_ref[...] = r.astype(o_ref.dtype)

        if nk == 1:
            first()
        else:
            pl.when(k == 0)(first)

            @pl.when(k > 0)
            def _():
                o_ref[...] += lax.dot_general(a_ref[...].astype(MXU_DTYPE), b_ref[...].astype(MXU_DTYPE), dn,
                                              preferred_element_type=F32)

    a_spec = pl.BlockSpec((bk, bm), lambda i, j, k: (k, i)) if ta else pl.BlockSpec((bm, bk), lambda i, j, k: (i, k))
    b_spec = pl.BlockSpec((bn, bk), lambda i, j, k: (j, k)) if tb else pl.BlockSpec((bk, bn), lambda i, j, k: (k, j))
    in_specs = [a_spec, b_spec]
    args = [a, b]
    if resid is not None:
        in_specs.append(pl.BlockSpec((bm, bn), lambda i, j, k: (i, j)))
        args.append(resid)
    if colscale is not None:
        in_specs.append(pl.BlockSpec((1, bn), lambda i, j, k: (0, j)))
        args.append(colscale)
    return pl.pallas_call(
        body, name=name, grid=(M // bm, N // bn, nk), in_specs=in_specs,
        out_specs=pl.BlockSpec((bm, bn), lambda i, j, k: (i, j)),
        out_shape=jax.ShapeDtypeStruct((M, N), out_dtype),
        compiler_params=_cparams("parallel", "parallel", "arbitrary"),
    )(*args)


HP = 4
BLK_SOFTMAX = 512
BLK_STICK = 256


def _kv_blocks_t(a, blk):
    S, C = a.shape
    return a.reshape(S // blk, blk, C).transpose(0, 2, 1)


def _smax_fwd_t(qT, k, vT3, *, dk, blk, name):
    S = k.shape[0]
    nb = S // blk
    H = k.shape[1] // dk

    def body(qT_ref, k_ref, vT_ref, oT_ref, lse_ref):
        i = pl.program_id(1)
        key = lax.broadcasted_iota(jnp.int32, (blk, blk), 0)
        qry = lax.broadcasted_iota(jnp.int32, (blk, blk), 1)
        qs = [qT_ref[h * dk:(h + 1) * dk, :] for h in range(HP)]

        def tile(j, carry, masked):
            r0 = pl.multiple_of(j * blk, blk)
            ss = [jnp.dot(k_ref[pl.ds(r0, blk), h * dk:(h + 1) * dk], qs[h], preferred_element_type=F32)
                  for h in range(HP)]
            stats, pes = [], []
            for h in range(HP):
                m, l, _ = carry[h]
                s = jnp.where(key <= qry, ss[h], NEG_INF) if masked else ss[h]
                mn = jnp.maximum(m, jnp.max(s, axis=0, keepdims=True))
                a = jnp.exp(m - mn)
                pe = jnp.exp(s - mn)
                stats.append((mn, a * l + jnp.sum(pe, axis=0, keepdims=True), a))
                pes.append(pe.astype(MXU_DTYPE))
            pvs = [jnp.dot(vT_ref[j, h * HEAD_DIM:(h + 1) * HEAD_DIM, :], pes[h], preferred_element_type=F32)
                   for h in range(HP)]
            return tuple((stats[h][0], stats[h][1], stats[h][2] * carry[h][2] + pvs[h]) for h in range(HP))

        init = tuple((jnp.full((1, blk), NEG_INF, F32), jnp.zeros((1, blk), F32), jnp.zeros((HEAD_DIM, blk), F32))
                     for _ in range(HP))
        carry = lax.fori_loop(0, i, functools.partial(tile, masked=False), init)
        carry = tile(i, carry, True)
        for h in range(HP):
            m, l, acc = carry[h]
            oT_ref[h * HEAD_DIM:(h + 1) * HEAD_DIM, :] = acc / l
            lse_ref[h, 0] = m + jnp.log(l)

    return pl.pallas_call(
        body, name=name, grid=(H // HP, nb),
        in_specs=[pl.BlockSpec((HP * dk, blk), lambda p, i: (p, i)),
                  pl.BlockSpec((S, HP * dk), lambda p, i: (0, p)),
                  pl.BlockSpec((nb, HP * HEAD_DIM, blk), lambda p, i: (0, p, 0))],
        out_specs=[pl.BlockSpec((HP * HEAD_DIM, blk), lambda p, i: (p, i)),
                   pl.BlockSpec((HP, 1, 1, blk), lambda p, i: (p, i, 0, 0))],
        out_shape=[jax.ShapeDtypeStruct((H * HEAD_DIM, S), F32), jax.ShapeDtypeStruct((H, nb, 1, blk), F32)],
        compiler_params=_cparams("arbitrary", "arbitrary"),
    )(qT, k, vT3)


def _smax_bwd_t(qT, q, k, kT3, v, dmix, dmixT, oT, lse, *, dk, dcb, qscale, blk, name):
    S = k.shape[0]
    nb = S // blk
    H = k.shape[1] // dk
    hd = HP * HEAD_DIM
    dcr = dcb * 128 // hd

    def body(qT_ref, q_ref, k_ref, kT_ref, v_ref, do_ref, doT_ref, oT_ref, lse_ref, dqT_ref, dk_ref, dv_ref):
        i = pl.program_id(1)

        @pl.when(i == 0)
        def _():
            dk_ref[...] = jnp.zeros_like(dk_ref)
            dv_ref[...] = jnp.zeros_like(dv_ref)

        key = lax.broadcasted_iota(jnp.int32, (blk, blk), 0)
        qry = lax.broadcasted_iota(jnp.int32, (blk, blk), 1)
        per_head = []
        for h in range(HP):
            hs = slice(h * HEAD_DIM, (h + 1) * HEAD_DIM)
            doT = doT_ref[hs, :]
            per_head.append(dict(
                qT=qT_ref[h * dk:(h + 1) * dk, :], q=q_ref[:, h * dk:(h + 1) * dk],
                doT=doT.astype(MXU_DTYPE), do=do_ref[:, hs].astype(MXU_DTYPE),
                delta=jnp.sum(doT * oT_ref[hs, :], axis=0, keepdims=True), lse=lse_ref[h, 0]))

        def tile(j, dqs, masked):
            r0 = pl.multiple_of(j * blk, blk)
            rows = pl.ds(r0, blk)
            ksl = [slice(h * dk, (h + 1) * dk) for h in range(HP)]
            hsl = [slice(h * HEAD_DIM, (h + 1) * HEAD_DIM) for h in range(HP)]
            ss = [jnp.dot(k_ref[rows, ksl[h]], per_head[h]["qT"], preferred_element_type=F32) for h in range(HP)]
            dps = [jnp.dot(v_ref[rows, hsl[h]], per_head[h]["doT"], preferred_element_type=F32) for h in range(HP)]
            prs, dss = [], []
            for h in range(HP):
                c = per_head[h]
                s = jnp.where(key <= qry, ss[h], NEG_INF) if masked else ss[h]
                pr = jnp.exp(s - c["lse"])
                dss.append((pr * (dps[h] - c["delta"])).astype(MXU_DTYPE))
                prs.append(pr.astype(MXU_DTYPE))
            for h in range(HP):
                dv_ref[rows, hsl[h]] += jnp.dot(prs[h], per_head[h]["do"], preferred_element_type=F32)
            for h in range(HP):
                dk_ref[rows, ksl[h]] += jnp.dot(dss[h], per_head[h]["q"], preferred_element_type=F32)
            return tuple(dqs[h] + jnp.dot(kT_ref[j, ksl[h], :], dss[h], preferred_element_type=F32) for h in range(HP))

        dqs = lax.fori_loop(0, i, functools.partial(tile, masked=False),
                            tuple(jnp.zeros((dk, blk), F32) for _ in range(HP)))
        dqs = tile(i, dqs, True)
        for h in range(HP):
            dqT_ref[h * dk:(h + 1) * dk, :] = dqs[h] * qscale

    return pl.pallas_call(
        body, name=name, grid=(H // HP, nb),
        in_specs=[pl.BlockSpec((HP * dk, blk), lambda p, i: (p, i)),
                  pl.BlockSpec((blk, HP * dk), lambda p, i: (i, p)),
                  pl.BlockSpec((S, HP * dk), lambda p, i: (0, p)),
                  pl.BlockSpec((nb, HP * dk, blk), lambda p, i: (0, p, 0)),
                  pl.BlockSpec((S, hd), lambda p, i: (0, p)),
                  pl.BlockSpec((blk, hd), lambda p, i: (i, dcr + p)),
                  pl.BlockSpec((hd, blk), lambda p, i: (dcr + p, i)),
                  pl.BlockSpec((hd, blk), lambda p, i: (p, i)),
                  pl.BlockSpec((HP, 1, 1, blk), lambda p, i: (p, i, 0, 0))],
        out_specs=[pl.BlockSpec((HP * dk, blk), lambda p, i: (p, i)),
                   pl.BlockSpec((S, HP * dk), lambda p, i: (0, p)),
                   pl.BlockSpec((S, hd), lambda p, i: (0, p))],
        out_shape=[jax.ShapeDtypeStruct((H * dk, S), F32), jax.ShapeDtypeStruct((S, H * dk), F32),
                   jax.ShapeDtypeStruct((S, H * HEAD_DIM), F32)],
        compiler_params=_cparams("arbitrary", "arbitrary"),
    )(qT, q, k, kT3, v, dmix, dmixT, oT, lse)


def _log1m_beta(z):
    return -(jnp.maximum(z, 0.0) + jnp.log(1.0 + jnp.exp(-jnp.abs(z))))


def _dot01_left(m01, x, parts=2):
    acc = None
    rem = x
    for _ in range(parts):
        part = rem.astype(MXU_DTYPE)
        rem = rem - part.astype(F32)
        t = jnp.dot(m01, part, preferred_element_type=F32)
        acc = t if acc is None else acc + t
    return acc


def _sb_fwd_t(qT, h_att, vT3, *, blk, name):
    S = h_att.shape[0]
    nb = S // blk
    kcb = COL_SK // (HP * HEAD_DIM)

    def body(qT_ref, k_ref, vT_ref, oT_ref, lt_ref):
        i = pl.program_id(1)
        key = lax.broadcasted_iota(jnp.int32, (blk, blk), 0)
        qry = lax.broadcasted_iota(jnp.int32, (blk, blk), 1)
        strict = key < qry
        later = (qry > key).astype(MXU_DTYPE)
        qs = [qT_ref[h * HEAD_DIM:(h + 1) * HEAD_DIM, :] for h in range(HP)]

        def tile(j, carry, mask):
            r0 = pl.multiple_of(j * blk, blk)
            hsl = [slice(h * HEAD_DIM, (h + 1) * HEAD_DIM) for h in range(HP)]
            zs = [jnp.dot(k_ref[pl.ds(r0, blk), hsl[h]], qs[h], preferred_element_type=F32) for h in range(HP)]
            lbs = []
            for h in range(HP):
                lb = _log1m_beta(zs[h])
                lbs.append(lb if mask is None else jnp.where(mask, lb, 0.0))
            sums = [_dot01_left(later, lbs[h]) for h in range(HP)]
            probs = []
            for h in range(HP):
                lt_ref[h, 0, j] = carry[h][0]
                a = jnp.exp(zs[h] + lbs[h] + sums[h] + carry[h][0])
                probs.append((a if mask is None else jnp.where(mask, a, 0.0)).astype(MXU_DTYPE))
            pvs = [jnp.dot(vT_ref[j, hsl[h], :], probs[h], preferred_element_type=F32) for h in range(HP)]
            return tuple((carry[h][0] + jnp.sum(lbs[h], axis=0, keepdims=True), carry[h][1] + pvs[h]) for h in range(HP))

        init = tuple((jnp.zeros((1, blk), F32), jnp.zeros((HEAD_DIM, blk), F32)) for _ in range(HP))
        carry = tile(i, init, strict)
        carry = lax.fori_loop(0, i, lambda jj, c: tile(i - 1 - jj, c, None), carry)
        for h in range(HP):
            oT_ref[h * HEAD_DIM:(h + 1) * HEAD_DIM, :] = carry[h][1]

    hd = HP * HEAD_DIM
    return pl.pallas_call(
        body, name=name, grid=(4 // HP, nb),
        in_specs=[pl.BlockSpec((hd, blk), lambda p, i: (p, i)),
                  pl.BlockSpec((S, hd), lambda p, i: (0, kcb + p)),
                  pl.BlockSpec((nb, hd, blk), lambda p, i: (0, p, 0))],
        out_specs=[pl.BlockSpec((hd, blk), lambda p, i: (p, i)),
                   pl.BlockSpec((HP, 1, nb, 1, blk), lambda p, i: (p, i, 0, 0, 0))],
        out_shape=[jax.ShapeDtypeStruct((GROUP_WIDTH, S), F32), jax.ShapeDtypeStruct((4, nb, nb, 1, blk), F32)],
        compiler_params=_cparams("arbitrary", "arbitrary"),
    )(qT, h_att, vT3)


def _sb_bwd_t(qT, h_att, kT3, dmix, dmixT, later_sums, *, dcb, qscale, blk, name):
    S = h_att.shape[0]
    nb = S // blk
    hd = HP * HEAD_DIM
    qcb, kcb, vcb = COL_SQ // hd, COL_SK // hd, COL_SV // hd
    dcr = dcb * 128 // hd

    def body(qT_ref, q_ref, k_ref, kT_ref, v_ref, do_ref, doT_ref, lt_ref, dqT_ref, dk_ref, dv_ref):
        i = pl.program_id(1)

        @pl.when(i == 0)
        def _():
            dk_ref[...] = jnp.zeros_like(dk_ref)
            dv_ref[...] = jnp.zeros_like(dv_ref)

        key = lax.broadcasted_iota(jnp.int32, (blk, blk), 0)
        qry = lax.broadcasted_iota(jnp.int32, (blk, blk), 1)
        strict = key < qry
        later = (qry > key).astype(MXU_DTYPE)
        before = (qry < key).astype(MXU_DTYPE)
        per_head = []
        for h in range(HP):
            hs = slice(h * HEAD_DIM, (h + 1) * HEAD_DIM)
            per_head.append(dict(qT=qT_ref[hs, :], q=q_ref[:, hs], doT=doT_ref[hs, :].astype(MXU_DTYPE),
                                 do=do_ref[:, hs].astype(MXU_DTYPE)))

        def tile(j, carry, mask):
            r0 = pl.multiple_of(j * blk, blk)
            rows = pl.ds(r0, blk)
            hsl = [slice(h * HEAD_DIM, (h + 1) * HEAD_DIM) for h in range(HP)]
            zs = [jnp.dot(k_ref[rows, hsl[h]], per_head[h]["qT"], preferred_element_type=F32) for h in range(HP)]
            das = [jnp.dot(v_ref[rows, hsl[h]], per_head[h]["doT"], preferred_element_type=F32) for h in range(HP)]
            lbs = []
            for h in range(HP):
                lb = _log1m_beta(zs[h])
                lbs.append(lb if mask is None else jnp.where(mask, lb, 0.0))
            sums = [_dot01_left(later, lbs[h]) for h in range(HP)]
            probs, gs = [], []
            for h in range(HP):
                a = jnp.exp(zs[h] + lbs[h] + sums[h] + lt_ref[h, 0, j])
                a = a if mask is None else jnp.where(mask, a, 0.0)
                gs.append(das[h] * a)
                probs.append(a.astype(MXU_DTYPE))
            for h in range(HP):
                dv_ref[rows, hsl[h]] += jnp.dot(probs[h], per_head[h]["do"], preferred_element_type=F32)
            es = [_dot01_left(before, gs[h]) for h in range(HP)]
            dzs = []
            for h in range(HP):
                dz = gs[h] * jnp.exp(lbs[h]) - (carry[h][0] + es[h]) * jnp.exp(zs[h] + lbs[h])
                dzs.append((dz if mask is None else jnp.where(mask, dz, 0.0)).astype(MXU_DTYPE))
            for h in range(HP):
                dk_ref[rows, hsl[h]] += jnp.dot(dzs[h], per_head[h]["q"], preferred_element_type=F32)
            return tuple((carry[h][0] + jnp.sum(gs[h], axis=0, keepdims=True),
                          carry[h][1] + jnp.dot(kT_ref[j, hsl[h], :], dzs[h], preferred_element_type=F32))
                         for h in range(HP))

        init = tuple((jnp.zeros((1, blk), F32), jnp.zeros((HEAD_DIM, blk), F32)) for _ in range(HP))
        carry = lax.fori_loop(0, i, lambda j, c: tile(j, c, None), init)
        carry = tile(i, carry, strict)
        for h in range(HP):
            dqT_ref[h * HEAD_DIM:(h + 1) * HEAD_DIM, :] = carry[h][1] * qscale

    return pl.pallas_call(
        body, name=name, grid=(4 // HP, nb),
        in_specs=[pl.BlockSpec((hd, blk), lambda p, i: (p, i)),
                  pl.BlockSpec((blk, hd), lambda p, i: (i, qcb + p)),
                  pl.BlockSpec((S, hd), lambda p, i: (0, kcb + p)),
                  pl.BlockSpec((nb, hd, blk), lambda p, i: (0, p, 0)),
                  pl.BlockSpec((S, hd), lambda p, i: (0, vcb + p)),
                  pl.BlockSpec((blk, hd), lambda p, i: (i, dcr + p)),
                  pl.BlockSpec((hd, blk), lambda p, i: (dcr + p, i)),
                  pl.BlockSpec((HP, 1, nb, 1, blk), lambda p, i: (p, i, 0, 0, 0))],
        out_specs=[pl.BlockSpec((hd, blk), lambda p, i: (p, i)),
                   pl.BlockSpec((S, hd), lambda p, i: (0, p)),
                   pl.BlockSpec((S, hd), lambda p, i: (0, p))],
        out_shape=[jax.ShapeDtypeStruct((GROUP_WIDTH, S), F32), jax.ShapeDtypeStruct((S, GROUP_WIDTH), F32),
                   jax.ShapeDtypeStruct((S, GROUP_WIDTH), F32)],
        compiler_params=_cparams("arbitrary", "arbitrary"),
    )(qT, h_att, h_att, kT3, h_att, dmix, dmixT, later_sums)


def _swa_scores(q_ref, k_ref, n, h, start):
    g = h // 2
    kb = k_ref[pl.ds(start, 2 * WINDOW), g * HEAD_DIM:(g + 1) * HEAD_DIM]
    s = lax.dot_general(q_ref[:, h * HEAD_DIM:(h + 1) * HEAD_DIM], kb, NT, preferred_element_type=F32) * (HEAD_DIM ** -0.5)
    dist = (n * WINDOW + lax.broadcasted_iota(jnp.int32, (WINDOW, 2 * WINDOW), 0)
            - start - lax.broadcasted_iota(jnp.int32, (WINDOW, 2 * WINDOW), 1))
    s = s - SWA_SLOPES[h] * dist.astype(F32)
    valid = (dist >= 0) & (dist < WINDOW)
    return jnp.where(valid, s, NEG_INF), kb


def _swa_fwd(h_att, sinks, *, name):
    S = h_att.shape[0]
    nb = S // WINDOW
    qcb, kcb, vcb = COL_WQ // 256, COL_WK // 128, COL_WV // 128

    def body(sink_ref, q_ref, k_ref, v_ref, o_ref, lse_ref):
        n = pl.program_id(0)
        start = pl.multiple_of(jnp.maximum(n - 1, 0) * WINDOW, WINDOW)
        scores = [_swa_scores(q_ref, k_ref, n, h, start)[0] for h in range(4)]
        probs = []
        for h in range(4):
            sink = sink_ref[h]
            m = jnp.maximum(jnp.max(scores[h], axis=1, keepdims=True), sink)
            e = jnp.exp(scores[h] - m)
            den = jnp.sum(e, axis=1, keepdims=True) + jnp.exp(sink - m)
            probs.append((e / den).astype(MXU_DTYPE))
            lse_ref[h] = m + jnp.log(den)
        for h in range(4):
            vb = v_ref[pl.ds(start, 2 * WINDOW), (h // 2) * HEAD_DIM:(h // 2 + 1) * HEAD_DIM]
            o_ref[:, h * HEAD_DIM:(h + 1) * HEAD_DIM] = jnp.dot(probs[h], vb, preferred_element_type=F32)

    return pl.pallas_call(
        body, name=name, grid=(nb,),
        in_specs=[pl.BlockSpec(memory_space=pltpu.SMEM),
                  pl.BlockSpec((WINDOW, 256), lambda n: (n, qcb)),
                  pl.BlockSpec((S, 128), lambda n: (0, kcb)),
                  pl.BlockSpec((S, 128), lambda n: (0, vcb))],
        out_specs=[pl.BlockSpec((WINDOW, 256), lambda n: (n, 0)), pl.BlockSpec((4, WINDOW, 1), lambda n: (0, n, 0))],
        out_shape=[jax.ShapeDtypeStruct((S, GROUP_WIDTH), F32), jax.ShapeDtypeStruct((4, S, 1), F32)],
        compiler_params=_cparams("arbitrary"),
    )(sinks, h_att, h_att, h_att)


def _swa_bwd(h_att, sinks, dmix, o_arr, lse, *, dcb, name):
    S = h_att.shape[0]
    nb = S // WINDOW
    qcb, kcb, vcb = COL_WQ // 256, COL_WK // 128, COL_WV // 128

    def body(sink_ref, q_ref, k_ref, v_ref, do_ref, o_ref, lse_ref, dq_ref, dk_ref, dv_ref, dsink_ref):
        n = pl.program_id(0)

        @pl.when(n == 0)
        def _():
            dk_ref[...] = jnp.zeros_like(dk_ref)
            dv_ref[...] = jnp.zeros_like(dv_ref)
            dsink_ref[...] = jnp.zeros_like(dsink_ref)

        start = pl.multiple_of(jnp.maximum(n - 1, 0) * WINDOW, WINDOW)
        rows = pl.ds(start, 2 * WINDOW)
        hsl = [slice(h * HEAD_DIM, (h + 1) * HEAD_DIM) for h in range(4)]
        gsl = [slice(g * HEAD_DIM, (g + 1) * HEAD_DIM) for g in range(2)]
        scale = HEAD_DIM ** -0.5
        sk = [_swa_scores(q_ref, k_ref, n, h, start) for h in range(4)]
        dobs = [do_ref[:, hsl[h]].astype(MXU_DTYPE) for h in range(4)]
        dps = [lax.dot_general(dobs[h], v_ref[rows, gsl[h // 2]], NT, preferred_element_type=F32) for h in range(4)]
        prs, dss = [], []
        for h in range(4):
            lse_h = lse_ref[h]
            pr = jnp.exp(sk[h][0] - lse_h)
            delta = jnp.sum(do_ref[:, hsl[h]] * o_ref[:, hsl[h]], axis=1, keepdims=True)
            dss.append((pr * (dps[h] - delta)).astype(MXU_DTYPE))
            prs.append(pr.astype(MXU_DTYPE))
            dsink_ref[h:h + 1, :] += jnp.zeros((1, 128), F32) - jnp.sum(jnp.exp(sink_ref[h] - lse_h) * delta)
        for h in range(4):
            dq_ref[:, hsl[h]] = jnp.dot(dss[h], sk[h][1], preferred_element_type=F32) * scale
        for g in range(2):
            dk_ref[rows, gsl[g]] += (lax.dot_general(dss[2 * g], q_ref[:, hsl[2 * g]], TN, preferred_element_type=F32)
                                     + lax.dot_general(dss[2 * g + 1], q_ref[:, hsl[2 * g + 1]], TN,
                                                       preferred_element_type=F32)) * scale
            dv_ref[rows, gsl[g]] += (lax.dot_general(prs[2 * g], dobs[2 * g], TN, preferred_element_type=F32)
                                     + lax.dot_general(prs[2 * g + 1], dobs[2 * g + 1], TN, preferred_element_type=F32))

    return pl.pallas_call(
        body, name=name, grid=(nb,),
        in_specs=[pl.BlockSpec(memory_space=pltpu.SMEM),
                  pl.BlockSpec((WINDOW, 256), lambda n: (n, qcb)),
                  pl.BlockSpec((S, 128), lambda n: (0, kcb)),
                  pl.BlockSpec((S, 128), lambda n: (0, vcb)),
                  pl.BlockSpec((WINDOW, 256), lambda n: (n, dcb)),
                  pl.BlockSpec((WINDOW, 256), lambda n: (n, 0)),
                  pl.BlockSpec((4, WINDOW, 1), lambda n: (0, n, 0))],
        out_specs=[pl.BlockSpec((WINDOW, 256), lambda n: (n, 0)),
                   pl.BlockSpec((S, 128), lambda n: (0, 0)),
                   pl.BlockSpec((S, 128), lambda n: (0, 0)),
                   pl.BlockSpec((4, 128), lambda n: (0, 0))],
        out_shape=[jax.ShapeDtypeStruct((S, GROUP_WIDTH), F32), jax.ShapeDtypeStruct((S, 128), F32),
                   jax.ShapeDtypeStruct((S, 128), F32), jax.ShapeDtypeStruct((4, 128), F32)],
        compiler_params=_cparams("arbitrary"),
    )(sinks, h_att, h_att, h_att, dmix, o_arr, lse)


def _tri(n, incl, upper):
    r = lax.broadcasted_iota(jnp.int32, (n, n), 0)
    c = lax.broadcasted_iota(jnp.int32, (n, n), 1)
    if upper:
        m = (r <= c) if incl else (r < c)
    else:
        m = (r >= c) if incl else (r > c)
    return m.astype(MXU_DTYPE)


def _fox_gate_fwd(fg, b_f, *, name):
    _, R, _ = fg.shape

    def body(b_ref, fg_ref, pos_ref, neg_ref):
        up_incl = _tri(128, True, True)
        ones = jnp.ones((128, 128), MXU_DTYPE)
        for h in range(4):
            z = fg_ref[h] + b_ref[h]
            logf = jnp.minimum(z, 0.0) - jnp.log(1.0 + jnp.exp(-jnp.abs(z)))
            within = _dot01(logf, up_incl, parts=3)
            totals = _dot01(logf, ones, parts=3)
            rem = within + _rows_other(totals, R, after=False)
            for part in range(3):
                piece = rem.astype(MXU_DTYPE)
                rem = rem - piece.astype(F32)
                pos_ref[h, part] = piece
                neg_ref[h, part] = -piece

    shape = (4, 3) + fg.shape[1:]
    return pl.pallas_call(
        body, name=name,
        in_specs=[pl.BlockSpec(memory_space=pltpu.SMEM), pl.BlockSpec(memory_space=pltpu.VMEM)],
        out_specs=[pl.BlockSpec(memory_space=pltpu.VMEM)] * 2,
        out_shape=[jax.ShapeDtypeStruct(shape, MXU_DTYPE)] * 2,
    )(b_f, fg)


def _rows_other(totals, n, after):
    r = lax.broadcasted_iota(jnp.int32, (n, n), 0)
    c = lax.broadcasted_iota(jnp.int32, (n, n), 1)
    m = ((c > r) if after else (c < r)).astype(MXU_DTYPE)
    acc = None
    rem = totals
    for _ in range(3):
        part = rem.astype(MXU_DTYPE)
        rem = rem - part.astype(F32)
        t = jnp.dot(m, part, preferred_element_type=F32)
        acc = t if acc is None else acc + t
    return acc


def _fox_gate_bwd(fg, b_f, dcum_k, dcum_q, *, q_unscale, name):
    _, R, _ = fg.shape

    def body(b_ref, fg_ref, dck_ref, dcq_ref, dfg_ref, db_ref):
        low_incl = _tri(128, True, False)
        ones = jnp.ones((128, 128), MXU_DTYPE)
        for h in range(4):
            dc = dcq_ref[h] * q_unscale - dck_ref[h]
            dlogf = _dot01(dc, low_incl, parts=3) + _rows_other(_dot01(dc, ones, parts=3), R, after=True)
            z = fg_ref[h] + b_ref[h]
            dz = dlogf * jnp.exp(jnp.minimum(-z, 0.0) - jnp.log(1.0 + jnp.exp(-jnp.abs(z))))
            dfg_ref[h] = dz
            db_ref[h:h + 1, :] = jnp.zeros((1, 128), F32) + jnp.sum(dz)

    return pl.pallas_call(
        body, name=name,
        in_specs=[pl.BlockSpec(memory_space=pltpu.SMEM)] + [pl.BlockSpec(memory_space=pltpu.VMEM)] * 3,
        out_specs=[pl.BlockSpec(memory_space=pltpu.VMEM), pl.BlockSpec(memory_space=pltpu.VMEM)],
        out_shape=[jax.ShapeDtypeStruct(fg.shape, F32), jax.ShapeDtypeStruct((4, 128), F32)],
    )(b_f, fg, dcum_k, dcum_q)


def _rope_rot(transpose):
    r = lax.broadcasted_iota(jnp.int32, (MLA_PAD, MLA_PAD), 0)
    c = lax.broadcasted_iota(jnp.int32, (MLA_PAD, MLA_PAD), 1)
    if transpose:
        r, c = c, r
    half = MLA_ROPE // 2
    lo, mid, hi = HEAD_DIM, HEAD_DIM + half, HEAD_DIM + MLA_ROPE
    minus = (c >= lo) & (c < mid) & (r == c + half)
    plus = (c >= mid) & (c < hi) & (r == c - half)
    return jnp.where(plus, 1.0, jnp.where(minus, -1.0, 0.0)).astype(MXU_DTYPE)


def _rope_lanes():
    lane = lax.broadcasted_iota(jnp.int32, (1, MLA_PAD), 1)
    return ((lane >= HEAD_DIM) & (lane < HEAD_DIM + MLA_ROPE)).astype(F32)


def _rms(x, g, eps=1e-6):
    r = lax.rsqrt(jnp.mean(x * x, axis=-1, keepdims=True) + eps)
    return x * r * g, r


def _rms_bwd(dy, x, r, g):
    xh = x * r
    dxh = dy * g
    dx = r * (dxh - xh * jnp.mean(dxh * xh, axis=-1, keepdims=True))
    return dx, dy * xh


def _mla_prep_fwd(lat, g_q, g_kv, wuq, wuk, wuv, cosm, sinm, *, bs, name):
    S = lat.shape[0]

    def body(lat_ref, gq_ref, gkv_ref, wuq_ref, wuk_ref, wuv_ref, cos_ref, sin_ref,
             q_ref, k_ref, v_ref, qT_ref, kT_ref, vT_ref):
        rot = _rope_rot(False)
        cosm_, sinm_ = cos_ref[...], sin_ref[...]
        nq, _ = _rms(lat_ref[:, 0:MLA_Q_RANK], gq_ref[...])
        nkv, _ = _rms(lat_ref[:, MLA_Q_RANK:MLA_Q_RANK + MLA_KV_RANK], gkv_ref[...])
        qlat = jnp.dot(nq.astype(MXU_DTYPE), wuq_ref[...], preferred_element_type=F32)
        klat = jnp.dot(nkv.astype(MXU_DTYPE), wuk_ref[...], preferred_element_type=F32)
        v = jnp.dot(nkv.astype(MXU_DTYPE), wuv_ref[...], preferred_element_type=F32)
        v_ref[...] = v.astype(v_ref.dtype)
        vT_ref[0] = v.T.astype(vT_ref.dtype)
        krb = lat_ref[:, 384:512]
        kr = krb * (cosm_ * _rope_lanes()) + _dot01(krb, rot, parts=3) * sinm_
        for h in range(4):
            sl = slice(h * MLA_PAD, (h + 1) * MLA_PAD)
            qh = qlat[:, sl]
            q = (qh * cosm_ + _dot01(qh, rot, parts=3) * sinm_) * (MLA_QK ** -0.5)
            k = klat[:, sl] + kr
            q_ref[:, sl] = q.astype(q_ref.dtype)
            k_ref[:, sl] = k.astype(k_ref.dtype)
            qT_ref[sl, :] = q.T.astype(qT_ref.dtype)
            kT_ref[0, sl, :] = k.T.astype(kT_ref.dtype)

    full = lambda a: pl.BlockSpec(a.shape, lambda i: (0,) * a.ndim)
    return pl.pallas_call(
        body, name=name, grid=(S // bs,),
        in_specs=[pl.BlockSpec((bs, LAT_W), lambda i: (i, 0)), full(g_q), full(g_kv), full(wuq), full(wuk), full(wuv),
                  pl.BlockSpec((bs, MLA_PAD), lambda i: (i, 0)), pl.BlockSpec((bs, MLA_PAD), lambda i: (i, 0))],
        out_specs=[pl.BlockSpec((bs, 512), lambda i: (i, 0)), pl.BlockSpec((bs, 512), lambda i: (i, 0)),
                   pl.BlockSpec((bs, 256), lambda i: (i, 0)), pl.BlockSpec((512, bs), lambda i: (0, i)),
                   pl.BlockSpec((1, 512, bs), lambda i: (i, 0, 0)), pl.BlockSpec((1, 256, bs), lambda i: (i, 0, 0))],
        out_shape=[jax.ShapeDtypeStruct((S, 512), MXU_DTYPE), jax.ShapeDtypeStruct((S, 512), MXU_DTYPE),
                   jax.ShapeDtypeStruct((S, 256), MXU_DTYPE), jax.ShapeDtypeStruct((512, S), MXU_DTYPE),
                   jax.ShapeDtypeStruct((S // bs, 512, bs), MXU_DTYPE), jax.ShapeDtypeStruct((S // bs, 256, bs), MXU_DTYPE)],
        compiler_params=_cparams("parallel"),
    )(lat, g_q, g_kv, wuq, wuk, wuv, cosm, sinm)


def _mla_prep_bwd(lat, g_q, g_kv, wuq, wuk, wuv, cosm, sinm, dq, dk, dv, *, bs, name):
    S = lat.shape[0]

    def body(lat_ref, gq_ref, gkv_ref, wuq_ref, wuk_ref, wuv_ref, cos_ref, sin_ref, dq_ref, dk_ref, dv_ref,
             dlat_ref, dwuq_ref, dwuk_ref, dwuv_ref, dgq_ref, dgkv_ref):
        @pl.when(pl.program_id(0) == 0)
        def _():
            for r in (dwuq_ref, dwuk_ref, dwuv_ref, dgq_ref, dgkv_ref):
                r[...] = jnp.zeros_like(r)

        rot_t = _rope_rot(True)
        cosm_, sinm_ = cos_ref[...], sin_ref[...]
        cq = lat_ref[:, 0:MLA_Q_RANK]
        ckv = lat_ref[:, MLA_Q_RANK:MLA_Q_RANK + MLA_KV_RANK]
        nq, rq = _rms(cq, gq_ref[...])
        nkv, rkv = _rms(ckv, gkv_ref[...])
        nqb, nkvb = nq.astype(MXU_DTYPE), nkv.astype(MXU_DTYPE)

        dqlat = []
        dkr = jnp.zeros((bs, MLA_PAD), F32)
        for h in range(4):
            sl = slice(h * MLA_PAD, (h + 1) * MLA_PAD)
            dqh = dq_ref[sl, :].T
            dqlat.append(dqh * cosm_ + _dot01(dqh * sinm_, rot_t, parts=3))
            dkr = dkr + dk_ref[:, sl]
        dqlat = jnp.concatenate(dqlat, axis=1).astype(MXU_DTYPE)
        dkb = dk_ref[...].astype(MXU_DTYPE)
        dvb = dv_ref[...].astype(MXU_DTYPE)

        dnq = lax.dot_general(dqlat, wuq_ref[...], NT, preferred_element_type=F32)
        dnkv = (lax.dot_general(dkb, wuk_ref[...], NT, preferred_element_type=F32)
                + lax.dot_general(dvb, wuv_ref[...], NT, preferred_element_type=F32))
        dwuq_ref[...] += lax.dot_general(nqb, dqlat, TN, preferred_element_type=F32)
        dwuk_ref[...] += lax.dot_general(nkvb, dkb, TN, preferred_element_type=F32)
        dwuv_ref[...] += lax.dot_general(nkvb, dvb, TN, preferred_element_type=F32)
        dcq, tq = _rms_bwd(dnq, cq, rq, gq_ref[...])
        dckv, tkv = _rms_bwd(dnkv, ckv, rkv, gkv_ref[...])
        dgq_ref[...] += jnp.sum(tq, axis=0, keepdims=True)
        dgkv_ref[...] += jnp.sum(tkv, axis=0, keepdims=True)
        dlat_ref[:, 0:MLA_Q_RANK] = dcq.astype(dlat_ref.dtype)
        dlat_ref[:, MLA_Q_RANK:MLA_Q_RANK + MLA_KV_RANK] = dckv.astype(dlat_ref.dtype)
        dkrb = dkr * (cosm_ * _rope_lanes()) + _dot01(dkr * sinm_, rot_t, parts=3)
        dlat_ref[:, 384:512] = dkrb.astype(dlat_ref.dtype)

    full = lambda a: pl.BlockSpec(a.shape, lambda i: (0,) * a.ndim)
    row = lambda w: pl.BlockSpec((bs, w), lambda i: (i, 0))
    acc = lambda *shape: pl.BlockSpec(shape, lambda i: (0,) * len(shape))
    return pl.pallas_call(
        body, name=name, grid=(S // bs,),
        in_specs=[row(LAT_W), full(g_q), full(g_kv), full(wuq), full(wuk), full(wuv), row(MLA_PAD), row(MLA_PAD),
                  pl.BlockSpec((512, bs), lambda i: (0, i)), row(512), row(256)],
        out_specs=[row(512), acc(256, 512), acc(128, 512), acc(128, 256), acc(1, 256), acc(1, 128)],
        out_shape=[jax.ShapeDtypeStruct((S, 512), MXU_DTYPE), jax.ShapeDtypeStruct((256, 512), F32),
                   jax.ShapeDtypeStruct((128, 512), F32), jax.ShapeDtypeStruct((128, 256), F32),
                   jax.ShapeDtypeStruct((1, 256), F32), jax.ShapeDtypeStruct((1, 128), F32)],
        compiler_params=_cparams("arbitrary"),
    )(lat, g_q, g_kv, wuq, wuk, wuv, cosm, sinm, dq, dk, dv)


def _row_spec(bs, w):
    return pl.BlockSpec((bs, w), lambda i: (i, 0))


def _vec_spec(w):
    return pl.BlockSpec((1, w), lambda i: (0, 0))


def _mix_specs(bs):
    return [pl.BlockSpec((GROUP_WIDTH, bs), lambda i: (0, i))] * 3 + [_row_spec(bs, GROUP_WIDTH)]


def _mix_groups(a_ref, b_ref, c_ref, d_ref):
    return [a_ref[...].T, b_ref[...].T, c_ref[...].T, d_ref[...]]


def _gnorm_fwd(outs, g, *, bs, name):
    S = outs[3].shape[0]

    def body(a_ref, b_ref, c_ref, d_ref, g_ref, o_ref, oT_ref):
        for k, x in enumerate(_mix_groups(a_ref, b_ref, c_ref, d_ref)):
            sl = slice(k * GROUP_WIDTH, (k + 1) * GROUP_WIDTH)
            y, _ = _rms(x, g_ref[:, sl])
            o_ref[:, sl] = y.astype(o_ref.dtype)
            oT_ref[sl, :] = y.T.astype(oT_ref.dtype)

    return pl.pallas_call(
        body, name=name, grid=(S // bs,),
        in_specs=_mix_specs(bs) + [_vec_spec(D_MODEL)],
        out_specs=[_row_spec(bs, D_MODEL), pl.BlockSpec((D_MODEL, bs), lambda i: (0, i))],
        out_shape=[jax.ShapeDtypeStruct((S, D_MODEL), MXU_DTYPE), jax.ShapeDtypeStruct((D_MODEL, S), MXU_DTYPE)],
        compiler_params=_cparams("parallel"),
    )(*outs, g)


def _gnorm_bwd(dgn, outs, g, *, bs, name):
    S = dgn.shape[0]

    def body(dgn_ref, a_ref, b_ref, c_ref, d_ref, g_ref, dmix_ref, dmixT_ref, dg_ref):
        @pl.when(pl.program_id(0) == 0)
        def _():
            dg_ref[...] = jnp.zeros_like(dg_ref)

        for k, x in enumerate(_mix_groups(a_ref, b_ref, c_ref, d_ref)):
            sl = slice(k * GROUP_WIDTH, (k + 1) * GROUP_WIDTH)
            _, r = _rms(x, g_ref[:, sl])
            dx, t = _rms_bwd(dgn_ref[:, sl], x, r, g_ref[:, sl])
            dmix_ref[:, sl] = dx
            dmixT_ref[sl, :] = dx.T
            dg_ref[:, sl] += jnp.sum(t, axis=0, keepdims=True)

    return pl.pallas_call(
        body, name=name, grid=(S // bs,),
        in_specs=[_row_spec(bs, D_MODEL)] + _mix_specs(bs) + [_vec_spec(D_MODEL)],
        out_specs=[_row_spec(bs, D_MODEL), pl.BlockSpec((D_MODEL, bs), lambda i: (0, i)), _vec_spec(D_MODEL)],
        out_shape=[jax.ShapeDtypeStruct((S, D_MODEL), F32), jax.ShapeDtypeStruct((D_MODEL, S), F32),
                   jax.ShapeDtypeStruct((1, D_MODEL), F32)],
        compiler_params=_cparams("arbitrary"),
    )(dgn, *outs, g)


def _ln_fwd(u, g, b, *, bs, name):
    S = u.shape[0]

    def body(u_ref, g_ref, b_ref, y_ref, yb_ref, ybT_ref, xh_ref, rs_ref):
        x = u_ref[...]
        mu = jnp.mean(x, axis=-1, keepdims=True)
        xc = x - mu
        rs = lax.rsqrt(jnp.mean(xc * xc, axis=-1, keepdims=True) + 1e-5)
        xh = xc * rs
        y = xh * g_ref[...] + b_ref[...]
        y_ref[...] = y
        yb_ref[...] = y.astype(yb_ref.dtype)
        ybT_ref[...] = y.T.astype(ybT_ref.dtype)
        xh_ref[...] = xh
        rs_ref[...] = rs

    return pl.pallas_call(
        body, name=name, grid=(S // bs,),
        in_specs=[_row_spec(bs, D_MODEL), _vec_spec(D_MODEL), _vec_spec(D_MODEL)],
        out_specs=[_row_spec(bs, D_MODEL), _row_spec(bs, D_MODEL), pl.BlockSpec((D_MODEL, bs), lambda i: (0, i)),
                   _row_spec(bs, D_MODEL), _row_spec(bs, 1)],
        out_shape=[jax.ShapeDtypeStruct((S, D_MODEL), F32), jax.ShapeDtypeStruct((S, D_MODEL), MXU_DTYPE),
                   jax.ShapeDtypeStruct((D_MODEL, S), MXU_DTYPE), jax.ShapeDtypeStruct((S, D_MODEL), F32),
                   jax.ShapeDtypeStruct((S, 1), F32)],
        compiler_params=_cparams("parallel"),
    )(u, g, b)


def _ln_bwd(dy, xh, rs, g, *, bs, name):
    S = dy.shape[0]

    def body(dy_ref, xh_ref, rs_ref, g_ref, du_ref, dub_ref, dg_ref, db_ref):
        @pl.when(pl.program_id(0) == 0)
        def _():
            dg_ref[...] = jnp.zeros_like(dg_ref)
            db_ref[...] = jnp.zeros_like(db_ref)

        dy_, xh_ = dy_ref[...], xh_ref[...]
        dxh = dy_ * g_ref[...]
        du = rs_ref[...] * (dxh - jnp.mean(dxh, axis=-1, keepdims=True)
                            - xh_ * jnp.mean(dxh * xh_, axis=-1, keepdims=True))
        du_ref[...] = du
        dub_ref[...] = du.astype(dub_ref.dtype)
        dg_ref[...] += jnp.sum(dy_ * xh_, axis=0, keepdims=True)
        db_ref[...] += jnp.sum(dy_, axis=0, keepdims=True)

    return pl.pallas_call(
        body, name=name, grid=(S // bs,),
        in_specs=[_row_spec(bs, D_MODEL), _row_spec(bs, D_MODEL), _row_spec(bs, 1), _vec_spec(D_MODEL)],
        out_specs=[_row_spec(bs, D_MODEL), _row_spec(bs, D_MODEL), _vec_spec(D_MODEL), _vec_spec(D_MODEL)],
        out_shape=[jax.ShapeDtypeStruct((S, D_MODEL), F32), jax.ShapeDtypeStruct((S, D_MODEL), MXU_DTYPE),
                   jax.ShapeDtypeStruct((1, D_MODEL), F32), jax.ShapeDtypeStruct((1, D_MODEL), F32)],
        compiler_params=_cparams("arbitrary"),
    )(dy, xh, rs, g)


def _swiglu_fwd(gu, *, bs, name):
    S = gu.shape[0]

    def body(gu_ref, a_ref, aT_ref):
        gt = gu_ref[:, :D_FF]
        a = gt / (1.0 + jnp.exp(-gt)) * gu_ref[:, D_FF:]
        a_ref[...] = a.astype(a_ref.dtype)
        aT_ref[...] = a.T.astype(aT_ref.dtype)

    return pl.pallas_call(
        body, name=name, grid=(S // bs,),
        in_specs=[_row_spec(bs, 2 * D_FF)],
        out_specs=[_row_spec(bs, D_FF), pl.BlockSpec((D_FF, bs), lambda i: (0, i))],
        out_shape=[jax.ShapeDtypeStruct((S, D_FF), MXU_DTYPE), jax.ShapeDtypeStruct((D_FF, S), MXU_DTYPE)],
        compiler_params=_cparams("parallel"),
    )(gu)


def _swiglu_bwd(da, gu, *, bs, name):
    S = gu.shape[0]

    def body(da_ref, gu_ref, dgu_ref):
        gt, da_ = gu_ref[:, :D_FF], da_ref[...]
        sg = 1.0 / (1.0 + jnp.exp(-gt))
        silu = gt * sg
        dgu_ref[:, :D_FF] = (da_ * gu_ref[:, D_FF:] * (sg + silu * (1.0 - sg))).astype(dgu_ref.dtype)
        dgu_ref[:, D_FF:] = (da_ * silu).astype(dgu_ref.dtype)

    return pl.pallas_call(
        body, name=name, grid=(S // bs,),
        in_specs=[_row_spec(bs, D_FF), _row_spec(bs, 2 * D_FF)],
        out_specs=_row_spec(bs, 2 * D_FF), out_shape=jax.ShapeDtypeStruct((S, 2 * D_FF), MXU_DTYPE),
        compiler_params=_cparams("parallel"),
    )(da, gu)


def _loss_head(y, target, *, bs, name):
    S = y.shape[0]

    def body(y_ref, t_ref, dy_ref, loss_ref):
        @pl.when(pl.program_id(0) == 0)
        def _():
            loss_ref[...] = jnp.zeros_like(loss_ref)

        e = y_ref[...] - t_ref[...]
        dy_ref[...] = e * (1.0 / D_MODEL)
        per_tok = jnp.mean(e * e, axis=-1, keepdims=True)
        loss_ref[...] += 0.5 * jnp.sum(per_tok, axis=0, keepdims=True)

    return pl.pallas_call(
        body, name=name, grid=(S // bs,),
        in_specs=[_row_spec(bs, D_MODEL), _row_spec(bs, D_MODEL)],
        out_specs=[_row_spec(bs, D_MODEL), pl.BlockSpec((1, 1), lambda i: (0, 0))],
        out_shape=[jax.ShapeDtypeStruct((S, D_MODEL), F32), jax.ShapeDtypeStruct((1, 1), F32)],
        compiler_params=_cparams("arbitrary"),
    )(y, target)


def _blk(n, target):
    if n <= target:
        return n
    best = None
    for b in range(128, target + 1, 128):
        if n % b == 0:
            best = b
    assert best is not None, n
    return best


def _rope_tables(S):
    pos = jnp.arange(S, dtype=F32)
    inv = ROPE_THETA ** (-jnp.arange(0, MLA_ROPE, 2, dtype=F32) / MLA_ROPE)
    ang = pos[:, None] * inv[None, :]
    cos, sin = jnp.cos(ang), jnp.sin(ang)
    one, zero, pad = jnp.ones((S, HEAD_DIM), F32), jnp.zeros((S, HEAD_DIM), F32), jnp.zeros((S, MLA_PAD - MLA_QK), F32)
    return jnp.concatenate([one, cos, cos, pad], axis=1), jnp.concatenate([zero, sin, sin, pad], axis=1)


def _prep_weights_a(w_in, w_uq, w_ukv):
    z = lambda n: jnp.zeros((D_MODEL, n), w_in.dtype)
    win_a = jnp.concatenate([w_in[:, 0:768], w_in[:, 1188:2468]], axis=1)
    win_l = jnp.concatenate([w_in[:, 772:1156], z(64), w_in[:, 1156:1188], z(32), w_in[:, 768:772], z(124)], axis=1)
    kv = w_ukv.reshape(MLA_KV_RANK, 4, 2 * HEAD_DIM)
    return dict(
        win_a=win_a, win_l=win_l, win_p=jnp.concatenate([win_a, win_l], axis=1),
        wuq=jnp.pad(w_uq.reshape(MLA_Q_RANK, 4, MLA_QK), ((0, 0), (0, 0), (0, MLA_PAD - MLA_QK))).reshape(MLA_Q_RANK, 512),
        wuk=jnp.pad(kv[:, :, :HEAD_DIM], ((0, 0), (0, 0), (0, HEAD_DIM))).reshape(MLA_KV_RANK, 512),
        wuv=kv[:, :, HEAD_DIM:].reshape(MLA_KV_RANK, 256))


def _unprep_grads(dwin_p, dwuq, dwuk, dwuv, dwo, dwgu, dwd):
    dw_in = jnp.concatenate([dwin_p[:, 0:768], dwin_p[:, 2560:2564], dwin_p[:, 2048:2432], dwin_p[:, 2496:2528],
                             dwin_p[:, 768:2048]], axis=1)
    dw_uq = dwuq.reshape(MLA_Q_RANK, 4, MLA_PAD)[:, :, :MLA_QK].reshape(MLA_Q_RANK, 4 * MLA_QK)
    dw_ukv = jnp.concatenate([dwuk.reshape(MLA_KV_RANK, 4, MLA_PAD)[:, :, :HEAD_DIM],
                              dwuv.reshape(MLA_KV_RANK, 4, HEAD_DIM)], axis=2).reshape(MLA_KV_RANK, 512)
    return dict(w_in=dw_in, mla_w_uq=dw_uq, mla_w_ukv=dw_ukv, w_o=dwo, w_gate=dwgu[:, :D_FF], w_up=dwgu[:, D_FF:],
                w_down=dwd)


def _layer_fwd(l, x, xb, xbT, W, P, tabs, blk, late_weights=None):
    S = x.shape[0]
    nb = S // blk
    n = lambda s: f"l{l}_{s}"
    bs = min(512, S)
    h_att = _mm(xb, W["win_a"], name=n("in_att"), out_dtype=MXU_DTYPE, bm=1024, bn=1024, bk=1024, colscale=Q_COLSCALE)
    lat = _mm(xb, W["win_l"], name=n("in_lat"), bm=2048, bn=LAT_W, bk=1024)
    fg = lat[:, 512:516].T.reshape(4, S // 128, 128)
    cpos, cneg = _fox_gate_fwd(fg, P["fox_b_f"], name=n("fox_gate"))
    one3 = jnp.ones((S, 4, 3), MXU_DTYPE)
    zpad = jnp.zeros((S, 4, MLA_PAD - HEAD_DIM - 6), MXU_DTYPE)
    per_tok = lambda parts: parts.reshape(4, 3, S).transpose(2, 0, 1)
    q_f = jnp.concatenate([h_att[:, COL_FQ:COL_FQ + 256].reshape(S, 4, HEAD_DIM), per_tok(cpos), one3, zpad],
                          axis=2).reshape(S, 4 * MLA_PAD)
    k_f = jnp.concatenate([h_att[:, COL_FK:COL_FK + 256].reshape(S, 4, HEAD_DIM), one3, per_tok(cneg), zpad],
                          axis=2).reshape(S, 4 * MLA_PAD)
    v_f = h_att[:, COL_FV:COL_FV + 256]
    oT_a, lse_a = _smax_fwd_t(q_f.T, k_f, _kv_blocks_t(v_f, blk), dk=MLA_PAD, blk=blk, name=n("fox_fwd"))
    q_m, k_m, v_m, qT_m, kT3_m, vT3_m = _mla_prep_fwd(lat, P["mla_g_q"], P["mla_g_kv"], W["wuq"], W["wuk"], W["wuv"],
                                                      *tabs, bs=blk, name=n("mla_prep"))
    oT_b, lse_b = _smax_fwd_t(qT_m, k_m, vT3_m, dk=MLA_PAD, blk=blk, name=n("mla_fwd"))
    qT_c = h_att[:, COL_SQ:COL_SQ + 256].T
    bsb = min(BLK_STICK, S)
    oT_c, lt_c = _sb_fwd_t(qT_c, h_att, _kv_blocks_t(h_att[:, COL_SV:COL_SV + 256], bsb), blk=bsb, name=n("sb_fwd"))
    out_d, lse_d = _swa_fwd(h_att, P["swa_sinks"], name=n("swa_fwd"))
    outs = (oT_a, oT_b, oT_c, out_d)
    gn, gnT = _gnorm_fwd(outs, P["mix_g"], bs=bs, name=n("gnorm"))
    if late_weights is not None:
        W = dict(W, **late_weights(gn))
    u1 = _mm(gn, W["w_o"], name=n("out_proj"), bm=1024, bn=1024, bk=1024, resid=x, alpha=ALPHA)
    x1, x1b, x1bT, xh1, rs1 = _ln_fwd(u1, P["ln1_g"], P["ln1_b"], bs=bs, name=n("ln1"))
    gu = _mm(x1b, W["wgu"], name=n("gate_up"), bm=2048, bn=512, bk=1024)
    a, aT = _swiglu_fwd(gu, bs=min(256, S), name=n("swiglu"))
    u2 = _mm(a, W["w_down"], name=n("down"), bm=1024, bn=1024, bk=_blk(D_FF, 1408), resid=x1, alpha=ALPHA)
    x2, x2b, x2bT, xh2, rs2 = _ln_fwd(u2, P["ln2_g"], P["ln2_b"], bs=bs, name=n("ln2"))
    saved = dict(xbT=xbT, gnT=gnT, x1bT=x1bT, h_att=h_att, lat=lat, fg=fg, outs=outs, oT_a=oT_a, oT_b=oT_b, q_f=q_f, k_f=k_f, v_f=v_f,
                 qT_c=qT_c, lse_a=lse_a, lse_b=lse_b, lse_d=lse_d, lt_c=lt_c, q_m=q_m, k_m=k_m, v_m=v_m, qT_m=qT_m, kT3_m=kT3_m,
                 xh1=xh1, rs1=rs1, gu=gu, aT=aT, xh2=xh2, rs2=rs2)
    return x2, x2b, x2bT, saved, W


def _layer_bwd(l, dx2, sv, W, P, tabs, blk, send_early=None):
    S = dx2.shape[0]
    n = lambda s: f"l{l}_{s}"
    bs = min(512, S)
    h_att = sv["h_att"]
    du2, du2b, dg2, db2 = _ln_bwd(dx2, sv["xh2"], sv["rs2"], P["ln2_g"], bs=bs, name=n("ln2_bwd"))
    da = _mm(du2b, W["w_down"], name=n("down_dx"), tb=True, bm=1024, bn=_blk(D_FF, 1408), bk=1024)
    dwd = _mm(sv["aT"], du2b, name=n("down_dw"), bm=_blk(D_FF, 1408), bn=1024, bk=1024)
    dgu = _swiglu_bwd(da, sv["gu"], bs=min(256, S), name=n("swiglu_bwd"))
    dx1 = _mm(dgu, W["wgu"], name=n("gate_up_dx"), tb=True, bm=1024, bn=1024, bk=_blk(2 * D_FF, 1408), resid=du2,
              alpha=ALPHA)
    dwgu = _mm(sv["x1bT"], dgu, name=n("gate_up_dw"), bm=1024, bn=_blk(2 * D_FF, 1408), bk=1024)
    du1, du1b, dg1, db1 = _ln_bwd(dx1, sv["xh1"], sv["rs1"], P["ln1_g"], bs=bs, name=n("ln1_bwd"))
    dgn = _mm(du1b, W["w_o"], name=n("out_proj_dx"), tb=True, bm=1024, bn=1024, bk=1024)
    dwo = _mm(sv["gnT"], du1b, name=n("out_proj_dw"), bm=1024, bn=1024, bk=1024)
    mix_g = P["mix_g"]
    if send_early is not None:
        mix_g = mix_g + send_early(dict(w_o=dwo, w_gate=dwgu[:, :D_FF], w_up=dwgu[:, D_FF:], w_down=dwd))[0, 0]
    dmix, dmixT, dmixg = _gnorm_bwd(dgn, sv["outs"], mix_g, bs=bs, name=n("gnorm_bwd"))
    q_f, k_f = sv["q_f"], sv["k_f"]
    dqT_a, dk_a, dva = _smax_bwd_t(q_f.T, q_f, k_f, _kv_blocks_t(k_f, blk), sv["v_f"], dmix, dmixT, sv["oT_a"],
                                   sv["lse_a"], dk=MLA_PAD, dcb=0, qscale=HEAD_DIM ** -0.5, blk=blk, name=n("fox_bwd"))
    dq_a, dk_a = dqT_a.T.reshape(S, 4, MLA_PAD), dk_a.reshape(S, 4, MLA_PAD)
    dqa, dka = dq_a[:, :, :HEAD_DIM].reshape(S, 256), dk_a[:, :, :HEAD_DIM].reshape(S, 256)
    dcq = dq_a[:, :, HEAD_DIM].T.reshape(4, S // 128, 128)
    dck = dk_a[:, :, HEAD_DIM + 3].T.reshape(4, S // 128, 128)
    q_m, k_m = sv["q_m"], sv["k_m"]
    dqT_b, dkb, dvb = _smax_bwd_t(sv["qT_m"], q_m, k_m, sv["kT3_m"], sv["v_m"], dmix, dmixT, sv["oT_b"],
                                  sv["lse_b"], dk=MLA_PAD, dcb=2, qscale=MLA_QK ** -0.5, blk=blk, name=n("mla_bwd"))
    bsb = min(BLK_STICK, S)
    dqT_c, dkc, dvc = _sb_bwd_t(sv["qT_c"], h_att, _kv_blocks_t(h_att[:, COL_SK:COL_SK + 256], bsb), dmix, dmixT,
                                sv["lt_c"], dcb=4, qscale=HEAD_DIM ** -0.5, blk=bsb, name=n("sb_bwd"))
    dqc = dqT_c.T
    dqd, dkd, dvd, dsink = _swa_bwd(h_att, P["swa_sinks"], dmix, sv["outs"][3], sv["lse_d"], dcb=3, name=n("swa_bwd"))
    dlat, dwuq, dwuk, dwuv, dgq, dgkv = _mla_prep_bwd(
        sv["lat"], P["mla_g_q"], P["mla_g_kv"], W["wuq"], W["wuk"], W["wuv"], *tabs, dqT_b, dkb, dvb,
        bs=bs, name=n("mla_prep_bwd"))
    dfg, dbf = _fox_gate_bwd(sv["fg"], P["fox_b_f"], dck, dcq, q_unscale=HEAD_DIM ** 0.5, name=n("fox_gate_bwd"))
    dfg_blk = jnp.pad(dfg.reshape(4, S).T, ((0, 0), (0, 124)))
    dh = jnp.concatenate([t.astype(MXU_DTYPE) for t in (dqa, dka, dva, dqc, dkc, dvc, dqd, dkd, dvd, dlat, dfg_blk)], axis=1)
    dx = _mm(dh, W["win_p"], name=n("in_dx"), tb=True, bm=1024, bn=1024, bk=_blk(PERM_W, 1024), resid=du1, alpha=ALPHA)
    dwin_p = _mm(sv["xbT"], dh, name=n("in_dw"), bm=1024, bn=_blk(PERM_W, 1024), bk=1024)
    grads = _unprep_grads(dwin_p, dwuq, dwuk, dwuv, dwo, dwgu, dwd)
    grads.update(fox_b_f=dbf[:, 0], mla_g_q=dgq[0], mla_g_kv=dgkv[0], swa_sinks=dsink[:, 0], mix_g=dmixg[0],
                 ln1_g=dg1[0], ln1_b=db1[0], ln2_g=dg2[0], ln2_b=db2[0])
    return dx, grads


BIG = ("w_in", "mla_w_uq", "mla_w_ukv", "w_o", "w_gate", "w_up", "w_down")
SMALL = ("fox_b_f", "mla_g_q", "mla_g_kv", "swa_sinks", "mix_g", "ln1_g", "ln1_b", "ln2_g", "ln2_b")
SHARD_AXIS = dict(w_in=2, mla_w_uq=2, mla_w_ukv=2, w_o=1, w_gate=2, w_up=2, w_down=1)
N_CHIPS = 4
ANY = pl.BlockSpec(memory_space=pl.ANY)


HBM = pl.BlockSpec(memory_space=pltpu.HBM)
SEM = pl.BlockSpec(memory_space=pltpu.SEMAPHORE)
N_PEER_CHIPS = N_CHIPS - 1


def _peer_copies(src_ref, land_ref, sems, scatter):
    x, y, c = lax.axis_index("x"), lax.axis_index("y"), lax.axis_index("c")
    me = 2 * x + y
    out = []
    for r, (px, py) in enumerate([(1 - x, y), (x, 1 - y), (1 - x, 1 - y)]):
        theirs = 2 * px + py
        send = pltpu.make_async_remote_copy(
            src_ref=src_ref.at[theirs] if scatter else src_ref, dst_ref=land_ref.at[me],
            send_sem=sems[2 * r], recv_sem=sems[2 * r + 1], device_id=(px, py, c), device_id_type=MESH)
        arrive = pltpu.make_async_remote_copy(
            src_ref=src_ref.at[me] if scatter else src_ref, dst_ref=land_ref.at[theirs],
            send_sem=sems[2 * r], recv_sem=sems[2 * r + 1], device_id=(px, py, c), device_id_type=MESH)
        out.append((send, arrive))
    return out


def _exchange_start(srcs, *, scatter, name):
    nt = len(srcs)
    ns = 2 * N_PEER_CHIPS * nt
    land_shapes = [s.shape if scatter else (N_CHIPS,) + s.shape for s in srcs]

    def body(*refs):
        src_refs, land_refs, outs = refs[:nt], refs[nt:2 * nt], refs[2 * nt:]
        for t in range(nt):
            for send, _ in _peer_copies(src_refs[t], land_refs[t], outs[6 * t:6 * t + 6], scatter):
                send.start()
        outs[-1][...] = jnp.zeros_like(outs[-1])

    res = pl.pallas_call(
        body, name=name,
        out_shape=(*[pltpu.SemaphoreType.DMA(())] * ns, *[pltpu.HBM(s.shape, s.dtype) for s in srcs],
                   *[pltpu.HBM(ls, s.dtype) for ls, s in zip(land_shapes, srcs)], jax.ShapeDtypeStruct((8, 128), F32)),
        in_specs=(HBM,) * (2 * nt), out_specs=(*[SEM] * ns, *[HBM] * (2 * nt), pl.BlockSpec(memory_space=pltpu.VMEM)),
        input_output_aliases={i: ns + i for i in range(2 * nt)},
        compiler_params=pltpu.CompilerParams(has_side_effects=pltpu.SideEffectType.DATAFLOW_SIDE_EFFECTING),
    )(*[pltpu.with_memory_space_constraint(s, pltpu.HBM) for s in srcs],
      *[pltpu.with_memory_space_constraint(lax.empty(ls, s.dtype), pltpu.HBM) for ls, s in zip(land_shapes, srcs)])
    return dict(sems=res[:ns], srcs=res[ns:ns + nt], lands=res[ns + nt:ns + 2 * nt], token=res[-1])


def _exchange_wait(started, after, *, scatter, name):
    nt = len(started["srcs"])
    ns = 2 * N_PEER_CHIPS * nt

    def body(*refs):
        src_refs, land_refs, sems = refs[:nt], refs[nt:2 * nt], refs[2 * nt:2 * nt + ns]
        for t in range(nt):
            for send, arrive in _peer_copies(src_refs[t], land_refs[t], sems[6 * t:6 * t + 6], scatter):
                send.wait_send()
                arrive.wait_recv()

    both = list(started["srcs"]) + list(started["lands"])
    res = pl.pallas_call(
        body, name=name, out_shape=tuple(pltpu.HBM(a.shape, a.dtype) for a in both),
        in_specs=(*[HBM] * (2 * nt), *[SEM] * ns, ANY), out_specs=(HBM,) * (2 * nt),
        input_output_aliases={i: i for i in range(2 * nt)},
        compiler_params=pltpu.CompilerParams(has_side_effects=pltpu.SideEffectType.DATAFLOW_SIDE_EFFECTING),
    )(*both, *started["sems"], after)
    return res[:nt], res[nt:]


def _core_exchange(tensors, *, name):
    nt = len(tensors)

    def body(*refs):
        ins, outs = refs[:nt], refs[nt:2 * nt]
        send_sems, recv_sems = refs[2 * nt:]
        sibling = (lax.axis_index("x"), lax.axis_index("y"), 1 - lax.axis_index("c"))
        copies = [pltpu.make_async_remote_copy(src_ref=ins[t], dst_ref=outs[t], send_sem=send_sems.at[t],
                                               recv_sem=recv_sems.at[t], device_id=sibling, device_id_type=MESH)
                  for t in range(nt)]
        for cp in copies:
            cp.start()
        for cp in copies:
            cp.wait_recv()
        for cp in copies:
            cp.wait_send()

    return pl.pallas_call(
        body, name=name, in_specs=[ANY] * nt, out_specs=[ANY] * nt,
        out_shape=[jax.ShapeDtypeStruct(t.shape, t.dtype) for t in tensors],
        scratch_shapes=[pltpu.SemaphoreType.DMA((nt,)), pltpu.SemaphoreType.DMA((nt,))],
        compiler_params=pltpu.CompilerParams(has_side_effects=True),
    )(*tensors)


def _all_sum_small(block, *, name):
    R = block.shape[0]
    n_dev = 8

    def body(x_ref, o_ref, slots, send_sems, recv_sems):
        x, y, c = lax.axis_index("x"), lax.axis_index("y"), lax.axis_index("c")
        me = 4 * x + 2 * y + c
        slots[me] = x_ref[...]
        sends, recvs = [], []
        for d in range(1, n_dev):
            px, py, pc = x ^ (d >> 2), y ^ ((d >> 1) & 1), c ^ (d & 1)
            theirs = 4 * px + 2 * py + pc
            sends.append(pltpu.make_async_remote_copy(
                src_ref=x_ref, dst_ref=slots.at[me], send_sem=send_sems.at[d - 1], recv_sem=recv_sems.at[d - 1],
                device_id=(px, py, pc), device_id_type=MESH))
            recvs.append(pltpu.make_async_remote_copy(
                src_ref=x_ref, dst_ref=slots.at[theirs], send_sem=send_sems.at[d - 1], recv_sem=recv_sems.at[d - 1],
                device_id=(px, py, pc), device_id_type=MESH))
        for cp in sends:
            cp.start()
        for cp in recvs:
            cp.wait_recv()
        for cp in sends:
            cp.wait_send()
        total = slots[0]
        for k in range(1, n_dev):
            total = total + slots[k]
        o_ref[...] = total

    return pl.pallas_call(
        body, name=name, in_specs=[pl.BlockSpec(memory_space=pltpu.VMEM)],
        out_specs=pl.BlockSpec(memory_space=pltpu.VMEM), out_shape=jax.ShapeDtypeStruct((R, 128), F32),
        scratch_shapes=[pltpu.VMEM((n_dev, R, 128), F32), pltpu.SemaphoreType.DMA((n_dev - 1,)),
                        pltpu.SemaphoreType.DMA((n_dev - 1,))],
        compiler_params=pltpu.CompilerParams(has_side_effects=True),
    )(block)


def _sum_chips_into(acc, land, own, me, layer, *, br, name):
    _, R, C = land.shape

    def body(me_ref, land_ref, own_ref, acc_ref, o_ref):
        mine = me_ref[0]
        total = None
        for k in range(N_CHIPS):
            part = jnp.where(mine == k, own_ref[...], land_ref[k]).astype(F32)
            total = part if total is None else total + part
        o_ref[0] = total

    return pl.pallas_call(
        body, name=name, grid=(R // br,),
        in_specs=[pl.BlockSpec(memory_space=pltpu.SMEM), pl.BlockSpec((N_CHIPS, br, C), lambda i: (0, i, 0)),
                  pl.BlockSpec((br, C), lambda i: (i, 0)), ANY],
        out_specs=pl.BlockSpec((1, br, C), lambda i: (layer, i, 0)),
        out_shape=jax.ShapeDtypeStruct(acc.shape, F32), input_output_aliases={3: 0},
        compiler_params=_cparams("parallel"),
    )(me, land, own, acc)


def _adamw_math(w, g, m, v):
    m = ADAM_B1 * m + (1.0 - ADAM_B1) * g
    v = ADAM_B2 * v + (1.0 - ADAM_B2) * (g * g)
    m_hat = m / (1.0 - ADAM_B1 ** ADAM_STEP)
    v_hat = v / (1.0 - ADAM_B2 ** ADAM_STEP)
    return -ADAM_LR * (m_hat / (jnp.sqrt(v_hat) + ADAM_EPS) + ADAM_WD * w), m, v


def _adamw(w, m, v, g_a, g_b, *, br, name):
    R, C = w.shape
    two = g_b is not None

    def body(*refs):
        if two:
            w_ref, m_ref, v_ref, ga_ref, gb_ref, g_ref, d_ref, nm_ref, nv_ref = refs
            g = ga_ref[...] + gb_ref[...]
        else:
            w_ref, m_ref, v_ref, ga_ref, g_ref, d_ref, nm_ref, nv_ref = refs
            g = ga_ref[...]
        g_ref[...] = g
        d_ref[...], nm_ref[...], nv_ref[...] = _adamw_math(w_ref[...], g, m_ref[...], v_ref[...])

    spec = pl.BlockSpec((br, C), lambda i: (i, 0))
    args = [w, m, v, g_a] + ([g_b] if two else [])
    return pl.pallas_call(
        body, name=name, grid=(R // br,), in_specs=[spec] * len(args), out_specs=[spec] * 4,
        out_shape=[jax.ShapeDtypeStruct((R, C), F32)] * 4,
        compiler_params=_cparams("parallel"),
    )(*args)


SMALL_ROWS = dict(fox_b_f=1, mla_g_q=2, mla_g_kv=1, swa_sinks=1, mix_g=8, ln1_g=8, ln1_b=8, ln2_g=8, ln2_b=8)
SMALL_ROWS_PER_LAYER = sum(SMALL_ROWS.values())


def _pack_small(vals, extra_rows):
    L = vals[SMALL[0]].shape[0]
    per_layer = []
    for name in SMALL:
        a = vals[name].astype(F32)
        a = jnp.pad(a, ((0, 0), (0, SMALL_ROWS[name] * 128 - a.shape[1])))
        per_layer.append(a.reshape(L, SMALL_ROWS[name], 128))
    out = jnp.concatenate(per_layer, axis=1).reshape(L * SMALL_ROWS_PER_LAYER, 128)
    return jnp.pad(out, ((0, extra_rows), (0, 0)))


def _unpack_small(block, shapes):
    L = shapes[SMALL[0]][0]
    body = block[:L * SMALL_ROWS_PER_LAYER].reshape(L, SMALL_ROWS_PER_LAYER, 128)
    out, r = {}, 0
    for name in SMALL:
        n = shapes[name][1]
        out[name] = body[:, r:r + SMALL_ROWS[name]].reshape(L, SMALL_ROWS[name] * 128)[:, :n]
        r += SMALL_ROWS[name]
    return out


def _to_chips(g, axis):
    L, a, b = g.shape
    if axis == 2:
        return g.reshape(L, a, N_CHIPS, b // N_CHIPS).transpose(2, 0, 1, 3)
    return g.reshape(L, N_CHIPS, a // N_CHIPS, b).transpose(1, 0, 2, 3)


def kernel(x, w_in, fox_b_f, mla_g_q, mla_g_kv, mla_w_uq, mla_w_ukv, swa_sinks, mix_g, w_o, ln1_g, ln1_b, w_gate, w_up, w_down, ln2_g, ln2_b, loss_target, m_w_in, m_fox_b_f, m_mla_g_q, m_mla_g_kv, m_mla_w_uq, m_mla_w_ukv, m_swa_sinks, m_mix_g, m_w_o, m_ln1_g, m_ln1_b, m_w_gate, m_w_up, m_w_down, m_ln2_g, m_ln2_b, v_w_in, v_fox_b_f, v_mla_g_q, v_mla_g_kv, v_mla_w_uq, v_mla_w_ukv, v_swa_sinks, v_mix_g, v_w_o, v_ln1_g, v_ln1_b, v_w_gate, v_w_up, v_w_down, v_ln2_g, v_ln2_b):
    w = dict(w_in=w_in, fox_b_f=fox_b_f, mla_g_q=mla_g_q, mla_g_kv=mla_g_kv, mla_w_uq=mla_w_uq, mla_w_ukv=mla_w_ukv,
             swa_sinks=swa_sinks, mix_g=mix_g, w_o=w_o, ln1_g=ln1_g, ln1_b=ln1_b, w_gate=w_gate, w_up=w_up,
             w_down=w_down, ln2_g=ln2_g, ln2_b=ln2_b)
    m = dict(w_in=m_w_in, fox_b_f=m_fox_b_f, mla_g_q=m_mla_g_q, mla_g_kv=m_mla_g_kv, mla_w_uq=m_mla_w_uq,
             mla_w_ukv=m_mla_w_ukv, swa_sinks=m_swa_sinks, mix_g=m_mix_g, w_o=m_w_o, ln1_g=m_ln1_g, ln1_b=m_ln1_b,
             w_gate=m_w_gate, w_up=m_w_up, w_down=m_w_down, ln2_g=m_ln2_g, ln2_b=m_ln2_b)
    v = dict(w_in=v_w_in, fox_b_f=v_fox_b_f, mla_g_q=v_mla_g_q, mla_g_kv=v_mla_g_kv, mla_w_uq=v_mla_w_uq,
             mla_w_ukv=v_mla_w_ukv, swa_sinks=v_swa_sinks, mix_g=v_mix_g, w_o=v_w_o, ln1_g=v_ln1_g, ln1_b=v_ln1_b,
             w_gate=v_w_gate, w_up=v_w_up, w_down=v_w_down, ln2_g=v_ln2_g, ln2_b=v_ln2_b)
    names = tuple(w)
    L = w_in.shape[0]
    S = x.shape[1]
    blk = min(BLK_SOFTMAX, S)
    bs = min(512, S)

    me = 2 * lax.axis_index("x") + lax.axis_index("y")
    axis_of = {k: SHARD_AXIS[k] - 1 for k in BIG}
    groups = (("w_in", "mla_w_uq", "mla_w_ukv"), ("w_o", "w_gate", "w_up", "w_down"))

    started, last = [], None
    for l in range(L):
        per_group = []
        for g, group in enumerate(groups):
            srcs = [w[k][l].astype(MXU_DTYPE) for k in group]
            if last is not None:
                t = min(range(len(srcs)), key=lambda i: srcs[i].size)
                srcs[t] = srcs[t] + last["token"][0, 0].astype(MXU_DTYPE)
            last = _exchange_start(srcs, scatter=False, name=f"gather_start{l}_{g}")
            per_group.append(last)
        started.append(per_group)
    all_started = sum(st["token"] for per_group in started for st in per_group)

    def gathered(l, g, after):
        mine, lands = _exchange_wait(started[l][g], after, scatter=False, name=f"gather_wait{l}_{g}")
        shard = lambda t, k: jnp.where(me == k, mine[t], lands[t][k])
        whole = lambda t, axis: jnp.concatenate([shard(t, k) for k in range(N_CHIPS)], axis=axis)
        if g == 0:
            return _prep_weights_a(*[whole(t, axis_of[name]) for t, name in enumerate(groups[0])])
        gate_up = jnp.concatenate([shard(t, k) for t in (1, 2) for k in range(N_CHIPS)], axis=1)
        return dict(w_o=whole(0, 0), wgu=gate_up, w_down=whole(3, 0))

    def scatter(l, g, grads):
        to_owner = [_to_chips(grads[k].astype(MXU_DTYPE)[None], axis_of[k] + 1)[:, 0] for k in groups[g]]
        return _exchange_start(to_owner, scatter=True, name=f"scatter_start{l}_{g}")

    tabs = _rope_tables(S)
    Ps = []
    for l in range(L):
        P = dict(fox_b_f=fox_b_f[l], swa_sinks=swa_sinks[l])
        for k in ("mla_g_q", "mla_g_kv", "mix_g", "ln1_g", "ln1_b", "ln2_g", "ln2_b"):
            P[k] = w[k][l][None, :]
        Ps.append(P)

    xa = x[0]
    xb = xa.astype(MXU_DTYPE)
    xbT = xb.T
    saved, Ws = [], []
    for l in range(L):
        W = gathered(l, 0, all_started if l == 0 else xa)
        late = lambda after, l=l: gathered(l, 1, after)
        xa, xb, xbT, sv, W = _layer_fwd(l, xa, xb, xbT, W, Ps[l], tabs, blk, late_weights=late)
        saved.append(sv)
        Ws.append(W)
    dx, loss_part = _loss_head(xa, loss_target[0], bs=bs, name="loss_head")

    layer_grads = [None] * L
    sent = [[None, None] for _ in range(L)]
    pin = None
    for l in reversed(range(L)):
        P = Ps[l] if pin is None else dict(Ps[l], ln2_g=Ps[l]["ln2_g"] + pin[0, 0])

        def send_early(grads, l=l):
            sent[l][1] = scatter(l, 1, grads)
            return sent[l][1]["token"]

        dx, layer_grads[l] = _layer_bwd(l, dx, saved[l], Ws[l], P, tabs, blk, send_early=send_early)
        sent[l][0] = scatter(l, 0, layer_grads[l])
        pin = sent[l][0]["token"]
    grad_x = dx[None]

    me_arr = me.astype(jnp.int32)[None]
    partial = {k: jnp.zeros(w[k].shape, F32) for k in BIG}
    after = dx
    for l in reversed(range(L)):
        for g in (1, 0):
            mine, lands = _exchange_wait(sent[l][g], after, scatter=True, name=f"scatter_wait{l}_{g}")
            for t, k in enumerate(groups[g]):
                own = lax.dynamic_index_in_dim(mine[t], me, 0, keepdims=False)
                partial[k] = _sum_chips_into(partial[k], lands[t], own, me_arr, l, br=_rows(own.shape[0]),
                                             name=f"sum_{k}_l{l}")
            after = partial[groups[g][-1]]
    partial = [partial[k] for k in BIG]
    sibling = _core_exchange(partial, name="swap_partials")
    local = {k: jnp.stack([layer_grads[l][k] for l in range(L)]) for k in SMALL}
    out = {}
    for k, mine, theirs in zip(BIG, partial, sibling):
        shp = w[k].shape
        two_d = lambda a: a.reshape(shp[0] * shp[1], shp[2])
        res = _adamw(two_d(w[k]), two_d(m[k]), two_d(v[k]), two_d(mine), two_d(theirs), br=_rows(shp[0] * shp[1]),
                     name=f"adamw_{k}")
        out[k] = [a.reshape(shp) for a in res]

    shapes = {k: w[k].shape for k in SMALL}
    extra = 8 + (-L * SMALL_ROWS_PER_LAYER) % 8
    block = _pack_small({k: local[k] for k in SMALL}, extra)
    block = block.at[L * SMALL_ROWS_PER_LAYER, 0].set(loss_part[0, 0])
    total = _all_sum_small(block, name="sum_small")
    loss = total[L * SMALL_ROWS_PER_LAYER, 0]
    res = _adamw(_pack_small({k: w[k] for k in SMALL}, extra), _pack_small({k: m[k] for k in SMALL}, extra),
                 _pack_small({k: v[k] for k in SMALL}, extra), total, None, br=total.shape[0], name="adamw_small")
    res = [_unpack_small(t, shapes) for t in res]
    for k in SMALL:
        out[k] = [r[k] for r in res]

    return (loss, grad_x, *[out[k][0] for k in names], *[out[k][1] for k in names],
            *[out[k][2] for k in names], *[out[k][3] for k in names])


def _rows(n):
    for b in (256, 128, 64, 32, 16, 8):
        if n % b == 0:
            return b
    return n
```

```python
import functools

import numpy as np
import jax
import jax.numpy as jnp
from jax import lax
from jax.experimental import pallas as pl
from jax.experimental.pallas import tpu as pltpu

F32 = jnp.float32
MXU_DTYPE = jnp.bfloat16
NEG_INF = -1e30

D_MODEL = 1024
DEPTH = 4
HEAD_DIM = 64
GROUP_WIDTH = 256
D_FF = 2816
MLA_Q_RANK = 256
MLA_KV_RANK = 128
MLA_ROPE = 32
MLA_QK = 96
MLA_PAD = 128
ROPE_THETA = 10000.0
WINDOW = 128
ALPHA = (2.0 * DEPTH) ** 0.25
SWA_SLOPES = tuple(float(2.0 ** (-8.0 * h / 4)) for h in range(1, 5))
ATT_W = 2048
LAT_W = 640
PERM_W = ATT_W + LAT_W
COL_FQ, COL_FK, COL_FV = 0, 256, 512
COL_SQ, COL_SK, COL_SV = 768, 1024, 1280
COL_WQ, COL_WK, COL_WV = 1536, 1792, 1920
Q_COLSCALE = np.ones((1, ATT_W), np.float32)
Q_COLSCALE[:, COL_FQ:COL_FQ + 256] = HEAD_DIM ** -0.5
Q_COLSCALE[:, COL_SQ:COL_SQ + 256] = HEAD_DIM ** -0.5

ADAM_LR, ADAM_B1, ADAM_B2, ADAM_EPS, ADAM_WD, ADAM_STEP = 0.001, 0.9, 0.999, 1e-08, 0.01, 10

VMEM_LIMIT = 56 * 1024 * 1024
NT = (((1,), (1,)), ((), ()))
TN = (((0,), (0,)), ((), ()))
MESH = pl.DeviceIdType.MESH


def _cparams(*sem):
    return pltpu.CompilerParams(dimension_semantics=sem, vmem_limit_bytes=VMEM_LIMIT)


def _dot01(x, m01, dn=None, parts=2):
    acc = None
    rem = x
    for _ in range(parts):
        part = rem.astype(MXU_DTYPE)
        rem = rem - part.astype(F32)
        if dn is None:
            t = jnp.dot(part, m01, preferred_element_type=F32)
        else:
            t = lax.dot_general(part, m01, dn, preferred_element_type=F32)
        acc = t if acc is None else acc + t
    return acc


def _mm(a, b, *, name, ta=False, tb=False, out_dtype=F32, bm=512, bn=512, bk=512, resid=None, alpha=1.0,
        colscale=None):
    M, K = (a.shape[1], a.shape[0]) if ta else a.shape
    N = b.shape[0] if tb else b.shape[1]
    assert (b.shape[1] if tb else b.shape[0]) == K
    assert resid is None or colscale is None
    bm, bn, bk = min(bm, M), min(bn, N), min(bk, K)
    assert M % bm == 0 and N % bn == 0 and K % bk == 0, (name, M, N, K, bm, bn, bk)
    nk = K // bk
    assert nk == 1 or (out_dtype == F32 and colscale is None), name
    dn = (((0 if ta else 1,), (1 if tb else 0,)), ((), ()))

    extra = resid is not None or colscale is not None

    def body(*refs):
        a_ref, b_ref = refs[:2]
        r_ref = refs[2] if extra else None
        o_ref = refs[3] if extra else refs[2]
        k = pl.program_id(2)

        def first():
            r = lax.dot_general(a_ref[...].astype(MXU_DTYPE), b_ref[...].astype(MXU_DTYPE), dn,
                                preferred_element_type=F32)
            if resid is not None:
                r = r + alpha * r_ref[...]
            if colscale is not None:
                r = r * r_ref[...]
            o_ref[...] = r.astype(o_ref.dtype)

        if nk == 1:
            first()
        else:
            pl.when(k == 0)(first)

            @pl.when(k > 0)
            def _():
                o_ref[...] += lax.dot_general(a_ref[...].astype(MXU_DTYPE), b_ref[...].astype(MXU_DTYPE), dn,
                                              preferred_element_type=F32)

    a_spec = pl.BlockSpec((bk, bm), lambda i, j, k: (k, i)) if ta else pl.BlockSpec((bm, bk), lambda i, j, k: (i, k))
    b_spec = pl.BlockSpec((bn, bk), lambda i, j, k: (j, k)) if tb else pl.BlockSpec((bk, bn), lambda i, j, k: (k, j))
    in_specs = [a_spec, b_spec]
    args = [a, b]
    if resid is not None:
        in_specs.append(pl.BlockSpec((bm, bn), lambda i, j, k: (i, j)))
        args.append(resid)
    if colscale is not None:
        in_specs.append(pl.BlockSpec((1, bn), lambda i, j, k: (0, j)))
        args.append(colscale)
    return pl.pallas_call(
        body, name=name, grid=(M // bm, N // bn, nk), in_specs=in_specs,
        out_specs=pl.BlockSpec((bm, bn), lambda i, j, k: (i, j)),
        out_shape=jax.ShapeDtypeStruct((M, N), out_dtype),
        compiler_params=_cparams("parallel", "parallel", "arbitrary"),
    )(*args)


HP = 4
BLK_SOFTMAX = 512
BLK_STICK = 256


def _kv_blocks_t(a, blk):
    S, C = a.shape
    return a.reshape(S // blk, blk, C).transpose(0, 2, 1)


def _smax_fwd_t(qT, k, vT3, *, dk, blk, name):
    S = k.shape[0]
    nb = S // blk
    H = k.shape[1] // dk

    def body(qT_ref, k_ref, vT_ref, oT_ref, lse_ref):
        i = pl.program_id(1)
        key = lax.broadcasted_iota(jnp.int32, (blk, blk), 0)
        qry = lax.broadcasted_iota(jnp.int32, (blk, blk), 1)
        qs = [qT_ref[h * dk:(h + 1) * dk, :] for h in range(HP)]

        def tile(j, carry, masked):
            r0 = pl.multiple_of(j * blk, blk)
            ss = [jnp.dot(k_ref[pl.ds(r0, blk), h * dk:(h + 1) * dk], qs[h], preferred_element_type=F32)
                  for h in range(HP)]
            stats, pes = [], []
            for h in range(HP):
                m, l, _ = carry[h]
                s = jnp.where(key <= qry, ss[h], NEG_INF) if masked else ss[h]
                mn = jnp.maximum(m, jnp.max(s, axis=0, keepdims=True))
                a = jnp.exp(m - mn)
                pe = jnp.exp(s - mn)
                stats.append((mn, a * l + jnp.sum(pe, axis=0, keepdims=True), a))
                pes.append(pe.astype(MXU_DTYPE))
            pvs = [jnp.dot(vT_ref[j, h * HEAD_DIM:(h + 1) * HEAD_DIM, :], pes[h], preferred_element_type=F32)
                   for h in range(HP)]
            return tuple((stats[h][0], stats[h][1], stats[h][2] * carry[h][2] + pvs[h]) for h in range(HP))

        init = tuple((jnp.full((1, blk), NEG_INF, F32), jnp.zeros((1, blk), F32), jnp.zeros((HEAD_DIM, blk), F32))
                     for _ in range(HP))
        carry = lax.fori_loop(0, i, functools.partial(tile, masked=False), init)
        carry = tile(i, carry, True)
        for h in range(HP):
            m, l, acc = carry[h]
            oT_ref[h * HEAD_DIM:(h + 1) * HEAD_DIM, :] = acc / l
            lse_ref[h, 0] = m + jnp.log(l)

    return pl.pallas_call(
        body, name=name, grid=(H // HP, nb),
        in_specs=[pl.BlockSpec((HP * dk, blk), lambda p, i: (p, i)),
                  pl.BlockSpec((S, HP * dk), lambda p, i: (0, p)),
                  pl.BlockSpec((nb, HP * HEAD_DIM, blk), lambda p, i: (0, p, 0))],
        out_specs=[pl.BlockSpec((HP * HEAD_DIM, blk), lambda p, i: (p, i)),
                   pl.BlockSpec((HP, 1, 1, blk), lambda p, i: (p, i, 0, 0))],
        out_shape=[jax.ShapeDtypeStruct((H * HEAD_DIM, S), F32), jax.ShapeDtypeStruct((H, nb, 1, blk), F32)],
        compiler_params=_cparams("arbitrary", "arbitrary"),
    )(qT, k, vT3)


def _smax_bwd_t(qT, q, k, kT3, v, dmix, dmixT, oT, lse, *, dk, dcb, qscale, blk, name):
    S = k.shape[0]
    nb = S // blk
    H = k.shape[1] // dk
    hd = HP * HEAD_DIM
    dcr = dcb * 128 // hd

    def body(qT_ref, q_ref, k_ref, kT_ref, v_ref, do_ref, doT_ref, oT_ref, lse_ref, dqT_ref, dk_ref, dv_ref):
        i = pl.program_id(1)

        @pl.when(i == 0)
        def _():
            dk_ref[...] = jnp.zeros_like(dk_ref)
            dv_ref[...] = jnp.zeros_like(dv_ref)

        key = lax.broadcasted_iota(jnp.int32, (blk, blk), 0)
        qry = lax.broadcasted_iota(jnp.int32, (blk, blk), 1)
        per_head = []
        for h in range(HP):
            hs = slice(h * HEAD_DIM, (h + 1) * HEAD_DIM)
            doT = doT_ref[hs, :]
            per_head.append(dict(
                qT=qT_ref[h * dk:(h + 1) * dk, :], q=q_ref[:, h * dk:(h + 1) * dk],
                doT=doT.astype(MXU_DTYPE), do=do_ref[:, hs].astype(MXU_DTYPE),
                delta=jnp.sum(doT * oT_ref[hs, :], axis=0, keepdims=True), lse=lse_ref[h, 0]))

        def tile(j, dqs, masked):
            r0 = pl.multiple_of(j * blk, blk)
            rows = pl.ds(r0, blk)
            ksl = [slice(h * dk, (h + 1) * dk) for h in range(HP)]
            hsl = [slice(h * HEAD_DIM, (h + 1) * HEAD_DIM) for h in range(HP)]
            ss = [jnp.dot(k_ref[rows, ksl[h]], per_head[h]["qT"], preferred_element_type=F32) for h in range(HP)]
            dps = [jnp.dot(v_ref[rows, hsl[h]], per_head[h]["doT"], preferred_element_type=F32) for h in range(HP)]
            prs, dss = [], []
            for h in range(HP):
                c = per_head[h]
                s = jnp.where(key <= qry, ss[h], NEG_INF) if masked else ss[h]
                pr = jnp.exp(s - c["lse"])
                dss.append((pr * (dps[h] - c["delta"])).astype(MXU_DTYPE))
                prs.append(pr.astype(MXU_DTYPE))
            for h in range(HP):
                dv_ref[rows, hsl[h]] += jnp.dot(prs[h], per_head[h]["do"], preferred_element_type=F32)
            for h in range(HP):
                dk_ref[rows, ksl[h]] += jnp.dot(dss[h], per_head[h]["q"], preferred_element_type=F32)
            return tuple(dqs[h] + jnp.dot(kT_ref[j, ksl[h], :], dss[h], preferred_element_type=F32) for h in range(HP))

        dqs = lax.fori_loop(0, i, functools.partial(tile, masked=False),
                            tuple(jnp.zeros((dk, blk), F32) for _ in range(HP)))
        dqs = tile(i, dqs, True)
        for h in range(HP):
            dqT_ref[h * dk:(h + 1) * dk, :] = dqs[h] * qscale

    return pl.pallas_call(
        body, name=name, grid=(H // HP, nb),
        in_specs=[pl.BlockSpec((HP * dk, blk), lambda p, i: (p, i)),
                  pl.BlockSpec((blk, HP * dk), lambda p, i: (i, p)),
                  pl.BlockSpec((S, HP * dk), lambda p, i: (0, p)),
                  pl.BlockSpec((nb, HP * dk, blk), lambda p, i: (0, p, 0)),
                  pl.BlockSpec((S, hd), lambda p, i: (0, p)),
                  pl.BlockSpec((blk, hd), lambda p, i: (i, dcr + p)),
                  pl.BlockSpec((hd, blk), lambda p, i: (dcr + p, i)),
                  pl.BlockSpec((hd, blk), lambda p, i: (p, i)),
                  pl.BlockSpec((HP, 1, 1, blk), lambda p, i: (p, i, 0, 0))],
        out_specs=[pl.BlockSpec((HP * dk, blk), lambda p, i: (p, i)),
                   pl.BlockSpec((S, HP * dk), lambda p, i: (0, p)),
                   pl.BlockSpec((S, hd), lambda p, i: (0, p))],
        out_shape=[jax.ShapeDtypeStruct((H * dk, S), F32), jax.ShapeDtypeStruct((S, H * dk), F32),
                   jax.ShapeDtypeStruct((S, H * HEAD_DIM), F32)],
        compiler_params=_cparams("arbitrary", "arbitrary"),
    )(qT, q, k, kT3, v, dmix, dmixT, oT, lse)


def _log1m_beta(z):
    return -(jnp.maximum(z, 0.0) + jnp.log(1.0 + jnp.exp(-jnp.abs(z))))


def _dot01_left(m01, x, parts=2):
    acc = None
    rem = x
    for _ in range(parts):
        part = rem.astype(MXU_DTYPE)
        rem = rem - part.astype(F32)
        t = jnp.dot(m01, part, preferred_element_type=F32)
        acc = t if acc is None else acc + t
    return acc


def _sb_fwd_t(qT, h_att, vT3, *, blk, name):
    S = h_att.shape[0]
    nb = S // blk
    kcb = COL_SK // (HP * HEAD_DIM)

    def body(qT_ref, k_ref, vT_ref, oT_ref, lt_ref):
        i = pl.program_id(1)
        key = lax.broadcasted_iota(jnp.int32, (blk, blk), 0)
        qry = lax.broadcasted_iota(jnp.int32, (blk, blk), 1)
        strict = key < qry
        later = (qry > key).astype(MXU_DTYPE)
        qs = [qT_ref[h * HEAD_DIM:(h + 1) * HEAD_DIM, :] for h in range(HP)]

        def tile(j, carry, mask):
            r0 = pl.multiple_of(j * blk, blk)
            hsl = [slice(h * HEAD_DIM, (h + 1) * HEAD_DIM) for h in range(HP)]
            zs = [jnp.dot(k_ref[pl.ds(r0, blk), hsl[h]], qs[h], preferred_element_type=F32) for h in range(HP)]
            lbs = []
            for h in range(HP):
                lb = _log1m_beta(zs[h])
                lbs.append(lb if mask is None else jnp.where(mask, lb, 0.0))
            sums = [_dot01_left(later, lbs[h]) for h in range(HP)]
            probs = []
            for h in range(HP):
                lt_ref[h, 0, j] = carry[h][0]
                a = jnp.exp(zs[h] + lbs[h] + sums[h] + carry[h][0])
                probs.append((a if mask is None else jnp.where(mask, a, 0.0)).astype(MXU_DTYPE))
            pvs = [jnp.dot(vT_ref[j, hsl[h], :], probs[h], preferred_element_type=F32) for h in range(HP)]
            return tuple((carry[h][0] + jnp.sum(lbs[h], axis=0, keepdims=True), carry[h][1] + pvs[h]) for h in range(HP))

        init = tuple((jnp.zeros((1, blk), F32), jnp.zeros((HEAD_DIM, blk), F32)) for _ in range(HP))
        carry = tile(i, init, strict)
        carry = lax.fori_loop(0, i, lambda jj, c: tile(i - 1 - jj, c, None), carry)
        for h in range(HP):
            oT_ref[h * HEAD_DIM:(h + 1) * HEAD_DIM, :] = carry[h][1]

    hd = HP * HEAD_DIM
    return pl.pallas_call(
        body, name=name, grid=(4 // HP, nb),
        in_specs=[pl.BlockSpec((hd, blk), lambda p, i: (p, i)),
                  pl.BlockSpec((S, hd), lambda p, i: (0, kcb + p)),
                  pl.BlockSpec((nb, hd, blk), lambda p, i: (0, p, 0))],
        out_specs=[pl.BlockSpec((hd, blk), lambda p, i: (p, i)),
                   pl.BlockSpec((HP, 1, nb, 1, blk), lambda p, i: (p, i, 0, 0, 0))],
        out_shape=[jax.ShapeDtypeStruct((GROUP_WIDTH, S), F32), jax.ShapeDtypeStruct((4, nb, nb, 1, blk), F32)],
        compiler_params=_cparams("arbitrary", "arbitrary"),
    )(qT, h_att, vT3)


def _sb_bwd_t(qT, h_att, kT3, dmix, dmixT, later_sums, *, dcb, qscale, blk, name):
    S = h_att.shape[0]
    nb = S // blk
    hd = HP * HEAD_DIM
    qcb, kcb, vcb = COL_SQ // hd, COL_SK // hd, COL_SV // hd
    dcr = dcb * 128 // hd

    def body(qT_ref, q_ref, k_ref, kT_ref, v_ref, do_ref, doT_ref, lt_ref, dqT_ref, dk_ref, dv_ref):
        i = pl.program_id(1)

        @pl.when(i == 0)
        def _():
            dk_ref[...] = jnp.zeros_like(dk_ref)
            dv_ref[...] = jnp.zeros_like(dv_ref)

        key = lax.broadcasted_iota(jnp.int32, (blk, blk), 0)
        qry = lax.broadcasted_iota(jnp.int32, (blk, blk), 1)
        strict = key < qry
        later = (qry > key).astype(MXU_DTYPE)
        before = (qry < key).astype(MXU_DTYPE)
        per_head = []
        for h in range(HP):
            hs = slice(h * HEAD_DIM, (h + 1) * HEAD_DIM)
            per_head.append(dict(qT=qT_ref[hs, :], q=q_ref[:, hs], doT=doT_ref[hs, :].astype(MXU_DTYPE),
                                 do=do_ref[:, hs].astype(MXU_DTYPE)))

        def tile(j, carry, mask):
            r0 = pl.multiple_of(j * blk, blk)
            rows = pl.ds(r0, blk)
            hsl = [slice(h * HEAD_DIM, (h + 1) * HEAD_DIM) for h in range(HP)]
            zs = [jnp.dot(k_ref[rows, hsl[h]], per_head[h]["qT"], preferred_element_type=F32) for h in range(HP)]
            das = [jnp.dot(v_ref[rows, hsl[h]], per_head[h]["doT"], preferred_element_type=F32) for h in range(HP)]
            lbs = []
            for h in range(HP):
                lb = _log1m_beta(zs[h])
                lbs.append(lb if mask is None else jnp.where(mask, lb, 0.0))
            sums = [_dot01_left(later, lbs[h]) for h in range(HP)]
            probs, gs = [], []
            for h in range(HP):
                a = jnp.exp(zs[h] + lbs[h] + sums[h] + lt_ref[h, 0, j])
                a = a if mask is None else jnp.where(mask, a, 0.0)
                gs.append(das[h] * a)
                probs.append(a.astype(MXU_DTYPE))
            for h in range(HP):
                dv_ref[rows, hsl[h]] += jnp.dot(probs[h], per_head[h]["do"], preferred_element_type=F32)
            es = [_dot01_left(before, gs[h]) for h in range(HP)]
            dzs = []
            for h in range(HP):
                dz = gs[h] * jnp.exp(lbs[h]) - (carry[h][0] + es[h]) * jnp.exp(zs[h] + lbs[h])
                dzs.append((dz if mask is None else jnp.where(mask, dz, 0.0)).astype(MXU_DTYPE))
            for h in range(HP):
                dk_ref[rows, hsl[h]] += jnp.dot(dzs[h], per_head[h]["q"], preferred_element_type=F32)
            return tuple((carry[h][0] + jnp.sum(gs[h], axis=0, keepdims=True),
                          carry[h][1] + jnp.dot(kT_ref[j, hsl[h], :], dzs[h], preferred_element_type=F32))
                         for h in range(HP))

        init = tuple((jnp.zeros((1, blk), F32), jnp.zeros((HEAD_DIM, blk), F32)) for _ in range(HP))
        carry = lax.fori_loop(0, i, lambda j, c: tile(j, c, None), init)
        carry = tile(i, carry, strict)
        for h in range(HP):
            dqT_ref[h * HEAD_DIM:(h + 1) * HEAD_DIM, :] = carry[h][1] * qscale

    return pl.pallas_call(
        body, name=name, grid=(4 // HP, nb),
        in_specs=[pl.BlockSpec((hd, blk), lambda p, i: (p, i)),
                  pl.BlockSpec((blk, hd), lambda p, i: (i, qcb + p)),
                  pl.BlockSpec((S, hd), lambda p, i: (0, kcb + p)),
                  pl.BlockSpec((nb, hd, blk), lambda p, i: (0, p, 0)),
                  pl.BlockSpec((S, hd), lambda p, i: (0, vcb + p)),
                  pl.BlockSpec((blk, hd), lambda p, i: (i, dcr + p)),
                  pl.BlockSpec((hd, blk), lambda p, i: (dcr + p, i)),
                  pl.BlockSpec((HP, 1, nb, 1, blk), lambda p, i: (p, i, 0, 0, 0))],
        out_specs=[pl.BlockSpec((hd, blk), lambda p, i: (p, i)),
                   pl.BlockSpec((S, hd), lambda p, i: (0, p)),
                   pl.BlockSpec((S, hd), lambda p, i: (0, p))],
        out_shape=[jax.ShapeDtypeStruct((GROUP_WIDTH, S), F32), jax.ShapeDtypeStruct((S, GROUP_WIDTH), F32),
                   jax.ShapeDtypeStruct((S, GROUP_WIDTH), F32)],
        compiler_params=_cparams("arbitrary", "arbitrary"),
    )(qT, h_att, h_att, kT3, h_att, dmix, dmixT, later_sums)


SWA_SUB = 4


def _swa_sub_blocks(S):
    return min(SWA_SUB, S // WINDOW)


def _swa_tiles(n, sub):
    tiles = []
    for b in range(sub):
        start = pl.multiple_of(jnp.maximum(n * sub + b - 1, 0) * WINDOW, WINDOW)
        tiles += [(b, h, slice(b * WINDOW, (b + 1) * WINDOW), start) for h in range(4)]
    return tiles


def _swa_scores(q_ref, k_ref, n, sub, tile):
    b, h, qrows, start = tile
    g = h // 2
    kb = k_ref[pl.ds(start, 2 * WINDOW), g * HEAD_DIM:(g + 1) * HEAD_DIM]
    s = lax.dot_general(q_ref[qrows, h * HEAD_DIM:(h + 1) * HEAD_DIM], kb, NT,
                        preferred_element_type=F32) * (HEAD_DIM ** -0.5)
    dist = ((n * sub + b) * WINDOW + lax.broadcasted_iota(jnp.int32, (WINDOW, 2 * WINDOW), 0)
            - start - lax.broadcasted_iota(jnp.int32, (WINDOW, 2 * WINDOW), 1))
    s = s - SWA_SLOPES[h] * dist.astype(F32)
    valid = (dist >= 0) & (dist < WINDOW)
    return jnp.where(valid, s, NEG_INF), kb


def _swa_fwd(h_att, sinks, *, name):
    S = h_att.shape[0]
    sub = _swa_sub_blocks(S)
    rows = sub * WINDOW
    qcb, kcb, vcb = COL_WQ // 256, COL_WK // 128, COL_WV // 128

    def body(sink_ref, q_ref, k_ref, v_ref, o_ref, lse_ref):
        n = pl.program_id(0)
        tiles = _swa_tiles(n, sub)
        scores = [_swa_scores(q_ref, k_ref, n, sub, t)[0] for t in tiles]
        probs = []
        for (b, h, qrows, start), s in zip(tiles, scores):
            sink = sink_ref[h]
            m = jnp.maximum(jnp.max(s, axis=1, keepdims=True), sink)
            e = jnp.exp(s - m)
            den = jnp.sum(e, axis=1, keepdims=True) + jnp.exp(sink - m)
            probs.append((e / den).astype(MXU_DTYPE))
            lse_ref[h, qrows] = m + jnp.log(den)
        for (b, h, qrows, start), p in zip(tiles, probs):
            vb = v_ref[pl.ds(start, 2 * WINDOW), (h // 2) * HEAD_DIM:(h // 2 + 1) * HEAD_DIM]
            o_ref[qrows, h * HEAD_DIM:(h + 1) * HEAD_DIM] = jnp.dot(p, vb, preferred_element_type=F32)

    return pl.pallas_call(
        body, name=name, grid=(S // rows,),
        in_specs=[pl.BlockSpec(memory_space=pltpu.SMEM),
                  pl.BlockSpec((rows, 256), lambda n: (n, qcb)),
                  pl.BlockSpec((S, 128), lambda n: (0, kcb)),
                  pl.BlockSpec((S, 128), lambda n: (0, vcb))],
        out_specs=[pl.BlockSpec((rows, 256), lambda n: (n, 0)), pl.BlockSpec((4, rows, 1), lambda n: (0, n, 0))],
        out_shape=[jax.ShapeDtypeStruct((S, GROUP_WIDTH), F32), jax.ShapeDtypeStruct((4, S, 1), F32)],
        compiler_params=_cparams("arbitrary"),
    )(sinks, h_att, h_att, h_att)


def _swa_bwd(h_att, sinks, dmix, o_arr, lse, *, dcb, name):
    S = h_att.shape[0]
    sub = _swa_sub_blocks(S)
    rows = sub * WINDOW
    qcb, kcb, vcb = COL_WQ // 256, COL_WK // 128, COL_WV // 128

    def body(sink_ref, q_ref, k_ref, v_ref, do_ref, o_ref, lse_ref, dq_ref, dk_ref, dv_ref, dsink_ref):
        n = pl.program_id(0)

        @pl.when(n == 0)
        def _():
            dk_ref[...] = jnp.zeros_like(dk_ref)
            dv_ref[...] = jnp.zeros_like(dv_ref)
            dsink_ref[...] = jnp.zeros_like(dsink_ref)

        tiles = _swa_tiles(n, sub)
        hsl = [slice(h * HEAD_DIM, (h + 1) * HEAD_DIM) for h in range(4)]
        gsl = [slice(g * HEAD_DIM, (g + 1) * HEAD_DIM) for g in range(2)]
        scale = HEAD_DIM ** -0.5
        sk = [_swa_scores(q_ref, k_ref, n, sub, t) for t in tiles]
        dobs = [do_ref[qrows, hsl[h]].astype(MXU_DTYPE) for b, h, qrows, start in tiles]
        dps = [lax.dot_general(dob, v_ref[pl.ds(start, 2 * WINDOW), gsl[h // 2]], NT, preferred_element_type=F32)
               for (b, h, qrows, start), dob in zip(tiles, dobs)]
        prs, dss = [], []
        for t, (b, h, qrows, start) in enumerate(tiles):
            lse_h = lse_ref[h, qrows]
            pr = jnp.exp(sk[t][0] - lse_h)
            delta = jnp.sum(do_ref[qrows, hsl[h]] * o_ref[qrows, hsl[h]], axis=1, keepdims=True)
            dss.append((pr * (dps[t] - delta)).astype(MXU_DTYPE))
            prs.append(pr.astype(MXU_DTYPE))
            dsink_ref[h:h + 1, :] += jnp.zeros((1, 128), F32) - jnp.sum(jnp.exp(sink_ref[h] - lse_h) * delta)
        for t, (b, h, qrows, start) in enumerate(tiles):
            dq_ref[qrows, hsl[h]] = jnp.dot(dss[t], sk[t][1], preferred_element_type=F32) * scale
        for b in range(sub):
            for g in range(2):
                t0, t1 = 4 * b + 2 * g, 4 * b + 2 * g + 1
                qrows, krows = tiles[t0][2], pl.ds(tiles[t0][3], 2 * WINDOW)
                dk_ref[krows, gsl[g]] += (
                    lax.dot_general(dss[t0], q_ref[qrows, hsl[2 * g]], TN, preferred_element_type=F32)
                    + lax.dot_general(dss[t1], q_ref[qrows, hsl[2 * g + 1]], TN, preferred_element_type=F32)) * scale
                dv_ref[krows, gsl[g]] += (lax.dot_general(prs[t0], dobs[t0], TN, preferred_element_type=F32)
                                          + lax.dot_general(prs[t1], dobs[t1], TN, preferred_element_type=F32))

    return pl.pallas_call(
        body, name=name, grid=(S // rows,),
        in_specs=[pl.BlockSpec(memory_space=pltpu.SMEM),
                  pl.BlockSpec((rows, 256), lambda n: (n, qcb)),
                  pl.BlockSpec((S, 128), lambda n: (0, kcb)),
                  pl.BlockSpec((S, 128), lambda n: (0, vcb)),
                  pl.BlockSpec((rows, 256), lambda n: (n, dcb)),
                  pl.BlockSpec((rows, 256), lambda n: (n, 0)),
                  pl.BlockSpec((4, rows, 1), lambda n: (0, n, 0))],
        out_specs=[pl.BlockSpec((rows, 256), lambda n: (n, 0)),
                   pl.BlockSpec((S, 128), lambda n: (0, 0)),
                   pl.BlockSpec((S, 128), lambda n: (0, 0)),
                   pl.BlockSpec((4, 128), lambda n: (0, 0))],
        out_shape=[jax.ShapeDtypeStruct((S, GROUP_WIDTH), F32), jax.ShapeDtypeStruct((S, 128), F32),
                   jax.ShapeDtypeStruct((S, 128), F32), jax.ShapeDtypeStruct((4, 128), F32)],
        compiler_params=_cparams("arbitrary"),
    )(sinks, h_att, h_att, h_att, dmix, o_arr, lse)


def _tri(n, incl, upper):
    r = lax.broadcasted_iota(jnp.int32, (n, n), 0)
    c = lax.broadcasted_iota(jnp.int32, (n, n), 1)
    if upper:
        m = (r <= c) if incl else (r < c)
    else:
        m = (r >= c) if incl else (r > c)
    return m.astype(MXU_DTYPE)


def _fox_gate_fwd(fg, b_f, *, name):
    _, R, _ = fg.shape

    def body(b_ref, fg_ref, pos_ref, neg_ref):
        up_incl = _tri(128, True, True)
        ones = jnp.ones((128, 128), MXU_DTYPE)
        for h in range(4):
            z = fg_ref[h] + b_ref[h]
            logf = jnp.minimum(z, 0.0) - jnp.log(1.0 + jnp.exp(-jnp.abs(z)))
            within = _dot01(logf, up_incl, parts=3)
            totals = _dot01(logf, ones, parts=3)
            rem = within + _rows_other(totals, R, after=False)
            for part in range(3):
                piece = rem.astype(MXU_DTYPE)
                rem = rem - piece.astype(F32)
                pos_ref[h, part] = piece
                neg_ref[h, part] = -piece

    shape = (4, 3) + fg.shape[1:]
    return pl.pallas_call(
        body, name=name,
        in_specs=[pl.BlockSpec(memory_space=pltpu.SMEM), pl.BlockSpec(memory_space=pltpu.VMEM)],
        out_specs=[pl.BlockSpec(memory_space=pltpu.VMEM)] * 2,
        out_shape=[jax.ShapeDtypeStruct(shape, MXU_DTYPE)] * 2,
    )(b_f, fg)


def _rows_other(totals, n, after):
    r = lax.broadcasted_iota(jnp.int32, (n, n), 0)
    c = lax.broadcasted_iota(jnp.int32, (n, n), 1)
    m = ((c > r) if after else (c < r)).astype(MXU_DTYPE)
    acc = None
    rem = totals
    for _ in range(3):
        part = rem.astype(MXU_DTYPE)
        rem = rem - part.astype(F32)
        t = jnp.dot(m, part, preferred_element_type=F32)
        acc = t if acc is None else acc + t
    return acc


def _fox_gate_bwd(fg, b_f, dcum_k, dcum_q, *, q_unscale, name):
    _, R, _ = fg.shape

    def body(b_ref, fg_ref, dck_ref, dcq_ref, dfg_ref, db_ref):
        low_incl = _tri(128, True, False)
        ones = jnp.ones((128, 128), MXU_DTYPE)
        for h in range(4):
            dc = dcq_ref[h] * q_unscale - dck_ref[h]
            dlogf = _dot01(dc, low_incl, parts=3) + _rows_other(_dot01(dc, ones, parts=3), R, after=True)
            z = fg_ref[h] + b_ref[h]
            dz = dlogf * jnp.exp(jnp.minimum(-z, 0.0) - jnp.log(1.0 + jnp.exp(-jnp.abs(z))))
            dfg_ref[h] = dz
            db_ref[h:h + 1, :] = jnp.zeros((1, 128), F32) + jnp.sum(dz)

    return pl.pallas_call(
        body, name=name,
        in_specs=[pl.BlockSpec(memory_space=pltpu.SMEM)] + [pl.BlockSpec(memory_space=pltpu.VMEM)] * 3,
        out_specs=[pl.BlockSpec(memory_space=pltpu.VMEM), pl.BlockSpec(memory_space=pltpu.VMEM)],
        out_shape=[jax.ShapeDtypeStruct(fg.shape, F32), jax.ShapeDtypeStruct((4, 128), F32)],
    )(b_f, fg, dcum_k, dcum_q)


def _rope_rot(transpose):
    r = lax.broadcasted_iota(jnp.int32, (MLA_PAD, MLA_PAD), 0)
    c = lax.broadcasted_iota(jnp.int32, (MLA_PAD, MLA_PAD), 1)
    if transpose:
        r, c = c, r
    half = MLA_ROPE // 2
    lo, mid, hi = HEAD_DIM, HEAD_DIM + half, HEAD_DIM + MLA_ROPE
    minus = (c >= lo) & (c < mid) & (r == c + half)
    plus = (c >= mid) & (c < hi) & (r == c - half)
    return jnp.where(plus, 1.0, jnp.where(minus, -1.0, 0.0)).astype(MXU_DTYPE)


def _rope_lanes():
    lane = lax.broadcasted_iota(jnp.int32, (1, MLA_PAD), 1)
    return ((lane >= HEAD_DIM) & (lane < HEAD_DIM + MLA_ROPE)).astype(F32)


def _rms(x, g, eps=1e-6):
    r = lax.rsqrt(jnp.mean(x * x, axis=-1, keepdims=True) + eps)
    return x * r * g, r


def _rms_bwd(dy, x, r, g):
    xh = x * r
    dxh = dy * g
    dx = r * (dxh - xh * jnp.mean(dxh * xh, axis=-1, keepdims=True))
    return dx, dy * xh


def _mla_prep_fwd(lat, g_q, g_kv, wuq, wuk, wuv, cosm, sinm, *, bs, name):
    S = lat.shape[0]

    def body(lat_ref, gq_ref, gkv_ref, wuq_ref, wuk_ref, wuv_ref, cos_ref, sin_ref,
             q_ref, k_ref, v_ref, qT_ref, kT_ref, vT_ref):
        rot = _rope_rot(False)
        cosm_, sinm_ = cos_ref[...], sin_ref[...]
        nq, _ = _rms(lat_ref[:, 0:MLA_Q_RANK], gq_ref[...])
        nkv, _ = _rms(lat_ref[:, MLA_Q_RANK:MLA_Q_RANK + MLA_KV_RANK], gkv_ref[...])
        qlat = jnp.dot(nq.astype(MXU_DTYPE), wuq_ref[...], preferred_element_type=F32)
        klat = jnp.dot(nkv.astype(MXU_DTYPE), wuk_ref[...], preferred_element_type=F32)
        v = jnp.dot(nkv.astype(MXU_DTYPE), wuv_ref[...], preferred_element_type=F32)
        v_ref[...] = v.astype(v_ref.dtype)
        vT_ref[0] = v.T.astype(vT_ref.dtype)
        krb = lat_ref[:, 384:512]
        kr = krb * (cosm_ * _rope_lanes()) + _dot01(krb, rot, parts=3) * sinm_
        for h in range(4):
            sl = slice(h * MLA_PAD, (h + 1) * MLA_PAD)
            qh = qlat[:, sl]
            q = (qh * cosm_ + _dot01(qh, rot, parts=3) * sinm_) * (MLA_QK ** -0.5)
            k = klat[:, sl] + kr
            q_ref[:, sl] = q.astype(q_ref.dtype)
            k_ref[:, sl] = k.astype(k_ref.dtype)
            qT_ref[sl, :] = q.T.astype(qT_ref.dtype)
            kT_ref[0, sl, :] = k.T.astype(kT_ref.dtype)

    full = lambda a: pl.BlockSpec(a.shape, lambda i: (0,) * a.ndim)
    return pl.pallas_call(
        body, name=name, grid=(S // bs,),
        in_specs=[pl.BlockSpec((bs, LAT_W), lambda i: (i, 0)), full(g_q), full(g_kv), full(wuq), full(wuk), full(wuv),
                  pl.BlockSpec((bs, MLA_PAD), lambda i: (i, 0)), pl.BlockSpec((bs, MLA_PAD), lambda i: (i, 0))],
        out_specs=[pl.BlockSpec((bs, 512), lambda i: (i, 0)), pl.BlockSpec((bs, 512), lambda i: (i, 0)),
                   pl.BlockSpec((bs, 256), lambda i: (i, 0)), pl.BlockSpec((512, bs), lambda i: (0, i)),
                   pl.BlockSpec((1, 512, bs), lambda i: (i, 0, 0)), pl.BlockSpec((1, 256, bs), lambda i: (i, 0, 0))],
        out_shape=[jax.ShapeDtypeStruct((S, 512), MXU_DTYPE), jax.ShapeDtypeStruct((S, 512), MXU_DTYPE),
                   jax.ShapeDtypeStruct((S, 256), MXU_DTYPE), jax.ShapeDtypeStruct((512, S), MXU_DTYPE),
                   jax.ShapeDtypeStruct((S // bs, 512, bs), MXU_DTYPE), jax.ShapeDtypeStruct((S // bs, 256, bs), MXU_DTYPE)],
        compiler_params=_cparams("parallel"),
    )(lat, g_q, g_kv, wuq, wuk, wuv, cosm, sinm)


def _mla_prep_bwd(lat, g_q, g_kv, wuq, wuk, wuv, cosm, sinm, dq, dk, dv, *, bs, name):
    S = lat.shape[0]

    def body(lat_ref, gq_ref, gkv_ref, wuq_ref, wuk_ref, wuv_ref, cos_ref, sin_ref, dq_ref, dk_ref, dv_ref,
             dlat_ref, dwuq_ref, dwuk_ref, dwuv_ref, dgq_ref, dgkv_ref):
        @pl.when(pl.program_id(0) == 0)
        def _():
            for r in (dwuq_ref, dwuk_ref, dwuv_ref, dgq_ref, dgkv_ref):
                r[...] = jnp.zeros_like(r)

        rot_t = _rope_rot(True)
        cosm_, sinm_ = cos_ref[...], sin_ref[...]
        cq = lat_ref[:, 0:MLA_Q_RANK]
        ckv = lat_ref[:, MLA_Q_RANK:MLA_Q_RANK + MLA_KV_RANK]
        nq, rq = _rms(cq, gq_ref[...])
        nkv, rkv = _rms(ckv, gkv_ref[...])
        nqb, nkvb = nq.astype(MXU_DTYPE), nkv.astype(MXU_DTYPE)

        dqlat = []
        dkr = jnp.zeros((bs, MLA_PAD), F32)
        for h in range(4):
            sl = slice(h * MLA_PAD, (h + 1) * MLA_PAD)
            dqh = dq_ref[sl, :].T
            dqlat.append(dqh * cosm_ + _dot01(dqh * sinm_, rot_t, parts=3))
            dkr = dkr + dk_ref[:, sl]
        dqlat = jnp.concatenate(dqlat, axis=1).astype(MXU_DTYPE)
        dkb = dk_ref[...].astype(MXU_DTYPE)
        dvb = dv_ref[...].astype(MXU_DTYPE)

        dnq = lax.dot_general(dqlat, wuq_ref[...], NT, preferred_element_type=F32)
        dnkv = (lax.dot_general(dkb, wuk_ref[...], NT, preferred_element_type=F32)
                + lax.dot_general(dvb, wuv_ref[...], NT, preferred_element_type=F32))
        dwuq_ref[...] += lax.dot_general(nqb, dqlat, TN, preferred_element_type=F32)
        dwuk_ref[...] += lax.dot_general(nkvb, dkb, TN, preferred_element_type=F32)
        dwuv_ref[...] += lax.dot_general(nkvb, dvb, TN, preferred_element_type=F32)
        dcq, tq = _rms_bwd(dnq, cq, rq, gq_ref[...])
        dckv, tkv = _rms_bwd(dnkv, ckv, rkv, gkv_ref[...])
        dgq_ref[...] += jnp.sum(tq, axis=0, keepdims=True)
        dgkv_ref[...] += jnp.sum(tkv, axis=0, keepdims=True)
        dlat_ref[:, 0:MLA_Q_RANK] = dcq.astype(dlat_ref.dtype)
        dlat_ref[:, MLA_Q_RANK:MLA_Q_RANK + MLA_KV_RANK] = dckv.astype(dlat_ref.dtype)
        dkrb = dkr * (cosm_ * _rope_lanes()) + _dot01(dkr * sinm_, rot_t, parts=3)
        dlat_ref[:, 384:512] = dkrb.astype(dlat_ref.dtype)

    full = lambda a: pl.BlockSpec(a.shape, lambda i: (0,) * a.ndim)
    row = lambda w: pl.BlockSpec((bs, w), lambda i: (i, 0))
    acc = lambda *shape: pl.BlockSpec(shape, lambda i: (0,) * len(shape))
    return pl.pallas_call(
        body, name=name, grid=(S // bs,),
        in_specs=[row(LAT_W), full(g_q), full(g_kv), full(wuq), full(wuk), full(wuv), row(MLA_PAD), row(MLA_PAD),
                  pl.BlockSpec((512, bs), lambda i: (0, i)), row(512), row(256)],
        out_specs=[row(512), acc(256, 512), acc(128, 512), acc(128, 256), acc(1, 256), acc(1, 128)],
        out_shape=[jax.ShapeDtypeStruct((S, 512), MXU_DTYPE), jax.ShapeDtypeStruct((256, 512), F32),
                   jax.ShapeDtypeStruct((128, 512), F32), jax.ShapeDtypeStruct((128, 256), F32),
                   jax.ShapeDtypeStruct((1, 256), F32), jax.ShapeDtypeStruct((1, 128), F32)],
        compiler_params=_cparams("arbitrary"),
    )(lat, g_q, g_kv, wuq, wuk, wuv, cosm, sinm, dq, dk, dv)


def _row_spec(bs, w):
    return pl.BlockSpec((bs, w), lambda i: (i, 0))


def _vec_spec(w):
    return pl.BlockSpec((1, w), lambda i: (0, 0))


def _mix_specs(bs):
    return [pl.BlockSpec((GROUP_WIDTH, bs), lambda i: (0, i))] * 3 + [_row_spec(bs, GROUP_WIDTH)]


def _mix_groups(a_ref, b_ref, c_ref, d_ref):
    return [a_ref[...].T, b_ref[...].T, c_ref[...].T, d_ref[...]]


def _gnorm_fwd(outs, g, *, bs, name):
    S = outs[3].shape[0]

    def body(a_ref, b_ref, c_ref, d_ref, g_ref, o_ref, oT_ref):
        for k, x in enumerate(_mix_groups(a_ref, b_ref, c_ref, d_ref)):
            sl = slice(k * GROUP_WIDTH, (k + 1) * GROUP_WIDTH)
            y, _ = _rms(x, g_ref[:, sl])
            o_ref[:, sl] = y.astype(o_ref.dtype)
            oT_ref[sl, :] = y.T.astype(oT_ref.dtype)

    return pl.pallas_call(
        body, name=name, grid=(S // bs,),
        in_specs=_mix_specs(bs) + [_vec_spec(D_MODEL)],
        out_specs=[_row_spec(bs, D_MODEL), pl.BlockSpec((D_MODEL, bs), lambda i: (0, i))],
        out_shape=[jax.ShapeDtypeStruct((S, D_MODEL), MXU_DTYPE), jax.ShapeDtypeStruct((D_MODEL, S), MXU_DTYPE)],
        compiler_params=_cparams("parallel"),
    )(*outs, g)


def _gnorm_bwd(dgn, outs, g, *, bs, name):
    S = dgn.shape[0]

    def body(dgn_ref, a_ref, b_ref, c_ref, d_ref, g_ref, dmix_ref, dmixT_ref, dg_ref):
        @pl.when(pl.program_id(0) == 0)
        def _():
            dg_ref[...] = jnp.zeros_like(dg_ref)

        for k, x in enumerate(_mix_groups(a_ref, b_ref, c_ref, d_ref)):
            sl = slice(k * GROUP_WIDTH, (k + 1) * GROUP_WIDTH)
            _, r = _rms(x, g_ref[:, sl])
            dx, t = _rms_bwd(dgn_ref[:, sl], x, r, g_ref[:, sl])
            dmix_ref[:, sl] = dx
            dmixT_ref[sl, :] = dx.T
            dg_ref[:, sl] += jnp.sum(t, axis=0, keepdims=True)

    return pl.pallas_call(
        body, name=name, grid=(S // bs,),
        in_specs=[_row_spec(bs, D_MODEL)] + _mix_specs(bs) + [_vec_spec(D_MODEL)],
        out_specs=[_row_spec(bs, D_MODEL), pl.BlockSpec((D_MODEL, bs), lambda i: (0, i)), _vec_spec(D_MODEL)],
        out_shape=[jax.ShapeDtypeStruct((S, D_MODEL), F32), jax.ShapeDtypeStruct((D_MODEL, S), F32),
                   jax.ShapeDtypeStruct((1, D_MODEL), F32)],
        compiler_params=_cparams("arbitrary"),
    )(dgn, *outs, g)


def _ln_fwd(u, g, b, *, bs, name):
    S = u.shape[0]

    def body(u_ref, g_ref, b_ref, y_ref, yb_ref, ybT_ref, xh_ref, rs_ref):
        x = u_ref[...]
        mu = jnp.mean(x, axis=-1, keepdims=True)
        xc = x - mu
        rs = lax.rsqrt(jnp.mean(xc * xc, axis=-1, keepdims=True) + 1e-5)
        xh = xc * rs
        y = xh * g_ref[...] + b_ref[...]
        y_ref[...] = y
        yb_ref[...] = y.astype(yb_ref.dtype)
        ybT_ref[...] = y.T.astype(ybT_ref.dtype)
        xh_ref[...] = xh
        rs_ref[...] = rs

    return pl.pallas_call(
        body, name=name, grid=(S // bs,),
        in_specs=[_row_spec(bs, D_MODEL), _vec_spec(D_MODEL), _vec_spec(D_MODEL)],
        out_specs=[_row_spec(bs, D_MODEL), _row_spec(bs, D_MODEL), pl.BlockSpec((D_MODEL, bs), lambda i: (0, i)),
                   _row_spec(bs, D_MODEL), _row_spec(bs, 1)],
        out_shape=[jax.ShapeDtypeStruct((S, D_MODEL), F32), jax.ShapeDtypeStruct((S, D_MODEL), MXU_DTYPE),
                   jax.ShapeDtypeStruct((D_MODEL, S), MXU_DTYPE), jax.ShapeDtypeStruct((S, D_MODEL), F32),
                   jax.ShapeDtypeStruct((S, 1), F32)],
        compiler_params=_cparams("parallel"),
    )(u, g, b)


def _ln_bwd(dy, xh, rs, g, *, bs, name):
    S = dy.shape[0]

    def body(dy_ref, xh_ref, rs_ref, g_ref, du_ref, dub_ref, dg_ref, db_ref):
        @pl.when(pl.program_id(0) == 0)
        def _():
            dg_ref[...] = jnp.zeros_like(dg_ref)
            db_ref[...] = jnp.zeros_like(db_ref)

        dy_, xh_ = dy_ref[...], xh_ref[...]
        dxh = dy_ * g_ref[...]
        du = rs_ref[...] * (dxh - jnp.mean(dxh, axis=-1, keepdims=True)
                            - xh_ * jnp.mean(dxh * xh_, axis=-1, keepdims=True))
        du_ref[...] = du
        dub_ref[...] = du.astype(dub_ref.dtype)
        dg_ref[...] += jnp.sum(dy_ * xh_, axis=0, keepdims=True)
        db_ref[...] += jnp.sum(dy_, axis=0, keepdims=True)

    return pl.pallas_call(
        body, name=name, grid=(S // bs,),
        in_specs=[_row_spec(bs, D_MODEL), _row_spec(bs, D_MODEL), _row_spec(bs, 1), _vec_spec(D_MODEL)],
        out_specs=[_row_spec(bs, D_MODEL), _row_spec(bs, D_MODEL), _vec_spec(D_MODEL), _vec_spec(D_MODEL)],
        out_shape=[jax.ShapeDtypeStruct((S, D_MODEL), F32), jax.ShapeDtypeStruct((S, D_MODEL), MXU_DTYPE),
                   jax.ShapeDtypeStruct((1, D_MODEL), F32), jax.ShapeDtypeStruct((1, D_MODEL), F32)],
        compiler_params=_cparams("arbitrary"),
    )(dy, xh, rs, g)


def _swiglu_fwd(gu, *, bs, name):
    S = gu.shape[0]

    def body(gu_ref, a_ref, aT_ref):
        gt = gu_ref[:, :D_FF]
        a = gt / (1.0 + jnp.exp(-gt)) * gu_ref[:, D_FF:]
        a_ref[...] = a.astype(a_ref.dtype)
        aT_ref[...] = a.T.astype(aT_ref.dtype)

    return pl.pallas_call(
        body, name=name, grid=(S // bs,),
        in_specs=[_row_spec(bs, 2 * D_FF)],
        out_specs=[_row_spec(bs, D_FF), pl.BlockSpec((D_FF, bs), lambda i: (0, i))],
        out_shape=[jax.ShapeDtypeStruct((S, D_FF), MXU_DTYPE), jax.ShapeDtypeStruct((D_FF, S), MXU_DTYPE)],
        compiler_params=_cparams("parallel"),
    )(gu)


def _swiglu_bwd(da, gu, *, bs, name):
    S = gu.shape[0]

    def body(da_ref, gu_ref, dgu_ref):
        gt, da_ = gu_ref[:, :D_FF], da_ref[...]
        sg = 1.0 / (1.0 + jnp.exp(-gt))
        silu = gt * sg
        dgu_ref[:, :D_FF] = (da_ * gu_ref[:, D_FF:] * (sg + silu * (1.0 - sg))).astype(dgu_ref.dtype)
        dgu_ref[:, D_FF:] = (da_ * silu).astype(dgu_ref.dtype)

    return pl.pallas_call(
        body, name=name, grid=(S // bs,),
        in_specs=[_row_spec(bs, D_FF), _row_spec(bs, 2 * D_FF)],
        out_specs=_row_spec(bs, 2 * D_FF), out_shape=jax.ShapeDtypeStruct((S, 2 * D_FF), MXU_DTYPE),
        compiler_params=_cparams("parallel"),
    )(da, gu)


def _loss_head(y, target, *, bs, name):
    S = y.shape[0]

    def body(y_ref, t_ref, dy_ref, loss_ref):
        @pl.when(pl.program_id(0) == 0)
        def _():
            loss_ref[...] = jnp.zeros_like(loss_ref)

        e = y_ref[...] - t_ref[...]
        dy_ref[...] = e * (1.0 / D_MODEL)
        per_tok = jnp.mean(e * e, axis=-1, keepdims=True)
        loss_ref[...] += 0.5 * jnp.sum(per_tok, axis=0, keepdims=True)

    return pl.pallas_call(
        body, name=name, grid=(S // bs,),
        in_specs=[_row_spec(bs, D_MODEL), _row_spec(bs, D_MODEL)],
        out_specs=[_row_spec(bs, D_MODEL), pl.BlockSpec((1, 1), lambda i: (0, 0))],
        out_shape=[jax.ShapeDtypeStruct((S, D_MODEL), F32), jax.ShapeDtypeStruct((1, 1), F32)],
        compiler_params=_cparams("arbitrary"),
    )(y, target)


def _blk(n, target):
    if n <= target:
        return n
    best = None
    for b in range(128, target + 1, 128):
        if n % b == 0:
            best = b
    assert best is not None, n
    return best


def _rope_tables(S):
    pos = jnp.arange(S, dtype=F32)
    inv = ROPE_THETA ** (-jnp.arange(0, MLA_ROPE, 2, dtype=F32) / MLA_ROPE)
    ang = pos[:, None] * inv[None, :]
    cos, sin = jnp.cos(ang), jnp.sin(ang)
    one, zero, pad = jnp.ones((S, HEAD_DIM), F32), jnp.zeros((S, HEAD_DIM), F32), jnp.zeros((S, MLA_PAD - MLA_QK), F32)
    return jnp.concatenate([one, cos, cos, pad], axis=1), jnp.concatenate([zero, sin, sin, pad], axis=1)


def _prep_weights_a(w_in, w_uq, w_ukv):
    z = lambda n: jnp.zeros((D_MODEL, n), w_in.dtype)
    win_a = jnp.concatenate([w_in[:, 0:768], w_in[:, 1188:2468]], axis=1)
    win_l = jnp.concatenate([w_in[:, 772:1156], z(64), w_in[:, 1156:1188], z(32), w_in[:, 768:772], z(124)], axis=1)
    kv = w_ukv.reshape(MLA_KV_RANK, 4, 2 * HEAD_DIM)
    return dict(
        win_a=win_a, win_l=win_l, win_p=jnp.concatenate([win_a, win_l], axis=1),
        wuq=jnp.pad(w_uq.reshape(MLA_Q_RANK, 4, MLA_QK), ((0, 0), (0, 0), (0, MLA_PAD - MLA_QK))).reshape(MLA_Q_RANK, 512),
        wuk=jnp.pad(kv[:, :, :HEAD_DIM], ((0, 0), (0, 0), (0, HEAD_DIM))).reshape(MLA_KV_RANK, 512),
        wuv=kv[:, :, HEAD_DIM:].reshape(MLA_KV_RANK, 256))


def _unprep_grads(dwin_p, dwuq, dwuk, dwuv, dwo, dwgu, dwd):
    dw_in = jnp.concatenate([dwin_p[:, 0:768], dwin_p[:, 2560:2564], dwin_p[:, 2048:2432], dwin_p[:, 2496:2528],
                             dwin_p[:, 768:2048]], axis=1)
    dw_uq = dwuq.reshape(MLA_Q_RANK, 4, MLA_PAD)[:, :, :MLA_QK].reshape(MLA_Q_RANK, 4 * MLA_QK)
    dw_ukv = jnp.concatenate([dwuk.reshape(MLA_KV_RANK, 4, MLA_PAD)[:, :, :HEAD_DIM],
                              dwuv.reshape(MLA_KV_RANK, 4, HEAD_DIM)], axis=2).reshape(MLA_KV_RANK, 512)
    return dict(w_in=dw_in, mla_w_uq=dw_uq, mla_w_ukv=dw_ukv, w_o=dwo, w_gate=dwgu[:, :D_FF], w_up=dwgu[:, D_FF:],
                w_down=dwd)


def _layer_fwd(l, x, xb, xbT, W, P, tabs, blk, late_weights=None):
    S = x.shape[0]
    nb = S // blk
    n = lambda s: f"l{l}_{s}"
    bs = min(512, S)
    h_att = _mm(xb, W["win_a"], name=n("in_att"), out_dtype=MXU_DTYPE, bm=1024, bn=1024, bk=1024, colscale=Q_COLSCALE)
    lat = _mm(xb, W["win_l"], name=n("in_lat"), bm=2048, bn=LAT_W, bk=1024)
    fg = lat[:, 512:516].T.reshape(4, S // 128, 128)
    cpos, cneg = _fox_gate_fwd(fg, P["fox_b_f"], name=n("fox_gate"))
    one3 = jnp.ones((S, 4, 3), MXU_DTYPE)
    zpad = jnp.zeros((S, 4, MLA_PAD - HEAD_DIM - 6), MXU_DTYPE)
    per_tok = lambda parts: parts.reshape(4, 3, S).transpose(2, 0, 1)
    q_f = jnp.concatenate([h_att[:, COL_FQ:COL_FQ + 256].reshape(S, 4, HEAD_DIM), per_tok(cpos), one3, zpad],
                          axis=2).reshape(S, 4 * MLA_PAD)
    k_f = jnp.concatenate([h_att[:, COL_FK:COL_FK + 256].reshape(S, 4, HEAD_DIM), one3, per_tok(cneg), zpad],
                          axis=2).reshape(S, 4 * MLA_PAD)
    v_f = h_att[:, COL_FV:COL_FV + 256]
    oT_a, lse_a = _smax_fwd_t(q_f.T, k_f, _kv_blocks_t(v_f, blk), dk=MLA_PAD, blk=blk, name=n("fox_fwd"))
    q_m, k_m, v_m, qT_m, kT3_m, vT3_m = _mla_prep_fwd(lat, P["mla_g_q"], P["mla_g_kv"], W["wuq"], W["wuk"], W["wuv"],
                                                      *tabs, bs=blk, name=n("mla_prep"))
    oT_b, lse_b = _smax_fwd_t(qT_m, k_m, vT3_m, dk=MLA_PAD, blk=blk, name=n("mla_fwd"))
    qT_c = h_att[:, COL_SQ:COL_SQ + 256].T
    bsb = min(BLK_STICK, S)
    oT_c, lt_c = _sb_fwd_t(qT_c, h_att, _kv_blocks_t(h_att[:, COL_SV:COL_SV + 256], bsb), blk=bsb, name=n("sb_fwd"))
    out_d, lse_d = _swa_fwd(h_att, P["swa_sinks"], name=n("swa_fwd"))
    outs = (oT_a, oT_b, oT_c, out_d)
    gn, gnT = _gnorm_fwd(outs, P["mix_g"], bs=bs, name=n("gnorm"))
    if late_weights is not None:
        W = dict(W, **late_weights(gn))
    u1 = _mm(gn, W["w_o"], name=n("out_proj"), bm=1024, bn=1024, bk=1024, resid=x, alpha=ALPHA)
    x1, x1b, x1bT, xh1, rs1 = _ln_fwd(u1, P["ln1_g"], P["ln1_b"], bs=bs, name=n("ln1"))
    gu = _mm(x1b, W["wgu"], name=n("gate_up"), bm=2048, bn=512, bk=1024)
    a, aT = _swiglu_fwd(gu, bs=min(256, S), name=n("swiglu"))
    u2 = _mm(a, W["w_down"], name=n("down"), bm=1024, bn=1024, bk=_blk(D_FF, 1408), resid=x1, alpha=ALPHA)
    x2, x2b, x2bT, xh2, rs2 = _ln_fwd(u2, P["ln2_g"], P["ln2_b"], bs=bs, name=n("ln2"))
    saved = dict(xbT=xbT, gnT=gnT, x1bT=x1bT, h_att=h_att, lat=lat, fg=fg, outs=outs, oT_a=oT_a, oT_b=oT_b, q_f=q_f, k_f=k_f, v_f=v_f,
                 qT_c=qT_c, lse_a=lse_a, lse_b=lse_b, lse_d=lse_d, lt_c=lt_c, q_m=q_m, k_m=k_m, v_m=v_m, qT_m=qT_m, kT3_m=kT3_m,
                 xh1=xh1, rs1=rs1, gu=gu, aT=aT, xh2=xh2, rs2=rs2)
    return x2, x2b, x2bT, saved, W


def _layer_bwd(l, dx2, sv, W, P, tabs, blk, send_early=None):
    S = dx2.shape[0]
    n = lambda s: f"l{l}_{s}"
    bs = min(512, S)
    h_att = sv["h_att"]
    du2, du2b, dg2, db2 = _ln_bwd(dx2, sv["xh2"], sv["rs2"], P["ln2_g"], bs=bs, name=n("ln2_bwd"))
    da = _mm(du2b, W["w_down"], name=n("down_dx"), tb=True, bm=1024, bn=_blk(D_FF, 1408), bk=1024)
    dwd = _mm(sv["aT"], du2b, name=n("down_dw"), bm=_blk(D_FF, 1408), bn=1024, bk=1024)
    dgu = _swiglu_bwd(da, sv["gu"], bs=min(256, S), name=n("swiglu_bwd"))
    dx1 = _mm(dgu, W["wgu"], name=n("gate_up_dx"), tb=True, bm=1024, bn=1024, bk=_blk(2 * D_FF, 1408), resid=du2,
              alpha=ALPHA)
    dwgu = _mm(sv["x1bT"], dgu, name=n("gate_up_dw"), bm=1024, bn=_blk(2 * D_FF, 1408), bk=1024)
    du1, du1b, dg1, db1 = _ln_bwd(dx1, sv["xh1"], sv["rs1"], P["ln1_g"], bs=bs, name=n("ln1_bwd"))
    dgn = _mm(du1b, W["w_o"], name=n("out_proj_dx"), tb=True, bm=1024, bn=1024, bk=1024)
    dwo = _mm(sv["gnT"], du1b, name=n("out_proj_dw"), bm=1024, bn=1024, bk=1024)
    mix_g = P["mix_g"]
    if send_early is not None:
        mix_g = mix_g + send_early(dict(w_o=dwo, w_gate=dwgu[:, :D_FF], w_up=dwgu[:, D_FF:], w_down=dwd))[0, 0]
    dmix, dmixT, dmixg = _gnorm_bwd(dgn, sv["outs"], mix_g, bs=bs, name=n("gnorm_bwd"))
    q_f, k_f = sv["q_f"], sv["k_f"]
    dqT_a, dk_a, dva = _smax_bwd_t(q_f.T, q_f, k_f, _kv_blocks_t(k_f, blk), sv["v_f"], dmix, dmixT, sv["oT_a"],
                                   sv["lse_a"], dk=MLA_PAD, dcb=0, qscale=HEAD_DIM ** -0.5, blk=blk, name=n("fox_bwd"))
    dq_a, dk_a = dqT_a.T.reshape(S, 4, MLA_PAD), dk_a.reshape(S, 4, MLA_PAD)
    dqa, dka = dq_a[:, :, :HEAD_DIM].reshape(S, 256), dk_a[:, :, :HEAD_DIM].reshape(S, 256)
    dcq = dq_a[:, :, HEAD_DIM].T.reshape(4, S // 128, 128)
    dck = dk_a[:, :, HEAD_DIM + 3].T.reshape(4, S // 128, 128)
    q_m, k_m = sv["q_m"], sv["k_m"]
    dqT_b, dkb, dvb = _smax_bwd_t(sv["qT_m"], q_m, k_m, sv["kT3_m"], sv["v_m"], dmix, dmixT, sv["oT_b"],
                                  sv["lse_b"], dk=MLA_PAD, dcb=2, qscale=MLA_QK ** -0.5, blk=blk, name=n("mla_bwd"))
    bsb = min(BLK_STICK, S)
    dqT_c, dkc, dvc = _sb_bwd_t(sv["qT_c"], h_att, _kv_blocks_t(h_att[:, COL_SK:COL_SK + 256], bsb), dmix, dmixT,
                                sv["lt_c"], dcb=4, qscale=HEAD_DIM ** -0.5, blk=bsb, name=n("sb_bwd"))
    dqc = dqT_c.T
    dqd, dkd, dvd, dsink = _swa_bwd(h_att, P["swa_sinks"], dmix, sv["outs"][3], sv["lse_d"], dcb=3, name=n("swa_bwd"))
    dlat, dwuq, dwuk, dwuv, dgq, dgkv = _mla_prep_bwd(
        sv["lat"], P["mla_g_q"], P["mla_g_kv"], W["wuq"], W["wuk"], W["wuv"], *tabs, dqT_b, dkb, dvb,
        bs=bs, name=n("mla_prep_bwd"))
    dfg, dbf = _fox_gate_bwd(sv["fg"], P["fox_b_f"], dck, dcq, q_unscale=HEAD_DIM ** 0.5, name=n("fox_gate_bwd"))
    dfg_blk = jnp.pad(dfg.reshape(4, S).T, ((0, 0), (0, 124)))
    dh = jnp.concatenate([t.astype(MXU_DTYPE) for t in (dqa, dka, dva, dqc, dkc, dvc, dqd, dkd, dvd, dlat, dfg_blk)], axis=1)
    dx = _mm(dh, W["win_p"], name=n("in_dx"), tb=True, bm=1024, bn=1024, bk=_blk(PERM_W, 1024), resid=du1, alpha=ALPHA)
    dwin_p = _mm(sv["xbT"], dh, name=n("in_dw"), bm=1024, bn=_blk(PERM_W, 1024), bk=1024)
    grads = _unprep_grads(dwin_p, dwuq, dwuk, dwuv, dwo, dwgu, dwd)
    grads.update(fox_b_f=dbf[:, 0], mla_g_q=dgq[0], mla_g_kv=dgkv[0], swa_sinks=dsink[:, 0], mix_g=dmixg[0],
                 ln1_g=dg1[0], ln1_b=db1[0], ln2_g=dg2[0], ln2_b=db2[0])
    return dx, grads


BIG = ("w_in", "mla_w_uq", "mla_w_ukv", "w_o", "w_gate", "w_up", "w_down")
SMALL = ("fox_b_f", "mla_g_q", "mla_g_kv", "swa_sinks", "mix_g", "ln1_g", "ln1_b", "ln2_g", "ln2_b")
SHARD_AXIS = dict(w_in=2, mla_w_uq=2, mla_w_ukv=2, w_o=1, w_gate=2, w_up=2, w_down=1)
N_CHIPS = 4
ANY = pl.BlockSpec(memory_space=pl.ANY)


HBM = pl.BlockSpec(memory_space=pltpu.HBM)
SEM = pl.BlockSpec(memory_space=pltpu.SEMAPHORE)
N_PEER_CHIPS = N_CHIPS - 1


def _peer_copies(src_ref, land_ref, sems, scatter):
    x, y, c = lax.axis_index("x"), lax.axis_index("y"), lax.axis_index("c")
    me = 2 * x + y
    out = []
    for r, (px, py) in enumerate([(1 - x, y), (x, 1 - y), (1 - x, 1 - y)]):
        theirs = 2 * px + py
        send = pltpu.make_async_remote_copy(
            src_ref=src_ref.at[theirs] if scatter else src_ref, dst_ref=land_ref.at[me],
            send_sem=sems[2 * r], recv_sem=sems[2 * r + 1], device_id=(px, py, c), device_id_type=MESH)
        arrive = pltpu.make_async_remote_copy(
            src_ref=src_ref.at[me] if scatter else src_ref, dst_ref=land_ref.at[theirs],
            send_sem=sems[2 * r], recv_sem=sems[2 * r + 1], device_id=(px, py, c), device_id_type=MESH)
        out.append((send, arrive))
    return out


def _exchange_start(srcs, *, scatter, name):
    nt = len(srcs)
    ns = 2 * N_PEER_CHIPS * nt
    land_shapes = [s.shape if scatter else (N_CHIPS,) + s.shape for s in srcs]

    def body(*refs):
        src_refs, land_refs, outs = refs[:nt], refs[nt:2 * nt], refs[2 * nt:]
        for t in range(nt):
            for send, _ in _peer_copies(src_refs[t], land_refs[t], outs[6 * t:6 * t + 6], scatter):
                send.start()
        outs[-1][...] = jnp.zeros_like(outs[-1])

    res = pl.pallas_call(
        body, name=name,
        out_shape=(*[pltpu.SemaphoreType.DMA(())] * ns, *[pltpu.HBM(s.shape, s.dtype) for s in srcs],
                   *[pltpu.HBM(ls, s.dtype) for ls, s in zip(land_shapes, srcs)], jax.ShapeDtypeStruct((8, 128), F32)),
        in_specs=(HBM,) * (2 * nt), out_specs=(*[SEM] * ns, *[HBM] * (2 * nt), pl.BlockSpec(memory_space=pltpu.VMEM)),
        input_output_aliases={i: ns + i for i in range(2 * nt)},
        compiler_params=pltpu.CompilerParams(has_side_effects=pltpu.SideEffectType.DATAFLOW_SIDE_EFFECTING),
    )(*[pltpu.with_memory_space_constraint(s, pltpu.HBM) for s in srcs],
      *[pltpu.with_memory_space_constraint(lax.empty(ls, s.dtype), pltpu.HBM) for ls, s in zip(land_shapes, srcs)])
    return dict(sems=res[:ns], srcs=res[ns:ns + nt], lands=res[ns + nt:ns + 2 * nt], token=res[-1])


def _exchange_wait(started, after, *, scatter, name):
    nt = len(started["srcs"])
    ns = 2 * N_PEER_CHIPS * nt

    def body(*refs):
        src_refs, land_refs, sems = refs[:nt], refs[nt:2 * nt], refs[2 * nt:2 * nt + ns]
        for t in range(nt):
            for send, arrive in _peer_copies(src_refs[t], land_refs[t], sems[6 * t:6 * t + 6], scatter):
                send.wait_send()
                arrive.wait_recv()

    both = list(started["srcs"]) + list(started["lands"])
    res = pl.pallas_call(
        body, name=name, out_shape=tuple(pltpu.HBM(a.shape, a.dtype) for a in both),
        in_specs=(*[HBM] * (2 * nt), *[SEM] * ns, ANY), out_specs=(HBM,) * (2 * nt),
        input_output_aliases={i: i for i in range(2 * nt)},
        compiler_params=pltpu.CompilerParams(has_side_effects=pltpu.SideEffectType.DATAFLOW_SIDE_EFFECTING),
    )(*both, *started["sems"], after)
    return res[:nt], res[nt:]


def _core_exchange(tensors, *, name):
    nt = len(tensors)

    def body(*refs):
        ins, outs = refs[:nt], refs[nt:2 * nt]
        send_sems, recv_sems = refs[2 * nt:]
        sibling = (lax.axis_index("x"), lax.axis_index("y"), 1 - lax.axis_index("c"))
        copies = [pltpu.make_async_remote_copy(src_ref=ins[t], dst_ref=outs[t], send_sem=send_sems.at[t],
                                               recv_sem=recv_sems.at[t], device_id=sibling, device_id_type=MESH)
                  for t in range(nt)]
        for cp in copies:
            cp.start()
        for cp in copies:
            cp.wait_recv()
        for cp in copies:
            cp.wait_send()

    return pl.pallas_call(
        body, name=name, in_specs=[ANY] * nt, out_specs=[ANY] * nt,
        out_shape=[jax.ShapeDtypeStruct(t.shape, t.dtype) for t in tensors],
        scratch_shapes=[pltpu.SemaphoreType.DMA((nt,)), pltpu.SemaphoreType.DMA((nt,))],
        compiler_params=pltpu.CompilerParams(has_side_effects=True),
    )(*tensors)


def _all_sum_small(block, *, name):
    R = block.shape[0]
    n_dev = 8

    def body(x_ref, o_ref, slots, send_sems, recv_sems):
        x, y, c = lax.axis_index("x"), lax.axis_index("y"), lax.axis_index("c")
        me = 4 * x + 2 * y + c
        slots[me] = x_ref[...]
        sends, recvs = [], []
        for d in range(1, n_dev):
            px, py, pc = x ^ (d >> 2), y ^ ((d >> 1) & 1), c ^ (d & 1)
            theirs = 4 * px + 2 * py + pc
            sends.append(pltpu.make_async_remote_copy(
                src_ref=x_ref, dst_ref=slots.at[me], send_sem=send_sems.at[d - 1], recv_sem=recv_sems.at[d - 1],
                device_id=(px, py, pc), device_id_type=MESH))
            recvs.append(pltpu.make_async_remote_copy(
                src_ref=x_ref, dst_ref=slots.at[theirs], send_sem=send_sems.at[d - 1], recv_sem=recv_sems.at[d - 1],
                device_id=(px, py, pc), device_id_type=MESH))
        for cp in sends:
            cp.start()
        for cp in recvs:
            cp.wait_recv()
        for cp in sends:
            cp.wait_send()
        total = slots[0]
        for k in range(1, n_dev):
            total = total + slots[k]
        o_ref[...] = total

    return pl.pallas_call(
        body, name=name, in_specs=[pl.BlockSpec(memory_space=pltpu.VMEM)],
        out_specs=pl.BlockSpec(memory_space=pltpu.VMEM), out_shape=jax.ShapeDtypeStruct((R, 128), F32),
        scratch_shapes=[pltpu.VMEM((n_dev, R, 128), F32), pltpu.SemaphoreType.DMA((n_dev - 1,)),
                        pltpu.SemaphoreType.DMA((n_dev - 1,))],
        compiler_params=pltpu.CompilerParams(has_side_effects=True),
    )(block)


def _sum_chips_into(acc, land, own, me, layer, *, br, name):
    _, R, C = land.shape

    def body(me_ref, land_ref, own_ref, acc_ref, o_ref):
        mine = me_ref[0]
        total = None
        for k in range(N_CHIPS):
            part = jnp.where(mine == k, own_ref[...], land_ref[k]).astype(F32)
            total = part if total is None else total + part
        o_ref[0] = total

    return pl.pallas_call(
        body, name=name, grid=(R // br,),
        in_specs=[pl.BlockSpec(memory_space=pltpu.SMEM), pl.BlockSpec((N_CHIPS, br, C), lambda i: (0, i, 0)),
                  pl.BlockSpec((br, C), lambda i: (i, 0)), ANY],
        out_specs=pl.BlockSpec((1, br, C), lambda i: (layer, i, 0)),
        out_shape=jax.ShapeDtypeStruct(acc.shape, F32), input_output_aliases={3: 0},
        compiler_params=_cparams("parallel"),
    )(me, land, own, acc)


def _adamw_math(w, g, m, v):
    m = ADAM_B1 * m + (1.0 - ADAM_B1) * g
    v = ADAM_B2 * v + (1.0 - ADAM_B2) * (g * g)
    m_hat = m / (1.0 - ADAM_B1 ** ADAM_STEP)
    v_hat = v / (1.0 - ADAM_B2 ** ADAM_STEP)
    return -ADAM_LR * (m_hat / (jnp.sqrt(v_hat) + ADAM_EPS) + ADAM_WD * w), m, v


def _adamw(w, m, v, g_a, g_b, *, br, name):
    R, C = w.shape
    two = g_b is not None

    def body(*refs):
        if two:
            w_ref, m_ref, v_ref, ga_ref, gb_ref, g_ref, d_ref, nm_ref, nv_ref = refs
            g = ga_ref[...] + gb_ref[...]
        else:
            w_ref, m_ref, v_ref, ga_ref, g_ref, d_ref, nm_ref, nv_ref = refs
            g = ga_ref[...]
        g_ref[...] = g
        d_ref[...], nm_ref[...], nv_ref[...] = _adamw_math(w_ref[...], g, m_ref[...], v_ref[...])

    spec = pl.BlockSpec((br, C), lambda i: (i, 0))
    args = [w, m, v, g_a] + ([g_b] if two else [])
    return pl.pallas_call(
        body, name=name, grid=(R // br,), in_specs=[spec] * len(args), out_specs=[spec] * 4,
        out_shape=[jax.ShapeDtypeStruct((R, C), F32)] * 4,
        compiler_params=_cparams("parallel"),
    )(*args)


SMALL_ROWS = dict(fox_b_f=1, mla_g_q=2, mla_g_kv=1, swa_sinks=1, mix_g=8, ln1_g=8, ln1_b=8, ln2_g=8, ln2_b=8)
SMALL_ROWS_PER_LAYER = sum(SMALL_ROWS.values())


def _pack_small(vals, extra_rows):
    L = vals[SMALL[0]].shape[0]
    per_layer = []
    for name in SMALL:
        a = vals[name].astype(F32)
        a = jnp.pad(a, ((0, 0), (0, SMALL_ROWS[name] * 128 - a.shape[1])))
        per_layer.append(a.reshape(L, SMALL_ROWS[name], 128))
    out = jnp.concatenate(per_layer, axis=1).reshape(L * SMALL_ROWS_PER_LAYER, 128)
    return jnp.pad(out, ((0, extra_rows), (0, 0)))


def _unpack_small(block, shapes):
    L = shapes[SMALL[0]][0]
    body = block[:L * SMALL_ROWS_PER_LAYER].reshape(L, SMALL_ROWS_PER_LAYER, 128)
    out, r = {}, 0
    for name in SMALL:
        n = shapes[name][1]
        out[name] = body[:, r:r + SMALL_ROWS[name]].reshape(L, SMALL_ROWS[name] * 128)[:, :n]
        r += SMALL_ROWS[name]
    return out


def _to_chips(g, axis):
    L, a, b = g.shape
    if axis == 2:
        return g.reshape(L, a, N_CHIPS, b // N_CHIPS).transpose(2, 0, 1, 3)
    return g.reshape(L, N_CHIPS, a // N_CHIPS, b).transpose(1, 0, 2, 3)


def kernel(x, w_in, fox_b_f, mla_g_q, mla_g_kv, mla_w_uq, mla_w_ukv, swa_sinks, mix_g, w_o, ln1_g, ln1_b, w_gate, w_up, w_down, ln2_g, ln2_b, loss_target, m_w_in, m_fox_b_f, m_mla_g_q, m_mla_g_kv, m_mla_w_uq, m_mla_w_ukv, m_swa_sinks, m_mix_g, m_w_o, m_ln1_g, m_ln1_b, m_w_gate, m_w_up, m_w_down, m_ln2_g, m_ln2_b, v_w_in, v_fox_b_f, v_mla_g_q, v_mla_g_kv, v_mla_w_uq, v_mla_w_ukv, v_swa_sinks, v_mix_g, v_w_o, v_ln1_g, v_ln1_b, v_w_gate, v_w_up, v_w_down, v_ln2_g, v_ln2_b):
    w = dict(w_in=w_in, fox_b_f=fox_b_f, mla_g_q=mla_g_q, mla_g_kv=mla_g_kv, mla_w_uq=mla_w_uq, mla_w_ukv=mla_w_ukv,
             swa_sinks=swa_sinks, mix_g=mix_g, w_o=w_o, ln1_g=ln1_g, ln1_b=ln1_b, w_gate=w_gate, w_up=w_up,
             w_down=w_down, ln2_g=ln2_g, ln2_b=ln2_b)
    m = dict(w_in=m_w_in, fox_b_f=m_fox_b_f, mla_g_q=m_mla_g_q, mla_g_kv=m_mla_g_kv, mla_w_uq=m_mla_w_uq,
             mla_w_ukv=m_mla_w_ukv, swa_sinks=m_swa_sinks, mix_g=m_mix_g, w_o=m_w_o, ln1_g=m_ln1_g, ln1_b=m_ln1_b,
             w_gate=m_w_gate, w_up=m_w_up, w_down=m_w_down, ln2_g=m_ln2_g, ln2_b=m_ln2_b)
    v = dict(w_in=v_w_in, fox_b_f=v_fox_b_f, mla_g_q=v_mla_g_q, mla_g_kv=v_mla_g_kv, mla_w_uq=v_mla_w_uq,
             mla_w_ukv=v_mla_w_ukv, swa_sinks=v_swa_sinks, mix_g=v_mix_g, w_o=v_w_o, ln1_g=v_ln1_g, ln1_b=v_ln1_b,
             w_gate=v_w_gate, w_up=v_w_up, w_down=v_w_down, ln2_g=v_ln2_g, ln2_b=v_ln2_b)
    names = tuple(w)
    L = w_in.shape[0]
    S = x.shape[1]
    blk = min(BLK_SOFTMAX, S)
    bs = min(512, S)

    me = 2 * lax.axis_index("x") + lax.axis_index("y")
    axis_of = {k: SHARD_AXIS[k] - 1 for k in BIG}
    groups = (("w_in", "mla_w_uq", "mla_w_ukv"), ("w_o", "w_gate", "w_up", "w_down"))

    started, last = [], None
    for l in range(L):
        per_group = []
        for g, group in enumerate(groups):
            srcs = [w[k][l].astype(MXU_DTYPE) for k in group]
            if last is not None:
                t = min(range(len(srcs)), key=lambda i: srcs[i].size)
                srcs[t] = srcs[t] + last["token"][0, 0].astype(MXU_DTYPE)
            last = _exchange_start(srcs, scatter=False, name=f"gather_start{l}_{g}")
            per_group.append(last)
        started.append(per_group)
    all_started = sum(st["token"] for per_group in started for st in per_group)

    def gathered(l, g, after):
        mine, lands = _exchange_wait(started[l][g], after, scatter=False, name=f"gather_wait{l}_{g}")
        shard = lambda t, k: jnp.where(me == k, mine[t], lands[t][k])
        whole = lambda t, axis: jnp.concatenate([shard(t, k) for k in range(N_CHIPS)], axis=axis)
        if g == 0:
            return _prep_weights_a(*[whole(t, axis_of[name]) for t, name in enumerate(groups[0])])
        gate_up = jnp.concatenate([shard(t, k) for t in (1, 2) for k in range(N_CHIPS)], axis=1)
        return dict(w_o=whole(0, 0), wgu=gate_up, w_down=whole(3, 0))

    def scatter(l, g, grads):
        to_owner = [_to_chips(grads[k].astype(MXU_DTYPE)[None], axis_of[k] + 1)[:, 0] for k in groups[g]]
        return _exchange_start(to_owner, scatter=True, name=f"scatter_start{l}_{g}")

    tabs = _rope_tables(S)
    Ps = []
    for l in range(L):
        P = dict(fox_b_f=fox_b_f[l], swa_sinks=swa_sinks[l])
        for k in ("mla_g_q", "mla_g_kv", "mix_g", "ln1_g", "ln1_b", "ln2_g", "ln2_b"):
            P[k] = w[k][l][None, :]
        Ps.append(P)

    xa = x[0]
    xb = xa.astype(MXU_DTYPE)
    xbT = xb.T
    saved, Ws = [], []
    for l in range(L):
        W = gathered(l, 0, all_started if l == 0 else xa)
        late = lambda after, l=l: gathered(l, 1, after)
        xa, xb, xbT, sv, W = _layer_fwd(l, xa, xb, xbT, W, Ps[l], tabs, blk, late_weights=late)
        saved.append(sv)
        Ws.append(W)
    dx, loss_part = _loss_head(xa, loss_target[0], bs=bs, name="loss_head")

    layer_grads = [None] * L
    sent = [[None, None] for _ in range(L)]
    pin = None
    for l in reversed(range(L)):
        P = Ps[l] if pin is None else dict(Ps[l], ln2_g=Ps[l]["ln2_g"] + pin[0, 0])

        def send_early(grads, l=l):
            sent[l][1] = scatter(l, 1, grads)
            return sent[l][1]["token"]

        dx, layer_grads[l] = _layer_bwd(l, dx, saved[l], Ws[l], P, tabs, blk, send_early=send_early)
        sent[l][0] = scatter(l, 0, layer_grads[l])
        pin = sent[l][0]["token"]
    grad_x = dx[None]

    me_arr = me.astype(jnp.int32)[None]
    partial = {k: jnp.zeros(w[k].shape, F32) for k in BIG}
    after = dx
    for l in reversed(range(L)):
        for g in (1, 0):
            mine, lands = _exchange_wait(sent[l][g], after, scatter=True, name=f"scatter_wait{l}_{g}")
            for t, k in enumerate(groups[g]):
                own = lax.dynamic_index_in_dim(mine[t], me, 0, keepdims=False)
                partial[k] = _sum_chips_into(partial[k], lands[t], own, me_arr, l, br=_rows(own.shape[0]),
                                             name=f"sum_{k}_l{l}")
            after = partial[groups[g][-1]]
    partial = [partial[k] for k in BIG]
    sibling = _core_exchange(partial, name="swap_partials")
    local = {k: jnp.stack([layer_grads[l][k] for l in range(L)]) for k in SMALL}
    out = {}
    for k, mine, theirs in zip(BIG, partial, sibling):
        shp = w[k].shape
        two_d = lambda a: a.reshape(shp[0] * shp[1], shp[2])
        res = _adamw(two_d(w[k]), two_d(m[k]), two_d(v[k]), two_d(mine), two_d(theirs), br=_rows(shp[0] * shp[1]),
                     name=f"adamw_{k}")
        out[k] = [a.reshape(shp) for a in res]

    shapes = {k: w[k].shape for k in SMALL}
    extra = 8 + (-L * SMALL_ROWS_PER_LAYER) % 8
    block = _pack_small({k: local[k] for k in SMALL}, extra)
    block = block.at[L * SMALL_ROWS_PER_LAYER, 0].set(loss_part[0, 0])
    total = _all_sum_small(block, name="sum_small")
    loss = total[L * SMALL_ROWS_PER_LAYER, 0]
    res = _adamw(_pack_small({k: w[k] for k in SMALL}, extra), _pack_small({k: m[k] for k in SMALL}, extra),
                 _pack_small({k: v[k] for k in SMALL}, extra), total, None, br=total.shape[0], name="adamw_small")
    res = [_unpack_small(t, shapes) for t in res]
    for k in SMALL:
        out[k] = [r[k] for r in res]

    return (loss, grad_x, *[out[k][0] for k in names], *[out[k][1] for k in names],
            *[out[k][2] for k in names], *[out[k][3] for k in names])


def _rows(n):
    for b in (256, 128, 64, 32, 16, 8):
        if n % b == 0:
            return b
    return n
```

```python
import functools

import numpy as np
import jax
import jax.numpy as jnp
from jax import lax
from jax.experimental import pallas as pl
from jax.experimental.pallas import tpu as pltpu

F32 = jnp.float32
MXU_DTYPE = jnp.bfloat16
NEG_INF = -1e30

D_MODEL = 1024
DEPTH = 4
HEAD_DIM = 64
GROUP_WIDTH = 256
D_FF = 2816
MLA_Q_RANK = 256
MLA_KV_RANK = 128
MLA_ROPE = 32
MLA_QK = 96
MLA_PAD = 128
ROPE_THETA = 10000.0
WINDOW = 128
ALPHA = (2.0 * DEPTH) ** 0.25
SWA_SLOPES = tuple(float(2.0 ** (-8.0 * h / 4)) for h in range(1, 5))
ATT_W = 2048
LAT_W = 640
PERM_W = ATT_W + LAT_W
COL_FQ, COL_FK, COL_FV = 0, 256, 512
COL_SQ, COL_SK, COL_SV = 768, 1024, 1280
COL_WQ, COL_WK, COL_WV = 1536, 1792, 1920
Q_COLSCALE = np.ones((1, ATT_W), np.float32)
Q_COLSCALE[:, COL_FQ:COL_FQ + 256] = HEAD_DIM ** -0.5
Q_COLSCALE[:, COL_SQ:COL_SQ + 256] = HEAD_DIM ** -0.5

ADAM_LR, ADAM_B1, ADAM_B2, ADAM_EPS, ADAM_WD, ADAM_STEP = 0.001, 0.9, 0.999, 1e-08, 0.01, 10

VMEM_LIMIT = 56 * 1024 * 1024
NT = (((1,), (1,)), ((), ()))
TN = (((0,), (0,)), ((), ()))
MESH = pl.DeviceIdType.MESH


def _cparams(*sem):
    return pltpu.CompilerParams(dimension_semantics=sem, vmem_limit_bytes=VMEM_LIMIT)


def _dot01(x, m01, dn=None, parts=2):
    acc = None
    rem = x
    for _ in range(parts):
        part = rem.astype(MXU_DTYPE)
        rem = rem - part.astype(F32)
        if dn is None:
            t = jnp.dot(part, m01, preferred_element_type=F32)
        else:
            t = lax.dot_general(part, m01, dn, preferred_element_type=F32)
        acc = t if acc is None else acc + t
    return acc


def _mm(a, b, *, name, ta=False, tb=False, out_dtype=F32, bm=512, bn=512, bk=512, resid=None, alpha=1.0,
        colscale=None):
    M, K = (a.shape[1], a.shape[0]) if ta else a.shape
    N = b.shape[0] if tb else b.shape[1]
    assert (b.shape[1] if tb else b.shape[0]) == K
    assert resid is None or colscale is None
    bm, bn, bk = min(bm, M), min(bn, N), min(bk, K)
    assert M % bm == 0 and N % bn == 0 and K % bk == 0, (name, M, N, K, bm, bn, bk)
    nk = K // bk
    assert nk == 1 or (out_dtype == F32 and colscale is None), name
    dn = (((0 if ta else 1,), (1 if tb else 0,)), ((), ()))

    extra = resid is not None or colscale is not None

    def body(*refs):
        a_ref, b_ref = refs[:2]
        r_ref = refs[2] if extra else None
        o_ref = refs[3] if extra else refs[2]
        k = pl.program_id(2)

        def first():
            r = lax.dot_general(a_ref[...].astype(MXU_DTYPE), b_ref[...].astype(MXU_DTYPE), dn,
                                preferred_element_type=F32)
            if resid is not None:
                r = r + alpha * r_ref[...]
            if colscale is not None:
                r = r * r_ref[...]
            o_ref[...] = r.astype(o_ref.dtype)

        if nk == 1:
            first()
        else:
            pl.when(k == 0)(first)

            @pl.when(k > 0)
            def _():
                o_ref[...] += lax.dot_general(a_ref[...].astype(MXU_DTYPE), b_ref[...].astype(MXU_DTYPE), dn,
                                              preferred_element_type=F32)

    a_spec = pl.BlockSpec((bk, bm), lambda i, j, k: (k, i)) if ta else pl.BlockSpec((bm, bk), lambda i, j, k: (i, k))
    b_spec = pl.BlockSpec((bn, bk), lambda i, j, k: (j, k)) if tb else pl.BlockSpec((bk, bn), lambda i, j, k: (k, j))
    in_specs = [a_spec, b_spec]
    args = [a, b]
    if resid is not None:
        in_specs.append(pl.BlockSpec((bm, bn), lambda i, j, k: (i, j)))
        args.append(resid)
    if colscale is not None:
        in_specs.append(pl.BlockSpec((1, bn), lambda i, j, k: (0, j)))
        args.append(colscale)
    return pl.pallas_call(
        body, name=name, grid=(M // bm, N // bn, nk), in_specs=in_specs,
        out_specs=pl.BlockSpec((bm, bn), lambda i, j, k: (i, j)),
        out_shape=jax.ShapeDtypeStruct((M, N), out_dtype),
        compiler_params=_cparams("parallel", "parallel", "arbitrary"),
    )(*args)


HP = 4
BLK_SOFTMAX = 512
BLK_STICK = 256


def _t(x):
    return x.astype(F32).T.astype(MXU_DTYPE)


def _fill_transposed(dst_ref, src_ref, nb, blk):
    for j in range(nb):
        dst_ref[j] = _t(src_ref[j * blk:(j + 1) * blk, :])


def _smax_fwd_t(q, k, v, *, dk, blk, name):
    S = k.shape[0]
    nb = S // blk
    H = k.shape[1] // dk

    def body(q_ref, k_ref, v_ref, oT_ref, lse_ref, vT_ref):
        i = pl.program_id(1)

        @pl.when(i == 0)
        def _():
            _fill_transposed(vT_ref, v_ref, nb, blk)

        key = lax.broadcasted_iota(jnp.int32, (blk, blk), 0)
        qry = lax.broadcasted_iota(jnp.int32, (blk, blk), 1)
        qs = [_t(q_ref[:, h * dk:(h + 1) * dk]) for h in range(HP)]

        def tile(j, carry, masked):
            r0 = pl.multiple_of(j * blk, blk)
            ss = [jnp.dot(k_ref[pl.ds(r0, blk), h * dk:(h + 1) * dk], qs[h], preferred_element_type=F32)
                  for h in range(HP)]
            stats, pes = [], []
            for h in range(HP):
                m, l, _ = carry[h]
                s = jnp.where(key <= qry, ss[h], NEG_INF) if masked else ss[h]
                mn = jnp.maximum(m, jnp.max(s, axis=0, keepdims=True))
                a = jnp.exp(m - mn)
                pe = jnp.exp(s - mn)
                stats.append((mn, a * l + jnp.sum(pe, axis=0, keepdims=True), a))
                pes.append(pe.astype(MXU_DTYPE))
            pvs = [jnp.dot(vT_ref[j, h * HEAD_DIM:(h + 1) * HEAD_DIM, :], pes[h], preferred_element_type=F32)
                   for h in range(HP)]
            return tuple((stats[h][0], stats[h][1], stats[h][2] * carry[h][2] + pvs[h]) for h in range(HP))

        init = tuple((jnp.full((1, blk), NEG_INF, F32), jnp.zeros((1, blk), F32), jnp.zeros((HEAD_DIM, blk), F32))
                     for _ in range(HP))
        carry = lax.fori_loop(0, i, functools.partial(tile, masked=False), init)
        carry = tile(i, carry, True)
        for h in range(HP):
            m, l, acc = carry[h]
            oT_ref[h * HEAD_DIM:(h + 1) * HEAD_DIM, :] = acc / l
            lse_ref[h, 0] = m + jnp.log(l)

    return pl.pallas_call(
        body, name=name, grid=(H // HP, nb),
        in_specs=[pl.BlockSpec((blk, HP * dk), lambda p, i: (i, p)),
                  pl.BlockSpec((S, HP * dk), lambda p, i: (0, p)),
                  pl.BlockSpec((S, HP * HEAD_DIM), lambda p, i: (0, p))],
        out_specs=[pl.BlockSpec((HP * HEAD_DIM, blk), lambda p, i: (p, i)),
                   pl.BlockSpec((HP, 1, 1, blk), lambda p, i: (p, i, 0, 0))],
        out_shape=[jax.ShapeDtypeStruct((H * HEAD_DIM, S), F32), jax.ShapeDtypeStruct((H, nb, 1, blk), F32)],
        scratch_shapes=[pltpu.VMEM((nb, HP * HEAD_DIM, blk), MXU_DTYPE)],
        compiler_params=_cparams("arbitrary", "arbitrary"),
    )(q, k, v)


def _smax_bwd_t(q, k, v, dmix, dmixT, oT, lse, *, dk, dcb, qscale, blk, name):
    S = k.shape[0]
    nb = S // blk
    H = k.shape[1] // dk
    hd = HP * HEAD_DIM
    dcr = dcb * 128 // hd

    def body(q_ref, k_ref, v_ref, do_ref, doT_ref, oT_ref, lse_ref, dqT_ref, dk_ref, dv_ref, kT_ref):
        i = pl.program_id(1)

        @pl.when(i == 0)
        def _():
            dk_ref[...] = jnp.zeros_like(dk_ref)
            dv_ref[...] = jnp.zeros_like(dv_ref)
            _fill_transposed(kT_ref, k_ref, nb, blk)

        key = lax.broadcasted_iota(jnp.int32, (blk, blk), 0)
        qry = lax.broadcasted_iota(jnp.int32, (blk, blk), 1)
        per_head = []
        for h in range(HP):
            hs = slice(h * HEAD_DIM, (h + 1) * HEAD_DIM)
            doT = doT_ref[hs, :]
            per_head.append(dict(
                qT=_t(q_ref[:, h * dk:(h + 1) * dk]), q=q_ref[:, h * dk:(h + 1) * dk],
                doT=doT.astype(MXU_DTYPE), do=do_ref[:, hs].astype(MXU_DTYPE),
                delta=jnp.sum(doT * oT_ref[hs, :], axis=0, keepdims=True), lse=lse_ref[h, 0]))

        def tile(j, dqs, masked):
            r0 = pl.multiple_of(j * blk, blk)
            rows = pl.ds(r0, blk)
            ksl = [slice(h * dk, (h + 1) * dk) for h in range(HP)]
            hsl = [slice(h * HEAD_DIM, (h + 1) * HEAD_DIM) for h in range(HP)]
            ss = [jnp.dot(k_ref[rows, ksl[h]], per_head[h]["qT"], preferred_element_type=F32) for h in range(HP)]
            dps = [jnp.dot(v_ref[rows, hsl[h]], per_head[h]["doT"], preferred_element_type=F32) for h in range(HP)]
            prs, dss = [], []
            for h in range(HP):
                c = per_head[h]
                s = jnp.where(key <= qry, ss[h], NEG_INF) if masked else ss[h]
                pr = jnp.exp(s - c["lse"])
                dss.append((pr * (dps[h] - c["delta"])).astype(MXU_DTYPE))
                prs.append(pr.astype(MXU_DTYPE))
            for h in range(HP):
                dv_ref[rows, hsl[h]] += jnp.dot(prs[h], per_head[h]["do"], preferred_element_type=F32)
            for h in range(HP):
                dk_ref[rows, ksl[h]] += jnp.dot(dss[h], per_head[h]["q"], preferred_element_type=F32)
            return tuple(dqs[h] + jnp.dot(kT_ref[j, ksl[h], :], dss[h], preferred_element_type=F32) for h in range(HP))

        dqs = lax.fori_loop(0, i, functools.partial(tile, masked=False),
                            tuple(jnp.zeros((dk, blk), F32) for _ in range(HP)))
        dqs = tile(i, dqs, True)
        for h in range(HP):
            dqT_ref[h * dk:(h + 1) * dk, :] = dqs[h] * qscale

    return pl.pallas_call(
        body, name=name, grid=(H // HP, nb),
        in_specs=[pl.BlockSpec((blk, HP * dk), lambda p, i: (i, p)),
                  pl.BlockSpec((S, HP * dk), lambda p, i: (0, p)),
                  pl.BlockSpec((S, hd), lambda p, i: (0, p)),
                  pl.BlockSpec((blk, hd), lambda p, i: (i, dcr + p)),
                  pl.BlockSpec((hd, blk), lambda p, i: (dcr + p, i)),
                  pl.BlockSpec((hd, blk), lambda p, i: (p, i)),
                  pl.BlockSpec((HP, 1, 1, blk), lambda p, i: (p, i, 0, 0))],
        out_specs=[pl.BlockSpec((HP * dk, blk), lambda p, i: (p, i)),
                   pl.BlockSpec((S, HP * dk), lambda p, i: (0, p)),
                   pl.BlockSpec((S, hd), lambda p, i: (0, p))],
        out_shape=[jax.ShapeDtypeStruct((H * dk, S), F32), jax.ShapeDtypeStruct((S, H * dk), F32),
                   jax.ShapeDtypeStruct((S, H * HEAD_DIM), F32)],
        scratch_shapes=[pltpu.VMEM((nb, HP * dk, blk), MXU_DTYPE)],
        compiler_params=_cparams("arbitrary", "arbitrary"),
    )(q, k, v, dmix, dmixT, oT, lse)


def _log1m_beta(z):
    return -(jnp.maximum(z, 0.0) + jnp.log(1.0 + jnp.exp(-jnp.abs(z))))


def _dot01_left(m01, x, parts=2):
    acc = None
    rem = x
    for _ in range(parts):
        part = rem.astype(MXU_DTYPE)
        rem = rem - part.astype(F32)
        t = jnp.dot(m01, part, preferred_element_type=F32)
        acc = t if acc is None else acc + t
    return acc


def _sb_fwd_t(h_att, *, blk, name):
    S = h_att.shape[0]
    nb = S // blk
    hd = HP * HEAD_DIM
    qcb, kcb, vcb = COL_SQ // hd, COL_SK // hd, COL_SV // hd

    def body(q_ref, k_ref, v_ref, oT_ref, lt_ref, vT_ref):
        i = pl.program_id(1)

        @pl.when(i == 0)
        def _():
            _fill_transposed(vT_ref, v_ref, nb, blk)

        key = lax.broadcasted_iota(jnp.int32, (blk, blk), 0)
        qry = lax.broadcasted_iota(jnp.int32, (blk, blk), 1)
        strict = key < qry
        later = (qry > key).astype(MXU_DTYPE)
        qs = [_t(q_ref[:, h * HEAD_DIM:(h + 1) * HEAD_DIM]) for h in range(HP)]

        def tile(j, carry, mask):
            r0 = pl.multiple_of(j * blk, blk)
            hsl = [slice(h * HEAD_DIM, (h + 1) * HEAD_DIM) for h in range(HP)]
            zs = [jnp.dot(k_ref[pl.ds(r0, blk), hsl[h]], qs[h], preferred_element_type=F32) for h in range(HP)]
            lbs = []
            for h in range(HP):
                lb = _log1m_beta(zs[h])
                lbs.append(lb if mask is None else jnp.where(mask, lb, 0.0))
            sums = [_dot01_left(later, lbs[h]) for h in range(HP)]
            probs = []
            for h in range(HP):
                lt_ref[h, 0, j] = carry[h][0]
                a = jnp.exp(zs[h] + lbs[h] + sums[h] + carry[h][0])
                probs.append((a if mask is None else jnp.where(mask, a, 0.0)).astype(MXU_DTYPE))
            pvs = [jnp.dot(vT_ref[j, hsl[h], :], probs[h], preferred_element_type=F32) for h in range(HP)]
            return tuple((carry[h][0] + jnp.sum(lbs[h], axis=0, keepdims=True), carry[h][1] + pvs[h]) for h in range(HP))

        init = tuple((jnp.zeros((1, blk), F32), jnp.zeros((HEAD_DIM, blk), F32)) for _ in range(HP))
        carry = tile(i, init, strict)
        carry = lax.fori_loop(0, i, lambda jj, c: tile(i - 1 - jj, c, None), carry)
        for h in range(HP):
            oT_ref[h * HEAD_DIM:(h + 1) * HEAD_DIM, :] = carry[h][1]

    return pl.pallas_call(
        body, name=name, grid=(4 // HP, nb),
        in_specs=[pl.BlockSpec((blk, hd), lambda p, i: (i, qcb + p)),
                  pl.BlockSpec((S, hd), lambda p, i: (0, kcb + p)),
                  pl.BlockSpec((S, hd), lambda p, i: (0, vcb + p))],
        out_specs=[pl.BlockSpec((hd, blk), lambda p, i: (p, i)),
                   pl.BlockSpec((HP, 1, nb, 1, blk), lambda p, i: (p, i, 0, 0, 0))],
        out_shape=[jax.ShapeDtypeStruct((GROUP_WIDTH, S), F32), jax.ShapeDtypeStruct((4, nb, nb, 1, blk), F32)],
        scratch_shapes=[pltpu.VMEM((nb, hd, blk), MXU_DTYPE)],
        compiler_params=_cparams("arbitrary", "arbitrary"),
    )(h_att, h_att, h_att)


def _sb_bwd_t(h_att, dmix, dmixT, later_sums, *, dcb, qscale, blk, name):
    S = h_att.shape[0]
    nb = S // blk
    hd = HP * HEAD_DIM
    qcb, kcb, vcb = COL_SQ // hd, COL_SK // hd, COL_SV // hd
    dcr = dcb * 128 // hd

    def body(q_ref, k_ref, v_ref, do_ref, doT_ref, lt_ref, dqT_ref, dk_ref, dv_ref, kT_ref):
        i = pl.program_id(1)

        @pl.when(i == 0)
        def _():
            dk_ref[...] = jnp.zeros_like(dk_ref)
            dv_ref[...] = jnp.zeros_like(dv_ref)
            _fill_transposed(kT_ref, k_ref, nb, blk)

        key = lax.broadcasted_iota(jnp.int32, (blk, blk), 0)
        qry = lax.broadcasted_iota(jnp.int32, (blk, blk), 1)
        strict = key < qry
        later = (qry > key).astype(MXU_DTYPE)
        before = (qry < key).astype(MXU_DTYPE)
        per_head = []
        for h in range(HP):
            hs = slice(h * HEAD_DIM, (h + 1) * HEAD_DIM)
            per_head.append(dict(qT=_t(q_ref[:, hs]), q=q_ref[:, hs], doT=doT_ref[hs, :].astype(MXU_DTYPE),
                                 do=do_ref[:, hs].astype(MXU_DTYPE)))

        def tile(j, carry, mask):
            r0 = pl.multiple_of(j * blk, blk)
            rows = pl.ds(r0, blk)
            hsl = [slice(h * HEAD_DIM, (h + 1) * HEAD_DIM) for h in range(HP)]
            zs = [jnp.dot(k_ref[rows, hsl[h]], per_head[h]["qT"], preferred_element_type=F32) for h in range(HP)]
            das = [jnp.dot(v_ref[rows, hsl[h]], per_head[h]["doT"], preferred_element_type=F32) for h in range(HP)]
            lbs = []
            for h in range(HP):
                lb = _log1m_beta(zs[h])
                lbs.append(lb if mask is None else jnp.where(mask, lb, 0.0))
            sums = [_dot01_left(later, lbs[h]) for h in range(HP)]
            probs, gs = [], []
            for h in range(HP):
                a = jnp.exp(zs[h] + lbs[h] + sums[h] + lt_ref[h, 0, j])
                a = a if mask is None else jnp.where(mask, a, 0.0)
                gs.append(das[h] * a)
                probs.append(a.astype(MXU_DTYPE))
            for h in range(HP):
                dv_ref[rows, hsl[h]] += jnp.dot(probs[h], per_head[h]["do"], preferred_element_type=F32)
            es = [_dot01_left(before, gs[h]) for h in range(HP)]
            dzs = []
            for h in range(HP):
                dz = gs[h] * jnp.exp(lbs[h]) - (carry[h][0] + es[h]) * jnp.exp(zs[h] + lbs[h])
                dzs.append((dz if mask is None else jnp.where(mask, dz, 0.0)).astype(MXU_DTYPE))
            for h in range(HP):
                dk_ref[rows, hsl[h]] += jnp.dot(dzs[h], per_head[h]["q"], preferred_element_type=F32)
            return tuple((carry[h][0] + jnp.sum(gs[h], axis=0, keepdims=True),
                          carry[h][1] + jnp.dot(kT_ref[j, hsl[h], :], dzs[h], preferred_element_type=F32))
                         for h in range(HP))

        init = tuple((jnp.zeros((1, blk), F32), jnp.zeros((HEAD_DIM, blk), F32)) for _ in range(HP))
        carry = lax.fori_loop(0, i, lambda j, c: tile(j, c, None), init)
        carry = tile(i, carry, strict)
        for h in range(HP):
            dqT_ref[h * HEAD_DIM:(h + 1) * HEAD_DIM, :] = carry[h][1] * qscale

    return pl.pallas_call(
        body, name=name, grid=(4 // HP, nb),
        in_specs=[pl.BlockSpec((blk, hd), lambda p, i: (i, qcb + p)),
                  pl.BlockSpec((S, hd), lambda p, i: (0, kcb + p)),
                  pl.BlockSpec((S, hd), lambda p, i: (0, vcb + p)),
                  pl.BlockSpec((blk, hd), lambda p, i: (i, dcr + p)),
                  pl.BlockSpec((hd, blk), lambda p, i: (dcr + p, i)),
                  pl.BlockSpec((HP, 1, nb, 1, blk), lambda p, i: (p, i, 0, 0, 0))],
        out_specs=[pl.BlockSpec((hd, blk), lambda p, i: (p, i)),
                   pl.BlockSpec((S, hd), lambda p, i: (0, p)),
                   pl.BlockSpec((S, hd), lambda p, i: (0, p))],
        out_shape=[jax.ShapeDtypeStruct((GROUP_WIDTH, S), F32), jax.ShapeDtypeStruct((S, GROUP_WIDTH), F32),
                   jax.ShapeDtypeStruct((S, GROUP_WIDTH), F32)],
        scratch_shapes=[pltpu.VMEM((nb, hd, blk), MXU_DTYPE)],
        compiler_params=_cparams("arbitrary", "arbitrary"),
    )(h_att, h_att, h_att, dmix, dmixT, later_sums)


SWA_SUB = 4


def _swa_sub_blocks(S):
    return min(SWA_SUB, S // WINDOW)


def _swa_tiles(n, sub):
    tiles = []
    for b in range(sub):
        start = pl.multiple_of(jnp.maximum(n * sub + b - 1, 0) * WINDOW, WINDOW)
        tiles += [(b, h, slice(b * WINDOW, (b + 1) * WINDOW), start) for h in range(4)]
    return tiles


def _swa_scores(q_ref, k_ref, n, sub, tile):
    b, h, qrows, start = tile
    g = h // 2
    kb = k_ref[pl.ds(start, 2 * WINDOW), g * HEAD_DIM:(g + 1) * HEAD_DIM]
    s = lax.dot_general(q_ref[qrows, h * HEAD_DIM:(h + 1) * HEAD_DIM], kb, NT,
                        preferred_element_type=F32) * (HEAD_DIM ** -0.5)
    dist = ((n * sub + b) * WINDOW + lax.broadcasted_iota(jnp.int32, (WINDOW, 2 * WINDOW), 0)
            - start - lax.broadcasted_iota(jnp.int32, (WINDOW, 2 * WINDOW), 1))
    s = s - SWA_SLOPES[h] * dist.astype(F32)
    valid = (dist >= 0) & (dist < WINDOW)
    return jnp.where(valid, s, NEG_INF), kb


def _swa_fwd(h_att, sinks, *, name):
    S = h_att.shape[0]
    sub = _swa_sub_blocks(S)
    rows = sub * WINDOW
    qcb, kcb, vcb = COL_WQ // 256, COL_WK // 128, COL_WV // 128

    def body(sink_ref, q_ref, k_ref, v_ref, o_ref, lse_ref):
        n = pl.program_id(0)
        tiles = _swa_tiles(n, sub)
        scores = [_swa_scores(q_ref, k_ref, n, sub, t)[0] for t in tiles]
        probs = []
        for (b, h, qrows, start), s in zip(tiles, scores):
            sink = sink_ref[h]
            m = jnp.maximum(jnp.max(s, axis=1, keepdims=True), sink)
            e = jnp.exp(s - m)
            den = jnp.sum(e, axis=1, keepdims=True) + jnp.exp(sink - m)
            probs.append((e / den).astype(MXU_DTYPE))
            lse_ref[h, qrows] = m + jnp.log(den)
        for (b, h, qrows, start), p in zip(tiles, probs):
            vb = v_ref[pl.ds(start, 2 * WINDOW), (h // 2) * HEAD_DIM:(h // 2 + 1) * HEAD_DIM]
            o_ref[qrows, h * HEAD_DIM:(h + 1) * HEAD_DIM] = jnp.dot(p, vb, preferred_element_type=F32)

    return pl.pallas_call(
        body, name=name, grid=(S // rows,),
        in_specs=[pl.BlockSpec(memory_space=pltpu.SMEM),
                  pl.BlockSpec((rows, 256), lambda n: (n, qcb)),
                  pl.BlockSpec((S, 128), lambda n: (0, kcb)),
                  pl.BlockSpec((S, 128), lambda n: (0, vcb))],
        out_specs=[pl.BlockSpec((rows, 256), lambda n: (n, 0)), pl.BlockSpec((4, rows, 1), lambda n: (0, n, 0))],
        out_shape=[jax.ShapeDtypeStruct((S, GROUP_WIDTH), F32), jax.ShapeDtypeStruct((4, S, 1), F32)],
        compiler_params=_cparams("arbitrary"),
    )(sinks, h_att, h_att, h_att)


def _swa_bwd(h_att, sinks, dmix, o_arr, lse, *, dcb, name):
    S = h_att.shape[0]
    sub = _swa_sub_blocks(S)
    rows = sub * WINDOW
    qcb, kcb, vcb = COL_WQ // 256, COL_WK // 128, COL_WV // 128

    def body(sink_ref, q_ref, k_ref, v_ref, do_ref, o_ref, lse_ref, dq_ref, dk_ref, dv_ref, dsink_ref):
        n = pl.program_id(0)

        @pl.when(n == 0)
        def _():
            dk_ref[...] = jnp.zeros_like(dk_ref)
            dv_ref[...] = jnp.zeros_like(dv_ref)
            dsink_ref[...] = jnp.zeros_like(dsink_ref)

        tiles = _swa_tiles(n, sub)
        hsl = [slice(h * HEAD_DIM, (h + 1) * HEAD_DIM) for h in range(4)]
        gsl = [slice(g * HEAD_DIM, (g + 1) * HEAD_DIM) for g in range(2)]
        scale = HEAD_DIM ** -0.5
        sk = [_swa_scores(q_ref, k_ref, n, sub, t) for t in tiles]
        dobs = [do_ref[qrows, hsl[h]].astype(MXU_DTYPE) for b, h, qrows, start in tiles]
        dps = [lax.dot_general(dob, v_ref[pl.ds(start, 2 * WINDOW), gsl[h // 2]], NT, preferred_element_type=F32)
               for (b, h, qrows, start), dob in zip(tiles, dobs)]
        prs, dss = [], []
        for t, (b, h, qrows, start) in enumerate(tiles):
            lse_h = lse_ref[h, qrows]
            pr = jnp.exp(sk[t][0] - lse_h)
            delta = jnp.sum(do_ref[qrows, hsl[h]] * o_ref[qrows, hsl[h]], axis=1, keepdims=True)
            dss.append((pr * (dps[t] - delta)).astype(MXU_DTYPE))
            prs.append(pr.astype(MXU_DTYPE))
            dsink_ref[h:h + 1, :] += jnp.zeros((1, 128), F32) - jnp.sum(jnp.exp(sink_ref[h] - lse_h) * delta)
        for t, (b, h, qrows, start) in enumerate(tiles):
            dq_ref[qrows, hsl[h]] = jnp.dot(dss[t], sk[t][1], preferred_element_type=F32) * scale
        for b in range(sub):
            for g in range(2):
                t0, t1 = 4 * b + 2 * g, 4 * b + 2 * g + 1
                qrows, krows = tiles[t0][2], pl.ds(tiles[t0][3], 2 * WINDOW)
                dk_ref[krows, gsl[g]] += (
                    lax.dot_general(dss[t0], q_ref[qrows, hsl[2 * g]], TN, preferred_element_type=F32)
                    + lax.dot_general(dss[t1], q_ref[qrows, hsl[2 * g + 1]], TN, preferred_element_type=F32)) * scale
                dv_ref[krows, gsl[g]] += (lax.dot_general(prs[t0], dobs[t0], TN, preferred_element_type=F32)
                                          + lax.dot_general(prs[t1], dobs[t1], TN, preferred_element_type=F32))

    return pl.pallas_call(
        body, name=name, grid=(S // rows,),
        in_specs=[pl.BlockSpec(memory_space=pltpu.SMEM),
                  pl.BlockSpec((rows, 256), lambda n: (n, qcb)),
                  pl.BlockSpec((S, 128), lambda n: (0, kcb)),
                  pl.BlockSpec((S, 128), lambda n: (0, vcb)),
                  pl.BlockSpec((rows, 256), lambda n: (n, dcb)),
                  pl.BlockSpec((rows, 256), lambda n: (n, 0)),
                  pl.BlockSpec((4, rows, 1), lambda n: (0, n, 0))],
        out_specs=[pl.BlockSpec((rows, 256), lambda n: (n, 0)),
                   pl.BlockSpec((S, 128), lambda n: (0, 0)),
                   pl.BlockSpec((S, 128), lambda n: (0, 0)),
                   pl.BlockSpec((4, 128), lambda n: (0, 0))],
        out_shape=[jax.ShapeDtypeStruct((S, GROUP_WIDTH), F32), jax.ShapeDtypeStruct((S, 128), F32),
                   jax.ShapeDtypeStruct((S, 128), F32), jax.ShapeDtypeStruct((4, 128), F32)],
        compiler_params=_cparams("arbitrary"),
    )(sinks, h_att, h_att, h_att, dmix, o_arr, lse)


def _tri(n, incl, upper):
    r = lax.broadcasted_iota(jnp.int32, (n, n), 0)
    c = lax.broadcasted_iota(jnp.int32, (n, n), 1)
    if upper:
        m = (r <= c) if incl else (r < c)
    else:
        m = (r >= c) if incl else (r > c)
    return m.astype(MXU_DTYPE)


def _fox_gate_fwd(fg, b_f, *, name):
    _, R, _ = fg.shape

    def body(b_ref, fg_ref, pos_ref, neg_ref):
        up_incl = _tri(128, True, True)
        ones = jnp.ones((128, 128), MXU_DTYPE)
        for h in range(4):
            z = fg_ref[h] + b_ref[h]
            logf = jnp.minimum(z, 0.0) - jnp.log(1.0 + jnp.exp(-jnp.abs(z)))
            within = _dot01(logf, up_incl, parts=3)
            totals = _dot01(logf, ones, parts=3)
            rem = within + _rows_other(totals, R, after=False)
            for part in range(3):
                piece = rem.astype(MXU_DTYPE)
                rem = rem - piece.astype(F32)
                pos_ref[h, part] = piece
                neg_ref[h, part] = -piece

    shape = (4, 3) + fg.shape[1:]
    return pl.pallas_call(
        body, name=name,
        in_specs=[pl.BlockSpec(memory_space=pltpu.SMEM), pl.BlockSpec(memory_space=pltpu.VMEM)],
        out_specs=[pl.BlockSpec(memory_space=pltpu.VMEM)] * 2,
        out_shape=[jax.ShapeDtypeStruct(shape, MXU_DTYPE)] * 2,
    )(b_f, fg)


def _rows_other(totals, n, after):
    r = lax.broadcasted_iota(jnp.int32, (n, n), 0)
    c = lax.broadcasted_iota(jnp.int32, (n, n), 1)
    m = ((c > r) if after else (c < r)).astype(MXU_DTYPE)
    acc = None
    rem = totals
    for _ in range(3):
        part = rem.astype(MXU_DTYPE)
        rem = rem - part.astype(F32)
        t = jnp.dot(m, part, preferred_element_type=F32)
        acc = t if acc is None else acc + t
    return acc


def _fox_gate_bwd(fg, b_f, dcum_k, dcum_q, *, q_unscale, name):
    _, R, _ = fg.shape

    def body(b_ref, fg_ref, dck_ref, dcq_ref, dfg_ref, db_ref):
        low_incl = _tri(128, True, False)
        ones = jnp.ones((128, 128), MXU_DTYPE)
        for h in range(4):
            dc = dcq_ref[h] * q_unscale - dck_ref[h]
            dlogf = _dot01(dc, low_incl, parts=3) + _rows_other(_dot01(dc, ones, parts=3), R, after=True)
            z = fg_ref[h] + b_ref[h]
            dz = dlogf * jnp.exp(jnp.minimum(-z, 0.0) - jnp.log(1.0 + jnp.exp(-jnp.abs(z))))
            dfg_ref[h] = dz
            db_ref[h:h + 1, :] = jnp.zeros((1, 128), F32) + jnp.sum(dz)

    return pl.pallas_call(
        body, name=name,
        in_specs=[pl.BlockSpec(memory_space=pltpu.SMEM)] + [pl.BlockSpec(memory_space=pltpu.VMEM)] * 3,
        out_specs=[pl.BlockSpec(memory_space=pltpu.VMEM), pl.BlockSpec(memory_space=pltpu.VMEM)],
        out_shape=[jax.ShapeDtypeStruct(fg.shape, F32), jax.ShapeDtypeStruct((4, 128), F32)],
    )(b_f, fg, dcum_k, dcum_q)


def _rope_rot(transpose):
    r = lax.broadcasted_iota(jnp.int32, (MLA_PAD, MLA_PAD), 0)
    c = lax.broadcasted_iota(jnp.int32, (MLA_PAD, MLA_PAD), 1)
    if transpose:
        r, c = c, r
    half = MLA_ROPE // 2
    lo, mid, hi = HEAD_DIM, HEAD_DIM + half, HEAD_DIM + MLA_ROPE
    minus = (c >= lo) & (c < mid) & (r == c + half)
    plus = (c >= mid) & (c < hi) & (r == c - half)
    return jnp.where(plus, 1.0, jnp.where(minus, -1.0, 0.0)).astype(MXU_DTYPE)


def _rope_lanes():
    lane = lax.broadcasted_iota(jnp.int32, (1, MLA_PAD), 1)
    return ((lane >= HEAD_DIM) & (lane < HEAD_DIM + MLA_ROPE)).astype(F32)


def _rms(x, g, eps=1e-6):
    r = lax.rsqrt(jnp.mean(x * x, axis=-1, keepdims=True) + eps)
    return x * r * g, r


def _rms_bwd(dy, x, r, g):
    xh = x * r
    dxh = dy * g
    dx = r * (dxh - xh * jnp.mean(dxh * xh, axis=-1, keepdims=True))
    return dx, dy * xh


def _mla_prep_fwd(lat, g_q, g_kv, wuq, wuk, wuv, cosm, sinm, *, bs, name):
    S = lat.shape[0]

    def body(lat_ref, gq_ref, gkv_ref, wuq_ref, wuk_ref, wuv_ref, cos_ref, sin_ref, q_ref, k_ref, v_ref):
        rot = _rope_rot(False)
        cosm_, sinm_ = cos_ref[...], sin_ref[...]
        nq, _ = _rms(lat_ref[:, 0:MLA_Q_RANK], gq_ref[...])
        nkv, _ = _rms(lat_ref[:, MLA_Q_RANK:MLA_Q_RANK + MLA_KV_RANK], gkv_ref[...])
        qlat = jnp.dot(nq.astype(MXU_DTYPE), wuq_ref[...], preferred_element_type=F32)
        klat = jnp.dot(nkv.astype(MXU_DTYPE), wuk_ref[...], preferred_element_type=F32)
        v_ref[...] = jnp.dot(nkv.astype(MXU_DTYPE), wuv_ref[...], preferred_element_type=F32).astype(v_ref.dtype)
        krb = lat_ref[:, 384:512]
        kr = krb * (cosm_ * _rope_lanes()) + _dot01(krb, rot, parts=3) * sinm_
        for h in range(4):
            sl = slice(h * MLA_PAD, (h + 1) * MLA_PAD)
            qh = qlat[:, sl]
            q_ref[:, sl] = ((qh * cosm_ + _dot01(qh, rot, parts=3) * sinm_) * (MLA_QK ** -0.5)).astype(q_ref.dtype)
            k_ref[:, sl] = (klat[:, sl] + kr).astype(k_ref.dtype)

    full = lambda a: pl.BlockSpec(a.shape, lambda i: (0,) * a.ndim)
    return pl.pallas_call(
        body, name=name, grid=(S // bs,),
        in_specs=[pl.BlockSpec((bs, LAT_W), lambda i: (i, 0)), full(g_q), full(g_kv), full(wuq), full(wuk), full(wuv),
                  pl.BlockSpec((bs, MLA_PAD), lambda i: (i, 0)), pl.BlockSpec((bs, MLA_PAD), lambda i: (i, 0))],
        out_specs=[pl.BlockSpec((bs, 512), lambda i: (i, 0)), pl.BlockSpec((bs, 512), lambda i: (i, 0)),
                   pl.BlockSpec((bs, 256), lambda i: (i, 0))],
        out_shape=[jax.ShapeDtypeStruct((S, 512), MXU_DTYPE), jax.ShapeDtypeStruct((S, 512), MXU_DTYPE),
                   jax.ShapeDtypeStruct((S, 256), MXU_DTYPE)],
        compiler_params=_cparams("parallel"),
    )(lat, g_q, g_kv, wuq, wuk, wuv, cosm, sinm)


def _mla_prep_bwd(lat, g_q, g_kv, wuq, wuk, wuv, cosm, sinm, dq, dk, dv, *, bs, name):
    S = lat.shape[0]

    def body(lat_ref, gq_ref, gkv_ref, wuq_ref, wuk_ref, wuv_ref, cos_ref, sin_ref, dq_ref, dk_ref, dv_ref,
             dlat_ref, dwuq_ref, dwuk_ref, dwuv_ref, dgq_ref, dgkv_ref):
        @pl.when(pl.program_id(0) == 0)
        def _():
            for r in (dwuq_ref, dwuk_ref, dwuv_ref, dgq_ref, dgkv_ref):
                r[...] = jnp.zeros_like(r)

        rot_t = _rope_rot(True)
        cosm_, sinm_ = cos_ref[...], sin_ref[...]
        cq = lat_ref[:, 0:MLA_Q_RANK]
        ckv = lat_ref[:, MLA_Q_RANK:MLA_Q_RANK + MLA_KV_RANK]
        nq, rq = _rms(cq, gq_ref[...])
        nkv, rkv = _rms(ckv, gkv_ref[...])
        nqb, nkvb = nq.astype(MXU_DTYPE), nkv.astype(MXU_DTYPE)

        dqlat = []
        dkr = jnp.zeros((bs, MLA_PAD), F32)
        for h in range(4):
            sl = slice(h * MLA_PAD, (h + 1) * MLA_PAD)
            dqh = dq_ref[sl, :].T
            dqlat.append(dqh * cosm_ + _dot01(dqh * sinm_, rot_t, parts=3))
            dkr = dkr + dk_ref[:, sl]
        dqlat = jnp.concatenate(dqlat, axis=1).astype(MXU_DTYPE)
        dkb = dk_ref[...].astype(MXU_DTYPE)
        dvb = dv_ref[...].astype(MXU_DTYPE)

        dnq = lax.dot_general(dqlat, wuq_ref[...], NT, preferred_element_type=F32)
        dnkv = (lax.dot_general(dkb, wuk_ref[...], NT, preferred_element_type=F32)
                + lax.dot_general(dvb, wuv_ref[...], NT, preferred_element_type=F32))
        dwuq_ref[...] += lax.dot_general(nqb, dqlat, TN, preferred_element_type=F32)
        dwuk_ref[...] += lax.dot_general(nkvb, dkb, TN, preferred_element_type=F32)
        dwuv_ref[...] += lax.dot_general(nkvb, dvb, TN, preferred_element_type=F32)
        dcq, tq = _rms_bwd(dnq, cq, rq, gq_ref[...])
        dckv, tkv = _rms_bwd(dnkv, ckv, rkv, gkv_ref[...])
        dgq_ref[...] += jnp.sum(tq, axis=0, keepdims=True)
        dgkv_ref[...] += jnp.sum(tkv, axis=0, keepdims=True)
        dlat_ref[:, 0:MLA_Q_RANK] = dcq.astype(dlat_ref.dtype)
        dlat_ref[:, MLA_Q_RANK:MLA_Q_RANK + MLA_KV_RANK] = dckv.astype(dlat_ref.dtype)
        dkrb = dkr * (cosm_ * _rope_lanes()) + _dot01(dkr * sinm_, rot_t, parts=3)
        dlat_ref[:, 384:512] = dkrb.astype(dlat_ref.dtype)

    full = lambda a: pl.BlockSpec(a.shape, lambda i: (0,) * a.ndim)
    row = lambda w: pl.BlockSpec((bs, w), lambda i: (i, 0))
    acc = lambda *shape: pl.BlockSpec(shape, lambda i: (0,) * len(shape))
    return pl.pallas_call(
        body, name=name, grid=(S // bs,),
        in_specs=[row(LAT_W), full(g_q), full(g_kv), full(wuq), full(wuk), full(wuv), row(MLA_PAD), row(MLA_PAD),
                  pl.BlockSpec((512, bs), lambda i: (0, i)), row(512), row(256)],
        out_specs=[row(512), acc(256, 512), acc(128, 512), acc(128, 256), acc(1, 256), acc(1, 128)],
        out_shape=[jax.ShapeDtypeStruct((S, 512), MXU_DTYPE), jax.ShapeDtypeStruct((256, 512), F32),
                   jax.ShapeDtypeStruct((128, 512), F32), jax.ShapeDtypeStruct((128, 256), F32),
                   jax.ShapeDtypeStruct((1, 256), F32), jax.ShapeDtypeStruct((1, 128), F32)],
        compiler_params=_cparams("arbitrary"),
    )(lat, g_q, g_kv, wuq, wuk, wuv, cosm, sinm, dq, dk, dv)


def _row_spec(bs, w):
    return pl.BlockSpec((bs, w), lambda i: (i, 0))


def _vec_spec(w):
    return pl.BlockSpec((1, w), lambda i: (0, 0))


def _mix_specs(bs):
    return [pl.BlockSpec((GROUP_WIDTH, bs), lambda i: (0, i))] * 3 + [_row_spec(bs, GROUP_WIDTH)]


def _mix_groups(a_ref, b_ref, c_ref, d_ref):
    return [a_ref[...].T, b_ref[...].T, c_ref[...].T, d_ref[...]]


def _gnorm_fwd(outs, g, *, bs, name):
    S = outs[3].shape[0]

    def body(a_ref, b_ref, c_ref, d_ref, g_ref, o_ref, oT_ref):
        for k, x in enumerate(_mix_groups(a_ref, b_ref, c_ref, d_ref)):
            sl = slice(k * GROUP_WIDTH, (k + 1) * GROUP_WIDTH)
            y, _ = _rms(x, g_ref[:, sl])
            o_ref[:, sl] = y.astype(o_ref.dtype)
            oT_ref[sl, :] = y.T.astype(oT_ref.dtype)

    return pl.pallas_call(
        body, name=name, grid=(S // bs,),
        in_specs=_mix_specs(bs) + [_vec_spec(D_MODEL)],
        out_specs=[_row_spec(bs, D_MODEL), pl.BlockSpec((D_MODEL, bs), lambda i: (0, i))],
        out_shape=[jax.ShapeDtypeStruct((S, D_MODEL), MXU_DTYPE), jax.ShapeDtypeStruct((D_MODEL, S), MXU_DTYPE)],
        compiler_params=_cparams("parallel"),
    )(*outs, g)


def _gnorm_bwd(dgn, outs, g, *, bs, name):
    S = dgn.shape[0]

    def body(dgn_ref, a_ref, b_ref, c_ref, d_ref, g_ref, dmix_ref, dmixT_ref, dg_ref):
        @pl.when(pl.program_id(0) == 0)
        def _():
            dg_ref[...] = jnp.zeros_like(dg_ref)

        for k, x in enumerate(_mix_groups(a_ref, b_ref, c_ref, d_ref)):
            sl = slice(k * GROUP_WIDTH, (k + 1) * GROUP_WIDTH)
            _, r = _rms(x, g_ref[:, sl])
            dx, t = _rms_bwd(dgn_ref[:, sl], x, r, g_ref[:, sl])
            dmix_ref[:, sl] = dx
            dmixT_ref[sl, :] = dx.T
            dg_ref[:, sl] += jnp.sum(t, axis=0, keepdims=True)

    return pl.pallas_call(
        body, name=name, grid=(S // bs,),
        in_specs=[_row_spec(bs, D_MODEL)] + _mix_specs(bs) + [_vec_spec(D_MODEL)],
        out_specs=[_row_spec(bs, D_MODEL), pl.BlockSpec((D_MODEL, bs), lambda i: (0, i)), _vec_spec(D_MODEL)],
        out_shape=[jax.ShapeDtypeStruct((S, D_MODEL), F32), jax.ShapeDtypeStruct((D_MODEL, S), F32),
                   jax.ShapeDtypeStruct((1, D_MODEL), F32)],
        compiler_params=_cparams("arbitrary"),
    )(dgn, *outs, g)


def _ln_fwd(u, g, b, *, bs, name):
    S = u.shape[0]

    def body(u_ref, g_ref, b_ref, y_ref, yb_ref, ybT_ref, xh_ref, rs_ref):
        x = u_ref[...]
        mu = jnp.mean(x, axis=-1, keepdims=True)
        xc = x - mu
        rs = lax.rsqrt(jnp.mean(xc * xc, axis=-1, keepdims=True) + 1e-5)
        xh = xc * rs
        y = xh * g_ref[...] + b_ref[...]
        y_ref[...] = y
        yb_ref[...] = y.astype(yb_ref.dtype)
        ybT_ref[...] = y.T.astype(ybT_ref.dtype)
        xh_ref[...] = xh
        rs_ref[...] = rs

    return pl.pallas_call(
        body, name=name, grid=(S // bs,),
        in_specs=[_row_spec(bs, D_MODEL), _vec_spec(D_MODEL), _vec_spec(D_MODEL)],
        out_specs=[_row_spec(bs, D_MODEL), _row_spec(bs, D_MODEL), pl.BlockSpec((D_MODEL, bs), lambda i: (0, i)),
                   _row_spec(bs, D_MODEL), _row_spec(bs, 1)],
        out_shape=[jax.ShapeDtypeStruct((S, D_MODEL), F32), jax.ShapeDtypeStruct((S, D_MODEL), MXU_DTYPE),
                   jax.ShapeDtypeStruct((D_MODEL, S), MXU_DTYPE), jax.ShapeDtypeStruct((S, D_MODEL), F32),
                   jax.ShapeDtypeStruct((S, 1), F32)],
        compiler_params=_cparams("parallel"),
    )(u, g, b)


def _ln_bwd(dy, xh, rs, g, *, bs, name):
    S = dy.shape[0]

    def body(dy_ref, xh_ref, rs_ref, g_ref, du_ref, dub_ref, dg_ref, db_ref):
        @pl.when(pl.program_id(0) == 0)
        def _():
            dg_ref[...] = jnp.zeros_like(dg_ref)
            db_ref[...] = jnp.zeros_like(db_ref)

        dy_, xh_ = dy_ref[...], xh_ref[...]
        dxh = dy_ * g_ref[...]
        du = rs_ref[...] * (dxh - jnp.mean(dxh, axis=-1, keepdims=True)
                            - xh_ * jnp.mean(dxh * xh_, axis=-1, keepdims=True))
        du_ref[...] = du
        dub_ref[...] = du.astype(dub_ref.dtype)
        dg_ref[...] += jnp.sum(dy_ * xh_, axis=0, keepdims=True)
        db_ref[...] += jnp.sum(dy_, axis=0, keepdims=True)

    return pl.pallas_call(
        body, name=name, grid=(S // bs,),
        in_specs=[_row_spec(bs, D_MODEL), _row_spec(bs, D_MODEL), _row_spec(bs, 1), _vec_spec(D_MODEL)],
        out_specs=[_row_spec(bs, D_MODEL), _row_spec(bs, D_MODEL), _vec_spec(D_MODEL), _vec_spec(D_MODEL)],
        out_shape=[jax.ShapeDtypeStruct((S, D_MODEL), F32), jax.ShapeDtypeStruct((S, D_MODEL), MXU_DTYPE),
                   jax.ShapeDtypeStruct((1, D_MODEL), F32), jax.ShapeDtypeStruct((1, D_MODEL), F32)],
        compiler_params=_cparams("arbitrary"),
    )(dy, xh, rs, g)


def _swiglu_fwd(gu, *, bs, name):
    S = gu.shape[0]

    def body(gu_ref, a_ref, aT_ref):
        gt = gu_ref[:, :D_FF]
        a = gt / (1.0 + jnp.exp(-gt)) * gu_ref[:, D_FF:]
        a_ref[...] = a.astype(a_ref.dtype)
        aT_ref[...] = a.T.astype(aT_ref.dtype)

    return pl.pallas_call(
        body, name=name, grid=(S // bs,),
        in_specs=[_row_spec(bs, 2 * D_FF)],
        out_specs=[_row_spec(bs, D_FF), pl.BlockSpec((D_FF, bs), lambda i: (0, i))],
        out_shape=[jax.ShapeDtypeStruct((S, D_FF), MXU_DTYPE), jax.ShapeDtypeStruct((D_FF, S), MXU_DTYPE)],
        compiler_params=_cparams("parallel"),
    )(gu)


def _swiglu_bwd(da, gu, *, bs, name):
    S = gu.shape[0]

    def body(da_ref, gu_ref, dgu_ref):
        gt, da_ = gu_ref[:, :D_FF], da_ref[...]
        sg = 1.0 / (1.0 + jnp.exp(-gt))
        silu = gt * sg
        dgu_ref[:, :D_FF] = (da_ * gu_ref[:, D_FF:] * (sg + silu * (1.0 - sg))).astype(dgu_ref.dtype)
        dgu_ref[:, D_FF:] = (da_ * silu).astype(dgu_ref.dtype)

    return pl.pallas_call(
        body, name=name, grid=(S // bs,),
        in_specs=[_row_spec(bs, D_FF), _row_spec(bs, 2 * D_FF)],
        out_specs=_row_spec(bs, 2 * D_FF), out_shape=jax.ShapeDtypeStruct((S, 2 * D_FF), MXU_DTYPE),
        compiler_params=_cparams("parallel"),
    )(da, gu)


def _loss_head(y, target, *, bs, name):
    S = y.shape[0]

    def body(y_ref, t_ref, dy_ref, loss_ref):
        @pl.when(pl.program_id(0) == 0)
        def _():
            loss_ref[...] = jnp.zeros_like(loss_ref)

        e = y_ref[...] - t_ref[...]
        dy_ref[...] = e * (1.0 / D_MODEL)
        per_tok = jnp.mean(e * e, axis=-1, keepdims=True)
        loss_ref[...] += 0.5 * jnp.sum(per_tok, axis=0, keepdims=True)

    return pl.pallas_call(
        body, name=name, grid=(S // bs,),
        in_specs=[_row_spec(bs, D_MODEL), _row_spec(bs, D_MODEL)],
        out_specs=[_row_spec(bs, D_MODEL), pl.BlockSpec((1, 1), lambda i: (0, 0))],
        out_shape=[jax.ShapeDtypeStruct((S, D_MODEL), F32), jax.ShapeDtypeStruct((1, 1), F32)],
        compiler_params=_cparams("arbitrary"),
    )(y, target)


def _blk(n, target):
    if n <= target:
        return n
    best = None
    for b in range(128, target + 1, 128):
        if n % b == 0:
            best = b
    assert best is not None, n
    return best


def _rope_tables(S):
    pos = jnp.arange(S, dtype=F32)
    inv = ROPE_THETA ** (-jnp.arange(0, MLA_ROPE, 2, dtype=F32) / MLA_ROPE)
    ang = pos[:, None] * inv[None, :]
    cos, sin = jnp.cos(ang), jnp.sin(ang)
    one, zero, pad = jnp.ones((S, HEAD_DIM), F32), jnp.zeros((S, HEAD_DIM), F32), jnp.zeros((S, MLA_PAD - MLA_QK), F32)
    return jnp.concatenate([one, cos, cos, pad], axis=1), jnp.concatenate([zero, sin, sin, pad], axis=1)


def _prep_weights_a(w_in, w_uq, w_ukv):
    z = lambda n: jnp.zeros((D_MODEL, n), w_in.dtype)
    win_a = jnp.concatenate([w_in[:, 0:768], w_in[:, 1188:2468]], axis=1)
    win_l = jnp.concatenate([w_in[:, 772:1156], z(64), w_in[:, 1156:1188], z(32), w_in[:, 768:772], z(124)], axis=1)
    kv = w_ukv.reshape(MLA_KV_RANK, 4, 2 * HEAD_DIM)
    return dict(
        win_a=win_a, win_l=win_l, win_p=jnp.concatenate([win_a, win_l], axis=1),
        wuq=jnp.pad(w_uq.reshape(MLA_Q_RANK, 4, MLA_QK), ((0, 0), (0, 0), (0, MLA_PAD - MLA_QK))).reshape(MLA_Q_RANK, 512),
        wuk=jnp.pad(kv[:, :, :HEAD_DIM], ((0, 0), (0, 0), (0, HEAD_DIM))).reshape(MLA_KV_RANK, 512),
        wuv=kv[:, :, HEAD_DIM:].reshape(MLA_KV_RANK, 256))


def _unprep_grads(dwin_p, dwuq, dwuk, dwuv, dwo, dwgu, dwd):
    dw_in = jnp.concatenate([dwin_p[:, 0:768], dwin_p[:, 2560:2564], dwin_p[:, 2048:2432], dwin_p[:, 2496:2528],
                             dwin_p[:, 768:2048]], axis=1)
    dw_uq = dwuq.reshape(MLA_Q_RANK, 4, MLA_PAD)[:, :, :MLA_QK].reshape(MLA_Q_RANK, 4 * MLA_QK)
    dw_ukv = jnp.concatenate([dwuk.reshape(MLA_KV_RANK, 4, MLA_PAD)[:, :, :HEAD_DIM],
                              dwuv.reshape(MLA_KV_RANK, 4, HEAD_DIM)], axis=2).reshape(MLA_KV_RANK, 512)
    return dict(w_in=dw_in, mla_w_uq=dw_uq, mla_w_ukv=dw_ukv, w_o=dwo, w_gate=dwgu[:, :D_FF], w_up=dwgu[:, D_FF:],
                w_down=dwd)


def _layer_fwd(l, x, xb, xbT, W, P, tabs, blk, late_weights=None):
    S = x.shape[0]
    nb = S // blk
    n = lambda s: f"l{l}_{s}"
    bs = min(512, S)
    h_att = _mm(xb, W["win_a"], name=n("in_att"), out_dtype=MXU_DTYPE, bm=1024, bn=1024, bk=1024, colscale=Q_COLSCALE)
    lat = _mm(xb, W["win_l"], name=n("in_lat"), bm=2048, bn=LAT_W, bk=1024)
    fg = lat[:, 512:516].T.reshape(4, S // 128, 128)
    cpos, cneg = _fox_gate_fwd(fg, P["fox_b_f"], name=n("fox_gate"))
    one3 = jnp.ones((S, 4, 3), MXU_DTYPE)
    zpad = jnp.zeros((S, 4, MLA_PAD - HEAD_DIM - 6), MXU_DTYPE)
    per_tok = lambda parts: parts.reshape(4, 3, S).transpose(2, 0, 1)
    q_f = jnp.concatenate([h_att[:, COL_FQ:COL_FQ + 256].reshape(S, 4, HEAD_DIM), per_tok(cpos), one3, zpad],
                          axis=2).reshape(S, 4 * MLA_PAD)
    k_f = jnp.concatenate([h_att[:, COL_FK:COL_FK + 256].reshape(S, 4, HEAD_DIM), one3, per_tok(cneg), zpad],
                          axis=2).reshape(S, 4 * MLA_PAD)
    v_f = h_att[:, COL_FV:COL_FV + 256]
    oT_a, lse_a = _smax_fwd_t(q_f, k_f, v_f, dk=MLA_PAD, blk=blk, name=n("fox_fwd"))
    q_m, k_m, v_m = _mla_prep_fwd(lat, P["mla_g_q"], P["mla_g_kv"], W["wuq"], W["wuk"], W["wuv"], *tabs,
                                  bs=bs, name=n("mla_prep"))
    oT_b, lse_b = _smax_fwd_t(q_m, k_m, v_m, dk=MLA_PAD, blk=blk, name=n("mla_fwd"))
    bsb = min(BLK_STICK, S)
    oT_c, lt_c = _sb_fwd_t(h_att, blk=bsb, name=n("sb_fwd"))
    out_d, lse_d = _swa_fwd(h_att, P["swa_sinks"], name=n("swa_fwd"))
    outs = (oT_a, oT_b, oT_c, out_d)
    gn, gnT = _gnorm_fwd(outs, P["mix_g"], bs=bs, name=n("gnorm"))
    if late_weights is not None:
        W = dict(W, **late_weights(gn))
    u1 = _mm(gn, W["w_o"], name=n("out_proj"), bm=1024, bn=1024, bk=1024, resid=x, alpha=ALPHA)
    x1, x1b, x1bT, xh1, rs1 = _ln_fwd(u1, P["ln1_g"], P["ln1_b"], bs=bs, name=n("ln1"))
    gu = _mm(x1b, W["wgu"], name=n("gate_up"), bm=2048, bn=512, bk=1024)
    a, aT = _swiglu_fwd(gu, bs=min(256, S), name=n("swiglu"))
    u2 = _mm(a, W["w_down"], name=n("down"), bm=1024, bn=1024, bk=_blk(D_FF, 1408), resid=x1, alpha=ALPHA)
    x2, x2b, x2bT, xh2, rs2 = _ln_fwd(u2, P["ln2_g"], P["ln2_b"], bs=bs, name=n("ln2"))
    saved = dict(xbT=xbT, gnT=gnT, x1bT=x1bT, h_att=h_att, lat=lat, fg=fg, outs=outs, oT_a=oT_a, oT_b=oT_b, q_f=q_f, k_f=k_f, v_f=v_f,
                 lse_a=lse_a, lse_b=lse_b, lse_d=lse_d, lt_c=lt_c, q_m=q_m, k_m=k_m, v_m=v_m,
                 xh1=xh1, rs1=rs1, gu=gu, aT=aT, xh2=xh2, rs2=rs2)
    return x2, x2b, x2bT, saved, W


def _layer_bwd(l, dx2, sv, W, P, tabs, blk, send_early=None):
    S = dx2.shape[0]
    n = lambda s: f"l{l}_{s}"
    bs = min(512, S)
    h_att = sv["h_att"]
    du2, du2b, dg2, db2 = _ln_bwd(dx2, sv["xh2"], sv["rs2"], P["ln2_g"], bs=bs, name=n("ln2_bwd"))
    da = _mm(du2b, W["w_down"], name=n("down_dx"), tb=True, bm=1024, bn=_blk(D_FF, 1408), bk=1024)
    dwd = _mm(sv["aT"], du2b, name=n("down_dw"), bm=_blk(D_FF, 1408), bn=1024, bk=1024)
    dgu = _swiglu_bwd(da, sv["gu"], bs=min(256, S), name=n("swiglu_bwd"))
    dx1 = _mm(dgu, W["wgu"], name=n("gate_up_dx"), tb=True, bm=1024, bn=1024, bk=_blk(2 * D_FF, 1408), resid=du2,
              alpha=ALPHA)
    dwgu = _mm(sv["x1bT"], dgu, name=n("gate_up_dw"), bm=1024, bn=_blk(2 * D_FF, 1408), bk=1024)
    du1, du1b, dg1, db1 = _ln_bwd(dx1, sv["xh1"], sv["rs1"], P["ln1_g"], bs=bs, name=n("ln1_bwd"))
    dgn = _mm(du1b, W["w_o"], name=n("out_proj_dx"), tb=True, bm=1024, bn=1024, bk=1024)
    dwo = _mm(sv["gnT"], du1b, name=n("out_proj_dw"), bm=1024, bn=1024, bk=1024)
    mix_g = P["mix_g"]
    if send_early is not None:
        mix_g = mix_g + send_early(dict(w_o=dwo, w_gate=dwgu[:, :D_FF], w_up=dwgu[:, D_FF:], w_down=dwd))[0, 0]
    dmix, dmixT, dmixg = _gnorm_bwd(dgn, sv["outs"], mix_g, bs=bs, name=n("gnorm_bwd"))
    q_f, k_f = sv["q_f"], sv["k_f"]
    dqT_a, dk_a, dva = _smax_bwd_t(q_f, k_f, sv["v_f"], dmix, dmixT, sv["oT_a"], sv["lse_a"], dk=MLA_PAD, dcb=0,
                                   qscale=HEAD_DIM ** -0.5, blk=blk, name=n("fox_bwd"))
    dq_a, dk_a = dqT_a.T.reshape(S, 4, MLA_PAD), dk_a.reshape(S, 4, MLA_PAD)
    dqa, dka = dq_a[:, :, :HEAD_DIM].reshape(S, 256), dk_a[:, :, :HEAD_DIM].reshape(S, 256)
    dcq = dq_a[:, :, HEAD_DIM].T.reshape(4, S // 128, 128)
    dck = dk_a[:, :, HEAD_DIM + 3].T.reshape(4, S // 128, 128)
    q_m, k_m = sv["q_m"], sv["k_m"]
    dqT_b, dkb, dvb = _smax_bwd_t(q_m, k_m, sv["v_m"], dmix, dmixT, sv["oT_b"], sv["lse_b"], dk=MLA_PAD, dcb=2,
                                  qscale=MLA_QK ** -0.5, blk=blk, name=n("mla_bwd"))
    dqT_c, dkc, dvc = _sb_bwd_t(h_att, dmix, dmixT, sv["lt_c"], dcb=4, qscale=HEAD_DIM ** -0.5,
                                blk=min(BLK_STICK, S), name=n("sb_bwd"))
    dqc = dqT_c.T
    dqd, dkd, dvd, dsink = _swa_bwd(h_att, P["swa_sinks"], dmix, sv["outs"][3], sv["lse_d"], dcb=3, name=n("swa_bwd"))
    dlat, dwuq, dwuk, dwuv, dgq, dgkv = _mla_prep_bwd(
        sv["lat"], P["mla_g_q"], P["mla_g_kv"], W["wuq"], W["wuk"], W["wuv"], *tabs, dqT_b, dkb, dvb,
        bs=bs, name=n("mla_prep_bwd"))
    dfg, dbf = _fox_gate_bwd(sv["fg"], P["fox_b_f"], dck, dcq, q_unscale=HEAD_DIM ** 0.5, name=n("fox_gate_bwd"))
    dfg_blk = jnp.pad(dfg.reshape(4, S).T, ((0, 0), (0, 124)))
    dh = jnp.concatenate([t.astype(MXU_DTYPE) for t in (dqa, dka, dva, dqc, dkc, dvc, dqd, dkd, dvd, dlat, dfg_blk)], axis=1)
    dx = _mm(dh, W["win_p"], name=n("in_dx"), tb=True, bm=1024, bn=1024, bk=_blk(PERM_W, 1024), resid=du1, alpha=ALPHA)
    dwin_p = _mm(sv["xbT"], dh, name=n("in_dw"), bm=1024, bn=_blk(PERM_W, 1024), bk=1024)
    grads = _unprep_grads(dwin_p, dwuq, dwuk, dwuv, dwo, dwgu, dwd)
    grads.update(fox_b_f=dbf[:, 0], mla_g_q=dgq[0], mla_g_kv=dgkv[0], swa_sinks=dsink[:, 0], mix_g=dmixg[0],
                 ln1_g=dg1[0], ln1_b=db1[0], ln2_g=dg2[0], ln2_b=db2[0])
    return dx, grads


BIG = ("w_in", "mla_w_uq", "mla_w_ukv", "w_o", "w_gate", "w_up", "w_down")
SMALL = ("fox_b_f", "mla_g_q", "mla_g_kv", "swa_sinks", "mix_g", "ln1_g", "ln1_b", "ln2_g", "ln2_b")
SHARD_AXIS = dict(w_in=2, mla_w_uq=2, mla_w_ukv=2, w_o=1, w_gate=2, w_up=2, w_down=1)
N_CHIPS = 4
ANY = pl.BlockSpec(memory_space=pl.ANY)


HBM = pl.BlockSpec(memory_space=pltpu.HBM)
SEM = pl.BlockSpec(memory_space=pltpu.SEMAPHORE)
N_PEER_CHIPS = N_CHIPS - 1


def _peer_copies(src_ref, land_ref, sems, scatter):
    x, y, c = lax.axis_index("x"), lax.axis_index("y"), lax.axis_index("c")
    me = 2 * x + y
    out = []
    for r, (px, py) in enumerate([(1 - x, y), (x, 1 - y), (1 - x, 1 - y)]):
        theirs = 2 * px + py
        send = pltpu.make_async_remote_copy(
            src_ref=src_ref.at[theirs] if scatter else src_ref, dst_ref=land_ref.at[me],
            send_sem=sems[2 * r], recv_sem=sems[2 * r + 1], device_id=(px, py, c), device_id_type=MESH)
        arrive = pltpu.make_async_remote_copy(
            src_ref=src_ref.at[me] if scatter else src_ref, dst_ref=land_ref.at[theirs],
            send_sem=sems[2 * r], recv_sem=sems[2 * r + 1], device_id=(px, py, c), device_id_type=MESH)
        out.append((send, arrive))
    return out


def _exchange_start(srcs, *, scatter, name):
    nt = len(srcs)
    ns = 2 * N_PEER_CHIPS * nt
    land_shapes = [s.shape if scatter else (N_CHIPS,) + s.shape for s in srcs]

    def body(*refs):
        src_refs, land_refs, outs = refs[:nt], refs[nt:2 * nt], refs[2 * nt:]
        for t in range(nt):
            for send, _ in _peer_copies(src_refs[t], land_refs[t], outs[6 * t:6 * t + 6], scatter):
                send.start()
        outs[-1][...] = jnp.zeros_like(outs[-1])

    res = pl.pallas_call(
        body, name=name,
        out_shape=(*[pltpu.SemaphoreType.DMA(())] * ns, *[pltpu.HBM(s.shape, s.dtype) for s in srcs],
                   *[pltpu.HBM(ls, s.dtype) for ls, s in zip(land_shapes, srcs)], jax.ShapeDtypeStruct((8, 128), F32)),
        in_specs=(HBM,) * (2 * nt), out_specs=(*[SEM] * ns, *[HBM] * (2 * nt), pl.BlockSpec(memory_space=pltpu.VMEM)),
        input_output_aliases={i: ns + i for i in range(2 * nt)},
        compiler_params=pltpu.CompilerParams(has_side_effects=pltpu.SideEffectType.DATAFLOW_SIDE_EFFECTING),
    )(*[pltpu.with_memory_space_constraint(s, pltpu.HBM) for s in srcs],
      *[pltpu.with_memory_space_constraint(lax.empty(ls, s.dtype), pltpu.HBM) for ls, s in zip(land_shapes, srcs)])
    return dict(sems=res[:ns], srcs=res[ns:ns + nt], lands=res[ns + nt:ns + 2 * nt], token=res[-1])


def _exchange_wait(started, after, *, scatter, name):
    nt = len(started["srcs"])
    ns = 2 * N_PEER_CHIPS * nt

    def body(*refs):
        src_refs, land_refs, sems = refs[:nt], refs[nt:2 * nt], refs[2 * nt:2 * nt + ns]
        for t in range(nt):
            for send, arrive in _peer_copies(src_refs[t], land_refs[t], sems[6 * t:6 * t + 6], scatter):
                send.wait_send()
                arrive.wait_recv()

    both = list(started["srcs"]) + list(started["lands"])
    res = pl.pallas_call(
        body, name=name, out_shape=tuple(pltpu.HBM(a.shape, a.dtype) for a in both),
        in_specs=(*[HBM] * (2 * nt), *[SEM] * ns, ANY), out_specs=(HBM,) * (2 * nt),
        input_output_aliases={i: i for i in range(2 * nt)},
        compiler_params=pltpu.CompilerParams(has_side_effects=pltpu.SideEffectType.DATAFLOW_SIDE_EFFECTING),
    )(*both, *started["sems"], after)
    return res[:nt], res[nt:]


def _core_exchange(tensors, *, name):
    nt = len(tensors)

    def body(*refs):
        ins, outs = refs[:nt], refs[nt:2 * nt]
        send_sems, recv_sems = refs[2 * nt:]
        sibling = (lax.axis_index("x"), lax.axis_index("y"), 1 - lax.axis_index("c"))
        copies = [pltpu.make_async_remote_copy(src_ref=ins[t], dst_ref=outs[t], send_sem=send_sems.at[t],
                                               recv_sem=recv_sems.at[t], device_id=sibling, device_id_type=MESH)
                  for t in range(nt)]
        for cp in copies:
            cp.start()
        for cp in copies:
            cp.wait_recv()
        for cp in copies:
            cp.wait_send()

    return pl.pallas_call(
        body, name=name, in_specs=[ANY] * nt, out_specs=[ANY] * nt,
        out_shape=[jax.ShapeDtypeStruct(t.shape, t.dtype) for t in tensors],
        scratch_shapes=[pltpu.SemaphoreType.DMA((nt,)), pltpu.SemaphoreType.DMA((nt,))],
        compiler_params=pltpu.CompilerParams(has_side_effects=True),
    )(*tensors)


def _all_sum_small(block, *, name):
    R = block.shape[0]
    n_dev = 8

    def body(x_ref, o_ref, slots, send_sems, recv_sems):
        x, y, c = lax.axis_index("x"), lax.axis_index("y"), lax.axis_index("c")
        me = 4 * x + 2 * y + c
        slots[me] = x_ref[...]
        sends, recvs = [], []
        for d in range(1, n_dev):
            px, py, pc = x ^ (d >> 2), y ^ ((d >> 1) & 1), c ^ (d & 1)
            theirs = 4 * px + 2 * py + pc
            sends.append(pltpu.make_async_remote_copy(
                src_ref=x_ref, dst_ref=slots.at[me], send_sem=send_sems.at[d - 1], recv_sem=recv_sems.at[d - 1],
                device_id=(px, py, pc), device_id_type=MESH))
            recvs.append(pltpu.make_async_remote_copy(
                src_ref=x_ref, dst_ref=slots.at[theirs], send_sem=send_sems.at[d - 1], recv_sem=recv_sems.at[d - 1],
                device_id=(px, py, pc), device_id_type=MESH))
        for cp in sends:
            cp.start()
        for cp in recvs:
            cp.wait_recv()
        for cp in sends:
            cp.wait_send()
        total = slots[0]
        for k in range(1, n_dev):
            total = total + slots[k]
        o_ref[...] = total

    return pl.pallas_call(
        body, name=name, in_specs=[pl.BlockSpec(memory_space=pltpu.VMEM)],
        out_specs=pl.BlockSpec(memory_space=pltpu.VMEM), out_shape=jax.ShapeDtypeStruct((R, 128), F32),
        scratch_shapes=[pltpu.VMEM((n_dev, R, 128), F32), pltpu.SemaphoreType.DMA((n_dev - 1,)),
                        pltpu.SemaphoreType.DMA((n_dev - 1,))],
        compiler_params=pltpu.CompilerParams(has_side_effects=True),
    )(block)


def _sum_chips_into(acc, land, own, me, layer, *, br, name):
    _, R, C = land.shape

    def body(me_ref, land_ref, own_ref, acc_ref, o_ref):
        mine = me_ref[0]
        total = None
        for k in range(N_CHIPS):
            part = jnp.where(mine == k, own_ref[...], land_ref[k]).astype(F32)
            total = part if total is None else total + part
        o_ref[0] = total

    return pl.pallas_call(
        body, name=name, grid=(R // br,),
        in_specs=[pl.BlockSpec(memory_space=pltpu.SMEM), pl.BlockSpec((N_CHIPS, br, C), lambda i: (0, i, 0)),
                  pl.BlockSpec((br, C), lambda i: (i, 0)), ANY],
        out_specs=pl.BlockSpec((1, br, C), lambda i: (layer, i, 0)),
        out_shape=jax.ShapeDtypeStruct(acc.shape, F32), input_output_aliases={3: 0},
        compiler_params=_cparams("parallel"),
    )(me, land, own, acc)


def _adamw_math(w, g, m, v):
    m = ADAM_B1 * m + (1.0 - ADAM_B1) * g
    v = ADAM_B2 * v + (1.0 - ADAM_B2) * (g * g)
    m_hat = m / (1.0 - ADAM_B1 ** ADAM_STEP)
    v_hat = v / (1.0 - ADAM_B2 ** ADAM_STEP)
    return -ADAM_LR * (m_hat / (jnp.sqrt(v_hat) + ADAM_EPS) + ADAM_WD * w), m, v


def _adamw(w, m, v, g_a, g_b, *, br, name):
    R, C = w.shape
    two = g_b is not None

    def body(*refs):
        if two:
            w_ref, m_ref, v_ref, ga_ref, gb_ref, g_ref, d_ref, nm_ref, nv_ref = refs
            g = ga_ref[...] + gb_ref[...]
        else:
            w_ref, m_ref, v_ref, ga_ref, g_ref, d_ref, nm_ref, nv_ref = refs
            g = ga_ref[...]
        g_ref[...] = g
        d_ref[...], nm_ref[...], nv_ref[...] = _adamw_math(w_ref[...], g, m_ref[...], v_ref[...])

    spec = pl.BlockSpec((br, C), lambda i: (i, 0))
    args = [w, m, v, g_a] + ([g_b] if two else [])
    return pl.pallas_call(
        body, name=name, grid=(R // br,), in_specs=[spec] * len(args), out_specs=[spec] * 4,
        out_shape=[jax.ShapeDtypeStruct((R, C), F32)] * 4,
        compiler_params=_cparams("parallel"),
    )(*args)


SMALL_ROWS = dict(fox_b_f=1, mla_g_q=2, mla_g_kv=1, swa_sinks=1, mix_g=8, ln1_g=8, ln1_b=8, ln2_g=8, ln2_b=8)
SMALL_ROWS_PER_LAYER = sum(SMALL_ROWS.values())


def _pack_small(vals, extra_rows):
    L = vals[SMALL[0]].shape[0]
    per_layer = []
    for name in SMALL:
        a = vals[name].astype(F32)
        a = jnp.pad(a, ((0, 0), (0, SMALL_ROWS[name] * 128 - a.shape[1])))
        per_layer.append(a.reshape(L, SMALL_ROWS[name], 128))
    out = jnp.concatenate(per_layer, axis=1).reshape(L * SMALL_ROWS_PER_LAYER, 128)
    return jnp.pad(out, ((0, extra_rows), (0, 0)))


def _unpack_small(block, shapes):
    L = shapes[SMALL[0]][0]
    body = block[:L * SMALL_ROWS_PER_LAYER].reshape(L, SMALL_ROWS_PER_LAYER, 128)
    out, r = {}, 0
    for name in SMALL:
        n = shapes[name][1]
        out[name] = body[:, r:r + SMALL_ROWS[name]].reshape(L, SMALL_ROWS[name] * 128)[:, :n]
        r += SMALL_ROWS[name]
    return out


def _to_chips(g, axis):
    L, a, b = g.shape
    if axis == 2:
        return g.reshape(L, a, N_CHIPS, b // N_CHIPS).transpose(2, 0, 1, 3)
    return g.reshape(L, N_CHIPS, a // N_CHIPS, b).transpose(1, 0, 2, 3)


def kernel(x, w_in, fox_b_f, mla_g_q, mla_g_kv, mla_w_uq, mla_w_ukv, swa_sinks, mix_g, w_o, ln1_g, ln1_b, w_gate, w_up, w_down, ln2_g, ln2_b, loss_target, m_w_in, m_fox_b_f, m_mla_g_q, m_mla_g_kv, m_mla_w_uq, m_mla_w_ukv, m_swa_sinks, m_mix_g, m_w_o, m_ln1_g, m_ln1_b, m_w_gate, m_w_up, m_w_down, m_ln2_g, m_ln2_b, v_w_in, v_fox_b_f, v_mla_g_q, v_mla_g_kv, v_mla_w_uq, v_mla_w_ukv, v_swa_sinks, v_mix_g, v_w_o, v_ln1_g, v_ln1_b, v_w_gate, v_w_up, v_w_down, v_ln2_g, v_ln2_b):
    w = dict(w_in=w_in, fox_b_f=fox_b_f, mla_g_q=mla_g_q, mla_g_kv=mla_g_kv, mla_w_uq=mla_w_uq, mla_w_ukv=mla_w_ukv,
             swa_sinks=swa_sinks, mix_g=mix_g, w_o=w_o, ln1_g=ln1_g, ln1_b=ln1_b, w_gate=w_gate, w_up=w_up,
             w_down=w_down, ln2_g=ln2_g, ln2_b=ln2_b)
    m = dict(w_in=m_w_in, fox_b_f=m_fox_b_f, mla_g_q=m_mla_g_q, mla_g_kv=m_mla_g_kv, mla_w_uq=m_mla_w_uq,
             mla_w_ukv=m_mla_w_ukv, swa_sinks=m_swa_sinks, mix_g=m_mix_g, w_o=m_w_o, ln1_g=m_ln1_g, ln1_b=m_ln1_b,
             w_gate=m_w_gate, w_up=m_w_up, w_down=m_w_down, ln2_g=m_ln2_g, ln2_b=m_ln2_b)
    v = dict(w_in=v_w_in, fox_b_f=v_fox_b_f, mla_g_q=v_mla_g_q, mla_g_kv=v_mla_g_kv, mla_w_uq=v_mla_w_uq,
             mla_w_ukv=v_mla_w_ukv, swa_sinks=v_swa_sinks, mix_g=v_mix_g, w_o=v_w_o, ln1_g=v_ln1_g, ln1_b=v_ln1_b,
             w_gate=v_w_gate, w_up=v_w_up, w_down=v_w_down, ln2_g=v_ln2_g, ln2_b=v_ln2_b)
    names = tuple(w)
    L = w_in.shape[0]
    S = x.shape[1]
    blk = min(BLK_SOFTMAX, S)
    bs = min(512, S)

    me = 2 * lax.axis_index("x") + lax.axis_index("y")
    axis_of = {k: SHARD_AXIS[k] - 1 for k in BIG}
    groups = (("w_in", "mla_w_uq", "mla_w_ukv"), ("w_o", "w_gate", "w_up", "w_down"))

    started, last = [], None
    for l in range(L):
        per_group = []
        for g, group in enumerate(groups):
            srcs = [w[k][l].astype(MXU_DTYPE) for k in group]
            if last is not None:
                t = min(range(len(srcs)), key=lambda i: srcs[i].size)
                srcs[t] = srcs[t] + last["token"][0, 0].astype(MXU_DTYPE)
            last = _exchange_start(srcs, scatter=False, name=f"gather_start{l}_{g}")
            per_group.append(last)
        started.append(per_group)
    all_started = sum(st["token"] for per_group in started for st in per_group)

    def gathered(l, g, after):
        mine, lands = _exchange_wait(started[l][g], after, scatter=False, name=f"gather_wait{l}_{g}")
        shard = lambda t, k: jnp.where(me == k, mine[t], lands[t][k])
        whole = lambda t, axis: jnp.concatenate([shard(t, k) for k in range(N_CHIPS)], axis=axis)
        if g == 0:
            return _prep_weights_a(*[whole(t, axis_of[name]) for t, name in enumerate(groups[0])])
        gate_up = jnp.concatenate([shard(t, k) for t in (1, 2) for k in range(N_CHIPS)], axis=1)
        return dict(w_o=whole(0, 0), wgu=gate_up, w_down=whole(3, 0))

    def scatter(l, g, grads):
        to_owner = [_to_chips(grads[k].astype(MXU_DTYPE)[None], axis_of[k] + 1)[:, 0] for k in groups[g]]
        return _exchange_start(to_owner, scatter=True, name=f"scatter_start{l}_{g}")

    tabs = _rope_tables(S)
    Ps = []
    for l in range(L):
        P = dict(fox_b_f=fox_b_f[l], swa_sinks=swa_sinks[l])
        for k in ("mla_g_q", "mla_g_kv", "mix_g", "ln1_g", "ln1_b", "ln2_g", "ln2_b"):
            P[k] = w[k][l][None, :]
        Ps.append(P)

    xa = x[0]
    xb = xa.astype(MXU_DTYPE)
    xbT = xb.T
    saved, Ws = [], []
    for l in range(L):
        W = gathered(l, 0, all_started if l == 0 else xa)
        late = lambda after, l=l: gathered(l, 1, after)
        xa, xb, xbT, sv, W = _layer_fwd(l, xa, xb, xbT, W, Ps[l], tabs, blk, late_weights=late)
        saved.append(sv)
        Ws.append(W)
    dx, loss_part = _loss_head(xa, loss_target[0], bs=bs, name="loss_head")

    layer_grads = [None] * L
    sent = [[None, None] for _ in range(L)]
    pin = None
    for l in reversed(range(L)):
        P = Ps[l] if pin is None else dict(Ps[l], ln2_g=Ps[l]["ln2_g"] + pin[0, 0])

        def send_early(grads, l=l):
            sent[l][1] = scatter(l, 1, grads)
            return sent[l][1]["token"]

        dx, layer_grads[l] = _layer_bwd(l, dx, saved[l], Ws[l], P, tabs, blk, send_early=send_early)
        sent[l][0] = scatter(l, 0, layer_grads[l])
        pin = sent[l][0]["token"]
    grad_x = dx[None]

    me_arr = me.astype(jnp.int32)[None]
    partial = {k: jnp.zeros(w[k].shape, F32) for k in BIG}
    after = dx
    for l in reversed(range(L)):
        for g in (1, 0):
            mine, lands = _exchange_wait(sent[l][g], after, scatter=True, name=f"scatter_wait{l}_{g}")
            for t, k in enumerate(groups[g]):
                own = lax.dynamic_index_in_dim(mine[t], me, 0, keepdims=False)
                partial[k] = _sum_chips_into(partial[k], lands[t], own, me_arr, l, br=_rows(own.shape[0]),
                                             name=f"sum_{k}_l{l}")
            after = partial[groups[g][-1]]
    partial = [partial[k] for k in BIG]
    sibling = _core_exchange(partial, name="swap_partials")
    local = {k: jnp.stack([layer_grads[l][k] for l in range(L)]) for k in SMALL}
    out = {}
    for k, mine, theirs in zip(BIG, partial, sibling):
        shp = w[k].shape
        two_d = lambda a: a.reshape(shp[0] * shp[1], shp[2])
        res = _adamw(two_d(w[k]), two_d(m[k]), two_d(v[k]), two_d(mine), two_d(theirs), br=_rows(shp[0] * shp[1]),
                     name=f"adamw_{k}")
        out[k] = [a.reshape(shp) for a in res]

    shapes = {k: w[k].shape for k in SMALL}
    extra = 8 + (-L * SMALL_ROWS_PER_LAYER) % 8
    block = _pack_small({k: local[k] for k in SMALL}, extra)
    block = block.at[L * SMALL_ROWS_PER_LAYER, 0].set(loss_part[0, 0])
    total = _all_sum_small(block, name="sum_small")
    loss = total[L * SMALL_ROWS_PER_LAYER, 0]
    res = _adamw(_pack_small({k: w[k] for k in SMALL}, extra), _pack_small({k: m[k] for k in SMALL}, extra),
                 _pack_small({k: v[k] for k in SMALL}, extra), total, None, br=total.shape[0], name="adamw_small")
    res = [_unpack_small(t, shapes) for t in res]
    for k in SMALL:
        out[k] = [r[k] for r in res]

    return (loss, grad_x, *[out[k][0] for k in names], *[out[k][1] for k in names],
            *[out[k][2] for k in names], *[out[k][3] for k in names])


def _rows(n):
    for b in (256, 128, 64, 32, 16, 8):
        if n % b == 0:
            return b
    return n
```

```python
import functools

import numpy as np
import jax
import jax.numpy as jnp
from jax import lax
from jax.experimental import pallas as pl
from jax.experimental.pallas import tpu as pltpu

F32 = jnp.float32
MXU_DTYPE = jnp.bfloat16
NEG_INF = -1e30

D_MODEL = 1024
DEPTH = 4
HEAD_DIM = 64
GROUP_WIDTH = 256
D_FF = 2816
MLA_Q_RANK = 256
MLA_KV_RANK = 128
MLA_ROPE = 32
MLA_QK = 96
MLA_PAD = 128
ROPE_THETA = 10000.0
WINDOW = 128
ALPHA = (2.0 * DEPTH) ** 0.25
SWA_SLOPES = tuple(float(2.0 ** (-8.0 * h / 4)) for h in range(1, 5))
ATT_W = 2048
LAT_W = 640
PERM_W = ATT_W + LAT_W
COL_FQ, COL_FK, COL_FV = 0, 256, 512
COL_SQ, COL_SK, COL_SV = 768, 1024, 1280
COL_WQ, COL_WK, COL_WV = 1536, 1792, 1920
Q_COLSCALE = np.ones((1, ATT_W), np.float32)
Q_COLSCALE[:, COL_FQ:COL_FQ + 256] = HEAD_DIM ** -0.5
Q_COLSCALE[:, COL_SQ:COL_SQ + 256] = HEAD_DIM ** -0.5

ADAM_LR, ADAM_B1, ADAM_B2, ADAM_EPS, ADAM_WD, ADAM_STEP = 0.001, 0.9, 0.999, 1e-08, 0.01, 10

VMEM_LIMIT = 56 * 1024 * 1024
NT = (((1,), (1,)), ((), ()))
TN = (((0,), (0,)), ((), ()))
MESH = pl.DeviceIdType.MESH


def _cparams(*sem):
    return pltpu.CompilerParams(dimension_semantics=sem, vmem_limit_bytes=VMEM_LIMIT)


def _dot01(x, m01, dn=None, parts=2):
    acc = None
    rem = x
    for _ in range(parts):
        part = rem.astype(MXU_DTYPE)
        rem = rem - part.astype(F32)
        if dn is None:
            t = jnp.dot(part, m01, preferred_element_type=F32)
        else:
            t = lax.dot_general(part, m01, dn, preferred_element_type=F32)
        acc = t if acc is None else acc + t
    return acc


def _mm(a, b, *, name, ta=False, tb=False, out_dtype=F32, bm=512, bn=512, bk=512, resid=None, alpha=1.0,
        colscale=None):
    M, K = (a.shape[1], a.shape[0]) if ta else a.shape
    N = b.shape[0] if tb else b.shape[1]
    assert (b.shape[1] if tb else b.shape[0]) == K
    assert resid is None or colscale is None
    bm, bn, bk = min(bm, M), min(bn, N), min(bk, K)
    assert M % bm == 0 and N % bn == 0 and K % bk == 0, (name, M, N, K, bm, bn, bk)
    nk = K // bk
    assert nk == 1 or (out_dtype == F32 and colscale is None), name
    dn = (((0 if ta else 1,), (1 if tb else 0,)), ((), ()))

    extra = resid is not None or colscale is not None

    def body(*refs):
        a_ref, b_ref = refs[:2]
        r_ref = refs[2] if extra else None
        o_ref = refs[3] if extra else refs[2]
        k = pl.program_id(2)

        def first():
            r = lax.dot_general(a_ref[...].astype(MXU_DTYPE), b_ref[...].astype(MXU_DTYPE), dn,
                                preferred_element_type=F32)
            if resid is not None:
                r = r + alpha * r_ref[...]
            if colscale is not None:
                r = r * r_ref[...]
            o_ref[...] = r.astype(o_ref.dtype)

        if nk == 1:
            first()
        else:
            pl.when(k == 0)(first)

            @pl.when(k > 0)
            def _():
                o_ref[...] += lax.dot_general(a_ref[...].astype(MXU_DTYPE), b_ref[...].astype(MXU_DTYPE), dn,
                                              preferred_element_type=F32)

    a_spec = pl.BlockSpec((bk, bm), lambda i, j, k: (k, i)) if ta else pl.BlockSpec((bm, bk), lambda i, j, k: (i, k))
    b_spec = pl.BlockSpec((bn, bk), lambda i, j, k: (j, k)) if tb else pl.BlockSpec((bk, bn), lambda i, j, k: (k, j))
    in_specs = [a_spec, b_spec]
    args = [a, b]
    if resid is not None:
        in_specs.append(pl.BlockSpec((bm, bn), lambda i, j, k: (i, j)))
        args.append(resid)
    if colscale is not None:
        in_specs.append(pl.BlockSpec((1, bn), lambda i, j, k: (0, j)))
        args.append(colscale)
    return pl.pallas_call(
        body, name=name, grid=(M // bm, N // bn, nk), in_specs=in_specs,
        out_specs=pl.BlockSpec((bm, bn), lambda i, j, k: (i, j)),
        out_shape=jax.ShapeDtypeStruct((M, N), out_dtype),
        compiler_params=_cparams("parallel", "parallel", "arbitrary"),
    )(*args)


HP = 4
BLK_SOFTMAX = 512
BLK_STICK = 256


def _t(x):
    return x.astype(F32).T.astype(MXU_DTYPE)


def _fill_transposed(dst_ref, src_ref, nb, blk):
    for j in range(nb):
        dst_ref[j] = _t(src_ref[j * blk:(j + 1) * blk, :])


def _smax_fwd_t(q, k, v, *, dk, blk, name):
    S = k.shape[0]
    nb = S // blk
    H = k.shape[1] // dk

    def body(q_ref, k_ref, v_ref, oT_ref, lse_ref, vT_ref):
        i = pl.program_id(1)

        @pl.when(i == 0)
        def _():
            _fill_transposed(vT_ref, v_ref, nb, blk)

        key = lax.broadcasted_iota(jnp.int32, (blk, blk), 0)
        qry = lax.broadcasted_iota(jnp.int32, (blk, blk), 1)
        qs = [_t(q_ref[:, h * dk:(h + 1) * dk]) for h in range(HP)]

        def tile(j, carry, masked):
            r0 = pl.multiple_of(j * blk, blk)
            ss = [jnp.dot(k_ref[pl.ds(r0, blk), h * dk:(h + 1) * dk], qs[h], preferred_element_type=F32)
                  for h in range(HP)]
            stats, pes = [], []
            for h in range(HP):
                m, l, _ = carry[h]
                s = jnp.where(key <= qry, ss[h], NEG_INF) if masked else ss[h]
                mn = jnp.maximum(m, jnp.max(s, axis=0, keepdims=True))
                a = jnp.exp(m - mn)
                pe = jnp.exp(s - mn)
                stats.append((mn, a * l + jnp.sum(pe, axis=0, keepdims=True), a))
                pes.append(pe.astype(MXU_DTYPE))
            pvs = [jnp.dot(vT_ref[j, h * HEAD_DIM:(h + 1) * HEAD_DIM, :], pes[h], preferred_element_type=F32)
                   for h in range(HP)]
            return tuple((stats[h][0], stats[h][1], stats[h][2] * carry[h][2] + pvs[h]) for h in range(HP))

        init = tuple((jnp.full((1, blk), NEG_INF, F32), jnp.zeros((1, blk), F32), jnp.zeros((HEAD_DIM, blk), F32))
                     for _ in range(HP))
        carry = lax.fori_loop(0, i, functools.partial(tile, masked=False), init)
        carry = tile(i, carry, True)
        for h in range(HP):
            m, l, acc = carry[h]
            oT_ref[h * HEAD_DIM:(h + 1) * HEAD_DIM, :] = acc / l
            lse_ref[h, 0] = m + jnp.log(l)

    return pl.pallas_call(
        body, name=name, grid=(H // HP, nb),
        in_specs=[pl.BlockSpec((blk, HP * dk), lambda p, i: (i, p)),
                  pl.BlockSpec((S, HP * dk), lambda p, i: (0, p)),
                  pl.BlockSpec((S, HP * HEAD_DIM), lambda p, i: (0, p))],
        out_specs=[pl.BlockSpec((HP * HEAD_DIM, blk), lambda p, i: (p, i)),
                   pl.BlockSpec((HP, 1, 1, blk), lambda p, i: (p, i, 0, 0))],
        out_shape=[jax.ShapeDtypeStruct((H * HEAD_DIM, S), F32), jax.ShapeDtypeStruct((H, nb, 1, blk), F32)],
        scratch_shapes=[pltpu.VMEM((nb, HP * HEAD_DIM, blk), MXU_DTYPE)],
        compiler_params=_cparams("arbitrary", "arbitrary"),
    )(q, k, v)


def _smax_bwd_t(q, k, v, dmix, dmixT, oT, lse, *, dk, dcb, qscale, blk, name):
    S = k.shape[0]
    nb = S // blk
    H = k.shape[1] // dk
    hd = HP * HEAD_DIM
    dcr = dcb * 128 // hd

    def body(q_ref, k_ref, v_ref, do_ref, doT_ref, oT_ref, lse_ref, dqT_ref, dk_ref, dv_ref, kT_ref):
        i = pl.program_id(1)

        @pl.when(i == 0)
        def _():
            dk_ref[...] = jnp.zeros_like(dk_ref)
            dv_ref[...] = jnp.zeros_like(dv_ref)
            _fill_transposed(kT_ref, k_ref, nb, blk)

        key = lax.broadcasted_iota(jnp.int32, (blk, blk), 0)
        qry = lax.broadcasted_iota(jnp.int32, (blk, blk), 1)
        per_head = []
        for h in range(HP):
            hs = slice(h * HEAD_DIM, (h + 1) * HEAD_DIM)
            doT = doT_ref[hs, :]
            per_head.append(dict(
                qT=_t(q_ref[:, h * dk:(h + 1) * dk]), q=q_ref[:, h * dk:(h + 1) * dk],
                doT=doT.astype(MXU_DTYPE), do=do_ref[:, hs].astype(MXU_DTYPE),
                delta=jnp.sum(doT * oT_ref[hs, :], axis=0, keepdims=True), lse=lse_ref[h, 0]))

        def tile(j, dqs, masked):
            r0 = pl.multiple_of(j * blk, blk)
            rows = pl.ds(r0, blk)
            ksl = [slice(h * dk, (h + 1) * dk) for h in range(HP)]
            hsl = [slice(h * HEAD_DIM, (h + 1) * HEAD_DIM) for h in range(HP)]
            ss = [jnp.dot(k_ref[rows, ksl[h]], per_head[h]["qT"], preferred_element_type=F32) for h in range(HP)]
            dps = [jnp.dot(v_ref[rows, hsl[h]], per_head[h]["doT"], preferred_element_type=F32) for h in range(HP)]
            prs, dss = [], []
            for h in range(HP):
                c = per_head[h]
                s = jnp.where(key <= qry, ss[h], NEG_INF) if masked else ss[h]
                pr = jnp.exp(s - c["lse"])
                dss.append((pr * (dps[h] - c["delta"])).astype(MXU_DTYPE))
                prs.append(pr.astype(MXU_DTYPE))
            for h in range(HP):
                dv_ref[rows, hsl[h]] += jnp.dot(prs[h], per_head[h]["do"], preferred_element_type=F32)
            for h in range(HP):
                dk_ref[rows, ksl[h]] += jnp.dot(dss[h], per_head[h]["q"], preferred_element_type=F32)
            return tuple(dqs[h] + jnp.dot(kT_ref[j, ksl[h], :], dss[h], preferred_element_type=F32) for h in range(HP))

        dqs = lax.fori_loop(0, i, functools.partial(tile, masked=False),
                            tuple(jnp.zeros((dk, blk), F32) for _ in range(HP)))
        dqs = tile(i, dqs, True)
        for h in range(HP):
            dqT_ref[h * dk:(h + 1) * dk, :] = dqs[h] * qscale

    return pl.pallas_call(
        body, name=name, grid=(H // HP, nb),
        in_specs=[pl.BlockSpec((blk, HP * dk), lambda p, i: (i, p)),
                  pl.BlockSpec((S, HP * dk), lambda p, i: (0, p)),
                  pl.BlockSpec((S, hd), lambda p, i: (0, p)),
                  pl.BlockSpec((blk, hd), lambda p, i: (i, dcr + p)),
                  pl.BlockSpec((hd, blk), lambda p, i: (dcr + p, i)),
                  pl.BlockSpec((hd, blk), lambda p, i: (p, i)),
                  pl.BlockSpec((HP, 1, 1, blk), lambda p, i: (p, i, 0, 0))],
        out_specs=[pl.BlockSpec((HP * dk, blk), lambda p, i: (p, i)),
                   pl.BlockSpec((S, HP * dk), lambda p, i: (0, p)),
                   pl.BlockSpec((S, hd), lambda p, i: (0, p))],
        out_shape=[jax.ShapeDtypeStruct((H * dk, S), F32), jax.ShapeDtypeStruct((S, H * dk), F32),
                   jax.ShapeDtypeStruct((S, H * HEAD_DIM), F32)],
        scratch_shapes=[pltpu.VMEM((nb, HP * dk, blk), MXU_DTYPE)],
        compiler_params=_cparams("arbitrary", "arbitrary"),
    )(q, k, v, dmix, dmixT, oT, lse)


def _log1m_beta(z):
    return -(jnp.maximum(z, 0.0) + jnp.log(1.0 + jnp.exp(-jnp.abs(z))))


def _dot01_left(m01, x, parts=2):
    acc = None
    rem = x
    for _ in range(parts):
        part = rem.astype(MXU_DTYPE)
        rem = rem - part.astype(F32)
        t = jnp.dot(m01, part, preferred_element_type=F32)
        acc = t if acc is None else acc + t
    return acc


def _sb_fwd_t(h_att, *, blk, name):
    S = h_att.shape[0]
    nb = S // blk
    hd = HP * HEAD_DIM
    qcb, kcb, vcb = COL_SQ // hd, COL_SK // hd, COL_SV // hd

    def body(q_ref, k_ref, v_ref, oT_ref, lt_ref, vT_ref):
        i = pl.program_id(1)

        @pl.when(i == 0)
        def _():
            _fill_transposed(vT_ref, v_ref, nb, blk)

        key = lax.broadcasted_iota(jnp.int32, (blk, blk), 0)
        qry = lax.broadcasted_iota(jnp.int32, (blk, blk), 1)
        strict = key < qry
        later = (qry > key).astype(MXU_DTYPE)
        qs = [_t(q_ref[:, h * HEAD_DIM:(h + 1) * HEAD_DIM]) for h in range(HP)]

        def tile(j, carry, mask):
            r0 = pl.multiple_of(j * blk, blk)
            hsl = [slice(h * HEAD_DIM, (h + 1) * HEAD_DIM) for h in range(HP)]
            zs = [jnp.dot(k_ref[pl.ds(r0, blk), hsl[h]], qs[h], preferred_element_type=F32) for h in range(HP)]
            lbs = []
            for h in range(HP):
                lb = _log1m_beta(zs[h])
                lbs.append(lb if mask is None else jnp.where(mask, lb, 0.0))
            sums = [_dot01_left(later, lbs[h]) for h in range(HP)]
            probs = []
            for h in range(HP):
                lt_ref[h, 0, j] = carry[h][0]
                a = jnp.exp(zs[h] + lbs[h] + sums[h] + carry[h][0])
                probs.append((a if mask is None else jnp.where(mask, a, 0.0)).astype(MXU_DTYPE))
            pvs = [jnp.dot(vT_ref[j, hsl[h], :], probs[h], preferred_element_type=F32) for h in range(HP)]
            return tuple((carry[h][0] + jnp.sum(lbs[h], axis=0, keepdims=True), carry[h][1] + pvs[h]) for h in range(HP))

        init = tuple((jnp.zeros((1, blk), F32), jnp.zeros((HEAD_DIM, blk), F32)) for _ in range(HP))
        carry = tile(i, init, strict)
        carry = lax.fori_loop(0, i, lambda jj, c: tile(i - 1 - jj, c, None), carry)
        for h in range(HP):
            oT_ref[h * HEAD_DIM:(h + 1) * HEAD_DIM, :] = carry[h][1]

    return pl.pallas_call(
        body, name=name, grid=(4 // HP, nb),
        in_specs=[pl.BlockSpec((blk, hd), lambda p, i: (i, qcb + p)),
                  pl.BlockSpec((S, hd), lambda p, i: (0, kcb + p)),
                  pl.BlockSpec((S, hd), lambda p, i: (0, vcb + p))],
        out_specs=[pl.BlockSpec((hd, blk), lambda p, i: (p, i)),
                   pl.BlockSpec((HP, 1, nb, 1, blk), lambda p, i: (p, i, 0, 0, 0))],
        out_shape=[jax.ShapeDtypeStruct((GROUP_WIDTH, S), F32), jax.ShapeDtypeStruct((4, nb, nb, 1, blk), F32)],
        scratch_shapes=[pltpu.VMEM((nb, hd, blk), MXU_DTYPE)],
        compiler_params=_cparams("arbitrary", "arbitrary"),
    )(h_att, h_att, h_att)


def _sb_bwd_t(h_att, dmix, dmixT, later_sums, *, dcb, qscale, blk, name):
    S = h_att.shape[0]
    nb = S // blk
    hd = HP * HEAD_DIM
    qcb, kcb, vcb = COL_SQ // hd, COL_SK // hd, COL_SV // hd
    dcr = dcb * 128 // hd

    def body(q_ref, k_ref, v_ref, do_ref, doT_ref, lt_ref, dqT_ref, dk_ref, dv_ref, kT_ref):
        i = pl.program_id(1)

        @pl.when(i == 0)
        def _():
            dk_ref[...] = jnp.zeros_like(dk_ref)
            dv_ref[...] = jnp.zeros_like(dv_ref)
            _fill_transposed(kT_ref, k_ref, nb, blk)

        key = lax.broadcasted_iota(jnp.int32, (blk, blk), 0)
        qry = lax.broadcasted_iota(jnp.int32, (blk, blk), 1)
        strict = key < qry
        later = (qry > key).astype(MXU_DTYPE)
        before = (qry < key).astype(MXU_DTYPE)
        per_head = []
        for h in range(HP):
            hs = slice(h * HEAD_DIM, (h + 1) * HEAD_DIM)
            per_head.append(dict(qT=_t(q_ref[:, hs]), q=q_ref[:, hs], doT=doT_ref[hs, :].astype(MXU_DTYPE),
                                 do=do_ref[:, hs].astype(MXU_DTYPE)))

        def tile(j, carry, mask):
            r0 = pl.multiple_of(j * blk, blk)
            rows = pl.ds(r0, blk)
            hsl = [slice(h * HEAD_DIM, (h + 1) * HEAD_DIM) for h in range(HP)]
            zs = [jnp.dot(k_ref[rows, hsl[h]], per_head[h]["qT"], preferred_element_type=F32) for h in range(HP)]
            das = [jnp.dot(v_ref[rows, hsl[h]], per_head[h]["doT"], preferred_element_type=F32) for h in range(HP)]
            lbs = []
            for h in range(HP):
                lb = _log1m_beta(zs[h])
                lbs.append(lb if mask is None else jnp.where(mask, lb, 0.0))
            sums = [_dot01_left(later, lbs[h]) for h in range(HP)]
            probs, gs = [], []
            for h in range(HP):
                a = jnp.exp(zs[h] + lbs[h] + sums[h] + lt_ref[h, 0, j])
                a = a if mask is None else jnp.where(mask, a, 0.0)
                gs.append(das[h] * a)
                probs.append(a.astype(MXU_DTYPE))
            for h in range(HP):
                dv_ref[rows, hsl[h]] += jnp.dot(probs[h], per_head[h]["do"], preferred_element_type=F32)
            es = [_dot01_left(before, gs[h]) for h in range(HP)]
            dzs = []
            for h in range(HP):
                dz = gs[h] * jnp.exp(lbs[h]) - (carry[h][0] + es[h]) * jnp.exp(zs[h] + lbs[h])
                dzs.append((dz if mask is None else jnp.where(mask, dz, 0.0)).astype(MXU_DTYPE))
            for h in range(HP):
                dk_ref[rows, hsl[h]] += jnp.dot(dzs[h], per_head[h]["q"], preferred_element_type=F32)
            return tuple((carry[h][0] + jnp.sum(gs[h], axis=0, keepdims=True),
                          carry[h][1] + jnp.dot(kT_ref[j, hsl[h], :], dzs[h], preferred_element_type=F32))
                         for h in range(HP))

        init = tuple((jnp.zeros((1, blk), F32), jnp.zeros((HEAD_DIM, blk), F32)) for _ in range(HP))
        carry = lax.fori_loop(0, i, lambda j, c: tile(j, c, None), init)
        carry = tile(i, carry, strict)
        for h in range(HP):
            dqT_ref[h * HEAD_DIM:(h + 1) * HEAD_DIM, :] = carry[h][1] * qscale

    return pl.pallas_call(
        body, name=name, grid=(4 // HP, nb),
        in_specs=[pl.BlockSpec((blk, hd), lambda p, i: (i, qcb + p)),
                  pl.BlockSpec((S, hd), lambda p, i: (0, kcb + p)),
                  pl.BlockSpec((S, hd), lambda p, i: (0, vcb + p)),
                  pl.BlockSpec((blk, hd), lambda p, i: (i, dcr + p)),
                  pl.BlockSpec((hd, blk), lambda p, i: (dcr + p, i)),
                  pl.BlockSpec((HP, 1, nb, 1, blk), lambda p, i: (p, i, 0, 0, 0))],
        out_specs=[pl.BlockSpec((hd, blk), lambda p, i: (p, i)),
                   pl.BlockSpec((S, hd), lambda p, i: (0, p)),
                   pl.BlockSpec((S, hd), lambda p, i: (0, p))],
        out_shape=[jax.ShapeDtypeStruct((GROUP_WIDTH, S), F32), jax.ShapeDtypeStruct((S, GROUP_WIDTH), F32),
                   jax.ShapeDtypeStruct((S, GROUP_WIDTH), F32)],
        scratch_shapes=[pltpu.VMEM((nb, hd, blk), MXU_DTYPE)],
        compiler_params=_cparams("arbitrary", "arbitrary"),
    )(h_att, h_att, h_att, dmix, dmixT, later_sums)


SWA_SUB = 4


def _swa_sub_blocks(S):
    return min(SWA_SUB, S // WINDOW)


def _swa_tiles(n, sub):
    tiles = []
    for b in range(sub):
        start = pl.multiple_of(jnp.maximum(n * sub + b - 1, 0) * WINDOW, WINDOW)
        tiles += [(b, h, slice(b * WINDOW, (b + 1) * WINDOW), start) for h in range(4)]
    return tiles


def _swa_scores(q_ref, k_ref, n, sub, tile):
    b, h, qrows, start = tile
    g = h // 2
    kb = k_ref[pl.ds(start, 2 * WINDOW), g * HEAD_DIM:(g + 1) * HEAD_DIM]
    s = lax.dot_general(q_ref[qrows, h * HEAD_DIM:(h + 1) * HEAD_DIM], kb, NT,
                        preferred_element_type=F32) * (HEAD_DIM ** -0.5)
    dist = ((n * sub + b) * WINDOW + lax.broadcasted_iota(jnp.int32, (WINDOW, 2 * WINDOW), 0)
            - start - lax.broadcasted_iota(jnp.int32, (WINDOW, 2 * WINDOW), 1))
    s = s - SWA_SLOPES[h] * dist.astype(F32)
    valid = (dist >= 0) & (dist < WINDOW)
    return jnp.where(valid, s, NEG_INF), kb


def _swa_fwd(h_att, sinks, *, name):
    S = h_att.shape[0]
    sub = _swa_sub_blocks(S)
    rows = sub * WINDOW
    qcb, kcb, vcb = COL_WQ // 256, COL_WK // 128, COL_WV // 128

    def body(sink_ref, q_ref, k_ref, v_ref, o_ref, lse_ref):
        n = pl.program_id(0)
        tiles = _swa_tiles(n, sub)
        scores = [_swa_scores(q_ref, k_ref, n, sub, t)[0] for t in tiles]
        probs = []
        for (b, h, qrows, start), s in zip(tiles, scores):
            sink = sink_ref[h]
            m = jnp.maximum(jnp.max(s, axis=1, keepdims=True), sink)
            e = jnp.exp(s - m)
            den = jnp.sum(e, axis=1, keepdims=True) + jnp.exp(sink - m)
            probs.append((e / den).astype(MXU_DTYPE))
            lse_ref[h, qrows] = m + jnp.log(den)
        for (b, h, qrows, start), p in zip(tiles, probs):
            vb = v_ref[pl.ds(start, 2 * WINDOW), (h // 2) * HEAD_DIM:(h // 2 + 1) * HEAD_DIM]
            o_ref[qrows, h * HEAD_DIM:(h + 1) * HEAD_DIM] = jnp.dot(p, vb, preferred_element_type=F32)

    return pl.pallas_call(
        body, name=name, grid=(S // rows,),
        in_specs=[pl.BlockSpec(memory_space=pltpu.SMEM),
                  pl.BlockSpec((rows, 256), lambda n: (n, qcb)),
                  pl.BlockSpec((S, 128), lambda n: (0, kcb)),
                  pl.BlockSpec((S, 128), lambda n: (0, vcb))],
        out_specs=[pl.BlockSpec((rows, 256), lambda n: (n, 0)), pl.BlockSpec((4, rows, 1), lambda n: (0, n, 0))],
        out_shape=[jax.ShapeDtypeStruct((S, GROUP_WIDTH), F32), jax.ShapeDtypeStruct((4, S, 1), F32)],
        compiler_params=_cparams("arbitrary"),
    )(sinks, h_att, h_att, h_att)


def _swa_bwd(h_att, sinks, dmix, o_arr, lse, *, dcb, name):
    S = h_att.shape[0]
    sub = _swa_sub_blocks(S)
    rows = sub * WINDOW
    qcb, kcb, vcb = COL_WQ // 256, COL_WK // 128, COL_WV // 128

    def body(sink_ref, q_ref, k_ref, v_ref, do_ref, o_ref, lse_ref, dq_ref, dk_ref, dv_ref, dsink_ref):
        n = pl.program_id(0)

        @pl.when(n == 0)
        def _():
            dk_ref[...] = jnp.zeros_like(dk_ref)
            dv_ref[...] = jnp.zeros_like(dv_ref)
            dsink_ref[...] = jnp.zeros_like(dsink_ref)

        tiles = _swa_tiles(n, sub)
        hsl = [slice(h * HEAD_DIM, (h + 1) * HEAD_DIM) for h in range(4)]
        gsl = [slice(g * HEAD_DIM, (g + 1) * HEAD_DIM) for g in range(2)]
        scale = HEAD_DIM ** -0.5
        sk = [_swa_scores(q_ref, k_ref, n, sub, t) for t in tiles]
        dobs = [do_ref[qrows, hsl[h]].astype(MXU_DTYPE) for b, h, qrows, start in tiles]
        dps = [lax.dot_general(dob, v_ref[pl.ds(start, 2 * WINDOW), gsl[h // 2]], NT, preferred_element_type=F32)
               for (b, h, qrows, start), dob in zip(tiles, dobs)]
        prs, dss = [], []
        for t, (b, h, qrows, start) in enumerate(tiles):
            lse_h = lse_ref[h, qrows]
            pr = jnp.exp(sk[t][0] - lse_h)
            delta = jnp.sum(do_ref[qrows, hsl[h]] * o_ref[qrows, hsl[h]], axis=1, keepdims=True)
            dss.append((pr * (dps[t] - delta)).astype(MXU_DTYPE))
            prs.append(pr.astype(MXU_DTYPE))
            dsink_ref[h:h + 1, :] += jnp.zeros((1, 128), F32) - jnp.sum(jnp.exp(sink_ref[h] - lse_h) * delta)
        for t, (b, h, qrows, start) in enumerate(tiles):
            dq_ref[qrows, hsl[h]] = jnp.dot(dss[t], sk[t][1], preferred_element_type=F32) * scale
        for b in range(sub):
            for g in range(2):
                t0, t1 = 4 * b + 2 * g, 4 * b + 2 * g + 1
                qrows, krows = tiles[t0][2], pl.ds(tiles[t0][3], 2 * WINDOW)
                dk_ref[krows, gsl[g]] += (
                    lax.dot_general(dss[t0], q_ref[qrows, hsl[2 * g]], TN, preferred_element_type=F32)
                    + lax.dot_general(dss[t1], q_ref[qrows, hsl[2 * g + 1]], TN, preferred_element_type=F32)) * scale
                dv_ref[krows, gsl[g]] += (lax.dot_general(prs[t0], dobs[t0], TN, preferred_element_type=F32)
                                          + lax.dot_general(prs[t1], dobs[t1], TN, preferred_element_type=F32))

    return pl.pallas_call(
        body, name=name, grid=(S // rows,),
        in_specs=[pl.BlockSpec(memory_space=pltpu.SMEM),
                  pl.BlockSpec((rows, 256), lambda n: (n, qcb)),
                  pl.BlockSpec((S, 128), lambda n: (0, kcb)),
                  pl.BlockSpec((S, 128), lambda n: (0, vcb)),
                  pl.BlockSpec((rows, 256), lambda n: (n, dcb)),
                  pl.BlockSpec((rows, 256), lambda n: (n, 0)),
                  pl.BlockSpec((4, rows, 1), lambda n: (0, n, 0))],
        out_specs=[pl.BlockSpec((rows, 256), lambda n: (n, 0)),
                   pl.BlockSpec((S, 128), lambda n: (0, 0)),
                   pl.BlockSpec((S, 128), lambda n: (0, 0)),
                   pl.BlockSpec((4, 128), lambda n: (0, 0))],
        out_shape=[jax.ShapeDtypeStruct((S, GROUP_WIDTH), F32), jax.ShapeDtypeStruct((S, 128), F32),
                   jax.ShapeDtypeStruct((S, 128), F32), jax.ShapeDtypeStruct((4, 128), F32)],
        compiler_params=_cparams("arbitrary"),
    )(sinks, h_att, h_att, h_att, dmix, o_arr, lse)


def _tri(n, incl, upper):
    r = lax.broadcasted_iota(jnp.int32, (n, n), 0)
    c = lax.broadcasted_iota(jnp.int32, (n, n), 1)
    if upper:
        m = (r <= c) if incl else (r < c)
    else:
        m = (r >= c) if incl else (r > c)
    return m.astype(MXU_DTYPE)


def _fox_gate_fwd(fg, b_f, *, name):
    _, R, _ = fg.shape

    def body(b_ref, fg_ref, pos_ref, neg_ref):
        up_incl = _tri(128, True, True)
        ones = jnp.ones((128, 128), MXU_DTYPE)
        for h in range(4):
            z = fg_ref[h] + b_ref[h]
            logf = jnp.minimum(z, 0.0) - jnp.log(1.0 + jnp.exp(-jnp.abs(z)))
            within = _dot01(logf, up_incl, parts=3)
            totals = _dot01(logf, ones, parts=3)
            rem = within + _rows_other(totals, R, after=False)
            for part in range(3):
                piece = rem.astype(MXU_DTYPE)
                rem = rem - piece.astype(F32)
                pos_ref[h, part] = piece
                neg_ref[h, part] = -piece

    shape = (4, 3) + fg.shape[1:]
    return pl.pallas_call(
        body, name=name,
        in_specs=[pl.BlockSpec(memory_space=pltpu.SMEM), pl.BlockSpec(memory_space=pltpu.VMEM)],
        out_specs=[pl.BlockSpec(memory_space=pltpu.VMEM)] * 2,
        out_shape=[jax.ShapeDtypeStruct(shape, MXU_DTYPE)] * 2,
    )(b_f, fg)


def _rows_other(totals, n, after):
    r = lax.broadcasted_iota(jnp.int32, (n, n), 0)
    c = lax.broadcasted_iota(jnp.int32, (n, n), 1)
    m = ((c > r) if after else (c < r)).astype(MXU_DTYPE)
    acc = None
    rem = totals
    for _ in range(3):
        part = rem.astype(MXU_DTYPE)
        rem = rem - part.astype(F32)
        t = jnp.dot(m, part, preferred_element_type=F32)
        acc = t if acc is None else acc + t
    return acc


def _fox_gate_bwd(fg, b_f, dcum_k, dcum_q, *, q_unscale, name):
    _, R, _ = fg.shape

    def body(b_ref, fg_ref, dck_ref, dcq_ref, dfg_ref, db_ref):
        low_incl = _tri(128, True, False)
        ones = jnp.ones((128, 128), MXU_DTYPE)
        for h in range(4):
            dc = dcq_ref[h] * q_unscale - dck_ref[h]
            dlogf = _dot01(dc, low_incl, parts=3) + _rows_other(_dot01(dc, ones, parts=3), R, after=True)
            z = fg_ref[h] + b_ref[h]
            dz = dlogf * jnp.exp(jnp.minimum(-z, 0.0) - jnp.log(1.0 + jnp.exp(-jnp.abs(z))))
            dfg_ref[h] = dz
            db_ref[h:h + 1, :] = jnp.zeros((1, 128), F32) + jnp.sum(dz)

    return pl.pallas_call(
        body, name=name,
        in_specs=[pl.BlockSpec(memory_space=pltpu.SMEM)] + [pl.BlockSpec(memory_space=pltpu.VMEM)] * 3,
        out_specs=[pl.BlockSpec(memory_space=pltpu.VMEM), pl.BlockSpec(memory_space=pltpu.VMEM)],
        out_shape=[jax.ShapeDtypeStruct(fg.shape, F32), jax.ShapeDtypeStruct((4, 128), F32)],
    )(b_f, fg, dcum_k, dcum_q)


def _rope_rot(transpose):
    r = lax.broadcasted_iota(jnp.int32, (MLA_PAD, MLA_PAD), 0)
    c = lax.broadcasted_iota(jnp.int32, (MLA_PAD, MLA_PAD), 1)
    if transpose:
        r, c = c, r
    half = MLA_ROPE // 2
    lo, mid, hi = HEAD_DIM, HEAD_DIM + half, HEAD_DIM + MLA_ROPE
    minus = (c >= lo) & (c < mid) & (r == c + half)
    plus = (c >= mid) & (c < hi) & (r == c - half)
    return jnp.where(plus, 1.0, jnp.where(minus, -1.0, 0.0)).astype(MXU_DTYPE)


def _rope_lanes():
    lane = lax.broadcasted_iota(jnp.int32, (1, MLA_PAD), 1)
    return ((lane >= HEAD_DIM) & (lane < HEAD_DIM + MLA_ROPE)).astype(F32)


def _rms(x, g, eps=1e-6):
    r = lax.rsqrt(jnp.mean(x * x, axis=-1, keepdims=True) + eps)
    return x * r * g, r


def _rms_bwd(dy, x, r, g):
    xh = x * r
    dxh = dy * g
    dx = r * (dxh - xh * jnp.mean(dxh * xh, axis=-1, keepdims=True))
    return dx, dy * xh


def _mla_prep_fwd(lat, g_q, g_kv, wuq, wuk, wuv, cosm, sinm, *, bs, name):
    S = lat.shape[0]

    def body(lat_ref, gq_ref, gkv_ref, wuq_ref, wuk_ref, wuv_ref, cos_ref, sin_ref, q_ref, k_ref, v_ref):
        rot = _rope_rot(False)
        cosm_, sinm_ = cos_ref[...], sin_ref[...]
        nq, _ = _rms(lat_ref[:, 0:MLA_Q_RANK], gq_ref[...])
        nkv, _ = _rms(lat_ref[:, MLA_Q_RANK:MLA_Q_RANK + MLA_KV_RANK], gkv_ref[...])
        qlat = jnp.dot(nq.astype(MXU_DTYPE), wuq_ref[...], preferred_element_type=F32)
        klat = jnp.dot(nkv.astype(MXU_DTYPE), wuk_ref[...], preferred_element_type=F32)
        v_ref[...] = jnp.dot(nkv.astype(MXU_DTYPE), wuv_ref[...], preferred_element_type=F32).astype(v_ref.dtype)
        krb = lat_ref[:, 384:512]
        kr = krb * (cosm_ * _rope_lanes()) + _dot01(krb, rot, parts=3) * sinm_
        for h in range(4):
            sl = slice(h * MLA_PAD, (h + 1) * MLA_PAD)
            qh = qlat[:, sl]
            q_ref[:, sl] = ((qh * cosm_ + _dot01(qh, rot, parts=3) * sinm_) * (MLA_QK ** -0.5)).astype(q_ref.dtype)
            k_ref[:, sl] = (klat[:, sl] + kr).astype(k_ref.dtype)

    full = lambda a: pl.BlockSpec(a.shape, lambda i: (0,) * a.ndim)
    return pl.pallas_call(
        body, name=name, grid=(S // bs,),
        in_specs=[pl.BlockSpec((bs, LAT_W), lambda i: (i, 0)), full(g_q), full(g_kv), full(wuq), full(wuk), full(wuv),
                  pl.BlockSpec((bs, MLA_PAD), lambda i: (i, 0)), pl.BlockSpec((bs, MLA_PAD), lambda i: (i, 0))],
        out_specs=[pl.BlockSpec((bs, 512), lambda i: (i, 0)), pl.BlockSpec((bs, 512), lambda i: (i, 0)),
                   pl.BlockSpec((bs, 256), lambda i: (i, 0))],
        out_shape=[jax.ShapeDtypeStruct((S, 512), MXU_DTYPE), jax.ShapeDtypeStruct((S, 512), MXU_DTYPE),
                   jax.ShapeDtypeStruct((S, 256), MXU_DTYPE)],
        compiler_params=_cparams("parallel"),
    )(lat, g_q, g_kv, wuq, wuk, wuv, cosm, sinm)


def _mla_prep_bwd(lat, g_q, g_kv, wuq, wuk, wuv, cosm, sinm, dq, dk, dv, *, bs, name):
    S = lat.shape[0]

    def body(lat_ref, gq_ref, gkv_ref, wuq_ref, wuk_ref, wuv_ref, cos_ref, sin_ref, dq_ref, dk_ref, dv_ref,
             dlat_ref, dwuq_ref, dwuk_ref, dwuv_ref, dgq_ref, dgkv_ref):
        @pl.when(pl.program_id(0) == 0)
        def _():
            for r in (dwuq_ref, dwuk_ref, dwuv_ref, dgq_ref, dgkv_ref):
                r[...] = jnp.zeros_like(r)

        rot_t = _rope_rot(True)
        cosm_, sinm_ = cos_ref[...], sin_ref[...]
        cq = lat_ref[:, 0:MLA_Q_RANK]
        ckv = lat_ref[:, MLA_Q_RANK:MLA_Q_RANK + MLA_KV_RANK]
        nq, rq = _rms(cq, gq_ref[...])
        nkv, rkv = _rms(ckv, gkv_ref[...])
        nqb, nkvb = nq.astype(MXU_DTYPE), nkv.astype(MXU_DTYPE)

        dqlat = []
        dkr = jnp.zeros((bs, MLA_PAD), F32)
        for h in range(4):
            sl = slice(h * MLA_PAD, (h + 1) * MLA_PAD)
            dqh = dq_ref[sl, :].T
            dqlat.append(dqh * cosm_ + _dot01(dqh * sinm_, rot_t, parts=3))
            dkr = dkr + dk_ref[:, sl]
        dqlat = jnp.concatenate(dqlat, axis=1).astype(MXU_DTYPE)
        dkb = dk_ref[...].astype(MXU_DTYPE)
        dvb = dv_ref[...].astype(MXU_DTYPE)

        dnq = lax.dot_general(dqlat, wuq_ref[...], NT, preferred_element_type=F32)
        dnkv = (lax.dot_general(dkb, wuk_ref[...], NT, preferred_element_type=F32)
                + lax.dot_general(dvb, wuv_ref[...], NT, preferred_element_type=F32))
        dwuq_ref[...] += lax.dot_general(nqb, dqlat, TN, preferred_element_type=F32)
        dwuk_ref[...] += lax.dot_general(nkvb, dkb, TN, preferred_element_type=F32)
        dwuv_ref[...] += lax.dot_general(nkvb, dvb, TN, preferred_element_type=F32)
        dcq, tq = _rms_bwd(dnq, cq, rq, gq_ref[...])
        dckv, tkv = _rms_bwd(dnkv, ckv, rkv, gkv_ref[...])
        dgq_ref[...] += jnp.sum(tq, axis=0, keepdims=True)
        dgkv_ref[...] += jnp.sum(tkv, axis=0, keepdims=True)
        dlat_ref[:, 0:MLA_Q_RANK] = dcq.astype(dlat_ref.dtype)
        dlat_ref[:, MLA_Q_RANK:MLA_Q_RANK + MLA_KV_RANK] = dckv.astype(dlat_ref.dtype)
        dkrb = dkr * (cosm_ * _rope_lanes()) + _dot01(dkr * sinm_, rot_t, parts=3)
        dlat_ref[:, 384:512] = dkrb.astype(dlat_ref.dtype)

    full = lambda a: pl.BlockSpec(a.shape, lambda i: (0,) * a.ndim)
    row = lambda w: pl.BlockSpec((bs, w), lambda i: (i, 0))
    acc = lambda *shape: pl.BlockSpec(shape, lambda i: (0,) * len(shape))
    return pl.pallas_call(
        body, name=name, grid=(S // bs,),
        in_specs=[row(LAT_W), full(g_q), full(g_kv), full(wuq), full(wuk), full(wuv), row(MLA_PAD), row(MLA_PAD),
                  pl.BlockSpec((512, bs), lambda i: (0, i)), row(512), row(256)],
        out_specs=[row(512), acc(256, 512), acc(128, 512), acc(128, 256), acc(1, 256), acc(1, 128)],
        out_shape=[jax.ShapeDtypeStruct((S, 512), MXU_DTYPE), jax.ShapeDtypeStruct((256, 512), F32),
                   jax.ShapeDtypeStruct((128, 512), F32), jax.ShapeDtypeStruct((128, 256), F32),
                   jax.ShapeDtypeStruct((1, 256), F32), jax.ShapeDtypeStruct((1, 128), F32)],
        compiler_params=_cparams("arbitrary"),
    )(lat, g_q, g_kv, wuq, wuk, wuv, cosm, sinm, dq, dk, dv)


def _row_spec(bs, w):
    return pl.BlockSpec((bs, w), lambda i: (i, 0))


def _vec_spec(w):
    return pl.BlockSpec((1, w), lambda i: (0, 0))


def _mix_specs(bs):
    return [pl.BlockSpec((GROUP_WIDTH, bs), lambda i: (0, i))] * 3 + [_row_spec(bs, GROUP_WIDTH)]


def _mix_groups(a_ref, b_ref, c_ref, d_ref):
    return [a_ref[...].T, b_ref[...].T, c_ref[...].T, d_ref[...]]


def _gnorm_fwd(outs, g, *, bs, name):
    S = outs[3].shape[0]

    def body(a_ref, b_ref, c_ref, d_ref, g_ref, o_ref, oT_ref):
        for k, x in enumerate(_mix_groups(a_ref, b_ref, c_ref, d_ref)):
            sl = slice(k * GROUP_WIDTH, (k + 1) * GROUP_WIDTH)
            y, _ = _rms(x, g_ref[:, sl])
            o_ref[:, sl] = y.astype(o_ref.dtype)
            oT_ref[sl, :] = y.T.astype(oT_ref.dtype)

    return pl.pallas_call(
        body, name=name, grid=(S // bs,),
        in_specs=_mix_specs(bs) + [_vec_spec(D_MODEL)],
        out_specs=[_row_spec(bs, D_MODEL), pl.BlockSpec((D_MODEL, bs), lambda i: (0, i))],
        out_shape=[jax.ShapeDtypeStruct((S, D_MODEL), MXU_DTYPE), jax.ShapeDtypeStruct((D_MODEL, S), MXU_DTYPE)],
        compiler_params=_cparams("parallel"),
    )(*outs, g)


def _gnorm_bwd(dgn, outs, g, *, bs, name):
    S = dgn.shape[0]

    def body(dgn_ref, a_ref, b_ref, c_ref, d_ref, g_ref, dmix_ref, dmixT_ref, dg_ref):
        @pl.when(pl.program_id(0) == 0)
        def _():
            dg_ref[...] = jnp.zeros_like(dg_ref)

        for k, x in enumerate(_mix_groups(a_ref, b_ref, c_ref, d_ref)):
            sl = slice(k * GROUP_WIDTH, (k + 1) * GROUP_WIDTH)
            _, r = _rms(x, g_ref[:, sl])
            dx, t = _rms_bwd(dgn_ref[:, sl], x, r, g_ref[:, sl])
            dmix_ref[:, sl] = dx
            dmixT_ref[sl, :] = dx.T
            dg_ref[:, sl] += jnp.sum(t, axis=0, keepdims=True)

    return pl.pallas_call(
        body, name=name, grid=(S // bs,),
        in_specs=[_row_spec(bs, D_MODEL)] + _mix_specs(bs) + [_vec_spec(D_MODEL)],
        out_specs=[_row_spec(bs, D_MODEL), pl.BlockSpec((D_MODEL, bs), lambda i: (0, i)), _vec_spec(D_MODEL)],
        out_shape=[jax.ShapeDtypeStruct((S, D_MODEL), F32), jax.ShapeDtypeStruct((D_MODEL, S), F32),
                   jax.ShapeDtypeStruct((1, D_MODEL), F32)],
        compiler_params=_cparams("arbitrary"),
    )(dgn, *outs, g)


def _ln_fwd(u, g, b, *, bs, name):
    S = u.shape[0]

    def body(u_ref, g_ref, b_ref, y_ref, yb_ref, ybT_ref, xh_ref, rs_ref):
        x = u_ref[...]
        mu = jnp.mean(x, axis=-1, keepdims=True)
        xc = x - mu
        rs = lax.rsqrt(jnp.mean(xc * xc, axis=-1, keepdims=True) + 1e-5)
        xh = xc * rs
        y = xh * g_ref[...] + b_ref[...]
        y_ref[...] = y
        yb_ref[...] = y.astype(yb_ref.dtype)
        ybT_ref[...] = y.T.astype(ybT_ref.dtype)
        xh_ref[...] = xh
        rs_ref[...] = rs

    return pl.pallas_call(
        body, name=name, grid=(S // bs,),
        in_specs=[_row_spec(bs, D_MODEL), _vec_spec(D_MODEL), _vec_spec(D_MODEL)],
        out_specs=[_row_spec(bs, D_MODEL), _row_spec(bs, D_MODEL), pl.BlockSpec((D_MODEL, bs), lambda i: (0, i)),
                   _row_spec(bs, D_MODEL), _row_spec(bs, 1)],
        out_shape=[jax.ShapeDtypeStruct((S, D_MODEL), F32), jax.ShapeDtypeStruct((S, D_MODEL), MXU_DTYPE),
                   jax.ShapeDtypeStruct((D_MODEL, S), MXU_DTYPE), jax.ShapeDtypeStruct((S, D_MODEL), F32),
                   jax.ShapeDtypeStruct((S, 1), F32)],
        compiler_params=_cparams("parallel"),
    )(u, g, b)


def _ln_bwd(dy, xh, rs, g, *, bs, name):
    S = dy.shape[0]

    def body(dy_ref, xh_ref, rs_ref, g_ref, du_ref, dub_ref, dg_ref, db_ref):
        @pl.when(pl.program_id(0) == 0)
        def _():
            dg_ref[...] = jnp.zeros_like(dg_ref)
            db_ref[...] = jnp.zeros_like(db_ref)

        dy_, xh_ = dy_ref[...], xh_ref[...]
        dxh = dy_ * g_ref[...]
        du = rs_ref[...] * (dxh - jnp.mean(dxh, axis=-1, keepdims=True)
                            - xh_ * jnp.mean(dxh * xh_, axis=-1, keepdims=True))
        du_ref[...] = du
        dub_ref[...] = du.astype(dub_ref.dtype)
        dg_ref[...] += jnp.sum(dy_ * xh_, axis=0, keepdims=True)
        db_ref[...] += jnp.sum(dy_, axis=0, keepdims=True)

    return pl.pallas_call(
        body, name=name, grid=(S // bs,),
        in_specs=[_row_spec(bs, D_MODEL), _row_spec(bs, D_MODEL), _row_spec(bs, 1), _vec_spec(D_MODEL)],
        out_specs=[_row_spec(bs, D_MODEL), _row_spec(bs, D_MODEL), _vec_spec(D_MODEL), _vec_spec(D_MODEL)],
        out_shape=[jax.ShapeDtypeStruct((S, D_MODEL), F32), jax.ShapeDtypeStruct((S, D_MODEL), MXU_DTYPE),
                   jax.ShapeDtypeStruct((1, D_MODEL), F32), jax.ShapeDtypeStruct((1, D_MODEL), F32)],
        compiler_params=_cparams("arbitrary"),
    )(dy, xh, rs, g)


def _swiglu_fwd(gu, *, bs, name):
    S = gu.shape[0]

    def body(gu_ref, a_ref, aT_ref):
        gt = gu_ref[:, :D_FF]
        a = gt / (1.0 + jnp.exp(-gt)) * gu_ref[:, D_FF:]
        a_ref[...] = a.astype(a_ref.dtype)
        aT_ref[...] = a.T.astype(aT_ref.dtype)

    return pl.pallas_call(
        body, name=name, grid=(S // bs,),
        in_specs=[_row_spec(bs, 2 * D_FF)],
        out_specs=[_row_spec(bs, D_FF), pl.BlockSpec((D_FF, bs), lambda i: (0, i))],
        out_shape=[jax.ShapeDtypeStruct((S, D_FF), MXU_DTYPE), jax.ShapeDtypeStruct((D_FF, S), MXU_DTYPE)],
        compiler_params=_cparams("parallel"),
    )(gu)


def _swiglu_bwd(da, gu, *, bs, name):
    S = gu.shape[0]

    def body(da_ref, gu_ref, dgu_ref):
        gt, da_ = gu_ref[:, :D_FF], da_ref[...]
        sg = 1.0 / (1.0 + jnp.exp(-gt))
        silu = gt * sg
        dgu_ref[:, :D_FF] = (da_ * gu_ref[:, D_FF:] * (sg + silu * (1.0 - sg))).astype(dgu_ref.dtype)
        dgu_ref[:, D_FF:] = (da_ * silu).astype(dgu_ref.dtype)

    return pl.pallas_call(
        body, name=name, grid=(S // bs,),
        in_specs=[_row_spec(bs, D_FF), _row_spec(bs, 2 * D_FF)],
        out_specs=_row_spec(bs, 2 * D_FF), out_shape=jax.ShapeDtypeStruct((S, 2 * D_FF), MXU_DTYPE),
        compiler_params=_cparams("parallel"),
    )(da, gu)


def _loss_head(y, target, *, bs, name):
    S = y.shape[0]

    def body(y_ref, t_ref, dy_ref, loss_ref):
        @pl.when(pl.program_id(0) == 0)
        def _():
            loss_ref[...] = jnp.zeros_like(loss_ref)

        e = y_ref[...] - t_ref[...]
        dy_ref[...] = e * (1.0 / D_MODEL)
        per_tok = jnp.mean(e * e, axis=-1, keepdims=True)
        loss_ref[...] += 0.5 * jnp.sum(per_tok, axis=0, keepdims=True)

    return pl.pallas_call(
        body, name=name, grid=(S // bs,),
        in_specs=[_row_spec(bs, D_MODEL), _row_spec(bs, D_MODEL)],
        out_specs=[_row_spec(bs, D_MODEL), pl.BlockSpec((1, 1), lambda i: (0, 0))],
        out_shape=[jax.ShapeDtypeStruct((S, D_MODEL), F32), jax.ShapeDtypeStruct((1, 1), F32)],
        compiler_params=_cparams("arbitrary"),
    )(y, target)


def _blk(n, target):
    if n <= target:
        return n
    best = None
    for b in range(128, target + 1, 128):
        if n % b == 0:
            best = b
    assert best is not None, n
    return best


def _rope_tables(S):
    pos = jnp.arange(S, dtype=F32)
    inv = ROPE_THETA ** (-jnp.arange(0, MLA_ROPE, 2, dtype=F32) / MLA_ROPE)
    ang = pos[:, None] * inv[None, :]
    cos, sin = jnp.cos(ang), jnp.sin(ang)
    one, zero, pad = jnp.ones((S, HEAD_DIM), F32), jnp.zeros((S, HEAD_DIM), F32), jnp.zeros((S, MLA_PAD - MLA_QK), F32)
    return jnp.concatenate([one, cos, cos, pad], axis=1), jnp.concatenate([zero, sin, sin, pad], axis=1)


def _prep_weights_a(w_in, w_uq, w_ukv):
    z = lambda n: jnp.zeros((D_MODEL, n), w_in.dtype)
    win_a = jnp.concatenate([w_in[:, 0:768], w_in[:, 1188:2468]], axis=1)
    win_l = jnp.concatenate([w_in[:, 772:1156], z(64), w_in[:, 1156:1188], z(32), w_in[:, 768:772], z(124)], axis=1)
    kv = w_ukv.reshape(MLA_KV_RANK, 4, 2 * HEAD_DIM)
    return dict(
        win_a=win_a, win_l=win_l, win_p=jnp.concatenate([win_a, win_l], axis=1),
        wuq=jnp.pad(w_uq.reshape(MLA_Q_RANK, 4, MLA_QK), ((0, 0), (0, 0), (0, MLA_PAD - MLA_QK))).reshape(MLA_Q_RANK, 512),
        wuk=jnp.pad(kv[:, :, :HEAD_DIM], ((0, 0), (0, 0), (0, HEAD_DIM))).reshape(MLA_KV_RANK, 512),
        wuv=kv[:, :, HEAD_DIM:].reshape(MLA_KV_RANK, 256))


def _unprep_grads(dwin_p, dwuq, dwuk, dwuv, dwo, dwgu, dwd):
    dw_in = jnp.concatenate([dwin_p[:, 0:768], dwin_p[:, 2560:2564], dwin_p[:, 2048:2432], dwin_p[:, 2496:2528],
                             dwin_p[:, 768:2048]], axis=1)
    dw_uq = dwuq.reshape(MLA_Q_RANK, 4, MLA_PAD)[:, :, :MLA_QK].reshape(MLA_Q_RANK, 4 * MLA_QK)
    dw_ukv = jnp.concatenate([dwuk.reshape(MLA_KV_RANK, 4, MLA_PAD)[:, :, :HEAD_DIM],
                              dwuv.reshape(MLA_KV_RANK, 4, HEAD_DIM)], axis=2).reshape(MLA_KV_RANK, 512)
    return dict(w_in=dw_in, mla_w_uq=dw_uq, mla_w_ukv=dw_ukv, w_o=dwo, w_gate=dwgu[:, :D_FF], w_up=dwgu[:, D_FF:],
                w_down=dwd)


def _layer_fwd(l, x, xb, xbT, W, P, tabs, blk, late_weights=None):
    S = x.shape[0]
    nb = S // blk
    n = lambda s: f"l{l}_{s}"
    bs = min(512, S)
    h_att = _mm(xb, W["win_a"], name=n("in_att"), out_dtype=MXU_DTYPE, bm=1024, bn=1024, bk=1024, colscale=Q_COLSCALE)
    lat = _mm(xb, W["win_l"], name=n("in_lat"), bm=2048, bn=LAT_W, bk=1024)
    fg = lat[:, 512:516].T.reshape(4, S // 128, 128)
    cpos, cneg = _fox_gate_fwd(fg, P["fox_b_f"], name=n("fox_gate"))
    one3 = jnp.ones((S, 4, 3), MXU_DTYPE)
    zpad = jnp.zeros((S, 4, MLA_PAD - HEAD_DIM - 6), MXU_DTYPE)
    per_tok = lambda parts: parts.reshape(4, 3, S).transpose(2, 0, 1)
    q_f = jnp.concatenate([h_att[:, COL_FQ:COL_FQ + 256].reshape(S, 4, HEAD_DIM), per_tok(cpos), one3, zpad],
                          axis=2).reshape(S, 4 * MLA_PAD)
    k_f = jnp.concatenate([h_att[:, COL_FK:COL_FK + 256].reshape(S, 4, HEAD_DIM), one3, per_tok(cneg), zpad],
                          axis=2).reshape(S, 4 * MLA_PAD)
    v_f = h_att[:, COL_FV:COL_FV + 256]
    oT_a, lse_a = _smax_fwd_t(q_f, k_f, v_f, dk=MLA_PAD, blk=blk, name=n("fox_fwd"))
    q_m, k_m, v_m = _mla_prep_fwd(lat, P["mla_g_q"], P["mla_g_kv"], W["wuq"], W["wuk"], W["wuv"], *tabs,
                                  bs=bs, name=n("mla_prep"))
    oT_b, lse_b = _smax_fwd_t(q_m, k_m, v_m, dk=MLA_PAD, blk=blk, name=n("mla_fwd"))
    bsb = min(BLK_STICK, S)
    oT_c, lt_c = _sb_fwd_t(h_att, blk=bsb, name=n("sb_fwd"))
    out_d, lse_d = _swa_fwd(h_att, P["swa_sinks"], name=n("swa_fwd"))
    outs = (oT_a, oT_b, oT_c, out_d)
    gn, gnT = _gnorm_fwd(outs, P["mix_g"], bs=bs, name=n("gnorm"))
    if late_weights is not None:
        W = dict(W, **late_weights(gn))
    u1 = _mm(gn, W["w_o"], name=n("out_proj"), bm=1024, bn=1024, bk=1024, resid=x, alpha=ALPHA)
    x1, x1b, x1bT, xh1, rs1 = _ln_fwd(u1, P["ln1_g"], P["ln1_b"], bs=bs, name=n("ln1"))
    gu = _mm(x1b, W["wgu"], name=n("gate_up"), bm=2048, bn=512, bk=1024)
    a, aT = _swiglu_fwd(gu, bs=min(256, S), name=n("swiglu"))
    u2 = _mm(a, W["w_down"], name=n("down"), bm=1024, bn=1024, bk=_blk(D_FF, 1408), resid=x1, alpha=ALPHA)
    x2, x2b, x2bT, xh2, rs2 = _ln_fwd(u2, P["ln2_g"], P["ln2_b"], bs=bs, name=n("ln2"))
    saved = dict(xbT=xbT, gnT=gnT, x1bT=x1bT, h_att=h_att, lat=lat, fg=fg, outs=outs, oT_a=oT_a, oT_b=oT_b, q_f=q_f, k_f=k_f, v_f=v_f,
                 lse_a=lse_a, lse_b=lse_b, lse_d=lse_d, lt_c=lt_c, q_m=q_m, k_m=k_m, v_m=v_m,
                 xh1=xh1, rs1=rs1, gu=gu, aT=aT, xh2=xh2, rs2=rs2)
    return x2, x2b, x2bT, saved, W


def _layer_bwd(l, dx2, sv, W, P, tabs, blk, send_early=None):
    S = dx2.shape[0]
    n = lambda s: f"l{l}_{s}"
    bs = min(512, S)
    h_att = sv["h_att"]
    du2, du2b, dg2, db2 = _ln_bwd(dx2, sv["xh2"], sv["rs2"], P["ln2_g"], bs=bs, name=n("ln2_bwd"))
    da = _mm(du2b, W["w_down"], name=n("down_dx"), tb=True, bm=1024, bn=_blk(D_FF, 1408), bk=1024)
    dwd = _mm(sv["aT"], du2b, name=n("down_dw"), bm=_blk(D_FF, 1408), bn=1024, bk=1024)
    dgu = _swiglu_bwd(da, sv["gu"], bs=min(256, S), name=n("swiglu_bwd"))
    dx1 = _mm(dgu, W["wgu"], name=n("gate_up_dx"), tb=True, bm=1024, bn=1024, bk=_blk(2 * D_FF, 1408), resid=du2,
              alpha=ALPHA)
    dwgu = _mm(sv["x1bT"], dgu, name=n("gate_up_dw"), bm=1024, bn=_blk(2 * D_FF, 1408), bk=1024)
    du1, du1b, dg1, db1 = _ln_bwd(dx1, sv["xh1"], sv["rs1"], P["ln1_g"], bs=bs, name=n("ln1_bwd"))
    dgn = _mm(du1b, W["w_o"], name=n("out_proj_dx"), tb=True, bm=1024, bn=1024, bk=1024)
    dwo = _mm(sv["gnT"], du1b, name=n("out_proj_dw"), bm=1024, bn=1024, bk=1024)
    mix_g = P["mix_g"]
    if send_early is not None:
        mix_g = mix_g + send_early(dict(w_o=dwo, w_gate=dwgu[:, :D_FF], w_up=dwgu[:, D_FF:], w_down=dwd))[0, 0]
    dmix, dmixT, dmixg = _gnorm_bwd(dgn, sv["outs"], mix_g, bs=bs, name=n("gnorm_bwd"))
    q_f, k_f = sv["q_f"], sv["k_f"]
    dqT_a, dk_a, dva = _smax_bwd_t(q_f, k_f, sv["v_f"], dmix, dmixT, sv["oT_a"], sv["lse_a"], dk=MLA_PAD, dcb=0,
                                   qscale=HEAD_DIM ** -0.5, blk=blk, name=n("fox_bwd"))
    dqT_a, dk_a = dqT_a.reshape(4, MLA_PAD, S), dk_a.reshape(S, 4, MLA_PAD)
    dqa, dka = dqT_a[:, :HEAD_DIM].reshape(256, S).T, dk_a[:, :, :HEAD_DIM].reshape(S, 256)
    dcq = dqT_a[:, HEAD_DIM].reshape(4, S // 128, 128)
    dck = dk_a[:, :, HEAD_DIM + 3].T.reshape(4, S // 128, 128)
    q_m, k_m = sv["q_m"], sv["k_m"]
    dqT_b, dkb, dvb = _smax_bwd_t(q_m, k_m, sv["v_m"], dmix, dmixT, sv["oT_b"], sv["lse_b"], dk=MLA_PAD, dcb=2,
                                  qscale=MLA_QK ** -0.5, blk=blk, name=n("mla_bwd"))
    dqT_c, dkc, dvc = _sb_bwd_t(h_att, dmix, dmixT, sv["lt_c"], dcb=4, qscale=HEAD_DIM ** -0.5,
                                blk=min(BLK_STICK, S), name=n("sb_bwd"))
    dqc = dqT_c.T
    dqd, dkd, dvd, dsink = _swa_bwd(h_att, P["swa_sinks"], dmix, sv["outs"][3], sv["lse_d"], dcb=3, name=n("swa_bwd"))
    dlat, dwuq, dwuk, dwuv, dgq, dgkv = _mla_prep_bwd(
        sv["lat"], P["mla_g_q"], P["mla_g_kv"], W["wuq"], W["wuk"], W["wuv"], *tabs, dqT_b, dkb, dvb,
        bs=bs, name=n("mla_prep_bwd"))
    dfg, dbf = _fox_gate_bwd(sv["fg"], P["fox_b_f"], dck, dcq, q_unscale=HEAD_DIM ** 0.5, name=n("fox_gate_bwd"))
    dfg_blk = jnp.pad(dfg.reshape(4, S).T, ((0, 0), (0, 124)))
    dh = jnp.concatenate([t.astype(MXU_DTYPE) for t in (dqa, dka, dva, dqc, dkc, dvc, dqd, dkd, dvd, dlat, dfg_blk)], axis=1)
    dx = _mm(dh, W["win_p"], name=n("in_dx"), tb=True, bm=1024, bn=1024, bk=_blk(PERM_W, 1024), resid=du1, alpha=ALPHA)
    dwin_p = _mm(sv["xbT"], dh, name=n("in_dw"), bm=1024, bn=_blk(PERM_W, 1024), bk=1024)
    grads = _unprep_grads(dwin_p, dwuq, dwuk, dwuv, dwo, dwgu, dwd)
    grads.update(fox_b_f=dbf[:, 0], mla_g_q=dgq[0], mla_g_kv=dgkv[0], swa_sinks=dsink[:, 0], mix_g=dmixg[0],
                 ln1_g=dg1[0], ln1_b=db1[0], ln2_g=dg2[0], ln2_b=db2[0])
    return dx, grads


BIG = ("w_in", "mla_w_uq", "mla_w_ukv", "w_o", "w_gate", "w_up", "w_down")
SMALL = ("fox_b_f", "mla_g_q", "mla_g_kv", "swa_sinks", "mix_g", "ln1_g", "ln1_b", "ln2_g", "ln2_b")
SHARD_AXIS = dict(w_in=2, mla_w_uq=2, mla_w_ukv=2, w_o=1, w_gate=2, w_up=2, w_down=1)
N_CHIPS = 4
ANY = pl.BlockSpec(memory_space=pl.ANY)


HBM = pl.BlockSpec(memory_space=pltpu.HBM)
SEM = pl.BlockSpec(memory_space=pltpu.SEMAPHORE)
N_PEER_CHIPS = N_CHIPS - 1


def _peer_copies(src_ref, land_ref, sems, scatter):
    x, y, c = lax.axis_index("x"), lax.axis_index("y"), lax.axis_index("c")
    me = 2 * x + y
    out = []
    for r, (px, py) in enumerate([(1 - x, y), (x, 1 - y), (1 - x, 1 - y)]):
        theirs = 2 * px + py
        send = pltpu.make_async_remote_copy(
            src_ref=src_ref.at[theirs] if scatter else src_ref, dst_ref=land_ref.at[me],
            send_sem=sems[2 * r], recv_sem=sems[2 * r + 1], device_id=(px, py, c), device_id_type=MESH)
        arrive = pltpu.make_async_remote_copy(
            src_ref=src_ref.at[me] if scatter else src_ref, dst_ref=land_ref.at[theirs],
            send_sem=sems[2 * r], recv_sem=sems[2 * r + 1], device_id=(px, py, c), device_id_type=MESH)
        out.append((send, arrive))
    return out


def _exchange_start(srcs, *, scatter, name):
    nt = len(srcs)
    ns = 2 * N_PEER_CHIPS * nt
    land_shapes = [s.shape if scatter else (N_CHIPS,) + s.shape for s in srcs]

    def body(*refs):
        src_refs, land_refs, outs = refs[:nt], refs[nt:2 * nt], refs[2 * nt:]
        for t in range(nt):
            for send, _ in _peer_copies(src_refs[t], land_refs[t], outs[6 * t:6 * t + 6], scatter):
                send.start()
        outs[-1][...] = jnp.zeros_like(outs[-1])

    res = pl.pallas_call(
        body, name=name,
        out_shape=(*[pltpu.SemaphoreType.DMA(())] * ns, *[pltpu.HBM(s.shape, s.dtype) for s in srcs],
                   *[pltpu.HBM(ls, s.dtype) for ls, s in zip(land_shapes, srcs)], jax.ShapeDtypeStruct((8, 128), F32)),
        in_specs=(HBM,) * (2 * nt), out_specs=(*[SEM] * ns, *[HBM] * (2 * nt), pl.BlockSpec(memory_space=pltpu.VMEM)),
        input_output_aliases={i: ns + i for i in range(2 * nt)},
        compiler_params=pltpu.CompilerParams(has_side_effects=pltpu.SideEffectType.DATAFLOW_SIDE_EFFECTING),
    )(*[pltpu.with_memory_space_constraint(s, pltpu.HBM) for s in srcs],
      *[pltpu.with_memory_space_constraint(lax.empty(ls, s.dtype), pltpu.HBM) for ls, s in zip(land_shapes, srcs)])
    return dict(sems=res[:ns], srcs=res[ns:ns + nt], lands=res[ns + nt:ns + 2 * nt], token=res[-1])


def _exchange_wait(started, after, *, scatter, name):
    nt = len(started["srcs"])
    ns = 2 * N_PEER_CHIPS * nt

    def body(*refs):
        src_refs, land_refs, sems = refs[:nt], refs[nt:2 * nt], refs[2 * nt:2 * nt + ns]
        for t in range(nt):
            for send, arrive in _peer_copies(src_refs[t], land_refs[t], sems[6 * t:6 * t + 6], scatter):
                send.wait_send()
                arrive.wait_recv()

    both = list(started["srcs"]) + list(started["lands"])
    res = pl.pallas_call(
        body, name=name, out_shape=tuple(pltpu.HBM(a.shape, a.dtype) for a in both),
        in_specs=(*[HBM] * (2 * nt), *[SEM] * ns, ANY), out_specs=(HBM,) * (2 * nt),
        input_output_aliases={i: i for i in range(2 * nt)},
        compiler_params=pltpu.CompilerParams(has_side_effects=pltpu.SideEffectType.DATAFLOW_SIDE_EFFECTING),
    )(*both, *started["sems"], after)
    return res[:nt], res[nt:]


def _core_exchange(tensors, *, name):
    nt = len(tensors)

    def body(*refs):
        ins, outs = refs[:nt], refs[nt:2 * nt]
        send_sems, recv_sems = refs[2 * nt:]
        sibling = (lax.axis_index("x"), lax.axis_index("y"), 1 - lax.axis_index("c"))
        copies = [pltpu.make_async_remote_copy(src_ref=ins[t], dst_ref=outs[t], send_sem=send_sems.at[t],
                                               recv_sem=recv_sems.at[t], device_id=sibling, device_id_type=MESH)
                  for t in range(nt)]
        for cp in copies:
            cp.start()
        for cp in copies:
            cp.wait_recv()
        for cp in copies:
            cp.wait_send()

    return pl.pallas_call(
        body, name=name, in_specs=[ANY] * nt, out_specs=[ANY] * nt,
        out_shape=[jax.ShapeDtypeStruct(t.shape, t.dtype) for t in tensors],
        scratch_shapes=[pltpu.SemaphoreType.DMA((nt,)), pltpu.SemaphoreType.DMA((nt,))],
        compiler_params=pltpu.CompilerParams(has_side_effects=True),
    )(*tensors)


def _all_sum_small(block, *, name):
    R = block.shape[0]
    n_dev = 8

    def body(x_ref, o_ref, slots, send_sems, recv_sems):
        x, y, c = lax.axis_index("x"), lax.axis_index("y"), lax.axis_index("c")
        me = 4 * x + 2 * y + c
        slots[me] = x_ref[...]
        sends, recvs = [], []
        for d in range(1, n_dev):
            px, py, pc = x ^ (d >> 2), y ^ ((d >> 1) & 1), c ^ (d & 1)
            theirs = 4 * px + 2 * py + pc
            sends.append(pltpu.make_async_remote_copy(
                src_ref=x_ref, dst_ref=slots.at[me], send_sem=send_sems.at[d - 1], recv_sem=recv_sems.at[d - 1],
                device_id=(px, py, pc), device_id_type=MESH))
            recvs.append(pltpu.make_async_remote_copy(
                src_ref=x_ref, dst_ref=slots.at[theirs], send_sem=send_sems.at[d - 1], recv_sem=recv_sems.at[d - 1],
                device_id=(px, py, pc), device_id_type=MESH))
        for cp in sends:
            cp.start()
        for cp in recvs:
            cp.wait_recv()
        for cp in sends:
            cp.wait_send()
        total = slots[0]
        for k in range(1, n_dev):
            total = total + slots[k]
        o_ref[...] = total

    return pl.pallas_call(
        body, name=name, in_specs=[pl.BlockSpec(memory_space=pltpu.VMEM)],
        out_specs=pl.BlockSpec(memory_space=pltpu.VMEM), out_shape=jax.ShapeDtypeStruct((R, 128), F32),
        scratch_shapes=[pltpu.VMEM((n_dev, R, 128), F32), pltpu.SemaphoreType.DMA((n_dev - 1,)),
                        pltpu.SemaphoreType.DMA((n_dev - 1,))],
        compiler_params=pltpu.CompilerParams(has_side_effects=True),
    )(block)


def _sum_chips_into(acc, land, own, me, layer, *, br, name):
    _, R, C = land.shape

    def body(me_ref, land_ref, own_ref, acc_ref, o_ref):
        mine = me_ref[0]
        total = None
        for k in range(N_CHIPS):
            part = jnp.where(mine == k, own_ref[...], land_ref[k]).astype(F32)
            total = part if total is None else total + part
        o_ref[0] = total

    return pl.pallas_call(
        body, name=name, grid=(R // br,),
        in_specs=[pl.BlockSpec(memory_space=pltpu.SMEM), pl.BlockSpec((N_CHIPS, br, C), lambda i: (0, i, 0)),
                  pl.BlockSpec((br, C), lambda i: (i, 0)), ANY],
        out_specs=pl.BlockSpec((1, br, C), lambda i: (layer, i, 0)),
        out_shape=jax.ShapeDtypeStruct(acc.shape, F32), input_output_aliases={3: 0},
        compiler_params=_cparams("parallel"),
    )(me, land, own, acc)


def _adamw_math(w, g, m, v):
    m = ADAM_B1 * m + (1.0 - ADAM_B1) * g
    v = ADAM_B2 * v + (1.0 - ADAM_B2) * (g * g)
    m_hat = m / (1.0 - ADAM_B1 ** ADAM_STEP)
    v_hat = v / (1.0 - ADAM_B2 ** ADAM_STEP)
    return -ADAM_LR * (m_hat / (jnp.sqrt(v_hat) + ADAM_EPS) + ADAM_WD * w), m, v


def _adamw(w, m, v, g_a, g_b, *, br, name):
    R, C = w.shape
    two = g_b is not None

    def body(*refs):
        if two:
            w_ref, m_ref, v_ref, ga_ref, gb_ref, g_ref, d_ref, nm_ref, nv_ref = refs
            g = ga_ref[...] + gb_ref[...]
        else:
            w_ref, m_ref, v_ref, ga_ref, g_ref, d_ref, nm_ref, nv_ref = refs
            g = ga_ref[...]
        g_ref[...] = g
        d_ref[...], nm_ref[...], nv_ref[...] = _adamw_math(w_ref[...], g, m_ref[...], v_ref[...])

    spec = pl.BlockSpec((br, C), lambda i: (i, 0))
    args = [w, m, v, g_a] + ([g_b] if two else [])
    return pl.pallas_call(
        body, name=name, grid=(R // br,), in_specs=[spec] * len(args), out_specs=[spec] * 4,
        out_shape=[jax.ShapeDtypeStruct((R, C), F32)] * 4,
        compiler_params=_cparams("parallel"),
    )(*args)


SMALL_ROWS = dict(fox_b_f=1, mla_g_q=2, mla_g_kv=1, swa_sinks=1, mix_g=8, ln1_g=8, ln1_b=8, ln2_g=8, ln2_b=8)
SMALL_ROWS_PER_LAYER = sum(SMALL_ROWS.values())


def _pack_small(vals, extra_rows):
    L = vals[SMALL[0]].shape[0]
    per_layer = []
    for name in SMALL:
        a = vals[name].astype(F32)
        a = jnp.pad(a, ((0, 0), (0, SMALL_ROWS[name] * 128 - a.shape[1])))
        per_layer.append(a.reshape(L, SMALL_ROWS[name], 128))
    out = jnp.concatenate(per_layer, axis=1).reshape(L * SMALL_ROWS_PER_LAYER, 128)
    return jnp.pad(out, ((0, extra_rows), (0, 0)))


def _unpack_small(block, shapes):
    L = shapes[SMALL[0]][0]
    body = block[:L * SMALL_ROWS_PER_LAYER].reshape(L, SMALL_ROWS_PER_LAYER, 128)
    out, r = {}, 0
    for name in SMALL:
        n = shapes[name][1]
        out[name] = body[:, r:r + SMALL_ROWS[name]].reshape(L, SMALL_ROWS[name] * 128)[:, :n]
        r += SMALL_ROWS[name]
    return out


def _to_chips(g, axis):
    L, a, b = g.shape
    if axis == 2:
        return g.reshape(L, a, N_CHIPS, b // N_CHIPS).transpose(2, 0, 1, 3)
    return g.reshape(L, N_CHIPS, a // N_CHIPS, b).transpose(1, 0, 2, 3)


def kernel(x, w_in, fox_b_f, mla_g_q, mla_g_kv, mla_w_uq, mla_w_ukv, swa_sinks, mix_g, w_o, ln1_g, ln1_b, w_gate, w_up, w_down, ln2_g, ln2_b, loss_target, m_w_in, m_fox_b_f, m_mla_g_q, m_mla_g_kv, m_mla_w_uq, m_mla_w_ukv, m_swa_sinks, m_mix_g, m_w_o, m_ln1_g, m_ln1_b, m_w_gate, m_w_up, m_w_down, m_ln2_g, m_ln2_b, v_w_in, v_fox_b_f, v_mla_g_q, v_mla_g_kv, v_mla_w_uq, v_mla_w_ukv, v_swa_sinks, v_mix_g, v_w_o, v_ln1_g, v_ln1_b, v_w_gate, v_w_up, v_w_down, v_ln2_g, v_ln2_b):
    w = dict(w_in=w_in, fox_b_f=fox_b_f, mla_g_q=mla_g_q, mla_g_kv=mla_g_kv, mla_w_uq=mla_w_uq, mla_w_ukv=mla_w_ukv,
             swa_sinks=swa_sinks, mix_g=mix_g, w_o=w_o, ln1_g=ln1_g, ln1_b=ln1_b, w_gate=w_gate, w_up=w_up,
             w_down=w_down, ln2_g=ln2_g, ln2_b=ln2_b)
    m = dict(w_in=m_w_in, fox_b_f=m_fox_b_f, mla_g_q=m_mla_g_q, mla_g_kv=m_mla_g_kv, mla_w_uq=m_mla_w_uq,
             mla_w_ukv=m_mla_w_ukv, swa_sinks=m_swa_sinks, mix_g=m_mix_g, w_o=m_w_o, ln1_g=m_ln1_g, ln1_b=m_ln1_b,
             w_gate=m_w_gate, w_up=m_w_up, w_down=m_w_down, ln2_g=m_ln2_g, ln2_b=m_ln2_b)
    v = dict(w_in=v_w_in, fox_b_f=v_fox_b_f, mla_g_q=v_mla_g_q, mla_g_kv=v_mla_g_kv, mla_w_uq=v_mla_w_uq,
             mla_w_ukv=v_mla_w_ukv, swa_sinks=v_swa_sinks, mix_g=v_mix_g, w_o=v_w_o, ln1_g=v_ln1_g, ln1_b=v_ln1_b,
             w_gate=v_w_gate, w_up=v_w_up, w_down=v_w_down, ln2_g=v_ln2_g, ln2_b=v_ln2_b)
    names = tuple(w)
    L = w_in.shape[0]
    S = x.shape[1]
    blk = min(BLK_SOFTMAX, S)
    bs = min(512, S)

    me = 2 * lax.axis_index("x") + lax.axis_index("y")
    axis_of = {k: SHARD_AXIS[k] - 1 for k in BIG}
    groups = (("w_in", "mla_w_uq", "mla_w_ukv"), ("w_o", "w_gate", "w_up", "w_down"))

    started, last = [], None
    for l in range(L):
        per_group = []
        for g, group in enumerate(groups):
            srcs = [w[k][l].astype(MXU_DTYPE) for k in group]
            if last is not None:
                t = min(range(len(srcs)), key=lambda i: srcs[i].size)
                srcs[t] = srcs[t] + last["token"][0, 0].astype(MXU_DTYPE)
            last = _exchange_start(srcs, scatter=False, name=f"gather_start{l}_{g}")
            per_group.append(last)
        started.append(per_group)
    all_started = sum(st["token"] for per_group in started for st in per_group)

    def gathered(l, g, after):
        mine, lands = _exchange_wait(started[l][g], after, scatter=False, name=f"gather_wait{l}_{g}")
        shard = lambda t, k: jnp.where(me == k, mine[t], lands[t][k])
        whole = lambda t, axis: jnp.concatenate([shard(t, k) for k in range(N_CHIPS)], axis=axis)
        if g == 0:
            return _prep_weights_a(*[whole(t, axis_of[name]) for t, name in enumerate(groups[0])])
        gate_up = jnp.concatenate([shard(t, k) for t in (1, 2) for k in range(N_CHIPS)], axis=1)
        return dict(w_o=whole(0, 0), wgu=gate_up, w_down=whole(3, 0))

    def scatter(l, g, grads):
        to_owner = [_to_chips(grads[k].astype(MXU_DTYPE)[None], axis_of[k] + 1)[:, 0] for k in groups[g]]
        return _exchange_start(to_owner, scatter=True, name=f"scatter_start{l}_{g}")

    tabs = _rope_tables(S)
    Ps = []
    for l in range(L):
        P = dict(fox_b_f=fox_b_f[l], swa_sinks=swa_sinks[l])
        for k in ("mla_g_q", "mla_g_kv", "mix_g", "ln1_g", "ln1_b", "ln2_g", "ln2_b"):
            P[k] = w[k][l][None, :]
        Ps.append(P)

    xa = x[0]
    xb = xa.astype(MXU_DTYPE)
    xbT = xb.T
    saved, Ws = [], []
    for l in range(L):
        W = gathered(l, 0, all_started if l == 0 else xa)
        late = lambda after, l=l: gathered(l, 1, after)
        xa, xb, xbT, sv, W = _layer_fwd(l, xa, xb, xbT, W, Ps[l], tabs, blk, late_weights=late)
        saved.append(sv)
        Ws.append(W)
    dx, loss_part = _loss_head(xa, loss_target[0], bs=bs, name="loss_head")

    layer_grads = [None] * L
    sent = [[None, None] for _ in range(L)]
    pin = None
    for l in reversed(range(L)):
        P = Ps[l] if pin is None else dict(Ps[l], ln2_g=Ps[l]["ln2_g"] + pin[0, 0])

        def send_early(grads, l=l):
            sent[l][1] = scatter(l, 1, grads)
            return sent[l][1]["token"]

        dx, layer_grads[l] = _layer_bwd(l, dx, saved[l], Ws[l], P, tabs, blk, send_early=send_early)
        sent[l][0] = scatter(l, 0, layer_grads[l])
        pin = sent[l][0]["token"]
    grad_x = dx[None]

    me_arr = me.astype(jnp.int32)[None]
    partial = {k: jnp.zeros(w[k].shape, F32) for k in BIG}
    after = dx
    for l in reversed(range(L)):
        for g in (1, 0):
            mine, lands = _exchange_wait(sent[l][g], after, scatter=True, name=f"scatter_wait{l}_{g}")
            for t, k in enumerate(groups[g]):
                own = lax.dynamic_index_in_dim(mine[t], me, 0, keepdims=False)
                partial[k] = _sum_chips_into(partial[k], lands[t], own, me_arr, l, br=_rows(own.shape[0]),
                                             name=f"sum_{k}_l{l}")
            after = partial[groups[g][-1]]
    partial = [partial[k] for k in BIG]
    sibling = _core_exchange(partial, name="swap_partials")
    local = {k: jnp.stack([layer_grads[l][k] for l in range(L)]) for k in SMALL}
    out = {}
    for k, mine, theirs in zip(BIG, partial, sibling):
        shp = w[k].shape
        two_d = lambda a: a.reshape(shp[0] * shp[1], shp[2])
        res = _adamw(two_d(w[k]), two_d(m[k]), two_d(v[k]), two_d(mine), two_d(theirs), br=_rows(shp[0] * shp[1]),
                     name=f"adamw_{k}")
        out[k] = [a.reshape(shp) for a in res]

    shapes = {k: w[k].shape for k in SMALL}
    extra = 8 + (-L * SMALL_ROWS_PER_LAYER) % 8
    block = _pack_small({k: local[k] for k in SMALL}, extra)
    block = block.at[L * SMALL_ROWS_PER_LAYER, 0].set(loss_part[0, 0])
    total = _all_sum_small(block, name="sum_small")
    loss = total[L * SMALL_ROWS_PER_LAYER, 0]
    res = _adamw(_pack_small({k: w[k] for k in SMALL}, extra), _pack_small({k: m[k] for k in SMALL}, extra),
                 _pack_small({k: v[k] for k in SMALL}, extra), total, None, br=total.shape[0], name="adamw_small")
    res = [_unpack_small(t, shapes) for t in res]
    for k in SMALL:
        out[k] = [r[k] for r in res]

    return (loss, grad_x, *[out[k][0] for k in names], *[out[k][1] for k in names],
            *[out[k][2] for k in names], *[out[k][3] for k in names])


def _rows(n):
    for b in (256, 128, 64, 32, 16, 8):
        if n % b == 0:
            return b
    return n
```

```python
import functools

import numpy as np
import jax
import jax.numpy as jnp
from jax import lax
from jax.experimental import pallas as pl
from jax.experimental.pallas import tpu as pltpu

F32 = jnp.float32
MXU_DTYPE = jnp.bfloat16
NEG_INF = -1e30

D_MODEL = 1024
DEPTH = 4
HEAD_DIM = 64
GROUP_WIDTH = 256
D_FF = 2816
MLA_Q_RANK = 256
MLA_KV_RANK = 128
MLA_ROPE = 32
MLA_QK = 96
MLA_PAD = 128
ROPE_THETA = 10000.0
WINDOW = 128
ALPHA = (2.0 * DEPTH) ** 0.25
SWA_SLOPES = tuple(float(2.0 ** (-8.0 * h / 4)) for h in range(1, 5))
ATT_W = 2048
LAT_W = 640
PERM_W = ATT_W + LAT_W
COL_FQ, COL_FK, COL_FV = 0, 256, 512
COL_SQ, COL_SK, COL_SV = 768, 1024, 1280
COL_WQ, COL_WK, COL_WV = 1536, 1792, 1920
Q_COLSCALE = np.ones((1, ATT_W), np.float32)
Q_COLSCALE[:, COL_FQ:COL_FQ + 256] = HEAD_DIM ** -0.5
Q_COLSCALE[:, COL_SQ:COL_SQ + 256] = HEAD_DIM ** -0.5

ADAM_LR, ADAM_B1, ADAM_B2, ADAM_EPS, ADAM_WD, ADAM_STEP = 0.001, 0.9, 0.999, 1e-08, 0.01, 10

VMEM_LIMIT = 56 * 1024 * 1024
NT = (((1,), (1,)), ((), ()))
TN = (((0,), (0,)), ((), ()))
MESH = pl.DeviceIdType.MESH


def _cparams(*sem):
    return pltpu.CompilerParams(dimension_semantics=sem, vmem_limit_bytes=VMEM_LIMIT)


def _dot01(x, m01, dn=None, parts=2):
    acc = None
    rem = x
    for _ in range(parts):
        part = rem.astype(MXU_DTYPE)
        rem = rem - part.astype(F32)
        if dn is None:
            t = jnp.dot(part, m01, preferred_element_type=F32)
        else:
            t = lax.dot_general(part, m01, dn, preferred_element_type=F32)
        acc = t if acc is None else acc + t
    return acc


def _mm(a, b, *, name, ta=False, tb=False, out_dtype=F32, bm=512, bn=512, bk=512, resid=None, alpha=1.0,
        colscale=None):
    M, K = (a.shape[1], a.shape[0]) if ta else a.shape
    N = b.shape[0] if tb else b.shape[1]
    assert (b.shape[1] if tb else b.shape[0]) == K
    assert resid is None or colscale is None
    bm, bn, bk = min(bm, M), min(bn, N), min(bk, K)
    assert M % bm == 0 and N % bn == 0 and K % bk == 0, (name, M, N, K, bm, bn, bk)
    nk = K // bk
    assert nk == 1 or (out_dtype == F32 and colscale is None), name
    dn = (((0 if ta else 1,), (1 if tb else 0,)), ((), ()))

    extra = resid is not None or colscale is not None

    def body(*refs):
        a_ref, b_ref = refs[:2]
        r_ref = refs[2] if extra else None
        o_ref = refs[3] if extra else refs[2]
        k = pl.program_id(2)

        def first():
            r = lax.dot_general(a_ref[...].astype(MXU_DTYPE), b_ref[...].astype(MXU_DTYPE), dn,
                                preferred_element_type=F32)
            if resid is not None:
                r = r + alpha * r_ref[...]
            if colscale is not None:
                r = r * r_ref[...]
            o_ref[...] = r.astype(o_ref.dtype)

        if nk == 1:
            first()
        else:
            pl.when(k == 0)(first)

            @pl.when(k > 0)
            def _():
                o_ref[...] += lax.dot_general(a_ref[...].astype(MXU_DTYPE), b_ref[...].astype(MXU_DTYPE), dn,
                                              preferred_element_type=F32)

    a_spec = pl.BlockSpec((bk, bm), lambda i, j, k: (k, i)) if ta else pl.BlockSpec((bm, bk), lambda i, j, k: (i, k))
    b_spec = pl.BlockSpec((bn, bk), lambda i, j, k: (j, k)) if tb else pl.BlockSpec((bk, bn), lambda i, j, k: (k, j))
    in_specs = [a_spec, b_spec]
    args = [a, b]
    if resid is not None:
        in_specs.append(pl.BlockSpec((bm, bn), lambda i, j, k: (i, j)))
        args.append(resid)
    if colscale is not None:
        in_specs.append(pl.BlockSpec((1, bn), lambda i, j, k: (0, j)))
        args.append(colscale)
    return pl.pallas_call(
        body, name=name, grid=(M // bm, N // bn, nk), in_specs=in_specs,
        out_specs=pl.BlockSpec((bm, bn), lambda i, j, k: (i, j)),
        out_shape=jax.ShapeDtypeStruct((M, N), out_dtype),
        compiler_params=_cparams("parallel", "parallel", "arbitrary"),
    )(*args)


HP = 4
BLK_SOFTMAX = 512
BLK_STICK = 256


def _t(x):
    return x.astype(F32).T.astype(MXU_DTYPE)


def _fill_transposed(dst_ref, src_ref, nb, blk):
    for j in range(nb):
        dst_ref[j] = _t(src_ref[j * blk:(j + 1) * blk, :])


def _smax_fwd_t(q, k, v, *, dk, blk, name):
    S = k.shape[0]
    nb = S // blk
    H = k.shape[1] // dk

    def body(q_ref, k_ref, v_ref, oT_ref, lse_ref, vT_ref):
        i = pl.program_id(1)

        @pl.when(i == 0)
        def _():
            _fill_transposed(vT_ref, v_ref, nb, blk)

        key = lax.broadcasted_iota(jnp.int32, (blk, blk), 0)
        qry = lax.broadcasted_iota(jnp.int32, (blk, blk), 1)
        qs = [_t(q_ref[:, h * dk:(h + 1) * dk]) for h in range(HP)]

        def tile(j, carry, masked):
            r0 = pl.multiple_of(j * blk, blk)
            ss = [jnp.dot(k_ref[pl.ds(r0, blk), h * dk:(h + 1) * dk], qs[h], preferred_element_type=F32)
                  for h in range(HP)]
            stats, pes = [], []
            for h in range(HP):
                m, l, _ = carry[h]
                s = jnp.where(key <= qry, ss[h], NEG_INF) if masked else ss[h]
                mn = jnp.maximum(m, jnp.max(s, axis=0, keepdims=True))
                a = jnp.exp(m - mn)
                pe = jnp.exp(s - mn)
                stats.append((mn, a * l + jnp.sum(pe, axis=0, keepdims=True), a))
                pes.append(pe.astype(MXU_DTYPE))
            pvs = [jnp.dot(vT_ref[j, h * HEAD_DIM:(h + 1) * HEAD_DIM, :], pes[h], preferred_element_type=F32)
                   for h in range(HP)]
            return tuple((stats[h][0], stats[h][1], stats[h][2] * carry[h][2] + pvs[h]) for h in range(HP))

        init = tuple((jnp.full((1, blk), NEG_INF, F32), jnp.zeros((1, blk), F32), jnp.zeros((HEAD_DIM, blk), F32))
                     for _ in range(HP))
        carry = lax.fori_loop(0, i, functools.partial(tile, masked=False), init)
        carry = tile(i, carry, True)
        for h in range(HP):
            m, l, acc = carry[h]
            oT_ref[h * HEAD_DIM:(h + 1) * HEAD_DIM, :] = acc / l
            lse_ref[h, 0] = m + jnp.log(l)

    return pl.pallas_call(
        body, name=name, grid=(H // HP, nb),
        in_specs=[pl.BlockSpec((blk, HP * dk), lambda p, i: (i, p)),
                  pl.BlockSpec((S, HP * dk), lambda p, i: (0, p)),
                  pl.BlockSpec((S, HP * HEAD_DIM), lambda p, i: (0, p))],
        out_specs=[pl.BlockSpec((HP * HEAD_DIM, blk), lambda p, i: (p, i)),
                   pl.BlockSpec((HP, 1, 1, blk), lambda p, i: (p, i, 0, 0))],
        out_shape=[jax.ShapeDtypeStruct((H * HEAD_DIM, S), F32), jax.ShapeDtypeStruct((H, nb, 1, blk), F32)],
        scratch_shapes=[pltpu.VMEM((nb, HP * HEAD_DIM, blk), MXU_DTYPE)],
        compiler_params=_cparams("arbitrary", "arbitrary"),
    )(q, k, v)


def _smax_bwd_t(q, k, v, dmix, dmixT, oT, lse, *, dk, dcb, qscale, blk, name):
    S = k.shape[0]
    nb = S // blk
    H = k.shape[1] // dk
    hd = HP * HEAD_DIM
    dcr = dcb * 128 // hd

    def body(q_ref, k_ref, v_ref, do_ref, doT_ref, oT_ref, lse_ref, dqT_ref, dk_ref, dv_ref, kT_ref):
        i = pl.program_id(1)

        @pl.when(i == 0)
        def _():
            dk_ref[...] = jnp.zeros_like(dk_ref)
            dv_ref[...] = jnp.zeros_like(dv_ref)
            _fill_transposed(kT_ref, k_ref, nb, blk)

        key = lax.broadcasted_iota(jnp.int32, (blk, blk), 0)
        qry = lax.broadcasted_iota(jnp.int32, (blk, blk), 1)
        per_head = []
        for h in range(HP):
            hs = slice(h * HEAD_DIM, (h + 1) * HEAD_DIM)
            doT = doT_ref[hs, :]
            per_head.append(dict(
                qT=_t(q_ref[:, h * dk:(h + 1) * dk]), q=q_ref[:, h * dk:(h + 1) * dk],
                doT=doT.astype(MXU_DTYPE), do=do_ref[:, hs].astype(MXU_DTYPE),
                delta=jnp.sum(doT * oT_ref[hs, :], axis=0, keepdims=True), lse=lse_ref[h, 0]))

        def tile(j, dqs, masked):
            r0 = pl.multiple_of(j * blk, blk)
            rows = pl.ds(r0, blk)
            ksl = [slice(h * dk, (h + 1) * dk) for h in range(HP)]
            hsl = [slice(h * HEAD_DIM, (h + 1) * HEAD_DIM) for h in range(HP)]
            ss = [jnp.dot(k_ref[rows, ksl[h]], per_head[h]["qT"], preferred_element_type=F32) for h in range(HP)]
            dps = [jnp.dot(v_ref[rows, hsl[h]], per_head[h]["doT"], preferred_element_type=F32) for h in range(HP)]
            prs, dss = [], []
            for h in range(HP):
                c = per_head[h]
                s = jnp.where(key <= qry, ss[h], NEG_INF) if masked else ss[h]
                pr = jnp.exp(s - c["lse"])
                dss.append((pr * (dps[h] - c["delta"])).astype(MXU_DTYPE))
                prs.append(pr.astype(MXU_DTYPE))
            for h in range(HP):
                dv_ref[rows, hsl[h]] += jnp.dot(prs[h], per_head[h]["do"], preferred_element_type=F32)
            for h in range(HP):
                dk_ref[rows, ksl[h]] += jnp.dot(dss[h], per_head[h]["q"], preferred_element_type=F32)
            return tuple(dqs[h] + jnp.dot(kT_ref[j, ksl[h], :], dss[h], preferred_element_type=F32) for h in range(HP))

        dqs = lax.fori_loop(0, i, functools.partial(tile, masked=False),
                            tuple(jnp.zeros((dk, blk), F32) for _ in range(HP)))
        dqs = tile(i, dqs, True)
        for h in range(HP):
            dqT_ref[h * dk:(h + 1) * dk, :] = dqs[h] * qscale

    return pl.pallas_call(
        body, name=name, grid=(H // HP, nb),
        in_specs=[pl.BlockSpec((blk, HP * dk), lambda p, i: (i, p)),
                  pl.BlockSpec((S, HP * dk), lambda p, i: (0, p)),
                  pl.BlockSpec((S, hd), lambda p, i: (0, p)),
                  pl.BlockSpec((blk, hd), lambda p, i: (i, dcr + p)),
                  pl.BlockSpec((hd, blk), lambda p, i: (dcr + p, i)),
                  pl.BlockSpec((hd, blk), lambda p, i: (p, i)),
                  pl.BlockSpec((HP, 1, 1, blk), lambda p, i: (p, i, 0, 0))],
        out_specs=[pl.BlockSpec((HP * dk, blk), lambda p, i: (p, i)),
                   pl.BlockSpec((S, HP * dk), lambda p, i: (0, p)),
                   pl.BlockSpec((S, hd), lambda p, i: (0, p))],
        out_shape=[jax.ShapeDtypeStruct((H * dk, S), F32), jax.ShapeDtypeStruct((S, H * dk), F32),
                   jax.ShapeDtypeStruct((S, H * HEAD_DIM), F32)],
        scratch_shapes=[pltpu.VMEM((nb, HP * dk, blk), MXU_DTYPE)],
        compiler_params=_cparams("arbitrary", "arbitrary"),
    )(q, k, v, dmix, dmixT, oT, lse)


def _log1m_beta(z):
    return -(jnp.maximum(z, 0.0) + jnp.log(1.0 + jnp.exp(-jnp.abs(z))))


def _dot01_left(m01, x, parts=2):
    acc = None
    rem = x
    for _ in range(parts):
        part = rem.astype(MXU_DTYPE)
        rem = rem - part.astype(F32)
        t = jnp.dot(m01, part, preferred_element_type=F32)
        acc = t if acc is None else acc + t
    return acc


def _sb_fwd_t(h_att, *, blk, name):
    S = h_att.shape[0]
    nb = S // blk
    hd = HP * HEAD_DIM
    qcb, kcb, vcb = COL_SQ // hd, COL_SK // hd, COL_SV // hd

    def body(q_ref, k_ref, v_ref, oT_ref, lt_ref, vT_ref):
        i = pl.program_id(1)

        @pl.when(i == 0)
        def _():
            _fill_transposed(vT_ref, v_ref, nb, blk)

        key = lax.broadcasted_iota(jnp.int32, (blk, blk), 0)
        qry = lax.broadcasted_iota(jnp.int32, (blk, blk), 1)
        strict = key < qry
        later = (qry > key).astype(MXU_DTYPE)
        qs = [_t(q_ref[:, h * HEAD_DIM:(h + 1) * HEAD_DIM]) for h in range(HP)]

        def tile(j, carry, mask):
            r0 = pl.multiple_of(j * blk, blk)
            hsl = [slice(h * HEAD_DIM, (h + 1) * HEAD_DIM) for h in range(HP)]
            zs = [jnp.dot(k_ref[pl.ds(r0, blk), hsl[h]], qs[h], preferred_element_type=F32) for h in range(HP)]
            lbs = []
            for h in range(HP):
                lb = _log1m_beta(zs[h])
                lbs.append(lb if mask is None else jnp.where(mask, lb, 0.0))
            sums = [_dot01_left(later, lbs[h]) for h in range(HP)]
            probs = []
            for h in range(HP):
                lt_ref[h, 0, j] = carry[h][0]
                a = jnp.exp(zs[h] + lbs[h] + sums[h] + carry[h][0])
                probs.append((a if mask is None else jnp.where(mask, a, 0.0)).astype(MXU_DTYPE))
            pvs = [jnp.dot(vT_ref[j, hsl[h], :], probs[h], preferred_element_type=F32) for h in range(HP)]
            return tuple((carry[h][0] + jnp.sum(lbs[h], axis=0, keepdims=True), carry[h][1] + pvs[h]) for h in range(HP))

        init = tuple((jnp.zeros((1, blk), F32), jnp.zeros((HEAD_DIM, blk), F32)) for _ in range(HP))
        carry = tile(i, init, strict)
        carry = lax.fori_loop(0, i, lambda jj, c: tile(i - 1 - jj, c, None), carry)
        for h in range(HP):
            oT_ref[h * HEAD_DIM:(h + 1) * HEAD_DIM, :] = carry[h][1]

    return pl.pallas_call(
        body, name=name, grid=(4 // HP, nb),
        in_specs=[pl.BlockSpec((blk, hd), lambda p, i: (i, qcb + p)),
                  pl.BlockSpec((S, hd), lambda p, i: (0, kcb + p)),
                  pl.BlockSpec((S, hd), lambda p, i: (0, vcb + p))],
        out_specs=[pl.BlockSpec((hd, blk), lambda p, i: (p, i)),
                   pl.BlockSpec((HP, 1, nb, 1, blk), lambda p, i: (p, i, 0, 0, 0))],
        out_shape=[jax.ShapeDtypeStruct((GROUP_WIDTH, S), F32), jax.ShapeDtypeStruct((4, nb, nb, 1, blk), F32)],
        scratch_shapes=[pltpu.VMEM((nb, hd, blk), MXU_DTYPE)],
        compiler_params=_cparams("arbitrary", "arbitrary"),
    )(h_att, h_att, h_att)


def _sb_bwd_t(h_att, dmix, dmixT, later_sums, *, dcb, qscale, blk, name):
    S = h_att.shape[0]
    nb = S // blk
    hd = HP * HEAD_DIM
    qcb, kcb, vcb = COL_SQ // hd, COL_SK // hd, COL_SV // hd
    dcr = dcb * 128 // hd

    def body(q_ref, k_ref, v_ref, do_ref, doT_ref, lt_ref, dqT_ref, dk_ref, dv_ref, kT_ref):
        i = pl.program_id(1)

        @pl.when(i == 0)
        def _():
            dk_ref[...] = jnp.zeros_like(dk_ref)
            dv_ref[...] = jnp.zeros_like(dv_ref)
            _fill_transposed(kT_ref, k_ref, nb, blk)

        key = lax.broadcasted_iota(jnp.int32, (blk, blk), 0)
        qry = lax.broadcasted_iota(jnp.int32, (blk, blk), 1)
        strict = key < qry
        later = (qry > key).astype(MXU_DTYPE)
        before = (qry < key).astype(MXU_DTYPE)
        per_head = []
        for h in range(HP):
            hs = slice(h * HEAD_DIM, (h + 1) * HEAD_DIM)
            per_head.append(dict(qT=_t(q_ref[:, hs]), q=q_ref[:, hs], doT=doT_ref[hs, :].astype(MXU_DTYPE),
                                 do=do_ref[:, hs].astype(MXU_DTYPE)))

        def tile(j, carry, mask):
            r0 = pl.multiple_of(j * blk, blk)
            rows = pl.ds(r0, blk)
            hsl = [slice(h * HEAD_DIM, (h + 1) * HEAD_DIM) for h in range(HP)]
            zs = [jnp.dot(k_ref[rows, hsl[h]], per_head[h]["qT"], preferred_element_type=F32) for h in range(HP)]
            das = [jnp.dot(v_ref[rows, hsl[h]], per_head[h]["doT"], preferred_element_type=F32) for h in range(HP)]
            lbs = []
            for h in range(HP):
                lb = _log1m_beta(zs[h])
                lbs.append(lb if mask is None else jnp.where(mask, lb, 0.0))
            sums = [_dot01_left(later, lbs[h]) for h in range(HP)]
            probs, gs = [], []
            for h in range(HP):
                a = jnp.exp(zs[h] + lbs[h] + sums[h] + lt_ref[h, 0, j])
                a = a if mask is None else jnp.where(mask, a, 0.0)
                gs.append(das[h] * a)
                probs.append(a.astype(MXU_DTYPE))
            for h in range(HP):
                dv_ref[rows, hsl[h]] += jnp.dot(probs[h], per_head[h]["do"], preferred_element_type=F32)
            es = [_dot01_left(before, gs[h]) for h in range(HP)]
            dzs = []
            for h in range(HP):
                dz = gs[h] * jnp.exp(lbs[h]) - (carry[h][0] + es[h]) * jnp.exp(zs[h] + lbs[h])
                dzs.append((dz if mask is None else jnp.where(mask, dz, 0.0)).astype(MXU_DTYPE))
            for h in range(HP):
                dk_ref[rows, hsl[h]] += jnp.dot(dzs[h], per_head[h]["q"], preferred_element_type=F32)
            return tuple((carry[h][0] + jnp.sum(gs[h], axis=0, keepdims=True),
                          carry[h][1] + jnp.dot(kT_ref[j, hsl[h], :], dzs[h], preferred_element_type=F32))
                         for h in range(HP))

        init = tuple((jnp.zeros((1, blk), F32), jnp.zeros((HEAD_DIM, blk), F32)) for _ in range(HP))
        carry = lax.fori_loop(0, i, lambda j, c: tile(j, c, None), init)
        carry = tile(i, carry, strict)
        for h in range(HP):
            dqT_ref[h * HEAD_DIM:(h + 1) * HEAD_DIM, :] = carry[h][1] * qscale

    return pl.pallas_call(
        body, name=name, grid=(4 // HP, nb),
        in_specs=[pl.BlockSpec((blk, hd), lambda p, i: (i, qcb + p)),
                  pl.BlockSpec((S, hd), lambda p, i: (0, kcb + p)),
                  pl.BlockSpec((S, hd), lambda p, i: (0, vcb + p)),
                  pl.BlockSpec((blk, hd), lambda p, i: (i, dcr + p)),
                  pl.BlockSpec((hd, blk), lambda p, i: (dcr + p, i)),
                  pl.BlockSpec((HP, 1, nb, 1, blk), lambda p, i: (p, i, 0, 0, 0))],
        out_specs=[pl.BlockSpec((hd, blk), lambda p, i: (p, i)),
                   pl.BlockSpec((S, hd), lambda p, i: (0, p)),
                   pl.BlockSpec((S, hd), lambda p, i: (0, p))],
        out_shape=[jax.ShapeDtypeStruct((GROUP_WIDTH, S), F32), jax.ShapeDtypeStruct((S, GROUP_WIDTH), F32),
                   jax.ShapeDtypeStruct((S, GROUP_WIDTH), F32)],
        scratch_shapes=[pltpu.VMEM((nb, hd, blk), MXU_DTYPE)],
        compiler_params=_cparams("arbitrary", "arbitrary"),
    )(h_att, h_att, h_att, dmix, dmixT, later_sums)


SWA_SUB = 4


def _swa_sub_blocks(S):
    return min(SWA_SUB, S // WINDOW)


def _swa_tiles(n, sub):
    tiles = []
    for b in range(sub):
        start = pl.multiple_of(jnp.maximum(n * sub + b - 1, 0) * WINDOW, WINDOW)
        tiles += [(b, h, slice(b * WINDOW, (b + 1) * WINDOW), start) for h in range(4)]
    return tiles


def _swa_scores(q_ref, k_ref, n, sub, tile):
    b, h, qrows, start = tile
    g = h // 2
    kb = k_ref[pl.ds(start, 2 * WINDOW), g * HEAD_DIM:(g + 1) * HEAD_DIM]
    s = lax.dot_general(q_ref[qrows, h * HEAD_DIM:(h + 1) * HEAD_DIM], kb, NT,
                        preferred_element_type=F32) * (HEAD_DIM ** -0.5)
    dist = ((n * sub + b) * WINDOW + lax.broadcasted_iota(jnp.int32, (WINDOW, 2 * WINDOW), 0)
            - start - lax.broadcasted_iota(jnp.int32, (WINDOW, 2 * WINDOW), 1))
    s = s - SWA_SLOPES[h] * dist.astype(F32)
    valid = (dist >= 0) & (dist < WINDOW)
    return jnp.where(valid, s, NEG_INF), kb


def _swa_softmax(s, sink):
    m = jnp.maximum(jnp.max(s, axis=1, keepdims=True), sink)
    e = jnp.exp(s - m)
    sink_e = jnp.exp(sink - m)
    den = jnp.sum(e, axis=1, keepdims=True) + sink_e
    return e / den, sink_e / den


def _swa_fwd(h_att, sinks, *, name):
    S = h_att.shape[0]
    sub = _swa_sub_blocks(S)
    rows = sub * WINDOW
    qcb, kcb, vcb = COL_WQ // 256, COL_WK // 128, COL_WV // 128

    def body(sink_ref, q_ref, k_ref, v_ref, o_ref):
        n = pl.program_id(0)
        tiles = _swa_tiles(n, sub)
        scores = [_swa_scores(q_ref, k_ref, n, sub, t)[0] for t in tiles]
        probs = [_swa_softmax(s, sink_ref[h])[0].astype(MXU_DTYPE) for (b, h, qrows, start), s in zip(tiles, scores)]
        for (b, h, qrows, start), p in zip(tiles, probs):
            vb = v_ref[pl.ds(start, 2 * WINDOW), (h // 2) * HEAD_DIM:(h // 2 + 1) * HEAD_DIM]
            o_ref[qrows, h * HEAD_DIM:(h + 1) * HEAD_DIM] = jnp.dot(p, vb, preferred_element_type=F32)

    return pl.pallas_call(
        body, name=name, grid=(S // rows,),
        in_specs=[pl.BlockSpec(memory_space=pltpu.SMEM),
                  pl.BlockSpec((rows, 256), lambda n: (n, qcb)),
                  pl.BlockSpec((S, 128), lambda n: (0, kcb)),
                  pl.BlockSpec((S, 128), lambda n: (0, vcb))],
        out_specs=pl.BlockSpec((rows, 256), lambda n: (n, 0)),
        out_shape=jax.ShapeDtypeStruct((S, GROUP_WIDTH), F32),
        compiler_params=_cparams("arbitrary"),
    )(sinks, h_att, h_att, h_att)


def _swa_bwd(h_att, sinks, dmix, o_arr, *, dcb, name):
    S = h_att.shape[0]
    sub = _swa_sub_blocks(S)
    rows = sub * WINDOW
    qcb, kcb, vcb = COL_WQ // 256, COL_WK // 128, COL_WV // 128

    def body(sink_ref, q_ref, k_ref, v_ref, do_ref, o_ref, dq_ref, dk_ref, dv_ref, dsink_ref):
        n = pl.program_id(0)

        @pl.when(n == 0)
        def _():
            dk_ref[...] = jnp.zeros_like(dk_ref)
            dv_ref[...] = jnp.zeros_like(dv_ref)
            dsink_ref[...] = jnp.zeros_like(dsink_ref)

        tiles = _swa_tiles(n, sub)
        hsl = [slice(h * HEAD_DIM, (h + 1) * HEAD_DIM) for h in range(4)]
        gsl = [slice(g * HEAD_DIM, (g + 1) * HEAD_DIM) for g in range(2)]
        scale = HEAD_DIM ** -0.5
        sk = [_swa_scores(q_ref, k_ref, n, sub, t) for t in tiles]
        dobs = [do_ref[qrows, hsl[h]].astype(MXU_DTYPE) for b, h, qrows, start in tiles]
        dps = [lax.dot_general(dob, v_ref[pl.ds(start, 2 * WINDOW), gsl[h // 2]], NT, preferred_element_type=F32)
               for (b, h, qrows, start), dob in zip(tiles, dobs)]
        prs, dss = [], []
        for t, (b, h, qrows, start) in enumerate(tiles):
            pr, sink_share = _swa_softmax(sk[t][0], sink_ref[h])
            delta = jnp.sum(do_ref[qrows, hsl[h]] * o_ref[qrows, hsl[h]], axis=1, keepdims=True)
            dss.append((pr * (dps[t] - delta)).astype(MXU_DTYPE))
            prs.append(pr.astype(MXU_DTYPE))
            dsink_ref[h:h + 1, :] += jnp.zeros((1, 128), F32) - jnp.sum(sink_share * delta)
        for t, (b, h, qrows, start) in enumerate(tiles):
            dq_ref[qrows, hsl[h]] = jnp.dot(dss[t], sk[t][1], preferred_element_type=F32) * scale
        for b in range(sub):
            for g in range(2):
                t0, t1 = 4 * b + 2 * g, 4 * b + 2 * g + 1
                qrows, krows = tiles[t0][2], pl.ds(tiles[t0][3], 2 * WINDOW)
                dk_ref[krows, gsl[g]] += (
                    lax.dot_general(dss[t0], q_ref[qrows, hsl[2 * g]], TN, preferred_element_type=F32)
                    + lax.dot_general(dss[t1], q_ref[qrows, hsl[2 * g + 1]], TN, preferred_element_type=F32)) * scale
                dv_ref[krows, gsl[g]] += (lax.dot_general(prs[t0], dobs[t0], TN, preferred_element_type=F32)
                                          + lax.dot_general(prs[t1], dobs[t1], TN, preferred_element_type=F32))

    return pl.pallas_call(
        body, name=name, grid=(S // rows,),
        in_specs=[pl.BlockSpec(memory_space=pltpu.SMEM),
                  pl.BlockSpec((rows, 256), lambda n: (n, qcb)),
                  pl.BlockSpec((S, 128), lambda n: (0, kcb)),
                  pl.BlockSpec((S, 128), lambda n: (0, vcb)),
                  pl.BlockSpec((rows, 256), lambda n: (n, dcb)),
                  pl.BlockSpec((rows, 256), lambda n: (n, 0))],
        out_specs=[pl.BlockSpec((rows, 256), lambda n: (n, 0)),
                   pl.BlockSpec((S, 128), lambda n: (0, 0)),
                   pl.BlockSpec((S, 128), lambda n: (0, 0)),
                   pl.BlockSpec((4, 128), lambda n: (0, 0))],
        out_shape=[jax.ShapeDtypeStruct((S, GROUP_WIDTH), F32), jax.ShapeDtypeStruct((S, 128), F32),
                   jax.ShapeDtypeStruct((S, 128), F32), jax.ShapeDtypeStruct((4, 128), F32)],
        compiler_params=_cparams("arbitrary"),
    )(sinks, h_att, h_att, h_att, dmix, o_arr)


def _tri(n, incl, upper):
    r = lax.broadcasted_iota(jnp.int32, (n, n), 0)
    c = lax.broadcasted_iota(jnp.int32, (n, n), 1)
    if upper:
        m = (r <= c) if incl else (r < c)
    else:
        m = (r >= c) if incl else (r > c)
    return m.astype(MXU_DTYPE)


def _fox_gate_fwd(fg, b_f, *, name):
    _, R, _ = fg.shape

    def body(b_ref, fg_ref, pos_ref, neg_ref):
        up_incl = _tri(128, True, True)
        ones = jnp.ones((128, 128), MXU_DTYPE)
        for h in range(4):
            z = fg_ref[h] + b_ref[h]
            logf = jnp.minimum(z, 0.0) - jnp.log(1.0 + jnp.exp(-jnp.abs(z)))
            within = _dot01(logf, up_incl, parts=3)
            totals = _dot01(logf, ones, parts=3)
            rem = within + _rows_other(totals, R, after=False)
            for part in range(3):
                piece = rem.astype(MXU_DTYPE)
                rem = rem - piece.astype(F32)
                pos_ref[h, part] = piece
                neg_ref[h, part] = -piece

    shape = (4, 3) + fg.shape[1:]
    return pl.pallas_call(
        body, name=name,
        in_specs=[pl.BlockSpec(memory_space=pltpu.SMEM), pl.BlockSpec(memory_space=pltpu.VMEM)],
        out_specs=[pl.BlockSpec(memory_space=pltpu.VMEM)] * 2,
        out_shape=[jax.ShapeDtypeStruct(shape, MXU_DTYPE)] * 2,
    )(b_f, fg)


def _rows_other(totals, n, after):
    r = lax.broadcasted_iota(jnp.int32, (n, n), 0)
    c = lax.broadcasted_iota(jnp.int32, (n, n), 1)
    m = ((c > r) if after else (c < r)).astype(MXU_DTYPE)
    acc = None
    rem = totals
    for _ in range(3):
        part = rem.astype(MXU_DTYPE)
        rem = rem - part.astype(F32)
        t = jnp.dot(m, part, preferred_element_type=F32)
        acc = t if acc is None else acc + t
    return acc


def _fox_gate_bwd(fg, b_f, dcum_k, dcum_q, *, q_unscale, name):
    _, R, _ = fg.shape

    def body(b_ref, fg_ref, dck_ref, dcq_ref, dfg_ref, db_ref):
        low_incl = _tri(128, True, False)
        ones = jnp.ones((128, 128), MXU_DTYPE)
        for h in range(4):
            dc = dcq_ref[h] * q_unscale - dck_ref[h]
            dlogf = _dot01(dc, low_incl, parts=3) + _rows_other(_dot01(dc, ones, parts=3), R, after=True)
            z = fg_ref[h] + b_ref[h]
            dz = dlogf * jnp.exp(jnp.minimum(-z, 0.0) - jnp.log(1.0 + jnp.exp(-jnp.abs(z))))
            dfg_ref[h] = dz
            db_ref[h:h + 1, :] = jnp.zeros((1, 128), F32) + jnp.sum(dz)

    return pl.pallas_call(
        body, name=name,
        in_specs=[pl.BlockSpec(memory_space=pltpu.SMEM)] + [pl.BlockSpec(memory_space=pltpu.VMEM)] * 3,
        out_specs=[pl.BlockSpec(memory_space=pltpu.VMEM), pl.BlockSpec(memory_space=pltpu.VMEM)],
        out_shape=[jax.ShapeDtypeStruct(fg.shape, F32), jax.ShapeDtypeStruct((4, 128), F32)],
    )(b_f, fg, dcum_k, dcum_q)


def _rope_rot(transpose):
    r = lax.broadcasted_iota(jnp.int32, (MLA_PAD, MLA_PAD), 0)
    c = lax.broadcasted_iota(jnp.int32, (MLA_PAD, MLA_PAD), 1)
    if transpose:
        r, c = c, r
    half = MLA_ROPE // 2
    lo, mid, hi = HEAD_DIM, HEAD_DIM + half, HEAD_DIM + MLA_ROPE
    minus = (c >= lo) & (c < mid) & (r == c + half)
    plus = (c >= mid) & (c < hi) & (r == c - half)
    return jnp.where(plus, 1.0, jnp.where(minus, -1.0, 0.0)).astype(MXU_DTYPE)


def _rope_lanes():
    lane = lax.broadcasted_iota(jnp.int32, (1, MLA_PAD), 1)
    return ((lane >= HEAD_DIM) & (lane < HEAD_DIM + MLA_ROPE)).astype(F32)


def _rms(x, g, eps=1e-6):
    r = lax.rsqrt(jnp.mean(x * x, axis=-1, keepdims=True) + eps)
    return x * r * g, r


def _rms_bwd(dy, x, r, g):
    xh = x * r
    dxh = dy * g
    dx = r * (dxh - xh * jnp.mean(dxh * xh, axis=-1, keepdims=True))
    return dx, dy * xh


def _mla_prep_fwd(lat, g_q, g_kv, wuq, wuk, wuv, cosm, sinm, *, bs, name):
    S = lat.shape[0]

    def body(lat_ref, gq_ref, gkv_ref, wuq_ref, wuk_ref, wuv_ref, cos_ref, sin_ref, q_ref, k_ref, v_ref):
        rot = _rope_rot(False)
        cosm_, sinm_ = cos_ref[...], sin_ref[...]
        nq, _ = _rms(lat_ref[:, 0:MLA_Q_RANK], gq_ref[...])
        nkv, _ = _rms(lat_ref[:, MLA_Q_RANK:MLA_Q_RANK + MLA_KV_RANK], gkv_ref[...])
        qlat = jnp.dot(nq.astype(MXU_DTYPE), wuq_ref[...], preferred_element_type=F32)
        klat = jnp.dot(nkv.astype(MXU_DTYPE), wuk_ref[...], preferred_element_type=F32)
        v_ref[...] = jnp.dot(nkv.astype(MXU_DTYPE), wuv_ref[...], preferred_element_type=F32).astype(v_ref.dtype)
        krb = lat_ref[:, 384:512]
        kr = krb * (cosm_ * _rope_lanes()) + _dot01(krb, rot, parts=3) * sinm_
        for h in range(4):
            sl = slice(h * MLA_PAD, (h + 1) * MLA_PAD)
            qh = qlat[:, sl]
            q_ref[:, sl] = ((qh * cosm_ + _dot01(qh, rot, parts=3) * sinm_) * (MLA_QK ** -0.5)).astype(q_ref.dtype)
            k_ref[:, sl] = (klat[:, sl] + kr).astype(k_ref.dtype)

    full = lambda a: pl.BlockSpec(a.shape, lambda i: (0,) * a.ndim)
    return pl.pallas_call(
        body, name=name, grid=(S // bs,),
        in_specs=[pl.BlockSpec((bs, LAT_W), lambda i: (i, 0)), full(g_q), full(g_kv), full(wuq), full(wuk), full(wuv),
                  pl.BlockSpec((bs, MLA_PAD), lambda i: (i, 0)), pl.BlockSpec((bs, MLA_PAD), lambda i: (i, 0))],
        out_specs=[pl.BlockSpec((bs, 512), lambda i: (i, 0)), pl.BlockSpec((bs, 512), lambda i: (i, 0)),
                   pl.BlockSpec((bs, 256), lambda i: (i, 0))],
        out_shape=[jax.ShapeDtypeStruct((S, 512), MXU_DTYPE), jax.ShapeDtypeStruct((S, 512), MXU_DTYPE),
                   jax.ShapeDtypeStruct((S, 256), MXU_DTYPE)],
        compiler_params=_cparams("parallel"),
    )(lat, g_q, g_kv, wuq, wuk, wuv, cosm, sinm)


def _mla_prep_bwd(lat, g_q, g_kv, wuq, wuk, wuv, cosm, sinm, dq, dk, dv, *, bs, name):
    S = lat.shape[0]

    def body(lat_ref, gq_ref, gkv_ref, wuq_ref, wuk_ref, wuv_ref, cos_ref, sin_ref, dq_ref, dk_ref, dv_ref,
             dlat_ref, dwuq_ref, dwuk_ref, dwuv_ref, dgq_ref, dgkv_ref):
        @pl.when(pl.program_id(0) == 0)
        def _():
            for r in (dwuq_ref, dwuk_ref, dwuv_ref, dgq_ref, dgkv_ref):
                r[...] = jnp.zeros_like(r)

        rot_t = _rope_rot(True)
        cosm_, sinm_ = cos_ref[...], sin_ref[...]
        cq = lat_ref[:, 0:MLA_Q_RANK]
        ckv = lat_ref[:, MLA_Q_RANK:MLA_Q_RANK + MLA_KV_RANK]
        nq, rq = _rms(cq, gq_ref[...])
        nkv, rkv = _rms(ckv, gkv_ref[...])
        nqb, nkvb = nq.astype(MXU_DTYPE), nkv.astype(MXU_DTYPE)

        dqlat = []
        dkr = jnp.zeros((bs, MLA_PAD), F32)
        for h in range(4):
            sl = slice(h * MLA_PAD, (h + 1) * MLA_PAD)
            dqh = dq_ref[sl, :].T
            dqlat.append(dqh * cosm_ + _dot01(dqh * sinm_, rot_t, parts=3))
            dkr = dkr + dk_ref[:, sl]
        dqlat = jnp.concatenate(dqlat, axis=1).astype(MXU_DTYPE)
        dkb = dk_ref[...].astype(MXU_DTYPE)
        dvb = dv_ref[...].astype(MXU_DTYPE)

        dnq = lax.dot_general(dqlat, wuq_ref[...], NT, preferred_element_type=F32)
        dnkv = (lax.dot_general(dkb, wuk_ref[...], NT, preferred_element_type=F32)
                + lax.dot_general(dvb, wuv_ref[...], NT, preferred_element_type=F32))
        dwuq_ref[...] += lax.dot_general(nqb, dqlat, TN, preferred_element_type=F32)
        dwuk_ref[...] += lax.dot_general(nkvb, dkb, TN, preferred_element_type=F32)
        dwuv_ref[...] += lax.dot_general(nkvb, dvb, TN, preferred_element_type=F32)
        dcq, tq = _rms_bwd(dnq, cq, rq, gq_ref[...])
        dckv, tkv = _rms_bwd(dnkv, ckv, rkv, gkv_ref[...])
        dgq_ref[...] += jnp.sum(tq, axis=0, keepdims=True)
        dgkv_ref[...] += jnp.sum(tkv, axis=0, keepdims=True)
        dlat_ref[:, 0:MLA_Q_RANK] = dcq.astype(dlat_ref.dtype)
        dlat_ref[:, MLA_Q_RANK:MLA_Q_RANK + MLA_KV_RANK] = dckv.astype(dlat_ref.dtype)
        dkrb = dkr * (cosm_ * _rope_lanes()) + _dot01(dkr * sinm_, rot_t, parts=3)
        dlat_ref[:, 384:512] = dkrb.astype(dlat_ref.dtype)

    full = lambda a: pl.BlockSpec(a.shape, lambda i: (0,) * a.ndim)
    row = lambda w: pl.BlockSpec((bs, w), lambda i: (i, 0))
    acc = lambda *shape: pl.BlockSpec(shape, lambda i: (0,) * len(shape))
    return pl.pallas_call(
        body, name=name, grid=(S // bs,),
        in_specs=[row(LAT_W), full(g_q), full(g_kv), full(wuq), full(wuk), full(wuv), row(MLA_PAD), row(MLA_PAD),
                  pl.BlockSpec((512, bs), lambda i: (0, i)), row(512), row(256)],
        out_specs=[row(512), acc(256, 512), acc(128, 512), acc(128, 256), acc(1, 256), acc(1, 128)],
        out_shape=[jax.ShapeDtypeStruct((S, 512), MXU_DTYPE), jax.ShapeDtypeStruct((256, 512), F32),
                   jax.ShapeDtypeStruct((128, 512), F32), jax.ShapeDtypeStruct((128, 256), F32),
                   jax.ShapeDtypeStruct((1, 256), F32), jax.ShapeDtypeStruct((1, 128), F32)],
        compiler_params=_cparams("arbitrary"),
    )(lat, g_q, g_kv, wuq, wuk, wuv, cosm, sinm, dq, dk, dv)


def _row_spec(bs, w):
    return pl.BlockSpec((bs, w), lambda i: (i, 0))


def _vec_spec(w):
    return pl.BlockSpec((1, w), lambda i: (0, 0))


def _mix_specs(bs):
    return [pl.BlockSpec((GROUP_WIDTH, bs), lambda i: (0, i))] * 3 + [_row_spec(bs, GROUP_WIDTH)]


def _mix_groups(a_ref, b_ref, c_ref, d_ref):
    return [a_ref[...].T, b_ref[...].T, c_ref[...].T, d_ref[...]]


def _gnorm_fwd(outs, g, *, bs, name):
    S = outs[3].shape[0]

    def body(a_ref, b_ref, c_ref, d_ref, g_ref, o_ref, oT_ref):
        for k, x in enumerate(_mix_groups(a_ref, b_ref, c_ref, d_ref)):
            sl = slice(k * GROUP_WIDTH, (k + 1) * GROUP_WIDTH)
            y, _ = _rms(x, g_ref[:, sl])
            o_ref[:, sl] = y.astype(o_ref.dtype)
            oT_ref[sl, :] = y.T.astype(oT_ref.dtype)

    return pl.pallas_call(
        body, name=name, grid=(S // bs,),
        in_specs=_mix_specs(bs) + [_vec_spec(D_MODEL)],
        out_specs=[_row_spec(bs, D_MODEL), pl.BlockSpec((D_MODEL, bs), lambda i: (0, i))],
        out_shape=[jax.ShapeDtypeStruct((S, D_MODEL), MXU_DTYPE), jax.ShapeDtypeStruct((D_MODEL, S), MXU_DTYPE)],
        compiler_params=_cparams("parallel"),
    )(*outs, g)


def _gnorm_bwd(dgn, outs, g, *, bs, name):
    S = dgn.shape[0]

    def body(dgn_ref, a_ref, b_ref, c_ref, d_ref, g_ref, dmix_ref, dmixT_ref, dg_ref):
        @pl.when(pl.program_id(0) == 0)
        def _():
            dg_ref[...] = jnp.zeros_like(dg_ref)

        for k, x in enumerate(_mix_groups(a_ref, b_ref, c_ref, d_ref)):
            sl = slice(k * GROUP_WIDTH, (k + 1) * GROUP_WIDTH)
            _, r = _rms(x, g_ref[:, sl])
            dx, t = _rms_bwd(dgn_ref[:, sl], x, r, g_ref[:, sl])
            dmix_ref[:, sl] = dx
            dmixT_ref[sl, :] = dx.T
            dg_ref[:, sl] += jnp.sum(t, axis=0, keepdims=True)

    return pl.pallas_call(
        body, name=name, grid=(S // bs,),
        in_specs=[_row_spec(bs, D_MODEL)] + _mix_specs(bs) + [_vec_spec(D_MODEL)],
        out_specs=[_row_spec(bs, D_MODEL), pl.BlockSpec((D_MODEL, bs), lambda i: (0, i)), _vec_spec(D_MODEL)],
        out_shape=[jax.ShapeDtypeStruct((S, D_MODEL), F32), jax.ShapeDtypeStruct((D_MODEL, S), F32),
                   jax.ShapeDtypeStruct((1, D_MODEL), F32)],
        compiler_params=_cparams("arbitrary"),
    )(dgn, *outs, g)


def _ln_fwd(u, g, b, *, bs, name):
    S = u.shape[0]

    def body(u_ref, g_ref, b_ref, y_ref, yb_ref, ybT_ref, xh_ref, rs_ref):
        x = u_ref[...]
        mu = jnp.mean(x, axis=-1, keepdims=True)
        xc = x - mu
        rs = lax.rsqrt(jnp.mean(xc * xc, axis=-1, keepdims=True) + 1e-5)
        xh = xc * rs
        y = xh * g_ref[...] + b_ref[...]
        y_ref[...] = y
        yb_ref[...] = y.astype(yb_ref.dtype)
        ybT_ref[...] = y.T.astype(ybT_ref.dtype)
        xh_ref[...] = xh
        rs_ref[...] = rs

    return pl.pallas_call(
        body, name=name, grid=(S // bs,),
        in_specs=[_row_spec(bs, D_MODEL), _vec_spec(D_MODEL), _vec_spec(D_MODEL)],
        out_specs=[_row_spec(bs, D_MODEL), _row_spec(bs, D_MODEL), pl.BlockSpec((D_MODEL, bs), lambda i: (0, i)),
                   _row_spec(bs, D_MODEL), _row_spec(bs, 1)],
        out_shape=[jax.ShapeDtypeStruct((S, D_MODEL), F32), jax.ShapeDtypeStruct((S, D_MODEL), MXU_DTYPE),
                   jax.ShapeDtypeStruct((D_MODEL, S), MXU_DTYPE), jax.ShapeDtypeStruct((S, D_MODEL), F32),
                   jax.ShapeDtypeStruct((S, 1), F32)],
        compiler_params=_cparams("parallel"),
    )(u, g, b)


def _ln_bwd(dy, xh, rs, g, *, bs, name):
    S = dy.shape[0]

    def body(dy_ref, xh_ref, rs_ref, g_ref, du_ref, dub_ref, dg_ref, db_ref):
        @pl.when(pl.program_id(0) == 0)
        def _():
            dg_ref[...] = jnp.zeros_like(dg_ref)
            db_ref[...] = jnp.zeros_like(db_ref)

        dy_, xh_ = dy_ref[...], xh_ref[...]
        dxh = dy_ * g_ref[...]
        du = rs_ref[...] * (dxh - jnp.mean(dxh, axis=-1, keepdims=True)
                            - xh_ * jnp.mean(dxh * xh_, axis=-1, keepdims=True))
        du_ref[...] = du
        dub_ref[...] = du.astype(dub_ref.dtype)
        dg_ref[...] += jnp.sum(dy_ * xh_, axis=0, keepdims=True)
        db_ref[...] += jnp.sum(dy_, axis=0, keepdims=True)

    return pl.pallas_call(
        body, name=name, grid=(S // bs,),
        in_specs=[_row_spec(bs, D_MODEL), _row_spec(bs, D_MODEL), _row_spec(bs, 1), _vec_spec(D_MODEL)],
        out_specs=[_row_spec(bs, D_MODEL), _row_spec(bs, D_MODEL), _vec_spec(D_MODEL), _vec_spec(D_MODEL)],
        out_shape=[jax.ShapeDtypeStruct((S, D_MODEL), F32), jax.ShapeDtypeStruct((S, D_MODEL), MXU_DTYPE),
                   jax.ShapeDtypeStruct((1, D_MODEL), F32), jax.ShapeDtypeStruct((1, D_MODEL), F32)],
        compiler_params=_cparams("arbitrary"),
    )(dy, xh, rs, g)


def _swiglu_fwd(gu, *, bs, name):
    S = gu.shape[0]

    def body(gu_ref, a_ref, aT_ref):
        gt = gu_ref[:, :D_FF]
        a = gt / (1.0 + jnp.exp(-gt)) * gu_ref[:, D_FF:]
        a_ref[...] = a.astype(a_ref.dtype)
        aT_ref[...] = a.T.astype(aT_ref.dtype)

    return pl.pallas_call(
        body, name=name, grid=(S // bs,),
        in_specs=[_row_spec(bs, 2 * D_FF)],
        out_specs=[_row_spec(bs, D_FF), pl.BlockSpec((D_FF, bs), lambda i: (0, i))],
        out_shape=[jax.ShapeDtypeStruct((S, D_FF), MXU_DTYPE), jax.ShapeDtypeStruct((D_FF, S), MXU_DTYPE)],
        compiler_params=_cparams("parallel"),
    )(gu)


def _swiglu_bwd(da, gu, *, bs, name):
    S = gu.shape[0]

    def body(da_ref, gu_ref, dgu_ref):
        gt, da_ = gu_ref[:, :D_FF], da_ref[...]
        sg = 1.0 / (1.0 + jnp.exp(-gt))
        silu = gt * sg
        dgu_ref[:, :D_FF] = (da_ * gu_ref[:, D_FF:] * (sg + silu * (1.0 - sg))).astype(dgu_ref.dtype)
        dgu_ref[:, D_FF:] = (da_ * silu).astype(dgu_ref.dtype)

    return pl.pallas_call(
        body, name=name, grid=(S // bs,),
        in_specs=[_row_spec(bs, D_FF), _row_spec(bs, 2 * D_FF)],
        out_specs=_row_spec(bs, 2 * D_FF), out_shape=jax.ShapeDtypeStruct((S, 2 * D_FF), MXU_DTYPE),
        compiler_params=_cparams("parallel"),
    )(da, gu)


def _loss_head(y, target, *, bs, name):
    S = y.shape[0]

    def body(y_ref, t_ref, dy_ref, loss_ref):
        @pl.when(pl.program_id(0) == 0)
        def _():
            loss_ref[...] = jnp.zeros_like(loss_ref)

        e = y_ref[...] - t_ref[...]
        dy_ref[...] = e * (1.0 / D_MODEL)
        per_tok = jnp.mean(e * e, axis=-1, keepdims=True)
        loss_ref[...] += 0.5 * jnp.sum(per_tok, axis=0, keepdims=True)

    return pl.pallas_call(
        body, name=name, grid=(S // bs,),
        in_specs=[_row_spec(bs, D_MODEL), _row_spec(bs, D_MODEL)],
        out_specs=[_row_spec(bs, D_MODEL), pl.BlockSpec((1, 1), lambda i: (0, 0))],
        out_shape=[jax.ShapeDtypeStruct((S, D_MODEL), F32), jax.ShapeDtypeStruct((1, 1), F32)],
        compiler_params=_cparams("arbitrary"),
    )(y, target)


def _blk(n, target):
    if n <= target:
        return n
    best = None
    for b in range(128, target + 1, 128):
        if n % b == 0:
            best = b
    assert best is not None, n
    return best


def _rope_tables(S):
    pos = jnp.arange(S, dtype=F32)
    inv = ROPE_THETA ** (-jnp.arange(0, MLA_ROPE, 2, dtype=F32) / MLA_ROPE)
    ang = pos[:, None] * inv[None, :]
    cos, sin = jnp.cos(ang), jnp.sin(ang)
    one, zero, pad = jnp.ones((S, HEAD_DIM), F32), jnp.zeros((S, HEAD_DIM), F32), jnp.zeros((S, MLA_PAD - MLA_QK), F32)
    return jnp.concatenate([one, cos, cos, pad], axis=1), jnp.concatenate([zero, sin, sin, pad], axis=1)


def _prep_weights_a(w_in, w_uq, w_ukv):
    z = lambda n: jnp.zeros((D_MODEL, n), w_in.dtype)
    win_a = jnp.concatenate([w_in[:, 0:768], w_in[:, 1188:2468]], axis=1)
    win_l = jnp.concatenate([w_in[:, 772:1156], z(64), w_in[:, 1156:1188], z(32), w_in[:, 768:772], z(124)], axis=1)
    kv = w_ukv.reshape(MLA_KV_RANK, 4, 2 * HEAD_DIM)
    return dict(
        win_a=win_a, win_l=win_l, win_p=jnp.concatenate([win_a, win_l], axis=1),
        wuq=jnp.pad(w_uq.reshape(MLA_Q_RANK, 4, MLA_QK), ((0, 0), (0, 0), (0, MLA_PAD - MLA_QK))).reshape(MLA_Q_RANK, 512),
        wuk=jnp.pad(kv[:, :, :HEAD_DIM], ((0, 0), (0, 0), (0, HEAD_DIM))).reshape(MLA_KV_RANK, 512),
        wuv=kv[:, :, HEAD_DIM:].reshape(MLA_KV_RANK, 256))


def _unprep_grads(dwin_p, dwuq, dwuk, dwuv, dwo, dwgu, dwd):
    dw_in = jnp.concatenate([dwin_p[:, 0:768], dwin_p[:, 2560:2564], dwin_p[:, 2048:2432], dwin_p[:, 2496:2528],
                             dwin_p[:, 768:2048]], axis=1)
    dw_uq = dwuq.reshape(MLA_Q_RANK, 4, MLA_PAD)[:, :, :MLA_QK].reshape(MLA_Q_RANK, 4 * MLA_QK)
    dw_ukv = jnp.concatenate([dwuk.reshape(MLA_KV_RANK, 4, MLA_PAD)[:, :, :HEAD_DIM],
                              dwuv.reshape(MLA_KV_RANK, 4, HEAD_DIM)], axis=2).reshape(MLA_KV_RANK, 512)
    return dict(w_in=dw_in, mla_w_uq=dw_uq, mla_w_ukv=dw_ukv, w_o=dwo, w_gate=dwgu[:, :D_FF], w_up=dwgu[:, D_FF:],
                w_down=dwd)


def _layer_fwd(l, x, xb, xbT, W, P, tabs, blk, late_weights=None):
    S = x.shape[0]
    nb = S // blk
    n = lambda s: f"l{l}_{s}"
    bs = min(512, S)
    h_att = _mm(xb, W["win_a"], name=n("in_att"), out_dtype=MXU_DTYPE, bm=1024, bn=1024, bk=1024, colscale=Q_COLSCALE)
    lat = _mm(xb, W["win_l"], name=n("in_lat"), bm=2048, bn=LAT_W, bk=1024)
    fg = lat[:, 512:516].T.reshape(4, S // 128, 128)
    cpos, cneg = _fox_gate_fwd(fg, P["fox_b_f"], name=n("fox_gate"))
    one3 = jnp.ones((S, 4, 3), MXU_DTYPE)
    zpad = jnp.zeros((S, 4, MLA_PAD - HEAD_DIM - 6), MXU_DTYPE)
    per_tok = lambda parts: parts.reshape(4, 3, S).transpose(2, 0, 1)
    q_f = jnp.concatenate([h_att[:, COL_FQ:COL_FQ + 256].reshape(S, 4, HEAD_DIM), per_tok(cpos), one3, zpad],
                          axis=2).reshape(S, 4 * MLA_PAD)
    k_f = jnp.concatenate([h_att[:, COL_FK:COL_FK + 256].reshape(S, 4, HEAD_DIM), one3, per_tok(cneg), zpad],
                          axis=2).reshape(S, 4 * MLA_PAD)
    v_f = h_att[:, COL_FV:COL_FV + 256]
    oT_a, lse_a = _smax_fwd_t(q_f, k_f, v_f, dk=MLA_PAD, blk=blk, name=n("fox_fwd"))
    q_m, k_m, v_m = _mla_prep_fwd(lat, P["mla_g_q"], P["mla_g_kv"], W["wuq"], W["wuk"], W["wuv"], *tabs,
                                  bs=bs, name=n("mla_prep"))
    oT_b, lse_b = _smax_fwd_t(q_m, k_m, v_m, dk=MLA_PAD, blk=blk, name=n("mla_fwd"))
    bsb = min(BLK_STICK, S)
    oT_c, lt_c = _sb_fwd_t(h_att, blk=bsb, name=n("sb_fwd"))
    out_d = _swa_fwd(h_att, P["swa_sinks"], name=n("swa_fwd"))
    outs = (oT_a, oT_b, oT_c, out_d)
    gn, gnT = _gnorm_fwd(outs, P["mix_g"], bs=bs, name=n("gnorm"))
    if late_weights is not None:
        W = dict(W, **late_weights(gn))
    u1 = _mm(gn, W["w_o"], name=n("out_proj"), bm=1024, bn=1024, bk=1024, resid=x, alpha=ALPHA)
    x1, x1b, x1bT, xh1, rs1 = _ln_fwd(u1, P["ln1_g"], P["ln1_b"], bs=bs, name=n("ln1"))
    gu = _mm(x1b, W["wgu"], name=n("gate_up"), bm=2048, bn=512, bk=1024)
    a, aT = _swiglu_fwd(gu, bs=min(256, S), name=n("swiglu"))
    u2 = _mm(a, W["w_down"], name=n("down"), bm=1024, bn=1024, bk=_blk(D_FF, 1408), resid=x1, alpha=ALPHA)
    x2, x2b, x2bT, xh2, rs2 = _ln_fwd(u2, P["ln2_g"], P["ln2_b"], bs=bs, name=n("ln2"))
    saved = dict(xbT=xbT, gnT=gnT, x1bT=x1bT, h_att=h_att, lat=lat, fg=fg, outs=outs, oT_a=oT_a, oT_b=oT_b, q_f=q_f, k_f=k_f, v_f=v_f,
                 lse_a=lse_a, lse_b=lse_b, lt_c=lt_c, q_m=q_m, k_m=k_m, v_m=v_m,
                 xh1=xh1, rs1=rs1, gu=gu, aT=aT, xh2=xh2, rs2=rs2)
    return x2, x2b, x2bT, saved, W


def _layer_bwd(l, dx2, sv, W, P, tabs, blk, send_early=None):
    S = dx2.shape[0]
    n = lambda s: f"l{l}_{s}"
    bs = min(512, S)
    h_att = sv["h_att"]
    du2, du2b, dg2, db2 = _ln_bwd(dx2, sv["xh2"], sv["rs2"], P["ln2_g"], bs=bs, name=n("ln2_bwd"))
    da = _mm(du2b, W["w_down"], name=n("down_dx"), tb=True, bm=1024, bn=_blk(D_FF, 1408), bk=1024)
    dwd = _mm(sv["aT"], du2b, name=n("down_dw"), bm=_blk(D_FF, 1408), bn=1024, bk=1024)
    dgu = _swiglu_bwd(da, sv["gu"], bs=min(256, S), name=n("swiglu_bwd"))
    dx1 = _mm(dgu, W["wgu"], name=n("gate_up_dx"), tb=True, bm=1024, bn=1024, bk=_blk(2 * D_FF, 1408), resid=du2,
              alpha=ALPHA)
    dwgu = _mm(sv["x1bT"], dgu, name=n("gate_up_dw"), bm=1024, bn=_blk(2 * D_FF, 1408), bk=1024)
    du1, du1b, dg1, db1 = _ln_bwd(dx1, sv["xh1"], sv["rs1"], P["ln1_g"], bs=bs, name=n("ln1_bwd"))
    dgn = _mm(du1b, W["w_o"], name=n("out_proj_dx"), tb=True, bm=1024, bn=1024, bk=1024)
    dwo = _mm(sv["gnT"], du1b, name=n("out_proj_dw"), bm=1024, bn=1024, bk=1024)
    mix_g = P["mix_g"]
    if send_early is not None:
        mix_g = mix_g + send_early(dict(w_o=dwo, w_gate=dwgu[:, :D_FF], w_up=dwgu[:, D_FF:], w_down=dwd))[0, 0]
    dmix, dmixT, dmixg = _gnorm_bwd(dgn, sv["outs"], mix_g, bs=bs, name=n("gnorm_bwd"))
    q_f, k_f = sv["q_f"], sv["k_f"]
    dqT_a, dk_a, dva = _smax_bwd_t(q_f, k_f, sv["v_f"], dmix, dmixT, sv["oT_a"], sv["lse_a"], dk=MLA_PAD, dcb=0,
                                   qscale=HEAD_DIM ** -0.5, blk=blk, name=n("fox_bwd"))
    dq_a, dk_a = dqT_a.T.reshape(S, 4, MLA_PAD), dk_a.reshape(S, 4, MLA_PAD)
    dqa, dka = dq_a[:, :, :HEAD_DIM].reshape(S, 256), dk_a[:, :, :HEAD_DIM].reshape(S, 256)
    dcq = dq_a[:, :, HEAD_DIM].T.reshape(4, S // 128, 128)
    dck = dk_a[:, :, HEAD_DIM + 3].T.reshape(4, S // 128, 128)
    q_m, k_m = sv["q_m"], sv["k_m"]
    dqT_b, dkb, dvb = _smax_bwd_t(q_m, k_m, sv["v_m"], dmix, dmixT, sv["oT_b"], sv["lse_b"], dk=MLA_PAD, dcb=2,
                                  qscale=MLA_QK ** -0.5, blk=blk, name=n("mla_bwd"))
    dqT_c, dkc, dvc = _sb_bwd_t(h_att, dmix, dmixT, sv["lt_c"], dcb=4, qscale=HEAD_DIM ** -0.5,
                                blk=min(BLK_STICK, S), name=n("sb_bwd"))
    dqc = dqT_c.T
    dqd, dkd, dvd, dsink = _swa_bwd(h_att, P["swa_sinks"], dmix, sv["outs"][3], dcb=3, name=n("swa_bwd"))
    dlat, dwuq, dwuk, dwuv, dgq, dgkv = _mla_prep_bwd(
        sv["lat"], P["mla_g_q"], P["mla_g_kv"], W["wuq"], W["wuk"], W["wuv"], *tabs, dqT_b, dkb, dvb,
        bs=bs, name=n("mla_prep_bwd"))
    dfg, dbf = _fox_gate_bwd(sv["fg"], P["fox_b_f"], dck, dcq, q_unscale=HEAD_DIM ** 0.5, name=n("fox_gate_bwd"))
    dfg_blk = jnp.pad(dfg.reshape(4, S).T, ((0, 0), (0, 124)))
    dh = jnp.concatenate([t.astype(MXU_DTYPE) for t in (dqa, dka, dva, dqc, dkc, dvc, dqd, dkd, dvd, dlat, dfg_blk)], axis=1)
    dx = _mm(dh, W["win_p"], name=n("in_dx"), tb=True, bm=1024, bn=1024, bk=_blk(PERM_W, 1024), resid=du1, alpha=ALPHA)
    dwin_p = _mm(sv["xbT"], dh, name=n("in_dw"), bm=1024, bn=_blk(PERM_W, 1024), bk=1024)
    grads = _unprep_grads(dwin_p, dwuq, dwuk, dwuv, dwo, dwgu, dwd)
    grads.update(fox_b_f=dbf[:, 0], mla_g_q=dgq[0], mla_g_kv=dgkv[0], swa_sinks=dsink[:, 0], mix_g=dmixg[0],
                 ln1_g=dg1[0], ln1_b=db1[0], ln2_g=dg2[0], ln2_b=db2[0])
    return dx, grads


BIG = ("w_in", "mla_w_uq", "mla_w_ukv", "w_o", "w_gate", "w_up", "w_down")
SMALL = ("fox_b_f", "mla_g_q", "mla_g_kv", "swa_sinks", "mix_g", "ln1_g", "ln1_b", "ln2_g", "ln2_b")
SHARD_AXIS = dict(w_in=2, mla_w_uq=2, mla_w_ukv=2, w_o=1, w_gate=2, w_up=2, w_down=1)
N_CHIPS = 4
ANY = pl.BlockSpec(memory_space=pl.ANY)


HBM = pl.BlockSpec(memory_space=pltpu.HBM)
SEM = pl.BlockSpec(memory_space=pltpu.SEMAPHORE)
N_PEER_CHIPS = N_CHIPS - 1


def _peer_copies(src_ref, land_ref, sems, scatter):
    x, y, c = lax.axis_index("x"), lax.axis_index("y"), lax.axis_index("c")
    me = 2 * x + y
    out = []
    for r, (px, py) in enumerate([(1 - x, y), (x, 1 - y), (1 - x, 1 - y)]):
        theirs = 2 * px + py
        send = pltpu.make_async_remote_copy(
            src_ref=src_ref.at[theirs] if scatter else src_ref, dst_ref=land_ref.at[me],
            send_sem=sems[2 * r], recv_sem=sems[2 * r + 1], device_id=(px, py, c), device_id_type=MESH)
        arrive = pltpu.make_async_remote_copy(
            src_ref=src_ref.at[me] if scatter else src_ref, dst_ref=land_ref.at[theirs],
            send_sem=sems[2 * r], recv_sem=sems[2 * r + 1], device_id=(px, py, c), device_id_type=MESH)
        out.append((send, arrive))
    return out


def _exchange_start(srcs, *, scatter, name):
    nt = len(srcs)
    ns = 2 * N_PEER_CHIPS * nt
    land_shapes = [s.shape if scatter else (N_CHIPS,) + s.shape for s in srcs]

    def body(*refs):
        src_refs, land_refs, outs = refs[:nt], refs[nt:2 * nt], refs[2 * nt:]
        for t in range(nt):
            for send, _ in _peer_copies(src_refs[t], land_refs[t], outs[6 * t:6 * t + 6], scatter):
                send.start()
        outs[-1][...] = jnp.zeros_like(outs[-1])

    res = pl.pallas_call(
        body, name=name,
        out_shape=(*[pltpu.SemaphoreType.DMA(())] * ns, *[pltpu.HBM(s.shape, s.dtype) for s in srcs],
                   *[pltpu.HBM(ls, s.dtype) for ls, s in zip(land_shapes, srcs)], jax.ShapeDtypeStruct((8, 128), F32)),
        in_specs=(HBM,) * (2 * nt), out_specs=(*[SEM] * ns, *[HBM] * (2 * nt), pl.BlockSpec(memory_space=pltpu.VMEM)),
        input_output_aliases={i: ns + i for i in range(2 * nt)},
        compiler_params=pltpu.CompilerParams(has_side_effects=pltpu.SideEffectType.DATAFLOW_SIDE_EFFECTING),
    )(*[pltpu.with_memory_space_constraint(s, pltpu.HBM) for s in srcs],
      *[pltpu.with_memory_space_constraint(lax.empty(ls, s.dtype), pltpu.HBM) for ls, s in zip(land_shapes, srcs)])
    return dict(sems=res[:ns], srcs=res[ns:ns + nt], lands=res[ns + nt:ns + 2 * nt], token=res[-1])


def _exchange_wait(started, after, *, scatter, name):
    nt = len(started["srcs"])
    ns = 2 * N_PEER_CHIPS * nt

    def body(*refs):
        src_refs, land_refs, sems = refs[:nt], refs[nt:2 * nt], refs[2 * nt:2 * nt + ns]
        for t in range(nt):
            for send, arrive in _peer_copies(src_refs[t], land_refs[t], sems[6 * t:6 * t + 6], scatter):
                send.wait_send()
                arrive.wait_recv()

    both = list(started["srcs"]) + list(started["lands"])
    res = pl.pallas_call(
        body, name=name, out_shape=tuple(pltpu.HBM(a.shape, a.dtype) for a in both),
        in_specs=(*[HBM] * (2 * nt), *[SEM] * ns, ANY), out_specs=(HBM,) * (2 * nt),
        input_output_aliases={i: i for i in range(2 * nt)},
        compiler_params=pltpu.CompilerParams(has_side_effects=pltpu.SideEffectType.DATAFLOW_SIDE_EFFECTING),
    )(*both, *started["sems"], after)
    return res[:nt], res[nt:]


def _core_exchange(tensors, *, name):
    nt = len(tensors)

    def body(*refs):
        ins, outs = refs[:nt], refs[nt:2 * nt]
        send_sems, recv_sems = refs[2 * nt:]
        sibling = (lax.axis_index("x"), lax.axis_index("y"), 1 - lax.axis_index("c"))
        copies = [pltpu.make_async_remote_copy(src_ref=ins[t], dst_ref=outs[t], send_sem=send_sems.at[t],
                                               recv_sem=recv_sems.at[t], device_id=sibling, device_id_type=MESH)
                  for t in range(nt)]
        for cp in copies:
            cp.start()
        for cp in copies:
            cp.wait_recv()
        for cp in copies:
            cp.wait_send()

    return pl.pallas_call(
        body, name=name, in_specs=[ANY] * nt, out_specs=[ANY] * nt,
        out_shape=[jax.ShapeDtypeStruct(t.shape, t.dtype) for t in tensors],
        scratch_shapes=[pltpu.SemaphoreType.DMA((nt,)), pltpu.SemaphoreType.DMA((nt,))],
        compiler_params=pltpu.CompilerParams(has_side_effects=True),
    )(*tensors)


def _all_sum_small(block, *, name):
    R = block.shape[0]
    n_dev = 8

    def body(x_ref, o_ref, slots, send_sems, recv_sems):
        x, y, c = lax.axis_index("x"), lax.axis_index("y"), lax.axis_index("c")
        me = 4 * x + 2 * y + c
        slots[me] = x_ref[...]
        sends, recvs = [], []
        for d in range(1, n_dev):
            px, py, pc = x ^ (d >> 2), y ^ ((d >> 1) & 1), c ^ (d & 1)
            theirs = 4 * px + 2 * py + pc
            sends.append(pltpu.make_async_remote_copy(
                src_ref=x_ref, dst_ref=slots.at[me], send_sem=send_sems.at[d - 1], recv_sem=recv_sems.at[d - 1],
                device_id=(px, py, pc), device_id_type=MESH))
            recvs.append(pltpu.make_async_remote_copy(
                src_ref=x_ref, dst_ref=slots.at[theirs], send_sem=send_sems.at[d - 1], recv_sem=recv_sems.at[d - 1],
                device_id=(px, py, pc), device_id_type=MESH))
        for cp in sends:
            cp.start()
        for cp in recvs:
            cp.wait_recv()
        for cp in sends:
            cp.wait_send()
        total = slots[0]
        for k in range(1, n_dev):
            total = total + slots[k]
        o_ref[...] = total

    return pl.pallas_call(
        body, name=name, in_specs=[pl.BlockSpec(memory_space=pltpu.VMEM)],
        out_specs=pl.BlockSpec(memory_space=pltpu.VMEM), out_shape=jax.ShapeDtypeStruct((R, 128), F32),
        scratch_shapes=[pltpu.VMEM((n_dev, R, 128), F32), pltpu.SemaphoreType.DMA((n_dev - 1,)),
                        pltpu.SemaphoreType.DMA((n_dev - 1,))],
        compiler_params=pltpu.CompilerParams(has_side_effects=True),
    )(block)


def _sum_chips_into(acc, land, own, me, layer, *, br, name):
    _, R, C = land.shape

    def body(me_ref, land_ref, own_ref, acc_ref, o_ref):
        mine = me_ref[0]
        total = None
        for k in range(N_CHIPS):
            part = jnp.where(mine == k, own_ref[...], land_ref[k]).astype(F32)
            total = part if total is None else total + part
        o_ref[0] = total

    return pl.pallas_call(
        body, name=name, grid=(R // br,),
        in_specs=[pl.BlockSpec(memory_space=pltpu.SMEM), pl.BlockSpec((N_CHIPS, br, C), lambda i: (0, i, 0)),
                  pl.BlockSpec((br, C), lambda i: (i, 0)), ANY],
        out_specs=pl.BlockSpec((1, br, C), lambda i: (layer, i, 0)),
        out_shape=jax.ShapeDtypeStruct(acc.shape, F32), input_output_aliases={3: 0},
        compiler_params=_cparams("parallel"),
    )(me, land, own, acc)


def _adamw_math(w, g, m, v):
    m = ADAM_B1 * m + (1.0 - ADAM_B1) * g
    v = ADAM_B2 * v + (1.0 - ADAM_B2) * (g * g)
    m_hat = m / (1.0 - ADAM_B1 ** ADAM_STEP)
    v_hat = v / (1.0 - ADAM_B2 ** ADAM_STEP)
    return -ADAM_LR * (m_hat / (jnp.sqrt(v_hat) + ADAM_EPS) + ADAM_WD * w), m, v


def _adamw(w, m, v, g_a, g_b, *, br, name):
    R, C = w.shape
    two = g_b is not None

    def body(*refs):
        if two:
            w_ref, m_ref, v_ref, ga_ref, gb_ref, g_ref, d_ref, nm_ref, nv_ref = refs
            g = ga_ref[...] + gb_ref[...]
        else:
            w_ref, m_ref, v_ref, ga_ref, g_ref, d_ref, nm_ref, nv_ref = refs
            g = ga_ref[...]
        g_ref[...] = g
        d_ref[...], nm_ref[...], nv_ref[...] = _adamw_math(w_ref[...], g, m_ref[...], v_ref[...])

    spec = pl.BlockSpec((br, C), lambda i: (i, 0))
    args = [w, m, v, g_a] + ([g_b] if two else [])
    return pl.pallas_call(
        body, name=name, grid=(R // br,), in_specs=[spec] * len(args), out_specs=[spec] * 4,
        out_shape=[jax.ShapeDtypeStruct((R, C), F32)] * 4,
        compiler_params=_cparams("parallel"),
    )(*args)


SMALL_ROWS = dict(fox_b_f=1, mla_g_q=2, mla_g_kv=1, swa_sinks=1, mix_g=8, ln1_g=8, ln1_b=8, ln2_g=8, ln2_b=8)
SMALL_ROWS_PER_LAYER = sum(SMALL_ROWS.values())


def _pack_small(vals, extra_rows):
    L = vals[SMALL[0]].shape[0]
    per_layer = []
    for name in SMALL:
        a = vals[name].astype(F32)
        a = jnp.pad(a, ((0, 0), (0, SMALL_ROWS[name] * 128 - a.shape[1])))
        per_layer.append(a.reshape(L, SMALL_ROWS[name], 128))
    out = jnp.concatenate(per_layer, axis=1).reshape(L * SMALL_ROWS_PER_LAYER, 128)
    return jnp.pad(out, ((0, extra_rows), (0, 0)))


def _unpack_small(block, shapes):
    L = shapes[SMALL[0]][0]
    body = block[:L * SMALL_ROWS_PER_LAYER].reshape(L, SMALL_ROWS_PER_LAYER, 128)
    out, r = {}, 0
    for name in SMALL:
        n = shapes[name][1]
        out[name] = body[:, r:r + SMALL_ROWS[name]].reshape(L, SMALL_ROWS[name] * 128)[:, :n]
        r += SMALL_ROWS[name]
    return out


def _to_chips(g, axis):
    L, a, b = g.shape
    if axis == 2:
        return g.reshape(L, a, N_CHIPS, b // N_CHIPS).transpose(2, 0, 1, 3)
    return g.reshape(L, N_CHIPS, a // N_CHIPS, b).transpose(1, 0, 2, 3)


def kernel(x, w_in, fox_b_f, mla_g_q, mla_g_kv, mla_w_uq, mla_w_ukv, swa_sinks, mix_g, w_o, ln1_g, ln1_b, w_gate, w_up, w_down, ln2_g, ln2_b, loss_target, m_w_in, m_fox_b_f, m_mla_g_q, m_mla_g_kv, m_mla_w_uq, m_mla_w_ukv, m_swa_sinks, m_mix_g, m_w_o, m_ln1_g, m_ln1_b, m_w_gate, m_w_up, m_w_down, m_ln2_g, m_ln2_b, v_w_in, v_fox_b_f, v_mla_g_q, v_mla_g_kv, v_mla_w_uq, v_mla_w_ukv, v_swa_sinks, v_mix_g, v_w_o, v_ln1_g, v_ln1_b, v_w_gate, v_w_up, v_w_down, v_ln2_g, v_ln2_b):
    w = dict(w_in=w_in, fox_b_f=fox_b_f, mla_g_q=mla_g_q, mla_g_kv=mla_g_kv, mla_w_uq=mla_w_uq, mla_w_ukv=mla_w_ukv,
             swa_sinks=swa_sinks, mix_g=mix_g, w_o=w_o, ln1_g=ln1_g, ln1_b=ln1_b, w_gate=w_gate, w_up=w_up,
             w_down=w_down, ln2_g=ln2_g, ln2_b=ln2_b)
    m = dict(w_in=m_w_in, fox_b_f=m_fox_b_f, mla_g_q=m_mla_g_q, mla_g_kv=m_mla_g_kv, mla_w_uq=m_mla_w_uq,
             mla_w_ukv=m_mla_w_ukv, swa_sinks=m_swa_sinks, mix_g=m_mix_g, w_o=m_w_o, ln1_g=m_ln1_g, ln1_b=m_ln1_b,
             w_gate=m_w_gate, w_up=m_w_up, w_down=m_w_down, ln2_g=m_ln2_g, ln2_b=m_ln2_b)
    v = dict(w_in=v_w_in, fox_b_f=v_fox_b_f, mla_g_q=v_mla_g_q, mla_g_kv=v_mla_g_kv, mla_w_uq=v_mla_w_uq,
             mla_w_ukv=v_mla_w_ukv, swa_sinks=v_swa_sinks, mix_g=v_mix_g, w_o=v_w_o, ln1_g=v_ln1_g, ln1_b=v_ln1_b,
             w_gate=v_w_gate, w_up=v_w_up, w_down=v_w_down, ln2_g=v_ln2_g, ln2_b=v_ln2_b)
    names = tuple(w)
    L = w_in.shape[0]
    S = x.shape[1]
    blk = min(BLK_SOFTMAX, S)
    bs = min(512, S)

    me = 2 * lax.axis_index("x") + lax.axis_index("y")
    axis_of = {k: SHARD_AXIS[k] - 1 for k in BIG}
    groups = (("w_in", "mla_w_uq", "mla_w_ukv"), ("w_o", "w_gate", "w_up", "w_down"))

    started, last = [], None
    for l in range(L):
        per_group = []
        for g, group in enumerate(groups):
            srcs = [w[k][l].astype(MXU_DTYPE) for k in group]
            if last is not None:
                t = min(range(len(srcs)), key=lambda i: srcs[i].size)
                srcs[t] = srcs[t] + last["token"][0, 0].astype(MXU_DTYPE)
            last = _exchange_start(srcs, scatter=False, name=f"gather_start{l}_{g}")
            per_group.append(last)
        started.append(per_group)
    all_started = sum(st["token"] for per_group in started for st in per_group)

    def gathered(l, g, after):
        mine, lands = _exchange_wait(started[l][g], after, scatter=False, name=f"gather_wait{l}_{g}")
        shard = lambda t, k: jnp.where(me == k, mine[t], lands[t][k])
        whole = lambda t, axis: jnp.concatenate([shard(t, k) for k in range(N_CHIPS)], axis=axis)
        if g == 0:
            return _prep_weights_a(*[whole(t, axis_of[name]) for t, name in enumerate(groups[0])])
        gate_up = jnp.concatenate([shard(t, k) for t in (1, 2) for k in range(N_CHIPS)], axis=1)
        return dict(w_o=whole(0, 0), wgu=gate_up, w_down=whole(3, 0))

    def scatter(l, g, grads):
        to_owner = [_to_chips(grads[k].astype(MXU_DTYPE)[None], axis_of[k] + 1)[:, 0] for k in groups[g]]
        return _exchange_start(to_owner, scatter=True, name=f"scatter_start{l}_{g}")

    tabs = _rope_tables(S)
    Ps = []
    for l in range(L):
        P = dict(fox_b_f=fox_b_f[l], swa_sinks=swa_sinks[l])
        for k in ("mla_g_q", "mla_g_kv", "mix_g", "ln1_g", "ln1_b", "ln2_g", "ln2_b"):
            P[k] = w[k][l][None, :]
        Ps.append(P)

    xa = x[0]
    xb = xa.astype(MXU_DTYPE)
    xbT = xb.T
    saved, Ws = [], []
    for l in range(L):
        W = gathered(l, 0, all_started if l == 0 else xa)
        late = lambda after, l=l: gathered(l, 1, after)
        xa, xb, xbT, sv, W = _layer_fwd(l, xa, xb, xbT, W, Ps[l], tabs, blk, late_weights=late)
        saved.append(sv)
        Ws.append(W)
    dx, loss_part = _loss_head(xa, loss_target[0], bs=bs, name="loss_head")

    layer_grads = [None] * L
    sent = [[None, None] for _ in range(L)]
    pin = None
    for l in reversed(range(L)):
        P = Ps[l] if pin is None else dict(Ps[l], ln2_g=Ps[l]["ln2_g"] + pin[0, 0])

        def send_early(grads, l=l):
            sent[l][1] = scatter(l, 1, grads)
            return sent[l][1]["token"]

        dx, layer_grads[l] = _layer_bwd(l, dx, saved[l], Ws[l], P, tabs, blk, send_early=send_early)
        sent[l][0] = scatter(l, 0, layer_grads[l])
        pin = sent[l][0]["token"]
    grad_x = dx[None]

    me_arr = me.astype(jnp.int32)[None]
    partial = {k: jnp.zeros(w[k].shape, F32) for k in BIG}
    after = dx
    for l in reversed(range(L)):
        for g in (1, 0):
            mine, lands = _exchange_wait(sent[l][g], after, scatter=True, name=f"scatter_wait{l}_{g}")
            for t, k in enumerate(groups[g]):
                own = lax.dynamic_index_in_dim(mine[t], me, 0, keepdims=False)
                partial[k] = _sum_chips_into(partial[k], lands[t], own, me_arr, l, br=_rows(own.shape[0]),
                                             name=f"sum_{k}_l{l}")
            after = partial[groups[g][-1]]
    partial = [partial[k] for k in BIG]
    sibling = _core_exchange(partial, name="swap_partials")
    local = {k: jnp.stack([layer_grads[l][k] for l in range(L)]) for k in SMALL}
    out = {}
    for k, mine, theirs in zip(BIG, partial, sibling):
        shp = w[k].shape
        two_d = lambda a: a.reshape(shp[0] * shp[1], shp[2])
        res = _adamw(two_d(w[k]), two_d(m[k]), two_d(v[k]), two_d(mine), two_d(theirs), br=_rows(shp[0] * shp[1]),
                     name=f"adamw_{k}")
        out[k] = [a.reshape(shp) for a in res]

    shapes = {k: w[k].shape for k in SMALL}
    extra = 8 + (-L * SMALL_ROWS_PER_LAYER) % 8
    block = _pack_small({k: local[k] for k in SMALL}, extra)
    block = block.at[L * SMALL_ROWS_PER_LAYER, 0].set(loss_part[0, 0])
    total = _all_sum_small(block, name="sum_small")
    loss = total[L * SMALL_ROWS_PER_LAYER, 0]
    res = _adamw(_pack_small({k: w[k] for k in SMALL}, extra), _pack_small({k: m[k] for k in SMALL}, extra),
                 _pack_small({k: v[k] for k in SMALL}, extra), total, None, br=total.shape[0], name="adamw_small")
    res = [_unpack_small(t, shapes) for t in res]
    for k in SMALL:
        out[k] = [r[k] for r in res]

    return (loss, grad_x, *[out[k][0] for k in names], *[out[k][1] for k in names],
            *[out[k][2] for k in names], *[out[k][3] for k in names])


def _rows(n):
    for b in (256, 128, 64, 32, 16, 8):
        if n % b == 0:
            return b
    return n
```

```python
import functools

import numpy as np
import jax
import jax.numpy as jnp
from jax import lax
from jax.experimental import pallas as pl
from jax.experimental.pallas import tpu as pltpu

F32 = jnp.float32
MXU_DTYPE = jnp.bfloat16
NEG_INF = -1e30

D_MODEL = 1024
DEPTH = 4
HEAD_DIM = 64
GROUP_WIDTH = 256
D_FF = 2816
MLA_Q_RANK = 256
MLA_KV_RANK = 128
MLA_ROPE = 32
MLA_QK = 96
MLA_PAD = 128
ROPE_THETA = 10000.0
WINDOW = 128
ALPHA = (2.0 * DEPTH) ** 0.25
SWA_SLOPES = tuple(float(2.0 ** (-8.0 * h / 4)) for h in range(1, 5))
ATT_W = 2048
LAT_W = 640
PERM_W = ATT_W + LAT_W
COL_FQ, COL_FK, COL_FV = 0, 256, 512
COL_SQ, COL_SK, COL_SV = 768, 1024, 1280
COL_WQ, COL_WK, COL_WV = 1536, 1792, 1920
Q_COLSCALE = np.ones((1, ATT_W), np.float32)
Q_COLSCALE[:, COL_FQ:COL_FQ + 256] = HEAD_DIM ** -0.5
Q_COLSCALE[:, COL_SQ:COL_SQ + 256] = HEAD_DIM ** -0.5

ADAM_LR, ADAM_B1, ADAM_B2, ADAM_EPS, ADAM_WD, ADAM_STEP = 0.001, 0.9, 0.999, 1e-08, 0.01, 10

VMEM_LIMIT = 56 * 1024 * 1024
NT = (((1,), (1,)), ((), ()))
TN = (((0,), (0,)), ((), ()))
MESH = pl.DeviceIdType.MESH


def _cparams(*sem):
    return pltpu.CompilerParams(dimension_semantics=sem, vmem_limit_bytes=VMEM_LIMIT)


def _dot01(x, m01, dn=None, parts=2):
    acc = None
    rem = x
    for _ in range(parts):
        part = rem.astype(MXU_DTYPE)
        rem = rem - part.astype(F32)
        if dn is None:
            t = jnp.dot(part, m01, preferred_element_type=F32)
        else:
            t = lax.dot_general(part, m01, dn, preferred_element_type=F32)
        acc = t if acc is None else acc + t
    return acc


def _mm(a, b, *, name, ta=False, tb=False, out_dtype=F32, bm=512, bn=512, bk=512, resid=None, alpha=1.0,
        colscale=None, after=None):
    M, K = (a.shape[1], a.shape[0]) if ta else a.shape
    N = b.shape[0] if tb else b.shape[1]
    assert (b.shape[1] if tb else b.shape[0]) == K
    assert resid is None or colscale is None
    bm, bn, bk = min(bm, M), min(bn, N), min(bk, K)
    assert M % bm == 0 and N % bn == 0 and K % bk == 0, (name, M, N, K, bm, bn, bk)
    nk = K // bk
    assert nk == 1 or (out_dtype == F32 and colscale is None), name
    dn = (((0 if ta else 1,), (1 if tb else 0,)), ((), ()))

    extra = resid is not None or colscale is not None

    def body(*refs):
        a_ref, b_ref = refs[:2]
        r_ref = refs[2] if extra else None
        o_ref = refs[n_out]
        k = pl.program_id(2)

        def first():
            r = lax.dot_general(a_ref[...].astype(MXU_DTYPE), b_ref[...].astype(MXU_DTYPE), dn,
                                preferred_element_type=F32)
            if resid is not None:
                r = r + alpha * r_ref[...]
            if colscale is not None:
                r = r * r_ref[...]
            o_ref[...] = r.astype(o_ref.dtype)

        if nk == 1:
            first()
        else:
            pl.when(k == 0)(first)

            @pl.when(k > 0)
            def _():
                o_ref[...] += lax.dot_general(a_ref[...].astype(MXU_DTYPE), b_ref[...].astype(MXU_DTYPE), dn,
                                              preferred_element_type=F32)

    a_spec = pl.BlockSpec((bk, bm), lambda i, j, k: (k, i)) if ta else pl.BlockSpec((bm, bk), lambda i, j, k: (i, k))
    b_spec = pl.BlockSpec((bn, bk), lambda i, j, k: (j, k)) if tb else pl.BlockSpec((bk, bn), lambda i, j, k: (k, j))
    in_specs = [a_spec, b_spec]
    args = [a, b]
    if resid is not None:
        in_specs.append(pl.BlockSpec((bm, bn), lambda i, j, k: (i, j)))
        args.append(resid)
    if colscale is not None:
        in_specs.append(pl.BlockSpec((1, bn), lambda i, j, k: (0, j)))
        args.append(colscale)
    if after is not None:
        in_specs.append(pl.BlockSpec((8, 128), lambda i, j, k: (0, 0)))
        args.append(after)
    n_out = len(args)
    return pl.pallas_call(
        body, name=name, grid=(M // bm, N // bn, nk), in_specs=in_specs,
        out_specs=pl.BlockSpec((bm, bn), lambda i, j, k: (i, j)),
        out_shape=jax.ShapeDtypeStruct((M, N), out_dtype),
        compiler_params=_cparams("parallel", "parallel", "arbitrary"),
    )(*args)


HP = 4
BLK_SOFTMAX = 512
BLK_STICK = 256


def _t(x):
    return x.astype(F32).T.astype(MXU_DTYPE)


def _fill_transposed(dst_ref, src_ref, nb, blk):
    for j in range(nb):
        dst_ref[j] = _t(src_ref[j * blk:(j + 1) * blk, :])


def _smax_fwd_t(q, k, v, *, dk, blk, name):
    S = k.shape[0]
    nb = S // blk
    H = k.shape[1] // dk

    def body(q_ref, k_ref, v_ref, oT_ref, lse_ref, vT_ref):
        i = pl.program_id(1)

        @pl.when(i == 0)
        def _():
            _fill_transposed(vT_ref, v_ref, nb, blk)

        key = lax.broadcasted_iota(jnp.int32, (blk, blk), 0)
        qry = lax.broadcasted_iota(jnp.int32, (blk, blk), 1)
        qs = [_t(q_ref[:, h * dk:(h + 1) * dk]) for h in range(HP)]

        def tile(j, carry, masked):
            r0 = pl.multiple_of(j * blk, blk)
            ss = [jnp.dot(k_ref[pl.ds(r0, blk), h * dk:(h + 1) * dk], qs[h], preferred_element_type=F32)
                  for h in range(HP)]
            stats, pes = [], []
            for h in range(HP):
                m, l, _ = carry[h]
                s = jnp.where(key <= qry, ss[h], NEG_INF) if masked else ss[h]
                mn = jnp.maximum(m, jnp.max(s, axis=0, keepdims=True))
                a = jnp.exp(m - mn)
                pe = jnp.exp(s - mn)
                stats.append((mn, a * l + jnp.sum(pe, axis=0, keepdims=True), a))
                pes.append(pe.astype(MXU_DTYPE))
            pvs = [jnp.dot(vT_ref[j, h * HEAD_DIM:(h + 1) * HEAD_DIM, :], pes[h], preferred_element_type=F32)
                   for h in range(HP)]
            return tuple((stats[h][0], stats[h][1], stats[h][2] * carry[h][2] + pvs[h]) for h in range(HP))

        init = tuple((jnp.full((1, blk), NEG_INF, F32), jnp.zeros((1, blk), F32), jnp.zeros((HEAD_DIM, blk), F32))
                     for _ in range(HP))
        carry = lax.fori_loop(0, i, functools.partial(tile, masked=False), init)
        carry = tile(i, carry, True)
        for h in range(HP):
            m, l, acc = carry[h]
            oT_ref[h * HEAD_DIM:(h + 1) * HEAD_DIM, :] = acc / l
            lse_ref[h, 0] = m + jnp.log(l)

    return pl.pallas_call(
        body, name=name, grid=(H // HP, nb),
        in_specs=[pl.BlockSpec((blk, HP * dk), lambda p, i: (i, p)),
                  pl.BlockSpec((S, HP * dk), lambda p, i: (0, p)),
                  pl.BlockSpec((S, HP * HEAD_DIM), lambda p, i: (0, p))],
        out_specs=[pl.BlockSpec((HP * HEAD_DIM, blk), lambda p, i: (p, i)),
                   pl.BlockSpec((HP, 1, 1, blk), lambda p, i: (p, i, 0, 0))],
        out_shape=[jax.ShapeDtypeStruct((H * HEAD_DIM, S), F32), jax.ShapeDtypeStruct((H, nb, 1, blk), F32)],
        scratch_shapes=[pltpu.VMEM((nb, HP * HEAD_DIM, blk), MXU_DTYPE)],
        compiler_params=_cparams("arbitrary", "arbitrary"),
    )(q, k, v)


def _smax_bwd_t(q, k, v, dmix, dmixT, oT, lse, *, dk, dcb, qscale, blk, name):
    S = k.shape[0]
    nb = S // blk
    H = k.shape[1] // dk
    hd = HP * HEAD_DIM
    dcr = dcb * 128 // hd

    def body(q_ref, k_ref, v_ref, do_ref, doT_ref, oT_ref, lse_ref, dqT_ref, dk_ref, dv_ref, kT_ref):
        i = pl.program_id(1)

        @pl.when(i == 0)
        def _():
            dk_ref[...] = jnp.zeros_like(dk_ref)
            dv_ref[...] = jnp.zeros_like(dv_ref)
            _fill_transposed(kT_ref, k_ref, nb, blk)

        key = lax.broadcasted_iota(jnp.int32, (blk, blk), 0)
        qry = lax.broadcasted_iota(jnp.int32, (blk, blk), 1)
        per_head = []
        for h in range(HP):
            hs = slice(h * HEAD_DIM, (h + 1) * HEAD_DIM)
            doT = doT_ref[hs, :]
            per_head.append(dict(
                qT=_t(q_ref[:, h * dk:(h + 1) * dk]), q=q_ref[:, h * dk:(h + 1) * dk],
                doT=doT.astype(MXU_DTYPE), do=do_ref[:, hs].astype(MXU_DTYPE),
                delta=jnp.sum(doT * oT_ref[hs, :], axis=0, keepdims=True), lse=lse_ref[h, 0]))

        def tile(j, dqs, masked):
            r0 = pl.multiple_of(j * blk, blk)
            rows = pl.ds(r0, blk)
            ksl = [slice(h * dk, (h + 1) * dk) for h in range(HP)]
            hsl = [slice(h * HEAD_DIM, (h + 1) * HEAD_DIM) for h in range(HP)]
            ss = [jnp.dot(k_ref[rows, ksl[h]], per_head[h]["qT"], preferred_element_type=F32) for h in range(HP)]
            dps = [jnp.dot(v_ref[rows, hsl[h]], per_head[h]["doT"], preferred_element_type=F32) for h in range(HP)]
            prs, dss = [], []
            for h in range(HP):
                c = per_head[h]
                s = jnp.where(key <= qry, ss[h], NEG_INF) if masked else ss[h]
                pr = jnp.exp(s - c["lse"])
                dss.append((pr * (dps[h] - c["delta"])).astype(MXU_DTYPE))
                prs.append(pr.astype(MXU_DTYPE))
            for h in range(HP):
                dv_ref[rows, hsl[h]] += jnp.dot(prs[h], per_head[h]["do"], preferred_element_type=F32)
            for h in range(HP):
                dk_ref[rows, ksl[h]] += jnp.dot(dss[h], per_head[h]["q"], preferred_element_type=F32)
            return tuple(dqs[h] + jnp.dot(kT_ref[j, ksl[h], :], dss[h], preferred_element_type=F32) for h in range(HP))

        dqs = lax.fori_loop(0, i, functools.partial(tile, masked=False),
                            tuple(jnp.zeros((dk, blk), F32) for _ in range(HP)))
        dqs = tile(i, dqs, True)
        for h in range(HP):
            dqT_ref[h * dk:(h + 1) * dk, :] = dqs[h] * qscale

    return pl.pallas_call(
        body, name=name, grid=(H // HP, nb),
        in_specs=[pl.BlockSpec((blk, HP * dk), lambda p, i: (i, p)),
                  pl.BlockSpec((S, HP * dk), lambda p, i: (0, p)),
                  pl.BlockSpec((S, hd), lambda p, i: (0, p)),
                  pl.BlockSpec((blk, hd), lambda p, i: (i, dcr + p)),
                  pl.BlockSpec((hd, blk), lambda p, i: (dcr + p, i)),
                  pl.BlockSpec((hd, blk), lambda p, i: (p, i)),
                  pl.BlockSpec((HP, 1, 1, blk), lambda p, i: (p, i, 0, 0))],
        out_specs=[pl.BlockSpec((HP * dk, blk), lambda p, i: (p, i)),
                   pl.BlockSpec((S, HP * dk), lambda p, i: (0, p)),
                   pl.BlockSpec((S, hd), lambda p, i: (0, p))],
        out_shape=[jax.ShapeDtypeStruct((H * dk, S), F32), jax.ShapeDtypeStruct((S, H * dk), F32),
                   jax.ShapeDtypeStruct((S, H * HEAD_DIM), F32)],
        scratch_shapes=[pltpu.VMEM((nb, HP * dk, blk), MXU_DTYPE)],
        compiler_params=_cparams("arbitrary", "arbitrary"),
    )(q, k, v, dmix, dmixT, oT, lse)


def _log1m_beta(z):
    return -(jnp.maximum(z, 0.0) + jnp.log(1.0 + jnp.exp(-jnp.abs(z))))


def _dot01_left(m01, x, parts=2):
    acc = None
    rem = x
    for _ in range(parts):
        part = rem.astype(MXU_DTYPE)
        rem = rem - part.astype(F32)
        t = jnp.dot(m01, part, preferred_element_type=F32)
        acc = t if acc is None else acc + t
    return acc


def _sb_fwd_t(h_att, *, blk, name):
    S = h_att.shape[0]
    nb = S // blk
    hd = HP * HEAD_DIM
    qcb, kcb, vcb = COL_SQ // hd, COL_SK // hd, COL_SV // hd

    def body(q_ref, k_ref, v_ref, oT_ref, lt_ref, vT_ref):
        i = pl.program_id(1)

        @pl.when(i == 0)
        def _():
            _fill_transposed(vT_ref, v_ref, nb, blk)

        key = lax.broadcasted_iota(jnp.int32, (blk, blk), 0)
        qry = lax.broadcasted_iota(jnp.int32, (blk, blk), 1)
        strict = key < qry
        later = (qry > key).astype(MXU_DTYPE)
        qs = [_t(q_ref[:, h * HEAD_DIM:(h + 1) * HEAD_DIM]) for h in range(HP)]

        def tile(j, carry, mask):
            r0 = pl.multiple_of(j * blk, blk)
            hsl = [slice(h * HEAD_DIM, (h + 1) * HEAD_DIM) for h in range(HP)]
            zs = [jnp.dot(k_ref[pl.ds(r0, blk), hsl[h]], qs[h], preferred_element_type=F32) for h in range(HP)]
            lbs = []
            for h in range(HP):
                lb = _log1m_beta(zs[h])
                lbs.append(lb if mask is None else jnp.where(mask, lb, 0.0))
            sums = [_dot01_left(later, lbs[h]) for h in range(HP)]
            probs = []
            for h in range(HP):
                lt_ref[h, 0, j] = carry[h][0]
                a = jnp.exp(zs[h] + lbs[h] + sums[h] + carry[h][0])
                probs.append((a if mask is None else jnp.where(mask, a, 0.0)).astype(MXU_DTYPE))
            pvs = [jnp.dot(vT_ref[j, hsl[h], :], probs[h], preferred_element_type=F32) for h in range(HP)]
            return tuple((carry[h][0] + jnp.sum(lbs[h], axis=0, keepdims=True), carry[h][1] + pvs[h]) for h in range(HP))

        init = tuple((jnp.zeros((1, blk), F32), jnp.zeros((HEAD_DIM, blk), F32)) for _ in range(HP))
        carry = tile(i, init, strict)
        carry = lax.fori_loop(0, i, lambda jj, c: tile(i - 1 - jj, c, None), carry)
        for h in range(HP):
            oT_ref[h * HEAD_DIM:(h + 1) * HEAD_DIM, :] = carry[h][1]

    return pl.pallas_call(
        body, name=name, grid=(4 // HP, nb),
        in_specs=[pl.BlockSpec((blk, hd), lambda p, i: (i, qcb + p)),
                  pl.BlockSpec((S, hd), lambda p, i: (0, kcb + p)),
                  pl.BlockSpec((S, hd), lambda p, i: (0, vcb + p))],
        out_specs=[pl.BlockSpec((hd, blk), lambda p, i: (p, i)),
                   pl.BlockSpec((HP, 1, nb, 1, blk), lambda p, i: (p, i, 0, 0, 0))],
        out_shape=[jax.ShapeDtypeStruct((GROUP_WIDTH, S), F32), jax.ShapeDtypeStruct((4, nb, nb, 1, blk), F32)],
        scratch_shapes=[pltpu.VMEM((nb, hd, blk), MXU_DTYPE)],
        compiler_params=_cparams("arbitrary", "arbitrary"),
    )(h_att, h_att, h_att)


def _sb_bwd_t(h_att, dmix, dmixT, later_sums, *, dcb, qscale, blk, name):
    S = h_att.shape[0]
    nb = S // blk
    hd = HP * HEAD_DIM
    qcb, kcb, vcb = COL_SQ // hd, COL_SK // hd, COL_SV // hd
    dcr = dcb * 128 // hd

    def body(q_ref, k_ref, v_ref, do_ref, doT_ref, lt_ref, dqT_ref, dk_ref, dv_ref, kT_ref):
        i = pl.program_id(1)

        @pl.when(i == 0)
        def _():
            dk_ref[...] = jnp.zeros_like(dk_ref)
            dv_ref[...] = jnp.zeros_like(dv_ref)
            _fill_transposed(kT_ref, k_ref, nb, blk)

        key = lax.broadcasted_iota(jnp.int32, (blk, blk), 0)
        qry = lax.broadcasted_iota(jnp.int32, (blk, blk), 1)
        strict = key < qry
        later = (qry > key).astype(MXU_DTYPE)
        before = (qry < key).astype(MXU_DTYPE)
        per_head = []
        for h in range(HP):
            hs = slice(h * HEAD_DIM, (h + 1) * HEAD_DIM)
            per_head.append(dict(qT=_t(q_ref[:, hs]), q=q_ref[:, hs], doT=doT_ref[hs, :].astype(MXU_DTYPE),
                                 do=do_ref[:, hs].astype(MXU_DTYPE)))

        def tile(j, carry, mask):
            r0 = pl.multiple_of(j * blk, blk)
            rows = pl.ds(r0, blk)
            hsl = [slice(h * HEAD_DIM, (h + 1) * HEAD_DIM) for h in range(HP)]
            zs = [jnp.dot(k_ref[rows, hsl[h]], per_head[h]["qT"], preferred_element_type=F32) for h in range(HP)]
            das = [jnp.dot(v_ref[rows, hsl[h]], per_head[h]["doT"], preferred_element_type=F32) for h in range(HP)]
            lbs = []
            for h in range(HP):
                lb = _log1m_beta(zs[h])
                lbs.append(lb if mask is None else jnp.where(mask, lb, 0.0))
            sums = [_dot01_left(later, lbs[h]) for h in range(HP)]
            probs, gs = [], []
            for h in range(HP):
                a = jnp.exp(zs[h] + lbs[h] + sums[h] + lt_ref[h, 0, j])
                a = a if mask is None else jnp.where(mask, a, 0.0)
                gs.append(das[h] * a)
                probs.append(a.astype(MXU_DTYPE))
            for h in range(HP):
                dv_ref[rows, hsl[h]] += jnp.dot(probs[h], per_head[h]["do"], preferred_element_type=F32)
            es = [_dot01_left(before, gs[h]) for h in range(HP)]
            dzs = []
            for h in range(HP):
                dz = gs[h] * jnp.exp(lbs[h]) - (carry[h][0] + es[h]) * jnp.exp(zs[h] + lbs[h])
                dzs.append((dz if mask is None else jnp.where(mask, dz, 0.0)).astype(MXU_DTYPE))
            for h in range(HP):
                dk_ref[rows, hsl[h]] += jnp.dot(dzs[h], per_head[h]["q"], preferred_element_type=F32)
            return tuple((carry[h][0] + jnp.sum(gs[h], axis=0, keepdims=True),
                          carry[h][1] + jnp.dot(kT_ref[j, hsl[h], :], dzs[h], preferred_element_type=F32))
                         for h in range(HP))

        init = tuple((jnp.zeros((1, blk), F32), jnp.zeros((HEAD_DIM, blk), F32)) for _ in range(HP))
        carry = lax.fori_loop(0, i, lambda j, c: tile(j, c, None), init)
        carry = tile(i, carry, strict)
        for h in range(HP):
            dqT_ref[h * HEAD_DIM:(h + 1) * HEAD_DIM, :] = carry[h][1] * qscale

    return pl.pallas_call(
        body, name=name, grid=(4 // HP, nb),
        in_specs=[pl.BlockSpec((blk, hd), lambda p, i: (i, qcb + p)),
                  pl.BlockSpec((S, hd), lambda p, i: (0, kcb + p)),
                  pl.BlockSpec((S, hd), lambda p, i: (0, vcb + p)),
                  pl.BlockSpec((blk, hd), lambda p, i: (i, dcr + p)),
                  pl.BlockSpec((hd, blk), lambda p, i: (dcr + p, i)),
                  pl.BlockSpec((HP, 1, nb, 1, blk), lambda p, i: (p, i, 0, 0, 0))],
        out_specs=[pl.BlockSpec((hd, blk), lambda p, i: (p, i)),
                   pl.BlockSpec((S, hd), lambda p, i: (0, p)),
                   pl.BlockSpec((S, hd), lambda p, i: (0, p))],
        out_shape=[jax.ShapeDtypeStruct((GROUP_WIDTH, S), F32), jax.ShapeDtypeStruct((S, GROUP_WIDTH), F32),
                   jax.ShapeDtypeStruct((S, GROUP_WIDTH), F32)],
        scratch_shapes=[pltpu.VMEM((nb, hd, blk), MXU_DTYPE)],
        compiler_params=_cparams("arbitrary", "arbitrary"),
    )(h_att, h_att, h_att, dmix, dmixT, later_sums)


SWA_SUB = 4


def _swa_sub_blocks(S):
    return min(SWA_SUB, S // WINDOW)


def _swa_tiles(n, sub):
    tiles = []
    for b in range(sub):
        start = pl.multiple_of(jnp.maximum(n * sub + b - 1, 0) * WINDOW, WINDOW)
        tiles += [(b, h, slice(b * WINDOW, (b + 1) * WINDOW), start) for h in range(4)]
    return tiles


def _swa_scores(q_ref, k_ref, n, sub, tile):
    b, h, qrows, start = tile
    g = h // 2
    kb = k_ref[pl.ds(start, 2 * WINDOW), g * HEAD_DIM:(g + 1) * HEAD_DIM]
    s = lax.dot_general(q_ref[qrows, h * HEAD_DIM:(h + 1) * HEAD_DIM], kb, NT,
                        preferred_element_type=F32) * (HEAD_DIM ** -0.5)
    dist = ((n * sub + b) * WINDOW + lax.broadcasted_iota(jnp.int32, (WINDOW, 2 * WINDOW), 0)
            - start - lax.broadcasted_iota(jnp.int32, (WINDOW, 2 * WINDOW), 1))
    s = s - SWA_SLOPES[h] * dist.astype(F32)
    valid = (dist >= 0) & (dist < WINDOW)
    return jnp.where(valid, s, NEG_INF), kb


def _swa_fwd(h_att, sinks, *, name):
    S = h_att.shape[0]
    sub = _swa_sub_blocks(S)
    rows = sub * WINDOW
    qcb, kcb, vcb = COL_WQ // 256, COL_WK // 128, COL_WV // 128

    def body(sink_ref, q_ref, k_ref, v_ref, o_ref, lse_ref):
        n = pl.program_id(0)
        tiles = _swa_tiles(n, sub)
        scores = [_swa_scores(q_ref, k_ref, n, sub, t)[0] for t in tiles]
        probs = []
        for (b, h, qrows, start), s in zip(tiles, scores):
            sink = sink_ref[h]
            m = jnp.maximum(jnp.max(s, axis=1, keepdims=True), sink)
            e = jnp.exp(s - m)
            den = jnp.sum(e, axis=1, keepdims=True) + jnp.exp(sink - m)
            probs.append((e / den).astype(MXU_DTYPE))
            lse_ref[h, qrows] = m + jnp.log(den)
        for (b, h, qrows, start), p in zip(tiles, probs):
            vb = v_ref[pl.ds(start, 2 * WINDOW), (h // 2) * HEAD_DIM:(h // 2 + 1) * HEAD_DIM]
            o_ref[qrows, h * HEAD_DIM:(h + 1) * HEAD_DIM] = jnp.dot(p, vb, preferred_element_type=F32)

    return pl.pallas_call(
        body, name=name, grid=(S // rows,),
        in_specs=[pl.BlockSpec(memory_space=pltpu.SMEM),
                  pl.BlockSpec((rows, 256), lambda n: (n, qcb)),
                  pl.BlockSpec((S, 128), lambda n: (0, kcb)),
                  pl.BlockSpec((S, 128), lambda n: (0, vcb))],
        out_specs=[pl.BlockSpec((rows, 256), lambda n: (n, 0)), pl.BlockSpec((4, rows, 1), lambda n: (0, n, 0))],
        out_shape=[jax.ShapeDtypeStruct((S, GROUP_WIDTH), F32), jax.ShapeDtypeStruct((4, S, 1), F32)],
        compiler_params=_cparams("arbitrary"),
    )(sinks, h_att, h_att, h_att)


def _swa_bwd(h_att, sinks, dmix, o_arr, lse, *, dcb, name):
    S = h_att.shape[0]
    sub = _swa_sub_blocks(S)
    rows = sub * WINDOW
    qcb, kcb, vcb = COL_WQ // 256, COL_WK // 128, COL_WV // 128

    def body(sink_ref, q_ref, k_ref, v_ref, do_ref, o_ref, lse_ref, dq_ref, dk_ref, dv_ref, dsink_ref):
        n = pl.program_id(0)

        @pl.when(n == 0)
        def _():
            dk_ref[...] = jnp.zeros_like(dk_ref)
            dv_ref[...] = jnp.zeros_like(dv_ref)
            dsink_ref[...] = jnp.zeros_like(dsink_ref)

        tiles = _swa_tiles(n, sub)
        hsl = [slice(h * HEAD_DIM, (h + 1) * HEAD_DIM) for h in range(4)]
        gsl = [slice(g * HEAD_DIM, (g + 1) * HEAD_DIM) for g in range(2)]
        scale = HEAD_DIM ** -0.5
        sk = [_swa_scores(q_ref, k_ref, n, sub, t) for t in tiles]
        dobs = [do_ref[qrows, hsl[h]].astype(MXU_DTYPE) for b, h, qrows, start in tiles]
        dps = [lax.dot_general(dob, v_ref[pl.ds(start, 2 * WINDOW), gsl[h // 2]], NT, preferred_element_type=F32)
               for (b, h, qrows, start), dob in zip(tiles, dobs)]
        prs, dss = [], []
        for t, (b, h, qrows, start) in enumerate(tiles):
            lse_h = lse_ref[h, qrows]
            pr = jnp.exp(sk[t][0] - lse_h)
            delta = jnp.sum(do_ref[qrows, hsl[h]] * o_ref[qrows, hsl[h]], axis=1, keepdims=True)
            dss.append((pr * (dps[t] - delta)).astype(MXU_DTYPE))
            prs.append(pr.astype(MXU_DTYPE))
            dsink_ref[h:h + 1, :] += jnp.zeros((1, 128), F32) - jnp.sum(jnp.exp(sink_ref[h] - lse_h) * delta)
        for t, (b, h, qrows, start) in enumerate(tiles):
            dq_ref[qrows, hsl[h]] = jnp.dot(dss[t], sk[t][1], preferred_element_type=F32) * scale
        for b in range(sub):
            for g in range(2):
                t0, t1 = 4 * b + 2 * g, 4 * b + 2 * g + 1
                qrows, krows = tiles[t0][2], pl.ds(tiles[t0][3], 2 * WINDOW)
                dk_ref[krows, gsl[g]] += (
                    lax.dot_general(dss[t0], q_ref[qrows, hsl[2 * g]], TN, preferred_element_type=F32)
                    + lax.dot_general(dss[t1], q_ref[qrows, hsl[2 * g + 1]], TN, preferred_element_type=F32)) * scale
                dv_ref[krows, gsl[g]] += (lax.dot_general(prs[t0], dobs[t0], TN, preferred_element_type=F32)
                                          + lax.dot_general(prs[t1], dobs[t1], TN, preferred_element_type=F32))

    return pl.pallas_call(
        body, name=name, grid=(S // rows,),
        in_specs=[pl.BlockSpec(memory_space=pltpu.SMEM),
                  pl.BlockSpec((rows, 256), lambda n: (n, qcb)),
                  pl.BlockSpec((S, 128), lambda n: (0, kcb)),
                  pl.BlockSpec((S, 128), lambda n: (0, vcb)),
                  pl.BlockSpec((rows, 256), lambda n: (n, dcb)),
                  pl.BlockSpec((rows, 256), lambda n: (n, 0)),
                  pl.BlockSpec((4, rows, 1), lambda n: (0, n, 0))],
        out_specs=[pl.BlockSpec((rows, 256), lambda n: (n, 0)),
                   pl.BlockSpec((S, 128), lambda n: (0, 0)),
                   pl.BlockSpec((S, 128), lambda n: (0, 0)),
                   pl.BlockSpec((4, 128), lambda n: (0, 0))],
        out_shape=[jax.ShapeDtypeStruct((S, GROUP_WIDTH), F32), jax.ShapeDtypeStruct((S, 128), F32),
                   jax.ShapeDtypeStruct((S, 128), F32), jax.ShapeDtypeStruct((4, 128), F32)],
        compiler_params=_cparams("arbitrary"),
    )(sinks, h_att, h_att, h_att, dmix, o_arr, lse)


def _tri(n, incl, upper):
    r = lax.broadcasted_iota(jnp.int32, (n, n), 0)
    c = lax.broadcasted_iota(jnp.int32, (n, n), 1)
    if upper:
        m = (r <= c) if incl else (r < c)
    else:
        m = (r >= c) if incl else (r > c)
    return m.astype(MXU_DTYPE)


def _fox_gate_fwd(fg, b_f, *, name):
    _, R, _ = fg.shape

    def body(b_ref, fg_ref, pos_ref, neg_ref):
        up_incl = _tri(128, True, True)
        ones = jnp.ones((128, 128), MXU_DTYPE)
        for h in range(4):
            z = fg_ref[h] + b_ref[h]
            logf = jnp.minimum(z, 0.0) - jnp.log(1.0 + jnp.exp(-jnp.abs(z)))
            within = _dot01(logf, up_incl, parts=3)
            totals = _dot01(logf, ones, parts=3)
            rem = within + _rows_other(totals, R, after=False)
            for part in range(3):
                piece = rem.astype(MXU_DTYPE)
                rem = rem - piece.astype(F32)
                pos_ref[h, part] = piece
                neg_ref[h, part] = -piece

    shape = (4, 3) + fg.shape[1:]
    return pl.pallas_call(
        body, name=name,
        in_specs=[pl.BlockSpec(memory_space=pltpu.SMEM), pl.BlockSpec(memory_space=pltpu.VMEM)],
        out_specs=[pl.BlockSpec(memory_space=pltpu.VMEM)] * 2,
        out_shape=[jax.ShapeDtypeStruct(shape, MXU_DTYPE)] * 2,
    )(b_f, fg)


def _rows_other(totals, n, after):
    r = lax.broadcasted_iota(jnp.int32, (n, n), 0)
    c = lax.broadcasted_iota(jnp.int32, (n, n), 1)
    m = ((c > r) if after else (c < r)).astype(MXU_DTYPE)
    acc = None
    rem = totals
    for _ in range(3):
        part = rem.astype(MXU_DTYPE)
        rem = rem - part.astype(F32)
        t = jnp.dot(m, part, preferred_element_type=F32)
        acc = t if acc is None else acc + t
    return acc


def _fox_gate_bwd(fg, b_f, dcum_k, dcum_q, *, q_unscale, name):
    _, R, _ = fg.shape

    def body(b_ref, fg_ref, dck_ref, dcq_ref, dfg_ref, db_ref):
        low_incl = _tri(128, True, False)
        ones = jnp.ones((128, 128), MXU_DTYPE)
        for h in range(4):
            dc = dcq_ref[h] * q_unscale - dck_ref[h]
            dlogf = _dot01(dc, low_incl, parts=3) + _rows_other(_dot01(dc, ones, parts=3), R, after=True)
            z = fg_ref[h] + b_ref[h]
            dz = dlogf * jnp.exp(jnp.minimum(-z, 0.0) - jnp.log(1.0 + jnp.exp(-jnp.abs(z))))
            dfg_ref[h] = dz
            db_ref[h:h + 1, :] = jnp.zeros((1, 128), F32) + jnp.sum(dz)

    return pl.pallas_call(
        body, name=name,
        in_specs=[pl.BlockSpec(memory_space=pltpu.SMEM)] + [pl.BlockSpec(memory_space=pltpu.VMEM)] * 3,
        out_specs=[pl.BlockSpec(memory_space=pltpu.VMEM), pl.BlockSpec(memory_space=pltpu.VMEM)],
        out_shape=[jax.ShapeDtypeStruct(fg.shape, F32), jax.ShapeDtypeStruct((4, 128), F32)],
    )(b_f, fg, dcum_k, dcum_q)


def _rope_rot(transpose):
    r = lax.broadcasted_iota(jnp.int32, (MLA_PAD, MLA_PAD), 0)
    c = lax.broadcasted_iota(jnp.int32, (MLA_PAD, MLA_PAD), 1)
    if transpose:
        r, c = c, r
    half = MLA_ROPE // 2
    lo, mid, hi = HEAD_DIM, HEAD_DIM + half, HEAD_DIM + MLA_ROPE
    minus = (c >= lo) & (c < mid) & (r == c + half)
    plus = (c >= mid) & (c < hi) & (r == c - half)
    return jnp.where(plus, 1.0, jnp.where(minus, -1.0, 0.0)).astype(MXU_DTYPE)


def _rope_lanes():
    lane = lax.broadcasted_iota(jnp.int32, (1, MLA_PAD), 1)
    return ((lane >= HEAD_DIM) & (lane < HEAD_DIM + MLA_ROPE)).astype(F32)


def _rms(x, g, eps=1e-6):
    r = lax.rsqrt(jnp.mean(x * x, axis=-1, keepdims=True) + eps)
    return x * r * g, r


def _rms_bwd(dy, x, r, g):
    xh = x * r
    dxh = dy * g
    dx = r * (dxh - xh * jnp.mean(dxh * xh, axis=-1, keepdims=True))
    return dx, dy * xh


def _mla_prep_fwd(lat, g_q, g_kv, wuq, wuk, wuv, cosm, sinm, *, bs, name):
    S = lat.shape[0]

    def body(lat_ref, gq_ref, gkv_ref, wuq_ref, wuk_ref, wuv_ref, cos_ref, sin_ref, q_ref, k_ref, v_ref):
        rot = _rope_rot(False)
        cosm_, sinm_ = cos_ref[...], sin_ref[...]
        nq, _ = _rms(lat_ref[:, 0:MLA_Q_RANK], gq_ref[...])
        nkv, _ = _rms(lat_ref[:, MLA_Q_RANK:MLA_Q_RANK + MLA_KV_RANK], gkv_ref[...])
        qlat = jnp.dot(nq.astype(MXU_DTYPE), wuq_ref[...], preferred_element_type=F32)
        klat = jnp.dot(nkv.astype(MXU_DTYPE), wuk_ref[...], preferred_element_type=F32)
        v_ref[...] = jnp.dot(nkv.astype(MXU_DTYPE), wuv_ref[...], preferred_element_type=F32).astype(v_ref.dtype)
        krb = lat_ref[:, 384:512]
        kr = krb * (cosm_ * _rope_lanes()) + _dot01(krb, rot, parts=3) * sinm_
        for h in range(4):
            sl = slice(h * MLA_PAD, (h + 1) * MLA_PAD)
            qh = qlat[:, sl]
            q_ref[:, sl] = ((qh * cosm_ + _dot01(qh, rot, parts=3) * sinm_) * (MLA_QK ** -0.5)).astype(q_ref.dtype)
            k_ref[:, sl] = (klat[:, sl] + kr).astype(k_ref.dtype)

    full = lambda a: pl.BlockSpec(a.shape, lambda i: (0,) * a.ndim)
    return pl.pallas_call(
        body, name=name, grid=(S // bs,),
        in_specs=[pl.BlockSpec((bs, LAT_W), lambda i: (i, 0)), full(g_q), full(g_kv), full(wuq), full(wuk), full(wuv),
                  pl.BlockSpec((bs, MLA_PAD), lambda i: (i, 0)), pl.BlockSpec((bs, MLA_PAD), lambda i: (i, 0))],
        out_specs=[pl.BlockSpec((bs, 512), lambda i: (i, 0)), pl.BlockSpec((bs, 512), lambda i: (i, 0)),
                   pl.BlockSpec((bs, 256), lambda i: (i, 0))],
        out_shape=[jax.ShapeDtypeStruct((S, 512), MXU_DTYPE), jax.ShapeDtypeStruct((S, 512), MXU_DTYPE),
                   jax.ShapeDtypeStruct((S, 256), MXU_DTYPE)],
        compiler_params=_cparams("parallel"),
    )(lat, g_q, g_kv, wuq, wuk, wuv, cosm, sinm)


def _mla_prep_bwd(lat, g_q, g_kv, wuq, wuk, wuv, cosm, sinm, dq, dk, dv, *, bs, name):
    S = lat.shape[0]

    def body(lat_ref, gq_ref, gkv_ref, wuq_ref, wuk_ref, wuv_ref, cos_ref, sin_ref, dq_ref, dk_ref, dv_ref,
             dlat_ref, dwuq_ref, dwuk_ref, dwuv_ref, dgq_ref, dgkv_ref):
        @pl.when(pl.program_id(0) == 0)
        def _():
            for r in (dwuq_ref, dwuk_ref, dwuv_ref, dgq_ref, dgkv_ref):
                r[...] = jnp.zeros_like(r)

        rot_t = _rope_rot(True)
        cosm_, sinm_ = cos_ref[...], sin_ref[...]
        cq = lat_ref[:, 0:MLA_Q_RANK]
        ckv = lat_ref[:, MLA_Q_RANK:MLA_Q_RANK + MLA_KV_RANK]
        nq, rq = _rms(cq, gq_ref[...])
        nkv, rkv = _rms(ckv, gkv_ref[...])
        nqb, nkvb = nq.astype(MXU_DTYPE), nkv.astype(MXU_DTYPE)

        dqlat = []
        dkr = jnp.zeros((bs, MLA_PAD), F32)
        for h in range(4):
            sl = slice(h * MLA_PAD, (h + 1) * MLA_PAD)
            dqh = dq_ref[sl, :].T
            dqlat.append(dqh * cosm_ + _dot01(dqh * sinm_, rot_t, parts=3))
            dkr = dkr + dk_ref[:, sl]
        dqlat = jnp.concatenate(dqlat, axis=1).astype(MXU_DTYPE)
        dkb = dk_ref[...].astype(MXU_DTYPE)
        dvb = dv_ref[...].astype(MXU_DTYPE)

        dnq = lax.dot_general(dqlat, wuq_ref[...], NT, preferred_element_type=F32)
        dnkv = (lax.dot_general(dkb, wuk_ref[...], NT, preferred_element_type=F32)
                + lax.dot_general(dvb, wuv_ref[...], NT, preferred_element_type=F32))
        dwuq_ref[...] += lax.dot_general(nqb, dqlat, TN, preferred_element_type=F32)
        dwuk_ref[...] += lax.dot_general(nkvb, dkb, TN, preferred_element_type=F32)
        dwuv_ref[...] += lax.dot_general(nkvb, dvb, TN, preferred_element_type=F32)
        dcq, tq = _rms_bwd(dnq, cq, rq, gq_ref[...])
        dckv, tkv = _rms_bwd(dnkv, ckv, rkv, gkv_ref[...])
        dgq_ref[...] += jnp.sum(tq, axis=0, keepdims=True)
        dgkv_ref[...] += jnp.sum(tkv, axis=0, keepdims=True)
        dlat_ref[:, 0:MLA_Q_RANK] = dcq.astype(dlat_ref.dtype)
        dlat_ref[:, MLA_Q_RANK:MLA_Q_RANK + MLA_KV_RANK] = dckv.astype(dlat_ref.dtype)
        dkrb = dkr * (cosm_ * _rope_lanes()) + _dot01(dkr * sinm_, rot_t, parts=3)
        dlat_ref[:, 384:512] = dkrb.astype(dlat_ref.dtype)

    full = lambda a: pl.BlockSpec(a.shape, lambda i: (0,) * a.ndim)
    row = lambda w: pl.BlockSpec((bs, w), lambda i: (i, 0))
    acc = lambda *shape: pl.BlockSpec(shape, lambda i: (0,) * len(shape))
    return pl.pallas_call(
        body, name=name, grid=(S // bs,),
        in_specs=[row(LAT_W), full(g_q), full(g_kv), full(wuq), full(wuk), full(wuv), row(MLA_PAD), row(MLA_PAD),
                  pl.BlockSpec((512, bs), lambda i: (0, i)), row(512), row(256)],
        out_specs=[row(512), acc(256, 512), acc(128, 512), acc(128, 256), acc(1, 256), acc(1, 128)],
        out_shape=[jax.ShapeDtypeStruct((S, 512), MXU_DTYPE), jax.ShapeDtypeStruct((256, 512), F32),
                   jax.ShapeDtypeStruct((128, 512), F32), jax.ShapeDtypeStruct((128, 256), F32),
                   jax.ShapeDtypeStruct((1, 256), F32), jax.ShapeDtypeStruct((1, 128), F32)],
        compiler_params=_cparams("arbitrary"),
    )(lat, g_q, g_kv, wuq, wuk, wuv, cosm, sinm, dq, dk, dv)


def _row_spec(bs, w):
    return pl.BlockSpec((bs, w), lambda i: (i, 0))


def _vec_spec(w):
    return pl.BlockSpec((1, w), lambda i: (0, 0))


def _mix_specs(bs):
    return [pl.BlockSpec((GROUP_WIDTH, bs), lambda i: (0, i))] * 3 + [_row_spec(bs, GROUP_WIDTH)]


def _mix_groups(a_ref, b_ref, c_ref, d_ref):
    return [a_ref[...].T, b_ref[...].T, c_ref[...].T, d_ref[...]]


def _gnorm_fwd(outs, g, *, bs, name):
    S = outs[3].shape[0]

    def body(a_ref, b_ref, c_ref, d_ref, g_ref, o_ref, oT_ref):
        for k, x in enumerate(_mix_groups(a_ref, b_ref, c_ref, d_ref)):
            sl = slice(k * GROUP_WIDTH, (k + 1) * GROUP_WIDTH)
            y, _ = _rms(x, g_ref[:, sl])
            o_ref[:, sl] = y.astype(o_ref.dtype)
            oT_ref[sl, :] = y.T.astype(oT_ref.dtype)

    return pl.pallas_call(
        body, name=name, grid=(S // bs,),
        in_specs=_mix_specs(bs) + [_vec_spec(D_MODEL)],
        out_specs=[_row_spec(bs, D_MODEL), pl.BlockSpec((D_MODEL, bs), lambda i: (0, i))],
        out_shape=[jax.ShapeDtypeStruct((S, D_MODEL), MXU_DTYPE), jax.ShapeDtypeStruct((D_MODEL, S), MXU_DTYPE)],
        compiler_params=_cparams("parallel"),
    )(*outs, g)


def _gnorm_bwd(dgn, outs, g, *, bs, name):
    S = dgn.shape[0]

    def body(dgn_ref, a_ref, b_ref, c_ref, d_ref, g_ref, dmix_ref, dmixT_ref, dg_ref):
        @pl.when(pl.program_id(0) == 0)
        def _():
            dg_ref[...] = jnp.zeros_like(dg_ref)

        for k, x in enumerate(_mix_groups(a_ref, b_ref, c_ref, d_ref)):
            sl = slice(k * GROUP_WIDTH, (k + 1) * GROUP_WIDTH)
            _, r = _rms(x, g_ref[:, sl])
            dx, t = _rms_bwd(dgn_ref[:, sl], x, r, g_ref[:, sl])
            dmix_ref[:, sl] = dx
            dmixT_ref[sl, :] = dx.T
            dg_ref[:, sl] += jnp.sum(t, axis=0, keepdims=True)

    return pl.pallas_call(
        body, name=name, grid=(S // bs,),
        in_specs=[_row_spec(bs, D_MODEL)] + _mix_specs(bs) + [_vec_spec(D_MODEL)],
        out_specs=[_row_spec(bs, D_MODEL), pl.BlockSpec((D_MODEL, bs), lambda i: (0, i)), _vec_spec(D_MODEL)],
        out_shape=[jax.ShapeDtypeStruct((S, D_MODEL), F32), jax.ShapeDtypeStruct((D_MODEL, S), F32),
                   jax.ShapeDtypeStruct((1, D_MODEL), F32)],
        compiler_params=_cparams("arbitrary"),
    )(dgn, *outs, g)


def _ln_fwd(u, g, b, *, bs, name):
    S = u.shape[0]

    def body(u_ref, g_ref, b_ref, y_ref, yb_ref, ybT_ref, xh_ref, rs_ref):
        x = u_ref[...]
        mu = jnp.mean(x, axis=-1, keepdims=True)
        xc = x - mu
        rs = lax.rsqrt(jnp.mean(xc * xc, axis=-1, keepdims=True) + 1e-5)
        xh = xc * rs
        y = xh * g_ref[...] + b_ref[...]
        y_ref[...] = y
        yb_ref[...] = y.astype(yb_ref.dtype)
        ybT_ref[...] = y.T.astype(ybT_ref.dtype)
        xh_ref[...] = xh
        rs_ref[...] = rs

    return pl.pallas_call(
        body, name=name, grid=(S // bs,),
        in_specs=[_row_spec(bs, D_MODEL), _vec_spec(D_MODEL), _vec_spec(D_MODEL)],
        out_specs=[_row_spec(bs, D_MODEL), _row_spec(bs, D_MODEL), pl.BlockSpec((D_MODEL, bs), lambda i: (0, i)),
                   _row_spec(bs, D_MODEL), _row_spec(bs, 1)],
        out_shape=[jax.ShapeDtypeStruct((S, D_MODEL), F32), jax.ShapeDtypeStruct((S, D_MODEL), MXU_DTYPE),
                   jax.ShapeDtypeStruct((D_MODEL, S), MXU_DTYPE), jax.ShapeDtypeStruct((S, D_MODEL), F32),
                   jax.ShapeDtypeStruct((S, 1), F32)],
        compiler_params=_cparams("parallel"),
    )(u, g, b)


def _ln_bwd(dy, xh, rs, g, *, bs, name):
    S = dy.shape[0]

    def body(dy_ref, xh_ref, rs_ref, g_ref, du_ref, dub_ref, dg_ref, db_ref):
        @pl.when(pl.program_id(0) == 0)
        def _():
            dg_ref[...] = jnp.zeros_like(dg_ref)
            db_ref[...] = jnp.zeros_like(db_ref)

        dy_, xh_ = dy_ref[...], xh_ref[...]
        dxh = dy_ * g_ref[...]
        du = rs_ref[...] * (dxh - jnp.mean(dxh, axis=-1, keepdims=True)
                            - xh_ * jnp.mean(dxh * xh_, axis=-1, keepdims=True))
        du_ref[...] = du
        dub_ref[...] = du.astype(dub_ref.dtype)
        dg_ref[...] += jnp.sum(dy_ * xh_, axis=0, keepdims=True)
        db_ref[...] += jnp.sum(dy_, axis=0, keepdims=True)

    return pl.pallas_call(
        body, name=name, grid=(S // bs,),
        in_specs=[_row_spec(bs, D_MODEL), _row_spec(bs, D_MODEL), _row_spec(bs, 1), _vec_spec(D_MODEL)],
        out_specs=[_row_spec(bs, D_MODEL), _row_spec(bs, D_MODEL), _vec_spec(D_MODEL), _vec_spec(D_MODEL)],
        out_shape=[jax.ShapeDtypeStruct((S, D_MODEL), F32), jax.ShapeDtypeStruct((S, D_MODEL), MXU_DTYPE),
                   jax.ShapeDtypeStruct((1, D_MODEL), F32), jax.ShapeDtypeStruct((1, D_MODEL), F32)],
        compiler_params=_cparams("arbitrary"),
    )(dy, xh, rs, g)


def _swiglu_fwd(gu, *, bs, name):
    S = gu.shape[0]

    def body(gu_ref, a_ref, aT_ref):
        gt = gu_ref[:, :D_FF]
        a = gt / (1.0 + jnp.exp(-gt)) * gu_ref[:, D_FF:]
        a_ref[...] = a.astype(a_ref.dtype)
        aT_ref[...] = a.T.astype(aT_ref.dtype)

    return pl.pallas_call(
        body, name=name, grid=(S // bs,),
        in_specs=[_row_spec(bs, 2 * D_FF)],
        out_specs=[_row_spec(bs, D_FF), pl.BlockSpec((D_FF, bs), lambda i: (0, i))],
        out_shape=[jax.ShapeDtypeStruct((S, D_FF), MXU_DTYPE), jax.ShapeDtypeStruct((D_FF, S), MXU_DTYPE)],
        compiler_params=_cparams("parallel"),
    )(gu)


def _swiglu_bwd(da, gu, *, bs, name):
    S = gu.shape[0]

    def body(da_ref, gu_ref, dgu_ref):
        gt, da_ = gu_ref[:, :D_FF], da_ref[...]
        sg = 1.0 / (1.0 + jnp.exp(-gt))
        silu = gt * sg
        dgu_ref[:, :D_FF] = (da_ * gu_ref[:, D_FF:] * (sg + silu * (1.0 - sg))).astype(dgu_ref.dtype)
        dgu_ref[:, D_FF:] = (da_ * silu).astype(dgu_ref.dtype)

    return pl.pallas_call(
        body, name=name, grid=(S // bs,),
        in_specs=[_row_spec(bs, D_FF), _row_spec(bs, 2 * D_FF)],
        out_specs=_row_spec(bs, 2 * D_FF), out_shape=jax.ShapeDtypeStruct((S, 2 * D_FF), MXU_DTYPE),
        compiler_params=_cparams("parallel"),
    )(da, gu)


def _loss_head(y, target, *, bs, name):
    S = y.shape[0]

    def body(y_ref, t_ref, dy_ref, loss_ref):
        @pl.when(pl.program_id(0) == 0)
        def _():
            loss_ref[...] = jnp.zeros_like(loss_ref)

        e = y_ref[...] - t_ref[...]
        dy_ref[...] = e * (1.0 / D_MODEL)
        per_tok = jnp.mean(e * e, axis=-1, keepdims=True)
        loss_ref[...] += 0.5 * jnp.sum(per_tok, axis=0, keepdims=True)

    return pl.pallas_call(
        body, name=name, grid=(S // bs,),
        in_specs=[_row_spec(bs, D_MODEL), _row_spec(bs, D_MODEL)],
        out_specs=[_row_spec(bs, D_MODEL), pl.BlockSpec((1, 1), lambda i: (0, 0))],
        out_shape=[jax.ShapeDtypeStruct((S, D_MODEL), F32), jax.ShapeDtypeStruct((1, 1), F32)],
        compiler_params=_cparams("arbitrary"),
    )(y, target)


def _blk(n, target):
    if n <= target:
        return n
    best = None
    for b in range(128, target + 1, 128):
        if n % b == 0:
            best = b
    assert best is not None, n
    return best


def _rope_tables(S):
    pos = jnp.arange(S, dtype=F32)
    inv = ROPE_THETA ** (-jnp.arange(0, MLA_ROPE, 2, dtype=F32) / MLA_ROPE)
    ang = pos[:, None] * inv[None, :]
    cos, sin = jnp.cos(ang), jnp.sin(ang)
    one, zero, pad = jnp.ones((S, HEAD_DIM), F32), jnp.zeros((S, HEAD_DIM), F32), jnp.zeros((S, MLA_PAD - MLA_QK), F32)
    return jnp.concatenate([one, cos, cos, pad], axis=1), jnp.concatenate([zero, sin, sin, pad], axis=1)


def _prep_weights_a(w_in, w_uq, w_ukv):
    z = lambda n: jnp.zeros((D_MODEL, n), w_in.dtype)
    win_a = jnp.concatenate([w_in[:, 0:768], w_in[:, 1188:2468]], axis=1)
    win_l = jnp.concatenate([w_in[:, 772:1156], z(64), w_in[:, 1156:1188], z(32), w_in[:, 768:772], z(124)], axis=1)
    kv = w_ukv.reshape(MLA_KV_RANK, 4, 2 * HEAD_DIM)
    return dict(
        win_a=win_a, win_l=win_l, win_p=jnp.concatenate([win_a, win_l], axis=1),
        wuq=jnp.pad(w_uq.reshape(MLA_Q_RANK, 4, MLA_QK), ((0, 0), (0, 0), (0, MLA_PAD - MLA_QK))).reshape(MLA_Q_RANK, 512),
        wuk=jnp.pad(kv[:, :, :HEAD_DIM], ((0, 0), (0, 0), (0, HEAD_DIM))).reshape(MLA_KV_RANK, 512),
        wuv=kv[:, :, HEAD_DIM:].reshape(MLA_KV_RANK, 256))


def _unprep_grads(dwin_p, dwuq, dwuk, dwuv, dwo, dwgu, dwd):
    dw_in = jnp.concatenate([dwin_p[:, 0:768], dwin_p[:, 2560:2564], dwin_p[:, 2048:2432], dwin_p[:, 2496:2528],
                             dwin_p[:, 768:2048]], axis=1)
    dw_uq = dwuq.reshape(MLA_Q_RANK, 4, MLA_PAD)[:, :, :MLA_QK].reshape(MLA_Q_RANK, 4 * MLA_QK)
    dw_ukv = jnp.concatenate([dwuk.reshape(MLA_KV_RANK, 4, MLA_PAD)[:, :, :HEAD_DIM],
                              dwuv.reshape(MLA_KV_RANK, 4, HEAD_DIM)], axis=2).reshape(MLA_KV_RANK, 512)
    return dict(w_in=dw_in, mla_w_uq=dw_uq, mla_w_ukv=dw_ukv, w_o=dwo, w_gate=dwgu[:, :D_FF], w_up=dwgu[:, D_FF:],
                w_down=dwd)


def _layer_fwd(l, x, xb, xbT, W, P, tabs, blk, late_weights=None):
    S = x.shape[0]
    nb = S // blk
    n = lambda s: f"l{l}_{s}"
    bs = min(512, S)
    h_att = _mm(xb, W["win_a"], name=n("in_att"), out_dtype=MXU_DTYPE, bm=1024, bn=1024, bk=1024, colscale=Q_COLSCALE)
    lat = _mm(xb, W["win_l"], name=n("in_lat"), bm=2048, bn=LAT_W, bk=1024)
    fg = lat[:, 512:516].T.reshape(4, S // 128, 128)
    cpos, cneg = _fox_gate_fwd(fg, P["fox_b_f"], name=n("fox_gate"))
    one3 = jnp.ones((S, 4, 3), MXU_DTYPE)
    zpad = jnp.zeros((S, 4, MLA_PAD - HEAD_DIM - 6), MXU_DTYPE)
    per_tok = lambda parts: parts.reshape(4, 3, S).transpose(2, 0, 1)
    q_f = jnp.concatenate([h_att[:, COL_FQ:COL_FQ + 256].reshape(S, 4, HEAD_DIM), per_tok(cpos), one3, zpad],
                          axis=2).reshape(S, 4 * MLA_PAD)
    k_f = jnp.concatenate([h_att[:, COL_FK:COL_FK + 256].reshape(S, 4, HEAD_DIM), one3, per_tok(cneg), zpad],
                          axis=2).reshape(S, 4 * MLA_PAD)
    v_f = h_att[:, COL_FV:COL_FV + 256]
    oT_a, lse_a = _smax_fwd_t(q_f, k_f, v_f, dk=MLA_PAD, blk=blk, name=n("fox_fwd"))
    q_m, k_m, v_m = _mla_prep_fwd(lat, P["mla_g_q"], P["mla_g_kv"], W["wuq"], W["wuk"], W["wuv"], *tabs,
                                  bs=bs, name=n("mla_prep"))
    oT_b, lse_b = _smax_fwd_t(q_m, k_m, v_m, dk=MLA_PAD, blk=blk, name=n("mla_fwd"))
    bsb = min(BLK_STICK, S)
    oT_c, lt_c = _sb_fwd_t(h_att, blk=bsb, name=n("sb_fwd"))
    out_d, lse_d = _swa_fwd(h_att, P["swa_sinks"], name=n("swa_fwd"))
    outs = (oT_a, oT_b, oT_c, out_d)
    gn, gnT = _gnorm_fwd(outs, P["mix_g"], bs=bs, name=n("gnorm"))
    if late_weights is not None:
        W = dict(W, **late_weights(gn))
    u1 = _mm(gn, W["w_o"], name=n("out_proj"), bm=1024, bn=1024, bk=1024, resid=x, alpha=ALPHA)
    x1, x1b, x1bT, xh1, rs1 = _ln_fwd(u1, P["ln1_g"], P["ln1_b"], bs=bs, name=n("ln1"))
    gu = _mm(x1b, W["wgu"], name=n("gate_up"), bm=2048, bn=512, bk=1024)
    a, aT = _swiglu_fwd(gu, bs=min(256, S), name=n("swiglu"))
    u2 = _mm(a, W["w_down"], name=n("down"), bm=1024, bn=1024, bk=_blk(D_FF, 1408), resid=x1, alpha=ALPHA)
    x2, x2b, x2bT, xh2, rs2 = _ln_fwd(u2, P["ln2_g"], P["ln2_b"], bs=bs, name=n("ln2"))
    saved = dict(xbT=xbT, gnT=gnT, x1bT=x1bT, h_att=h_att, lat=lat, fg=fg, outs=outs, oT_a=oT_a, oT_b=oT_b, q_f=q_f, k_f=k_f, v_f=v_f,
                 lse_a=lse_a, lse_b=lse_b, lse_d=lse_d, lt_c=lt_c, q_m=q_m, k_m=k_m, v_m=v_m,
                 xh1=xh1, rs1=rs1, gu=gu, aT=aT, xh2=xh2, rs2=rs2)
    return x2, x2b, x2bT, saved, W


def _layer_bwd(l, dx2, sv, W, P, tabs, blk, send_early=None, send_late=None):
    S = dx2.shape[0]
    n = lambda s: f"l{l}_{s}"
    bs = min(512, S)
    h_att = sv["h_att"]
    du2, du2b, dg2, db2 = _ln_bwd(dx2, sv["xh2"], sv["rs2"], P["ln2_g"], bs=bs, name=n("ln2_bwd"))
    da = _mm(du2b, W["w_down"], name=n("down_dx"), tb=True, bm=1024, bn=_blk(D_FF, 1408), bk=1024)
    dwd = _mm(sv["aT"], du2b, name=n("down_dw"), bm=_blk(D_FF, 1408), bn=1024, bk=1024)
    dgu = _swiglu_bwd(da, sv["gu"], bs=min(256, S), name=n("swiglu_bwd"))
    dx1 = _mm(dgu, W["wgu"], name=n("gate_up_dx"), tb=True, bm=1024, bn=1024, bk=_blk(2 * D_FF, 1408), resid=du2,
              alpha=ALPHA)
    dwgu = _mm(sv["x1bT"], dgu, name=n("gate_up_dw"), bm=1024, bn=_blk(2 * D_FF, 1408), bk=1024)
    du1, du1b, dg1, db1 = _ln_bwd(dx1, sv["xh1"], sv["rs1"], P["ln1_g"], bs=bs, name=n("ln1_bwd"))
    dgn = _mm(du1b, W["w_o"], name=n("out_proj_dx"), tb=True, bm=1024, bn=1024, bk=1024)
    dwo = _mm(sv["gnT"], du1b, name=n("out_proj_dw"), bm=1024, bn=1024, bk=1024)
    mix_g = P["mix_g"]
    if send_early is not None:
        mix_g = mix_g + send_early(dict(w_o=dwo, w_gate=dwgu[:, :D_FF], w_up=dwgu[:, D_FF:], w_down=dwd))[0, 0]
    dmix, dmixT, dmixg = _gnorm_bwd(dgn, sv["outs"], mix_g, bs=bs, name=n("gnorm_bwd"))
    q_f, k_f = sv["q_f"], sv["k_f"]
    dqT_a, dk_a, dva = _smax_bwd_t(q_f, k_f, sv["v_f"], dmix, dmixT, sv["oT_a"], sv["lse_a"], dk=MLA_PAD, dcb=0,
                                   qscale=HEAD_DIM ** -0.5, blk=blk, name=n("fox_bwd"))
    dq_a, dk_a = dqT_a.T.reshape(S, 4, MLA_PAD), dk_a.reshape(S, 4, MLA_PAD)
    dqa, dka = dq_a[:, :, :HEAD_DIM].reshape(S, 256), dk_a[:, :, :HEAD_DIM].reshape(S, 256)
    dcq = dq_a[:, :, HEAD_DIM].T.reshape(4, S // 128, 128)
    dck = dk_a[:, :, HEAD_DIM + 3].T.reshape(4, S // 128, 128)
    q_m, k_m = sv["q_m"], sv["k_m"]
    dqT_b, dkb, dvb = _smax_bwd_t(q_m, k_m, sv["v_m"], dmix, dmixT, sv["oT_b"], sv["lse_b"], dk=MLA_PAD, dcb=2,
                                  qscale=MLA_QK ** -0.5, blk=blk, name=n("mla_bwd"))
    dqT_c, dkc, dvc = _sb_bwd_t(h_att, dmix, dmixT, sv["lt_c"], dcb=4, qscale=HEAD_DIM ** -0.5,
                                blk=min(BLK_STICK, S), name=n("sb_bwd"))
    dqc = dqT_c.T
    dqd, dkd, dvd, dsink = _swa_bwd(h_att, P["swa_sinks"], dmix, sv["outs"][3], sv["lse_d"], dcb=3, name=n("swa_bwd"))
    dlat, dwuq, dwuk, dwuv, dgq, dgkv = _mla_prep_bwd(
        sv["lat"], P["mla_g_q"], P["mla_g_kv"], W["wuq"], W["wuk"], W["wuv"], *tabs, dqT_b, dkb, dvb,
        bs=bs, name=n("mla_prep_bwd"))
    dfg, dbf = _fox_gate_bwd(sv["fg"], P["fox_b_f"], dck, dcq, q_unscale=HEAD_DIM ** 0.5, name=n("fox_gate_bwd"))
    dfg_blk = jnp.pad(dfg.reshape(4, S).T, ((0, 0), (0, 124)))
    dh = jnp.concatenate([t.astype(MXU_DTYPE) for t in (dqa, dka, dva, dqc, dkc, dvc, dqd, dkd, dvd, dlat, dfg_blk)], axis=1)
    dwin_p = _mm(sv["xbT"], dh, name=n("in_dw"), bm=1024, bn=_blk(PERM_W, 1024), bk=1024)
    grads = _unprep_grads(dwin_p, dwuq, dwuk, dwuv, dwo, dwgu, dwd)
    grads.update(fox_b_f=dbf[:, 0], mla_g_q=dgq[0], mla_g_kv=dgkv[0], swa_sinks=dsink[:, 0], mix_g=dmixg[0],
                 ln1_g=dg1[0], ln1_b=db1[0], ln2_g=dg2[0], ln2_b=db2[0])
    started = send_late(grads) if send_late is not None else None
    dx = _mm(dh, W["win_p"], name=n("in_dx"), tb=True, bm=1024, bn=1024, bk=_blk(PERM_W, 1024), resid=du1, alpha=ALPHA,
             after=started)
    return dx, grads


BIG = ("w_in", "mla_w_uq", "mla_w_ukv", "w_o", "w_gate", "w_up", "w_down")
SMALL = ("fox_b_f", "mla_g_q", "mla_g_kv", "swa_sinks", "mix_g", "ln1_g", "ln1_b", "ln2_g", "ln2_b")
SHARD_AXIS = dict(w_in=2, mla_w_uq=2, mla_w_ukv=2, w_o=1, w_gate=2, w_up=2, w_down=1)
N_CHIPS = 4
ANY = pl.BlockSpec(memory_space=pl.ANY)


HBM = pl.BlockSpec(memory_space=pltpu.HBM)
SEM = pl.BlockSpec(memory_space=pltpu.SEMAPHORE)
N_PEER_CHIPS = N_CHIPS - 1


def _peer_copies(src_ref, land_ref, sems, scatter):
    x, y, c = lax.axis_index("x"), lax.axis_index("y"), lax.axis_index("c")
    me = 2 * x + y
    out = []
    for r, (px, py) in enumerate([(1 - x, y), (x, 1 - y), (1 - x, 1 - y)]):
        theirs = 2 * px + py
        send = pltpu.make_async_remote_copy(
            src_ref=src_ref.at[theirs] if scatter else src_ref, dst_ref=land_ref.at[me],
            send_sem=sems[2 * r], recv_sem=sems[2 * r + 1], device_id=(px, py, c), device_id_type=MESH)
        arrive = pltpu.make_async_remote_copy(
            src_ref=src_ref.at[me] if scatter else src_ref, dst_ref=land_ref.at[theirs],
            send_sem=sems[2 * r], recv_sem=sems[2 * r + 1], device_id=(px, py, c), device_id_type=MESH)
        out.append((send, arrive))
    return out


def _exchange_start(srcs, *, scatter, name):
    nt = len(srcs)
    ns = 2 * N_PEER_CHIPS * nt
    land_shapes = [s.shape if scatter else (N_CHIPS,) + s.shape for s in srcs]

    def body(*refs):
        src_refs, land_refs, outs = refs[:nt], refs[nt:2 * nt], refs[2 * nt:]
        for t in range(nt):
            for send, _ in _peer_copies(src_refs[t], land_refs[t], outs[6 * t:6 * t + 6], scatter):
                send.start()
        outs[-1][...] = jnp.zeros_like(outs[-1])

    res = pl.pallas_call(
        body, name=name,
        out_shape=(*[pltpu.SemaphoreType.DMA(())] * ns, *[pltpu.HBM(s.shape, s.dtype) for s in srcs],
                   *[pltpu.HBM(ls, s.dtype) for ls, s in zip(land_shapes, srcs)], jax.ShapeDtypeStruct((8, 128), F32)),
        in_specs=(HBM,) * (2 * nt), out_specs=(*[SEM] * ns, *[HBM] * (2 * nt), pl.BlockSpec(memory_space=pltpu.VMEM)),
        input_output_aliases={i: ns + i for i in range(2 * nt)},
        compiler_params=pltpu.CompilerParams(has_side_effects=pltpu.SideEffectType.DATAFLOW_SIDE_EFFECTING),
    )(*[pltpu.with_memory_space_constraint(s, pltpu.HBM) for s in srcs],
      *[pltpu.with_memory_space_constraint(lax.empty(ls, s.dtype), pltpu.HBM) for ls, s in zip(land_shapes, srcs)])
    return dict(sems=res[:ns], srcs=res[ns:ns + nt], lands=res[ns + nt:ns + 2 * nt], token=res[-1])


def _exchange_wait(started, after, *, scatter, name):
    nt = len(started["srcs"])
    ns = 2 * N_PEER_CHIPS * nt

    def body(*refs):
        src_refs, land_refs, sems = refs[:nt], refs[nt:2 * nt], refs[2 * nt:2 * nt + ns]
        for t in range(nt):
            for send, arrive in _peer_copies(src_refs[t], land_refs[t], sems[6 * t:6 * t + 6], scatter):
                send.wait_send()
                arrive.wait_recv()

    both = list(started["srcs"]) + list(started["lands"])
    res = pl.pallas_call(
        body, name=name, out_shape=tuple(pltpu.HBM(a.shape, a.dtype) for a in both),
        in_specs=(*[HBM] * (2 * nt), *[SEM] * ns, ANY), out_specs=(HBM,) * (2 * nt),
        input_output_aliases={i: i for i in range(2 * nt)},
        compiler_params=pltpu.CompilerParams(has_side_effects=pltpu.SideEffectType.DATAFLOW_SIDE_EFFECTING),
    )(*both, *started["sems"], after)
    return res[:nt], res[nt:]


def _core_exchange(tensors, *, name):
    nt = len(tensors)

    def body(*refs):
        ins, outs = refs[:nt], refs[nt:2 * nt]
        send_sems, recv_sems = refs[2 * nt:]
        sibling = (lax.axis_index("x"), lax.axis_index("y"), 1 - lax.axis_index("c"))
        copies = [pltpu.make_async_remote_copy(src_ref=ins[t], dst_ref=outs[t], send_sem=send_sems.at[t],
                                               recv_sem=recv_sems.at[t], device_id=sibling, device_id_type=MESH)
                  for t in range(nt)]
        for cp in copies:
            cp.start()
        for cp in copies:
            cp.wait_recv()
        for cp in copies:
            cp.wait_send()

    return pl.pallas_call(
        body, name=name, in_specs=[ANY] * nt, out_specs=[ANY] * nt,
        out_shape=[jax.ShapeDtypeStruct(t.shape, t.dtype) for t in tensors],
        scratch_shapes=[pltpu.SemaphoreType.DMA((nt,)), pltpu.SemaphoreType.DMA((nt,))],
        compiler_params=pltpu.CompilerParams(has_side_effects=True),
    )(*tensors)


def _all_sum_small(block, *, name):
    R = block.shape[0]
    n_dev = 8

    def body(x_ref, o_ref, slots, send_sems, recv_sems):
        x, y, c = lax.axis_index("x"), lax.axis_index("y"), lax.axis_index("c")
        me = 4 * x + 2 * y + c
        slots[me] = x_ref[...]
        sends, recvs = [], []
        for d in range(1, n_dev):
            px, py, pc = x ^ (d >> 2), y ^ ((d >> 1) & 1), c ^ (d & 1)
            theirs = 4 * px + 2 * py + pc
            sends.append(pltpu.make_async_remote_copy(
                src_ref=x_ref, dst_ref=slots.at[me], send_sem=send_sems.at[d - 1], recv_sem=recv_sems.at[d - 1],
                device_id=(px, py, pc), device_id_type=MESH))
            recvs.append(pltpu.make_async_remote_copy(
                src_ref=x_ref, dst_ref=slots.at[theirs], send_sem=send_sems.at[d - 1], recv_sem=recv_sems.at[d - 1],
                device_id=(px, py, pc), device_id_type=MESH))
        for cp in sends:
            cp.start()
        for cp in recvs:
            cp.wait_recv()
        for cp in sends:
            cp.wait_send()
        total = slots[0]
        for k in range(1, n_dev):
            total = total + slots[k]
        o_ref[...] = total

    return pl.pallas_call(
        body, name=name, in_specs=[pl.BlockSpec(memory_space=pltpu.VMEM)],
        out_specs=pl.BlockSpec(memory_space=pltpu.VMEM), out_shape=jax.ShapeDtypeStruct((R, 128), F32),
        scratch_shapes=[pltpu.VMEM((n_dev, R, 128), F32), pltpu.SemaphoreType.DMA((n_dev - 1,)),
                        pltpu.SemaphoreType.DMA((n_dev - 1,))],
        compiler_params=pltpu.CompilerParams(has_side_effects=True),
    )(block)


def _sum_chips_into(acc, land, own, me, layer, *, br, name):
    _, R, C = land.shape

    def body(me_ref, land_ref, own_ref, acc_ref, o_ref):
        mine = me_ref[0]
        total = None
        for k in range(N_CHIPS):
            part = jnp.where(mine == k, own_ref[...], land_ref[k]).astype(F32)
            total = part if total is None else total + part
        o_ref[0] = total

    return pl.pallas_call(
        body, name=name, grid=(R // br,),
        in_specs=[pl.BlockSpec(memory_space=pltpu.SMEM), pl.BlockSpec((N_CHIPS, br, C), lambda i: (0, i, 0)),
                  pl.BlockSpec((br, C), lambda i: (i, 0)), ANY],
        out_specs=pl.BlockSpec((1, br, C), lambda i: (layer, i, 0)),
        out_shape=jax.ShapeDtypeStruct(acc.shape, F32), input_output_aliases={3: 0},
        compiler_params=_cparams("parallel"),
    )(me, land, own, acc)


def _adamw_math(w, g, m, v):
    m = ADAM_B1 * m + (1.0 - ADAM_B1) * g
    v = ADAM_B2 * v + (1.0 - ADAM_B2) * (g * g)
    m_hat = m / (1.0 - ADAM_B1 ** ADAM_STEP)
    v_hat = v / (1.0 - ADAM_B2 ** ADAM_STEP)
    return -ADAM_LR * (m_hat / (jnp.sqrt(v_hat) + ADAM_EPS) + ADAM_WD * w), m, v


def _adamw(w, m, v, g_a, g_b, *, br, name):
    R, C = w.shape
    two = g_b is not None

    def body(*refs):
        if two:
            w_ref, m_ref, v_ref, ga_ref, gb_ref, g_ref, d_ref, nm_ref, nv_ref = refs
            g = ga_ref[...] + gb_ref[...]
        else:
            w_ref, m_ref, v_ref, ga_ref, g_ref, d_ref, nm_ref, nv_ref = refs
            g = ga_ref[...]
        g_ref[...] = g
        d_ref[...], nm_ref[...], nv_ref[...] = _adamw_math(w_ref[...], g, m_ref[...], v_ref[...])

    spec = pl.BlockSpec((br, C), lambda i: (i, 0))
    args = [w, m, v, g_a] + ([g_b] if two else [])
    return pl.pallas_call(
        body, name=name, grid=(R // br,), in_specs=[spec] * len(args), out_specs=[spec] * 4,
        out_shape=[jax.ShapeDtypeStruct((R, C), F32)] * 4,
        compiler_params=_cparams("parallel"),
    )(*args)


SMALL_ROWS = dict(fox_b_f=1, mla_g_q=2, mla_g_kv=1, swa_sinks=1, mix_g=8, ln1_g=8, ln1_b=8, ln2_g=8, ln2_b=8)
SMALL_ROWS_PER_LAYER = sum(SMALL_ROWS.values())


def _pack_small(vals, extra_rows):
    L = vals[SMALL[0]].shape[0]
    per_layer = []
    for name in SMALL:
        a = vals[name].astype(F32)
        a = jnp.pad(a, ((0, 0), (0, SMALL_ROWS[name] * 128 - a.shape[1])))
        per_layer.append(a.reshape(L, SMALL_ROWS[name], 128))
    out = jnp.concatenate(per_layer, axis=1).reshape(L * SMALL_ROWS_PER_LAYER, 128)
    return jnp.pad(out, ((0, extra_rows), (0, 0)))


def _unpack_small(block, shapes):
    L = shapes[SMALL[0]][0]
    body = block[:L * SMALL_ROWS_PER_LAYER].reshape(L, SMALL_ROWS_PER_LAYER, 128)
    out, r = {}, 0
    for name in SMALL:
        n = shapes[name][1]
        out[name] = body[:, r:r + SMALL_ROWS[name]].reshape(L, SMALL_ROWS[name] * 128)[:, :n]
        r += SMALL_ROWS[name]
    return out


def _to_chips(g, axis):
    L, a, b = g.shape
    if axis == 2:
        return g.reshape(L, a, N_CHIPS, b // N_CHIPS).transpose(2, 0, 1, 3)
    return g.reshape(L, N_CHIPS, a // N_CHIPS, b).transpose(1, 0, 2, 3)


def kernel(x, w_in, fox_b_f, mla_g_q, mla_g_kv, mla_w_uq, mla_w_ukv, swa_sinks, mix_g, w_o, ln1_g, ln1_b, w_gate, w_up, w_down, ln2_g, ln2_b, loss_target, m_w_in, m_fox_b_f, m_mla_g_q, m_mla_g_kv, m_mla_w_uq, m_mla_w_ukv, m_swa_sinks, m_mix_g, m_w_o, m_ln1_g, m_ln1_b, m_w_gate, m_w_up, m_w_down, m_ln2_g, m_ln2_b, v_w_in, v_fox_b_f, v_mla_g_q, v_mla_g_kv, v_mla_w_uq, v_mla_w_ukv, v_swa_sinks, v_mix_g, v_w_o, v_ln1_g, v_ln1_b, v_w_gate, v_w_up, v_w_down, v_ln2_g, v_ln2_b):
    w = dict(w_in=w_in, fox_b_f=fox_b_f, mla_g_q=mla_g_q, mla_g_kv=mla_g_kv, mla_w_uq=mla_w_uq, mla_w_ukv=mla_w_ukv,
             swa_sinks=swa_sinks, mix_g=mix_g, w_o=w_o, ln1_g=ln1_g, ln1_b=ln1_b, w_gate=w_gate, w_up=w_up,
             w_down=w_down, ln2_g=ln2_g, ln2_b=ln2_b)
    m = dict(w_in=m_w_in, fox_b_f=m_fox_b_f, mla_g_q=m_mla_g_q, mla_g_kv=m_mla_g_kv, mla_w_uq=m_mla_w_uq,
             mla_w_ukv=m_mla_w_ukv, swa_sinks=m_swa_sinks, mix_g=m_mix_g, w_o=m_w_o, ln1_g=m_ln1_g, ln1_b=m_ln1_b,
             w_gate=m_w_gate, w_up=m_w_up, w_down=m_w_down, ln2_g=m_ln2_g, ln2_b=m_ln2_b)
    v = dict(w_in=v_w_in, fox_b_f=v_fox_b_f, mla_g_q=v_mla_g_q, mla_g_kv=v_mla_g_kv, mla_w_uq=v_mla_w_uq,
             mla_w_ukv=v_mla_w_ukv, swa_sinks=v_swa_sinks, mix_g=v_mix_g, w_o=v_w_o, ln1_g=v_ln1_g, ln1_b=v_ln1_b,
             w_gate=v_w_gate, w_up=v_w_up, w_down=v_w_down, ln2_g=v_ln2_g, ln2_b=v_ln2_b)
    names = tuple(w)
    L = w_in.shape[0]
    S = x.shape[1]
    blk = min(BLK_SOFTMAX, S)
    bs = min(512, S)

    me = 2 * lax.axis_index("x") + lax.axis_index("y")
    axis_of = {k: SHARD_AXIS[k] - 1 for k in BIG}
    groups = (("w_in", "mla_w_uq", "mla_w_ukv"), ("w_o", "w_gate", "w_up", "w_down"))

    started, last = [], None
    for l in range(L):
        per_group = []
        for g, group in enumerate(groups):
            srcs = [w[k][l].astype(MXU_DTYPE) for k in group]
            if last is not None:
                t = min(range(len(srcs)), key=lambda i: srcs[i].size)
                srcs[t] = srcs[t] + last["token"][0, 0].astype(MXU_DTYPE)
            last = _exchange_start(srcs, scatter=False, name=f"gather_start{l}_{g}")
            per_group.append(last)
        started.append(per_group)
    all_started = sum(st["token"] for per_group in started for st in per_group)

    def gathered(l, g, after):
        mine, lands = _exchange_wait(started[l][g], after, scatter=False, name=f"gather_wait{l}_{g}")
        shard = lambda t, k: jnp.where(me == k, mine[t], lands[t][k])
        whole = lambda t, axis: jnp.concatenate([shard(t, k) for k in range(N_CHIPS)], axis=axis)
        if g == 0:
            return _prep_weights_a(*[whole(t, axis_of[name]) for t, name in enumerate(groups[0])])
        gate_up = jnp.concatenate([shard(t, k) for t in (1, 2) for k in range(N_CHIPS)], axis=1)
        return dict(w_o=whole(0, 0), wgu=gate_up, w_down=whole(3, 0))

    def scatter(l, g, grads):
        to_owner = [_to_chips(grads[k].astype(MXU_DTYPE)[None], axis_of[k] + 1)[:, 0] for k in groups[g]]
        return _exchange_start(to_owner, scatter=True, name=f"scatter_start{l}_{g}")

    tabs = _rope_tables(S)
    Ps = []
    for l in range(L):
        P = dict(fox_b_f=fox_b_f[l], swa_sinks=swa_sinks[l])
        for k in ("mla_g_q", "mla_g_kv", "mix_g", "ln1_g", "ln1_b", "ln2_g", "ln2_b"):
            P[k] = w[k][l][None, :]
        Ps.append(P)

    xa = x[0]
    xb = xa.astype(MXU_DTYPE)
    xbT = xb.T
    saved, Ws = [], []
    for l in range(L):
        W = gathered(l, 0, all_started if l == 0 else xa)
        late = lambda after, l=l: gathered(l, 1, after)
        xa, xb, xbT, sv, W = _layer_fwd(l, xa, xb, xbT, W, Ps[l], tabs, blk, late_weights=late)
        saved.append(sv)
        Ws.append(W)
    dx, loss_part = _loss_head(xa, loss_target[0], bs=bs, name="loss_head")

    layer_grads = [None] * L
    sent = [[None, None] for _ in range(L)]
    pin = None
    for l in reversed(range(L)):
        P = Ps[l] if pin is None else dict(Ps[l], ln2_g=Ps[l]["ln2_g"] + pin[0, 0])

        def send_early(grads, l=l):
            sent[l][1] = scatter(l, 1, grads)
            return sent[l][1]["token"]

        def send_late(grads, l=l):
            sent[l][0] = scatter(l, 0, grads)
            return sent[l][0]["token"]

        dx, layer_grads[l] = _layer_bwd(l, dx, saved[l], Ws[l], P, tabs, blk, send_early=send_early, send_late=send_late)
        pin = sent[l][0]["token"]
    grad_x = dx[None]

    me_arr = me.astype(jnp.int32)[None]
    partial = {k: jnp.zeros(w[k].shape, F32) for k in BIG}
    after = dx
    for l in reversed(range(L)):
        for g in (1, 0):
            mine, lands = _exchange_wait(sent[l][g], after, scatter=True, name=f"scatter_wait{l}_{g}")
            for t, k in enumerate(groups[g]):
                own = lax.dynamic_index_in_dim(mine[t], me, 0, keepdims=False)
                partial[k] = _sum_chips_into(partial[k], lands[t], own, me_arr, l, br=_rows(own.shape[0]),
                                             name=f"sum_{k}_l{l}")
            after = partial[groups[g][-1]]
    partial = [partial[k] for k in BIG]
    sibling = _core_exchange(partial, name="swap_partials")
    local = {k: jnp.stack([layer_grads[l][k] for l in range(L)]) for k in SMALL}
    out = {}
    for k, mine, theirs in zip(BIG, partial, sibling):
        shp = w[k].shape
        two_d = lambda a: a.reshape(shp[0] * shp[1], shp[2])
        res = _adamw(two_d(w[k]), two_d(m[k]), two_d(v[k]), two_d(mine), two_d(theirs), br=_rows(shp[0] * shp[1]),
                     name=f"adamw_{k}")
        out[k] = [a.reshape(shp) for a in res]

    shapes = {k: w[k].shape for k in SMALL}
    extra = 8 + (-L * SMALL_ROWS_PER_LAYER) % 8
    block = _pack_small({k: local[k] for k in SMALL}, extra)
    block = block.at[L * SMALL_ROWS_PER_LAYER, 0].set(loss_part[0, 0])
    total = _all_sum_small(block, name="sum_small")
    loss = total[L * SMALL_ROWS_PER_LAYER, 0]
    res = _adamw(_pack_small({k: w[k] for k in SMALL}, extra), _pack_small({k: m[k] for k in SMALL}, extra),
                 _pack_small({k: v[k] for k in SMALL}, extra), total, None, br=total.shape[0], name="adamw_small")
    res = [_unpack_small(t, shapes) for t in res]
    for k in SMALL:
        out[k] = [r[k] for r in res]

    return (loss, grad_x, *[out[k][0] for k in names], *[out[k][1] for k in names],
            *[out[k][2] for k in names], *[out[k][3] for k in names])


def _rows(n):
    for b in (256, 128, 64, 32, 16, 8):
        if n % b == 0:
            return b
    return n
```

```python
import functools

import numpy as np
import jax
import jax.numpy as jnp
from jax import lax
from jax.experimental import pallas as pl
from jax.experimental.pallas import tpu as pltpu

F32 = jnp.float32
MXU_DTYPE = jnp.bfloat16
NEG_INF = -1e30

D_MODEL = 1024
DEPTH = 4
HEAD_DIM = 64
GROUP_WIDTH = 256
D_FF = 2816
MLA_Q_RANK = 256
MLA_KV_RANK = 128
MLA_ROPE = 32
MLA_QK = 96
MLA_PAD = 128
ROPE_THETA = 10000.0
WINDOW = 128
ALPHA = (2.0 * DEPTH) ** 0.25
SWA_SLOPES = tuple(float(2.0 ** (-8.0 * h / 4)) for h in range(1, 5))
ATT_W = 2048
LAT_W = 640
PERM_W = ATT_W + LAT_W
COL_FQ, COL_FK, COL_FV = 0, 256, 512
COL_SQ, COL_SK, COL_SV = 768, 1024, 1280
COL_WQ, COL_WK, COL_WV = 1536, 1792, 1920
Q_COLSCALE = np.ones((1, ATT_W), np.float32)
Q_COLSCALE[:, COL_FQ:COL_FQ + 256] = HEAD_DIM ** -0.5
Q_COLSCALE[:, COL_SQ:COL_SQ + 256] = HEAD_DIM ** -0.5

ADAM_LR, ADAM_B1, ADAM_B2, ADAM_EPS, ADAM_WD, ADAM_STEP = 0.001, 0.9, 0.999, 1e-08, 0.01, 10

VMEM_LIMIT = 56 * 1024 * 1024
NT = (((1,), (1,)), ((), ()))
TN = (((0,), (0,)), ((), ()))
MESH = pl.DeviceIdType.MESH


def _cparams(*sem):
    return pltpu.CompilerParams(dimension_semantics=sem, vmem_limit_bytes=VMEM_LIMIT)


def _dot01(x, m01, dn=None, parts=2):
    acc = None
    rem = x
    for _ in range(parts):
        part = rem.astype(MXU_DTYPE)
        rem = rem - part.astype(F32)
        if dn is None:
            t = jnp.dot(part, m01, preferred_element_type=F32)
        else:
            t = lax.dot_general(part, m01, dn, preferred_element_type=F32)
        acc = t if acc is None else acc + t
    return acc


def _mm(a, b, *, name, ta=False, tb=False, out_dtype=F32, bm=512, bn=512, bk=512, resid=None, alpha=1.0,
        colscale=None, after=None):
    M, K = (a.shape[1], a.shape[0]) if ta else a.shape
    N = b.shape[0] if tb else b.shape[1]
    assert (b.shape[1] if tb else b.shape[0]) == K
    assert resid is None or colscale is None
    bm, bn, bk = min(bm, M), min(bn, N), min(bk, K)
    assert M % bm == 0 and N % bn == 0 and K % bk == 0, (name, M, N, K, bm, bn, bk)
    nk = K // bk
    assert nk == 1 or (out_dtype == F32 and colscale is None), name
    dn = (((0 if ta else 1,), (1 if tb else 0,)), ((), ()))

    extra = resid is not None or colscale is not None

    def body(*refs):
        a_ref, b_ref = refs[:2]
        r_ref = refs[2] if extra else None
        o_ref = refs[n_out]
        k = pl.program_id(2)

        def first():
            r = lax.dot_general(a_ref[...].astype(MXU_DTYPE), b_ref[...].astype(MXU_DTYPE), dn,
                                preferred_element_type=F32)
            if resid is not None:
                r = r + alpha * r_ref[...]
            if colscale is not None:
                r = r * r_ref[...]
            o_ref[...] = r.astype(o_ref.dtype)

        if nk == 1:
            first()
        else:
            pl.when(k == 0)(first)

            @pl.when(k > 0)
            def _():
                o_ref[...] += lax.dot_general(a_ref[...].astype(MXU_DTYPE), b_ref[...].astype(MXU_DTYPE), dn,
                                              preferred_element_type=F32)

    a_spec = pl.BlockSpec((bk, bm), lambda i, j, k: (k, i)) if ta else pl.BlockSpec((bm, bk), lambda i, j, k: (i, k))
    b_spec = pl.BlockSpec((bn, bk), lambda i, j, k: (j, k)) if tb else pl.BlockSpec((bk, bn), lambda i, j, k: (k, j))
    in_specs = [a_spec, b_spec]
    args = [a, b]
    if resid is not None:
        in_specs.append(pl.BlockSpec((bm, bn), lambda i, j, k: (i, j)))
        args.append(resid)
    if colscale is not None:
        in_specs.append(pl.BlockSpec((1, bn), lambda i, j, k: (0, j)))
        args.append(colscale)
    if after is not None:
        in_specs.append(pl.BlockSpec((8, 128), lambda i, j, k: (0, 0)))
        args.append(after)
    n_out = len(args)
    return pl.pallas_call(
        body, name=name, grid=(M // bm, N // bn, nk), in_specs=in_specs,
        out_specs=pl.BlockSpec((bm, bn), lambda i, j, k: (i, j)),
        out_shape=jax.ShapeDtypeStruct((M, N), out_dtype),
        compiler_params=_cparams("parallel", "parallel", "arbitrary"),
    )(*args)


HP = 4
BLK_SOFTMAX = 512
BLK_STICK = 256


def _t(x):
    return x.astype(F32).T.astype(MXU_DTYPE)


def _fill_transposed(dst_ref, src_ref, nb, blk):
    for j in range(nb):
        dst_ref[j] = _t(src_ref[j * blk:(j + 1) * blk, :])


def _smax_fwd_t(q, k, v, *, dk, blk, name):
    S = k.shape[0]
    nb = S // blk
    H = k.shape[1] // dk

    def body(q_ref, k_ref, v_ref, oT_ref, lse_ref, vT_ref):
        i = pl.program_id(1)

        @pl.when(i == 0)
        def _():
            _fill_transposed(vT_ref, v_ref, nb, blk)

        key = lax.broadcasted_iota(jnp.int32, (blk, blk), 0)
        qry = lax.broadcasted_iota(jnp.int32, (blk, blk), 1)
        qs = [_t(q_ref[:, h * dk:(h + 1) * dk]) for h in range(HP)]

        def tile(j, carry, masked):
            r0 = pl.multiple_of(j * blk, blk)
            ss = [jnp.dot(k_ref[pl.ds(r0, blk), h * dk:(h + 1) * dk], qs[h], preferred_element_type=F32)
                  for h in range(HP)]
            stats, pes = [], []
            for h in range(HP):
                m, l, _ = carry[h]
                s = jnp.where(key <= qry, ss[h], NEG_INF) if masked else ss[h]
                mn = jnp.maximum(m, jnp.max(s, axis=0, keepdims=True))
                a = jnp.exp(m - mn)
                pe = jnp.exp(s - mn)
                stats.append((mn, a * l + jnp.sum(pe, axis=0, keepdims=True), a))
                pes.append(pe.astype(MXU_DTYPE))
            pvs = [jnp.dot(vT_ref[j, h * HEAD_DIM:(h + 1) * HEAD_DIM, :], pes[h], preferred_element_type=F32)
                   for h in range(HP)]
            return tuple((stats[h][0], stats[h][1], stats[h][2] * carry[h][2] + pvs[h]) for h in range(HP))

        init = tuple((jnp.full((1, blk), NEG_INF, F32), jnp.zeros((1, blk), F32), jnp.zeros((HEAD_DIM, blk), F32))
                     for _ in range(HP))
        carry = lax.fori_loop(0, i, functools.partial(tile, masked=False), init)
        carry = tile(i, carry, True)
        for h in range(HP):
            m, l, acc = carry[h]
            oT_ref[h * HEAD_DIM:(h + 1) * HEAD_DIM, :] = acc / l
            lse_ref[h, 0] = m + jnp.log(l)

    return pl.pallas_call(
        body, name=name, grid=(H // HP, nb),
        in_specs=[pl.BlockSpec((blk, HP * dk), lambda p, i: (i, p)),
                  pl.BlockSpec((S, HP * dk), lambda p, i: (0, p)),
                  pl.BlockSpec((S, HP * HEAD_DIM), lambda p, i: (0, p))],
        out_specs=[pl.BlockSpec((HP * HEAD_DIM, blk), lambda p, i: (p, i)),
                   pl.BlockSpec((HP, 1, 1, blk), lambda p, i: (p, i, 0, 0))],
        out_shape=[jax.ShapeDtypeStruct((H * HEAD_DIM, S), F32), jax.ShapeDtypeStruct((H, nb, 1, blk), F32)],
        scratch_shapes=[pltpu.VMEM((nb, HP * HEAD_DIM, blk), MXU_DTYPE)],
        compiler_params=_cparams("arbitrary", "arbitrary"),
    )(q, k, v)


def _smax_bwd_t(q, k, v, dmix, dmixT, oT, lse, *, dk, dcb, qscale, blk, name):
    S = k.shape[0]
    nb = S // blk
    H = k.shape[1] // dk
    hd = HP * HEAD_DIM
    dcr = dcb * 128 // hd

    def body(q_ref, k_ref, v_ref, do_ref, doT_ref, oT_ref, lse_ref, dqT_ref, dk_ref, dv_ref, kT_ref):
        i = pl.program_id(1)

        @pl.when(i == 0)
        def _():
            dk_ref[...] = jnp.zeros_like(dk_ref)
            dv_ref[...] = jnp.zeros_like(dv_ref)
            _fill_transposed(kT_ref, k_ref, nb, blk)

        key = lax.broadcasted_iota(jnp.int32, (blk, blk), 0)
        qry = lax.broadcasted_iota(jnp.int32, (blk, blk), 1)
        per_head = []
        for h in range(HP):
            hs = slice(h * HEAD_DIM, (h + 1) * HEAD_DIM)
            doT = doT_ref[hs, :]
            per_head.append(dict(
                qT=_t(q_ref[:, h * dk:(h + 1) * dk]), q=q_ref[:, h * dk:(h + 1) * dk],
                doT=doT.astype(MXU_DTYPE), do=do_ref[:, hs].astype(MXU_DTYPE),
                delta=jnp.sum(doT * oT_ref[hs, :], axis=0, keepdims=True), lse=lse_ref[h, 0]))

        def tile(j, dqs, masked):
            r0 = pl.multiple_of(j * blk, blk)
            rows = pl.ds(r0, blk)
            ksl = [slice(h * dk, (h + 1) * dk) for h in range(HP)]
            hsl = [slice(h * HEAD_DIM, (h + 1) * HEAD_DIM) for h in range(HP)]
            ss = [jnp.dot(k_ref[rows, ksl[h]], per_head[h]["qT"], preferred_element_type=F32) for h in range(HP)]
            dps = [jnp.dot(v_ref[rows, hsl[h]], per_head[h]["doT"], preferred_element_type=F32) for h in range(HP)]
            prs, dss = [], []
            for h in range(HP):
                c = per_head[h]
                s = jnp.where(key <= qry, ss[h], NEG_INF) if masked else ss[h]
                pr = jnp.exp(s - c["lse"])
                dss.append((pr * (dps[h] - c["delta"])).astype(MXU_DTYPE))
                prs.append(pr.astype(MXU_DTYPE))
            for h in range(HP):
                dv_ref[rows, hsl[h]] += jnp.dot(prs[h], per_head[h]["do"], preferred_element_type=F32)
            for h in range(HP):
                dk_ref[rows, ksl[h]] += jnp.dot(dss[h], per_head[h]["q"], preferred_element_type=F32)
            return tuple(dqs[h] + jnp.dot(kT_ref[j, ksl[h], :], dss[h], preferred_element_type=F32) for h in range(HP))

        dqs = lax.fori_loop(0, i, functools.partial(tile, masked=False),
                            tuple(jnp.zeros((dk, blk), F32) for _ in range(HP)))
        dqs = tile(i, dqs, True)
        for h in range(HP):
            dqT_ref[h * dk:(h + 1) * dk, :] = dqs[h] * qscale

    return pl.pallas_call(
        body, name=name, grid=(H // HP, nb),
        in_specs=[pl.BlockSpec((blk, HP * dk), lambda p, i: (i, p)),
                  pl.BlockSpec((S, HP * dk), lambda p, i: (0, p)),
                  pl.BlockSpec((S, hd), lambda p, i: (0, p)),
                  pl.BlockSpec((blk, hd), lambda p, i: (i, dcr + p)),
                  pl.BlockSpec((hd, blk), lambda p, i: (dcr + p, i)),
                  pl.BlockSpec((hd, blk), lambda p, i: (p, i)),
                  pl.BlockSpec((HP, 1, 1, blk), lambda p, i: (p, i, 0, 0))],
        out_specs=[pl.BlockSpec((HP * dk, blk), lambda p, i: (p, i)),
                   pl.BlockSpec((S, HP * dk), lambda p, i: (0, p)),
                   pl.BlockSpec((S, hd), lambda p, i: (0, p))],
        out_shape=[jax.ShapeDtypeStruct((H * dk, S), F32), jax.ShapeDtypeStruct((S, H * dk), F32),
                   jax.ShapeDtypeStruct((S, H * HEAD_DIM), F32)],
        scratch_shapes=[pltpu.VMEM((nb, HP * dk, blk), MXU_DTYPE)],
        compiler_params=_cparams("arbitrary", "arbitrary"),
    )(q, k, v, dmix, dmixT, oT, lse)


def _log1m_beta(z):
    return -(jnp.maximum(z, 0.0) + jnp.log(1.0 + jnp.exp(-jnp.abs(z))))


def _dot01_left(m01, x, parts=2):
    acc = None
    rem = x
    for _ in range(parts):
        part = rem.astype(MXU_DTYPE)
        rem = rem - part.astype(F32)
        t = jnp.dot(m01, part, preferred_element_type=F32)
        acc = t if acc is None else acc + t
    return acc


def _sb_fwd_t(h_att, *, blk, name):
    S = h_att.shape[0]
    nb = S // blk
    hd = HP * HEAD_DIM
    qcb, kcb, vcb = COL_SQ // hd, COL_SK // hd, COL_SV // hd

    def body(q_ref, k_ref, v_ref, oT_ref, lt_ref, vT_ref):
        i = pl.program_id(1)

        @pl.when(i == 0)
        def _():
            _fill_transposed(vT_ref, v_ref, nb, blk)

        key = lax.broadcasted_iota(jnp.int32, (blk, blk), 0)
        qry = lax.broadcasted_iota(jnp.int32, (blk, blk), 1)
        strict = key < qry
        later = (qry > key).astype(MXU_DTYPE)
        qs = [_t(q_ref[:, h * HEAD_DIM:(h + 1) * HEAD_DIM]) for h in range(HP)]

        def tile(j, carry, mask):
            r0 = pl.multiple_of(j * blk, blk)
            hsl = [slice(h * HEAD_DIM, (h + 1) * HEAD_DIM) for h in range(HP)]
            zs = [jnp.dot(k_ref[pl.ds(r0, blk), hsl[h]], qs[h], preferred_element_type=F32) for h in range(HP)]
            lbs = []
            for h in range(HP):
                lb = _log1m_beta(zs[h])
                lbs.append(lb if mask is None else jnp.where(mask, lb, 0.0))
            sums = [_dot01_left(later, lbs[h]) for h in range(HP)]
            probs = []
            for h in range(HP):
                lt_ref[h, 0, j] = carry[h][0]
                a = jnp.exp(zs[h] + lbs[h] + sums[h] + carry[h][0])
                probs.append((a if mask is None else jnp.where(mask, a, 0.0)).astype(MXU_DTYPE))
            pvs = [jnp.dot(vT_ref[j, hsl[h], :], probs[h], preferred_element_type=F32) for h in range(HP)]
            return tuple((carry[h][0] + jnp.sum(lbs[h], axis=0, keepdims=True), carry[h][1] + pvs[h]) for h in range(HP))

        init = tuple((jnp.zeros((1, blk), F32), jnp.zeros((HEAD_DIM, blk), F32)) for _ in range(HP))
        carry = tile(i, init, strict)
        carry = lax.fori_loop(0, i, lambda jj, c: tile(i - 1 - jj, c, None), carry)
        for h in range(HP):
            oT_ref[h * HEAD_DIM:(h + 1) * HEAD_DIM, :] = carry[h][1]

    return pl.pallas_call(
        body, name=name, grid=(4 // HP, nb),
        in_specs=[pl.BlockSpec((blk, hd), lambda p, i: (i, qcb + p)),
                  pl.BlockSpec((S, hd), lambda p, i: (0, kcb + p)),
                  pl.BlockSpec((S, hd), lambda p, i: (0, vcb + p))],
        out_specs=[pl.BlockSpec((hd, blk), lambda p, i: (p, i)),
                   pl.BlockSpec((HP, 1, nb, 1, blk), lambda p, i: (p, i, 0, 0, 0))],
        out_shape=[jax.ShapeDtypeStruct((GROUP_WIDTH, S), F32), jax.ShapeDtypeStruct((4, nb, nb, 1, blk), F32)],
        scratch_shapes=[pltpu.VMEM((nb, hd, blk), MXU_DTYPE)],
        compiler_params=_cparams("arbitrary", "arbitrary"),
    )(h_att, h_att, h_att)


def _sb_bwd_t(h_att, dmix, dmixT, later_sums, *, dcb, qscale, blk, name):
    S = h_att.shape[0]
    nb = S // blk
    hd = HP * HEAD_DIM
    qcb, kcb, vcb = COL_SQ // hd, COL_SK // hd, COL_SV // hd
    dcr = dcb * 128 // hd

    def body(q_ref, k_ref, v_ref, do_ref, doT_ref, lt_ref, dqT_ref, dk_ref, dv_ref, kT_ref):
        i = pl.program_id(1)

        @pl.when(i == 0)
        def _():
            dk_ref[...] = jnp.zeros_like(dk_ref)
            dv_ref[...] = jnp.zeros_like(dv_ref)
            _fill_transposed(kT_ref, k_ref, nb, blk)

        key = lax.broadcasted_iota(jnp.int32, (blk, blk), 0)
        qry = lax.broadcasted_iota(jnp.int32, (blk, blk), 1)
        strict = key < qry
        later = (qry > key).astype(MXU_DTYPE)
        before = (qry < key).astype(MXU_DTYPE)
        per_head = []
        for h in range(HP):
            hs = slice(h * HEAD_DIM, (h + 1) * HEAD_DIM)
            per_head.append(dict(qT=_t(q_ref[:, hs]), q=q_ref[:, hs], doT=doT_ref[hs, :].astype(MXU_DTYPE),
                                 do=do_ref[:, hs].astype(MXU_DTYPE)))

        def tile(j, carry, mask):
            r0 = pl.multiple_of(j * blk, blk)
            rows = pl.ds(r0, blk)
            hsl = [slice(h * HEAD_DIM, (h + 1) * HEAD_DIM) for h in range(HP)]
            zs = [jnp.dot(k_ref[rows, hsl[h]], per_head[h]["qT"], preferred_element_type=F32) for h in range(HP)]
            das = [jnp.dot(v_ref[rows, hsl[h]], per_head[h]["doT"], preferred_element_type=F32) for h in range(HP)]
            lbs = []
            for h in range(HP):
                lb = _log1m_beta(zs[h])
                lbs.append(lb if mask is None else jnp.where(mask, lb, 0.0))
            sums = [_dot01_left(later, lbs[h]) for h in range(HP)]
            probs, gs = [], []
            for h in range(HP):
                a = jnp.exp(zs[h] + lbs[h] + sums[h] + lt_ref[h, 0, j])
                a = a if mask is None else jnp.where(mask, a, 0.0)
                gs.append(das[h] * a)
                probs.append(a.astype(MXU_DTYPE))
            for h in range(HP):
                dv_ref[rows, hsl[h]] += jnp.dot(probs[h], per_head[h]["do"], preferred_element_type=F32)
            es = [_dot01_left(before, gs[h]) for h in range(HP)]
            dzs = []
            for h in range(HP):
                dz = gs[h] * jnp.exp(lbs[h]) - (carry[h][0] + es[h]) * jnp.exp(zs[h] + lbs[h])
                dzs.append((dz if mask is None else jnp.where(mask, dz, 0.0)).astype(MXU_DTYPE))
            for h in range(HP):
                dk_ref[rows, hsl[h]] += jnp.dot(dzs[h], per_head[h]["q"], preferred_element_type=F32)
            return tuple((carry[h][0] + jnp.sum(gs[h], axis=0, keepdims=True),
                          carry[h][1] + jnp.dot(kT_ref[j, hsl[h], :], dzs[h], preferred_element_type=F32))
                         for h in range(HP))

        init = tuple((jnp.zeros((1, blk), F32), jnp.zeros((HEAD_DIM, blk), F32)) for _ in range(HP))
        carry = lax.fori_loop(0, i, lambda j, c: tile(j, c, None), init)
        carry = tile(i, carry, strict)
        for h in range(HP):
            dqT_ref[h * HEAD_DIM:(h + 1) * HEAD_DIM, :] = carry[h][1] * qscale

    return pl.pallas_call(
        body, name=name, grid=(4 // HP, nb),
        in_specs=[pl.BlockSpec((blk, hd), lambda p, i: (i, qcb + p)),
                  pl.BlockSpec((S, hd), lambda p, i: (0, kcb + p)),
                  pl.BlockSpec((S, hd), lambda p, i: (0, vcb + p)),
                  pl.BlockSpec((blk, hd), lambda p, i: (i, dcr + p)),
                  pl.BlockSpec((hd, blk), lambda p, i: (dcr + p, i)),
                  pl.BlockSpec((HP, 1, nb, 1, blk), lambda p, i: (p, i, 0, 0, 0))],
        out_specs=[pl.BlockSpec((hd, blk), lambda p, i: (p, i)),
                   pl.BlockSpec((S, hd), lambda p, i: (0, p)),
                   pl.BlockSpec((S, hd), lambda p, i: (0, p))],
        out_shape=[jax.ShapeDtypeStruct((GROUP_WIDTH, S), F32), jax.ShapeDtypeStruct((S, GROUP_WIDTH), F32),
                   jax.ShapeDtypeStruct((S, GROUP_WIDTH), F32)],
        scratch_shapes=[pltpu.VMEM((nb, hd, blk), MXU_DTYPE)],
        compiler_params=_cparams("arbitrary", "arbitrary"),
    )(h_att, h_att, h_att, dmix, dmixT, later_sums)


SWA_SUB = 4


def _swa_sub_blocks(S):
    return min(SWA_SUB, S // WINDOW)


def _swa_tiles(n, sub):
    tiles = []
    for b in range(sub):
        start = pl.multiple_of(jnp.maximum(n * sub + b - 1, 0) * WINDOW, WINDOW)
        tiles += [(b, h, slice(b * WINDOW, (b + 1) * WINDOW), start) for h in range(4)]
    return tiles


def _swa_scores(q_ref, k_ref, n, sub, tile):
    b, h, qrows, start = tile
    g = h // 2
    kb = k_ref[pl.ds(start, 2 * WINDOW), g * HEAD_DIM:(g + 1) * HEAD_DIM]
    s = lax.dot_general(q_ref[qrows, h * HEAD_DIM:(h + 1) * HEAD_DIM], kb, NT,
                        preferred_element_type=F32) * (HEAD_DIM ** -0.5)
    dist = ((n * sub + b) * WINDOW + lax.broadcasted_iota(jnp.int32, (WINDOW, 2 * WINDOW), 0)
            - start - lax.broadcasted_iota(jnp.int32, (WINDOW, 2 * WINDOW), 1))
    s = s - SWA_SLOPES[h] * dist.astype(F32)
    valid = (dist >= 0) & (dist < WINDOW)
    return jnp.where(valid, s, NEG_INF), kb


def _swa_fwd(h_att, sinks, *, name):
    S = h_att.shape[0]
    sub = _swa_sub_blocks(S)
    rows = sub * WINDOW
    qcb, kcb, vcb = COL_WQ // 256, COL_WK // 128, COL_WV // 128

    def body(sink_ref, q_ref, k_ref, v_ref, o_ref, lse_ref):
        n = pl.program_id(0)
        tiles = _swa_tiles(n, sub)
        scores = [_swa_scores(q_ref, k_ref, n, sub, t)[0] for t in tiles]
        probs = []
        for (b, h, qrows, start), s in zip(tiles, scores):
            sink = sink_ref[h]
            m = jnp.maximum(jnp.max(s, axis=1, keepdims=True), sink)
            e = jnp.exp(s - m)
            den = jnp.sum(e, axis=1, keepdims=True) + jnp.exp(sink - m)
            probs.append((e / den).astype(MXU_DTYPE))
            lse_ref[h, qrows] = m + jnp.log(den)
        for (b, h, qrows, start), p in zip(tiles, probs):
            vb = v_ref[pl.ds(start, 2 * WINDOW), (h // 2) * HEAD_DIM:(h // 2 + 1) * HEAD_DIM]
            o_ref[qrows, h * HEAD_DIM:(h + 1) * HEAD_DIM] = jnp.dot(p, vb, preferred_element_type=F32)

    return pl.pallas_call(
        body, name=name, grid=(S // rows,),
        in_specs=[pl.BlockSpec(memory_space=pltpu.SMEM),
                  pl.BlockSpec((rows, 256), lambda n: (n, qcb)),
                  pl.BlockSpec((S, 128), lambda n: (0, kcb)),
                  pl.BlockSpec((S, 128), lambda n: (0, vcb))],
        out_specs=[pl.BlockSpec((rows, 256), lambda n: (n, 0)), pl.BlockSpec((4, rows, 1), lambda n: (0, n, 0))],
        out_shape=[jax.ShapeDtypeStruct((S, GROUP_WIDTH), F32), jax.ShapeDtypeStruct((4, S, 1), F32)],
        compiler_params=_cparams("arbitrary"),
    )(sinks, h_att, h_att, h_att)


def _swa_bwd(h_att, sinks, dmix, o_arr, lse, *, dcb, name):
    S = h_att.shape[0]
    sub = _swa_sub_blocks(S)
    rows = sub * WINDOW
    qcb, kcb, vcb = COL_WQ // 256, COL_WK // 128, COL_WV // 128

    def body(sink_ref, q_ref, k_ref, v_ref, do_ref, o_ref, lse_ref, dq_ref, dk_ref, dv_ref, dsink_ref):
        n = pl.program_id(0)

        @pl.when(n == 0)
        def _():
            dk_ref[...] = jnp.zeros_like(dk_ref)
            dv_ref[...] = jnp.zeros_like(dv_ref)
            dsink_ref[...] = jnp.zeros_like(dsink_ref)

        tiles = _swa_tiles(n, sub)
        hsl = [slice(h * HEAD_DIM, (h + 1) * HEAD_DIM) for h in range(4)]
        gsl = [slice(g * HEAD_DIM, (g + 1) * HEAD_DIM) for g in range(2)]
        scale = HEAD_DIM ** -0.5
        sk = [_swa_scores(q_ref, k_ref, n, sub, t) for t in tiles]
        dobs = [do_ref[qrows, hsl[h]].astype(MXU_DTYPE) for b, h, qrows, start in tiles]
        dps = [lax.dot_general(dob, v_ref[pl.ds(start, 2 * WINDOW), gsl[h // 2]], NT, preferred_element_type=F32)
               for (b, h, qrows, start), dob in zip(tiles, dobs)]
        prs, dss = [], []
        for t, (b, h, qrows, start) in enumerate(tiles):
            lse_h = lse_ref[h, qrows]
            pr = jnp.exp(sk[t][0] - lse_h)
            delta = jnp.sum(do_ref[qrows, hsl[h]] * o_ref[qrows, hsl[h]], axis=1, keepdims=True)
            dss.append((pr * (dps[t] - delta)).astype(MXU_DTYPE))
            prs.append(pr.astype(MXU_DTYPE))
            dsink_ref[h:h + 1, :] += jnp.zeros((1, 128), F32) - jnp.sum(jnp.exp(sink_ref[h] - lse_h) * delta)
        for t, (b, h, qrows, start) in enumerate(tiles):
            dq_ref[qrows, hsl[h]] = jnp.dot(dss[t], sk[t][1], preferred_element_type=F32) * scale
        for b in range(sub):
            for g in range(2):
                t0, t1 = 4 * b + 2 * g, 4 * b + 2 * g + 1
                qrows, krows = tiles[t0][2], pl.ds(tiles[t0][3], 2 * WINDOW)
                dk_ref[krows, gsl[g]] += (
                    lax.dot_general(dss[t0], q_ref[qrows, hsl[2 * g]], TN, preferred_element_type=F32)
                    + lax.dot_general(dss[t1], q_ref[qrows, hsl[2 * g + 1]], TN, preferred_element_type=F32)) * scale
                dv_ref[krows, gsl[g]] += (lax.dot_general(prs[t0], dobs[t0], TN, preferred_element_type=F32)
                                          + lax.dot_general(prs[t1], dobs[t1], TN, preferred_element_type=F32))

    return pl.pallas_call(
        body, name=name, grid=(S // rows,),
        in_specs=[pl.BlockSpec(memory_space=pltpu.SMEM),
                  pl.BlockSpec((rows, 256), lambda n: (n, qcb)),
                  pl.BlockSpec((S, 128), lambda n: (0, kcb)),
                  pl.BlockSpec((S, 128), lambda n: (0, vcb)),
                  pl.BlockSpec((rows, 256), lambda n: (n, dcb)),
                  pl.BlockSpec((rows, 256), lambda n: (n, 0)),
                  pl.BlockSpec((4, rows, 1), lambda n: (0, n, 0))],
        out_specs=[pl.BlockSpec((rows, 256), lambda n: (n, 0)),
                   pl.BlockSpec((S, 128), lambda n: (0, 0)),
                   pl.BlockSpec((S, 128), lambda n: (0, 0)),
                   pl.BlockSpec((4, 128), lambda n: (0, 0))],
        out_shape=[jax.ShapeDtypeStruct((S, GROUP_WIDTH), F32), jax.ShapeDtypeStruct((S, 128), F32),
                   jax.ShapeDtypeStruct((S, 128), F32), jax.ShapeDtypeStruct((4, 128), F32)],
        compiler_params=_cparams("arbitrary"),
    )(sinks, h_att, h_att, h_att, dmix, o_arr, lse)


def _tri(n, incl, upper):
    r = lax.broadcasted_iota(jnp.int32, (n, n), 0)
    c = lax.broadcasted_iota(jnp.int32, (n, n), 1)
    if upper:
        m = (r <= c) if incl else (r < c)
    else:
        m = (r >= c) if incl else (r > c)
    return m.astype(MXU_DTYPE)


def _fox_gate_fwd(fg, b_f, *, name):
    _, R, _ = fg.shape

    def body(b_ref, fg_ref, pos_ref, neg_ref):
        up_incl = _tri(128, True, True)
        ones = jnp.ones((128, 128), MXU_DTYPE)
        for h in range(4):
            z = fg_ref[h] + b_ref[h]
            logf = jnp.minimum(z, 0.0) - jnp.log(1.0 + jnp.exp(-jnp.abs(z)))
            within = _dot01(logf, up_incl, parts=3)
            totals = _dot01(logf, ones, parts=3)
            rem = within + _rows_other(totals, R, after=False)
            for part in range(3):
                piece = rem.astype(MXU_DTYPE)
                rem = rem - piece.astype(F32)
                pos_ref[h, part] = piece
                neg_ref[h, part] = -piece

    shape = (4, 3) + fg.shape[1:]
    return pl.pallas_call(
        body, name=name,
        in_specs=[pl.BlockSpec(memory_space=pltpu.SMEM), pl.BlockSpec(memory_space=pltpu.VMEM)],
        out_specs=[pl.BlockSpec(memory_space=pltpu.VMEM)] * 2,
        out_shape=[jax.ShapeDtypeStruct(shape, MXU_DTYPE)] * 2,
    )(b_f, fg)


def _rows_other(totals, n, after):
    r = lax.broadcasted_iota(jnp.int32, (n, n), 0)
    c = lax.broadcasted_iota(jnp.int32, (n, n), 1)
    m = ((c > r) if after else (c < r)).astype(MXU_DTYPE)
    acc = None
    rem = totals
    for _ in range(3):
        part = rem.astype(MXU_DTYPE)
        rem = rem - part.astype(F32)
        t = jnp.dot(m, part, preferred_element_type=F32)
        acc = t if acc is None else acc + t
    return acc


def _fox_gate_bwd(fg, b_f, dcum_k, dcum_q, *, q_unscale, name):
    _, R, _ = fg.shape

    def body(b_ref, fg_ref, dck_ref, dcq_ref, dfg_ref, db_ref):
        low_incl = _tri(128, True, False)
        ones = jnp.ones((128, 128), MXU_DTYPE)
        for h in range(4):
            dc = dcq_ref[h] * q_unscale - dck_ref[h]
            dlogf = _dot01(dc, low_incl, parts=3) + _rows_other(_dot01(dc, ones, parts=3), R, after=True)
            z = fg_ref[h] + b_ref[h]
            dz = dlogf * jnp.exp(jnp.minimum(-z, 0.0) - jnp.log(1.0 + jnp.exp(-jnp.abs(z))))
            dfg_ref[h] = dz
            db_ref[h:h + 1, :] = jnp.zeros((1, 128), F32) + jnp.sum(dz)

    return pl.pallas_call(
        body, name=name,
        in_specs=[pl.BlockSpec(memory_space=pltpu.SMEM)] + [pl.BlockSpec(memory_space=pltpu.VMEM)] * 3,
        out_specs=[pl.BlockSpec(memory_space=pltpu.VMEM), pl.BlockSpec(memory_space=pltpu.VMEM)],
        out_shape=[jax.ShapeDtypeStruct(fg.shape, F32), jax.ShapeDtypeStruct((4, 128), F32)],
    )(b_f, fg, dcum_k, dcum_q)


def _rope_rot(transpose):
    r = lax.broadcasted_iota(jnp.int32, (MLA_PAD, MLA_PAD), 0)
    c = lax.broadcasted_iota(jnp.int32, (MLA_PAD, MLA_PAD), 1)
    if transpose:
        r, c = c, r
    half = MLA_ROPE // 2
    lo, mid, hi = HEAD_DIM, HEAD_DIM + half, HEAD_DIM + MLA_ROPE
    minus = (c >= lo) & (c < mid) & (r == c + half)
    plus = (c >= mid) & (c < hi) & (r == c - half)
    return jnp.where(plus, 1.0, jnp.where(minus, -1.0, 0.0)).astype(MXU_DTYPE)


def _rope_lanes():
    lane = lax.broadcasted_iota(jnp.int32, (1, MLA_PAD), 1)
    return ((lane >= HEAD_DIM) & (lane < HEAD_DIM + MLA_ROPE)).astype(F32)


def _rms(x, g, eps=1e-6):
    r = lax.rsqrt(jnp.mean(x * x, axis=-1, keepdims=True) + eps)
    return x * r * g, r


def _rms_bwd(dy, x, r, g):
    xh = x * r
    dxh = dy * g
    dx = r * (dxh - xh * jnp.mean(dxh * xh, axis=-1, keepdims=True))
    return dx, dy * xh


def _mla_prep_fwd(lat, g_q, g_kv, wuq, wuk, wuv, cosm, sinm, *, bs, name):
    S = lat.shape[0]

    def body(lat_ref, gq_ref, gkv_ref, wuq_ref, wuk_ref, wuv_ref, cos_ref, sin_ref, q_ref, k_ref, v_ref):
        rot = _rope_rot(False)
        cosm_, sinm_ = cos_ref[...], sin_ref[...]
        nq, _ = _rms(lat_ref[:, 0:MLA_Q_RANK], gq_ref[...])
        nkv, _ = _rms(lat_ref[:, MLA_Q_RANK:MLA_Q_RANK + MLA_KV_RANK], gkv_ref[...])
        qlat = jnp.dot(nq.astype(MXU_DTYPE), wuq_ref[...], preferred_element_type=F32)
        klat = jnp.dot(nkv.astype(MXU_DTYPE), wuk_ref[...], preferred_element_type=F32)
        v_ref[...] = jnp.dot(nkv.astype(MXU_DTYPE), wuv_ref[...], preferred_element_type=F32).astype(v_ref.dtype)
        krb = lat_ref[:, 384:512]
        kr = krb * (cosm_ * _rope_lanes()) + _dot01(krb, rot, parts=3) * sinm_
        for h in range(4):
            sl = slice(h * MLA_PAD, (h + 1) * MLA_PAD)
            qh = qlat[:, sl]
            q_ref[:, sl] = ((qh * cosm_ + _dot01(qh, rot, parts=3) * sinm_) * (MLA_QK ** -0.5)).astype(q_ref.dtype)
            k_ref[:, sl] = (klat[:, sl] + kr).astype(k_ref.dtype)

    full = lambda a: pl.BlockSpec(a.shape, lambda i: (0,) * a.ndim)
    return pl.pallas_call(
        body, name=name, grid=(S // bs,),
        in_specs=[pl.BlockSpec((bs, LAT_W), lambda i: (i, 0)), full(g_q), full(g_kv), full(wuq), full(wuk), full(wuv),
                  pl.BlockSpec((bs, MLA_PAD), lambda i: (i, 0)), pl.BlockSpec((bs, MLA_PAD), lambda i: (i, 0))],
        out_specs=[pl.BlockSpec((bs, 512), lambda i: (i, 0)), pl.BlockSpec((bs, 512), lambda i: (i, 0)),
                   pl.BlockSpec((bs, 256), lambda i: (i, 0))],
        out_shape=[jax.ShapeDtypeStruct((S, 512), MXU_DTYPE), jax.ShapeDtypeStruct((S, 512), MXU_DTYPE),
                   jax.ShapeDtypeStruct((S, 256), MXU_DTYPE)],
        compiler_params=_cparams("parallel"),
    )(lat, g_q, g_kv, wuq, wuk, wuv, cosm, sinm)


def _mla_prep_bwd(lat, g_q, g_kv, wuq, wuk, wuv, cosm, sinm, dq, dk, dv, *, bs, name):
    S = lat.shape[0]

    def body(lat_ref, gq_ref, gkv_ref, wuq_ref, wuk_ref, wuv_ref, cos_ref, sin_ref, dq_ref, dk_ref, dv_ref,
             dlat_ref, dwuq_ref, dwuk_ref, dwuv_ref, dgq_ref, dgkv_ref):
        @pl.when(pl.program_id(0) == 0)
        def _():
            for r in (dwuq_ref, dwuk_ref, dwuv_ref, dgq_ref, dgkv_ref):
                r[...] = jnp.zeros_like(r)

        rot_t = _rope_rot(True)
        cosm_, sinm_ = cos_ref[...], sin_ref[...]
        cq = lat_ref[:, 0:MLA_Q_RANK]
        ckv = lat_ref[:, MLA_Q_RANK:MLA_Q_RANK + MLA_KV_RANK]
        nq, rq = _rms(cq, gq_ref[...])
        nkv, rkv = _rms(ckv, gkv_ref[...])
        nqb, nkvb = nq.astype(MXU_DTYPE), nkv.astype(MXU_DTYPE)

        dqlat = []
        dkr = jnp.zeros((bs, MLA_PAD), F32)
        for h in range(4):
            sl = slice(h * MLA_PAD, (h + 1) * MLA_PAD)
            dqh = dq_ref[sl, :].T
            dqlat.append(dqh * cosm_ + _dot01(dqh * sinm_, rot_t, parts=3))
            dkr = dkr + dk_ref[:, sl]
        dqlat = jnp.concatenate(dqlat, axis=1).astype(MXU_DTYPE)
        dkb = dk_ref[...].astype(MXU_DTYPE)
        dvb = dv_ref[...].astype(MXU_DTYPE)

        dnq = lax.dot_general(dqlat, wuq_ref[...], NT, preferred_element_type=F32)
        dnkv = (lax.dot_general(dkb, wuk_ref[...], NT, preferred_element_type=F32)
                + lax.dot_general(dvb, wuv_ref[...], NT, preferred_element_type=F32))
        dwuq_ref[...] += lax.dot_general(nqb, dqlat, TN, preferred_element_type=F32)
        dwuk_ref[...] += lax.dot_general(nkvb, dkb, TN, preferred_element_type=F32)
        dwuv_ref[...] += lax.dot_general(nkvb, dvb, TN, preferred_element_type=F32)
        dcq, tq = _rms_bwd(dnq, cq, rq, gq_ref[...])
        dckv, tkv = _rms_bwd(dnkv, ckv, rkv, gkv_ref[...])
        dgq_ref[...] += jnp.sum(tq, axis=0, keepdims=True)
        dgkv_ref[...] += jnp.sum(tkv, axis=0, keepdims=True)
        dlat_ref[:, 0:MLA_Q_RANK] = dcq.astype(dlat_ref.dtype)
        dlat_ref[:, MLA_Q_RANK:MLA_Q_RANK + MLA_KV_RANK] = dckv.astype(dlat_ref.dtype)
        dkrb = dkr * (cosm_ * _rope_lanes()) + _dot01(dkr * sinm_, rot_t, parts=3)
        dlat_ref[:, 384:512] = dkrb.astype(dlat_ref.dtype)

    full = lambda a: pl.BlockSpec(a.shape, lambda i: (0,) * a.ndim)
    row = lambda w: pl.BlockSpec((bs, w), lambda i: (i, 0))
    acc = lambda *shape: pl.BlockSpec(shape, lambda i: (0,) * len(shape))
    return pl.pallas_call(
        body, name=name, grid=(S // bs,),
        in_specs=[row(LAT_W), full(g_q), full(g_kv), full(wuq), full(wuk), full(wuv), row(MLA_PAD), row(MLA_PAD),
                  pl.BlockSpec((512, bs), lambda i: (0, i)), row(512), row(256)],
        out_specs=[row(512), acc(256, 512), acc(128, 512), acc(128, 256), acc(1, 256), acc(1, 128)],
        out_shape=[jax.ShapeDtypeStruct((S, 512), MXU_DTYPE), jax.ShapeDtypeStruct((256, 512), F32),
                   jax.ShapeDtypeStruct((128, 512), F32), jax.ShapeDtypeStruct((128, 256), F32),
                   jax.ShapeDtypeStruct((1, 256), F32), jax.ShapeDtypeStruct((1, 128), F32)],
        compiler_params=_cparams("arbitrary"),
    )(lat, g_q, g_kv, wuq, wuk, wuv, cosm, sinm, dq, dk, dv)


def _row_spec(bs, w):
    return pl.BlockSpec((bs, w), lambda i: (i, 0))


def _vec_spec(w):
    return pl.BlockSpec((1, w), lambda i: (0, 0))


def _mix_specs(bs):
    return [pl.BlockSpec((GROUP_WIDTH, bs), lambda i: (0, i))] * 3 + [_row_spec(bs, GROUP_WIDTH)]


def _mix_groups(a_ref, b_ref, c_ref, d_ref):
    return [a_ref[...].T, b_ref[...].T, c_ref[...].T, d_ref[...]]


def _gnorm_fwd(outs, g, *, bs, name):
    S = outs[3].shape[0]

    def body(a_ref, b_ref, c_ref, d_ref, g_ref, o_ref, oT_ref):
        for k, x in enumerate(_mix_groups(a_ref, b_ref, c_ref, d_ref)):
            sl = slice(k * GROUP_WIDTH, (k + 1) * GROUP_WIDTH)
            y, _ = _rms(x, g_ref[:, sl])
            o_ref[:, sl] = y.astype(o_ref.dtype)
            oT_ref[sl, :] = y.T.astype(oT_ref.dtype)

    return pl.pallas_call(
        body, name=name, grid=(S // bs,),
        in_specs=_mix_specs(bs) + [_vec_spec(D_MODEL)],
        out_specs=[_row_spec(bs, D_MODEL), pl.BlockSpec((D_MODEL, bs), lambda i: (0, i))],
        out_shape=[jax.ShapeDtypeStruct((S, D_MODEL), MXU_DTYPE), jax.ShapeDtypeStruct((D_MODEL, S), MXU_DTYPE)],
        compiler_params=_cparams("parallel"),
    )(*outs, g)


def _gnorm_bwd(dgn, outs, g, *, bs, name):
    S = dgn.shape[0]

    def body(dgn_ref, a_ref, b_ref, c_ref, d_ref, g_ref, dmix_ref, dmixT_ref, dg_ref):
        @pl.when(pl.program_id(0) == 0)
        def _():
            dg_ref[...] = jnp.zeros_like(dg_ref)

        for k, x in enumerate(_mix_groups(a_ref, b_ref, c_ref, d_ref)):
            sl = slice(k * GROUP_WIDTH, (k + 1) * GROUP_WIDTH)
            _, r = _rms(x, g_ref[:, sl])
            dx, t = _rms_bwd(dgn_ref[:, sl], x, r, g_ref[:, sl])
            dmix_ref[:, sl] = dx
            dmixT_ref[sl, :] = dx.T
            dg_ref[:, sl] += jnp.sum(t, axis=0, keepdims=True)

    return pl.pallas_call(
        body, name=name, grid=(S // bs,),
        in_specs=[_row_spec(bs, D_MODEL)] + _mix_specs(bs) + [_vec_spec(D_MODEL)],
        out_specs=[_row_spec(bs, D_MODEL), pl.BlockSpec((D_MODEL, bs), lambda i: (0, i)), _vec_spec(D_MODEL)],
        out_shape=[jax.ShapeDtypeStruct((S, D_MODEL), F32), jax.ShapeDtypeStruct((D_MODEL, S), F32),
                   jax.ShapeDtypeStruct((1, D_MODEL), F32)],
        compiler_params=_cparams("arbitrary"),
    )(dgn, *outs, g)


def _ln_fwd(u, g, b, *, bs, name):
    S = u.shape[0]

    def body(u_ref, g_ref, b_ref, y_ref, yb_ref, ybT_ref, xh_ref, rs_ref):
        x = u_ref[...]
        mu = jnp.mean(x, axis=-1, keepdims=True)
        xc = x - mu
        rs = lax.rsqrt(jnp.mean(xc * xc, axis=-1, keepdims=True) + 1e-5)
        xh = xc * rs
        y = xh * g_ref[...] + b_ref[...]
        y_ref[...] = y
        yb_ref[...] = y.astype(yb_ref.dtype)
        ybT_ref[...] = y.T.astype(ybT_ref.dtype)
        xh_ref[...] = xh
        rs_ref[...] = rs

    return pl.pallas_call(
        body, name=name, grid=(S // bs,),
        in_specs=[_row_spec(bs, D_MODEL), _vec_spec(D_MODEL), _vec_spec(D_MODEL)],
        out_specs=[_row_spec(bs, D_MODEL), _row_spec(bs, D_MODEL), pl.BlockSpec((D_MODEL, bs), lambda i: (0, i)),
                   _row_spec(bs, D_MODEL), _row_spec(bs, 1)],
        out_shape=[jax.ShapeDtypeStruct((S, D_MODEL), F32), jax.ShapeDtypeStruct((S, D_MODEL), MXU_DTYPE),
                   jax.ShapeDtypeStruct((D_MODEL, S), MXU_DTYPE), jax.ShapeDtypeStruct((S, D_MODEL), F32),
                   jax.ShapeDtypeStruct((S, 1), F32)],
        compiler_params=_cparams("parallel"),
    )(u, g, b)


def _ln_bwd(dy, xh, rs, g, *, bs, name):
    S = dy.shape[0]

    def body(dy_ref, xh_ref, rs_ref, g_ref, du_ref, dub_ref, dg_ref, db_ref):
        @pl.when(pl.program_id(0) == 0)
        def _():
            dg_ref[...] = jnp.zeros_like(dg_ref)
            db_ref[...] = jnp.zeros_like(db_ref)

        dy_, xh_ = dy_ref[...], xh_ref[...]
        dxh = dy_ * g_ref[...]
        du = rs_ref[...] * (dxh - jnp.mean(dxh, axis=-1, keepdims=True)
                            - xh_ * jnp.mean(dxh * xh_, axis=-1, keepdims=True))
        du_ref[...] = du
        dub_ref[...] = du.astype(dub_ref.dtype)
        dg_ref[...] += jnp.sum(dy_ * xh_, axis=0, keepdims=True)
        db_ref[...] += jnp.sum(dy_, axis=0, keepdims=True)

    return pl.pallas_call(
        body, name=name, grid=(S // bs,),
        in_specs=[_row_spec(bs, D_MODEL), _row_spec(bs, D_MODEL), _row_spec(bs, 1), _vec_spec(D_MODEL)],
        out_specs=[_row_spec(bs, D_MODEL), _row_spec(bs, D_MODEL), _vec_spec(D_MODEL), _vec_spec(D_MODEL)],
        out_shape=[jax.ShapeDtypeStruct((S, D_MODEL), F32), jax.ShapeDtypeStruct((S, D_MODEL), MXU_DTYPE),
                   jax.ShapeDtypeStruct((1, D_MODEL), F32), jax.ShapeDtypeStruct((1, D_MODEL), F32)],
        compiler_params=_cparams("arbitrary"),
    )(dy, xh, rs, g)


def _swiglu_fwd(gu, *, bs, name):
    S = gu.shape[0]

    def body(gu_ref, a_ref, aT_ref):
        gt = gu_ref[:, :D_FF]
        a = gt / (1.0 + jnp.exp(-gt)) * gu_ref[:, D_FF:]
        a_ref[...] = a.astype(a_ref.dtype)
        aT_ref[...] = a.T.astype(aT_ref.dtype)

    return pl.pallas_call(
        body, name=name, grid=(S // bs,),
        in_specs=[_row_spec(bs, 2 * D_FF)],
        out_specs=[_row_spec(bs, D_FF), pl.BlockSpec((D_FF, bs), lambda i: (0, i))],
        out_shape=[jax.ShapeDtypeStruct((S, D_FF), MXU_DTYPE), jax.ShapeDtypeStruct((D_FF, S), MXU_DTYPE)],
        compiler_params=_cparams("parallel"),
    )(gu)


def _swiglu_bwd(da, gu, *, bs, name):
    S = gu.shape[0]

    def body(da_ref, gu_ref, dgu_ref):
        gt, da_ = gu_ref[:, :D_FF], da_ref[...]
        sg = 1.0 / (1.0 + jnp.exp(-gt))
        silu = gt * sg
        dgu_ref[:, :D_FF] = (da_ * gu_ref[:, D_FF:] * (sg + silu * (1.0 - sg))).astype(dgu_ref.dtype)
        dgu_ref[:, D_FF:] = (da_ * silu).astype(dgu_ref.dtype)

    return pl.pallas_call(
        body, name=name, grid=(S // bs,),
        in_specs=[_row_spec(bs, D_FF), _row_spec(bs, 2 * D_FF)],
        out_specs=_row_spec(bs, 2 * D_FF), out_shape=jax.ShapeDtypeStruct((S, 2 * D_FF), MXU_DTYPE),
        compiler_params=_cparams("parallel"),
    )(da, gu)


def _loss_head(y, target, *, bs, name):
    S = y.shape[0]

    def body(y_ref, t_ref, dy_ref, loss_ref):
        @pl.when(pl.program_id(0) == 0)
        def _():
            loss_ref[...] = jnp.zeros_like(loss_ref)

        e = y_ref[...] - t_ref[...]
        dy_ref[...] = e * (1.0 / D_MODEL)
        per_tok = jnp.mean(e * e, axis=-1, keepdims=True)
        loss_ref[...] += 0.5 * jnp.sum(per_tok, axis=0, keepdims=True)

    return pl.pallas_call(
        body, name=name, grid=(S // bs,),
        in_specs=[_row_spec(bs, D_MODEL), _row_spec(bs, D_MODEL)],
        out_specs=[_row_spec(bs, D_MODEL), pl.BlockSpec((1, 1), lambda i: (0, 0))],
        out_shape=[jax.ShapeDtypeStruct((S, D_MODEL), F32), jax.ShapeDtypeStruct((1, 1), F32)],
        compiler_params=_cparams("arbitrary"),
    )(y, target)


def _blk(n, target):
    if n <= target:
        return n
    best = None
    for b in range(128, target + 1, 128):
        if n % b == 0:
            best = b
    assert best is not None, n
    return best


def _rope_tables(S):
    pos = jnp.arange(S, dtype=F32)
    inv = ROPE_THETA ** (-jnp.arange(0, MLA_ROPE, 2, dtype=F32) / MLA_ROPE)
    ang = pos[:, None] * inv[None, :]
    cos, sin = jnp.cos(ang), jnp.sin(ang)
    one, zero, pad = jnp.ones((S, HEAD_DIM), F32), jnp.zeros((S, HEAD_DIM), F32), jnp.zeros((S, MLA_PAD - MLA_QK), F32)
    return jnp.concatenate([one, cos, cos, pad], axis=1), jnp.concatenate([zero, sin, sin, pad], axis=1)


def _prep_weights_a(w_in, w_uq, w_ukv):
    z = lambda n: jnp.zeros((D_MODEL, n), w_in.dtype)
    win_a = jnp.concatenate([w_in[:, 0:768], w_in[:, 1188:2468]], axis=1)
    win_l = jnp.concatenate([w_in[:, 772:1156], z(64), w_in[:, 1156:1188], z(32), w_in[:, 768:772], z(124)], axis=1)
    kv = w_ukv.reshape(MLA_KV_RANK, 4, 2 * HEAD_DIM)
    return dict(
        win_a=win_a, win_l=win_l, win_p=jnp.concatenate([win_a, win_l], axis=1),
        wuq=jnp.pad(w_uq.reshape(MLA_Q_RANK, 4, MLA_QK), ((0, 0), (0, 0), (0, MLA_PAD - MLA_QK))).reshape(MLA_Q_RANK, 512),
        wuk=jnp.pad(kv[:, :, :HEAD_DIM], ((0, 0), (0, 0), (0, HEAD_DIM))).reshape(MLA_KV_RANK, 512),
        wuv=kv[:, :, HEAD_DIM:].reshape(MLA_KV_RANK, 256))


def _unprep_grads(dwin_p, dwuq, dwuk, dwuv, dwo, dwgu, dwd):
    dw_in = jnp.concatenate([dwin_p[:, 0:768], dwin_p[:, 2560:2564], dwin_p[:, 2048:2432], dwin_p[:, 2496:2528],
                             dwin_p[:, 768:2048]], axis=1)
    dw_uq = dwuq.reshape(MLA_Q_RANK, 4, MLA_PAD)[:, :, :MLA_QK].reshape(MLA_Q_RANK, 4 * MLA_QK)
    dw_ukv = jnp.concatenate([dwuk.reshape(MLA_KV_RANK, 4, MLA_PAD)[:, :, :HEAD_DIM],
                              dwuv.reshape(MLA_KV_RANK, 4, HEAD_DIM)], axis=2).reshape(MLA_KV_RANK, 512)
    return dict(w_in=dw_in, mla_w_uq=dw_uq, mla_w_ukv=dw_ukv, w_o=dwo, w_gate=dwgu[:, :D_FF], w_up=dwgu[:, D_FF:],
                w_down=dwd)


def _layer_fwd(l, x, xb, xbT, W, P, tabs, blk, late_weights=None):
    S = x.shape[0]
    nb = S // blk
    n = lambda s: f"l{l}_{s}"
    bs = min(512, S)
    h_att = _mm(xb, W["win_a"], name=n("in_att"), out_dtype=MXU_DTYPE, bm=1024, bn=1024, bk=1024, colscale=Q_COLSCALE)
    lat = _mm(xb, W["win_l"], name=n("in_lat"), bm=2048, bn=LAT_W, bk=1024)
    fg = lat[:, 512:516].T.reshape(4, S // 128, 128)
    cpos, cneg = _fox_gate_fwd(fg, P["fox_b_f"], name=n("fox_gate"))
    one3 = jnp.ones((S, 4, 3), MXU_DTYPE)
    zpad = jnp.zeros((S, 4, MLA_PAD - HEAD_DIM - 6), MXU_DTYPE)
    per_tok = lambda parts: parts.reshape(4, 3, S).transpose(2, 0, 1)
    q_f = jnp.concatenate([h_att[:, COL_FQ:COL_FQ + 256].reshape(S, 4, HEAD_DIM), per_tok(cpos), one3, zpad],
                          axis=2).reshape(S, 4 * MLA_PAD)
    k_f = jnp.concatenate([h_att[:, COL_FK:COL_FK + 256].reshape(S, 4, HEAD_DIM), one3, per_tok(cneg), zpad],
                          axis=2).reshape(S, 4 * MLA_PAD)
    v_f = h_att[:, COL_FV:COL_FV + 256]
    oT_a, lse_a = _smax_fwd_t(q_f, k_f, v_f, dk=MLA_PAD, blk=blk, name=n("fox_fwd"))
    q_m, k_m, v_m = _mla_prep_fwd(lat, P["mla_g_q"], P["mla_g_kv"], W["wuq"], W["wuk"], W["wuv"], *tabs,
                                  bs=bs, name=n("mla_prep"))
    oT_b, lse_b = _smax_fwd_t(q_m, k_m, v_m, dk=MLA_PAD, blk=blk, name=n("mla_fwd"))
    bsb = min(BLK_STICK, S)
    oT_c, lt_c = _sb_fwd_t(h_att, blk=bsb, name=n("sb_fwd"))
    out_d, lse_d = _swa_fwd(h_att, P["swa_sinks"], name=n("swa_fwd"))
    outs = (oT_a, oT_b, oT_c, out_d)
    gn, gnT = _gnorm_fwd(outs, P["mix_g"], bs=bs, name=n("gnorm"))
    if late_weights is not None:
        W = dict(W, **late_weights(gn))
    u1 = _mm(gn, W["w_o"], name=n("out_proj"), bm=1024, bn=1024, bk=1024, resid=x, alpha=ALPHA)
    x1, x1b, x1bT, xh1, rs1 = _ln_fwd(u1, P["ln1_g"], P["ln1_b"], bs=bs, name=n("ln1"))
    gu = _mm(x1b, W["wgu"], name=n("gate_up"), bm=2048, bn=512, bk=1024)
    a, aT = _swiglu_fwd(gu, bs=min(256, S), name=n("swiglu"))
    u2 = _mm(a, W["w_down"], name=n("down"), bm=1024, bn=1024, bk=_blk(D_FF, 1408), resid=x1, alpha=ALPHA)
    x2, x2b, x2bT, xh2, rs2 = _ln_fwd(u2, P["ln2_g"], P["ln2_b"], bs=bs, name=n("ln2"))
    saved = dict(xbT=xbT, gnT=gnT, x1bT=x1bT, h_att=h_att, lat=lat, fg=fg, outs=outs, oT_a=oT_a, oT_b=oT_b, q_f=q_f, k_f=k_f, v_f=v_f,
                 lse_a=lse_a, lse_b=lse_b, lse_d=lse_d, lt_c=lt_c, q_m=q_m, k_m=k_m, v_m=v_m,
                 xh1=xh1, rs1=rs1, gu=gu, aT=aT, xh2=xh2, rs2=rs2)
    return x2, x2b, x2bT, saved, W


def _layer_bwd(l, dx2, sv, W, P, tabs, blk, send_early=None, send_late=None):
    S = dx2.shape[0]
    n = lambda s: f"l{l}_{s}"
    bs = min(512, S)
    h_att = sv["h_att"]
    du2, du2b, dg2, db2 = _ln_bwd(dx2, sv["xh2"], sv["rs2"], P["ln2_g"], bs=bs, name=n("ln2_bwd"))
    da = _mm(du2b, W["w_down"], name=n("down_dx"), tb=True, bm=1024, bn=_blk(D_FF, 1408), bk=1024)
    dwd = _mm(sv["aT"], du2b, name=n("down_dw"), bm=_blk(D_FF, 1408), bn=1024, bk=1024)
    dgu = _swiglu_bwd(da, sv["gu"], bs=min(256, S), name=n("swiglu_bwd"))
    dx1 = _mm(dgu, W["wgu"], name=n("gate_up_dx"), tb=True, bm=1024, bn=1024, bk=_blk(2 * D_FF, 1408), resid=du2,
              alpha=ALPHA)
    dwgu = _mm(sv["x1bT"], dgu, name=n("gate_up_dw"), bm=1024, bn=_blk(2 * D_FF, 1408), bk=1024)
    du1, du1b, dg1, db1 = _ln_bwd(dx1, sv["xh1"], sv["rs1"], P["ln1_g"], bs=bs, name=n("ln1_bwd"))
    dgn = _mm(du1b, W["w_o"], name=n("out_proj_dx"), tb=True, bm=1024, bn=1024, bk=1024)
    dwo = _mm(sv["gnT"], du1b, name=n("out_proj_dw"), bm=1024, bn=1024, bk=1024)
    mix_g = P["mix_g"]
    if send_early is not None:
        mix_g = mix_g + send_early(dict(w_o=dwo, w_gate=dwgu[:, :D_FF], w_up=dwgu[:, D_FF:], w_down=dwd))[0, 0]
    dmix, dmixT, dmixg = _gnorm_bwd(dgn, sv["outs"], mix_g, bs=bs, name=n("gnorm_bwd"))
    q_f, k_f = sv["q_f"], sv["k_f"]
    dqT_a, dk_a, dva = _smax_bwd_t(q_f, k_f, sv["v_f"], dmix, dmixT, sv["oT_a"], sv["lse_a"], dk=MLA_PAD, dcb=0,
                                   qscale=HEAD_DIM ** -0.5, blk=blk, name=n("fox_bwd"))
    dq_a, dk_a = dqT_a.T.reshape(S, 4, MLA_PAD), dk_a.reshape(S, 4, MLA_PAD)
    dqa, dka = dq_a[:, :, :HEAD_DIM].reshape(S, 256), dk_a[:, :, :HEAD_DIM].reshape(S, 256)
    dcq = dq_a[:, :, HEAD_DIM].T.reshape(4, S // 128, 128)
    dck = dk_a[:, :, HEAD_DIM + 3].T.reshape(4, S // 128, 128)
    q_m, k_m = sv["q_m"], sv["k_m"]
    dqT_b, dkb, dvb = _smax_bwd_t(q_m, k_m, sv["v_m"], dmix, dmixT, sv["oT_b"], sv["lse_b"], dk=MLA_PAD, dcb=2,
                                  qscale=MLA_QK ** -0.5, blk=blk, name=n("mla_bwd"))
    dqT_c, dkc, dvc = _sb_bwd_t(h_att, dmix, dmixT, sv["lt_c"], dcb=4, qscale=HEAD_DIM ** -0.5,
                                blk=min(BLK_STICK, S), name=n("sb_bwd"))
    dqc = dqT_c.T
    dqd, dkd, dvd, dsink = _swa_bwd(h_att, P["swa_sinks"], dmix, sv["outs"][3], sv["lse_d"], dcb=3, name=n("swa_bwd"))
    dlat, dwuq, dwuk, dwuv, dgq, dgkv = _mla_prep_bwd(
        sv["lat"], P["mla_g_q"], P["mla_g_kv"], W["wuq"], W["wuk"], W["wuv"], *tabs, dqT_b, dkb, dvb,
        bs=bs, name=n("mla_prep_bwd"))
    dfg, dbf = _fox_gate_bwd(sv["fg"], P["fox_b_f"], dck, dcq, q_unscale=HEAD_DIM ** 0.5, name=n("fox_gate_bwd"))
    dfg_blk = jnp.pad(dfg.reshape(4, S).T, ((0, 0), (0, 124)))
    dh = jnp.concatenate([t.astype(MXU_DTYPE) for t in (dqa, dka, dva, dqc, dkc, dvc, dqd, dkd, dvd, dlat, dfg_blk)], axis=1)
    dwin_p = _mm(sv["xbT"], dh, name=n("in_dw"), bm=1024, bn=_blk(PERM_W, 1024), bk=1024)
    grads = _unprep_grads(dwin_p, dwuq, dwuk, dwuv, dwo, dwgu, dwd)
    grads.update(fox_b_f=dbf[:, 0], mla_g_q=dgq[0], mla_g_kv=dgkv[0], swa_sinks=dsink[:, 0], mix_g=dmixg[0],
                 ln1_g=dg1[0], ln1_b=db1[0], ln2_g=dg2[0], ln2_b=db2[0])
    started = send_late(grads) if send_late is not None else None
    dx = _mm(dh, W["win_p"], name=n("in_dx"), tb=True, bm=1024, bn=1024, bk=_blk(PERM_W, 1024), resid=du1, alpha=ALPHA,
             after=started)
    return dx, grads


BIG = ("w_in", "mla_w_uq", "mla_w_ukv", "w_o", "w_gate", "w_up", "w_down")
SMALL = ("fox_b_f", "mla_g_q", "mla_g_kv", "swa_sinks", "mix_g", "ln1_g", "ln1_b", "ln2_g", "ln2_b")
SHARD_AXIS = dict(w_in=2, mla_w_uq=2, mla_w_ukv=2, w_o=1, w_gate=2, w_up=2, w_down=1)
N_CHIPS = 4
ANY = pl.BlockSpec(memory_space=pl.ANY)


HBM = pl.BlockSpec(memory_space=pltpu.HBM)
SEM = pl.BlockSpec(memory_space=pltpu.SEMAPHORE)
N_PEER_CHIPS = N_CHIPS - 1


def _peer_copies(src_ref, land_ref, sems, scatter):
    x, y, c = lax.axis_index("x"), lax.axis_index("y"), lax.axis_index("c")
    me = 2 * x + y
    out = []
    for r, (px, py) in enumerate([(1 - x, y), (x, 1 - y), (1 - x, 1 - y)]):
        theirs = 2 * px + py
        send = pltpu.make_async_remote_copy(
            src_ref=src_ref.at[theirs] if scatter else src_ref, dst_ref=land_ref.at[me],
            send_sem=sems[2 * r], recv_sem=sems[2 * r + 1], device_id=(px, py, c), device_id_type=MESH)
        arrive = pltpu.make_async_remote_copy(
            src_ref=src_ref.at[me] if scatter else src_ref, dst_ref=land_ref.at[theirs],
            send_sem=sems[2 * r], recv_sem=sems[2 * r + 1], device_id=(px, py, c), device_id_type=MESH)
        out.append((send, arrive))
    return out


def _exchange_start(srcs, *, scatter, name):
    nt = len(srcs)
    ns = 2 * N_PEER_CHIPS * nt
    land_shapes = [s.shape if scatter else (N_CHIPS,) + s.shape for s in srcs]

    def body(*refs):
        src_refs, land_refs, outs = refs[:nt], refs[nt:2 * nt], refs[2 * nt:]
        for t in range(nt):
            for send, _ in _peer_copies(src_refs[t], land_refs[t], outs[6 * t:6 * t + 6], scatter):
                send.start()
        outs[-1][...] = jnp.zeros_like(outs[-1])

    res = pl.pallas_call(
        body, name=name,
        out_shape=(*[pltpu.SemaphoreType.DMA(())] * ns, *[pltpu.HBM(s.shape, s.dtype) for s in srcs],
                   *[pltpu.HBM(ls, s.dtype) for ls, s in zip(land_shapes, srcs)], jax.ShapeDtypeStruct((8, 128), F32)),
        in_specs=(HBM,) * (2 * nt), out_specs=(*[SEM] * ns, *[HBM] * (2 * nt), pl.BlockSpec(memory_space=pltpu.VMEM)),
        input_output_aliases={i: ns + i for i in range(2 * nt)},
        compiler_params=pltpu.CompilerParams(has_side_effects=pltpu.SideEffectType.DATAFLOW_SIDE_EFFECTING),
    )(*[pltpu.with_memory_space_constraint(s, pltpu.HBM) for s in srcs],
      *[pltpu.with_memory_space_constraint(lax.empty(ls, s.dtype), pltpu.HBM) for ls, s in zip(land_shapes, srcs)])
    return dict(sems=res[:ns], srcs=res[ns:ns + nt], lands=res[ns + nt:ns + 2 * nt], token=res[-1])


def _exchange_wait(started, after, *, scatter, name):
    nt = len(started["srcs"])
    ns = 2 * N_PEER_CHIPS * nt

    def body(*refs):
        src_refs, land_refs, sems = refs[:nt], refs[nt:2 * nt], refs[2 * nt:2 * nt + ns]
        for t in range(nt):
            for send, arrive in _peer_copies(src_refs[t], land_refs[t], sems[6 * t:6 * t + 6], scatter):
                send.wait_send()
                arrive.wait_recv()

    both = list(started["srcs"]) + list(started["lands"])
    res = pl.pallas_call(
        body, name=name, out_shape=tuple(pltpu.HBM(a.shape, a.dtype) for a in both),
        in_specs=(*[HBM] * (2 * nt), *[SEM] * ns, ANY), out_specs=(HBM,) * (2 * nt),
        input_output_aliases={i: i for i in range(2 * nt)},
        compiler_params=pltpu.CompilerParams(has_side_effects=pltpu.SideEffectType.DATAFLOW_SIDE_EFFECTING),
    )(*both, *started["sems"], after)
    return res[:nt], res[nt:]


def _core_exchange(tensors, *, name):
    nt = len(tensors)

    def body(*refs):
        ins, outs = refs[:nt], refs[nt:2 * nt]
        send_sems, recv_sems = refs[2 * nt:]
        sibling = (lax.axis_index("x"), lax.axis_index("y"), 1 - lax.axis_index("c"))
        copies = [pltpu.make_async_remote_copy(src_ref=ins[t], dst_ref=outs[t], send_sem=send_sems.at[t],
                                               recv_sem=recv_sems.at[t], device_id=sibling, device_id_type=MESH)
                  for t in range(nt)]
        for cp in copies:
            cp.start()
        for cp in copies:
            cp.wait_recv()
        for cp in copies:
            cp.wait_send()

    return pl.pallas_call(
        body, name=name, in_specs=[ANY] * nt, out_specs=[ANY] * nt,
        out_shape=[jax.ShapeDtypeStruct(t.shape, t.dtype) for t in tensors],
        scratch_shapes=[pltpu.SemaphoreType.DMA((nt,)), pltpu.SemaphoreType.DMA((nt,))],
        compiler_params=pltpu.CompilerParams(has_side_effects=True),
    )(*tensors)


def _all_sum_small(block, *, name):
    R = block.shape[0]
    n_dev = 8

    def body(x_ref, o_ref, slots, send_sems, recv_sems):
        x, y, c = lax.axis_index("x"), lax.axis_index("y"), lax.axis_index("c")
        me = 4 * x + 2 * y + c
        slots[me] = x_ref[...]
        sends, recvs = [], []
        for d in range(1, n_dev):
            px, py, pc = x ^ (d >> 2), y ^ ((d >> 1) & 1), c ^ (d & 1)
            theirs = 4 * px + 2 * py + pc
            sends.append(pltpu.make_async_remote_copy(
                src_ref=x_ref, dst_ref=slots.at[me], send_sem=send_sems.at[d - 1], recv_sem=recv_sems.at[d - 1],
                device_id=(px, py, pc), device_id_type=MESH))
            recvs.append(pltpu.make_async_remote_copy(
                src_ref=x_ref, dst_ref=slots.at[theirs], send_sem=send_sems.at[d - 1], recv_sem=recv_sems.at[d - 1],
                device_id=(px, py, pc), device_id_type=MESH))
        for cp in sends:
            cp.start()
        for cp in recvs:
            cp.wait_recv()
        for cp in sends:
            cp.wait_send()
        total = slots[0]
        for k in range(1, n_dev):
            total = total + slots[k]
        o_ref[...] = total

    return pl.pallas_call(
        body, name=name, in_specs=[pl.BlockSpec(memory_space=pltpu.VMEM)],
        out_specs=pl.BlockSpec(memory_space=pltpu.VMEM), out_shape=jax.ShapeDtypeStruct((R, 128), F32),
        scratch_shapes=[pltpu.VMEM((n_dev, R, 128), F32), pltpu.SemaphoreType.DMA((n_dev - 1,)),
                        pltpu.SemaphoreType.DMA((n_dev - 1,))],
        compiler_params=pltpu.CompilerParams(has_side_effects=True),
    )(block)


def _sum_chips_into(acc, land, own, me, layer, *, br, name):
    _, R, C = land.shape

    def body(me_ref, land_ref, own_ref, acc_ref, o_ref):
        mine = me_ref[0]
        total = None
        for k in range(N_CHIPS):
            part = jnp.where(mine == k, own_ref[...], land_ref[k]).astype(F32)
            total = part if total is None else total + part
        o_ref[0] = total

    return pl.pallas_call(
        body, name=name, grid=(R // br,),
        in_specs=[pl.BlockSpec(memory_space=pltpu.SMEM), pl.BlockSpec((N_CHIPS, br, C), lambda i: (0, i, 0)),
                  pl.BlockSpec((br, C), lambda i: (i, 0)), ANY],
        out_specs=pl.BlockSpec((1, br, C), lambda i: (layer, i, 0)),
        out_shape=jax.ShapeDtypeStruct(acc.shape, F32), input_output_aliases={3: 0},
        compiler_params=_cparams("parallel"),
    )(me, land, own, acc)


def _adamw_math(w, g, m, v):
    m = ADAM_B1 * m + (1.0 - ADAM_B1) * g
    v = ADAM_B2 * v + (1.0 - ADAM_B2) * (g * g)
    m_hat = m / (1.0 - ADAM_B1 ** ADAM_STEP)
    v_hat = v / (1.0 - ADAM_B2 ** ADAM_STEP)
    return -ADAM_LR * (m_hat / (jnp.sqrt(v_hat) + ADAM_EPS) + ADAM_WD * w), m, v


def _adamw(w, m, v, g_a, g_b, *, br, name):
    R, C = w.shape
    two = g_b is not None

    def body(*refs):
        if two:
            w_ref, m_ref, v_ref, ga_ref, gb_ref, g_ref, d_ref, nm_ref, nv_ref = refs
            g = ga_ref[...] + gb_ref[...]
        else:
            w_ref, m_ref, v_ref, ga_ref, g_ref, d_ref, nm_ref, nv_ref = refs
            g = ga_ref[...]
        g_ref[...] = g
        d_ref[...], nm_ref[...], nv_ref[...] = _adamw_math(w_ref[...], g, m_ref[...], v_ref[...])

    spec = pl.BlockSpec((br, C), lambda i: (i, 0))
    args = [w, m, v, g_a] + ([g_b] if two else [])
    return pl.pallas_call(
        body, name=name, grid=(R // br,), in_specs=[spec] * len(args), out_specs=[spec] * 4,
        out_shape=[jax.ShapeDtypeStruct((R, C), F32)] * 4,
        compiler_params=_cparams("parallel"),
    )(*args)


SMALL_ROWS = dict(fox_b_f=1, mla_g_q=2, mla_g_kv=1, swa_sinks=1, mix_g=8, ln1_g=8, ln1_b=8, ln2_g=8, ln2_b=8)
SMALL_ROWS_PER_LAYER = sum(SMALL_ROWS.values())


def _pack_small(vals, extra_rows):
    L = vals[SMALL[0]].shape[0]
    per_layer = []
    for name in SMALL:
        a = vals[name].astype(F32)
        a = jnp.pad(a, ((0, 0), (0, SMALL_ROWS[name] * 128 - a.shape[1])))
        per_layer.append(a.reshape(L, SMALL_ROWS[name], 128))
    out = jnp.concatenate(per_layer, axis=1).reshape(L * SMALL_ROWS_PER_LAYER, 128)
    return jnp.pad(out, ((0, extra_rows), (0, 0)))


def _unpack_small(block, shapes):
    L = shapes[SMALL[0]][0]
    body = block[:L * SMALL_ROWS_PER_LAYER].reshape(L, SMALL_ROWS_PER_LAYER, 128)
    out, r = {}, 0
    for name in SMALL:
        n = shapes[name][1]
        out[name] = body[:, r:r + SMALL_ROWS[name]].reshape(L, SMALL_ROWS[name] * 128)[:, :n]
        r += SMALL_ROWS[name]
    return out


def _to_chips(g, axis):
    L, a, b = g.shape
    if axis == 2:
        return g.reshape(L, a, N_CHIPS, b // N_CHIPS).transpose(2, 0, 1, 3)
    return g.reshape(L, N_CHIPS, a // N_CHIPS, b).transpose(1, 0, 2, 3)


def kernel(x, w_in, fox_b_f, mla_g_q, mla_g_kv, mla_w_uq, mla_w_ukv, swa_sinks, mix_g, w_o, ln1_g, ln1_b, w_gate, w_up, w_down, ln2_g, ln2_b, loss_target, m_w_in, m_fox_b_f, m_mla_g_q, m_mla_g_kv, m_mla_w_uq, m_mla_w_ukv, m_swa_sinks, m_mix_g, m_w_o, m_ln1_g, m_ln1_b, m_w_gate, m_w_up, m_w_down, m_ln2_g, m_ln2_b, v_w_in, v_fox_b_f, v_mla_g_q, v_mla_g_kv, v_mla_w_uq, v_mla_w_ukv, v_swa_sinks, v_mix_g, v_w_o, v_ln1_g, v_ln1_b, v_w_gate, v_w_up, v_w_down, v_ln2_g, v_ln2_b):
    w = dict(w_in=w_in, fox_b_f=fox_b_f, mla_g_q=mla_g_q, mla_g_kv=mla_g_kv, mla_w_uq=mla_w_uq, mla_w_ukv=mla_w_ukv,
             swa_sinks=swa_sinks, mix_g=mix_g, w_o=w_o, ln1_g=ln1_g, ln1_b=ln1_b, w_gate=w_gate, w_up=w_up,
             w_down=w_down, ln2_g=ln2_g, ln2_b=ln2_b)
    m = dict(w_in=m_w_in, fox_b_f=m_fox_b_f, mla_g_q=m_mla_g_q, mla_g_kv=m_mla_g_kv, mla_w_uq=m_mla_w_uq,
             mla_w_ukv=m_mla_w_ukv, swa_sinks=m_swa_sinks, mix_g=m_mix_g, w_o=m_w_o, ln1_g=m_ln1_g, ln1_b=m_ln1_b,
             w_gate=m_w_gate, w_up=m_w_up, w_down=m_w_down, ln2_g=m_ln2_g, ln2_b=m_ln2_b)
    v = dict(w_in=v_w_in, fox_b_f=v_fox_b_f, mla_g_q=v_mla_g_q, mla_g_kv=v_mla_g_kv, mla_w_uq=v_mla_w_uq,
             mla_w_ukv=v_mla_w_ukv, swa_sinks=v_swa_sinks, mix_g=v_mix_g, w_o=v_w_o, ln1_g=v_ln1_g, ln1_b=v_ln1_b,
             w_gate=v_w_gate, w_up=v_w_up, w_down=v_w_down, ln2_g=v_ln2_g, ln2_b=v_ln2_b)
    names = tuple(w)
    L = w_in.shape[0]
    S = x.shape[1]
    blk = min(BLK_SOFTMAX, S)
    bs = min(512, S)

    me = 2 * lax.axis_index("x") + lax.axis_index("y")
    axis_of = {k: SHARD_AXIS[k] - 1 for k in BIG}
    groups = (("w_in", "mla_w_uq", "mla_w_ukv"), ("w_o", "w_gate", "w_up", "w_down"))

    started, last = [], None
    for l in range(L):
        per_group = []
        for g, group in enumerate(groups):
            srcs = [w[k][l].astype(MXU_DTYPE) for k in group]
            if last is not None:
                t = min(range(len(srcs)), key=lambda i: srcs[i].size)
                srcs[t] = srcs[t] + last["token"][0, 0].astype(MXU_DTYPE)
            last = _exchange_start(srcs, scatter=False, name=f"gather_start{l}_{g}")
            per_group.append(last)
        started.append(per_group)
    all_started = sum(st["token"] for per_group in started for st in per_group)

    def gathered(l, g, after):
        mine, lands = _exchange_wait(started[l][g], after, scatter=False, name=f"gather_wait{l}_{g}")
        shard = lambda t, k: jnp.where(me == k, mine[t], lands[t][k])
        whole = lambda t, axis: jnp.concatenate([shard(t, k) for k in range(N_CHIPS)], axis=axis)
        if g == 0:
            return _prep_weights_a(*[whole(t, axis_of[name]) for t, name in enumerate(groups[0])])
        gate_up = jnp.concatenate([shard(t, k) for t in (1, 2) for k in range(N_CHIPS)], axis=1)
        return dict(w_o=whole(0, 0), wgu=gate_up, w_down=whole(3, 0))

    def scatter(l, g, grads):
        to_owner = [_to_chips(grads[k].astype(MXU_DTYPE)[None], axis_of[k] + 1)[:, 0] for k in groups[g]]
        return _exchange_start(to_owner, scatter=True, name=f"scatter_start{l}_{g}")

    tabs = _rope_tables(S)
    Ps = []
    for l in range(L):
        P = dict(fox_b_f=fox_b_f[l], swa_sinks=swa_sinks[l])
        for k in ("mla_g_q", "mla_g_kv", "mix_g", "ln1_g", "ln1_b", "ln2_g", "ln2_b"):
            P[k] = w[k][l][None, :]
        Ps.append(P)

    xa = x[0]
    xb = xa.astype(MXU_DTYPE)
    xbT = xb.T
    saved, Ws = [], []
    for l in range(L):
        W = gathered(l, 0, all_started if l == 0 else xa)
        late = lambda after, l=l: gathered(l, 1, after)
        xa, xb, xbT, sv, W = _layer_fwd(l, xa, xb, xbT, W, Ps[l], tabs, blk, late_weights=late)
        saved.append(sv)
        Ws.append(W)
    dx, loss_part = _loss_head(xa, loss_target[0], bs=bs, name="loss_head")

    layer_grads = [None] * L
    sent = [[None, None] for _ in range(L)]
    pin = None
    for l in reversed(range(L)):
        P = Ps[l] if pin is None else dict(Ps[l], ln2_g=Ps[l]["ln2_g"] + pin[0, 0])

        def send_early(grads, l=l):
            sent[l][1] = scatter(l, 1, grads)
            return sent[l][1]["token"]

        def send_late(grads, l=l):
            sent[l][0] = scatter(l, 0, grads)
            return sent[l][0]["token"]

        dx, layer_grads[l] = _layer_bwd(l, dx, saved[l], Ws[l], P, tabs, blk, send_early=send_early, send_late=send_late)
        pin = sent[l][0]["token"]
    grad_x = dx[None]

    me_arr = me.astype(jnp.int32)[None]
    partial = {k: jnp.zeros(w[k].shape, F32) for k in BIG}
    out = {}

    def collect(l, g, after):
        mine, lands = _exchange_wait(sent[l][g], after, scatter=True, name=f"scatter_wait{l}_{g}")
        for t, k in enumerate(groups[g]):
            own = lax.dynamic_index_in_dim(mine[t], me, 0, keepdims=False)
            partial[k] = _sum_chips_into(partial[k], lands[t], own, me_arr, l, br=_rows(own.shape[0]),
                                         name=f"sum_{k}_l{l}")
        return partial[groups[g][-1]]

    def update(g):
        keys = groups[g]
        sibling = _core_exchange([partial[k] for k in keys], name=f"swap_partials_{g}")
        for k, theirs in zip(keys, sibling):
            shp = w[k].shape
            two_d = lambda a: a.reshape(shp[0] * shp[1], shp[2])
            res = _adamw(two_d(w[k]), two_d(m[k]), two_d(v[k]), two_d(partial[k]), two_d(theirs),
                         br=_rows(shp[0] * shp[1]), name=f"adamw_{k}")
            out[k] = [a.reshape(shp) for a in res]
        return out[keys[-1]][1]

    after = dx
    for l in reversed(range(L)):
        for g in (1, 0):
            if (l, g) != (0, 0):
                after = collect(l, g, after)
    after = update(1)
    collect(0, 0, after)
    update(0)
    local = {k: jnp.stack([layer_grads[l][k] for l in range(L)]) for k in SMALL}

    shapes = {k: w[k].shape for k in SMALL}
    extra = 8 + (-L * SMALL_ROWS_PER_LAYER) % 8
    block = _pack_small({k: local[k] for k in SMALL}, extra)
    block = block.at[L * SMALL_ROWS_PER_LAYER, 0].set(loss_part[0, 0])
    total = _all_sum_small(block, name="sum_small")
    loss = total[L * SMALL_ROWS_PER_LAYER, 0]
    res = _adamw(_pack_small({k: w[k] for k in SMALL}, extra), _pack_small({k: m[k] for k in SMALL}, extra),
                 _pack_small({k: v[k] for k in SMALL}, extra), total, None, br=total.shape[0], name="adamw_small")
    res = [_unpack_small(t, shapes) for t in res]
    for k in SMALL:
        out[k] = [r[k] for r in res]

    return (loss, grad_x, *[out[k][0] for k in names], *[out[k][1] for k in names],
            *[out[k][2] for k in names], *[out[k][3] for k in names])


def _rows(n):
    for b in (256, 128, 64, 32, 16, 8):
        if n % b == 0:
            return b
    return n
```
